```python
import math
import jax
import jax.numpy as jnp
from jax import lax
import numpy as np

D_MODEL = 1024
BATCH = 16
SEQ = 2048
DEPTH = 4

CHUNK = 64
EPS = 1e-6
N_BRANCH = 3
BRANCH_WIDTH = 512

DN_HEADS = 4
DN_DK = 128
DN_DV = 128
DN_CONV = 4
DN_QK = DN_HEADS * DN_DK
DN_VW = DN_HEADS * DN_DV
DN_QKV = 2 * DN_QK + DN_VW

RET_HEADS = 4
RET_DK = 64
RET_DV = 128
RET_QK = RET_HEADS * RET_DK
RET_VW = RET_HEADS * RET_DV
ROPE_BASE = 10000.0

LRU_WIDTH = 512
LRU_BLOCKS = 4
LRU_BLOCK = LRU_WIDTH // LRU_BLOCKS
LRU_CONV = 4
LRU_C = 8.0

D_FF = 2816
FFN_CONV = 3

IN_SIZES = (DN_QKV, DN_HEADS, DN_HEADS, DN_VW, RET_QK, RET_QK, RET_VW, RET_VW, LRU_WIDTH, LRU_WIDTH, N_BRANCH * D_MODEL)
N_IN = sum(IN_SIZES)

kernel_name = 'hybrid_deltanet_retention_rglru_convffn'


def _rms_norm(x, g):
    xf = x.astype(jnp.float32)
    y = xf * lax.rsqrt(jnp.mean(xf * xf, axis=-1, keepdims=True) + EPS)
    return (y * g.astype(jnp.float32)).astype(x.dtype)


def _l2norm(t):
    return t * lax.rsqrt(jnp.sum(t * t, axis=-1, keepdims=True) + EPS)


def _head_rms(t):
    return t * lax.rsqrt(jnp.mean(t * t, axis=-1, keepdims=True) + EPS)


def _head_groupnorm(t):
    mu = jnp.mean(t, axis=-1, keepdims=True)
    var = jnp.mean(jnp.square(t - mu), axis=-1, keepdims=True)
    return (t - mu) * lax.rsqrt(var + EPS)


def _causal_dwconv(x, w, b=None):
    width = w.shape[0]
    seq = x.shape[1]
    xp = jnp.pad(x, ((0, 0), (width - 1, 0), (0, 0)))
    y = xp[:, 0:seq, :] * w[0]
    for j in range(1, width):
        y = y + xp[:, j:j + seq, :] * w[j]
    return y if b is None else y + b


def _to_chunks(t, n):
    b, _, h, d = t.shape
    return t.reshape(b, n, CHUNK, h, d).transpose(0, 3, 1, 2, 4)


def _from_chunks(t):
    b, h, n, c, d = t.shape
    return t.transpose(0, 2, 3, 1, 4).reshape(b, n * c, h, d)


def _rotary(t, positions):
    half = t.shape[-1] // 2
    inv = ROPE_BASE ** (-jnp.arange(half, dtype=jnp.float32) / half)
    ang = positions.astype(jnp.float32)[:, :, None, None] * inv
    cos, sin = jnp.cos(ang), jnp.sin(ang)
    t1, t2 = t[..., :half], t[..., half:]
    return jnp.concatenate([t1 * cos - t2 * sin, t1 * sin + t2 * cos], axis=-1)


def _gated_delta_rule(q, k, v, g, beta):
    bsz, seq, nh, dk = q.shape
    dv = v.shape[-1]
    n = seq // CHUNK
    q = _l2norm(q) * (dk ** -0.5)
    k = _l2norm(k)
    q, k, v = _to_chunks(q, n), _to_chunks(k, n), _to_chunks(v, n)
    g = _to_chunks(g[..., None], n)[..., 0]
    beta = _to_chunks(beta[..., None], n)[..., 0]
    g = jnp.cumsum(g, axis=-1)
    causal = jnp.tril(jnp.ones((CHUNK, CHUNK), dtype=bool))
    strict = jnp.tril(jnp.ones((CHUNK, CHUNK), dtype=bool), -1)
    diff = g[..., :, None] - g[..., None, :]
    decay = jnp.where(causal, jnp.exp(jnp.where(causal, diff, 0.0)), 0.0)
    k_beta = k * beta[..., None]
    kk = jnp.einsum('bhnck,bhnsk->bhncs', k_beta, k)
    eye = jnp.eye(CHUNK, dtype=jnp.float32)
    lower = eye + jnp.where(strict, kk * decay, 0.0)
    t_inv = lax.linalg.triangular_solve(lower, jnp.broadcast_to(eye, lower.shape), left_side=True, lower=True)
    u = jnp.einsum('bhncs,bhnsv->bhncv', t_inv, v * beta[..., None])
    w = jnp.einsum('bhncs,bhnsk->bhnck', t_inv, k_beta * jnp.exp(g)[..., None])
    attn = jnp.einsum('bhnck,bhnsk->bhncs', q, k) * decay
    q_dec = q * jnp.exp(g)[..., None]
    k_dec = k * jnp.exp(g[..., -1:] - g)[..., None]
    g_end = jnp.exp(g[..., -1])

    def step(state, inp):
        qd, kd, uc, wc, ac, ge = inp
        v_new = uc - jnp.einsum('bhck,bhkv->bhcv', wc, state)
        out = jnp.einsum('bhck,bhkv->bhcv', qd, state) + jnp.einsum('bhcs,bhsv->bhcv', ac, v_new)
        state = state * ge[..., None, None] + jnp.einsum('bhck,bhcv->bhkv', kd, v_new)
        return state, out

    xs = (jnp.moveaxis(q_dec, 2, 0), jnp.moveaxis(k_dec, 2, 0), jnp.moveaxis(u, 2, 0),
          jnp.moveaxis(w, 2, 0), jnp.moveaxis(attn, 2, 0), jnp.moveaxis(g_end, 2, 0))
    s0 = jnp.zeros((bsz, nh, dk, dv), jnp.float32)
    _, out = lax.scan(step, s0, xs)
    return _from_chunks(jnp.moveaxis(out, 0, 2))


def _retention(q, k, v, log_gamma):
    bsz, seq, nh, dk = q.shape
    dv = v.shape[-1]
    n = seq // CHUNK
    k = k * (dk ** -0.5)
    q, k, v = _to_chunks(q, n), _to_chunks(k, n), _to_chunks(v, n)
    idx = jnp.arange(CHUNK, dtype=jnp.float32)
    dist = idx[:, None] - idx[None, :]
    causal = dist >= 0
    dmask = jnp.where(causal, jnp.exp(jnp.where(causal, dist, 0.0) * log_gamma[:, None, None]), 0.0)
    scores = jnp.einsum('bhnck,bhnsk->bhncs', q, k) * dmask[:, None]
    inner = jnp.einsum('bhncs,bhnsv->bhncv', scores, v)
    xi = jnp.exp((idx + 1.0) * log_gamma[:, None])
    zeta = jnp.exp((CHUNK - 1.0 - idx) * log_gamma[:, None])
    g_chunk = jnp.exp(CHUNK * log_gamma)
    kz = k * zeta[:, None, :, None]

    def step(state, inp):
        qc, kc, vc = inp
        out = jnp.einsum('bhck,bhkv->bhcv', qc, state) * xi[:, :, None]
        state = state * g_chunk[:, None, None] + jnp.einsum('bhck,bhcv->bhkv', kc, vc)
        return state, out

    s0 = jnp.zeros((bsz, nh, dk, dv), jnp.float32)
    _, cross = lax.scan(step, s0, (jnp.moveaxis(q, 2, 0), jnp.moveaxis(kz, 2, 0), jnp.moveaxis(v, 2, 0)))
    return _from_chunks(jnp.moveaxis(cross, 0, 2) + inner)


def _rg_lru(x, w_a, b_a, w_x, b_x, lam):
    bsz, seq, width = x.shape
    xb = x.reshape(bsz, seq, LRU_BLOCKS, LRU_BLOCK)
    r = jax.nn.sigmoid(jnp.einsum('btnc,ncd->btnd', xb, w_a) + b_a).reshape(bsz, seq, width)
    i = jax.nn.sigmoid(jnp.einsum('btnc,ncd->btnd', xb, w_x) + b_x).reshape(bsz, seq, width)
    log_a = -LRU_C * r * jax.nn.softplus(-lam)
    a = jnp.exp(log_a)
    b = jnp.sqrt(-jnp.expm1(2.0 * log_a)) * (i * x)

    def combine(left, right):
        a1, b1 = left
        a2, b2 = right
        return a1 * a2, a2 * b1 + b2

    _, h = lax.associative_scan(combine, (a, b), axis=1)
    return h


def _hybrid_mixer(h, positions, log_gamma, w_in, dn_conv_w, dn_a_log, dn_dt_bias, dn_norm_w,
                  lru_conv_w, lru_conv_b, lru_wa, lru_ba, lru_wx, lru_bx, lru_lambda, w_branch, w_out):
    bsz, seq, _ = h.shape
    f32 = jnp.float32
    u = h @ w_in
    split_points = [int(p) for p in np.cumsum(IN_SIZES)[:-1]]
    (a_qkv, a_alpha, a_beta, a_z, b_q, b_k, b_v, b_g, c_x, c_g, gate_logits) = jnp.split(u, split_points, axis=-1)

    a_qkv = jax.nn.silu(_causal_dwconv(a_qkv, dn_conv_w)).astype(f32)
    qa, ka, va = jnp.split(a_qkv, [DN_QK, 2 * DN_QK], axis=-1)
    qa = qa.reshape(bsz, seq, DN_HEADS, DN_DK)
    ka = ka.reshape(bsz, seq, DN_HEADS, DN_DK)
    va = va.reshape(bsz, seq, DN_HEADS, DN_DV)
    g_a = -jnp.exp(dn_a_log.astype(f32)) * jax.nn.softplus(a_alpha.astype(f32) + dn_dt_bias.astype(f32))
    beta_a = jax.nn.sigmoid(a_beta.astype(f32))
    oa = _gated_delta_rule(qa, ka, va, g_a, beta_a)
    z_a = a_z.astype(f32).reshape(bsz, seq, DN_HEADS, DN_DV)
    oa = _head_rms(oa) * dn_norm_w.astype(f32) * jax.nn.silu(z_a)
    y_a = oa.reshape(bsz, seq, DN_VW).astype(h.dtype)

    qb = _rotary(b_q.astype(f32).reshape(bsz, seq, RET_HEADS, RET_DK), positions)
    kb = _rotary(b_k.astype(f32).reshape(bsz, seq, RET_HEADS, RET_DK), positions)
    vb = b_v.astype(f32).reshape(bsz, seq, RET_HEADS, RET_DV)
    ob = _retention(qb, kb, vb, log_gamma)
    ob = _head_groupnorm(ob) * jax.nn.silu(b_g.astype(f32).reshape(bsz, seq, RET_HEADS, RET_DV))
    y_b = ob.reshape(bsz, seq, RET_VW).astype(h.dtype)

    xc = _causal_dwconv(c_x, lru_conv_w, lru_conv_b).astype(f32)
    hc = _rg_lru(xc, lru_wa.astype(f32), lru_ba.astype(f32), lru_wx.astype(f32), lru_bx.astype(f32), lru_lambda.astype(f32))
    y_c = (hc * jax.nn.gelu(c_g.astype(f32))).astype(h.dtype)

    ys = jnp.stack([y_a, y_b, y_c], axis=2)
    branch = jnp.einsum('btnw,nwd->btnd', ys, w_branch)
    gates = jax.nn.sigmoid(gate_logits.reshape(bsz, seq, N_BRANCH, D_MODEL))
    merged = jnp.sum(gates * branch, axis=2)
    return merged @ w_out


def _conv_ffn(h, w_up, conv_w, conv_b, w_down):
    up = _causal_dwconv(h @ w_up, conv_w, conv_b)
    gate, val = jnp.split(up, 2, axis=-1)
    return (jax.nn.silu(gate) * val) @ w_down


def _fwd_setup_inputs(seed: int = 0) -> dict:
    key = jax.random.key(seed)
    ks = jax.random.split(key, 24)
    f32 = jnp.float32

    def nrm(k, shape, scale):
        return jax.random.normal(k, shape, f32) * scale

    x = nrm(ks[0], (BATCH, SEQ, D_MODEL), 1.0)
    offsets = jax.random.randint(ks[1], (BATCH, 1), 0, 4096, dtype=jnp.int32)
    positions = offsets + jnp.arange(SEQ, dtype=jnp.int32)[None, :]
    attn_norm = 1.0 + nrm(ks[2], (DEPTH, D_MODEL), 0.02)
    w_in = nrm(ks[3], (DEPTH, D_MODEL, N_IN), D_MODEL ** -0.5)
    dn_conv_w = nrm(ks[4], (DEPTH, DN_CONV, DN_QKV), DN_CONV ** -0.5)
    dn_a_log = jnp.log(jax.random.uniform(ks[5], (DEPTH, DN_HEADS), f32, 1.0, 16.0))
    dt = jnp.exp(jax.random.uniform(ks[6], (DEPTH, DN_HEADS), f32, math.log(0.001), math.log(0.1)))
    dn_dt_bias = dt + jnp.log(-jnp.expm1(-dt))
    dn_norm_w = 1.0 + nrm(ks[7], (DEPTH, DN_DV), 0.02)
    lru_conv_w = nrm(ks[8], (DEPTH, LRU_CONV, LRU_WIDTH), LRU_CONV ** -0.5)
    lru_conv_b = nrm(ks[9], (DEPTH, LRU_WIDTH), 0.01)
    lru_wa = nrm(ks[10], (DEPTH, LRU_BLOCKS, LRU_BLOCK, LRU_BLOCK), LRU_BLOCK ** -0.5)
    lru_ba = nrm(ks[11], (DEPTH, LRU_BLOCKS, LRU_BLOCK), 0.01)
    lru_wx = nrm(ks[12], (DEPTH, LRU_BLOCKS, LRU_BLOCK, LRU_BLOCK), LRU_BLOCK ** -0.5)
    lru_bx = nrm(ks[13], (DEPTH, LRU_BLOCKS, LRU_BLOCK), 0.01)
    a0 = jax.random.uniform(ks[14], (DEPTH, LRU_WIDTH), f32, 0.9, 0.999) ** (1.0 / LRU_C)
    lru_lambda = jnp.log(a0) - jnp.log1p(-a0)
    w_branch = nrm(ks[15], (DEPTH, N_BRANCH, BRANCH_WIDTH, D_MODEL), BRANCH_WIDTH ** -0.5)
    w_out = nrm(ks[16], (DEPTH, D_MODEL, D_MODEL), D_MODEL ** -0.5)
    ffn_norm = 1.0 + nrm(ks[17], (DEPTH, D_MODEL), 0.02)
    w_up = nrm(ks[18], (DEPTH, D_MODEL, 2 * D_FF), D_MODEL ** -0.5)
    ffn_conv_w = nrm(ks[19], (DEPTH, FFN_CONV, 2 * D_FF), FFN_CONV ** -0.5)
    ffn_conv_b = nrm(ks[20], (DEPTH, 2 * D_FF), 0.01)
    w_down = nrm(ks[21], (DEPTH, D_FF, D_MODEL), D_FF ** -0.5)
    final_norm = 1.0 + nrm(ks[22], (D_MODEL,), 0.02)
    return {'x': x, 'positions': positions, 'attn_norm': attn_norm, 'w_in': w_in, 'dn_conv_w': dn_conv_w,
            'dn_a_log': dn_a_log, 'dn_dt_bias': dn_dt_bias, 'dn_norm_w': dn_norm_w, 'lru_conv_w': lru_conv_w,
            'lru_conv_b': lru_conv_b, 'lru_wa': lru_wa, 'lru_ba': lru_ba, 'lru_wx': lru_wx, 'lru_bx': lru_bx,
            'lru_lambda': lru_lambda, 'w_branch': w_branch, 'w_out': w_out, 'ffn_norm': ffn_norm, 'w_up': w_up,
            'ffn_conv_w': ffn_conv_w, 'ffn_conv_b': ffn_conv_b, 'w_down': w_down, 'final_norm': final_norm}


def _fwd_reference(x, positions, attn_norm, w_in, dn_conv_w, dn_a_log, dn_dt_bias, dn_norm_w, lru_conv_w, lru_conv_b,
              lru_wa, lru_ba, lru_wx, lru_bx, lru_lambda, w_branch, w_out, ffn_norm, w_up, ffn_conv_w, ffn_conv_b,
              w_down, final_norm):
    log_gamma = jnp.log(1.0 - 2.0 ** (-5.0 - jnp.arange(RET_HEADS, dtype=jnp.float32)))
    for layer in range(DEPTH):
        h = _rms_norm(x, attn_norm[layer])
        x = x + _hybrid_mixer(h, positions, log_gamma, w_in[layer], dn_conv_w[layer], dn_a_log[layer],
                              dn_dt_bias[layer], dn_norm_w[layer], lru_conv_w[layer], lru_conv_b[layer],
                              lru_wa[layer], lru_ba[layer], lru_wx[layer], lru_bx[layer], lru_lambda[layer],
                              w_branch[layer], w_out[layer])
        h = _rms_norm(x, ffn_norm[layer])
        x = x + _conv_ffn(h, w_up[layer], ffn_conv_w[layer], ffn_conv_b[layer], w_down[layer])
    return _rms_norm(x, final_norm)


import jax as _jax
import jax.numpy as _jnp

TWIN_FORMAT = 'train_step'
FWD_PARAMS = ['x', 'positions', 'attn_norm', 'w_in', 'dn_conv_w', 'dn_a_log', 'dn_dt_bias', 'dn_norm_w', 'lru_conv_w', 'lru_conv_b', 'lru_wa', 'lru_ba', 'lru_wx', 'lru_bx', 'lru_lambda', 'w_branch', 'w_out', 'ffn_norm', 'w_up', 'ffn_conv_w', 'ffn_conv_b', 'w_down', 'final_norm']
TWIN_WEIGHTS = ['attn_norm', 'w_in', 'dn_conv_w', 'dn_a_log', 'dn_dt_bias', 'dn_norm_w', 'lru_conv_w', 'lru_conv_b', 'lru_wa', 'lru_ba', 'lru_wx', 'lru_bx', 'lru_lambda', 'w_branch', 'w_out', 'ffn_norm', 'w_up', 'ffn_conv_w', 'ffn_conv_b', 'w_down', 'final_norm']
TWIN_DIFF_INPUT = 'x'
TWIN_INPUTS = ['x', 'positions', 'attn_norm', 'w_in', 'dn_conv_w', 'dn_a_log', 'dn_dt_bias', 'dn_norm_w', 'lru_conv_w', 'lru_conv_b', 'lru_wa', 'lru_ba', 'lru_wx', 'lru_bx', 'lru_lambda', 'w_branch', 'w_out', 'ffn_norm', 'w_up', 'ffn_conv_w', 'ffn_conv_b', 'w_down', 'final_norm', 'loss_target', 'm_attn_norm', 'm_w_in', 'm_dn_conv_w', 'm_dn_a_log', 'm_dn_dt_bias', 'm_dn_norm_w', 'm_lru_conv_w', 'm_lru_conv_b', 'm_lru_wa', 'm_lru_ba', 'm_lru_wx', 'm_lru_bx', 'm_lru_lambda', 'm_w_branch', 'm_w_out', 'm_ffn_norm', 'm_w_up', 'm_ffn_conv_w', 'm_ffn_conv_b', 'm_w_down', 'm_final_norm', 'v_attn_norm', 'v_w_in', 'v_dn_conv_w', 'v_dn_a_log', 'v_dn_dt_bias', 'v_dn_norm_w', 'v_lru_conv_w', 'v_lru_conv_b', 'v_lru_wa', 'v_lru_ba', 'v_lru_wx', 'v_lru_bx', 'v_lru_lambda', 'v_w_branch', 'v_w_out', 'v_ffn_norm', 'v_w_up', 'v_ffn_conv_w', 'v_ffn_conv_b', 'v_w_down', 'v_final_norm']
TWIN_OUTPUTS = ['loss', 'grad_x', 'grad_attn_norm', 'grad_w_in', 'grad_dn_conv_w', 'grad_dn_a_log', 'grad_dn_dt_bias', 'grad_dn_norm_w', 'grad_lru_conv_w', 'grad_lru_conv_b', 'grad_lru_wa', 'grad_lru_ba', 'grad_lru_wx', 'grad_lru_bx', 'grad_lru_lambda', 'grad_w_branch', 'grad_w_out', 'grad_ffn_norm', 'grad_w_up', 'grad_ffn_conv_w', 'grad_ffn_conv_b', 'grad_w_down', 'grad_final_norm', 'delta_attn_norm', 'delta_w_in', 'delta_dn_conv_w', 'delta_dn_a_log', 'delta_dn_dt_bias', 'delta_dn_norm_w', 'delta_lru_conv_w', 'delta_lru_conv_b', 'delta_lru_wa', 'delta_lru_ba', 'delta_lru_wx', 'delta_lru_bx', 'delta_lru_lambda', 'delta_w_branch', 'delta_w_out', 'delta_ffn_norm', 'delta_w_up', 'delta_ffn_conv_w', 'delta_ffn_conv_b', 'delta_w_down', 'delta_final_norm', 'new_m_attn_norm', 'new_m_w_in', 'new_m_dn_conv_w', 'new_m_dn_a_log', 'new_m_dn_dt_bias', 'new_m_dn_norm_w', 'new_m_lru_conv_w', 'new_m_lru_conv_b', 'new_m_lru_wa', 'new_m_lru_ba', 'new_m_lru_wx', 'new_m_lru_bx', 'new_m_lru_lambda', 'new_m_w_branch', 'new_m_w_out', 'new_m_ffn_norm', 'new_m_w_up', 'new_m_ffn_conv_w', 'new_m_ffn_conv_b', 'new_m_w_down', 'new_m_final_norm', 'new_v_attn_norm', 'new_v_w_in', 'new_v_dn_conv_w', 'new_v_dn_a_log', 'new_v_dn_dt_bias', 'new_v_dn_norm_w', 'new_v_lru_conv_w', 'new_v_lru_conv_b', 'new_v_lru_wa', 'new_v_lru_ba', 'new_v_lru_wx', 'new_v_lru_bx', 'new_v_lru_lambda', 'new_v_w_branch', 'new_v_w_out', 'new_v_ffn_norm', 'new_v_w_up', 'new_v_ffn_conv_w', 'new_v_ffn_conv_b', 'new_v_w_down', 'new_v_final_norm']
TWIN_LEAF_KINDS = {'loss': 'loss', 'grad_x': 'grad_x', 'grad_attn_norm': 'grad_w', 'grad_w_in': 'grad_w', 'grad_dn_conv_w': 'grad_w', 'grad_dn_a_log': 'grad_w', 'grad_dn_dt_bias': 'grad_w', 'grad_dn_norm_w': 'grad_w', 'grad_lru_conv_w': 'grad_w', 'grad_lru_conv_b': 'grad_w', 'grad_lru_wa': 'grad_w', 'grad_lru_ba': 'grad_w', 'grad_lru_wx': 'grad_w', 'grad_lru_bx': 'grad_w', 'grad_lru_lambda': 'grad_w', 'grad_w_branch': 'grad_w', 'grad_w_out': 'grad_w', 'grad_ffn_norm': 'grad_w', 'grad_w_up': 'grad_w', 'grad_ffn_conv_w': 'grad_w', 'grad_ffn_conv_b': 'grad_w', 'grad_w_down': 'grad_w', 'grad_final_norm': 'grad_w', 'delta_attn_norm': 'delta_w', 'delta_w_in': 'delta_w', 'delta_dn_conv_w': 'delta_w', 'delta_dn_a_log': 'delta_w', 'delta_dn_dt_bias': 'delta_w', 'delta_dn_norm_w': 'delta_w', 'delta_lru_conv_w': 'delta_w', 'delta_lru_conv_b': 'delta_w', 'delta_lru_wa': 'delta_w', 'delta_lru_ba': 'delta_w', 'delta_lru_wx': 'delta_w', 'delta_lru_bx': 'delta_w', 'delta_lru_lambda': 'delta_w', 'delta_w_branch': 'delta_w', 'delta_w_out': 'delta_w', 'delta_ffn_norm': 'delta_w', 'delta_w_up': 'delta_w', 'delta_ffn_conv_w': 'delta_w', 'delta_ffn_conv_b': 'delta_w', 'delta_w_down': 'delta_w', 'delta_final_norm': 'delta_w', 'new_m_attn_norm': 'new_m', 'new_m_w_in': 'new_m', 'new_m_dn_conv_w': 'new_m', 'new_m_dn_a_log': 'new_m', 'new_m_dn_dt_bias': 'new_m', 'new_m_dn_norm_w': 'new_m', 'new_m_lru_conv_w': 'new_m', 'new_m_lru_conv_b': 'new_m', 'new_m_lru_wa': 'new_m', 'new_m_lru_ba': 'new_m', 'new_m_lru_wx': 'new_m', 'new_m_lru_bx': 'new_m', 'new_m_lru_lambda': 'new_m', 'new_m_w_branch': 'new_m', 'new_m_w_out': 'new_m', 'new_m_ffn_norm': 'new_m', 'new_m_w_up': 'new_m', 'new_m_ffn_conv_w': 'new_m', 'new_m_ffn_conv_b': 'new_m', 'new_m_w_down': 'new_m', 'new_m_final_norm': 'new_m', 'new_v_attn_norm': 'new_v', 'new_v_w_in': 'new_v', 'new_v_dn_conv_w': 'new_v', 'new_v_dn_a_log': 'new_v', 'new_v_dn_dt_bias': 'new_v', 'new_v_dn_norm_w': 'new_v', 'new_v_lru_conv_w': 'new_v', 'new_v_lru_conv_b': 'new_v', 'new_v_lru_wa': 'new_v', 'new_v_lru_ba': 'new_v', 'new_v_lru_wx': 'new_v', 'new_v_lru_bx': 'new_v', 'new_v_lru_lambda': 'new_v', 'new_v_w_branch': 'new_v', 'new_v_w_out': 'new_v', 'new_v_ffn_norm': 'new_v', 'new_v_w_up': 'new_v', 'new_v_ffn_conv_w': 'new_v', 'new_v_ffn_conv_b': 'new_v', 'new_v_w_down': 'new_v', 'new_v_final_norm': 'new_v'}


def _forward(args):
    return _fwd_reference(*[args[k] for k in FWD_PARAMS])


def _output_shape():
    out = _jax.eval_shape(lambda: _forward(_fwd_setup_inputs(0)))
    return out.shape, out.dtype

N_MICROBATCH = 1
ADAM_LR = 0.001
ADAM_B1 = 0.9
ADAM_B2 = 0.999
ADAM_EPS = 1e-08
ADAM_WD = 0.01
ADAM_STEP = 10
PER_EXAMPLE_BATCH_AXIS = {'x': 0, 'positions': 0, 'loss_target': 0}
SHARED_INPUTS = []
_WEIGHT_DTYPES = {'attn_norm': _jnp.float32, 'w_in': _jnp.float32, 'dn_conv_w': _jnp.float32, 'dn_a_log': _jnp.float32, 'dn_dt_bias': _jnp.float32, 'dn_norm_w': _jnp.float32, 'lru_conv_w': _jnp.float32, 'lru_conv_b': _jnp.float32, 'lru_wa': _jnp.float32, 'lru_ba': _jnp.float32, 'lru_wx': _jnp.float32, 'lru_bx': _jnp.float32, 'lru_lambda': _jnp.float32, 'w_branch': _jnp.float32, 'w_out': _jnp.float32, 'ffn_norm': _jnp.float32, 'w_up': _jnp.float32, 'ffn_conv_w': _jnp.float32, 'ffn_conv_b': _jnp.float32, 'w_down': _jnp.float32, 'final_norm': _jnp.float32}
MOMENT_SCALE = {'attn_norm': 1.879143e-01, 'w_in': 6.750438e-02, 'dn_conv_w': 7.032164e-02, 'dn_a_log': 5.034264e-01, 'dn_dt_bias': 4.908573e-01, 'dn_norm_w': 2.094318e-01, 'lru_conv_w': 8.140477e-02, 'lru_conv_b': 9.290485e-01, 'lru_wa': 2.379411e-02, 'lru_ba': 2.097184e-02, 'lru_wx': 4.358538e-02, 'lru_bx': 2.553572e-02, 'lru_lambda': 4.651779e-02, 'w_branch': 6.049939e-02, 'w_out': 1.049238e-01, 'ffn_norm': 1.437152e-01, 'w_up': 6.091966e-02, 'ffn_conv_w': 6.094769e-02, 'ffn_conv_b': 6.568242e-02, 'w_down': 9.943791e-02, 'final_norm': 3.191274e+01}


def _to_microbatches(a, axis):
    t = _jnp.moveaxis(a, axis, 0)
    t = t.reshape((N_MICROBATCH, t.shape[0] // N_MICROBATCH) + t.shape[1:])
    return _jnp.moveaxis(t, 1, axis + 1)


def setup_inputs(seed: int = 0) -> dict:
    inp = _fwd_setup_inputs(seed)
    key = _jax.random.fold_in(_jax.random.key(seed), 7919)
    shape, _ = _output_shape()
    out = dict(inp)
    out["loss_target"] = _jax.random.normal(_jax.random.fold_in(key, 0), shape, _jnp.float32)
    for i, name in enumerate(TWIN_WEIGHTS):
        w = inp[name].astype(_jnp.float32)
        if MOMENT_SCALE is None:
            s = _jnp.sqrt(_jnp.mean(_jnp.square(w)) + 1e-30)
        else:
            s = MOMENT_SCALE[name]
        km, kv = _jax.random.split(_jax.random.fold_in(key, i + 1))
        out[name] = w
        out["m_" + name] = s * _jax.random.normal(km, w.shape, _jnp.float32)
        out["v_" + name] = (s * s) * _jax.random.uniform(kv, w.shape, _jnp.float32, 0.5, 1.5)
    if N_MICROBATCH > 1:
        for name, axis in PER_EXAMPLE_BATCH_AXIS.items():
            out[name] = _to_microbatches(out[name], axis)
    return {'x': out['x'], 'positions': out['positions'], 'attn_norm': out['attn_norm'], 'w_in': out['w_in'], 'dn_conv_w': out['dn_conv_w'], 'dn_a_log': out['dn_a_log'], 'dn_dt_bias': out['dn_dt_bias'], 'dn_norm_w': out['dn_norm_w'], 'lru_conv_w': out['lru_conv_w'], 'lru_conv_b': out['lru_conv_b'], 'lru_wa': out['lru_wa'], 'lru_ba': out['lru_ba'], 'lru_wx': out['lru_wx'], 'lru_bx': out['lru_bx'], 'lru_lambda': out['lru_lambda'], 'w_branch': out['w_branch'], 'w_out': out['w_out'], 'ffn_norm': out['ffn_norm'], 'w_up': out['w_up'], 'ffn_conv_w': out['ffn_conv_w'], 'ffn_conv_b': out['ffn_conv_b'], 'w_down': out['w_down'], 'final_norm': out['final_norm'], 'loss_target': out['loss_target'], 'm_attn_norm': out['m_attn_norm'], 'm_w_in': out['m_w_in'], 'm_dn_conv_w': out['m_dn_conv_w'], 'm_dn_a_log': out['m_dn_a_log'], 'm_dn_dt_bias': out['m_dn_dt_bias'], 'm_dn_norm_w': out['m_dn_norm_w'], 'm_lru_conv_w': out['m_lru_conv_w'], 'm_lru_conv_b': out['m_lru_conv_b'], 'm_lru_wa': out['m_lru_wa'], 'm_lru_ba': out['m_lru_ba'], 'm_lru_wx': out['m_lru_wx'], 'm_lru_bx': out['m_lru_bx'], 'm_lru_lambda': out['m_lru_lambda'], 'm_w_branch': out['m_w_branch'], 'm_w_out': out['m_w_out'], 'm_ffn_norm': out['m_ffn_norm'], 'm_w_up': out['m_w_up'], 'm_ffn_conv_w': out['m_ffn_conv_w'], 'm_ffn_conv_b': out['m_ffn_conv_b'], 'm_w_down': out['m_w_down'], 'm_final_norm': out['m_final_norm'], 'v_attn_norm': out['v_attn_norm'], 'v_w_in': out['v_w_in'], 'v_dn_conv_w': out['v_dn_conv_w'], 'v_dn_a_log': out['v_dn_a_log'], 'v_dn_dt_bias': out['v_dn_dt_bias'], 'v_dn_norm_w': out['v_dn_norm_w'], 'v_lru_conv_w': out['v_lru_conv_w'], 'v_lru_conv_b': out['v_lru_conv_b'], 'v_lru_wa': out['v_lru_wa'], 'v_lru_ba': out['v_lru_ba'], 'v_lru_wx': out['v_lru_wx'], 'v_lru_bx': out['v_lru_bx'], 'v_lru_lambda': out['v_lru_lambda'], 'v_w_branch': out['v_w_branch'], 'v_w_out': out['v_w_out'], 'v_ffn_norm': out['v_ffn_norm'], 'v_w_up': out['v_w_up'], 'v_ffn_conv_w': out['v_ffn_conv_w'], 'v_ffn_conv_b': out['v_ffn_conv_b'], 'v_w_down': out['v_w_down'], 'v_final_norm': out['v_final_norm']}


def _loss(weights, diff, rest, loss_target):
    with _jax.named_scope("forward"):
        args = {**rest, TWIN_DIFF_INPUT: diff, **{k: w.astype(_WEIGHT_DTYPES[k]) for k, w in weights.items()}}
        y = _forward(args)
    with _jax.named_scope("loss_head"):
        err = _jnp.square(y.astype(_jnp.float32) - loss_target)
        return 0.5 * _jnp.sum(_jnp.mean(err, axis=-1)) if err.ndim else 0.5 * err


def _adamw(w, g, m, v):
    m = ADAM_B1 * m + (1.0 - ADAM_B1) * g
    v = ADAM_B2 * v + (1.0 - ADAM_B2) * _jnp.square(g)
    m_hat = m / (1.0 - ADAM_B1 ** ADAM_STEP)
    v_hat = v / (1.0 - ADAM_B2 ** ADAM_STEP)
    delta = -ADAM_LR * (m_hat / (_jnp.sqrt(v_hat) + ADAM_EPS) + ADAM_WD * w)
    return delta, m, v


def reference(x, positions, attn_norm, w_in, dn_conv_w, dn_a_log, dn_dt_bias, dn_norm_w, lru_conv_w, lru_conv_b, lru_wa, lru_ba, lru_wx, lru_bx, lru_lambda, w_branch, w_out, ffn_norm, w_up, ffn_conv_w, ffn_conv_b, w_down, final_norm, loss_target, m_attn_norm, m_w_in, m_dn_conv_w, m_dn_a_log, m_dn_dt_bias, m_dn_norm_w, m_lru_conv_w, m_lru_conv_b, m_lru_wa, m_lru_ba, m_lru_wx, m_lru_bx, m_lru_lambda, m_w_branch, m_w_out, m_ffn_norm, m_w_up, m_ffn_conv_w, m_ffn_conv_b, m_w_down, m_final_norm, v_attn_norm, v_w_in, v_dn_conv_w, v_dn_a_log, v_dn_dt_bias, v_dn_norm_w, v_lru_conv_w, v_lru_conv_b, v_lru_wa, v_lru_ba, v_lru_wx, v_lru_bx, v_lru_lambda, v_w_branch, v_w_out, v_ffn_norm, v_w_up, v_ffn_conv_w, v_ffn_conv_b, v_w_down, v_final_norm):
    given = dict(x=x, positions=positions, attn_norm=attn_norm, w_in=w_in, dn_conv_w=dn_conv_w, dn_a_log=dn_a_log, dn_dt_bias=dn_dt_bias, dn_norm_w=dn_norm_w, lru_conv_w=lru_conv_w, lru_conv_b=lru_conv_b, lru_wa=lru_wa, lru_ba=lru_ba, lru_wx=lru_wx, lru_bx=lru_bx, lru_lambda=lru_lambda, w_branch=w_branch, w_out=w_out, ffn_norm=ffn_norm, w_up=w_up, ffn_conv_w=ffn_conv_w, ffn_conv_b=ffn_conv_b, w_down=w_down, final_norm=final_norm, loss_target=loss_target, m_attn_norm=m_attn_norm, m_w_in=m_w_in, m_dn_conv_w=m_dn_conv_w, m_dn_a_log=m_dn_a_log, m_dn_dt_bias=m_dn_dt_bias, m_dn_norm_w=m_dn_norm_w, m_lru_conv_w=m_lru_conv_w, m_lru_conv_b=m_lru_conv_b, m_lru_wa=m_lru_wa, m_lru_ba=m_lru_ba, m_lru_wx=m_lru_wx, m_lru_bx=m_lru_bx, m_lru_lambda=m_lru_lambda, m_w_branch=m_w_branch, m_w_out=m_w_out, m_ffn_norm=m_ffn_norm, m_w_up=m_w_up, m_ffn_conv_w=m_ffn_conv_w, m_ffn_conv_b=m_ffn_conv_b, m_w_down=m_w_down, m_final_norm=m_final_norm, v_attn_norm=v_attn_norm, v_w_in=v_w_in, v_dn_conv_w=v_dn_conv_w, v_dn_a_log=v_dn_a_log, v_dn_dt_bias=v_dn_dt_bias, v_dn_norm_w=v_dn_norm_w, v_lru_conv_w=v_lru_conv_w, v_lru_conv_b=v_lru_conv_b, v_lru_wa=v_lru_wa, v_lru_ba=v_lru_ba, v_lru_wx=v_lru_wx, v_lru_bx=v_lru_bx, v_lru_lambda=v_lru_lambda, v_w_branch=v_w_branch, v_w_out=v_w_out, v_ffn_norm=v_ffn_norm, v_w_up=v_w_up, v_ffn_conv_w=v_ffn_conv_w, v_ffn_conv_b=v_ffn_conv_b, v_w_down=v_w_down, v_final_norm=v_final_norm)
    weights = {n: given[n] for n in TWIN_WEIGHTS}
    shared = {n: given[n] for n in SHARED_INPUTS}
    per_example = {n: given[n] for n in ['x', 'positions']}
    grad_fn = _jax.value_and_grad(_loss, argnums=(0, 1))

    def one_microbatch(ex, loss_target):
        ex = dict(ex)
        diff = ex.pop(TWIN_DIFF_INPUT)
        return grad_fn(weights, diff, {**shared, **ex}, loss_target)

    if N_MICROBATCH == 1:
        loss, (grad_w, grad_x) = one_microbatch(per_example, given["loss_target"])
    else:
        def body(carry, xs):
            loss_sum, grad_sum = carry
            l_k, (gw_k, gx_k) = one_microbatch(xs[0], xs[1])
            with _jax.named_scope("update"):
                return (loss_sum + l_k, _jax.tree.map(_jnp.add, grad_sum, gw_k)), gx_k

        init = (_jnp.zeros((), _jnp.float32), _jax.tree.map(_jnp.zeros_like, weights))
        (loss, grad_w), grad_x = _jax.lax.scan(body, init, (per_example, given["loss_target"]))
    with _jax.named_scope("update"):
        delta_w, new_m, new_v = {}, {}, {}
        for n in TWIN_WEIGHTS:
            delta_w[n], new_m[n], new_v[n] = _adamw(weights[n], grad_w[n], given["m_" + n], given["v_" + n])
    return (loss, grad_x, *[grad_w[n] for n in TWIN_WEIGHTS], *[delta_w[n] for n in TWIN_WEIGHTS],
            *[new_m[n] for n in TWIN_WEIGHTS], *[new_v[n] for n in TWIN_WEIGHTS])
```

```python
import functools
import math

import jax
import jax.numpy as jnp
from jax import lax
from jax.experimental import pallas as pl
from jax.experimental.pallas import tpu as pltpu

f32 = jnp.float32
bf16 = jnp.bfloat16

D_MODEL = 1024
DEPTH = 4
CHUNK = 64
EPS = 1e-6
DN_HEADS, DN_DK = 4, 128
RET_HEADS, RET_DK, RET_DV = 4, 64, 128
ROPE_BASE = 10000.0
LRU_C = 8.0
D_FF = 2816
N_DEV = 8
LANE = 128
VMEM_LIMIT = 56 * 1024 * 1024

ADAM_LR, ADAM_B1, ADAM_B2, ADAM_EPS, ADAM_WD, ADAM_STEP = 0.001, 0.9, 0.999, 1e-8, 0.01, 10

U_GATES, U_QKV, U_RV, U_RG, U_Z, U_CX, U_CG, U_RQ, U_RK, U_AB = (
    0, 3072, 4608, 5120, 5632, 6144, 6656, 7168, 7424, 7680)
U_PAD = 8192
_IN_SEGS = ((0, 1536, U_QKV), (1536, 8, U_AB), (1544, 512, U_Z), (2056, 256, U_RQ), (2312, 256, U_RK),
            (2568, 512, U_RV), (3080, 512, U_RG), (3592, 512, U_CX), (4104, 512, U_CG), (4616, 3072, U_GATES))
N_IN = 7688


def _params():
    return pltpu.CompilerParams(vmem_limit_bytes=VMEM_LIMIT)


def _pick(dim, pref):
    best = None
    for d in range(LANE, min(dim, pref) + 1, LANE):
        if dim % d == 0:
            best = d
    return best if best is not None else dim


@functools.partial(jax.custom_vjp, nondiff_argnums=(1, 2))
def sroll(x, shift, axis):
    return pltpu.roll(x, shift, axis)


def _sroll_fwd(x, shift, axis):
    return pltpu.roll(x, shift, axis), None


def _sroll_bwd(shift, axis, _, g):
    n = g.shape[axis]
    return (pltpu.roll(g, (n - shift) % n, axis),)


sroll.defvjp(_sroll_fwd, _sroll_bwd)

_DIMS = {"nn": (((1,), (0,)), ((), ())), "nt": (((1,), (1,)), ((), ())), "tn": (((0,), (0,)), ((), ()))}


def _dg(a, b, dims):
    return lax.dot_general(a.astype(bf16), b.astype(bf16), _DIMS[dims], preferred_element_type=f32)


@functools.partial(jax.custom_vjp, nondiff_argnums=(2,))
def bdot(a, b, dims):
    return _dg(a, b, dims)


def _bdot_fwd(a, b, dims):
    return _dg(a, b, dims), (a.astype(bf16), b.astype(bf16))


def _bdot_bwd(dims, res, g):
    a, b = res
    if dims == "nn":
        return _dg(g, b, "nt"), _dg(a, g, "tn")
    if dims == "nt":
        return _dg(g, b, "nn"), _dg(g, a, "tn")
    return _dg(b, g, "nt"), _dg(a, g, "nn")


bdot.defvjp(_bdot_fwd, _bdot_bwd)


def _fdot(a, b, dims):
    return lax.dot_general(a, b, _DIMS[dims], precision=lax.Precision.HIGHEST, preferred_element_type=f32)


@jax.custom_vjp
def unit_lower_inv(a):
    size = a.shape[0]
    row = lax.broadcasted_iota(jnp.int32, a.shape, 0)
    col = lax.broadcasted_iota(jnp.int32, a.shape, 1)
    eye = jnp.where(row == col, 1.0, 0.0).astype(f32)
    n = -a
    p = eye + n
    span = 2
    while span < size:
        n = _fdot(n, n, "nn")
        p = p + _fdot(p, n, "nn")
        span *= 2
    return p


def _uli_fwd(a):
    x = unit_lower_inv(a)
    return x, x


def _uli_bwd(x, g):
    return (-_fdot(_fdot(x, g, "tn"), x, "nt"),)


unit_lower_inv.defvjp(_uli_fwd, _uli_bwd)


def cumsum_rows(x):
    rows = x.shape[0]
    row = lax.broadcasted_iota(jnp.int32, x.shape, 0)
    s = 1
    while s < rows:
        x = x + jnp.where(row >= s, sroll(x, s, 0), 0.0)
        s *= 2
    return x


def _expm1(x):
    return jnp.tanh(0.5 * x) * (jnp.exp(x) + 1.0)


def _lane_pick(x, lane):
    idx = lax.broadcasted_iota(jnp.int32, x.shape, 1)
    return jnp.sum(jnp.where(idx == lane, x, 0.0), axis=1, keepdims=True)


def _row_pick(x, r):
    idx = lax.broadcasted_iota(jnp.int32, x.shape, 0)
    return jnp.sum(jnp.where(idx == r, x, 0.0), axis=0, keepdims=True)


def _causal_conv(x, halo, w, width):
    xe = jnp.concatenate([halo, x], axis=0)
    acc = xe * w[width - 1:width]
    for k in range(width - 1):
        acc = acc + sroll(xe, width - 1 - k, 0) * w[k:k + 1]
    return acc[8:]


def f_norm(ins, ps):
    (x,), (g,) = ins, ps
    return [x * lax.rsqrt(jnp.mean(x * x, axis=-1, keepdims=True) + EPS) * g]


def f_dn_pre(kind, mains, halos, ps):
    y = _causal_conv(mains[0], halos[0], ps[0], 4)
    y = y * jax.nn.sigmoid(y)
    if kind < 2:
        y = y * lax.rsqrt(jnp.sum(y * y, axis=-1, keepdims=True) + EPS)
    if kind == 0:
        y = y * (DN_DK ** -0.5)
    return [y]


def f_dn_gates(ins, ps):
    (u,), (p,) = ins, ps
    lane = lax.broadcasted_iota(jnp.int32, u.shape, 1)
    g = -jnp.exp(p[0:1]) * jax.nn.softplus(u + p[1:2])
    beta = jax.nn.sigmoid(u)
    return [jnp.where(lane < 4, g, jnp.where(lane < 8, beta, 0.0))]


def f_dn_post(ins, ps):
    (o, z), (nw,) = ins, ps
    y = o * lax.rsqrt(jnp.mean(o * o, axis=-1, keepdims=True) + EPS) * nw
    return [y * (z * jax.nn.sigmoid(z))]


def _rot_half(t):
    lane = lax.broadcasted_iota(jnp.int32, t.shape, 1)
    width = t.shape[1]
    first = (lane % RET_DK) < (RET_DK // 2)
    return jnp.where(first, -sroll(t, width - RET_DK // 2, 1), sroll(t, RET_DK // 2, 1))


def f_ret_pre(ins, ps):
    q, k, cos, sin = ins
    qr = q * cos + _rot_half(q) * sin
    kr = (k * cos + _rot_half(k) * sin) * (RET_DK ** -0.5)
    return [qr, kr]


def f_ret_post(ins, ps):
    o, g = ins
    mu = jnp.mean(o, axis=-1, keepdims=True)
    var = jnp.mean(jnp.square(o - mu), axis=-1, keepdims=True)
    return [(o - mu) * lax.rsqrt(var + EPS) * (g * jax.nn.sigmoid(g))]


def f_lru_pre(mains, halos, ps):
    cw, cb, wa, ba, wx, bx, lam = ps
    xc = _causal_conv(mains[0], halos[0], cw, 4) + cb
    r = jax.nn.sigmoid(bdot(xc, wa, "nn") + ba)
    i = jax.nn.sigmoid(bdot(xc, wx, "nn") + bx)
    log_a = -LRU_C * r * jax.nn.softplus(-lam)
    a = jnp.exp(log_a)
    b = jnp.sqrt(-_expm1(2.0 * log_a)) * (i * xc)
    return [a, b]


def f_lru_post(ins, ps):
    h, g = ins
    return [h * jax.nn.gelu(g)]


def f_merge(ins, ps):
    g0, g1, g2, b0, b1, b2 = ins
    return [jax.nn.sigmoid(g0) * b0 + jax.nn.sigmoid(g1) * b1 + jax.nn.sigmoid(g2) * b2]


def f_ffn_mid(mains, halos, ps):
    cwg, cwv, cbg, cbv = ps
    gate = _causal_conv(mains[0], halos[0], cwg, 3) + cbg
    val = _causal_conv(mains[1], halos[1], cwv, 3) + cbv
    return [gate * jax.nn.sigmoid(gate) * val]


def mm(a, b, dims, name, add=None, tm=512, tn=1024, tk=1024):
    if dims == "tn":
        kdim, m = a.shape
        n = b.shape[1]
    else:
        m, kdim = a.shape
        n = b.shape[0] if dims == "nt" else b.shape[1]
    tm, tn, tk = _pick(m, tm), _pick(n, tn), _pick(kdim, tk)
    nk = kdim // tk
    a_spec = pl.BlockSpec((tk, tm), lambda i, j, k: (k, i)) if dims == "tn" else pl.BlockSpec((tm, tk), lambda i, j, k: (i, k))
    b_spec = pl.BlockSpec((tn, tk), lambda i, j, k: (j, k)) if dims == "nt" else pl.BlockSpec((tk, tn), lambda i, j, k: (k, j))
    o_spec = pl.BlockSpec((tm, tn), lambda i, j, k: (i, j))
    has_add = add is not None

    def body(*refs):
        if has_add:
            a_ref, b_ref, add_ref, o_ref, acc_ref = refs
        else:
            a_ref, b_ref, o_ref, acc_ref = refs
        k = pl.program_id(2)

        @pl.when(k == 0)
        def _():
            acc_ref[...] = jnp.zeros_like(acc_ref)

        acc_ref[...] += _dg(a_ref[...], b_ref[...], dims)

        @pl.when(k == nk - 1)
        def _():
            o_ref[...] = acc_ref[...] + add_ref[...] if has_add else acc_ref[...]

    args = (a, b, add) if has_add else (a, b)
    in_specs = [a_spec, b_spec] + ([o_spec] if has_add else [])
    return pl.pallas_call(
        body, name=name, grid=(m // tm, n // tn, nk), in_specs=in_specs, out_specs=o_spec,
        out_shape=jax.ShapeDtypeStruct((m, n), f32), scratch_shapes=[pltpu.VMEM((tm, tn), f32)],
        compiler_params=_params())(*args)


def rowmap(fn, ins, params, outs, ncol, name, rows=256):
    n = ins[0][0].shape[0]
    r = min(rows, n)
    nin, npar = len(ins), len(params)

    def body(*refs):
        vals = [x[...] for x in refs[:nin]]
        pv = [p[...] for p in refs[nin:nin + npar]]
        for o_ref, o in zip(refs[nin + npar:], fn(vals, pv)):
            o_ref[...] = o.astype(o_ref.dtype)

    in_specs = [pl.BlockSpec((r, cb), functools.partial(lambda j, i, off: (i, off + j), off=off)) for _, cb, off in ins]
    in_specs += [pl.BlockSpec(bs, functools.partial(lambda j, i, f: f(j), f=f)) for _, bs, f in params]
    out_specs = [pl.BlockSpec((r, cb), lambda j, i: (i, j)) for cb, _ in outs]
    out_shape = [jax.ShapeDtypeStruct((n, cb * ncol), dt) for cb, dt in outs]
    res = pl.pallas_call(body, name=name, grid=(ncol, n // r), in_specs=in_specs, out_specs=out_specs,
                         out_shape=out_shape, compiler_params=_params())(*[a for a, _, _ in ins], *[a for a, _, _ in params])
    return res


def rowmap_bwd(fn, ins, params, douts, ncol, name, rows=256, add=None):
    n = ins[0][0].shape[0]
    r = min(rows, n)
    nin, npar, nout = len(ins), len(params), len(douts)
    has_add = add is not None

    def body(*refs):
        vals = [x[...] for x in refs[:nin]]
        pv = [p[...] for p in refs[nin:nin + npar]]
        dys = [d[...] for d in refs[nin + npar:nin + npar + nout]]
        k0 = nin + npar + nout
        add_ref = refs[k0] if has_add else None
        k0 += 1 if has_add else 0
        din_refs = refs[k0:k0 + nin]
        dp_refs = refs[k0 + nin:]
        _, vjp = jax.vjp(fn, vals, pv)
        dvals, dpv = vjp(dys)
        for idx, (d_ref, d) in enumerate(zip(din_refs, dvals)):
            d_ref[...] = d + add_ref[...] if (has_add and idx == 0) else d

        @pl.when(pl.program_id(1) == 0)
        def _():
            for d_ref in dp_refs:
                d_ref[...] = jnp.zeros_like(d_ref)

        for d_ref, d in zip(dp_refs, dpv):
            d_ref[...] += d

    in_specs = [pl.BlockSpec((r, cb), functools.partial(lambda j, i, off: (i, off + j), off=off)) for _, cb, off in ins]
    in_specs += [pl.BlockSpec(bs, functools.partial(lambda j, i, f: f(j), f=f)) for _, bs, f in params]
    in_specs += [pl.BlockSpec((r, d.shape[1] // ncol), lambda j, i: (i, j)) for d in douts]
    if has_add:
        in_specs += [pl.BlockSpec((r, ins[0][1]), lambda j, i: (i, j))]
    out_specs = [pl.BlockSpec((r, cb), lambda j, i: (i, j)) for _, cb, _ in ins]
    pshapes = [tuple(d for d in bs if d is not None) for _, bs, _ in params]
    out_specs += [pl.BlockSpec((None,) + ps, functools.partial(lambda j, i, nd: (j,) + (0,) * nd, nd=len(ps))) for ps in pshapes]
    out_shape = [jax.ShapeDtypeStruct((n, cb * ncol), f32) for _, cb, _ in ins]
    out_shape += [jax.ShapeDtypeStruct((ncol,) + ps, f32) for ps in pshapes]
    args = [a for a, _, _ in ins] + [a for a, _, _ in params] + list(douts) + ([add] if has_add else [])
    res = pl.pallas_call(body, name=name, grid=(ncol, n // r), in_specs=in_specs, out_specs=out_specs,
                         out_shape=out_shape, compiler_params=_params())(*args)
    return res[:nin], res[nin:]


SEQ_ROWS = 256


def seqmap(fn, ins, params, nouts, ncol, name):
    bsz, seq, _ = ins[0][0].shape
    r = min(SEQ_ROWS, seq)
    nin, npar = len(ins), len(params)

    def body(*refs):
        in_refs = refs[:nin]
        pv = [p[...] for p in refs[nin:nin + npar]]
        out_refs = refs[nin + npar:]

        def step(i, carry):
            r0 = pl.multiple_of(i * r, r)
            h0 = pl.multiple_of(jnp.maximum(r0 - 8, 0), 8)
            mains = [x[pl.ds(r0, r), :] for x in in_refs]
            halos = [jnp.where(i == 0, 0.0, x[pl.ds(h0, 8), :]) for x in in_refs]
            for o_ref, o in zip(out_refs, fn(mains, halos, pv)):
                o_ref[pl.ds(r0, r), :] = o
            return carry

        lax.fori_loop(0, seq // r, step, 0)

    in_specs = [pl.BlockSpec((None, seq, LANE), functools.partial(lambda j, b, off: (b, 0, off + j), off=off)) for _, off in ins]
    in_specs += [pl.BlockSpec(bs, functools.partial(lambda j, b, f: f(j), f=f)) for _, bs, f in params]
    out_specs = [pl.BlockSpec((None, seq, LANE), lambda j, b: (b, 0, j)) for _ in range(nouts)]
    out_shape = [jax.ShapeDtypeStruct((bsz, seq, LANE * ncol), f32) for _ in range(nouts)]
    return pl.pallas_call(body, name=name, grid=(ncol, bsz), in_specs=in_specs, out_specs=out_specs,
                          out_shape=out_shape, compiler_params=_params())(*[a for a, _ in ins], *[a for a, _, _ in params])


def seqmap_bwd(fn, ins, params, douts, ncol, name):
    bsz, seq, _ = ins[0][0].shape
    r = min(SEQ_ROWS, seq)
    nin, npar, nout = len(ins), len(params), len(douts)

    def body(*refs):
        in_refs = refs[:nin]
        pv = [p[...] for p in refs[nin:nin + npar]]
        dy_refs = refs[nin + npar:nin + npar + nout]
        din_refs = refs[nin + npar + nout:nin + npar + nout + nin]
        dp_refs = refs[nin + npar + nout + nin:]

        def step(i, dp_acc):
            r0 = pl.multiple_of(i * r, r)
            h0 = pl.multiple_of(jnp.maximum(r0 - 8, 0), 8)
            mains = [x[pl.ds(r0, r), :] for x in in_refs]
            halos_raw = [x[pl.ds(h0, 8), :] for x in in_refs]

            def tile(mains, halos_raw, pv):
                return fn(mains, [jnp.where(i == 0, 0.0, h) for h in halos_raw], pv)

            _, vjp = jax.vjp(tile, mains, halos_raw, pv)
            dm, dh, dp = vjp([d[pl.ds(r0, r), :] for d in dy_refs])
            for d_ref, m, h in zip(din_refs, dm, dh):
                d_ref[pl.ds(r0, r), :] = m
                d_ref[pl.ds(h0, 8), :] += h
            return [acc + d for acc, d in zip(dp_acc, dp)]

        dp = lax.fori_loop(0, seq // r, step, [jnp.zeros(p.shape, f32) for p in pv])

        @pl.when(pl.program_id(1) == 0)
        def _():
            for d_ref in dp_refs:
                d_ref[...] = jnp.zeros_like(d_ref)

        for d_ref, d in zip(dp_refs, dp):
            d_ref[...] += d

    in_specs = [pl.BlockSpec((None, seq, LANE), functools.partial(lambda j, b, off: (b, 0, off + j), off=off)) for _, off in ins]
    in_specs += [pl.BlockSpec(bs, functools.partial(lambda j, b, f: f(j), f=f)) for _, bs, f in params]
    in_specs += [pl.BlockSpec((None, seq, LANE), lambda j, b: (b, 0, j)) for _ in range(nout)]
    out_specs = [pl.BlockSpec((None, seq, LANE), lambda j, b: (b, 0, j)) for _ in range(nin)]
    pshapes = [tuple(d for d in bs if d is not None) for _, bs, _ in params]
    out_specs += [pl.BlockSpec((None,) + ps, functools.partial(lambda j, b, nd: (j,) + (0,) * nd, nd=len(ps))) for ps in pshapes]
    out_shape = [jax.ShapeDtypeStruct((bsz, seq, LANE * ncol), f32) for _ in range(nin)]
    out_shape += [jax.ShapeDtypeStruct((ncol,) + ps, f32) for ps in pshapes]
    args = [a for a, _ in ins] + [a for a, _, _ in params] + list(douts)
    res = pl.pallas_call(body, name=name, grid=(ncol, bsz), in_specs=in_specs, out_specs=out_specs,
                         out_shape=out_shape, compiler_params=_params())(*args)
    return res[:nin], res[nin:]


def _tri_masks():
    row = lax.broadcasted_iota(jnp.int32, (CHUNK, CHUNK), 0)
    col = lax.broadcasted_iota(jnp.int32, (CHUNK, CHUNK), 1)
    return row >= col, row > col


def dn_chunk(state, q, k, v, gb):
    causal, strict = _tri_masks()
    gc = cumsum_rows(gb)
    gct = gc.T
    new_state, outs = [], []
    for h in range(DN_HEADS):
        sl = slice(h * DN_DK, (h + 1) * DN_DK)
        qh, kh, vh, s = q[:, sl], k[:, sl], v[:, sl], state[h]
        g_col = _lane_pick(gc, h)
        beta = _lane_pick(gb, DN_HEADS + h)
        g_row = _row_pick(gct, h)[:, :CHUNK]
        diff = g_col - g_row
        decay = jnp.where(causal, jnp.exp(jnp.where(causal, diff, 0.0)), 0.0)
        k_beta = kh * beta
        kk = bdot(k_beta, kh, "nt")
        t_inv = unit_lower_inv(jnp.where(strict, kk * decay, 0.0))
        eg = jnp.exp(g_col)
        u = bdot(t_inv, vh * beta, "nn")
        w = bdot(t_inv, k_beta * eg, "nn")
        attn = bdot(qh, kh, "nt") * decay
        g_last = _row_pick(g_col, CHUNK - 1)
        v_new = u - bdot(w, s, "nn")
        outs.append(bdot(qh * eg, s, "nn") + bdot(attn, v_new, "nn"))
        new_state.append(s * jnp.exp(g_last) + bdot(kh * jnp.exp(g_last - g_col), v_new, "tn"))
    return new_state, jnp.concatenate(outs, axis=1)


def ret_chunk(state, q, k, v):
    causal, _ = _tri_masks()
    row = lax.broadcasted_iota(jnp.int32, (CHUNK, CHUNK), 0)
    col = lax.broadcasted_iota(jnp.int32, (CHUNK, CHUNK), 1)
    dist = (row - col).astype(f32)
    idx = lax.broadcasted_iota(jnp.int32, (CHUNK, 1), 0).astype(f32)
    lane = lax.broadcasted_iota(jnp.int32, q.shape, 1)
    new_state, outs = [], []
    for h in range(RET_HEADS):
        lg = math.log(1.0 - 2.0 ** (-5.0 - h))
        in_head = (lane // RET_DK) == h
        vh, s = v[:, h * RET_DV:(h + 1) * RET_DV], state[h]
        dmask = jnp.where(causal, jnp.exp(jnp.where(causal, dist, 0.0) * lg), 0.0)
        scores = bdot(jnp.where(in_head, q, 0.0), k, "nt") * dmask
        inner = bdot(scores, vh, "nn")
        cross = bdot(q, s, "nn") * jnp.exp((idx + 1.0) * lg)
        outs.append(cross + inner)
        kz = jnp.where(in_head, k, 0.0) * jnp.exp((CHUNK - 1.0 - idx) * lg)
        new_state.append(s * math.exp(CHUNK * lg) + bdot(kz, vh, "tn"))
    return new_state, jnp.concatenate(outs, axis=1)


def chunk_scan(chunk_fn, ins, state_shape, out_width, name):
    bsz, seq, _ = ins[0].shape
    nchunk = seq // CHUNK
    nin = len(ins)
    nh = state_shape[0]

    def body(*refs):
        in_refs = refs[:nin]
        o_ref, ck_ref, s_ref = refs[nin:]

        @pl.when(pl.program_id(1) == 0)
        def _():
            s_ref[...] = jnp.zeros_like(s_ref)

        state = [s_ref[h] for h in range(nh)]
        for h in range(nh):
            ck_ref[h] = state[h]
        new_state, out = chunk_fn(state, *[x[...] for x in in_refs])
        o_ref[...] = out
        for h in range(nh):
            s_ref[h] = new_state[h]

    in_specs = [pl.BlockSpec((None, CHUNK, x.shape[2]), lambda b, n: (b, n, 0)) for x in ins]
    out_specs = [pl.BlockSpec((None, CHUNK, out_width), lambda b, n: (b, n, 0)),
                 pl.BlockSpec((None, None) + tuple(state_shape), lambda b, n: (b, n, 0, 0, 0))]
    out_shape = [jax.ShapeDtypeStruct((bsz, seq, out_width), f32),
                 jax.ShapeDtypeStruct((bsz, nchunk) + tuple(state_shape), f32)]
    return pl.pallas_call(body, name=name, grid=(bsz, nchunk), in_specs=in_specs, out_specs=out_specs, out_shape=out_shape,
                          scratch_shapes=[pltpu.VMEM(tuple(state_shape), f32)], compiler_params=_params())(*ins)


def chunk_scan_bwd(chunk_fn, ins, ckpt, dout, name):
    bsz, seq, _ = ins[0].shape
    nchunk = seq // CHUNK
    nin = len(ins)
    state_shape = ckpt.shape[2:]
    nh = state_shape[0]

    def body(*refs):
        in_refs = refs[:nin]
        ck_ref, do_ref = refs[nin:nin + 2]
        din_refs = refs[nin + 2:nin + 2 + nin]
        ds_ref = refs[-1]

        @pl.when(pl.program_id(1) == 0)
        def _():
            ds_ref[...] = jnp.zeros_like(ds_ref)

        state = [ck_ref[h] for h in range(nh)]
        _, vjp = jax.vjp(chunk_fn, state, *[x[...] for x in in_refs])
        grads = vjp(([ds_ref[h] for h in range(nh)], do_ref[...]))
        for h in range(nh):
            ds_ref[h] = grads[0][h]
        for d_ref, d in zip(din_refs, grads[1:]):
            d_ref[...] = d

    rev = lambda b, n: (b, nchunk - 1 - n, 0)
    in_specs = [pl.BlockSpec((None, CHUNK, x.shape[2]), rev) for x in ins]
    in_specs += [pl.BlockSpec((None, None) + tuple(state_shape), lambda b, n: (b, nchunk - 1 - n, 0, 0, 0)),
                 pl.BlockSpec((None, CHUNK, dout.shape[2]), rev)]
    out_specs = [pl.BlockSpec((None, CHUNK, x.shape[2]), rev) for x in ins]
    out_shape = [jax.ShapeDtypeStruct(x.shape, f32) for x in ins]
    return pl.pallas_call(body, name=name, grid=(bsz, nchunk), in_specs=in_specs, out_specs=out_specs, out_shape=out_shape,
                          scratch_shapes=[pltpu.VMEM(tuple(state_shape), f32)], compiler_params=_params())(*ins, ckpt, dout)


LRU_ROWS = 512


def lru_scan(a, b):
    bsz, seq, width = a.shape
    rb = min(LRU_ROWS, seq)

    def body(a_ref, b_ref, h_ref, hp_ref, carry_ref):
        @pl.when(pl.program_id(1) == 0)
        def _():
            carry_ref[...] = jnp.zeros_like(carry_ref)

        row = lax.broadcasted_iota(jnp.int32, (8, width), 0)

        def tile(t, hprev):
            r0 = pl.multiple_of(t * 8, 8)
            ca, cbv = a_ref[pl.ds(r0, 8), :], b_ref[pl.ds(r0, 8), :]
            for s in (1, 2, 4):
                m = row >= s
                cbv = jnp.where(m, ca * pltpu.roll(cbv, s, 0) + cbv, cbv)
                ca = jnp.where(m, ca * pltpu.roll(ca, s, 0), ca)
            h = cbv + ca * hprev
            h_ref[pl.ds(r0, 8), :] = h
            hp_ref[pl.ds(r0, 8), :] = jnp.where(row == 0, hprev, pltpu.roll(h, 1, 0))
            return _row_pick(h, 7)

        carry_ref[0:1, :] = lax.fori_loop(0, rb // 8, tile, carry_ref[0:1, :])

    spec = pl.BlockSpec((None, rb, width), lambda bi, i: (bi, i, 0))
    return pl.pallas_call(body, name="lru_scan", grid=(bsz, seq // rb), in_specs=[spec, spec], out_specs=[spec, spec],
                          out_shape=[jax.ShapeDtypeStruct(a.shape, f32)] * 2,
                          scratch_shapes=[pltpu.VMEM((8, width), f32)], compiler_params=_params())(a, b)


def lru_scan_bwd(a, hp, dh):
    bsz, seq, width = a.shape
    rb = min(LRU_ROWS, seq)
    nblk = seq // rb

    def body(a_ref, hp_ref, dh_ref, da_ref, db_ref, carry_ref):
        @pl.when(pl.program_id(1) == 0)
        def _():
            carry_ref[...] = jnp.zeros_like(carry_ref)

        row = lax.broadcasted_iota(jnp.int32, (8, width), 0)
        ntile = rb // 8

        def tile(t, mu_next):
            r0 = pl.multiple_of((ntile - 1 - t) * 8, 8)
            ca, dh_t = a_ref[pl.ds(r0, 8), :], dh_ref[pl.ds(r0, 8), :]
            cbv = ca * dh_t
            for s in (1, 2, 4):
                m = row < 8 - s
                cbv = jnp.where(m, ca * pltpu.roll(cbv, 8 - s, 0) + cbv, cbv)
                ca = jnp.where(m, ca * pltpu.roll(ca, 8 - s, 0), ca)
            mu = cbv + ca * mu_next
            lam = dh_t + jnp.where(row == 7, mu_next, pltpu.roll(mu, 7, 0))
            db_ref[pl.ds(r0, 8), :] = lam
            da_ref[pl.ds(r0, 8), :] = lam * hp_ref[pl.ds(r0, 8), :]
            return _row_pick(mu, 0)

        carry_ref[0:1, :] = lax.fori_loop(0, ntile, tile, carry_ref[0:1, :])

    spec = pl.BlockSpec((None, rb, width), lambda bi, i: (bi, nblk - 1 - i, 0))
    return pl.pallas_call(body, name="lru_scan_bwd", grid=(bsz, nblk), in_specs=[spec] * 3, out_specs=[spec, spec],
                          out_shape=[jax.ShapeDtypeStruct(a.shape, f32)] * 2,
                          scratch_shapes=[pltpu.VMEM((8, width), f32)], compiler_params=_params())(a, hp, dh)


def final_loss(x, g, target):
    n, d = x.shape
    r = min(256, n)

    def body(x_ref, g_ref, t_ref, loss_ref, dx_ref, dg_ref):
        @pl.when(pl.program_id(0) == 0)
        def _():
            loss_ref[...] = jnp.zeros_like(loss_ref)
            dg_ref[...] = jnp.zeros_like(dg_ref)

        tgt = t_ref[...]

        def loss_fn(xv, gv):
            y = f_norm([xv], [gv])[0]
            return 0.5 * jnp.sum(jnp.mean(jnp.square(y - tgt), axis=-1, keepdims=True), axis=0, keepdims=True)

        val, vjp = jax.vjp(loss_fn, x_ref[...], g_ref[...])
        dx, dg = vjp(jnp.ones_like(val))
        loss_ref[...] += val
        dx_ref[...] = dx
        dg_ref[...] += dg

    row = pl.BlockSpec((r, d), lambda i: (i, 0))
    return pl.pallas_call(
        body, name="final_loss", grid=(n // r,), in_specs=[row, pl.BlockSpec((1, d), lambda i: (0, 0)), row],
        out_specs=[pl.BlockSpec((8, LANE), lambda i: (0, 0)), row, pl.BlockSpec((1, d), lambda i: (0, 0))],
        out_shape=[jax.ShapeDtypeStruct((8, LANE), f32), jax.ShapeDtypeStruct((n, d), f32), jax.ShapeDtypeStruct((1, d), f32)],
        compiler_params=_params())(x, g, target)


def exchange(x, scatter, name):
    blk = x.shape[1:] if scatter else x.shape

    def body(x_ref, o_ref, send_sems, recv_sems, local_sem):
        mx, my, mc = lax.axis_index("x"), lax.axis_index("y"), lax.axis_index("c")
        me = 4 * mx + 2 * my + mc
        src_own = x_ref.at[me] if scatter else x_ref
        local = pltpu.make_async_copy(src_own, o_ref.at[me], local_sem)
        local.start()
        copies = []
        for k in range(1, N_DEV):
            px, py, pc = (mx + (k >> 2)) % 2, (my + ((k >> 1) & 1)) % 2, (mc + (k & 1)) % 2
            src = x_ref.at[4 * px + 2 * py + pc] if scatter else x_ref
            cp = pltpu.make_async_remote_copy(
                src_ref=src, dst_ref=o_ref.at[me], send_sem=send_sems.at[k - 1], recv_sem=recv_sems.at[k - 1],
                device_id=(px, py, pc), device_id_type=pl.DeviceIdType.MESH)
            cp.start()
            copies.append(cp)
        for cp in copies:
            cp.wait()
        local.wait()

    return pl.pallas_call(
        body, name=name, in_specs=[pl.BlockSpec(memory_space=pl.ANY)], out_specs=pl.BlockSpec(memory_space=pl.ANY),
        out_shape=jax.ShapeDtypeStruct((N_DEV,) + tuple(blk), x.dtype),
        scratch_shapes=[pltpu.SemaphoreType.DMA((N_DEV - 1,)), pltpu.SemaphoreType.DMA((N_DEV - 1,)), pltpu.SemaphoreType.DMA],
        compiler_params=pltpu.CompilerParams(has_side_effects=True))(x)


def sum_slots(x, name):
    _, rows_total, _ = x.shape
    r = _pick_rows(rows_total, 2048)

    def body(x_ref, o_ref):
        acc = x_ref[0].astype(f32)
        for s in range(1, N_DEV):
            acc = acc + x_ref[s].astype(f32)
        o_ref[...] = acc

    return pl.pallas_call(body, name=name, grid=(rows_total // r,),
                          in_specs=[pl.BlockSpec((N_DEV, r, LANE), lambda i: (0, i, 0))],
                          out_specs=pl.BlockSpec((r, LANE), lambda i: (i, 0)),
                          out_shape=jax.ShapeDtypeStruct((rows_total, LANE), f32), compiler_params=_params())(x)


def _pick_rows(total, pref):
    best = None
    for d in range(16, min(total, pref) + 1, 16):
        if total % d == 0:
            best = d
    return best if best is not None else total


def adamw(w, g, m, v, name):
    shape = w.shape
    if w.ndim == 1:
        w2, g2, m2, v2 = (t.reshape(1, -1) for t in (w, g, m, v))
    else:
        w2, g2, m2, v2 = (t.reshape(-1, shape[-1]) for t in (w, g, m, v))
    rows_total, cols = w2.shape
    r = _pick_rows(rows_total, max(16, (512 * 1024) // max(cols, 1) // 16 * 16))
    c1, c2 = 1.0 / (1.0 - ADAM_B1 ** ADAM_STEP), 1.0 / (1.0 - ADAM_B2 ** ADAM_STEP)

    def body(w_ref, g_ref, m_ref, v_ref, d_ref, nm_ref, nv_ref):
        gv = g_ref[...]
        nm = ADAM_B1 * m_ref[...] + (1.0 - ADAM_B1) * gv
        nv = ADAM_B2 * v_ref[...] + (1.0 - ADAM_B2) * jnp.square(gv)
        d_ref[...] = -ADAM_LR * ((nm * c1) / (jnp.sqrt(nv * c2) + ADAM_EPS) + ADAM_WD * w_ref[...])
        nm_ref[...] = nm
        nv_ref[...] = nv

    spec = pl.BlockSpec((r, cols), lambda i: (i, 0))
    outs = pl.pallas_call(body, name=name, grid=(rows_total // r,), in_specs=[spec] * 4, out_specs=[spec] * 3,
                          out_shape=[jax.ShapeDtypeStruct((rows_total, cols), f32)] * 3, compiler_params=_params())(w2, g2, m2, v2)
    return tuple(o.reshape(shape) for o in outs)


def _const(j):
    return lambda _: j


def _layer_fwd(x, wl, cos, sin, bsz, seq):
    n = x.shape[0]
    sv = {"x_in": x}
    row1 = lambda a: (a, (1, a.shape[1]), lambda j: (0, 0))
    h = rowmap(f_norm, [(x, D_MODEL, 0)], [row1(wl["attn_norm"])], [(D_MODEL, f32)], 1, "norm_fwd")[0]
    u = mm(h, wl["w_in"], "nn", "mm_in")
    sv["h"], sv["u"] = h, u
    u3 = u.reshape(bsz, seq, U_PAD)

    qkv = []
    for kind in range(3):
        cw = (wl["dn_conv_w"], (4, LANE), functools.partial(lambda j, kind: (0, 4 * kind + j), kind=kind))
        qkv.append(seqmap(functools.partial(f_dn_pre, kind), [(u3, U_QKV // LANE + 4 * kind)], [cw], 1, 4, "dn_pre%d" % kind)[0])
    gb = rowmap(f_dn_gates, [(u, LANE, U_AB // LANE)], [(wl["dn_gate_p"], (8, LANE), lambda j: (0, 0))], [(LANE, f32)], 1,
                "dn_gates")[0]
    gb3 = gb.reshape(bsz, seq, LANE)
    o_a, ck_a = chunk_scan(dn_chunk, qkv + [gb3], (DN_HEADS, DN_DK, DN_DK), 512, "dn_scan")
    y_a = rowmap(f_dn_post, [(o_a.reshape(n, 512), LANE, 0), (u, LANE, U_Z // LANE)],
                 [(wl["dn_norm_w"], (1, LANE), lambda j: (0, 0))], [(LANE, f32)], 4, "dn_post")[0]
    sv.update(q_a=qkv[0], k_a=qkv[1], v_a=qkv[2], gb=gb3, o_a=o_a, ck_a=ck_a, y_a=y_a)

    q_b, k_b = rowmap(f_ret_pre, [(u, 256, U_RQ // 256), (u, 256, U_RK // 256), (cos, 256, 0), (sin, 256, 0)], [],
                      [(256, f32), (256, f32)], 1, "ret_pre")
    q_b3, k_b3 = q_b.reshape(bsz, seq, 256), k_b.reshape(bsz, seq, 256)
    v_b3 = lax.slice_in_dim(u3, U_RV, U_RV + 512, axis=2)
    o_b, ck_b = chunk_scan(ret_chunk, [q_b3, k_b3, v_b3], (RET_HEADS, 256, RET_DV), 512, "ret_scan")
    y_b = rowmap(f_ret_post, [(o_b.reshape(n, 512), LANE, 0), (u, LANE, U_RG // LANE)], [], [(LANE, f32)], 4, "ret_post")[0]
    sv.update(q_b=q_b3, k_b=k_b3, v_b=v_b3, o_b=o_b, ck_b=ck_b, y_b=y_b)

    lru_params = _lru_params(wl)
    a_c, b_c = seqmap(f_lru_pre, [(u3, U_CX // LANE)], lru_params, 2, 4, "lru_pre")
    h_c, hp_c = lru_scan(a_c, b_c)
    y_c = rowmap(f_lru_post, [(h_c.reshape(n, 512), 512, 0), (u, 512, U_CG // 512)], [], [(512, f32)], 1, "lru_post")[0]
    sv.update(a_c=a_c, hp_c=hp_c, h_c=h_c, y_c=y_c)

    br = [mm(y, wl["w_branch"][i], "nn", "mm_branch") for i, y in enumerate((y_a, y_b, y_c))]
    merged = rowmap(f_merge, [(u, D_MODEL, i) for i in range(3)] + [(b, D_MODEL, 0) for b in br], [], [(D_MODEL, f32)], 1,
                    "merge")[0]
    x_mid = mm(merged, wl["w_out"], "nn", "mm_out", add=x)
    sv.update(br=br, merged=merged, x_mid=x_mid)

    h2 = rowmap(f_norm, [(x_mid, D_MODEL, 0)], [row1(wl["ffn_norm"])], [(D_MODEL, f32)], 1, "norm_fwd")[0]
    up = mm(h2, wl["w_up"], "nn", "mm_up")
    act = seqmap(f_ffn_mid, [(up.reshape(bsz, seq, 2 * D_FF), 0), (up.reshape(bsz, seq, 2 * D_FF), D_FF // LANE)],
                 _ffn_params(wl), 1, D_FF // LANE, "ffn_mid")[0]
    act = act.reshape(n, D_FF)
    x_out = mm(act, wl["w_down"], "nn", "mm_down", add=x_mid)
    sv.update(h2=h2, up=up, act=act)
    return x_out, sv


def _lru_params(wl):
    col = lambda a: (a, (a.shape[0], LANE), lambda j: (0, j))
    blk = lambda a: (a, (None, LANE, LANE), lambda j: (j, 0, 0))
    return [col(wl["lru_conv_w"]), col(wl["lru_conv_b"]), blk(wl["lru_wa"]), col(wl["lru_ba"]), blk(wl["lru_wx"]),
            col(wl["lru_bx"]), col(wl["lru_lambda"])]


def _ffn_params(wl):
    nb = D_FF // LANE
    return [(wl["ffn_conv_w"], (3, LANE), lambda j: (0, j)), (wl["ffn_conv_w"], (3, LANE), lambda j: (0, nb + j)),
            (wl["ffn_conv_b"], (1, LANE), lambda j: (0, j)), (wl["ffn_conv_b"], (1, LANE), lambda j: (0, nb + j))]


def _layer_bwd(dx, sv, wl, cos, sin, bsz, seq):
    n = dx.shape[0]
    gr = {}
    u, x_in, x_mid = sv["u"], sv["x_in"], sv["x_mid"]
    u3 = u.reshape(bsz, seq, U_PAD)
    row1 = lambda a: (a, (1, a.shape[1]), lambda j: (0, 0))

    d_act = mm(dx, wl["w_down"], "nt", "mm_down_dx")
    gr["w_down"] = mm(sv["act"], dx, "tn", "mm_down_dw")
    up3 = sv["up"].reshape(bsz, seq, 2 * D_FF)
    (d_gate, d_val), dps = seqmap_bwd(f_ffn_mid, [(up3, 0), (up3, D_FF // LANE)], _ffn_params(wl),
                                      [d_act.reshape(bsz, seq, D_FF)], D_FF // LANE, "ffn_mid_bwd")
    gr["ffn_conv_w"] = jnp.concatenate([_cols(dps[0]), _cols(dps[1])], axis=1)
    gr["ffn_conv_b"] = jnp.concatenate([_cols(dps[2]), _cols(dps[3])], axis=1)
    d_up = jnp.concatenate([d_gate, d_val], axis=2).reshape(n, 2 * D_FF)
    d_h2 = mm(d_up, wl["w_up"], "nt", "mm_up_dx")
    gr["w_up"] = mm(sv["h2"], d_up, "tn", "mm_up_dw")
    (dx_mid,), (dg,) = rowmap_bwd(f_norm, [(x_mid, D_MODEL, 0)], [row1(wl["ffn_norm"])], [d_h2], 1, "norm_bwd", add=dx)
    gr["ffn_norm"] = dg[0, 0]

    d_merged = mm(dx_mid, wl["w_out"], "nt", "mm_out_dx")
    gr["w_out"] = mm(sv["merged"], dx_mid, "tn", "mm_out_dw")
    dm, _ = rowmap_bwd(f_merge, [(u, D_MODEL, i) for i in range(3)] + [(b, D_MODEL, 0) for b in sv["br"]], [], [d_merged], 1,
                       "merge_bwd")
    d_gl, d_br = dm[:3], dm[3:]
    ys = (sv["y_a"], sv["y_b"], sv["y_c"])
    d_ys = [mm(d_br[i], wl["w_branch"][i], "nt", "mm_branch_dx") for i in range(3)]
    gr["w_branch"] = jnp.stack([mm(ys[i], d_br[i], "tn", "mm_branch_dw") for i in range(3)])

    (d_hc, d_cg), _ = rowmap_bwd(f_lru_post, [(sv["h_c"].reshape(n, 512), 512, 0), (u, 512, U_CG // 512)], [], [d_ys[2]], 1,
                                 "lru_post_bwd")
    d_a, d_b = lru_scan_bwd(sv["a_c"], sv["hp_c"], d_hc.reshape(bsz, seq, 512))
    (d_cx,), dps = seqmap_bwd(f_lru_pre, [(u3, U_CX // LANE)], _lru_params(wl), [d_a, d_b], 4, "lru_pre_bwd")
    gr["lru_conv_w"], gr["lru_conv_b"] = _cols(dps[0]), _cols(dps[1])
    gr["lru_wa"], gr["lru_ba"], gr["lru_wx"], gr["lru_bx"] = dps[2], dps[3][:, 0], dps[4], dps[5][:, 0]
    gr["lru_lambda"] = _cols(dps[6])

    (d_ob, d_rg), _ = rowmap_bwd(f_ret_post, [(sv["o_b"].reshape(n, 512), LANE, 0), (u, LANE, U_RG // LANE)], [], [d_ys[1]], 4,
                                 "ret_post_bwd")
    d_qb, d_kb, d_rv = chunk_scan_bwd(ret_chunk, [sv["q_b"], sv["k_b"], sv["v_b"]], sv["ck_b"], d_ob.reshape(bsz, seq, 512),
                                      "ret_scan_bwd")
    dr, _ = rowmap_bwd(f_ret_pre, [(u, 256, U_RQ // 256), (u, 256, U_RK // 256), (cos, 256, 0), (sin, 256, 0)], [],
                       [d_qb.reshape(n, 256), d_kb.reshape(n, 256)], 1, "ret_pre_bwd")
    d_rq, d_rk = dr[0], dr[1]

    (d_oa, d_z), (dnw,) = rowmap_bwd(f_dn_post, [(sv["o_a"].reshape(n, 512), LANE, 0), (u, LANE, U_Z // LANE)],
                                     [(wl["dn_norm_w"], (1, LANE), lambda j: (0, 0))], [d_ys[0]], 4, "dn_post_bwd")
    gr["dn_norm_w"] = jnp.sum(dnw, axis=0)[0]
    d_q, d_k, d_v, d_gb = chunk_scan_bwd(dn_chunk, [sv["q_a"], sv["k_a"], sv["v_a"], sv["gb"]], sv["ck_a"],
                                         d_oa.reshape(bsz, seq, 512), "dn_scan_bwd")
    (d_ab,), (dgp,) = rowmap_bwd(f_dn_gates, [(u, LANE, U_AB // LANE)], [(wl["dn_gate_p"], (8, LANE), lambda j: (0, 0))],
                                 [d_gb.reshape(n, LANE)], 1, "dn_gates_bwd")
    gr["dn_a_log"], gr["dn_dt_bias"] = dgp[0, 0, :DN_HEADS], dgp[0, 1, :DN_HEADS]
    d_qkv, d_cw = [], []
    for kind, d_t in enumerate((d_q, d_k, d_v)):
        cw = (wl["dn_conv_w"], (4, LANE), functools.partial(lambda j, kind: (0, 4 * kind + j), kind=kind))
        (d_in,), (dcw,) = seqmap_bwd(functools.partial(f_dn_pre, kind), [(u3, U_QKV // LANE + 4 * kind)], [cw], [d_t], 4,
                                     "dn_pre%d_bwd" % kind)
        d_qkv.append(d_in.reshape(n, 512))
        d_cw.append(_cols(dcw))
    gr["dn_conv_w"] = jnp.concatenate(d_cw, axis=1)

    pad = jnp.zeros((n, U_PAD - U_AB - LANE), f32)
    du = jnp.concatenate(list(d_gl) + d_qkv + [d_rv.reshape(n, 512), d_rg, d_z, d_cx.reshape(n, 512), d_cg, d_rq, d_rk, d_ab, pad],
                         axis=1)
    d_h = mm(du, wl["w_in"], "nt", "mm_in_dx")
    gr["w_in"] = mm(sv["h"], du, "tn", "mm_in_dw")
    (dx_in,), (dg,) = rowmap_bwd(f_norm, [(x_in, D_MODEL, 0)], [row1(wl["attn_norm"])], [d_h], 1, "norm_bwd", add=dx_mid)
    gr["attn_norm"] = dg[0, 0]
    return dx_in, gr


def _cols(dp):
    ncol, p, _ = dp.shape
    return jnp.transpose(dp, (1, 0, 2)).reshape(p, ncol * LANE)


def _pad_w_in(w):
    out = jnp.zeros((w.shape[0], U_PAD), w.dtype)
    for src, width, dst in _IN_SEGS:
        out = lax.dynamic_update_slice_in_dim(out, lax.slice_in_dim(w, src, src + width, axis=1), dst, axis=1)
    return out


def _unpad_w_in(wp):
    return jnp.concatenate([lax.slice_in_dim(wp, dst, dst + width, axis=1) for _, width, dst in _IN_SEGS], axis=1)


def _rope_tables(positions):
    half = RET_DK // 2
    inv = ROPE_BASE ** (-jnp.arange(half, dtype=f32) / half)
    ang = positions.astype(f32).reshape(-1, 1) * inv
    cos, sin = jnp.cos(ang), jnp.sin(ang)
    return jnp.tile(cos, (1, 2 * RET_HEADS)), jnp.tile(sin, (1, 2 * RET_HEADS))


def _layer_weights(full, layer):
    wl = {}
    wl["w_in"] = _pad_w_in(full["w_in"][layer])
    wl["w_branch"] = full["w_branch"][layer]
    wl["w_out"] = full["w_out"][layer]
    wl["w_up"] = full["w_up"][layer]
    wl["w_down"] = full["w_down"][layer]
    for k in ("attn_norm", "ffn_norm", "dn_norm_w", "lru_conv_b", "lru_lambda", "ffn_conv_b"):
        wl[k] = full[k][layer].reshape(1, -1)
    for k in ("dn_conv_w", "lru_conv_w", "ffn_conv_w", "lru_wa", "lru_wx"):
        wl[k] = full[k][layer]
    wl["lru_ba"] = full["lru_ba"][layer].reshape(1, -1)
    wl["lru_bx"] = full["lru_bx"][layer].reshape(1, -1)
    gp = jnp.zeros((8, LANE), f32)
    gp = gp.at[0, :DN_HEADS].set(full["dn_a_log"][layer]).at[1, :DN_HEADS].set(full["dn_dt_bias"][layer])
    wl["dn_gate_p"] = gp
    return wl


def local_step(x, positions, target, full):
    bsz, seq, d = x.shape
    n = bsz * seq
    cos, sin = _rope_tables(positions)
    wls = [_layer_weights(full, layer) for layer in range(DEPTH)]
    xs = x.reshape(n, d)
    saved = []
    for layer in range(DEPTH):
        xs, sv = _layer_fwd(xs, wls[layer], cos, sin, bsz, seq)
        saved.append(sv)
    loss, dx, d_final = final_loss(xs, full["final_norm"].reshape(1, d), target.reshape(n, d))
    grads = []
    for layer in reversed(range(DEPTH)):
        dx, gr = _layer_bwd(dx, saved[layer], wls[layer], cos, sin, bsz, seq)
        gr["w_in"] = _unpad_w_in(gr["w_in"])
        gr["lru_conv_b"], gr["lru_lambda"], gr["ffn_conv_b"] = gr["lru_conv_b"][0], gr["lru_lambda"][0], gr["ffn_conv_b"][0]
        grads.append(gr)
    grads = grads[::-1]
    stacked = {k: jnp.stack([g[k] for g in grads]) for k in grads[0]}
    stacked["final_norm"] = d_final[0]
    return loss[0, 0], dx.reshape(bsz, seq, d), stacked


BIG = (("w_in", 2), ("w_branch", 3), ("w_out", 1), ("w_up", 2), ("w_down", 1))
SMALL_SHARDED = (("dn_conv_w", 2), ("lru_conv_w", 2), ("ffn_conv_w", 2))
REPLICATED = ("attn_norm", "dn_a_log", "dn_dt_bias", "dn_norm_w", "lru_conv_b", "lru_wa", "lru_ba", "lru_wx", "lru_bx",
              "lru_lambda", "ffn_norm", "ffn_conv_b", "final_norm")
WEIGHTS = ("attn_norm", "w_in", "dn_conv_w", "dn_a_log", "dn_dt_bias", "dn_norm_w", "lru_conv_w", "lru_conv_b", "lru_wa",
           "lru_ba", "lru_wx", "lru_bx", "lru_lambda", "w_branch", "w_out", "ffn_norm", "w_up", "ffn_conv_w", "ffn_conv_b",
           "w_down", "final_norm")


def _pack(arrs, dtype, align=16 * LANE):
    flat = jnp.concatenate([a.reshape(-1).astype(dtype) for a in arrs])
    pad = (-flat.shape[0]) % align
    return jnp.pad(flat, (0, pad)).reshape(-1, LANE)


def _unpack(rows, shapes):
    flat = rows.reshape(-1)
    out, pos = [], 0
    for shp in shapes:
        size = math.prod(shp)
        out.append(lax.slice_in_dim(flat, pos, pos + size).reshape(shp))
        pos += size
    return out


def _gather_weights(w, names_axes, dtype, name):
    shards = [w[k] for k, _ in names_axes]
    got = exchange(_pack(shards, dtype), False, name)
    per_dev = [_unpack(got[p], [s.shape for s in shards]) for p in range(N_DEV)]
    return {k: jnp.concatenate([per_dev[p][i] for p in range(N_DEV)], axis=ax) for i, (k, ax) in enumerate(names_axes)}


def _split8(a, axis):
    size = a.shape[axis] // N_DEV
    return [lax.slice_in_dim(a, p * size, (p + 1) * size, axis=axis) for p in range(N_DEV)]


def kernel(x, positions, attn_norm, w_in, dn_conv_w, dn_a_log, dn_dt_bias, dn_norm_w, lru_conv_w, lru_conv_b, lru_wa, lru_ba, lru_wx, lru_bx, lru_lambda, w_branch, w_out, ffn_norm, w_up, ffn_conv_w, ffn_conv_b, w_down, final_norm, loss_target, m_attn_norm, m_w_in, m_dn_conv_w, m_dn_a_log, m_dn_dt_bias, m_dn_norm_w, m_lru_conv_w, m_lru_conv_b, m_lru_wa, m_lru_ba, m_lru_wx, m_lru_bx, m_lru_lambda, m_w_branch, m_w_out, m_ffn_norm, m_w_up, m_ffn_conv_w, m_ffn_conv_b, m_w_down, m_final_norm, v_attn_norm, v_w_in, v_dn_conv_w, v_dn_a_log, v_dn_dt_bias, v_dn_norm_w, v_lru_conv_w, v_lru_conv_b, v_lru_wa, v_lru_ba, v_lru_wx, v_lru_bx, v_lru_lambda, v_w_branch, v_w_out, v_ffn_norm, v_w_up, v_ffn_conv_w, v_ffn_conv_b, v_w_down, v_final_norm):
    w = dict(attn_norm=attn_norm, w_in=w_in, dn_conv_w=dn_conv_w, dn_a_log=dn_a_log, dn_dt_bias=dn_dt_bias, dn_norm_w=dn_norm_w,
             lru_conv_w=lru_conv_w, lru_conv_b=lru_conv_b, lru_wa=lru_wa, lru_ba=lru_ba, lru_wx=lru_wx, lru_bx=lru_bx,
             lru_lambda=lru_lambda, w_branch=w_branch, w_out=w_out, ffn_norm=ffn_norm, w_up=w_up, ffn_conv_w=ffn_conv_w,
             ffn_conv_b=ffn_conv_b, w_down=w_down, final_norm=final_norm)
    m = dict(attn_norm=m_attn_norm, w_in=m_w_in, dn_conv_w=m_dn_conv_w, dn_a_log=m_dn_a_log, dn_dt_bias=m_dn_dt_bias,
             dn_norm_w=m_dn_norm_w, lru_conv_w=m_lru_conv_w, lru_conv_b=m_lru_conv_b, lru_wa=m_lru_wa, lru_ba=m_lru_ba,
             lru_wx=m_lru_wx, lru_bx=m_lru_bx, lru_lambda=m_lru_lambda, w_branch=m_w_branch, w_out=m_w_out, ffn_norm=m_ffn_norm,
             w_up=m_w_up, ffn_conv_w=m_ffn_conv_w, ffn_conv_b=m_ffn_conv_b, w_down=m_w_down, final_norm=m_final_norm)
    v = dict(attn_norm=v_attn_norm, w_in=v_w_in, dn_conv_w=v_dn_conv_w, dn_a_log=v_dn_a_log, dn_dt_bias=v_dn_dt_bias,
             dn_norm_w=v_dn_norm_w, lru_conv_w=v_lru_conv_w, lru_conv_b=v_lru_conv_b, lru_wa=v_lru_wa, lru_ba=v_lru_ba,
             lru_wx=v_lru_wx, lru_bx=v_lru_bx, lru_lambda=v_lru_lambda, w_branch=v_w_branch, w_out=v_w_out, ffn_norm=v_ffn_norm,
             w_up=v_w_up, ffn_conv_w=v_ffn_conv_w, ffn_conv_b=v_ffn_conv_b, w_down=v_w_down, final_norm=v_final_norm)

    full = {k: w[k] for k in REPLICATED}
    full.update(_gather_weights(w, BIG, bf16, "gather_big"))
    full.update(_gather_weights(w, SMALL_SHARDED, f32, "gather_small"))

    loss_part, grad_x, g_full = local_step(x, positions, loss_target, full)

    blocks = [_pack([_split8(g_full[k], ax)[p] for k, ax in BIG], bf16) for p in range(N_DEV)]
    got = exchange(jnp.stack(blocks), True, "scatter_big")
    g_big = _unpack(sum_slots(got, "sum_big"), [w[k].shape for k, _ in BIG])
    grads = {k: g for (k, _), g in zip(BIG, g_big)}

    small_names = [k for k, _ in SMALL_SHARDED] + list(REPLICATED)
    small = _pack([loss_part.reshape(1)] + [g_full[k] for k in small_names], f32)
    tot = _unpack(sum_slots(exchange(small, False, "gather_grads"), "sum_small"),
                  [(1,)] + [g_full[k].shape for k in small_names])
    loss = tot[0][0]
    me = 4 * lax.axis_index("x") + 2 * lax.axis_index("y") + lax.axis_index("c")
    for k, g in zip(small_names, tot[1:]):
        ax = dict(SMALL_SHARDED).get(k)
        if ax is None:
            grads[k] = g
        else:
            size = g.shape[ax] // N_DEV
            grads[k] = lax.dynamic_slice_in_dim(g, me * size, size, axis=ax)

    upd = {k: adamw(w[k], grads[k], m[k], v[k], "adamw_" + k) for k in WEIGHTS}
    return (loss, grad_x, *[grads[k] for k in WEIGHTS], *[upd[k][0] for k in WEIGHTS], *[upd[k][1] for k in WEIGHTS],
            *[upd[k][2] for k in WEIGHTS])
```

```python
import functools
import math

import jax
import jax.numpy as jnp
from jax import lax
from jax.experimental import pallas as pl
from jax.experimental.pallas import tpu as pltpu

f32 = jnp.float32
bf16 = jnp.bfloat16

D_MODEL = 1024
DEPTH = 4
CHUNK = 64
EPS = 1e-6
DN_HEADS, DN_DK = 4, 128
RET_HEADS, RET_DK, RET_DV = 4, 64, 128
ROPE_BASE = 10000.0
LRU_C = 8.0
D_FF = 2816
N_DEV = 8
LANE = 128
VMEM_LIMIT = 56 * 1024 * 1024

ADAM_LR, ADAM_B1, ADAM_B2, ADAM_EPS, ADAM_WD, ADAM_STEP = 0.001, 0.9, 0.999, 1e-8, 0.01, 10

U_GATES, U_QKV, U_RV, U_RG, U_Z, U_CX, U_CG, U_RQ, U_RK, U_AB = (
    0, 3072, 4608, 5120, 5632, 6144, 6656, 7168, 7424, 7680)
U_PAD = 8192
_IN_SEGS = ((0, 1536, U_QKV), (1536, 8, U_AB), (1544, 512, U_Z), (2056, 256, U_RQ), (2312, 256, U_RK),
            (2568, 512, U_RV), (3080, 512, U_RG), (3592, 512, U_CX), (4104, 512, U_CG), (4616, 3072, U_GATES))
N_IN = 7688


def _params():
    return pltpu.CompilerParams(vmem_limit_bytes=VMEM_LIMIT)


def _pick(dim, pref):
    best = None
    for d in range(LANE, min(dim, pref) + 1, LANE):
        if dim % d == 0:
            best = d
    return best if best is not None else dim


@functools.partial(jax.custom_vjp, nondiff_argnums=(1, 2))
def sroll(x, shift, axis):
    return pltpu.roll(x, shift, axis)


def _sroll_fwd(x, shift, axis):
    return pltpu.roll(x, shift, axis), None


def _sroll_bwd(shift, axis, _, g):
    n = g.shape[axis]
    return (pltpu.roll(g, (n - shift) % n, axis),)


sroll.defvjp(_sroll_fwd, _sroll_bwd)

_DIMS = {"nn": (((1,), (0,)), ((), ())), "nt": (((1,), (1,)), ((), ())), "tn": (((0,), (0,)), ((), ()))}


def _dg(a, b, dims):
    return lax.dot_general(a.astype(bf16), b.astype(bf16), _DIMS[dims], preferred_element_type=f32)


@functools.partial(jax.custom_vjp, nondiff_argnums=(2,))
def bdot(a, b, dims):
    return _dg(a, b, dims)


def _bdot_fwd(a, b, dims):
    return _dg(a, b, dims), (a.astype(bf16), b.astype(bf16))


def _bdot_bwd(dims, res, g):
    a, b = res
    if dims == "nn":
        return _dg(g, b, "nt"), _dg(a, g, "tn")
    if dims == "nt":
        return _dg(g, b, "nn"), _dg(g, a, "tn")
    return _dg(b, g, "nt"), _dg(a, g, "nn")


bdot.defvjp(_bdot_fwd, _bdot_bwd)


def _fdot(a, b, dims):
    return lax.dot_general(a, b, _DIMS[dims], precision=lax.Precision.HIGH, preferred_element_type=f32)


@jax.custom_vjp
def unit_lower_inv_all(mats):
    shape = mats[0].shape
    row = lax.broadcasted_iota(jnp.int32, shape, 0)
    col = lax.broadcasted_iota(jnp.int32, shape, 1)
    eye = jnp.where(row == col, 1.0, 0.0).astype(f32)
    n = [-a for a in mats]
    p = [eye + x for x in n]
    span = 2
    while span < shape[0]:
        n = [_fdot(x, x, "nn") for x in n]
        p = [y + _fdot(y, x, "nn") for y, x in zip(p, n)]
        span *= 2
    return p


def _uli_fwd(mats):
    x = unit_lower_inv_all(mats)
    return x, x


def _uli_bwd(xs, gs):
    t = [_fdot(x, g, "tn") for x, g in zip(xs, gs)]
    return ([-_fdot(y, x, "nt") for y, x in zip(t, xs)],)


unit_lower_inv_all.defvjp(_uli_fwd, _uli_bwd)


def cumsum_rows(x):
    rows = x.shape[0]
    row = lax.broadcasted_iota(jnp.int32, x.shape, 0)
    s = 1
    while s < rows:
        x = x + jnp.where(row >= s, sroll(x, s, 0), 0.0)
        s *= 2
    return x


def _expm1(x):
    return jnp.tanh(0.5 * x) * (jnp.exp(x) + 1.0)


def _lane_pick(x, lane):
    idx = lax.broadcasted_iota(jnp.int32, x.shape, 1)
    return jnp.sum(jnp.where(idx == lane, x, 0.0), axis=1, keepdims=True)


def _row_pick(x, r):
    idx = lax.broadcasted_iota(jnp.int32, x.shape, 0)
    return jnp.sum(jnp.where(idx == r, x, 0.0), axis=0, keepdims=True)


def _causal_conv(x, halo, w, width):
    xe = jnp.concatenate([halo, x], axis=0)
    acc = xe * w[width - 1:width]
    for k in range(width - 1):
        acc = acc + sroll(xe, width - 1 - k, 0) * w[k:k + 1]
    return acc[8:]


def f_norm(ins, ps):
    (x,), (g,) = ins, ps
    return [x * lax.rsqrt(jnp.mean(x * x, axis=-1, keepdims=True) + EPS) * g]


def f_dn_pre(kind, mains, halos, ps):
    y = _causal_conv(mains[0], halos[0], ps[0], 4)
    y = y * jax.nn.sigmoid(y)
    if kind < 2:
        y = y * lax.rsqrt(jnp.sum(y * y, axis=-1, keepdims=True) + EPS)
    if kind == 0:
        y = y * (DN_DK ** -0.5)
    return [y]


def f_dn_gates(ins, ps):
    (u,), (p,) = ins, ps
    lane = lax.broadcasted_iota(jnp.int32, u.shape, 1)
    g = -jnp.exp(p[0:1]) * jax.nn.softplus(u + p[1:2])
    beta = jax.nn.sigmoid(u)
    return [jnp.where(lane < 4, g, jnp.where(lane < 8, beta, 0.0))]


def f_dn_post(ins, ps):
    (o, z), (nw,) = ins, ps
    y = o * lax.rsqrt(jnp.mean(o * o, axis=-1, keepdims=True) + EPS) * nw
    return [y * (z * jax.nn.sigmoid(z))]


def _rot_half(t):
    lane = lax.broadcasted_iota(jnp.int32, t.shape, 1)
    width = t.shape[1]
    first = (lane % RET_DK) < (RET_DK // 2)
    return jnp.where(first, -sroll(t, width - RET_DK // 2, 1), sroll(t, RET_DK // 2, 1))


def f_ret_pre(ins, ps):
    q, k, cos, sin = ins
    qr = q * cos + _rot_half(q) * sin
    kr = (k * cos + _rot_half(k) * sin) * (RET_DK ** -0.5)
    return [qr, kr]


def f_ret_post(ins, ps):
    o, g = ins
    mu = jnp.mean(o, axis=-1, keepdims=True)
    var = jnp.mean(jnp.square(o - mu), axis=-1, keepdims=True)
    return [(o - mu) * lax.rsqrt(var + EPS) * (g * jax.nn.sigmoid(g))]


def f_lru_pre(mains, halos, ps):
    cw, cb, wa, ba, wx, bx, lam = ps
    xc = _causal_conv(mains[0], halos[0], cw, 4) + cb
    r = jax.nn.sigmoid(bdot(xc, wa, "nn") + ba)
    i = jax.nn.sigmoid(bdot(xc, wx, "nn") + bx)
    log_a = -LRU_C * r * jax.nn.softplus(-lam)
    a = jnp.exp(log_a)
    b = jnp.sqrt(-_expm1(2.0 * log_a)) * (i * xc)
    return [a, b]


def f_lru_post(ins, ps):
    h, g = ins
    return [h * jax.nn.gelu(g)]


def f_merge(ins, ps):
    g0, g1, g2, b0, b1, b2 = ins
    return [jax.nn.sigmoid(g0) * b0 + jax.nn.sigmoid(g1) * b1 + jax.nn.sigmoid(g2) * b2]


def f_ffn_mid(mains, halos, ps):
    cwg, cwv, cbg, cbv = ps
    gate = _causal_conv(mains[0], halos[0], cwg, 3) + cbg
    val = _causal_conv(mains[1], halos[1], cwv, 3) + cbv
    return [gate * jax.nn.sigmoid(gate) * val]


def mm(a, b, dims, name, add=None, tm=512, tn=1024, tk=1024):
    if dims == "tn":
        kdim, m = a.shape
        n = b.shape[1]
    else:
        m, kdim = a.shape
        n = b.shape[0] if dims == "nt" else b.shape[1]
    tm, tn, tk = _pick(m, tm), _pick(n, tn), _pick(kdim, tk)
    nk = kdim // tk
    a_spec = pl.BlockSpec((tk, tm), lambda i, j, k: (k, i)) if dims == "tn" else pl.BlockSpec((tm, tk), lambda i, j, k: (i, k))
    b_spec = pl.BlockSpec((tn, tk), lambda i, j, k: (j, k)) if dims == "nt" else pl.BlockSpec((tk, tn), lambda i, j, k: (k, j))
    o_spec = pl.BlockSpec((tm, tn), lambda i, j, k: (i, j))
    has_add = add is not None

    def body(*refs):
        if has_add:
            a_ref, b_ref, add_ref, o_ref, acc_ref = refs
        else:
            a_ref, b_ref, o_ref, acc_ref = refs
        k = pl.program_id(2)

        @pl.when(k == 0)
        def _():
            acc_ref[...] = jnp.zeros_like(acc_ref)

        acc_ref[...] += _dg(a_ref[...], b_ref[...], dims)

        @pl.when(k == nk - 1)
        def _():
            o_ref[...] = acc_ref[...] + add_ref[...] if has_add else acc_ref[...]

    args = (a, b, add) if has_add else (a, b)
    in_specs = [a_spec, b_spec] + ([o_spec] if has_add else [])
    return pl.pallas_call(
        body, name=name, grid=(m // tm, n // tn, nk), in_specs=in_specs, out_specs=o_spec,
        out_shape=jax.ShapeDtypeStruct((m, n), f32), scratch_shapes=[pltpu.VMEM((tm, tn), f32)],
        compiler_params=_params())(*args)


def rowmap(fn, ins, params, outs, ncol, name, rows=256):
    n = ins[0][0].shape[0]
    r = min(rows, n)
    nin, npar = len(ins), len(params)

    def body(*refs):
        vals = [x[...] for x in refs[:nin]]
        pv = [p[...] for p in refs[nin:nin + npar]]
        for o_ref, o in zip(refs[nin + npar:], fn(vals, pv)):
            o_ref[...] = o.astype(o_ref.dtype)

    in_specs = [pl.BlockSpec((r, cb), functools.partial(lambda j, i, off: (i, off + j), off=off)) for _, cb, off in ins]
    in_specs += [pl.BlockSpec(bs, functools.partial(lambda j, i, f: f(j), f=f)) for _, bs, f in params]
    out_specs = [pl.BlockSpec((r, cb), lambda j, i: (i, j)) for cb, _ in outs]
    out_shape = [jax.ShapeDtypeStruct((n, cb * ncol), dt) for cb, dt in outs]
    res = pl.pallas_call(body, name=name, grid=(ncol, n // r), in_specs=in_specs, out_specs=out_specs,
                         out_shape=out_shape, compiler_params=_params())(*[a for a, _, _ in ins], *[a for a, _, _ in params])
    return res


def rowmap_bwd(fn, ins, params, douts, ncol, name, rows=256, add=None):
    n = ins[0][0].shape[0]
    r = min(rows, n)
    nin, npar, nout = len(ins), len(params), len(douts)
    add = [None] * nin if add is None else list(add)
    add_idx = [i for i in range(nin) if add[i] is not None]

    def body(*refs):
        vals = [x[...] for x in refs[:nin]]
        pv = [p[...] for p in refs[nin:nin + npar]]
        dys = [d[...] for d in refs[nin + npar:nin + npar + nout]]
        k0 = nin + npar + nout
        add_refs = dict(zip(add_idx, refs[k0:k0 + len(add_idx)]))
        k0 += len(add_idx)
        din_refs = refs[k0:k0 + nin]
        dp_refs = refs[k0 + nin:]
        _, vjp = jax.vjp(fn, vals, pv)
        dvals, dpv = vjp(dys)
        for idx, (d_ref, d) in enumerate(zip(din_refs, dvals)):
            d_ref[...] = d + add_refs[idx][...] if idx in add_refs else d

        @pl.when(pl.program_id(1) == 0)
        def _():
            for d_ref in dp_refs:
                d_ref[...] = jnp.zeros_like(d_ref)

        for d_ref, d in zip(dp_refs, dpv):
            d_ref[...] += d

    in_specs = [pl.BlockSpec((r, cb), functools.partial(lambda j, i, off: (i, off + j), off=off)) for _, cb, off in ins]
    in_specs += [pl.BlockSpec(bs, functools.partial(lambda j, i, f: f(j), f=f)) for _, bs, f in params]
    in_specs += [pl.BlockSpec((r, d.shape[1] // ncol), lambda j, i: (i, j)) for d in douts]
    in_specs += [pl.BlockSpec((r, ins[i][1]), lambda j, i: (i, j)) for i in add_idx]
    out_specs = [pl.BlockSpec((r, cb), lambda j, i: (i, j)) for _, cb, _ in ins]
    pshapes = [tuple(d for d in bs if d is not None) for _, bs, _ in params]
    out_specs += [pl.BlockSpec((None,) + ps, functools.partial(lambda j, i, nd: (j,) + (0,) * nd, nd=len(ps))) for ps in pshapes]
    out_shape = [jax.ShapeDtypeStruct((n, cb * ncol), f32) for _, cb, _ in ins]
    out_shape += [jax.ShapeDtypeStruct((ncol,) + ps, f32) for ps in pshapes]
    args = [a for a, _, _ in ins] + [a for a, _, _ in params] + list(douts) + [add[i] for i in add_idx]
    res = pl.pallas_call(body, name=name, grid=(ncol, n // r), in_specs=in_specs, out_specs=out_specs,
                         out_shape=out_shape, compiler_params=_params())(*args)
    return res[:nin], res[nin:]


SEQ_ROWS = 256


def seqmap(fn, ins, params, nouts, ncol, name):
    bsz, seq, _ = ins[0][0].shape
    r = min(SEQ_ROWS, seq)
    nin, npar = len(ins), len(params)

    def body(*refs):
        in_refs = refs[:nin]
        pv = [p[...] for p in refs[nin:nin + npar]]
        out_refs = refs[nin + npar:]

        def step(i, carry):
            r0 = pl.multiple_of(i * r, r)
            h0 = pl.multiple_of(jnp.maximum(r0 - 8, 0), 8)
            mains = [x[pl.ds(r0, r), :] for x in in_refs]
            halos = [jnp.where(i == 0, 0.0, x[pl.ds(h0, 8), :]) for x in in_refs]
            for o_ref, o in zip(out_refs, fn(mains, halos, pv)):
                o_ref[pl.ds(r0, r), :] = o
            return carry

        lax.fori_loop(0, seq // r, step, 0)

    in_specs = [pl.BlockSpec((None, seq, LANE), functools.partial(lambda j, b, off: (b, 0, off + j), off=off)) for _, off in ins]
    in_specs += [pl.BlockSpec(bs, functools.partial(lambda j, b, f: f(j), f=f)) for _, bs, f in params]
    out_specs = [pl.BlockSpec((None, seq, LANE), lambda j, b: (b, 0, j)) for _ in range(nouts)]
    out_shape = [jax.ShapeDtypeStruct((bsz, seq, LANE * ncol), f32) for _ in range(nouts)]
    return pl.pallas_call(body, name=name, grid=(ncol, bsz), in_specs=in_specs, out_specs=out_specs,
                          out_shape=out_shape, compiler_params=_params())(*[a for a, _ in ins], *[a for a, _, _ in params])


def seqmap_bwd(fn, ins, params, douts, ncol, name):
    bsz, seq, _ = ins[0][0].shape
    r = min(SEQ_ROWS, seq)
    nin, npar, nout = len(ins), len(params), len(douts)

    def body(*refs):
        in_refs = refs[:nin]
        pv = [p[...] for p in refs[nin:nin + npar]]
        dy_refs = refs[nin + npar:nin + npar + nout]
        din_refs = refs[nin + npar + nout:nin + npar + nout + nin]
        dp_refs = refs[nin + npar + nout + nin:]

        def step(i, dp_acc):
            r0 = pl.multiple_of(i * r, r)
            h0 = pl.multiple_of(jnp.maximum(r0 - 8, 0), 8)
            mains = [x[pl.ds(r0, r), :] for x in in_refs]
            halos_raw = [x[pl.ds(h0, 8), :] for x in in_refs]

            def tile(mains, halos_raw, pv):
                return fn(mains, [jnp.where(i == 0, 0.0, h) for h in halos_raw], pv)

            _, vjp = jax.vjp(tile, mains, halos_raw, pv)
            dm, dh, dp = vjp([d[pl.ds(r0, r), :] for d in dy_refs])
            for d_ref, m, h in zip(din_refs, dm, dh):
                d_ref[pl.ds(r0, r), :] = m
                d_ref[pl.ds(h0, 8), :] += h
            return [acc + d for acc, d in zip(dp_acc, dp)]

        dp = lax.fori_loop(0, seq // r, step, [jnp.zeros(p.shape, f32) for p in pv])

        @pl.when(pl.program_id(1) == 0)
        def _():
            for d_ref in dp_refs:
                d_ref[...] = jnp.zeros_like(d_ref)

        for d_ref, d in zip(dp_refs, dp):
            d_ref[...] += d

    in_specs = [pl.BlockSpec((None, seq, LANE), functools.partial(lambda j, b, off: (b, 0, off + j), off=off)) for _, off in ins]
    in_specs += [pl.BlockSpec(bs, functools.partial(lambda j, b, f: f(j), f=f)) for _, bs, f in params]
    in_specs += [pl.BlockSpec((None, seq, LANE), lambda j, b: (b, 0, j)) for _ in range(nout)]
    out_specs = [pl.BlockSpec((None, seq, LANE), lambda j, b: (b, 0, j)) for _ in range(nin)]
    pshapes = [tuple(d for d in bs if d is not None) for _, bs, _ in params]
    out_specs += [pl.BlockSpec((None,) + ps, functools.partial(lambda j, b, nd: (j,) + (0,) * nd, nd=len(ps))) for ps in pshapes]
    out_shape = [jax.ShapeDtypeStruct((bsz, seq, LANE * ncol), f32) for _ in range(nin)]
    out_shape += [jax.ShapeDtypeStruct((ncol,) + ps, f32) for ps in pshapes]
    args = [a for a, _ in ins] + [a for a, _, _ in params] + list(douts)
    res = pl.pallas_call(body, name=name, grid=(ncol, bsz), in_specs=in_specs, out_specs=out_specs,
                         out_shape=out_shape, compiler_params=_params())(*args)
    return res[:nin], res[nin:]


def _tri_masks():
    row = lax.broadcasted_iota(jnp.int32, (CHUNK, CHUNK), 0)
    col = lax.broadcasted_iota(jnp.int32, (CHUNK, CHUNK), 1)
    return row >= col, row > col


CHUNKS_PER_STEP = 4


def _by_rows(parts, per_row):
    rows = [jnp.concatenate(parts[i:i + per_row], axis=1) for i in range(0, len(parts), per_row)]
    return jnp.concatenate(rows, axis=0)


def dn_prep(vals, ps):
    q, k, v, gb = vals
    nchunk = q.shape[0] // CHUNK
    causal, strict = _tri_masks()
    gbs = [gb[c * CHUNK:(c + 1) * CHUNK] for c in range(nchunk)]
    gcs = [cumsum_rows(g) for g in gbs]
    gcts = [g.T for g in gcs]
    chains = [(c, h) for c in range(nchunk) for h in range(DN_HEADS)]
    part = lambda t, c, h: t[c * CHUNK:(c + 1) * CHUNK, h * DN_DK:(h + 1) * DN_DK]
    qh = [part(q, c, h) for c, h in chains]
    kh = [part(k, c, h) for c, h in chains]
    vh = [part(v, c, h) for c, h in chains]
    g_col = [_lane_pick(gcs[c], h) for c, h in chains]
    beta = [_lane_pick(gbs[c], DN_HEADS + h) for c, h in chains]
    g_row = [_row_pick(gcts[c], h)[:, :CHUNK] for c, h in chains]
    decay = [jnp.where(causal, jnp.exp(jnp.where(causal, gc - gr, 0.0)), 0.0) for gc, gr in zip(g_col, g_row)]
    k_beta = [a * b for a, b in zip(kh, beta)]
    eg = [jnp.exp(g) for g in g_col]
    kk = [bdot(a, b, "nt") for a, b in zip(k_beta, kh)]
    qk = [bdot(a, b, "nt") for a, b in zip(qh, kh)]
    t_inv = unit_lower_inv_all([jnp.where(strict, a * d, 0.0) for a, d in zip(kk, decay)])
    u = [bdot(t, a * b, "nn") for t, a, b in zip(t_inv, vh, beta)]
    w = [bdot(t, a * e, "nn") for t, a, e in zip(t_inv, k_beta, eg)]
    attn = [jnp.concatenate([a * d, jnp.zeros((CHUNK, DN_DK - CHUNK), f32)], axis=1) for a, d in zip(qk, decay)]
    qd = [a * e for a, e in zip(qh, eg)]
    kd = [a * jnp.exp(_row_pick(g, CHUNK - 1) - g) for a, g in zip(kh, g_col)]
    g_last = jnp.concatenate([jnp.broadcast_to(_row_pick(g, CHUNK - 1), g.shape) for g in gcs], axis=0)
    return [_by_rows(t, DN_HEADS) for t in (u, w, attn, qd, kd)] + [g_last]


def dn_step(state, u, w, attn, qd, kd, g_last):
    bsz = u.shape[0]
    chains = [(b, h) for b in range(bsz) for h in range(DN_HEADS)]
    part = lambda t, b, h: t[b, :, h * DN_DK:(h + 1) * DN_DK]
    ws = [bdot(part(w, b, h), s, "nn") for (b, h), s in zip(chains, state)]
    qs = [bdot(part(qd, b, h), s, "nn") for (b, h), s in zip(chains, state)]
    v_new = [part(u, b, h) - x for (b, h), x in zip(chains, ws)]
    av = [bdot(attn[b, :, h * DN_DK:h * DN_DK + CHUNK], x, "nn") for (b, h), x in zip(chains, v_new)]
    kv = [bdot(part(kd, b, h), x, "tn") for (b, h), x in zip(chains, v_new)]
    ge = [jnp.exp(_row_pick(_lane_pick(g_last[b], h), 0)) for b, h in chains]
    new_state = [s * g + x for s, g, x in zip(state, ge, kv)]
    outs = [a + b for a, b in zip(qs, av)]
    return new_state, jnp.concatenate([jnp.concatenate(outs[b * DN_HEADS:(b + 1) * DN_HEADS], axis=1)[None]
                                       for b in range(bsz)], axis=0)


def _ret_log_gamma(h):
    return math.log(1.0 - 2.0 ** (-5.0 - h))


def ret_prep(vals, ps):
    q, k, v = vals
    nchunk = q.shape[0] // CHUNK
    causal, _ = _tri_masks()
    row = lax.broadcasted_iota(jnp.int32, (CHUNK, CHUNK), 0)
    col = lax.broadcasted_iota(jnp.int32, (CHUNK, CHUNK), 1)
    dist = (row - col).astype(f32)
    lane = lax.broadcasted_iota(jnp.int32, (CHUNK, q.shape[1]), 1)
    dmask = [jnp.where(causal, jnp.exp(jnp.where(causal, dist, 0.0) * _ret_log_gamma(h)), 0.0) for h in range(RET_HEADS)]
    chains = [(c, h) for c in range(nchunk) for h in range(RET_HEADS)]
    rows = lambda t, c: t[c * CHUNK:(c + 1) * CHUNK]
    scores = [bdot(jnp.where((lane // RET_DK) == h, rows(q, c), 0.0), rows(k, c), "nt") * dmask[h] for c, h in chains]
    inner = [bdot(s, rows(v, c)[:, h * RET_DV:(h + 1) * RET_DV], "nn") for s, (c, h) in zip(scores, chains)]
    return [_by_rows(inner, RET_HEADS)]


def ret_step(state, q, k, v, inner):
    bsz = q.shape[0]
    idx = lax.broadcasted_iota(jnp.int32, (CHUNK, 1), 0).astype(f32)
    lane = lax.broadcasted_iota(jnp.int32, (CHUNK, q.shape[2]), 1)
    chains = [(b, h) for b in range(bsz) for h in range(RET_HEADS)]
    part = lambda t, b, h: t[b, :, h * RET_DV:(h + 1) * RET_DV]
    cross = [bdot(q[b], s, "nn") for (b, h), s in zip(chains, state)]
    kz = [jnp.where((lane // RET_DK) == h, k[b], 0.0) * jnp.exp((CHUNK - 1.0 - idx) * _ret_log_gamma(h)) for b, h in chains]
    kv = [bdot(a, part(v, b, h), "tn") for a, (b, h) in zip(kz, chains)]
    outs = [x * jnp.exp((idx + 1.0) * _ret_log_gamma(h)) + part(inner, b, h) for x, (b, h) in zip(cross, chains)]
    new_state = [s * math.exp(CHUNK * _ret_log_gamma(h)) + x for s, x, (b, h) in zip(state, kv, chains)]
    return new_state, jnp.concatenate([jnp.concatenate(outs[b * RET_HEADS:(b + 1) * RET_HEADS], axis=1)[None]
                                       for b in range(bsz)], axis=0)


def chunk_scan(step_fn, ins, state_shape, out_width, name):
    bsz, seq, _ = ins[0].shape
    nchunk = seq // CHUNK
    nin = len(ins)
    nh = state_shape[0]

    def body(*refs):
        in_refs = refs[:nin]
        o_ref, ck_ref, s_ref = refs[nin:]

        @pl.when(pl.program_id(0) == 0)
        def _():
            s_ref[...] = jnp.zeros_like(s_ref)

        state = [s_ref[i] for i in range(bsz * nh)]
        for i in range(bsz * nh):
            ck_ref[i // nh, i % nh] = state[i]
        new_state, out = step_fn(state, *[x[...] for x in in_refs])
        o_ref[...] = out
        for i in range(bsz * nh):
            s_ref[i] = new_state[i]

    in_specs = [pl.BlockSpec((bsz, CHUNK, x.shape[2]), lambda n: (0, n, 0)) for x in ins]
    out_specs = [pl.BlockSpec((bsz, CHUNK, out_width), lambda n: (0, n, 0)),
                 pl.BlockSpec((bsz, None) + tuple(state_shape), lambda n: (0, n, 0, 0, 0))]
    out_shape = [jax.ShapeDtypeStruct((bsz, seq, out_width), f32),
                 jax.ShapeDtypeStruct((bsz, nchunk) + tuple(state_shape), f32)]
    return pl.pallas_call(body, name=name, grid=(nchunk,), in_specs=in_specs, out_specs=out_specs, out_shape=out_shape,
                          scratch_shapes=[pltpu.VMEM((bsz * nh,) + tuple(state_shape[1:]), f32)],
                          compiler_params=_params())(*ins)


def chunk_scan_bwd(step_fn, ins, ckpt, dout, name):
    bsz, seq, _ = ins[0].shape
    nchunk = seq // CHUNK
    nin = len(ins)
    state_shape = ckpt.shape[2:]
    nh = state_shape[0]

    def body(*refs):
        in_refs = refs[:nin]
        ck_ref, do_ref = refs[nin:nin + 2]
        din_refs = refs[nin + 2:nin + 2 + nin]
        ds_ref = refs[-1]

        @pl.when(pl.program_id(0) == 0)
        def _():
            ds_ref[...] = jnp.zeros_like(ds_ref)

        state = [ck_ref[i // nh, i % nh] for i in range(bsz * nh)]
        _, vjp = jax.vjp(step_fn, state, *[x[...] for x in in_refs])
        grads = vjp(([ds_ref[i] for i in range(bsz * nh)], do_ref[...]))
        for i in range(bsz * nh):
            ds_ref[i] = grads[0][i]
        for d_ref, d in zip(din_refs, grads[1:]):
            d_ref[...] = d

    rev = lambda n: (0, nchunk - 1 - n, 0)
    in_specs = [pl.BlockSpec((bsz, CHUNK, x.shape[2]), rev) for x in ins]
    in_specs += [pl.BlockSpec((bsz, None) + tuple(state_shape), lambda n: (0, nchunk - 1 - n, 0, 0, 0)),
                 pl.BlockSpec((bsz, CHUNK, dout.shape[2]), rev)]
    out_specs = [pl.BlockSpec((bsz, CHUNK, x.shape[2]), rev) for x in ins]
    out_shape = [jax.ShapeDtypeStruct(x.shape, f32) for x in ins]
    return pl.pallas_call(body, name=name, grid=(nchunk,), in_specs=in_specs, out_specs=out_specs, out_shape=out_shape,
                          scratch_shapes=[pltpu.VMEM((bsz * nh,) + tuple(state_shape[1:]), f32)],
                          compiler_params=_params())(*ins, ckpt, dout)


LRU_ROWS = 512


def lru_scan(a, b):
    bsz, seq, width = a.shape
    rb = min(LRU_ROWS, seq)

    def body(a_ref, b_ref, h_ref, hp_ref, carry_ref):
        @pl.when(pl.program_id(1) == 0)
        def _():
            carry_ref[...] = jnp.zeros_like(carry_ref)

        row = lax.broadcasted_iota(jnp.int32, (8, width), 0)

        def tile(t, hprev):
            r0 = pl.multiple_of(t * 8, 8)
            ca, cbv = a_ref[pl.ds(r0, 8), :], b_ref[pl.ds(r0, 8), :]
            for s in (1, 2, 4):
                m = row >= s
                cbv = jnp.where(m, ca * pltpu.roll(cbv, s, 0) + cbv, cbv)
                ca = jnp.where(m, ca * pltpu.roll(ca, s, 0), ca)
            h = cbv + ca * hprev
            h_ref[pl.ds(r0, 8), :] = h
            hp_ref[pl.ds(r0, 8), :] = jnp.where(row == 0, hprev, pltpu.roll(h, 1, 0))
            return _row_pick(h, 7)

        carry_ref[0:1, :] = lax.fori_loop(0, rb // 8, tile, carry_ref[0:1, :])

    spec = pl.BlockSpec((None, rb, width), lambda bi, i: (bi, i, 0))
    return pl.pallas_call(body, name="lru_scan", grid=(bsz, seq // rb), in_specs=[spec, spec], out_specs=[spec, spec],
                          out_shape=[jax.ShapeDtypeStruct(a.shape, f32)] * 2,
                          scratch_shapes=[pltpu.VMEM((8, width), f32)], compiler_params=_params())(a, b)


def lru_scan_bwd(a, hp, dh):
    bsz, seq, width = a.shape
    rb = min(LRU_ROWS, seq)
    nblk = seq // rb

    def body(a_ref, hp_ref, dh_ref, da_ref, db_ref, carry_ref):
        @pl.when(pl.program_id(1) == 0)
        def _():
            carry_ref[...] = jnp.zeros_like(carry_ref)

        row = lax.broadcasted_iota(jnp.int32, (8, width), 0)
        ntile = rb // 8

        def tile(t, mu_next):
            r0 = pl.multiple_of((ntile - 1 - t) * 8, 8)
            ca, dh_t = a_ref[pl.ds(r0, 8), :], dh_ref[pl.ds(r0, 8), :]
            cbv = ca * dh_t
            for s in (1, 2, 4):
                m = row < 8 - s
                cbv = jnp.where(m, ca * pltpu.roll(cbv, 8 - s, 0) + cbv, cbv)
                ca = jnp.where(m, ca * pltpu.roll(ca, 8 - s, 0), ca)
            mu = cbv + ca * mu_next
            lam = dh_t + jnp.where(row == 7, mu_next, pltpu.roll(mu, 7, 0))
            db_ref[pl.ds(r0, 8), :] = lam
            da_ref[pl.ds(r0, 8), :] = lam * hp_ref[pl.ds(r0, 8), :]
            return _row_pick(mu, 0)

        carry_ref[0:1, :] = lax.fori_loop(0, ntile, tile, carry_ref[0:1, :])

    spec = pl.BlockSpec((None, rb, width), lambda bi, i: (bi, nblk - 1 - i, 0))
    return pl.pallas_call(body, name="lru_scan_bwd", grid=(bsz, nblk), in_specs=[spec] * 3, out_specs=[spec, spec],
                          out_shape=[jax.ShapeDtypeStruct(a.shape, f32)] * 2,
                          scratch_shapes=[pltpu.VMEM((8, width), f32)], compiler_params=_params())(a, hp, dh)


def final_loss(x, g, target):
    n, d = x.shape
    r = min(256, n)

    def body(x_ref, g_ref, t_ref, loss_ref, dx_ref, dg_ref):
        @pl.when(pl.program_id(0) == 0)
        def _():
            loss_ref[...] = jnp.zeros_like(loss_ref)
            dg_ref[...] = jnp.zeros_like(dg_ref)

        tgt = t_ref[...]

        def loss_fn(xv, gv):
            y = f_norm([xv], [gv])[0]
            return 0.5 * jnp.sum(jnp.mean(jnp.square(y - tgt), axis=-1, keepdims=True), axis=0, keepdims=True)

        val, vjp = jax.vjp(loss_fn, x_ref[...], g_ref[...])
        dx, dg = vjp(jnp.ones_like(val))
        loss_ref[...] += val
        dx_ref[...] = dx
        dg_ref[...] += dg

    row = pl.BlockSpec((r, d), lambda i: (i, 0))
    return pl.pallas_call(
        body, name="final_loss", grid=(n // r,), in_specs=[row, pl.BlockSpec((1, d), lambda i: (0, 0)), row],
        out_specs=[pl.BlockSpec((8, LANE), lambda i: (0, 0)), row, pl.BlockSpec((1, d), lambda i: (0, 0))],
        out_shape=[jax.ShapeDtypeStruct((8, LANE), f32), jax.ShapeDtypeStruct((n, d), f32), jax.ShapeDtypeStruct((1, d), f32)],
        compiler_params=_params())(x, g, target)


def exchange(x, scatter, name):
    blk = x.shape[1:] if scatter else x.shape

    def body(x_ref, o_ref, send_sems, recv_sems, local_sem):
        mx, my, mc = lax.axis_index("x"), lax.axis_index("y"), lax.axis_index("c")
        me = 4 * mx + 2 * my + mc
        src_own = x_ref.at[me] if scatter else x_ref
        local = pltpu.make_async_copy(src_own, o_ref.at[me], local_sem)
        local.start()
        copies = []
        for k in range(1, N_DEV):
            px, py, pc = (mx + (k >> 2)) % 2, (my + ((k >> 1) & 1)) % 2, (mc + (k & 1)) % 2
            src = x_ref.at[4 * px + 2 * py + pc] if scatter else x_ref
            cp = pltpu.make_async_remote_copy(
                src_ref=src, dst_ref=o_ref.at[me], send_sem=send_sems.at[k - 1], recv_sem=recv_sems.at[k - 1],
                device_id=(px, py, pc), device_id_type=pl.DeviceIdType.MESH)
            cp.start()
            copies.append(cp)
        for cp in copies:
            cp.wait()
        local.wait()

    return pl.pallas_call(
        body, name=name, in_specs=[pl.BlockSpec(memory_space=pl.ANY)], out_specs=pl.BlockSpec(memory_space=pl.ANY),
        out_shape=jax.ShapeDtypeStruct((N_DEV,) + tuple(blk), x.dtype),
        scratch_shapes=[pltpu.SemaphoreType.DMA((N_DEV - 1,)), pltpu.SemaphoreType.DMA((N_DEV - 1,)), pltpu.SemaphoreType.DMA],
        compiler_params=pltpu.CompilerParams(has_side_effects=True))(x)


_HBM = pl.BlockSpec(memory_space=pltpu.HBM)
_SEM = pl.BlockSpec(memory_space=pltpu.SEMAPHORE)
_EFFECT = pltpu.SideEffectType.DATAFLOW_SIDE_EFFECTING


def _peer(k):
    mx, my, mc = lax.axis_index("x"), lax.axis_index("y"), lax.axis_index("c")
    px, py, pc = (mx + (k >> 2)) % 2, (my + ((k >> 1) & 1)) % 2, (mc + (k & 1)) % 2
    return (px, py, pc), 4 * px + 2 * py + pc


def _peer_copy(k, x_ref, land_ref, send_sems, recv_sems, scatter):
    me = 4 * lax.axis_index("x") + 2 * lax.axis_index("y") + lax.axis_index("c")
    dev, slot = _peer(k)
    return pltpu.make_async_remote_copy(
        src_ref=x_ref.at[slot] if scatter else x_ref, dst_ref=land_ref.at[me], send_sem=send_sems.at[k - 1],
        recv_sem=recv_sems.at[k - 1], device_id=dev, device_id_type=pl.DeviceIdType.MESH)


def exchange_start(x, scatter, name):
    blk = x.shape[1:] if scatter else x.shape
    land = lax.empty((N_DEV,) + tuple(blk), x.dtype)

    def body(x_ref, land_ref, send_sems, recv_sems, x_thru, land_thru):
        for k in range(1, N_DEV):
            _peer_copy(k, x_ref, land_ref, send_sems, recv_sems, scatter).start()

    return pl.pallas_call(
        body, name=name, in_specs=(_HBM, _HBM), out_specs=(_SEM, _SEM, _HBM, _HBM), input_output_aliases={0: 2, 1: 3},
        out_shape=(pltpu.SemaphoreType.DMA((N_DEV - 1,)), pltpu.SemaphoreType.DMA((N_DEV - 1,)),
                   pltpu.HBM(x.shape, x.dtype), pltpu.HBM(land.shape, land.dtype)),
        compiler_params=pltpu.CompilerParams(has_side_effects=_EFFECT),
    )(pltpu.with_memory_space_constraint(x, pltpu.HBM), pltpu.with_memory_space_constraint(land, pltpu.HBM))


def exchange_wait(started, after, scatter, name):
    send_sems, recv_sems, x_thru, land_thru = started

    def body(x_ref, land_ref, send_sems, recv_sems, after_ref, x_dead, got_ref):
        for k in range(1, N_DEV):
            cp = _peer_copy(k, x_ref, land_ref, send_sems, recv_sems, scatter)
            cp.wait_send()
            cp.wait_recv()

    return pl.pallas_call(
        body, name=name, in_specs=(_HBM, _HBM, _SEM, _SEM, pl.BlockSpec(memory_space=pl.ANY)), out_specs=(_HBM, _HBM),
        input_output_aliases={0: 0, 1: 1},
        out_shape=(pltpu.HBM(x_thru.shape, x_thru.dtype), pltpu.HBM(land_thru.shape, land_thru.dtype)),
        compiler_params=pltpu.CompilerParams(has_side_effects=_EFFECT),
    )(x_thru, land_thru, send_sems, recv_sems, after)[1]


def sum_slots(x, name, own=None):
    _, rows_total, _ = x.shape
    r = _pick_rows(rows_total, 2048)
    if own is not None:
        def body_own(x_ref, own_ref, o_ref):
            me = 4 * lax.axis_index("x") + 2 * lax.axis_index("y") + lax.axis_index("c")
            acc = None
            for s in range(N_DEV):
                v = jnp.where(me == s, own_ref[...], x_ref[s]).astype(f32)
                acc = v if acc is None else acc + v
            o_ref[...] = acc

        return pl.pallas_call(body_own, name=name, grid=(rows_total // r,),
                              in_specs=[pl.BlockSpec((N_DEV, r, LANE), lambda i: (0, i, 0)), pl.BlockSpec((r, LANE), lambda i: (i, 0))],
                              out_specs=pl.BlockSpec((r, LANE), lambda i: (i, 0)),
                              out_shape=jax.ShapeDtypeStruct((rows_total, LANE), f32), compiler_params=_params())(x, own)

    def body(x_ref, o_ref):
        acc = x_ref[0].astype(f32)
        for s in range(1, N_DEV):
            acc = acc + x_ref[s].astype(f32)
        o_ref[...] = acc

    return pl.pallas_call(body, name=name, grid=(rows_total // r,),
                          in_specs=[pl.BlockSpec((N_DEV, r, LANE), lambda i: (0, i, 0))],
                          out_specs=pl.BlockSpec((r, LANE), lambda i: (i, 0)),
                          out_shape=jax.ShapeDtypeStruct((rows_total, LANE), f32), compiler_params=_params())(x)


def _pick_rows(total, pref):
    best = None
    for d in range(16, min(total, pref) + 1, 16):
        if total % d == 0:
            best = d
    return best if best is not None else total


def adamw(w, g, m, v, name):
    shape = w.shape
    if w.ndim == 1:
        w2, g2, m2, v2 = (t.reshape(1, -1) for t in (w, g, m, v))
    else:
        w2, g2, m2, v2 = (t.reshape(-1, shape[-1]) for t in (w, g, m, v))
    rows_total, cols = w2.shape
    r = _pick_rows(rows_total, max(16, (512 * 1024) // max(cols, 1) // 16 * 16))
    c1, c2 = 1.0 / (1.0 - ADAM_B1 ** ADAM_STEP), 1.0 / (1.0 - ADAM_B2 ** ADAM_STEP)

    def body(w_ref, g_ref, m_ref, v_ref, d_ref, nm_ref, nv_ref):
        gv = g_ref[...]
        nm = ADAM_B1 * m_ref[...] + (1.0 - ADAM_B1) * gv
        nv = ADAM_B2 * v_ref[...] + (1.0 - ADAM_B2) * jnp.square(gv)
        d_ref[...] = -ADAM_LR * ((nm * c1) / (jnp.sqrt(nv * c2) + ADAM_EPS) + ADAM_WD * w_ref[...])
        nm_ref[...] = nm
        nv_ref[...] = nv

    spec = pl.BlockSpec((r, cols), lambda i: (i, 0))
    outs = pl.pallas_call(body, name=name, grid=(rows_total // r,), in_specs=[spec] * 4, out_specs=[spec] * 3,
                          out_shape=[jax.ShapeDtypeStruct((rows_total, cols), f32)] * 3, compiler_params=_params())(w2, g2, m2, v2)
    return tuple(o.reshape(shape) for o in outs)


def _const(j):
    return lambda _: j


def _layer_fwd(x, wl, cos, sin, bsz, seq):
    n = x.shape[0]
    sv = {"x_in": x}
    row1 = lambda a: (a, (1, a.shape[1]), lambda j: (0, 0))
    h = rowmap(f_norm, [(x, D_MODEL, 0)], [row1(wl["attn_norm"])], [(D_MODEL, f32)], 1, "norm_fwd")[0]
    u = mm(h, wl["w_in"], "nn", "mm_in")
    sv["h"], sv["u"] = h, u
    u3 = u.reshape(bsz, seq, U_PAD)

    qkv = []
    for kind in range(3):
        cw = (wl["dn_conv_w"], (4, LANE), functools.partial(lambda j, kind: (0, 4 * kind + j), kind=kind))
        qkv.append(seqmap(functools.partial(f_dn_pre, kind), [(u3, U_QKV // LANE + 4 * kind)], [cw], 1, 4, "dn_pre%d" % kind)[0])
    gb = rowmap(f_dn_gates, [(u, LANE, U_AB // LANE)], [(wl["dn_gate_p"], (8, LANE), lambda j: (0, 0))], [(LANE, f32)], 1,
                "dn_gates")[0]
    gb3 = gb.reshape(bsz, seq, LANE)
    crow = CHUNK * CHUNKS_PER_STEP
    dn_in = [(t.reshape(n, 512), 512, 0) for t in qkv] + [(gb, LANE, 0)]
    prep_a = rowmap(dn_prep, dn_in, [], [(512, f32)] * 5 + [(LANE, f32)], 1, "dn_prep", rows=crow)
    prep_a = [t.reshape(bsz, seq, t.shape[1]) for t in prep_a]
    o_a, ck_a = chunk_scan(dn_step, prep_a, (DN_HEADS, DN_DK, DN_DK), 512, "dn_scan")
    y_a = rowmap(f_dn_post, [(o_a.reshape(n, 512), LANE, 0), (u, LANE, U_Z // LANE)],
                 [(wl["dn_norm_w"], (1, LANE), lambda j: (0, 0))], [(LANE, f32)], 4, "dn_post")[0]
    sv.update(dn_in=dn_in, prep_a=prep_a, o_a=o_a, ck_a=ck_a, y_a=y_a)

    q_b, k_b = rowmap(f_ret_pre, [(u, 256, U_RQ // 256), (u, 256, U_RK // 256), (cos, 256, 0), (sin, 256, 0)], [],
                      [(256, f32), (256, f32)], 1, "ret_pre")
    q_b3, k_b3 = q_b.reshape(bsz, seq, 256), k_b.reshape(bsz, seq, 256)
    v_b3 = lax.slice_in_dim(u3, U_RV, U_RV + 512, axis=2)
    ret_in = [(q_b, 256, 0), (k_b, 256, 0), (u, 512, U_RV // 512)]
    inner = rowmap(ret_prep, ret_in, [], [(512, f32)], 1, "ret_prep", rows=crow)[0]
    ret_seq = [q_b3, k_b3, v_b3, inner.reshape(bsz, seq, 512)]
    o_b, ck_b = chunk_scan(ret_step, ret_seq, (RET_HEADS, 256, RET_DV), 512, "ret_scan")
    y_b = rowmap(f_ret_post, [(o_b.reshape(n, 512), LANE, 0), (u, LANE, U_RG // LANE)], [], [(LANE, f32)], 4, "ret_post")[0]
    sv.update(ret_in=ret_in, ret_seq=ret_seq, o_b=o_b, ck_b=ck_b, y_b=y_b)

    lru_params = _lru_params(wl)
    a_c, b_c = seqmap(f_lru_pre, [(u3, U_CX // LANE)], lru_params, 2, 4, "lru_pre")
    h_c, hp_c = lru_scan(a_c, b_c)
    y_c = rowmap(f_lru_post, [(h_c.reshape(n, 512), 512, 0), (u, 512, U_CG // 512)], [], [(512, f32)], 1, "lru_post")[0]
    sv.update(a_c=a_c, hp_c=hp_c, h_c=h_c, y_c=y_c)

    br = [mm(y, wl["w_branch"][i], "nn", "mm_branch") for i, y in enumerate((y_a, y_b, y_c))]
    merged = rowmap(f_merge, [(u, D_MODEL, i) for i in range(3)] + [(b, D_MODEL, 0) for b in br], [], [(D_MODEL, f32)], 1,
                    "merge")[0]
    x_mid = mm(merged, wl["w_out"], "nn", "mm_out", add=x)
    sv.update(br=br, merged=merged, x_mid=x_mid)

    h2 = rowmap(f_norm, [(x_mid, D_MODEL, 0)], [row1(wl["ffn_norm"])], [(D_MODEL, f32)], 1, "norm_fwd")[0]
    up = mm(h2, wl["w_up"], "nn", "mm_up")
    act = seqmap(f_ffn_mid, [(up.reshape(bsz, seq, 2 * D_FF), 0), (up.reshape(bsz, seq, 2 * D_FF), D_FF // LANE)],
                 _ffn_params(wl), 1, D_FF // LANE, "ffn_mid")[0]
    act = act.reshape(n, D_FF)
    x_out = mm(act, wl["w_down"], "nn", "mm_down", add=x_mid)
    sv.update(h2=h2, up=up, act=act)
    return x_out, sv


def _lru_params(wl):
    col = lambda a: (a, (a.shape[0], LANE), lambda j: (0, j))
    blk = lambda a: (a, (None, LANE, LANE), lambda j: (j, 0, 0))
    return [col(wl["lru_conv_w"]), col(wl["lru_conv_b"]), blk(wl["lru_wa"]), col(wl["lru_ba"]), blk(wl["lru_wx"]),
            col(wl["lru_bx"]), col(wl["lru_lambda"])]


def _ffn_params(wl):
    nb = D_FF // LANE
    return [(wl["ffn_conv_w"], (3, LANE), lambda j: (0, j)), (wl["ffn_conv_w"], (3, LANE), lambda j: (0, nb + j)),
            (wl["ffn_conv_b"], (1, LANE), lambda j: (0, j)), (wl["ffn_conv_b"], (1, LANE), lambda j: (0, nb + j))]


def _layer_bwd(dx, sv, wl, cos, sin, bsz, seq):
    n = dx.shape[0]
    gr = {}
    u, x_in, x_mid = sv["u"], sv["x_in"], sv["x_mid"]
    u3 = u.reshape(bsz, seq, U_PAD)
    row1 = lambda a: (a, (1, a.shape[1]), lambda j: (0, 0))

    d_act = mm(dx, wl["w_down"], "nt", "mm_down_dx")
    gr["w_down"] = mm(sv["act"], dx, "tn", "mm_down_dw")
    up3 = sv["up"].reshape(bsz, seq, 2 * D_FF)
    (d_gate, d_val), dps = seqmap_bwd(f_ffn_mid, [(up3, 0), (up3, D_FF // LANE)], _ffn_params(wl),
                                      [d_act.reshape(bsz, seq, D_FF)], D_FF // LANE, "ffn_mid_bwd")
    gr["ffn_conv_w"] = jnp.concatenate([_cols(dps[0]), _cols(dps[1])], axis=1)
    gr["ffn_conv_b"] = jnp.concatenate([_cols(dps[2]), _cols(dps[3])], axis=1)
    d_up = jnp.concatenate([d_gate, d_val], axis=2).reshape(n, 2 * D_FF)
    d_h2 = mm(d_up, wl["w_up"], "nt", "mm_up_dx")
    gr["w_up"] = mm(sv["h2"], d_up, "tn", "mm_up_dw")
    (dx_mid,), (dg,) = rowmap_bwd(f_norm, [(x_mid, D_MODEL, 0)], [row1(wl["ffn_norm"])], [d_h2], 1, "norm_bwd", add=[dx])
    gr["ffn_norm"] = dg[0, 0]

    d_merged = mm(dx_mid, wl["w_out"], "nt", "mm_out_dx")
    gr["w_out"] = mm(sv["merged"], dx_mid, "tn", "mm_out_dw")
    dm, _ = rowmap_bwd(f_merge, [(u, D_MODEL, i) for i in range(3)] + [(b, D_MODEL, 0) for b in sv["br"]], [], [d_merged], 1,
                       "merge_bwd")
    d_gl, d_br = dm[:3], dm[3:]
    ys = (sv["y_a"], sv["y_b"], sv["y_c"])
    d_ys = [mm(d_br[i], wl["w_branch"][i], "nt", "mm_branch_dx") for i in range(3)]
    gr["w_branch"] = jnp.stack([mm(ys[i], d_br[i], "tn", "mm_branch_dw") for i in range(3)])

    (d_hc, d_cg), _ = rowmap_bwd(f_lru_post, [(sv["h_c"].reshape(n, 512), 512, 0), (u, 512, U_CG // 512)], [], [d_ys[2]], 1,
                                 "lru_post_bwd")
    d_a, d_b = lru_scan_bwd(sv["a_c"], sv["hp_c"], d_hc.reshape(bsz, seq, 512))
    (d_cx,), dps = seqmap_bwd(f_lru_pre, [(u3, U_CX // LANE)], _lru_params(wl), [d_a, d_b], 4, "lru_pre_bwd")
    gr["lru_conv_w"], gr["lru_conv_b"] = _cols(dps[0]), _cols(dps[1])
    gr["lru_wa"], gr["lru_ba"], gr["lru_wx"], gr["lru_bx"] = dps[2], dps[3][:, 0], dps[4], dps[5][:, 0]
    gr["lru_lambda"] = _cols(dps[6])

    (d_ob, d_rg), _ = rowmap_bwd(f_ret_post, [(sv["o_b"].reshape(n, 512), LANE, 0), (u, LANE, U_RG // LANE)], [], [d_ys[1]], 4,
                                 "ret_post_bwd")
    crow = CHUNK * CHUNKS_PER_STEP
    d_ret = chunk_scan_bwd(ret_step, sv["ret_seq"], sv["ck_b"], d_ob.reshape(bsz, seq, 512), "ret_scan_bwd")
    d_ret = [t.reshape(n, t.shape[2]) for t in d_ret]
    (d_qb, d_kb, d_rv), _ = rowmap_bwd(ret_prep, sv["ret_in"], [], [d_ret[3]], 1, "ret_prep_bwd", rows=crow,
                                       add=d_ret[:3])
    dr, _ = rowmap_bwd(f_ret_pre, [(u, 256, U_RQ // 256), (u, 256, U_RK // 256), (cos, 256, 0), (sin, 256, 0)], [],
                       [d_qb, d_kb], 1, "ret_pre_bwd")
    d_rq, d_rk = dr[0], dr[1]

    (d_oa, d_z), (dnw,) = rowmap_bwd(f_dn_post, [(sv["o_a"].reshape(n, 512), LANE, 0), (u, LANE, U_Z // LANE)],
                                     [(wl["dn_norm_w"], (1, LANE), lambda j: (0, 0))], [d_ys[0]], 4, "dn_post_bwd")
    gr["dn_norm_w"] = jnp.sum(dnw, axis=0)[0]
    d_prep = chunk_scan_bwd(dn_step, sv["prep_a"], sv["ck_a"], d_oa.reshape(bsz, seq, 512), "dn_scan_bwd")
    (d_q, d_k, d_v, d_gb), _ = rowmap_bwd(dn_prep, sv["dn_in"], [], [t.reshape(n, t.shape[2]) for t in d_prep], 1,
                                          "dn_prep_bwd", rows=crow)
    d_q, d_k, d_v = (t.reshape(bsz, seq, 512) for t in (d_q, d_k, d_v))
    (d_ab,), (dgp,) = rowmap_bwd(f_dn_gates, [(u, LANE, U_AB // LANE)], [(wl["dn_gate_p"], (8, LANE), lambda j: (0, 0))],
                                 [d_gb], 1, "dn_gates_bwd")
    gr["dn_a_log"], gr["dn_dt_bias"] = dgp[0, 0, :DN_HEADS], dgp[0, 1, :DN_HEADS]
    d_qkv, d_cw = [], []
    for kind, d_t in enumerate((d_q, d_k, d_v)):
        cw = (wl["dn_conv_w"], (4, LANE), functools.partial(lambda j, kind: (0, 4 * kind + j), kind=kind))
        (d_in,), (dcw,) = seqmap_bwd(functools.partial(f_dn_pre, kind), [(u3, U_QKV // LANE + 4 * kind)], [cw], [d_t], 4,
                                     "dn_pre%d_bwd" % kind)
        d_qkv.append(d_in.reshape(n, 512))
        d_cw.append(_cols(dcw))
    gr["dn_conv_w"] = jnp.concatenate(d_cw, axis=1)

    pad = jnp.zeros((n, U_PAD - U_AB - LANE), f32)
    du = jnp.concatenate(list(d_gl) + d_qkv + [d_rv.reshape(n, 512), d_rg, d_z, d_cx.reshape(n, 512), d_cg, d_rq, d_rk, d_ab, pad],
                         axis=1)
    d_h = mm(du, wl["w_in"], "nt", "mm_in_dx")
    gr["w_in"] = mm(sv["h"], du, "tn", "mm_in_dw")
    (dx_in,), (dg,) = rowmap_bwd(f_norm, [(x_in, D_MODEL, 0)], [row1(wl["attn_norm"])], [d_h], 1, "norm_bwd", add=[dx_mid])
    gr["attn_norm"] = dg[0, 0]
    return dx_in, gr


def _cols(dp):
    ncol, p, _ = dp.shape
    return jnp.transpose(dp, (1, 0, 2)).reshape(p, ncol * LANE)


def _pad_w_in(w):
    segs = sorted(_IN_SEGS, key=lambda s: s[2])
    parts = [lax.slice_in_dim(w, src, src + width, axis=1) for src, width, _ in segs]
    end = segs[-1][2] + segs[-1][1]
    return jnp.concatenate(parts + [jnp.zeros((w.shape[0], U_PAD - end), w.dtype)], axis=1)


def _unpad_w_in(wp):
    return jnp.concatenate([lax.slice_in_dim(wp, dst, dst + width, axis=1) for _, width, dst in _IN_SEGS], axis=1)


def _rope_tables(positions):
    half = RET_DK // 2
    inv = ROPE_BASE ** (-jnp.arange(half, dtype=f32) / half)
    ang = positions.astype(f32).reshape(-1, 1) * inv
    cos, sin = jnp.cos(ang), jnp.sin(ang)
    return jnp.tile(cos, (1, 2 * RET_HEADS)), jnp.tile(sin, (1, 2 * RET_HEADS))


def _layer_weights(lw):
    wl = {}
    wl["w_in"] = _pad_w_in(lw["w_in"])
    for k in ("w_branch", "w_out", "w_up", "w_down", "dn_conv_w", "lru_conv_w", "ffn_conv_w", "lru_wa", "lru_wx"):
        wl[k] = lw[k]
    for k in ("attn_norm", "ffn_norm", "dn_norm_w", "lru_conv_b", "lru_lambda", "ffn_conv_b", "lru_ba", "lru_bx"):
        wl[k] = lw[k].reshape(1, -1)
    gp = jnp.zeros((8, LANE), f32)
    wl["dn_gate_p"] = gp.at[0, :DN_HEADS].set(lw["dn_a_log"]).at[1, :DN_HEADS].set(lw["dn_dt_bias"])
    return wl


def forward_backward(x, positions, target, layer_weights, final_norm, on_layer_grads):
    bsz, seq, d = x.shape
    n = bsz * seq
    cos, sin = _rope_tables(positions)
    xs = x.reshape(n, d)
    saved, wls = [], []
    for layer in range(DEPTH):
        wls.append(_layer_weights(layer_weights(layer, xs)))
        xs, sv = _layer_fwd(xs, wls[layer], cos, sin, bsz, seq)
        saved.append(sv)
    loss, dx, d_final = final_loss(xs, final_norm.reshape(1, d), target.reshape(n, d))
    for layer in reversed(range(DEPTH)):
        dx, gr = _layer_bwd(dx, saved[layer], wls[layer], cos, sin, bsz, seq)
        gr["w_in"] = _unpad_w_in(gr["w_in"])
        gr["lru_conv_b"], gr["lru_lambda"], gr["ffn_conv_b"] = gr["lru_conv_b"][0], gr["lru_lambda"][0], gr["ffn_conv_b"][0]
        on_layer_grads(layer, gr, dx)
    return loss[0, 0], dx.reshape(bsz, seq, d), d_final[0]


def local_step(x, positions, target, full):
    grads = {}
    loss, gx, d_final = forward_backward(
        x, positions, target, lambda layer, _: {k: a[layer] for k, a in full.items() if k != "final_norm"},
        full["final_norm"], lambda layer, gr, _: grads.__setitem__(layer, gr))
    stacked = {k: jnp.stack([grads[layer][k] for layer in range(DEPTH)]) for k in grads[0]}
    stacked["final_norm"] = d_final
    return loss, gx, stacked


BIG = (("w_in", 2), ("w_branch", 3), ("w_out", 1), ("w_up", 2), ("w_down", 1))
SMALL_SHARDED = (("dn_conv_w", 2), ("lru_conv_w", 2), ("ffn_conv_w", 2))
REPLICATED = ("attn_norm", "dn_a_log", "dn_dt_bias", "dn_norm_w", "lru_conv_b", "lru_wa", "lru_ba", "lru_wx", "lru_bx",
              "lru_lambda", "ffn_norm", "ffn_conv_b", "final_norm")
WEIGHTS = ("attn_norm", "w_in", "dn_conv_w", "dn_a_log", "dn_dt_bias", "dn_norm_w", "lru_conv_w", "lru_conv_b", "lru_wa",
           "lru_ba", "lru_wx", "lru_bx", "lru_lambda", "w_branch", "w_out", "ffn_norm", "w_up", "ffn_conv_w", "ffn_conv_b",
           "w_down", "final_norm")


def _pack(arrs, dtype, align=16 * LANE):
    flat = jnp.concatenate([a.reshape(-1).astype(dtype) for a in arrs])
    pad = (-flat.shape[0]) % align
    return jnp.pad(flat, (0, pad)).reshape(-1, LANE)


def _unpack(rows, shapes):
    flat = rows.reshape(-1)
    out, pos = [], 0
    for shp in shapes:
        size = math.prod(shp)
        out.append(lax.slice_in_dim(flat, pos, pos + size).reshape(shp))
        pos += size
    return out


def _gather_weights(w, names_axes, dtype, name):
    shards = [w[k] for k, _ in names_axes]
    got = exchange(_pack(shards, dtype), False, name)
    per_dev = [_unpack(got[p], [s.shape for s in shards]) for p in range(N_DEV)]
    return {k: jnp.concatenate([per_dev[p][i] for p in range(N_DEV)], axis=ax) for i, (k, ax) in enumerate(names_axes)}


def _split8(a, axis):
    size = a.shape[axis] // N_DEV
    return [lax.slice_in_dim(a, p * size, (p + 1) * size, axis=axis) for p in range(N_DEV)]


def kernel(x, positions, attn_norm, w_in, dn_conv_w, dn_a_log, dn_dt_bias, dn_norm_w, lru_conv_w, lru_conv_b, lru_wa, lru_ba, lru_wx, lru_bx, lru_lambda, w_branch, w_out, ffn_norm, w_up, ffn_conv_w, ffn_conv_b, w_down, final_norm, loss_target, m_attn_norm, m_w_in, m_dn_conv_w, m_dn_a_log, m_dn_dt_bias, m_dn_norm_w, m_lru_conv_w, m_lru_conv_b, m_lru_wa, m_lru_ba, m_lru_wx, m_lru_bx, m_lru_lambda, m_w_branch, m_w_out, m_ffn_norm, m_w_up, m_ffn_conv_w, m_ffn_conv_b, m_w_down, m_final_norm, v_attn_norm, v_w_in, v_dn_conv_w, v_dn_a_log, v_dn_dt_bias, v_dn_norm_w, v_lru_conv_w, v_lru_conv_b, v_lru_wa, v_lru_ba, v_lru_wx, v_lru_bx, v_lru_lambda, v_w_branch, v_w_out, v_ffn_norm, v_w_up, v_ffn_conv_w, v_ffn_conv_b, v_w_down, v_final_norm):
    w = dict(attn_norm=attn_norm, w_in=w_in, dn_conv_w=dn_conv_w, dn_a_log=dn_a_log, dn_dt_bias=dn_dt_bias, dn_norm_w=dn_norm_w,
             lru_conv_w=lru_conv_w, lru_conv_b=lru_conv_b, lru_wa=lru_wa, lru_ba=lru_ba, lru_wx=lru_wx, lru_bx=lru_bx,
             lru_lambda=lru_lambda, w_branch=w_branch, w_out=w_out, ffn_norm=ffn_norm, w_up=w_up, ffn_conv_w=ffn_conv_w,
             ffn_conv_b=ffn_conv_b, w_down=w_down, final_norm=final_norm)
    m = dict(attn_norm=m_attn_norm, w_in=m_w_in, dn_conv_w=m_dn_conv_w, dn_a_log=m_dn_a_log, dn_dt_bias=m_dn_dt_bias,
             dn_norm_w=m_dn_norm_w, lru_conv_w=m_lru_conv_w, lru_conv_b=m_lru_conv_b, lru_wa=m_lru_wa, lru_ba=m_lru_ba,
             lru_wx=m_lru_wx, lru_bx=m_lru_bx, lru_lambda=m_lru_lambda, w_branch=m_w_branch, w_out=m_w_out, ffn_norm=m_ffn_norm,
             w_up=m_w_up, ffn_conv_w=m_ffn_conv_w, ffn_conv_b=m_ffn_conv_b, w_down=m_w_down, final_norm=m_final_norm)
    v = dict(attn_norm=v_attn_norm, w_in=v_w_in, dn_conv_w=v_dn_conv_w, dn_a_log=v_dn_a_log, dn_dt_bias=v_dn_dt_bias,
             dn_norm_w=v_dn_norm_w, lru_conv_w=v_lru_conv_w, lru_conv_b=v_lru_conv_b, lru_wa=v_lru_wa, lru_ba=v_lru_ba,
             lru_wx=v_lru_wx, lru_bx=v_lru_bx, lru_lambda=v_lru_lambda, w_branch=v_w_branch, w_out=v_w_out, ffn_norm=v_ffn_norm,
             w_up=v_w_up, ffn_conv_w=v_ffn_conv_w, ffn_conv_b=v_ffn_conv_b, w_down=v_w_down, final_norm=v_final_norm)

    me = 4 * lax.axis_index("x") + 2 * lax.axis_index("y") + lax.axis_index("c")
    small_full = _gather_weights(w, SMALL_SHARDED, f32, "gather_small")

    shard_shapes = [w[k].shape[1:] for k, _ in BIG]
    packs = [_pack([w[k][layer] for k, _ in BIG], bf16) for layer in range(DEPTH)]
    gathers = [exchange_start(packs[layer], False, "gather_big_start%d" % layer) for layer in range(DEPTH)]

    def layer_weights(layer, x_in):
        land = exchange_wait(gathers[layer], x_in, False, "gather_big_wait%d" % layer)
        per_dev = [_unpack(jnp.where(me == p, packs[layer], land[p]), shard_shapes) for p in range(N_DEV)]
        lw = {k: jnp.concatenate([per_dev[p][i] for p in range(N_DEV)], axis=ax - 1) for i, (k, ax) in enumerate(BIG)}
        lw.update({k: small_full[k][layer] for k, _ in SMALL_SHARDED})
        lw.update({k: w[k][layer] for k in REPLICATED if k != "final_norm"})
        return lw

    scatters, layer_grads = {}, {}

    def on_layer_grads(layer, gr, dx):
        blocks = jnp.stack([_pack([_split8(gr[k], ax - 1)[p] for k, ax in BIG], bf16) for p in range(N_DEV)])
        own = lax.dynamic_index_in_dim(blocks, me, axis=0, keepdims=False)
        scatters[layer] = (exchange_start(blocks, True, "scatter_big_start%d" % layer), own)
        layer_grads[layer] = gr

    loss_part, grad_x, d_final = forward_backward(x, positions, loss_target, layer_weights, final_norm, on_layer_grads)
    g_full = {k: jnp.stack([layer_grads[layer][k] for layer in range(DEPTH)]) for k in layer_grads[0] if k not in dict(BIG)}
    g_full["final_norm"] = d_final

    small_names = [k for k, _ in SMALL_SHARDED] + list(REPLICATED)
    small = _pack([loss_part.reshape(1)] + [g_full[k] for k in small_names], f32)
    small_sum = sum_slots(exchange(small, False, "gather_grads"), "sum_small")
    tot = _unpack(small_sum, [(1,)] + [g_full[k].shape for k in small_names])
    loss = tot[0][0]

    per_layer = []
    for layer in range(DEPTH):
        land = exchange_wait(scatters[layer][0], small_sum, True, "scatter_big_wait%d" % layer)
        per_layer.append(_unpack(sum_slots(land, "sum_big", own=scatters[layer][1]), shard_shapes))
    grads = {k: jnp.stack([per_layer[layer][i] for layer in range(DEPTH)]) for i, (k, _) in enumerate(BIG)}

    for k, g in zip(small_names, tot[1:]):
        ax = dict(SMALL_SHARDED).get(k)
        if ax is None:
            grads[k] = g
        else:
            size = g.shape[ax] // N_DEV
            grads[k] = lax.dynamic_slice_in_dim(g, me * size, size, axis=ax)

    upd = {k: adamw(w[k], grads[k], m[k], v[k], "adamw_" + k) for k in WEIGHTS}
    return (loss, grad_x, *[grads[k] for k in WEIGHTS], *[upd[k][0] for k in WEIGHTS], *[upd[k][1] for k in WEIGHTS],
            *[upd[k][2] for k in WEIGHTS])
```

```python
import functools
import math

import jax
import jax.numpy as jnp
from jax import lax
from jax.experimental import pallas as pl
from jax.experimental.pallas import tpu as pltpu

f32 = jnp.float32
bf16 = jnp.bfloat16

D_MODEL = 1024
DEPTH = 4
CHUNK = 64
EPS = 1e-6
DN_HEADS, DN_DK = 4, 128
RET_HEADS, RET_DK, RET_DV = 4, 64, 128
ROPE_BASE = 10000.0
LRU_C = 8.0
D_FF = 2816
N_DEV = 8
LANE = 128
VMEM_LIMIT = 56 * 1024 * 1024

ADAM_LR, ADAM_B1, ADAM_B2, ADAM_EPS, ADAM_WD, ADAM_STEP = 0.001, 0.9, 0.999, 1e-8, 0.01, 10

U_GATES, U_QKV, U_RV, U_RG, U_Z, U_CX, U_CG, U_RQ, U_RK, U_AB = (
    0, 3072, 4608, 5120, 5632, 6144, 6656, 7168, 7424, 7680)
U_PAD = 8192
_IN_SEGS = ((0, 1536, U_QKV), (1536, 8, U_AB), (1544, 512, U_Z), (2056, 256, U_RQ), (2312, 256, U_RK),
            (2568, 512, U_RV), (3080, 512, U_RG), (3592, 512, U_CX), (4104, 512, U_CG), (4616, 3072, U_GATES))
N_IN = 7688


def _params():
    return pltpu.CompilerParams(vmem_limit_bytes=VMEM_LIMIT)


def _pick(dim, pref):
    best = None
    for d in range(LANE, min(dim, pref) + 1, LANE):
        if dim % d == 0:
            best = d
    return best if best is not None else dim


@functools.partial(jax.custom_vjp, nondiff_argnums=(1, 2))
def sroll(x, shift, axis):
    return pltpu.roll(x, shift, axis)


def _sroll_fwd(x, shift, axis):
    return pltpu.roll(x, shift, axis), None


def _sroll_bwd(shift, axis, _, g):
    n = g.shape[axis]
    return (pltpu.roll(g, (n - shift) % n, axis),)


sroll.defvjp(_sroll_fwd, _sroll_bwd)

_DIMS = {"nn": (((1,), (0,)), ((), ())), "nt": (((1,), (1,)), ((), ())), "tn": (((0,), (0,)), ((), ()))}


def _dg(a, b, dims):
    return lax.dot_general(a.astype(bf16), b.astype(bf16), _DIMS[dims], preferred_element_type=f32)


@functools.partial(jax.custom_vjp, nondiff_argnums=(2,))
def bdot(a, b, dims):
    return _dg(a, b, dims)


def _bdot_fwd(a, b, dims):
    return _dg(a, b, dims), (a.astype(bf16), b.astype(bf16))


def _bdot_bwd(dims, res, g):
    a, b = res
    if dims == "nn":
        return _dg(g, b, "nt"), _dg(a, g, "tn")
    if dims == "nt":
        return _dg(g, b, "nn"), _dg(g, a, "tn")
    return _dg(b, g, "nt"), _dg(a, g, "nn")


bdot.defvjp(_bdot_fwd, _bdot_bwd)


def _fdot(a, b, dims):
    return lax.dot_general(a, b, _DIMS[dims], precision=lax.Precision.HIGH, preferred_element_type=f32)


@jax.custom_vjp
def unit_lower_inv_all(mats):
    shape = mats[0].shape
    row = lax.broadcasted_iota(jnp.int32, shape, 0)
    col = lax.broadcasted_iota(jnp.int32, shape, 1)
    eye = jnp.where(row == col, 1.0, 0.0).astype(f32)
    n = [-a for a in mats]
    p = [eye + x for x in n]
    span = 2
    while span < shape[0]:
        n = [_fdot(x, x, "nn") for x in n]
        p = [y + _fdot(y, x, "nn") for y, x in zip(p, n)]
        span *= 2
    return p


def _uli_fwd(mats):
    x = unit_lower_inv_all(mats)
    return x, x


def _uli_bwd(xs, gs):
    t = [_fdot(x, g, "tn") for x, g in zip(xs, gs)]
    return ([-_fdot(y, x, "nt") for y, x in zip(t, xs)],)


unit_lower_inv_all.defvjp(_uli_fwd, _uli_bwd)


def cumsum_rows(x):
    rows = x.shape[0]
    row = lax.broadcasted_iota(jnp.int32, x.shape, 0)
    s = 1
    while s < rows:
        x = x + jnp.where(row >= s, sroll(x, s, 0), 0.0)
        s *= 2
    return x


def _expm1(x):
    return jnp.tanh(0.5 * x) * (jnp.exp(x) + 1.0)


def _lane_pick(x, lane):
    idx = lax.broadcasted_iota(jnp.int32, x.shape, 1)
    return jnp.sum(jnp.where(idx == lane, x, 0.0), axis=1, keepdims=True)


def _row_pick(x, r):
    idx = lax.broadcasted_iota(jnp.int32, x.shape, 0)
    return jnp.sum(jnp.where(idx == r, x, 0.0), axis=0, keepdims=True)


def _causal_conv(x, halo, w, width):
    xe = jnp.concatenate([halo, x], axis=0)
    acc = xe * w[width - 1:width]
    for k in range(width - 1):
        acc = acc + sroll(xe, width - 1 - k, 0) * w[k:k + 1]
    return acc[8:]


def f_norm(ins, ps):
    (x,), (g,) = ins, ps
    return [x * lax.rsqrt(jnp.mean(x * x, axis=-1, keepdims=True) + EPS) * g]


def f_dn_pre(kind, mains, halos, ps):
    y = _causal_conv(mains[0], halos[0], ps[0], 4)
    y = y * jax.nn.sigmoid(y)
    if kind < 2:
        y = y * lax.rsqrt(jnp.sum(y * y, axis=-1, keepdims=True) + EPS)
    if kind == 0:
        y = y * (DN_DK ** -0.5)
    return [y]


def f_dn_gates(ins, ps):
    (u,), (p,) = ins, ps
    lane = lax.broadcasted_iota(jnp.int32, u.shape, 1)
    g = -jnp.exp(p[0:1]) * jax.nn.softplus(u + p[1:2])
    beta = jax.nn.sigmoid(u)
    return [jnp.where(lane < 4, g, jnp.where(lane < 8, beta, 0.0))]


def f_dn_post(ins, ps):
    (o, z), (nw,) = ins, ps
    y = o * lax.rsqrt(jnp.mean(o * o, axis=-1, keepdims=True) + EPS) * nw
    return [y * (z * jax.nn.sigmoid(z))]


def _rot_half(t):
    lane = lax.broadcasted_iota(jnp.int32, t.shape, 1)
    width = t.shape[1]
    first = (lane % RET_DK) < (RET_DK // 2)
    return jnp.where(first, -sroll(t, width - RET_DK // 2, 1), sroll(t, RET_DK // 2, 1))


def f_ret_pre(ins, ps):
    q, k, cos, sin = ins
    qr = q * cos + _rot_half(q) * sin
    kr = (k * cos + _rot_half(k) * sin) * (RET_DK ** -0.5)
    return [qr, kr]


def f_ret_post(ins, ps):
    o, g = ins
    mu = jnp.mean(o, axis=-1, keepdims=True)
    var = jnp.mean(jnp.square(o - mu), axis=-1, keepdims=True)
    return [(o - mu) * lax.rsqrt(var + EPS) * (g * jax.nn.sigmoid(g))]


def f_lru_pre(mains, halos, ps):
    cw, cb, wa, ba, wx, bx, lam = ps
    xc = _causal_conv(mains[0], halos[0], cw, 4) + cb
    r = jax.nn.sigmoid(bdot(xc, wa, "nn") + ba)
    i = jax.nn.sigmoid(bdot(xc, wx, "nn") + bx)
    log_a = -LRU_C * r * jax.nn.softplus(-lam)
    a = jnp.exp(log_a)
    b = jnp.sqrt(-_expm1(2.0 * log_a)) * (i * xc)
    return [a, b]


def f_lru_post(ins, ps):
    h, g = ins
    return [h * jax.nn.gelu(g)]


def f_merge(ins, ps):
    g0, g1, g2, b0, b1, b2 = ins
    return [jax.nn.sigmoid(g0) * b0 + jax.nn.sigmoid(g1) * b1 + jax.nn.sigmoid(g2) * b2]


def f_ffn_mid(mains, halos, ps):
    cwg, cwv, cbg, cbv = ps
    gate = _causal_conv(mains[0], halos[0], cwg, 3) + cbg
    val = _causal_conv(mains[1], halos[1], cwv, 3) + cbv
    return [gate * jax.nn.sigmoid(gate) * val]


def mm(a, b, dims, name, add=None, tm=512, tn=1024, tk=1024):
    if dims == "tn":
        kdim, m = a.shape
        n = b.shape[1]
    else:
        m, kdim = a.shape
        n = b.shape[0] if dims == "nt" else b.shape[1]
    tm, tn, tk = _pick(m, tm), _pick(n, tn), _pick(kdim, tk)
    nk = kdim // tk
    a_spec = pl.BlockSpec((tk, tm), lambda i, j, k: (k, i)) if dims == "tn" else pl.BlockSpec((tm, tk), lambda i, j, k: (i, k))
    b_spec = pl.BlockSpec((tn, tk), lambda i, j, k: (j, k)) if dims == "nt" else pl.BlockSpec((tk, tn), lambda i, j, k: (k, j))
    o_spec = pl.BlockSpec((tm, tn), lambda i, j, k: (i, j))
    has_add = add is not None

    def body(*refs):
        if has_add:
            a_ref, b_ref, add_ref, o_ref, acc_ref = refs
        else:
            a_ref, b_ref, o_ref, acc_ref = refs
        k = pl.program_id(2)

        @pl.when(k == 0)
        def _():
            acc_ref[...] = jnp.zeros_like(acc_ref)

        acc_ref[...] += _dg(a_ref[...], b_ref[...], dims)

        @pl.when(k == nk - 1)
        def _():
            o_ref[...] = acc_ref[...] + add_ref[...] if has_add else acc_ref[...]

    args = (a, b, add) if has_add else (a, b)
    in_specs = [a_spec, b_spec] + ([o_spec] if has_add else [])
    return pl.pallas_call(
        body, name=name, grid=(m // tm, n // tn, nk), in_specs=in_specs, out_specs=o_spec,
        out_shape=jax.ShapeDtypeStruct((m, n), f32), scratch_shapes=[pltpu.VMEM((tm, tn), f32)],
        compiler_params=_params())(*args)


def rowmap(fn, ins, params, outs, ncol, name, rows=256):
    n = ins[0][0].shape[0]
    r = min(rows, n)
    nin, npar = len(ins), len(params)

    def body(*refs):
        vals = [x[...] for x in refs[:nin]]
        pv = [p[...] for p in refs[nin:nin + npar]]
        for o_ref, o in zip(refs[nin + npar:], fn(vals, pv)):
            o_ref[...] = o.astype(o_ref.dtype)

    in_specs = [pl.BlockSpec((r, cb), functools.partial(lambda j, i, off: (i, off + j), off=off)) for _, cb, off in ins]
    in_specs += [pl.BlockSpec(bs, functools.partial(lambda j, i, f: f(j), f=f)) for _, bs, f in params]
    out_specs = [pl.BlockSpec((r, cb), lambda j, i: (i, j)) for cb, _ in outs]
    out_shape = [jax.ShapeDtypeStruct((n, cb * ncol), dt) for cb, dt in outs]
    res = pl.pallas_call(body, name=name, grid=(ncol, n // r), in_specs=in_specs, out_specs=out_specs,
                         out_shape=out_shape, compiler_params=_params())(*[a for a, _, _ in ins], *[a for a, _, _ in params])
    return res


def rowmap_bwd(fn, ins, params, douts, ncol, name, rows=256, add=None):
    n = ins[0][0].shape[0]
    r = min(rows, n)
    nin, npar, nout = len(ins), len(params), len(douts)
    add = [None] * nin if add is None else list(add)
    add_idx = [i for i in range(nin) if add[i] is not None]

    def body(*refs):
        vals = [x[...] for x in refs[:nin]]
        pv = [p[...] for p in refs[nin:nin + npar]]
        dys = [d[...] for d in refs[nin + npar:nin + npar + nout]]
        k0 = nin + npar + nout
        add_refs = dict(zip(add_idx, refs[k0:k0 + len(add_idx)]))
        k0 += len(add_idx)
        din_refs = refs[k0:k0 + nin]
        dp_refs = refs[k0 + nin:]
        _, vjp = jax.vjp(fn, vals, pv)
        dvals, dpv = vjp(dys)
        for idx, (d_ref, d) in enumerate(zip(din_refs, dvals)):
            d_ref[...] = d + add_refs[idx][...] if idx in add_refs else d

        @pl.when(pl.program_id(1) == 0)
        def _():
            for d_ref in dp_refs:
                d_ref[...] = jnp.zeros_like(d_ref)

        for d_ref, d in zip(dp_refs, dpv):
            d_ref[...] += d

    in_specs = [pl.BlockSpec((r, cb), functools.partial(lambda j, i, off: (i, off + j), off=off)) for _, cb, off in ins]
    in_specs += [pl.BlockSpec(bs, functools.partial(lambda j, i, f: f(j), f=f)) for _, bs, f in params]
    in_specs += [pl.BlockSpec((r, d.shape[1] // ncol), lambda j, i: (i, j)) for d in douts]
    in_specs += [pl.BlockSpec((r, ins[i][1]), lambda j, i: (i, j)) for i in add_idx]
    out_specs = [pl.BlockSpec((r, cb), lambda j, i: (i, j)) for _, cb, _ in ins]
    pshapes = [tuple(d for d in bs if d is not None) for _, bs, _ in params]
    out_specs += [pl.BlockSpec((None,) + ps, functools.partial(lambda j, i, nd: (j,) + (0,) * nd, nd=len(ps))) for ps in pshapes]
    out_shape = [jax.ShapeDtypeStruct((n, cb * ncol), f32) for _, cb, _ in ins]
    out_shape += [jax.ShapeDtypeStruct((ncol,) + ps, f32) for ps in pshapes]
    args = [a for a, _, _ in ins] + [a for a, _, _ in params] + list(douts) + [add[i] for i in add_idx]
    res = pl.pallas_call(body, name=name, grid=(ncol, n // r), in_specs=in_specs, out_specs=out_specs,
                         out_shape=out_shape, compiler_params=_params())(*args)
    return res[:nin], res[nin:]


SEQ_ROWS = 256


def seqmap(fn, ins, params, nouts, ncol, name):
    bsz, seq, _ = ins[0][0].shape
    r = min(SEQ_ROWS, seq)
    nin, npar = len(ins), len(params)

    def body(*refs):
        in_refs = refs[:nin]
        pv = [p[...] for p in refs[nin:nin + npar]]
        out_refs = refs[nin + npar:]

        def step(i, carry):
            r0 = pl.multiple_of(i * r, r)
            h0 = pl.multiple_of(jnp.maximum(r0 - 8, 0), 8)
            mains = [x[pl.ds(r0, r), :] for x in in_refs]
            halos = [jnp.where(i == 0, 0.0, x[pl.ds(h0, 8), :]) for x in in_refs]
            for o_ref, o in zip(out_refs, fn(mains, halos, pv)):
                o_ref[pl.ds(r0, r), :] = o
            return carry

        lax.fori_loop(0, seq // r, step, 0)

    in_specs = [pl.BlockSpec((None, seq, LANE), functools.partial(lambda j, b, off: (b, 0, off + j), off=off)) for _, off in ins]
    in_specs += [pl.BlockSpec(bs, functools.partial(lambda j, b, f: f(j), f=f)) for _, bs, f in params]
    out_specs = [pl.BlockSpec((None, seq, LANE), lambda j, b: (b, 0, j)) for _ in range(nouts)]
    out_shape = [jax.ShapeDtypeStruct((bsz, seq, LANE * ncol), f32) for _ in range(nouts)]
    return pl.pallas_call(body, name=name, grid=(ncol, bsz), in_specs=in_specs, out_specs=out_specs,
                          out_shape=out_shape, compiler_params=_params())(*[a for a, _ in ins], *[a for a, _, _ in params])


def seqmap_bwd(fn, ins, params, douts, ncol, name):
    bsz, seq, _ = ins[0][0].shape
    r = min(SEQ_ROWS, seq)
    nin, npar, nout = len(ins), len(params), len(douts)

    def body(*refs):
        in_refs = refs[:nin]
        pv = [p[...] for p in refs[nin:nin + npar]]
        dy_refs = refs[nin + npar:nin + npar + nout]
        din_refs = refs[nin + npar + nout:nin + npar + nout + nin]
        dp_refs = refs[nin + npar + nout + nin:]

        def step(i, dp_acc):
            r0 = pl.multiple_of(i * r, r)
            h0 = pl.multiple_of(jnp.maximum(r0 - 8, 0), 8)
            mains = [x[pl.ds(r0, r), :] for x in in_refs]
            halos_raw = [x[pl.ds(h0, 8), :] for x in in_refs]

            def tile(mains, halos_raw, pv):
                return fn(mains, [jnp.where(i == 0, 0.0, h) for h in halos_raw], pv)

            _, vjp = jax.vjp(tile, mains, halos_raw, pv)
            dm, dh, dp = vjp([d[pl.ds(r0, r), :] for d in dy_refs])
            for d_ref, m, h in zip(din_refs, dm, dh):
                d_ref[pl.ds(r0, r), :] = m
                d_ref[pl.ds(h0, 8), :] += h
            return [acc + d for acc, d in zip(dp_acc, dp)]

        dp = lax.fori_loop(0, seq // r, step, [jnp.zeros(p.shape, f32) for p in pv])

        @pl.when(pl.program_id(1) == 0)
        def _():
            for d_ref in dp_refs:
                d_ref[...] = jnp.zeros_like(d_ref)

        for d_ref, d in zip(dp_refs, dp):
            d_ref[...] += d

    in_specs = [pl.BlockSpec((None, seq, LANE), functools.partial(lambda j, b, off: (b, 0, off + j), off=off)) for _, off in ins]
    in_specs += [pl.BlockSpec(bs, functools.partial(lambda j, b, f: f(j), f=f)) for _, bs, f in params]
    in_specs += [pl.BlockSpec((None, seq, LANE), lambda j, b: (b, 0, j)) for _ in range(nout)]
    out_specs = [pl.BlockSpec((None, seq, LANE), lambda j, b: (b, 0, j)) for _ in range(nin)]
    pshapes = [tuple(d for d in bs if d is not None) for _, bs, _ in params]
    out_specs += [pl.BlockSpec((None,) + ps, functools.partial(lambda j, b, nd: (j,) + (0,) * nd, nd=len(ps))) for ps in pshapes]
    out_shape = [jax.ShapeDtypeStruct((bsz, seq, LANE * ncol), f32) for _ in range(nin)]
    out_shape += [jax.ShapeDtypeStruct((ncol,) + ps, f32) for ps in pshapes]
    args = [a for a, _ in ins] + [a for a, _, _ in params] + list(douts)
    res = pl.pallas_call(body, name=name, grid=(ncol, bsz), in_specs=in_specs, out_specs=out_specs,
                         out_shape=out_shape, compiler_params=_params())(*args)
    return res[:nin], res[nin:]


def _tri_masks():
    row = lax.broadcasted_iota(jnp.int32, (CHUNK, CHUNK), 0)
    col = lax.broadcasted_iota(jnp.int32, (CHUNK, CHUNK), 1)
    return row >= col, row > col


CHUNKS_PER_STEP = 4


def _by_rows(parts, per_row):
    rows = [jnp.concatenate(parts[i:i + per_row], axis=1) for i in range(0, len(parts), per_row)]
    return jnp.concatenate(rows, axis=0)


def dn_prep(vals, ps):
    q, k, v, gb = vals
    nchunk = q.shape[0] // CHUNK
    causal, strict = _tri_masks()
    gbs = [gb[c * CHUNK:(c + 1) * CHUNK] for c in range(nchunk)]
    gcs = [cumsum_rows(g) for g in gbs]
    gcts = [g.T for g in gcs]
    chains = [(c, h) for c in range(nchunk) for h in range(DN_HEADS)]
    part = lambda t, c, h: t[c * CHUNK:(c + 1) * CHUNK, h * DN_DK:(h + 1) * DN_DK]
    qh = [part(q, c, h) for c, h in chains]
    kh = [part(k, c, h) for c, h in chains]
    vh = [part(v, c, h) for c, h in chains]
    g_col = [_lane_pick(gcs[c], h) for c, h in chains]
    beta = [_lane_pick(gbs[c], DN_HEADS + h) for c, h in chains]
    g_row = [_row_pick(gcts[c], h)[:, :CHUNK] for c, h in chains]
    decay = [jnp.where(causal, jnp.exp(jnp.where(causal, gc - gr, 0.0)), 0.0) for gc, gr in zip(g_col, g_row)]
    k_beta = [a * b for a, b in zip(kh, beta)]
    eg = [jnp.exp(g) for g in g_col]
    kk = [bdot(a, b, "nt") for a, b in zip(k_beta, kh)]
    qk = [bdot(a, b, "nt") for a, b in zip(qh, kh)]
    t_inv = unit_lower_inv_all([jnp.where(strict, a * d, 0.0) for a, d in zip(kk, decay)])
    u = [bdot(t, a * b, "nn") for t, a, b in zip(t_inv, vh, beta)]
    w = [bdot(t, a * e, "nn") for t, a, e in zip(t_inv, k_beta, eg)]
    attn = [jnp.concatenate([a * d, jnp.zeros((CHUNK, DN_DK - CHUNK), f32)], axis=1) for a, d in zip(qk, decay)]
    qd = [a * e for a, e in zip(qh, eg)]
    kd = [a * jnp.exp(_row_pick(g, CHUNK - 1) - g) for a, g in zip(kh, g_col)]
    g_last = jnp.concatenate([jnp.broadcast_to(_row_pick(g, CHUNK - 1), g.shape) for g in gcs], axis=0)
    return [_by_rows(t, DN_HEADS) for t in (u, w, attn, qd, kd)] + [g_last]


def dn_step(state, u, w, attn, qd, kd, g_last):
    bsz = u.shape[0]
    chains = [(b, h) for b in range(bsz) for h in range(DN_HEADS)]
    part = lambda t, b, h: t[b, :, h * DN_DK:(h + 1) * DN_DK]
    ws = [bdot(part(w, b, h), s, "nn") for (b, h), s in zip(chains, state)]
    qs = [bdot(part(qd, b, h), s, "nn") for (b, h), s in zip(chains, state)]
    v_new = [part(u, b, h) - x for (b, h), x in zip(chains, ws)]
    av = [bdot(attn[b, :, h * DN_DK:h * DN_DK + CHUNK], x, "nn") for (b, h), x in zip(chains, v_new)]
    kv = [bdot(part(kd, b, h), x, "tn") for (b, h), x in zip(chains, v_new)]
    ge = [jnp.exp(_row_pick(_lane_pick(g_last[b], h), 0)) for b, h in chains]
    new_state = [s * g + x for s, g, x in zip(state, ge, kv)]
    outs = [a + b for a, b in zip(qs, av)]
    return new_state, jnp.concatenate([jnp.concatenate(outs[b * DN_HEADS:(b + 1) * DN_HEADS], axis=1)[None]
                                       for b in range(bsz)], axis=0)


def _ret_log_gamma(h):
    return math.log(1.0 - 2.0 ** (-5.0 - h))


def ret_prep(vals, ps):
    q, k, v = vals
    nchunk = q.shape[0] // CHUNK
    causal, _ = _tri_masks()
    row = lax.broadcasted_iota(jnp.int32, (CHUNK, CHUNK), 0)
    col = lax.broadcasted_iota(jnp.int32, (CHUNK, CHUNK), 1)
    dist = (row - col).astype(f32)
    lane = lax.broadcasted_iota(jnp.int32, (CHUNK, q.shape[1]), 1)
    dmask = [jnp.where(causal, jnp.exp(jnp.where(causal, dist, 0.0) * _ret_log_gamma(h)), 0.0) for h in range(RET_HEADS)]
    chains = [(c, h) for c in range(nchunk) for h in range(RET_HEADS)]
    rows = lambda t, c: t[c * CHUNK:(c + 1) * CHUNK]
    scores = [bdot(jnp.where((lane // RET_DK) == h, rows(q, c), 0.0), rows(k, c), "nt") * dmask[h] for c, h in chains]
    inner = [bdot(s, rows(v, c)[:, h * RET_DV:(h + 1) * RET_DV], "nn") for s, (c, h) in zip(scores, chains)]
    return [_by_rows(inner, RET_HEADS)]


def ret_step(state, q, k, v, inner):
    bsz = q.shape[0]
    idx = lax.broadcasted_iota(jnp.int32, (CHUNK, 1), 0).astype(f32)
    lane = lax.broadcasted_iota(jnp.int32, (CHUNK, q.shape[2]), 1)
    chains = [(b, h) for b in range(bsz) for h in range(RET_HEADS)]
    part = lambda t, b, h: t[b, :, h * RET_DV:(h + 1) * RET_DV]
    cross = [bdot(q[b], s, "nn") for (b, h), s in zip(chains, state)]
    kz = [jnp.where((lane // RET_DK) == h, k[b], 0.0) * jnp.exp((CHUNK - 1.0 - idx) * _ret_log_gamma(h)) for b, h in chains]
    kv = [bdot(a, part(v, b, h), "tn") for a, (b, h) in zip(kz, chains)]
    outs = [x * jnp.exp((idx + 1.0) * _ret_log_gamma(h)) + part(inner, b, h) for x, (b, h) in zip(cross, chains)]
    new_state = [s * math.exp(CHUNK * _ret_log_gamma(h)) + x for s, x, (b, h) in zip(state, kv, chains)]
    return new_state, jnp.concatenate([jnp.concatenate(outs[b * RET_HEADS:(b + 1) * RET_HEADS], axis=1)[None]
                                       for b in range(bsz)], axis=0)


def chunk_scan(step_fn, ins, state_shape, out_width, name):
    bsz, seq, _ = ins[0].shape
    nchunk = seq // CHUNK
    nin = len(ins)
    nh = state_shape[0]

    def body(*refs):
        in_refs = refs[:nin]
        o_ref, ck_ref, s_ref = refs[nin:]

        @pl.when(pl.program_id(0) == 0)
        def _():
            s_ref[...] = jnp.zeros_like(s_ref)

        state = [s_ref[i] for i in range(bsz * nh)]
        for i in range(bsz * nh):
            ck_ref[i // nh, i % nh] = state[i]
        new_state, out = step_fn(state, *[x[...] for x in in_refs])
        o_ref[...] = out
        for i in range(bsz * nh):
            s_ref[i] = new_state[i]

    in_specs = [pl.BlockSpec((bsz, CHUNK, x.shape[2]), lambda n: (0, n, 0)) for x in ins]
    out_specs = [pl.BlockSpec((bsz, CHUNK, out_width), lambda n: (0, n, 0)),
                 pl.BlockSpec((bsz, None) + tuple(state_shape), lambda n: (0, n, 0, 0, 0))]
    out_shape = [jax.ShapeDtypeStruct((bsz, seq, out_width), f32),
                 jax.ShapeDtypeStruct((bsz, nchunk) + tuple(state_shape), f32)]
    return pl.pallas_call(body, name=name, grid=(nchunk,), in_specs=in_specs, out_specs=out_specs, out_shape=out_shape,
                          scratch_shapes=[pltpu.VMEM((bsz * nh,) + tuple(state_shape[1:]), f32)],
                          compiler_params=_params())(*ins)


def chunk_scan_bwd(step_fn, ins, ckpt, dout, name):
    bsz, seq, _ = ins[0].shape
    nchunk = seq // CHUNK
    nin = len(ins)
    state_shape = ckpt.shape[2:]
    nh = state_shape[0]

    def body(*refs):
        in_refs = refs[:nin]
        ck_ref, do_ref = refs[nin:nin + 2]
        din_refs = refs[nin + 2:nin + 2 + nin]
        ds_ref = refs[-1]

        @pl.when(pl.program_id(0) == 0)
        def _():
            ds_ref[...] = jnp.zeros_like(ds_ref)

        state = [ck_ref[i // nh, i % nh] for i in range(bsz * nh)]
        _, vjp = jax.vjp(step_fn, state, *[x[...] for x in in_refs])
        grads = vjp(([ds_ref[i] for i in range(bsz * nh)], do_ref[...]))
        for i in range(bsz * nh):
            ds_ref[i] = grads[0][i]
        for d_ref, d in zip(din_refs, grads[1:]):
            d_ref[...] = d

    rev = lambda n: (0, nchunk - 1 - n, 0)
    in_specs = [pl.BlockSpec((bsz, CHUNK, x.shape[2]), rev) for x in ins]
    in_specs += [pl.BlockSpec((bsz, None) + tuple(state_shape), lambda n: (0, nchunk - 1 - n, 0, 0, 0)),
                 pl.BlockSpec((bsz, CHUNK, dout.shape[2]), rev)]
    out_specs = [pl.BlockSpec((bsz, CHUNK, x.shape[2]), rev) for x in ins]
    out_shape = [jax.ShapeDtypeStruct(x.shape, f32) for x in ins]
    return pl.pallas_call(body, name=name, grid=(nchunk,), in_specs=in_specs, out_specs=out_specs, out_shape=out_shape,
                          scratch_shapes=[pltpu.VMEM((bsz * nh,) + tuple(state_shape[1:]), f32)],
                          compiler_params=_params())(*ins, ckpt, dout)


LRU_ROWS = 512


def lru_scan(a, b):
    bsz, seq, width = a.shape
    rb = min(LRU_ROWS, seq)

    def body(a_ref, b_ref, h_ref, hp_ref, carry_ref):
        @pl.when(pl.program_id(1) == 0)
        def _():
            carry_ref[...] = jnp.zeros_like(carry_ref)

        row = lax.broadcasted_iota(jnp.int32, (8, width), 0)

        def tile(t, hprev):
            r0 = pl.multiple_of(t * 8, 8)
            ca, cbv = a_ref[pl.ds(r0, 8), :], b_ref[pl.ds(r0, 8), :]
            for s in (1, 2, 4):
                m = row >= s
                cbv = jnp.where(m, ca * pltpu.roll(cbv, s, 0) + cbv, cbv)
                ca = jnp.where(m, ca * pltpu.roll(ca, s, 0), ca)
            h = cbv + ca * hprev
            h_ref[pl.ds(r0, 8), :] = h
            hp_ref[pl.ds(r0, 8), :] = jnp.where(row == 0, hprev, pltpu.roll(h, 1, 0))
            return _row_pick(h, 7)

        carry_ref[0:1, :] = lax.fori_loop(0, rb // 8, tile, carry_ref[0:1, :])

    spec = pl.BlockSpec((None, rb, width), lambda bi, i: (bi, i, 0))
    return pl.pallas_call(body, name="lru_scan", grid=(bsz, seq // rb), in_specs=[spec, spec], out_specs=[spec, spec],
                          out_shape=[jax.ShapeDtypeStruct(a.shape, f32)] * 2,
                          scratch_shapes=[pltpu.VMEM((8, width), f32)], compiler_params=_params())(a, b)


def lru_scan_bwd(a, hp, dh):
    bsz, seq, width = a.shape
    rb = min(LRU_ROWS, seq)
    nblk = seq // rb

    def body(a_ref, hp_ref, dh_ref, da_ref, db_ref, carry_ref):
        @pl.when(pl.program_id(1) == 0)
        def _():
            carry_ref[...] = jnp.zeros_like(carry_ref)

        row = lax.broadcasted_iota(jnp.int32, (8, width), 0)
        ntile = rb // 8

        def tile(t, mu_next):
            r0 = pl.multiple_of((ntile - 1 - t) * 8, 8)
            ca, dh_t = a_ref[pl.ds(r0, 8), :], dh_ref[pl.ds(r0, 8), :]
            cbv = ca * dh_t
            for s in (1, 2, 4):
                m = row < 8 - s
                cbv = jnp.where(m, ca * pltpu.roll(cbv, 8 - s, 0) + cbv, cbv)
                ca = jnp.where(m, ca * pltpu.roll(ca, 8 - s, 0), ca)
            mu = cbv + ca * mu_next
            lam = dh_t + jnp.where(row == 7, mu_next, pltpu.roll(mu, 7, 0))
            db_ref[pl.ds(r0, 8), :] = lam
            da_ref[pl.ds(r0, 8), :] = lam * hp_ref[pl.ds(r0, 8), :]
            return _row_pick(mu, 0)

        carry_ref[0:1, :] = lax.fori_loop(0, ntile, tile, carry_ref[0:1, :])

    spec = pl.BlockSpec((None, rb, width), lambda bi, i: (bi, nblk - 1 - i, 0))
    return pl.pallas_call(body, name="lru_scan_bwd", grid=(bsz, nblk), in_specs=[spec] * 3, out_specs=[spec, spec],
                          out_shape=[jax.ShapeDtypeStruct(a.shape, f32)] * 2,
                          scratch_shapes=[pltpu.VMEM((8, width), f32)], compiler_params=_params())(a, hp, dh)


def final_loss(x, g, target):
    n, d = x.shape
    r = min(256, n)

    def body(x_ref, g_ref, t_ref, loss_ref, dx_ref, dg_ref):
        @pl.when(pl.program_id(0) == 0)
        def _():
            loss_ref[...] = jnp.zeros_like(loss_ref)
            dg_ref[...] = jnp.zeros_like(dg_ref)

        tgt = t_ref[...]

        def loss_fn(xv, gv):
            y = f_norm([xv], [gv])[0]
            return 0.5 * jnp.sum(jnp.mean(jnp.square(y - tgt), axis=-1, keepdims=True), axis=0, keepdims=True)

        val, vjp = jax.vjp(loss_fn, x_ref[...], g_ref[...])
        dx, dg = vjp(jnp.ones_like(val))
        loss_ref[...] += val
        dx_ref[...] = dx
        dg_ref[...] += dg

    row = pl.BlockSpec((r, d), lambda i: (i, 0))
    return pl.pallas_call(
        body, name="final_loss", grid=(n // r,), in_specs=[row, pl.BlockSpec((1, d), lambda i: (0, 0)), row],
        out_specs=[pl.BlockSpec((8, LANE), lambda i: (0, 0)), row, pl.BlockSpec((1, d), lambda i: (0, 0))],
        out_shape=[jax.ShapeDtypeStruct((8, LANE), f32), jax.ShapeDtypeStruct((n, d), f32), jax.ShapeDtypeStruct((1, d), f32)],
        compiler_params=_params())(x, g, target)


def exchange(x, scatter, name):
    blk = x.shape[1:] if scatter else x.shape

    def body(x_ref, o_ref, send_sems, recv_sems, local_sem):
        mx, my, mc = lax.axis_index("x"), lax.axis_index("y"), lax.axis_index("c")
        me = 4 * mx + 2 * my + mc
        src_own = x_ref.at[me] if scatter else x_ref
        local = pltpu.make_async_copy(src_own, o_ref.at[me], local_sem)
        local.start()
        copies = []
        for k in range(1, N_DEV):
            px, py, pc = (mx + (k >> 2)) % 2, (my + ((k >> 1) & 1)) % 2, (mc + (k & 1)) % 2
            src = x_ref.at[4 * px + 2 * py + pc] if scatter else x_ref
            cp = pltpu.make_async_remote_copy(
                src_ref=src, dst_ref=o_ref.at[me], send_sem=send_sems.at[k - 1], recv_sem=recv_sems.at[k - 1],
                device_id=(px, py, pc), device_id_type=pl.DeviceIdType.MESH)
            cp.start()
            copies.append(cp)
        for cp in copies:
            cp.wait()
        local.wait()

    return pl.pallas_call(
        body, name=name, in_specs=[pl.BlockSpec(memory_space=pl.ANY)], out_specs=pl.BlockSpec(memory_space=pl.ANY),
        out_shape=jax.ShapeDtypeStruct((N_DEV,) + tuple(blk), x.dtype),
        scratch_shapes=[pltpu.SemaphoreType.DMA((N_DEV - 1,)), pltpu.SemaphoreType.DMA((N_DEV - 1,)), pltpu.SemaphoreType.DMA],
        compiler_params=pltpu.CompilerParams(has_side_effects=True))(x)


_HBM = pl.BlockSpec(memory_space=pltpu.HBM)
_SEM = pl.BlockSpec(memory_space=pltpu.SEMAPHORE)
_EFFECT = pltpu.SideEffectType.DATAFLOW_SIDE_EFFECTING


def _peer(k):
    mx, my, mc = lax.axis_index("x"), lax.axis_index("y"), lax.axis_index("c")
    px, py, pc = (mx + (k >> 2)) % 2, (my + ((k >> 1) & 1)) % 2, (mc + (k & 1)) % 2
    return (px, py, pc), 4 * px + 2 * py + pc


def _peer_copy(k, i, x_ref, land_ref, send_sems, recv_sems, scatter):
    me = 4 * lax.axis_index("x") + 2 * lax.axis_index("y") + lax.axis_index("c")
    dev, slot = _peer(k)
    sem = i * (N_DEV - 1) + k - 1
    return pltpu.make_async_remote_copy(
        src_ref=x_ref.at[slot] if scatter else x_ref, dst_ref=land_ref.at[me], send_sem=send_sems.at[sem],
        recv_sem=recv_sems.at[sem], device_id=dev, device_id_type=pl.DeviceIdType.MESH)


def exchange_start(xs, scatters, name):
    nx = len(xs)
    lands = [lax.empty((N_DEV,) + tuple(x.shape[1:] if sc else x.shape), x.dtype) for x, sc in zip(xs, scatters)]
    nsem = nx * (N_DEV - 1)

    def body(*refs):
        x_refs, land_refs = refs[:nx], refs[nx:2 * nx]
        send_sems, recv_sems = refs[2 * nx:2 * nx + 2]
        token = refs[-1]
        for i in range(nx):
            for k in range(1, N_DEV):
                _peer_copy(k, i, x_refs[i], land_refs[i], send_sems, recv_sems, scatters[i]).start()
        token[...] = jnp.zeros_like(token)

    hbm = lambda a: pltpu.HBM(a.shape, a.dtype)
    res = pl.pallas_call(
        body, name=name, in_specs=(_HBM,) * (2 * nx),
        out_specs=(_SEM, _SEM) + (_HBM,) * (2 * nx) + (pl.BlockSpec(memory_space=pltpu.VMEM),),
        input_output_aliases={i: 2 + i for i in range(2 * nx)},
        out_shape=(pltpu.SemaphoreType.DMA((nsem,)), pltpu.SemaphoreType.DMA((nsem,)), *[hbm(a) for a in xs],
                   *[hbm(a) for a in lands], jax.ShapeDtypeStruct((8, LANE), f32)),
        compiler_params=pltpu.CompilerParams(has_side_effects=_EFFECT),
    )(*[pltpu.with_memory_space_constraint(a, pltpu.HBM) for a in list(xs) + lands])
    return (res[0], res[1], list(res[2:2 + nx]), list(res[2 + nx:2 + 2 * nx]), tuple(scatters)), res[-1]


def exchange_wait(started, after, name):
    send_sems, recv_sems, x_thrus, land_thrus, scatters = started
    nx = len(x_thrus)

    def body(*refs):
        x_refs, land_refs = refs[:nx], refs[nx:2 * nx]
        send_sems, recv_sems = refs[2 * nx:2 * nx + 2]
        for i in range(nx):
            for k in range(1, N_DEV):
                cp = _peer_copy(k, i, x_refs[i], land_refs[i], send_sems, recv_sems, scatters[i])
                cp.wait_send()
                cp.wait_recv()

    hbm = lambda a: pltpu.HBM(a.shape, a.dtype)
    res = pl.pallas_call(
        body, name=name, in_specs=(_HBM,) * (2 * nx) + (_SEM, _SEM, pl.BlockSpec(memory_space=pl.ANY)),
        out_specs=(_HBM,) * (2 * nx), input_output_aliases={i: i for i in range(2 * nx)},
        out_shape=tuple(hbm(a) for a in list(x_thrus) + list(land_thrus)),
        compiler_params=pltpu.CompilerParams(has_side_effects=_EFFECT),
    )(*x_thrus, *land_thrus, send_sems, recv_sems, after)
    return list(res[:nx]), list(res[nx:])


def sum_slots(x, name, own=None):
    _, rows_total, cols = x.shape
    row_bytes = N_DEV * ((cols + LANE - 1) // LANE) * LANE * x.dtype.itemsize
    r = _pick_rows(rows_total, max(16, (4 * 1024 * 1024) // row_bytes // 16 * 16))
    if own is not None:
        def body_own(x_ref, own_ref, o_ref):
            me = 4 * lax.axis_index("x") + 2 * lax.axis_index("y") + lax.axis_index("c")
            acc = None
            for s in range(N_DEV):
                v = jnp.where(me == s, own_ref[...], x_ref[s]).astype(f32)
                acc = v if acc is None else acc + v
            o_ref[...] = acc

        return pl.pallas_call(body_own, name=name, grid=(rows_total // r,),
                              in_specs=[pl.BlockSpec((N_DEV, r, cols), lambda i: (0, i, 0)), pl.BlockSpec((r, cols), lambda i: (i, 0))],
                              out_specs=pl.BlockSpec((r, cols), lambda i: (i, 0)),
                              out_shape=jax.ShapeDtypeStruct((rows_total, cols), f32), compiler_params=_params())(x, own)

    def body(x_ref, o_ref):
        acc = x_ref[0].astype(f32)
        for s in range(1, N_DEV):
            acc = acc + x_ref[s].astype(f32)
        o_ref[...] = acc

    return pl.pallas_call(body, name=name, grid=(rows_total // r,),
                          in_specs=[pl.BlockSpec((N_DEV, r, cols), lambda i: (0, i, 0))],
                          out_specs=pl.BlockSpec((r, cols), lambda i: (i, 0)),
                          out_shape=jax.ShapeDtypeStruct((rows_total, cols), f32), compiler_params=_params())(x)


def _pick_rows(total, pref):
    best = None
    for d in range(16, min(total, pref) + 1, 16):
        if total % d == 0:
            best = d
    return best if best is not None else total


def adamw(w, g, m, v, name):
    shape = w.shape
    if w.ndim == 1:
        w2, g2, m2, v2 = (t.reshape(1, -1) for t in (w, g, m, v))
    else:
        w2, g2, m2, v2 = (t.reshape(-1, shape[-1]) for t in (w, g, m, v))
    rows_total, cols = w2.shape
    r = _pick_rows(rows_total, max(16, (512 * 1024) // max(cols, 1) // 16 * 16))
    c1, c2 = 1.0 / (1.0 - ADAM_B1 ** ADAM_STEP), 1.0 / (1.0 - ADAM_B2 ** ADAM_STEP)

    def body(w_ref, g_ref, m_ref, v_ref, d_ref, nm_ref, nv_ref):
        gv = g_ref[...]
        nm = ADAM_B1 * m_ref[...] + (1.0 - ADAM_B1) * gv
        nv = ADAM_B2 * v_ref[...] + (1.0 - ADAM_B2) * jnp.square(gv)
        d_ref[...] = -ADAM_LR * ((nm * c1) / (jnp.sqrt(nv * c2) + ADAM_EPS) + ADAM_WD * w_ref[...])
        nm_ref[...] = nm
        nv_ref[...] = nv

    spec = pl.BlockSpec((r, cols), lambda i: (i, 0))
    outs = pl.pallas_call(body, name=name, grid=(rows_total // r,), in_specs=[spec] * 4, out_specs=[spec] * 3,
                          out_shape=[jax.ShapeDtypeStruct((rows_total, cols), f32)] * 3, compiler_params=_params())(w2, g2, m2, v2)
    return tuple(o.reshape(shape) for o in outs)


def _const(j):
    return lambda _: j


def _layer_fwd(x, wl, cos, sin, bsz, seq):
    n = x.shape[0]
    sv = {"x_in": x}
    row1 = lambda a: (a, (1, a.shape[1]), lambda j: (0, 0))
    h = rowmap(f_norm, [(x, D_MODEL, 0)], [row1(wl["attn_norm"])], [(D_MODEL, f32)], 1, "norm_fwd")[0]
    u = mm(h, wl["w_in"], "nn", "mm_in")
    sv["h"], sv["u"] = h, u
    u3 = u.reshape(bsz, seq, U_PAD)

    qkv = []
    for kind in range(3):
        cw = (wl["dn_conv_w"], (4, LANE), functools.partial(lambda j, kind: (0, 4 * kind + j), kind=kind))
        qkv.append(seqmap(functools.partial(f_dn_pre, kind), [(u3, U_QKV // LANE + 4 * kind)], [cw], 1, 4, "dn_pre%d" % kind)[0])
    gb = rowmap(f_dn_gates, [(u, LANE, U_AB // LANE)], [(wl["dn_gate_p"], (8, LANE), lambda j: (0, 0))], [(LANE, f32)], 1,
                "dn_gates")[0]
    gb3 = gb.reshape(bsz, seq, LANE)
    crow = CHUNK * CHUNKS_PER_STEP
    dn_in = [(t.reshape(n, 512), 512, 0) for t in qkv] + [(gb, LANE, 0)]
    prep_a = rowmap(dn_prep, dn_in, [], [(512, f32)] * 5 + [(LANE, f32)], 1, "dn_prep", rows=crow)
    prep_a = [t.reshape(bsz, seq, t.shape[1]) for t in prep_a]
    o_a, ck_a = chunk_scan(dn_step, prep_a, (DN_HEADS, DN_DK, DN_DK), 512, "dn_scan")
    y_a = rowmap(f_dn_post, [(o_a.reshape(n, 512), LANE, 0), (u, LANE, U_Z // LANE)],
                 [(wl["dn_norm_w"], (1, LANE), lambda j: (0, 0))], [(LANE, f32)], 4, "dn_post")[0]
    sv.update(dn_in=dn_in, prep_a=prep_a, o_a=o_a, ck_a=ck_a, y_a=y_a)

    q_b, k_b = rowmap(f_ret_pre, [(u, 256, U_RQ // 256), (u, 256, U_RK // 256), (cos, 256, 0), (sin, 256, 0)], [],
                      [(256, f32), (256, f32)], 1, "ret_pre")
    q_b3, k_b3 = q_b.reshape(bsz, seq, 256), k_b.reshape(bsz, seq, 256)
    v_b3 = lax.slice_in_dim(u3, U_RV, U_RV + 512, axis=2)
    ret_in = [(q_b, 256, 0), (k_b, 256, 0), (u, 512, U_RV // 512)]
    inner = rowmap(ret_prep, ret_in, [], [(512, f32)], 1, "ret_prep", rows=crow)[0]
    ret_seq = [q_b3, k_b3, v_b3, inner.reshape(bsz, seq, 512)]
    o_b, ck_b = chunk_scan(ret_step, ret_seq, (RET_HEADS, 256, RET_DV), 512, "ret_scan")
    y_b = rowmap(f_ret_post, [(o_b.reshape(n, 512), LANE, 0), (u, LANE, U_RG // LANE)], [], [(LANE, f32)], 4, "ret_post")[0]
    sv.update(ret_in=ret_in, ret_seq=ret_seq, o_b=o_b, ck_b=ck_b, y_b=y_b)

    lru_params = _lru_params(wl)
    a_c, b_c = seqmap(f_lru_pre, [(u3, U_CX // LANE)], lru_params, 2, 4, "lru_pre")
    h_c, hp_c = lru_scan(a_c, b_c)
    y_c = rowmap(f_lru_post, [(h_c.reshape(n, 512), 512, 0), (u, 512, U_CG // 512)], [], [(512, f32)], 1, "lru_post")[0]
    sv.update(a_c=a_c, hp_c=hp_c, h_c=h_c, y_c=y_c)

    br = [mm(y, wl["w_branch"][i], "nn", "mm_branch") for i, y in enumerate((y_a, y_b, y_c))]
    merged = rowmap(f_merge, [(u, D_MODEL, i) for i in range(3)] + [(b, D_MODEL, 0) for b in br], [], [(D_MODEL, f32)], 1,
                    "merge")[0]
    x_mid = mm(merged, wl["w_out"], "nn", "mm_out", add=x)
    sv.update(br=br, merged=merged, x_mid=x_mid)

    h2 = rowmap(f_norm, [(x_mid, D_MODEL, 0)], [row1(wl["ffn_norm"])], [(D_MODEL, f32)], 1, "norm_fwd")[0]
    up = mm(h2, wl["w_up"], "nn", "mm_up")
    act = seqmap(f_ffn_mid, [(up.reshape(bsz, seq, 2 * D_FF), 0), (up.reshape(bsz, seq, 2 * D_FF), D_FF // LANE)],
                 _ffn_params(wl), 1, D_FF // LANE, "ffn_mid")[0]
    act = act.reshape(n, D_FF)
    x_out = mm(act, wl["w_down"], "nn", "mm_down", add=x_mid)
    sv.update(h2=h2, up=up, act=act)
    return x_out, sv


def _lru_params(wl):
    col = lambda a: (a, (a.shape[0], LANE), lambda j: (0, j))
    blk = lambda a: (a, (None, LANE, LANE), lambda j: (j, 0, 0))
    return [col(wl["lru_conv_w"]), col(wl["lru_conv_b"]), blk(wl["lru_wa"]), col(wl["lru_ba"]), blk(wl["lru_wx"]),
            col(wl["lru_bx"]), col(wl["lru_lambda"])]


def _ffn_params(wl):
    nb = D_FF // LANE
    return [(wl["ffn_conv_w"], (3, LANE), lambda j: (0, j)), (wl["ffn_conv_w"], (3, LANE), lambda j: (0, nb + j)),
            (wl["ffn_conv_b"], (1, LANE), lambda j: (0, j)), (wl["ffn_conv_b"], (1, LANE), lambda j: (0, nb + j))]


def _layer_bwd(dx, sv, wl, cos, sin, bsz, seq):
    n = dx.shape[0]
    gr = {}
    u, x_in, x_mid = sv["u"], sv["x_in"], sv["x_mid"]
    u3 = u.reshape(bsz, seq, U_PAD)
    row1 = lambda a: (a, (1, a.shape[1]), lambda j: (0, 0))

    d_act = mm(dx, wl["w_down"], "nt", "mm_down_dx")
    gr["w_down"] = mm(sv["act"], dx, "tn", "mm_down_dw")
    up3 = sv["up"].reshape(bsz, seq, 2 * D_FF)
    (d_gate, d_val), dps = seqmap_bwd(f_ffn_mid, [(up3, 0), (up3, D_FF // LANE)], _ffn_params(wl),
                                      [d_act.reshape(bsz, seq, D_FF)], D_FF // LANE, "ffn_mid_bwd")
    gr["ffn_conv_w"] = jnp.concatenate([_cols(dps[0]), _cols(dps[1])], axis=1)
    gr["ffn_conv_b"] = jnp.concatenate([_cols(dps[2]), _cols(dps[3])], axis=1)
    d_up = jnp.concatenate([d_gate, d_val], axis=2).reshape(n, 2 * D_FF)
    d_h2 = mm(d_up, wl["w_up"], "nt", "mm_up_dx")
    gr["w_up"] = mm(sv["h2"], d_up, "tn", "mm_up_dw")
    (dx_mid,), (dg,) = rowmap_bwd(f_norm, [(x_mid, D_MODEL, 0)], [row1(wl["ffn_norm"])], [d_h2], 1, "norm_bwd", add=[dx])
    gr["ffn_norm"] = dg[0, 0]

    d_merged = mm(dx_mid, wl["w_out"], "nt", "mm_out_dx")
    gr["w_out"] = mm(sv["merged"], dx_mid, "tn", "mm_out_dw")
    dm, _ = rowmap_bwd(f_merge, [(u, D_MODEL, i) for i in range(3)] + [(b, D_MODEL, 0) for b in sv["br"]], [], [d_merged], 1,
                       "merge_bwd")
    d_gl, d_br = dm[:3], dm[3:]
    ys = (sv["y_a"], sv["y_b"], sv["y_c"])
    d_ys = [mm(d_br[i], wl["w_branch"][i], "nt", "mm_branch_dx") for i in range(3)]
    gr["w_branch"] = jnp.stack([mm(ys[i], d_br[i], "tn", "mm_branch_dw") for i in range(3)])

    (d_hc, d_cg), _ = rowmap_bwd(f_lru_post, [(sv["h_c"].reshape(n, 512), 512, 0), (u, 512, U_CG // 512)], [], [d_ys[2]], 1,
                                 "lru_post_bwd")
    d_a, d_b = lru_scan_bwd(sv["a_c"], sv["hp_c"], d_hc.reshape(bsz, seq, 512))
    (d_cx,), dps = seqmap_bwd(f_lru_pre, [(u3, U_CX // LANE)], _lru_params(wl), [d_a, d_b], 4, "lru_pre_bwd")
    gr["lru_conv_w"], gr["lru_conv_b"] = _cols(dps[0]), _cols(dps[1])
    gr["lru_wa"], gr["lru_ba"], gr["lru_wx"], gr["lru_bx"] = dps[2], dps[3][:, 0], dps[4], dps[5][:, 0]
    gr["lru_lambda"] = _cols(dps[6])

    (d_ob, d_rg), _ = rowmap_bwd(f_ret_post, [(sv["o_b"].reshape(n, 512), LANE, 0), (u, LANE, U_RG // LANE)], [], [d_ys[1]], 4,
                                 "ret_post_bwd")
    crow = CHUNK * CHUNKS_PER_STEP
    d_ret = chunk_scan_bwd(ret_step, sv["ret_seq"], sv["ck_b"], d_ob.reshape(bsz, seq, 512), "ret_scan_bwd")
    d_ret = [t.reshape(n, t.shape[2]) for t in d_ret]
    (d_qb, d_kb, d_rv), _ = rowmap_bwd(ret_prep, sv["ret_in"], [], [d_ret[3]], 1, "ret_prep_bwd", rows=crow,
                                       add=d_ret[:3])
    dr, _ = rowmap_bwd(f_ret_pre, [(u, 256, U_RQ // 256), (u, 256, U_RK // 256), (cos, 256, 0), (sin, 256, 0)], [],
                       [d_qb, d_kb], 1, "ret_pre_bwd")
    d_rq, d_rk = dr[0], dr[1]

    (d_oa, d_z), (dnw,) = rowmap_bwd(f_dn_post, [(sv["o_a"].reshape(n, 512), LANE, 0), (u, LANE, U_Z // LANE)],
                                     [(wl["dn_norm_w"], (1, LANE), lambda j: (0, 0))], [d_ys[0]], 4, "dn_post_bwd")
    gr["dn_norm_w"] = jnp.sum(dnw, axis=0)[0]
    d_prep = chunk_scan_bwd(dn_step, sv["prep_a"], sv["ck_a"], d_oa.reshape(bsz, seq, 512), "dn_scan_bwd")
    (d_q, d_k, d_v, d_gb), _ = rowmap_bwd(dn_prep, sv["dn_in"], [], [t.reshape(n, t.shape[2]) for t in d_prep], 1,
                                          "dn_prep_bwd", rows=crow)
    d_q, d_k, d_v = (t.reshape(bsz, seq, 512) for t in (d_q, d_k, d_v))
    (d_ab,), (dgp,) = rowmap_bwd(f_dn_gates, [(u, LANE, U_AB // LANE)], [(wl["dn_gate_p"], (8, LANE), lambda j: (0, 0))],
                                 [d_gb], 1, "dn_gates_bwd")
    gr["dn_a_log"], gr["dn_dt_bias"] = dgp[0, 0, :DN_HEADS], dgp[0, 1, :DN_HEADS]
    d_qkv, d_cw = [], []
    for kind, d_t in enumerate((d_q, d_k, d_v)):
        cw = (wl["dn_conv_w"], (4, LANE), functools.partial(lambda j, kind: (0, 4 * kind + j), kind=kind))
        (d_in,), (dcw,) = seqmap_bwd(functools.partial(f_dn_pre, kind), [(u3, U_QKV // LANE + 4 * kind)], [cw], [d_t], 4,
                                     "dn_pre%d_bwd" % kind)
        d_qkv.append(d_in.reshape(n, 512))
        d_cw.append(_cols(dcw))
    gr["dn_conv_w"] = jnp.concatenate(d_cw, axis=1)

    pad = jnp.zeros((n, U_PAD - U_AB - LANE), f32)
    du = jnp.concatenate(list(d_gl) + d_qkv + [d_rv.reshape(n, 512), d_rg, d_z, d_cx.reshape(n, 512), d_cg, d_rq, d_rk, d_ab, pad],
                         axis=1)
    d_h = mm(du, wl["w_in"], "nt", "mm_in_dx")
    gr["w_in"] = mm(sv["h"], du, "tn", "mm_in_dw")
    (dx_in,), (dg,) = rowmap_bwd(f_norm, [(x_in, D_MODEL, 0)], [row1(wl["attn_norm"])], [d_h], 1, "norm_bwd", add=[dx_mid])
    gr["attn_norm"] = dg[0, 0]
    return dx_in, gr


def _cols(dp):
    ncol, p, _ = dp.shape
    return jnp.transpose(dp, (1, 0, 2)).reshape(p, ncol * LANE)


def _pad_w_in(w):
    segs = sorted(_IN_SEGS, key=lambda s: s[2])
    parts = [lax.slice_in_dim(w, src, src + width, axis=1) for src, width, _ in segs]
    end = segs[-1][2] + segs[-1][1]
    return jnp.concatenate(parts + [jnp.zeros((w.shape[0], U_PAD - end), w.dtype)], axis=1)


def _unpad_w_in(wp):
    return jnp.concatenate([lax.slice_in_dim(wp, dst, dst + width, axis=1) for _, width, dst in _IN_SEGS], axis=1)


def _rope_tables(positions):
    half = RET_DK // 2
    inv = ROPE_BASE ** (-jnp.arange(half, dtype=f32) / half)
    ang = positions.astype(f32).reshape(-1, 1) * inv
    cos, sin = jnp.cos(ang), jnp.sin(ang)
    return jnp.tile(cos, (1, 2 * RET_HEADS)), jnp.tile(sin, (1, 2 * RET_HEADS))


def _layer_weights(lw):
    wl = {}
    wl["w_in"] = _pad_w_in(lw["w_in"])
    for k in ("w_branch", "w_out", "w_up", "w_down", "dn_conv_w", "lru_conv_w", "ffn_conv_w", "lru_wa", "lru_wx"):
        wl[k] = lw[k]
    for k in ("attn_norm", "ffn_norm", "dn_norm_w", "lru_conv_b", "lru_lambda", "ffn_conv_b", "lru_ba", "lru_bx"):
        wl[k] = lw[k].reshape(1, -1)
    gp = jnp.zeros((8, LANE), f32)
    wl["dn_gate_p"] = gp.at[0, :DN_HEADS].set(lw["dn_a_log"]).at[1, :DN_HEADS].set(lw["dn_dt_bias"])
    return wl


def forward_backward(x, positions, target, layer_weights, final_norm, on_layer_grads):
    bsz, seq, d = x.shape
    n = bsz * seq
    cos, sin = _rope_tables(positions)
    xs = x.reshape(n, d)
    saved, wls = [], []
    for layer in range(DEPTH):
        wls.append(_layer_weights(layer_weights(layer, xs)))
        xs, sv = _layer_fwd(xs, wls[layer], cos, sin, bsz, seq)
        saved.append(sv)
    loss, dx, d_final = final_loss(xs, final_norm.reshape(1, d), target.reshape(n, d))
    for layer in reversed(range(DEPTH)):
        dx, gr = _layer_bwd(dx, saved[layer], wls[layer], cos, sin, bsz, seq)
        gr["w_in"] = _unpad_w_in(gr["w_in"])
        gr["lru_conv_b"], gr["lru_lambda"], gr["ffn_conv_b"] = gr["lru_conv_b"][0], gr["lru_lambda"][0], gr["ffn_conv_b"][0]
        token = on_layer_grads(layer, gr, loss[0, 0], d_final[0])
        if token is not None and layer > 0:
            wls[layer - 1]["ffn_norm"] = wls[layer - 1]["ffn_norm"] + token[0:1, 0:1]
    return dx.reshape(bsz, seq, d)


def local_step(x, positions, target, full):
    grads, head = {}, {}

    def on_layer_grads(layer, gr, loss, d_final):
        grads[layer] = gr
        head.update(loss=loss, d_final=d_final)

    gx = forward_backward(x, positions, target, lambda layer, _: {k: a[layer] for k, a in full.items() if k != "final_norm"},
                          full["final_norm"], on_layer_grads)
    stacked = {k: jnp.stack([grads[layer][k] for layer in range(DEPTH)]) for k in grads[0]}
    stacked["final_norm"] = head["d_final"]
    return head["loss"], gx, stacked


BIG = (("w_in", 2), ("w_branch", 3), ("w_out", 1), ("w_up", 2), ("w_down", 1))
SMALL_SHARDED = (("dn_conv_w", 2), ("lru_conv_w", 2), ("ffn_conv_w", 2))
REPLICATED = ("attn_norm", "dn_a_log", "dn_dt_bias", "dn_norm_w", "lru_conv_b", "lru_wa", "lru_ba", "lru_wx", "lru_bx",
              "lru_lambda", "ffn_norm", "ffn_conv_b", "final_norm")
WEIGHTS = ("attn_norm", "w_in", "dn_conv_w", "dn_a_log", "dn_dt_bias", "dn_norm_w", "lru_conv_w", "lru_conv_b", "lru_wa",
           "lru_ba", "lru_wx", "lru_bx", "lru_lambda", "w_branch", "w_out", "ffn_norm", "w_up", "ffn_conv_w", "ffn_conv_b",
           "w_down", "final_norm")


def _pack(arrs, dtype, align=16 * LANE):
    flat = jnp.concatenate([a.reshape(-1).astype(dtype) for a in arrs])
    pad = (-flat.shape[0]) % align
    return jnp.pad(flat, (0, pad)).reshape(-1, LANE)


def _unpack(rows, shapes):
    flat = rows.reshape(-1)
    out, pos = [], 0
    for shp in shapes:
        size = math.prod(shp)
        out.append(lax.slice_in_dim(flat, pos, pos + size).reshape(shp))
        pos += size
    return out


def _gather_weights(w, names_axes, dtype, name):
    shards = [w[k] for k, _ in names_axes]
    got = exchange(_pack(shards, dtype), False, name)
    per_dev = [_unpack(got[p], [s.shape for s in shards]) for p in range(N_DEV)]
    return {k: jnp.concatenate([per_dev[p][i] for p in range(N_DEV)], axis=ax) for i, (k, ax) in enumerate(names_axes)}


def _split8(a, axis):
    size = a.shape[axis] // N_DEV
    return [lax.slice_in_dim(a, p * size, (p + 1) * size, axis=axis) for p in range(N_DEV)]


def kernel(x, positions, attn_norm, w_in, dn_conv_w, dn_a_log, dn_dt_bias, dn_norm_w, lru_conv_w, lru_conv_b, lru_wa, lru_ba, lru_wx, lru_bx, lru_lambda, w_branch, w_out, ffn_norm, w_up, ffn_conv_w, ffn_conv_b, w_down, final_norm, loss_target, m_attn_norm, m_w_in, m_dn_conv_w, m_dn_a_log, m_dn_dt_bias, m_dn_norm_w, m_lru_conv_w, m_lru_conv_b, m_lru_wa, m_lru_ba, m_lru_wx, m_lru_bx, m_lru_lambda, m_w_branch, m_w_out, m_ffn_norm, m_w_up, m_ffn_conv_w, m_ffn_conv_b, m_w_down, m_final_norm, v_attn_norm, v_w_in, v_dn_conv_w, v_dn_a_log, v_dn_dt_bias, v_dn_norm_w, v_lru_conv_w, v_lru_conv_b, v_lru_wa, v_lru_ba, v_lru_wx, v_lru_bx, v_lru_lambda, v_w_branch, v_w_out, v_ffn_norm, v_w_up, v_ffn_conv_w, v_ffn_conv_b, v_w_down, v_final_norm):
    w = dict(attn_norm=attn_norm, w_in=w_in, dn_conv_w=dn_conv_w, dn_a_log=dn_a_log, dn_dt_bias=dn_dt_bias, dn_norm_w=dn_norm_w,
             lru_conv_w=lru_conv_w, lru_conv_b=lru_conv_b, lru_wa=lru_wa, lru_ba=lru_ba, lru_wx=lru_wx, lru_bx=lru_bx,
             lru_lambda=lru_lambda, w_branch=w_branch, w_out=w_out, ffn_norm=ffn_norm, w_up=w_up, ffn_conv_w=ffn_conv_w,
             ffn_conv_b=ffn_conv_b, w_down=w_down, final_norm=final_norm)
    m = dict(attn_norm=m_attn_norm, w_in=m_w_in, dn_conv_w=m_dn_conv_w, dn_a_log=m_dn_a_log, dn_dt_bias=m_dn_dt_bias,
             dn_norm_w=m_dn_norm_w, lru_conv_w=m_lru_conv_w, lru_conv_b=m_lru_conv_b, lru_wa=m_lru_wa, lru_ba=m_lru_ba,
             lru_wx=m_lru_wx, lru_bx=m_lru_bx, lru_lambda=m_lru_lambda, w_branch=m_w_branch, w_out=m_w_out, ffn_norm=m_ffn_norm,
             w_up=m_w_up, ffn_conv_w=m_ffn_conv_w, ffn_conv_b=m_ffn_conv_b, w_down=m_w_down, final_norm=m_final_norm)
    v = dict(attn_norm=v_attn_norm, w_in=v_w_in, dn_conv_w=v_dn_conv_w, dn_a_log=v_dn_a_log, dn_dt_bias=v_dn_dt_bias,
             dn_norm_w=v_dn_norm_w, lru_conv_w=v_lru_conv_w, lru_conv_b=v_lru_conv_b, lru_wa=v_lru_wa, lru_ba=v_lru_ba,
             lru_wx=v_lru_wx, lru_bx=v_lru_bx, lru_lambda=v_lru_lambda, w_branch=v_w_branch, w_out=v_w_out, ffn_norm=v_ffn_norm,
             w_up=v_w_up, ffn_conv_w=v_ffn_conv_w, ffn_conv_b=v_ffn_conv_b, w_down=v_w_down, final_norm=v_final_norm)

    me = 4 * lax.axis_index("x") + 2 * lax.axis_index("y") + lax.axis_index("c")
    small_full = _gather_weights(w, SMALL_SHARDED, f32, "gather_small")

    gathers, tokens = [], []
    for layer in range(DEPTH):
        started, token = exchange_start([w[k][layer].astype(bf16) for k, _ in BIG], [False] * len(BIG),
                                        "gather_big_start%d" % layer)
        gathers.append(started)
        tokens.append(token[0:1, 0:1])
    all_started = tokens[0] + tokens[1] + tokens[2] + tokens[3]

    def layer_weights(layer, x_in):
        own, lands = exchange_wait(gathers[layer], x_in, "gather_big_wait%d" % layer)
        lw = {k: jnp.concatenate([jnp.where(me == p, own[i], lands[i][p]) for p in range(N_DEV)], axis=ax - 1)
              for i, (k, ax) in enumerate(BIG)}
        lw.update({k: small_full[k][layer] for k, _ in SMALL_SHARDED})
        lw.update({k: w[k][layer] for k in REPLICATED if k != "final_norm"})
        if layer == 0:
            lw["attn_norm"] = lw["attn_norm"] + all_started[0]
        return lw

    small_names = [k for k, _ in SMALL_SHARDED] + [k for k in REPLICATED if k != "final_norm"]
    scatters, small_shapes = {}, {}

    def on_layer_grads(layer, gr, loss_part, d_final):
        big = [jnp.stack(_split8(gr[k], ax - 1)).astype(bf16) for k, ax in BIG]
        small = [gr[k] for k in small_names] + ([loss_part.reshape(1), d_final] if layer == DEPTH - 1 else [])
        small_shapes[layer] = [a.shape for a in small]
        scatters[layer], token = exchange_start(big + [_pack(small, f32)], [True] * len(BIG) + [False],
                                                "scatter_grads_start%d" % layer)
        return token

    grad_x = forward_backward(x, positions, loss_target, layer_weights, final_norm, on_layer_grads)

    big_sums, small_sums = {}, {}
    for layer in reversed(range(DEPTH)):
        own, lands = exchange_wait(scatters[layer], grad_x, "scatter_grads_wait%d" % layer)
        per_weight = []
        for i, (k, _) in enumerate(BIG):
            shard = w[k].shape[1:]
            mine = lax.dynamic_index_in_dim(own[i], me, axis=0, keepdims=False).reshape(-1, shard[-1])
            per_weight.append(sum_slots(lands[i].reshape(N_DEV, -1, shard[-1]), "sum_" + k, own=mine).reshape(shard))
        big_sums[layer] = per_weight
        small_sums[layer] = _unpack(sum_slots(lands[-1], "sum_small", own=own[-1]), small_shapes[layer])
    grads = {k: jnp.stack([big_sums[layer][i] for layer in range(DEPTH)]) for i, (k, _) in enumerate(BIG)}
    loss, grads["final_norm"] = small_sums[DEPTH - 1][len(small_names)][0], small_sums[DEPTH - 1][len(small_names) + 1]
    for i, k in enumerate(small_names):
        g = jnp.stack([small_sums[layer][i] for layer in range(DEPTH)])
        ax = dict(SMALL_SHARDED).get(k)
        if ax is None:
            grads[k] = g
        else:
            size = g.shape[ax] // N_DEV
            grads[k] = lax.dynamic_slice_in_dim(g, me * size, size, axis=ax)

    upd = {k: adamw(w[k], grads[k], m[k], v[k], "adamw_" + k) for k in WEIGHTS}
    return (loss, grad_x, *[grads[k] for k in WEIGHTS], *[upd[k][0] for k in WEIGHTS], *[upd[k][1] for k in WEIGHTS],
            *[upd[k][2] for k in WEIGHTS])
```

```python
import functools
import math

import jax
import jax.numpy as jnp
from jax import lax
from jax.experimental import pallas as pl
from jax.experimental.pallas import tpu as pltpu

f32 = jnp.float32
bf16 = jnp.bfloat16

D_MODEL = 1024
DEPTH = 4
CHUNK = 64
EPS = 1e-6
DN_HEADS, DN_DK = 4, 128
RET_HEADS, RET_DK, RET_DV = 4, 64, 128
ROPE_BASE = 10000.0
LRU_C = 8.0
D_FF = 2816
N_DEV = 8
LANE = 128
VMEM_LIMIT = 56 * 1024 * 1024

ADAM_LR, ADAM_B1, ADAM_B2, ADAM_EPS, ADAM_WD, ADAM_STEP = 0.001, 0.9, 0.999, 1e-8, 0.01, 10

U_GATES, U_QKV, U_RV, U_RG, U_Z, U_CX, U_CG, U_RQ, U_RK, U_AB = (
    0, 3072, 4608, 5120, 5632, 6144, 6656, 7168, 7424, 7680)
U_PAD = 8192
_IN_SEGS = ((0, 1536, U_QKV), (1536, 8, U_AB), (1544, 512, U_Z), (2056, 256, U_RQ), (2312, 256, U_RK),
            (2568, 512, U_RV), (3080, 512, U_RG), (3592, 512, U_CX), (4104, 512, U_CG), (4616, 3072, U_GATES))
N_IN = 7688


def _params():
    return pltpu.CompilerParams(vmem_limit_bytes=VMEM_LIMIT)


def _pick(dim, pref):
    best = None
    for d in range(LANE, min(dim, pref) + 1, LANE):
        if dim % d == 0:
            best = d
    return best if best is not None else dim


@functools.partial(jax.custom_vjp, nondiff_argnums=(1, 2))
def sroll(x, shift, axis):
    return pltpu.roll(x, shift, axis)


def _sroll_fwd(x, shift, axis):
    return pltpu.roll(x, shift, axis), None


def _sroll_bwd(shift, axis, _, g):
    n = g.shape[axis]
    return (pltpu.roll(g, (n - shift) % n, axis),)


sroll.defvjp(_sroll_fwd, _sroll_bwd)

_DIMS = {"nn": (((1,), (0,)), ((), ())), "nt": (((1,), (1,)), ((), ())), "tn": (((0,), (0,)), ((), ()))}


def _dg(a, b, dims):
    return lax.dot_general(a.astype(bf16), b.astype(bf16), _DIMS[dims], preferred_element_type=f32)


@functools.partial(jax.custom_vjp, nondiff_argnums=(2,))
def bdot(a, b, dims):
    return _dg(a, b, dims)


def _bdot_fwd(a, b, dims):
    return _dg(a, b, dims), (a.astype(bf16), b.astype(bf16))


def _bdot_bwd(dims, res, g):
    a, b = res
    if dims == "nn":
        return _dg(g, b, "nt"), _dg(a, g, "tn")
    if dims == "nt":
        return _dg(g, b, "nn"), _dg(g, a, "tn")
    return _dg(b, g, "nt"), _dg(a, g, "nn")


bdot.defvjp(_bdot_fwd, _bdot_bwd)


def _fdot(a, b, dims):
    return lax.dot_general(a, b, _DIMS[dims], precision=lax.Precision.HIGH, preferred_element_type=f32)


@jax.custom_vjp
def unit_lower_inv_all(mats):
    shape = mats[0].shape
    row = lax.broadcasted_iota(jnp.int32, shape, 0)
    col = lax.broadcasted_iota(jnp.int32, shape, 1)
    eye = jnp.where(row == col, 1.0, 0.0).astype(f32)
    n = [-a for a in mats]
    p = [eye + x for x in n]
    span = 2
    while span < shape[0]:
        n = [_fdot(x, x, "nn") for x in n]
        p = [y + _fdot(y, x, "nn") for y, x in zip(p, n)]
        span *= 2
    return p


def _uli_fwd(mats):
    x = unit_lower_inv_all(mats)
    return x, x


def _uli_bwd(xs, gs):
    t = [_fdot(x, g, "tn") for x, g in zip(xs, gs)]
    return ([-_fdot(y, x, "nt") for y, x in zip(t, xs)],)


unit_lower_inv_all.defvjp(_uli_fwd, _uli_bwd)


def cumsum_rows(x):
    rows = x.shape[0]
    row = lax.broadcasted_iota(jnp.int32, x.shape, 0)
    s = 1
    while s < rows:
        x = x + jnp.where(row >= s, sroll(x, s, 0), 0.0)
        s *= 2
    return x


def _expm1(x):
    return jnp.tanh(0.5 * x) * (jnp.exp(x) + 1.0)


def _lane_pick(x, lane):
    idx = lax.broadcasted_iota(jnp.int32, x.shape, 1)
    return jnp.sum(jnp.where(idx == lane, x, 0.0), axis=1, keepdims=True)


def _row_pick(x, r):
    idx = lax.broadcasted_iota(jnp.int32, x.shape, 0)
    return jnp.sum(jnp.where(idx == r, x, 0.0), axis=0, keepdims=True)


def _causal_conv(x, halo, w, width):
    xe = jnp.concatenate([halo, x], axis=0)
    acc = xe * w[width - 1:width]
    for k in range(width - 1):
        acc = acc + sroll(xe, width - 1 - k, 0) * w[k:k + 1]
    return acc[8:]


def f_norm(ins, ps):
    (x,), (g,) = ins, ps
    return [x * lax.rsqrt(jnp.mean(x * x, axis=-1, keepdims=True) + EPS) * g]


def f_dn_pre(kind, mains, halos, ps):
    y = _causal_conv(mains[0], halos[0], ps[0], 4)
    y = y * jax.nn.sigmoid(y)
    if kind < 2:
        y = y * lax.rsqrt(jnp.sum(y * y, axis=-1, keepdims=True) + EPS)
    if kind == 0:
        y = y * (DN_DK ** -0.5)
    return [y]


def f_dn_gates(ins, ps):
    (u,), (p,) = ins, ps
    lane = lax.broadcasted_iota(jnp.int32, u.shape, 1)
    g = -jnp.exp(p[0:1]) * jax.nn.softplus(u + p[1:2])
    beta = jax.nn.sigmoid(u)
    return [jnp.where(lane < 4, g, jnp.where(lane < 8, beta, 0.0))]


def f_dn_post(ins, ps):
    (o, z), (nw,) = ins, ps
    y = o * lax.rsqrt(jnp.mean(o * o, axis=-1, keepdims=True) + EPS) * nw
    return [y * (z * jax.nn.sigmoid(z))]


def _rot_half(t):
    lane = lax.broadcasted_iota(jnp.int32, t.shape, 1)
    width = t.shape[1]
    first = (lane % RET_DK) < (RET_DK // 2)
    return jnp.where(first, -sroll(t, width - RET_DK // 2, 1), sroll(t, RET_DK // 2, 1))


def f_ret_pre(ins, ps):
    q, k, cos, sin = ins
    qr = q * cos + _rot_half(q) * sin
    kr = (k * cos + _rot_half(k) * sin) * (RET_DK ** -0.5)
    return [qr, kr]


def f_ret_post(ins, ps):
    o, g = ins
    mu = jnp.mean(o, axis=-1, keepdims=True)
    var = jnp.mean(jnp.square(o - mu), axis=-1, keepdims=True)
    return [(o - mu) * lax.rsqrt(var + EPS) * (g * jax.nn.sigmoid(g))]


def f_lru_pre(mains, halos, ps):
    cw, cb, wa, ba, wx, bx, lam = ps
    xc = _causal_conv(mains[0], halos[0], cw, 4) + cb
    r = jax.nn.sigmoid(bdot(xc, wa, "nn") + ba)
    i = jax.nn.sigmoid(bdot(xc, wx, "nn") + bx)
    log_a = -LRU_C * r * jax.nn.softplus(-lam)
    a = jnp.exp(log_a)
    b = jnp.sqrt(-_expm1(2.0 * log_a)) * (i * xc)
    return [a, b]


def f_lru_post(ins, ps):
    h, g = ins
    return [h * jax.nn.gelu(g)]


def f_merge(ins, ps):
    g0, g1, g2, b0, b1, b2 = ins
    return [jax.nn.sigmoid(g0) * b0 + jax.nn.sigmoid(g1) * b1 + jax.nn.sigmoid(g2) * b2]


def f_ffn_mid(mains, halos, ps):
    cwg, cwv, cbg, cbv = ps
    gate = _causal_conv(mains[0], halos[0], cwg, 3) + cbg
    val = _causal_conv(mains[1], halos[1], cwv, 3) + cbv
    return [gate * jax.nn.sigmoid(gate) * val]


def mm(a, b, dims, name, add=None, dep=None, tm=1024, tn=1024, tk=1024):
    if dims == "tn":
        kdim, m = a.shape
        n = b.shape[1]
    else:
        m, kdim = a.shape
        n = b.shape[0] if dims == "nt" else b.shape[1]
    tm, tn, tk = _pick(m, tm), _pick(n, tn), _pick(kdim, tk)
    nk = kdim // tk
    a_spec = pl.BlockSpec((tk, tm), lambda i, j, k: (k, i)) if dims == "tn" else pl.BlockSpec((tm, tk), lambda i, j, k: (i, k))
    b_spec = pl.BlockSpec((tn, tk), lambda i, j, k: (j, k)) if dims == "nt" else pl.BlockSpec((tk, tn), lambda i, j, k: (k, j))
    o_spec = pl.BlockSpec((tm, tn), lambda i, j, k: (i, j))
    has_add, has_dep = add is not None, dep is not None

    def body(*refs):
        a_ref, b_ref = refs[:2]
        add_ref = refs[2] if has_add else None
        o_ref = refs[2 + has_add + has_dep]
        if nk == 1:
            prod = _dg(a_ref[...], b_ref[...], dims)
            o_ref[...] = prod + add_ref[...] if has_add else prod
            return
        acc_ref = refs[-1]
        k = pl.program_id(2)

        @pl.when(k == 0)
        def _():
            acc_ref[...] = jnp.zeros_like(acc_ref)

        acc_ref[...] += _dg(a_ref[...], b_ref[...], dims)

        @pl.when(k == nk - 1)
        def _():
            o_ref[...] = acc_ref[...] + add_ref[...] if has_add else acc_ref[...]

    args = [a, b] + ([add] if has_add else []) + ([dep] if has_dep else [])
    in_specs = [a_spec, b_spec] + ([o_spec] if has_add else [])
    in_specs += [pl.BlockSpec((8, LANE), lambda i, j, k: (0, 0))] if has_dep else []
    return pl.pallas_call(
        body, name=name, grid=(m // tm, n // tn, nk), in_specs=in_specs, out_specs=o_spec,
        out_shape=jax.ShapeDtypeStruct((m, n), f32), scratch_shapes=[pltpu.VMEM((tm, tn), f32)] if nk > 1 else [],
        compiler_params=_params())(*args)


def rowmap(fn, ins, params, outs, ncol, name, rows=256):
    n = ins[0][0].shape[0]
    r = min(rows, n)
    nin, npar = len(ins), len(params)

    def body(*refs):
        vals = [x[...] for x in refs[:nin]]
        pv = [p[...] for p in refs[nin:nin + npar]]
        for o_ref, o in zip(refs[nin + npar:], fn(vals, pv)):
            o_ref[...] = o.astype(o_ref.dtype)

    in_specs = [pl.BlockSpec((r, cb), functools.partial(lambda j, i, off: (i, off + j), off=off)) for _, cb, off in ins]
    in_specs += [pl.BlockSpec(bs, functools.partial(lambda j, i, f: f(j), f=f)) for _, bs, f in params]
    out_specs = [pl.BlockSpec((r, cb), lambda j, i: (i, j)) for cb, _ in outs]
    out_shape = [jax.ShapeDtypeStruct((n, cb * ncol), dt) for cb, dt in outs]
    res = pl.pallas_call(body, name=name, grid=(ncol, n // r), in_specs=in_specs, out_specs=out_specs,
                         out_shape=out_shape, compiler_params=_params())(*[a for a, _, _ in ins], *[a for a, _, _ in params])
    return res


def rowmap_bwd(fn, ins, params, douts, ncol, name, rows=256, add=None, din_dtypes=None):
    n = ins[0][0].shape[0]
    r = min(rows, n)
    nin, npar, nout = len(ins), len(params), len(douts)
    add = [None] * nin if add is None else list(add)
    add_idx = [i for i in range(nin) if add[i] is not None]
    din_dtypes = [f32] * nin if din_dtypes is None else list(din_dtypes)
    kept = [i for i in range(nin) if din_dtypes[i] is not None]

    def body(*refs):
        vals = [x[...] for x in refs[:nin]]
        pv = [p[...] for p in refs[nin:nin + npar]]
        dys = [d[...] for d in refs[nin + npar:nin + npar + nout]]
        k0 = nin + npar + nout
        add_refs = dict(zip(add_idx, refs[k0:k0 + len(add_idx)]))
        k0 += len(add_idx)
        din_refs = refs[k0:k0 + len(kept)]
        dp_refs = refs[k0 + len(kept):]
        _, vjp = jax.vjp(fn, vals, pv)
        dvals, dpv = vjp(dys)
        for d_ref, idx in zip(din_refs, kept):
            d = dvals[idx] + add_refs[idx][...] if idx in add_refs else dvals[idx]
            d_ref[...] = d.astype(d_ref.dtype)

        @pl.when(pl.program_id(1) == 0)
        def _():
            for d_ref in dp_refs:
                d_ref[...] = jnp.zeros_like(d_ref)

        for d_ref, d in zip(dp_refs, dpv):
            d_ref[...] += d

    in_specs = [pl.BlockSpec((r, cb), functools.partial(lambda j, i, off: (i, off + j), off=off)) for _, cb, off in ins]
    in_specs += [pl.BlockSpec(bs, functools.partial(lambda j, i, f: f(j), f=f)) for _, bs, f in params]
    in_specs += [pl.BlockSpec((r, d.shape[1] // ncol), lambda j, i: (i, j)) for d in douts]
    in_specs += [pl.BlockSpec((r, ins[i][1]), lambda j, i: (i, j)) for i in add_idx]
    out_specs = [pl.BlockSpec((r, ins[i][1]), lambda j, i: (i, j)) for i in kept]
    pshapes = [tuple(d for d in bs if d is not None) for _, bs, _ in params]
    out_specs += [pl.BlockSpec((None,) + ps, functools.partial(lambda j, i, nd: (j,) + (0,) * nd, nd=len(ps))) for ps in pshapes]
    out_shape = [jax.ShapeDtypeStruct((n, ins[i][1] * ncol), din_dtypes[i]) for i in kept]
    out_shape += [jax.ShapeDtypeStruct((ncol,) + ps, f32) for ps in pshapes]
    args = [a for a, _, _ in ins] + [a for a, _, _ in params] + list(douts) + [add[i] for i in add_idx]
    res = pl.pallas_call(body, name=name, grid=(ncol, n // r), in_specs=in_specs, out_specs=out_specs,
                         out_shape=out_shape, compiler_params=_params())(*args)
    dins = [None] * nin
    for pos, i in enumerate(kept):
        dins[i] = res[pos]
    return dins, res[len(kept):]


SEQ_ROWS = 256


def seqmap(fn, ins, params, nouts, ncol, name, out_dtype=f32):
    bsz, seq, _ = ins[0][0].shape
    r = min(SEQ_ROWS, seq)
    nin, npar = len(ins), len(params)

    def body(*refs):
        in_refs = refs[:nin]
        pv = [p[...] for p in refs[nin:nin + npar]]
        out_refs = refs[nin + npar:]

        def step(i, carry):
            r0 = pl.multiple_of(i * r, r)
            h0 = pl.multiple_of(jnp.maximum(r0 - 8, 0), 8)
            mains = [x[pl.ds(r0, r), :] for x in in_refs]
            halos = [jnp.where(i == 0, 0.0, x[pl.ds(h0, 8), :]) for x in in_refs]
            for o_ref, o in zip(out_refs, fn(mains, halos, pv)):
                o_ref[pl.ds(r0, r), :] = o.astype(o_ref.dtype)
            return carry

        lax.fori_loop(0, seq // r, step, 0)

    in_specs = [pl.BlockSpec((None, seq, LANE), functools.partial(lambda j, b, off: (b, 0, off + j), off=off)) for _, off in ins]
    in_specs += [pl.BlockSpec(bs, functools.partial(lambda j, b, f: f(j), f=f)) for _, bs, f in params]
    out_specs = [pl.BlockSpec((None, seq, LANE), lambda j, b: (b, 0, j)) for _ in range(nouts)]
    out_shape = [jax.ShapeDtypeStruct((bsz, seq, LANE * ncol), out_dtype) for _ in range(nouts)]
    return pl.pallas_call(body, name=name, grid=(ncol, bsz), in_specs=in_specs, out_specs=out_specs,
                          out_shape=out_shape, compiler_params=_params())(*[a for a, _ in ins], *[a for a, _, _ in params])


def seqmap_bwd(fn, ins, params, douts, ncol, name, din_dtype=f32):
    bsz, seq, _ = ins[0][0].shape
    r = min(SEQ_ROWS, seq)
    nin, npar, nout = len(ins), len(params), len(douts)
    narrow = din_dtype != f32

    def body(*refs):
        in_refs = refs[:nin]
        pv = [p[...] for p in refs[nin:nin + npar]]
        dy_refs = refs[nin + npar:nin + npar + nout]
        dout_refs = refs[nin + npar + nout:nin + npar + nout + nin]
        dp_refs = refs[nin + npar + nout + nin:nin + npar + nout + nin + npar]
        din_refs = refs[nin + npar + nout + nin + npar:] if narrow else dout_refs

        def step(i, dp_acc):
            r0 = pl.multiple_of(i * r, r)
            h0 = pl.multiple_of(jnp.maximum(r0 - 8, 0), 8)
            mains = [x[pl.ds(r0, r), :] for x in in_refs]
            halos_raw = [x[pl.ds(h0, 8), :] for x in in_refs]

            def tile(mains, halos_raw, pv):
                return fn(mains, [jnp.where(i == 0, 0.0, h) for h in halos_raw], pv)

            _, vjp = jax.vjp(tile, mains, halos_raw, pv)
            dm, dh, dp = vjp([d[pl.ds(r0, r), :] for d in dy_refs])
            for d_ref, m, h in zip(din_refs, dm, dh):
                d_ref[pl.ds(r0, r), :] = m
                d_ref[pl.ds(h0, 8), :] += h
            return [acc + d for acc, d in zip(dp_acc, dp)]

        dp = lax.fori_loop(0, seq // r, step, [jnp.zeros(p.shape, f32) for p in pv])
        if narrow:
            for o_ref, d_ref in zip(dout_refs, din_refs):
                o_ref[...] = d_ref[...].astype(o_ref.dtype)

        @pl.when(pl.program_id(1) == 0)
        def _():
            for d_ref in dp_refs:
                d_ref[...] = jnp.zeros_like(d_ref)

        for d_ref, d in zip(dp_refs, dp):
            d_ref[...] += d

    in_specs = [pl.BlockSpec((None, seq, LANE), functools.partial(lambda j, b, off: (b, 0, off + j), off=off)) for _, off in ins]
    in_specs += [pl.BlockSpec(bs, functools.partial(lambda j, b, f: f(j), f=f)) for _, bs, f in params]
    in_specs += [pl.BlockSpec((None, seq, LANE), lambda j, b: (b, 0, j)) for _ in range(nout)]
    out_specs = [pl.BlockSpec((None, seq, LANE), lambda j, b: (b, 0, j)) for _ in range(nin)]
    pshapes = [tuple(d for d in bs if d is not None) for _, bs, _ in params]
    out_specs += [pl.BlockSpec((None,) + ps, functools.partial(lambda j, b, nd: (j,) + (0,) * nd, nd=len(ps))) for ps in pshapes]
    out_shape = [jax.ShapeDtypeStruct((bsz, seq, LANE * ncol), din_dtype) for _ in range(nin)]
    out_shape += [jax.ShapeDtypeStruct((ncol,) + ps, f32) for ps in pshapes]
    args = [a for a, _ in ins] + [a for a, _, _ in params] + list(douts)
    res = pl.pallas_call(body, name=name, grid=(ncol, bsz), in_specs=in_specs, out_specs=out_specs, out_shape=out_shape,
                         scratch_shapes=[pltpu.VMEM((seq, LANE), f32) for _ in range(nin)] if narrow else [],
                         compiler_params=_params())(*args)
    return res[:nin], res[nin:]


def _tri_masks():
    row = lax.broadcasted_iota(jnp.int32, (CHUNK, CHUNK), 0)
    col = lax.broadcasted_iota(jnp.int32, (CHUNK, CHUNK), 1)
    return row >= col, row > col


CHUNKS_PER_STEP = 4


def _by_rows(parts, per_row):
    rows = [jnp.concatenate(parts[i:i + per_row], axis=1) for i in range(0, len(parts), per_row)]
    return jnp.concatenate(rows, axis=0)


def dn_prep(vals, ps):
    q, k, v, gb = vals
    nchunk = q.shape[0] // CHUNK
    causal, strict = _tri_masks()
    gbs = [gb[c * CHUNK:(c + 1) * CHUNK] for c in range(nchunk)]
    gcs = [cumsum_rows(g) for g in gbs]
    gcts = [g.T for g in gcs]
    chains = [(c, h) for c in range(nchunk) for h in range(DN_HEADS)]
    part = lambda t, c, h: t[c * CHUNK:(c + 1) * CHUNK, h * DN_DK:(h + 1) * DN_DK]
    qh = [part(q, c, h) for c, h in chains]
    kh = [part(k, c, h) for c, h in chains]
    vh = [part(v, c, h) for c, h in chains]
    g_col = [_lane_pick(gcs[c], h) for c, h in chains]
    beta = [_lane_pick(gbs[c], DN_HEADS + h) for c, h in chains]
    g_row = [_row_pick(gcts[c], h)[:, :CHUNK] for c, h in chains]
    decay = [jnp.where(causal, jnp.exp(jnp.where(causal, gc - gr, 0.0)), 0.0) for gc, gr in zip(g_col, g_row)]
    k_beta = [a * b for a, b in zip(kh, beta)]
    eg = [jnp.exp(g) for g in g_col]
    kk = [bdot(a, b, "nt") for a, b in zip(k_beta, kh)]
    qk = [bdot(a, b, "nt") for a, b in zip(qh, kh)]
    t_inv = unit_lower_inv_all([jnp.where(strict, a * d, 0.0) for a, d in zip(kk, decay)])
    u = [bdot(t, a * b, "nn") for t, a, b in zip(t_inv, vh, beta)]
    w = [bdot(t, a * e, "nn") for t, a, e in zip(t_inv, k_beta, eg)]
    attn = [jnp.concatenate([a * d, jnp.zeros((CHUNK, DN_DK - CHUNK), f32)], axis=1) for a, d in zip(qk, decay)]
    qd = [a * e for a, e in zip(qh, eg)]
    kd = [a * jnp.exp(_row_pick(g, CHUNK - 1) - g) for a, g in zip(kh, g_col)]
    g_last = jnp.concatenate([jnp.broadcast_to(_row_pick(g, CHUNK - 1), g.shape) for g in gcs], axis=0)
    return [_by_rows(t, DN_HEADS) for t in (u, w, attn, qd, kd)] + [g_last]


def dn_step(state, u, w, attn, qd, kd, g_last):
    bsz = u.shape[0]
    chains = [(b, h) for b in range(bsz) for h in range(DN_HEADS)]
    part = lambda t, b, h: t[b, :, h * DN_DK:(h + 1) * DN_DK]
    ws = [bdot(part(w, b, h), s, "nn") for (b, h), s in zip(chains, state)]
    qs = [bdot(part(qd, b, h), s, "nn") for (b, h), s in zip(chains, state)]
    v_new = [part(u, b, h) - x for (b, h), x in zip(chains, ws)]
    av = [bdot(attn[b, :, h * DN_DK:h * DN_DK + CHUNK], x, "nn") for (b, h), x in zip(chains, v_new)]
    kv = [bdot(part(kd, b, h), x, "tn") for (b, h), x in zip(chains, v_new)]
    ge = [jnp.exp(_row_pick(_lane_pick(g_last[b], h), 0)) for b, h in chains]
    new_state = [s * g + x for s, g, x in zip(state, ge, kv)]
    outs = [a + b for a, b in zip(qs, av)]
    return new_state, jnp.concatenate([jnp.concatenate(outs[b * DN_HEADS:(b + 1) * DN_HEADS], axis=1)[None]
                                       for b in range(bsz)], axis=0)


def _ret_log_gamma(h):
    return math.log(1.0 - 2.0 ** (-5.0 - h))


def ret_prep(vals, ps):
    q, k, v = vals
    nchunk = q.shape[0] // CHUNK
    causal, _ = _tri_masks()
    row = lax.broadcasted_iota(jnp.int32, (CHUNK, CHUNK), 0)
    col = lax.broadcasted_iota(jnp.int32, (CHUNK, CHUNK), 1)
    dist = (row - col).astype(f32)
    lane = lax.broadcasted_iota(jnp.int32, (CHUNK, q.shape[1]), 1)
    dmask = [jnp.where(causal, jnp.exp(jnp.where(causal, dist, 0.0) * _ret_log_gamma(h)), 0.0) for h in range(RET_HEADS)]
    chains = [(c, h) for c in range(nchunk) for h in range(RET_HEADS)]
    rows = lambda t, c: t[c * CHUNK:(c + 1) * CHUNK]
    scores = [bdot(jnp.where((lane // RET_DK) == h, rows(q, c), 0.0), rows(k, c), "nt") * dmask[h] for c, h in chains]
    inner = [bdot(s, rows(v, c)[:, h * RET_DV:(h + 1) * RET_DV], "nn") for s, (c, h) in zip(scores, chains)]
    return [_by_rows(inner, RET_HEADS)]


def ret_step(state, q, k, v, inner):
    bsz = q.shape[0]
    idx = lax.broadcasted_iota(jnp.int32, (CHUNK, 1), 0).astype(f32)
    lane = lax.broadcasted_iota(jnp.int32, (CHUNK, q.shape[2]), 1)
    chains = [(b, h) for b in range(bsz) for h in range(RET_HEADS)]
    part = lambda t, b, h: t[b, :, h * RET_DV:(h + 1) * RET_DV]
    cross = [bdot(q[b], s, "nn") for (b, h), s in zip(chains, state)]
    kz = [jnp.where((lane // RET_DK) == h, k[b], 0.0) * jnp.exp((CHUNK - 1.0 - idx) * _ret_log_gamma(h)) for b, h in chains]
    kv = [bdot(a, part(v, b, h), "tn") for a, (b, h) in zip(kz, chains)]
    outs = [x * jnp.exp((idx + 1.0) * _ret_log_gamma(h)) + part(inner, b, h) for x, (b, h) in zip(cross, chains)]
    new_state = [s * math.exp(CHUNK * _ret_log_gamma(h)) + x for s, x, (b, h) in zip(state, kv, chains)]
    return new_state, jnp.concatenate([jnp.concatenate(outs[b * RET_HEADS:(b + 1) * RET_HEADS], axis=1)[None]
                                       for b in range(bsz)], axis=0)


def chunk_scan(step_fn, ins, state_shape, out_width, name):
    bsz, seq, _ = ins[0].shape
    nchunk = seq // CHUNK
    nin = len(ins)
    nh = state_shape[0]

    def body(*refs):
        in_refs = refs[:nin]
        o_ref, ck_ref, s_ref = refs[nin:]

        @pl.when(pl.program_id(0) == 0)
        def _():
            s_ref[...] = jnp.zeros_like(s_ref)

        state = [s_ref[i] for i in range(bsz * nh)]
        for i in range(bsz * nh):
            ck_ref[i // nh, i % nh] = state[i]
        new_state, out = step_fn(state, *[x[...] for x in in_refs])
        o_ref[...] = out
        for i in range(bsz * nh):
            s_ref[i] = new_state[i]

    in_specs = [pl.BlockSpec((bsz, CHUNK, x.shape[2]), lambda n: (0, n, 0)) for x in ins]
    out_specs = [pl.BlockSpec((bsz, CHUNK, out_width), lambda n: (0, n, 0)),
                 pl.BlockSpec((bsz, None) + tuple(state_shape), lambda n: (0, n, 0, 0, 0))]
    out_shape = [jax.ShapeDtypeStruct((bsz, seq, out_width), f32),
                 jax.ShapeDtypeStruct((bsz, nchunk) + tuple(state_shape), f32)]
    return pl.pallas_call(body, name=name, grid=(nchunk,), in_specs=in_specs, out_specs=out_specs, out_shape=out_shape,
                          scratch_shapes=[pltpu.VMEM((bsz * nh,) + tuple(state_shape[1:]), f32)],
                          compiler_params=_params())(*ins)


def chunk_scan_bwd(step_fn, ins, ckpt, dout, name):
    bsz, seq, _ = ins[0].shape
    nchunk = seq // CHUNK
    nin = len(ins)
    state_shape = ckpt.shape[2:]
    nh = state_shape[0]

    def body(*refs):
        in_refs = refs[:nin]
        ck_ref, do_ref = refs[nin:nin + 2]
        din_refs = refs[nin + 2:nin + 2 + nin]
        ds_ref = refs[-1]

        @pl.when(pl.program_id(0) == 0)
        def _():
            ds_ref[...] = jnp.zeros_like(ds_ref)

        state = [ck_ref[i // nh, i % nh] for i in range(bsz * nh)]
        _, vjp = jax.vjp(step_fn, state, *[x[...] for x in in_refs])
        grads = vjp(([ds_ref[i] for i in range(bsz * nh)], do_ref[...]))
        for i in range(bsz * nh):
            ds_ref[i] = grads[0][i]
        for d_ref, d in zip(din_refs, grads[1:]):
            d_ref[...] = d

    rev = lambda n: (0, nchunk - 1 - n, 0)
    in_specs = [pl.BlockSpec((bsz, CHUNK, x.shape[2]), rev) for x in ins]
    in_specs += [pl.BlockSpec((bsz, None) + tuple(state_shape), lambda n: (0, nchunk - 1 - n, 0, 0, 0)),
                 pl.BlockSpec((bsz, CHUNK, dout.shape[2]), rev)]
    out_specs = [pl.BlockSpec((bsz, CHUNK, x.shape[2]), rev) for x in ins]
    out_shape = [jax.ShapeDtypeStruct(x.shape, f32) for x in ins]
    return pl.pallas_call(body, name=name, grid=(nchunk,), in_specs=in_specs, out_specs=out_specs, out_shape=out_shape,
                          scratch_shapes=[pltpu.VMEM((bsz * nh,) + tuple(state_shape[1:]), f32)],
                          compiler_params=_params())(*ins, ckpt, dout)


LRU_ROWS = 512


def lru_scan(a, b):
    bsz, seq, width = a.shape
    rb = min(LRU_ROWS, seq)

    def body(a_ref, b_ref, h_ref, hp_ref, carry_ref):
        @pl.when(pl.program_id(1) == 0)
        def _():
            carry_ref[...] = jnp.zeros_like(carry_ref)

        row = lax.broadcasted_iota(jnp.int32, (8, width), 0)

        def tile(t, hprev):
            r0 = pl.multiple_of(t * 8, 8)
            ca, cbv = a_ref[pl.ds(r0, 8), :], b_ref[pl.ds(r0, 8), :]
            for s in (1, 2, 4):
                m = row >= s
                cbv = jnp.where(m, ca * pltpu.roll(cbv, s, 0) + cbv, cbv)
                ca = jnp.where(m, ca * pltpu.roll(ca, s, 0), ca)
            h = cbv + ca * hprev
            h_ref[pl.ds(r0, 8), :] = h
            hp_ref[pl.ds(r0, 8), :] = jnp.where(row == 0, hprev, pltpu.roll(h, 1, 0))
            return _row_pick(h, 7)

        carry_ref[0:1, :] = lax.fori_loop(0, rb // 8, tile, carry_ref[0:1, :])

    spec = pl.BlockSpec((None, rb, width), lambda bi, i: (bi, i, 0))
    return pl.pallas_call(body, name="lru_scan", grid=(bsz, seq // rb), in_specs=[spec, spec], out_specs=[spec, spec],
                          out_shape=[jax.ShapeDtypeStruct(a.shape, f32)] * 2,
                          scratch_shapes=[pltpu.VMEM((8, width), f32)], compiler_params=_params())(a, b)


def lru_scan_bwd(a, hp, dh):
    bsz, seq, width = a.shape
    rb = min(LRU_ROWS, seq)
    nblk = seq // rb

    def body(a_ref, hp_ref, dh_ref, da_ref, db_ref, carry_ref):
        @pl.when(pl.program_id(1) == 0)
        def _():
            carry_ref[...] = jnp.zeros_like(carry_ref)

        row = lax.broadcasted_iota(jnp.int32, (8, width), 0)
        ntile = rb // 8

        def tile(t, mu_next):
            r0 = pl.multiple_of((ntile - 1 - t) * 8, 8)
            ca, dh_t = a_ref[pl.ds(r0, 8), :], dh_ref[pl.ds(r0, 8), :]
            cbv = ca * dh_t
            for s in (1, 2, 4):
                m = row < 8 - s
                cbv = jnp.where(m, ca * pltpu.roll(cbv, 8 - s, 0) + cbv, cbv)
                ca = jnp.where(m, ca * pltpu.roll(ca, 8 - s, 0), ca)
            mu = cbv + ca * mu_next
            lam = dh_t + jnp.where(row == 7, mu_next, pltpu.roll(mu, 7, 0))
            db_ref[pl.ds(r0, 8), :] = lam
            da_ref[pl.ds(r0, 8), :] = lam * hp_ref[pl.ds(r0, 8), :]
            return _row_pick(mu, 0)

        carry_ref[0:1, :] = lax.fori_loop(0, ntile, tile, carry_ref[0:1, :])

    spec = pl.BlockSpec((None, rb, width), lambda bi, i: (bi, nblk - 1 - i, 0))
    return pl.pallas_call(body, name="lru_scan_bwd", grid=(bsz, nblk), in_specs=[spec] * 3, out_specs=[spec, spec],
                          out_shape=[jax.ShapeDtypeStruct(a.shape, f32)] * 2,
                          scratch_shapes=[pltpu.VMEM((8, width), f32)], compiler_params=_params())(a, hp, dh)


def final_loss(x, g, target):
    n, d = x.shape
    r = min(256, n)

    def body(x_ref, g_ref, t_ref, loss_ref, dx_ref, dg_ref):
        @pl.when(pl.program_id(0) == 0)
        def _():
            loss_ref[...] = jnp.zeros_like(loss_ref)
            dg_ref[...] = jnp.zeros_like(dg_ref)

        tgt = t_ref[...]

        def loss_fn(xv, gv):
            y = f_norm([xv], [gv])[0]
            return 0.5 * jnp.sum(jnp.mean(jnp.square(y - tgt), axis=-1, keepdims=True), axis=0, keepdims=True)

        val, vjp = jax.vjp(loss_fn, x_ref[...], g_ref[...])
        dx, dg = vjp(jnp.ones_like(val))
        loss_ref[...] += val
        dx_ref[...] = dx
        dg_ref[...] += dg

    row = pl.BlockSpec((r, d), lambda i: (i, 0))
    return pl.pallas_call(
        body, name="final_loss", grid=(n // r,), in_specs=[row, pl.BlockSpec((1, d), lambda i: (0, 0)), row],
        out_specs=[pl.BlockSpec((8, LANE), lambda i: (0, 0)), row, pl.BlockSpec((1, d), lambda i: (0, 0))],
        out_shape=[jax.ShapeDtypeStruct((8, LANE), f32), jax.ShapeDtypeStruct((n, d), f32), jax.ShapeDtypeStruct((1, d), f32)],
        compiler_params=_params())(x, g, target)


_HBM = pl.BlockSpec(memory_space=pltpu.HBM)
_SEM = pl.BlockSpec(memory_space=pltpu.SEMAPHORE)
_EFFECT = pltpu.SideEffectType.DATAFLOW_SIDE_EFFECTING


def _peer(k):
    mx, my, mc = lax.axis_index("x"), lax.axis_index("y"), lax.axis_index("c")
    px, py, pc = (mx + (k >> 2)) % 2, (my + ((k >> 1) & 1)) % 2, (mc + (k & 1)) % 2
    return (px, py, pc), 4 * px + 2 * py + pc


def _peer_copy(k, i, x_ref, land_ref, send_sems, recv_sems, scatter):
    me = 4 * lax.axis_index("x") + 2 * lax.axis_index("y") + lax.axis_index("c")
    dev, slot = _peer(k)
    sem = i * (N_DEV - 1) + k - 1
    return pltpu.make_async_remote_copy(
        src_ref=x_ref.at[slot] if scatter else x_ref, dst_ref=land_ref.at[me], send_sem=send_sems.at[sem],
        recv_sem=recv_sems.at[sem], device_id=dev, device_id_type=pl.DeviceIdType.MESH)


def exchange_start(xs, scatters, name):
    nx = len(xs)
    lands = [lax.empty((N_DEV,) + tuple(x.shape[1:] if sc else x.shape), x.dtype) for x, sc in zip(xs, scatters)]
    nsem = nx * (N_DEV - 1)

    def body(*refs):
        x_refs, land_refs = refs[:nx], refs[nx:2 * nx]
        send_sems, recv_sems = refs[2 * nx:2 * nx + 2]
        token = refs[-1]
        for i in range(nx):
            for k in range(1, N_DEV):
                _peer_copy(k, i, x_refs[i], land_refs[i], send_sems, recv_sems, scatters[i]).start()
        token[...] = jnp.zeros_like(token)

    hbm = lambda a: pltpu.HBM(a.shape, a.dtype)
    res = pl.pallas_call(
        body, name=name, in_specs=(_HBM,) * (2 * nx),
        out_specs=(_SEM, _SEM) + (_HBM,) * (2 * nx) + (pl.BlockSpec(memory_space=pltpu.VMEM),),
        input_output_aliases={i: 2 + i for i in range(2 * nx)},
        out_shape=(pltpu.SemaphoreType.DMA((nsem,)), pltpu.SemaphoreType.DMA((nsem,)), *[hbm(a) for a in xs],
                   *[hbm(a) for a in lands], jax.ShapeDtypeStruct((8, LANE), f32)),
        compiler_params=pltpu.CompilerParams(has_side_effects=_EFFECT),
    )(*[pltpu.with_memory_space_constraint(a, pltpu.HBM) for a in list(xs) + lands])
    return (res[0], res[1], list(res[2:2 + nx]), list(res[2 + nx:2 + 2 * nx]), tuple(scatters)), res[-1]


def exchange_wait(started, after, name):
    send_sems, recv_sems, x_thrus, land_thrus, scatters = started
    nx = len(x_thrus)

    def body(*refs):
        x_refs, land_refs = refs[:nx], refs[nx:2 * nx]
        send_sems, recv_sems = refs[2 * nx:2 * nx + 2]
        for i in range(nx):
            for k in range(1, N_DEV):
                cp = _peer_copy(k, i, x_refs[i], land_refs[i], send_sems, recv_sems, scatters[i])
                cp.wait_send()
                cp.wait_recv()

    hbm = lambda a: pltpu.HBM(a.shape, a.dtype)
    res = pl.pallas_call(
        body, name=name, in_specs=(_HBM,) * (2 * nx) + (_SEM, _SEM, pl.BlockSpec(memory_space=pl.ANY)),
        out_specs=(_HBM,) * (2 * nx), input_output_aliases={i: i for i in range(2 * nx)},
        out_shape=tuple(hbm(a) for a in list(x_thrus) + list(land_thrus)),
        compiler_params=pltpu.CompilerParams(has_side_effects=_EFFECT),
    )(*x_thrus, *land_thrus, send_sems, recv_sems, after)
    return list(res[:nx]), list(res[nx:])


def sum_slots(x, name, own=None):
    _, rows_total, cols = x.shape
    row_bytes = N_DEV * ((cols + LANE - 1) // LANE) * LANE * x.dtype.itemsize
    r = _pick_rows(rows_total, max(16, (4 * 1024 * 1024) // row_bytes // 16 * 16))
    if own is not None:
        def body_own(x_ref, own_ref, o_ref):
            me = 4 * lax.axis_index("x") + 2 * lax.axis_index("y") + lax.axis_index("c")
            acc = None
            for s in range(N_DEV):
                v = jnp.where(me == s, own_ref[...], x_ref[s]).astype(f32)
                acc = v if acc is None else acc + v
            o_ref[...] = acc

        return pl.pallas_call(body_own, name=name, grid=(rows_total // r,),
                              in_specs=[pl.BlockSpec((N_DEV, r, cols), lambda i: (0, i, 0)), pl.BlockSpec((r, cols), lambda i: (i, 0))],
                              out_specs=pl.BlockSpec((r, cols), lambda i: (i, 0)),
                              out_shape=jax.ShapeDtypeStruct((rows_total, cols), f32), compiler_params=_params())(x, own)

    def body(x_ref, o_ref):
        acc = x_ref[0].astype(f32)
        for s in range(1, N_DEV):
            acc = acc + x_ref[s].astype(f32)
        o_ref[...] = acc

    return pl.pallas_call(body, name=name, grid=(rows_total // r,),
                          in_specs=[pl.BlockSpec((N_DEV, r, cols), lambda i: (0, i, 0))],
                          out_specs=pl.BlockSpec((r, cols), lambda i: (i, 0)),
                          out_shape=jax.ShapeDtypeStruct((rows_total, cols), f32), compiler_params=_params())(x)


def _pick_rows(total, pref):
    best = None
    for d in range(16, min(total, pref) + 1, 16):
        if total % d == 0:
            best = d
    return best if best is not None else total


def adamw(w, g, m, v, name):
    shape = w.shape
    if w.ndim == 1:
        w2, g2, m2, v2 = (t.reshape(1, -1) for t in (w, g, m, v))
    else:
        w2, g2, m2, v2 = (t.reshape(-1, shape[-1]) for t in (w, g, m, v))
    rows_total, cols = w2.shape
    r = _pick_rows(rows_total, max(16, (512 * 1024) // max(cols, 1) // 16 * 16))
    c1, c2 = 1.0 / (1.0 - ADAM_B1 ** ADAM_STEP), 1.0 / (1.0 - ADAM_B2 ** ADAM_STEP)

    def body(w_ref, g_ref, m_ref, v_ref, d_ref, nm_ref, nv_ref):
        gv = g_ref[...]
        nm = ADAM_B1 * m_ref[...] + (1.0 - ADAM_B1) * gv
        nv = ADAM_B2 * v_ref[...] + (1.0 - ADAM_B2) * jnp.square(gv)
        d_ref[...] = -ADAM_LR * ((nm * c1) / (jnp.sqrt(nv * c2) + ADAM_EPS) + ADAM_WD * w_ref[...])
        nm_ref[...] = nm
        nv_ref[...] = nv

    spec = pl.BlockSpec((r, cols), lambda i: (i, 0))
    outs = pl.pallas_call(body, name=name, grid=(rows_total // r,), in_specs=[spec] * 4, out_specs=[spec] * 3,
                          out_shape=[jax.ShapeDtypeStruct((rows_total, cols), f32)] * 3, compiler_params=_params())(w2, g2, m2, v2)
    return tuple(o.reshape(shape) for o in outs)


def _const(j):
    return lambda _: j


def _layer_fwd(x, wl, fetch_rest, cos, sin, bsz, seq):
    n = x.shape[0]
    sv = {"x_in": x}
    row1 = lambda a: (a, (1, a.shape[1]), lambda j: (0, 0))
    h = rowmap(f_norm, [(x, D_MODEL, 0)], [row1(wl["attn_norm"])], [(D_MODEL, bf16)], 1, "norm_fwd")[0]
    u = mm(h, wl["w_in"], "nn", "mm_in")
    sv["h"], sv["u"] = h, u
    u3 = u.reshape(bsz, seq, U_PAD)
    wl = dict(wl)
    wl.update(fetch_rest(u))
    sv["wl"] = wl

    qkv = []
    for kind in range(3):
        cw = (wl["dn_conv_w"], (4, LANE), functools.partial(lambda j, kind: (0, 4 * kind + j), kind=kind))
        qkv.append(seqmap(functools.partial(f_dn_pre, kind), [(u3, U_QKV // LANE + 4 * kind)], [cw], 1, 4, "dn_pre%d" % kind)[0])
    gb = rowmap(f_dn_gates, [(u, LANE, U_AB // LANE)], [(wl["dn_gate_p"], (8, LANE), lambda j: (0, 0))], [(LANE, f32)], 1,
                "dn_gates")[0]
    gb3 = gb.reshape(bsz, seq, LANE)
    crow = CHUNK * CHUNKS_PER_STEP
    dn_in = [(t.reshape(n, 512), 512, 0) for t in qkv] + [(gb, LANE, 0)]
    prep_a = rowmap(dn_prep, dn_in, [], [(512, f32)] * 5 + [(LANE, f32)], 1, "dn_prep", rows=crow)
    prep_a = [t.reshape(bsz, seq, t.shape[1]) for t in prep_a]
    o_a, ck_a = chunk_scan(dn_step, prep_a, (DN_HEADS, DN_DK, DN_DK), 512, "dn_scan")
    y_a = rowmap(f_dn_post, [(o_a.reshape(n, 512), LANE, 0), (u, LANE, U_Z // LANE)],
                 [(wl["dn_norm_w"], (1, LANE), lambda j: (0, 0))], [(LANE, bf16)], 4, "dn_post")[0]
    sv.update(dn_in=dn_in, prep_a=prep_a, o_a=o_a, ck_a=ck_a, y_a=y_a)

    q_b, k_b = rowmap(f_ret_pre, [(u, 256, U_RQ // 256), (u, 256, U_RK // 256), (cos, 256, 0), (sin, 256, 0)], [],
                      [(256, f32), (256, f32)], 1, "ret_pre")
    q_b3, k_b3 = q_b.reshape(bsz, seq, 256), k_b.reshape(bsz, seq, 256)
    v_b3 = lax.slice_in_dim(u3, U_RV, U_RV + 512, axis=2)
    ret_in = [(q_b, 256, 0), (k_b, 256, 0), (u, 512, U_RV // 512)]
    inner = rowmap(ret_prep, ret_in, [], [(512, f32)], 1, "ret_prep", rows=crow)[0]
    ret_seq = [q_b3, k_b3, v_b3, inner.reshape(bsz, seq, 512)]
    o_b, ck_b = chunk_scan(ret_step, ret_seq, (RET_HEADS, 256, RET_DV), 512, "ret_scan")
    y_b = rowmap(f_ret_post, [(o_b.reshape(n, 512), LANE, 0), (u, LANE, U_RG // LANE)], [], [(LANE, bf16)], 4, "ret_post")[0]
    sv.update(ret_in=ret_in, ret_seq=ret_seq, o_b=o_b, ck_b=ck_b, y_b=y_b)

    lru_params = _lru_params(wl)
    a_c, b_c = seqmap(f_lru_pre, [(u3, U_CX // LANE)], lru_params, 2, 4, "lru_pre")
    h_c, hp_c = lru_scan(a_c, b_c)
    y_c = rowmap(f_lru_post, [(h_c.reshape(n, 512), 512, 0), (u, 512, U_CG // 512)], [], [(512, bf16)], 1, "lru_post")[0]
    sv.update(a_c=a_c, hp_c=hp_c, h_c=h_c, y_c=y_c)

    br = [mm(y, wl["w_branch"][i], "nn", "mm_branch") for i, y in enumerate((y_a, y_b, y_c))]
    merged = rowmap(f_merge, [(u, D_MODEL, i) for i in range(3)] + [(b, D_MODEL, 0) for b in br], [], [(D_MODEL, bf16)], 1,
                    "merge")[0]
    x_mid = mm(merged, wl["w_out"], "nn", "mm_out", add=x)
    sv.update(br=br, merged=merged, x_mid=x_mid)

    h2 = rowmap(f_norm, [(x_mid, D_MODEL, 0)], [row1(wl["ffn_norm"])], [(D_MODEL, bf16)], 1, "norm_fwd")[0]
    up = mm(h2, wl["w_up"], "nn", "mm_up")
    act = seqmap(f_ffn_mid, [(up.reshape(bsz, seq, 2 * D_FF), 0), (up.reshape(bsz, seq, 2 * D_FF), D_FF // LANE)],
                 _ffn_params(wl), 1, D_FF // LANE, "ffn_mid", out_dtype=bf16)[0]
    act = act.reshape(n, D_FF)
    x_out = mm(act, wl["w_down"], "nn", "mm_down", add=x_mid)
    sv.update(h2=h2, up=up, act=act)
    return x_out, sv


def _lru_params(wl):
    col = lambda a: (a, (a.shape[0], LANE), lambda j: (0, j))
    blk = lambda a: (a, (None, LANE, LANE), lambda j: (j, 0, 0))
    return [col(wl["lru_conv_w"]), col(wl["lru_conv_b"]), blk(wl["lru_wa"]), col(wl["lru_ba"]), blk(wl["lru_wx"]),
            col(wl["lru_bx"]), col(wl["lru_lambda"])]


def _ffn_params(wl):
    nb = D_FF // LANE
    return [(wl["ffn_conv_w"], (3, LANE), lambda j: (0, j)), (wl["ffn_conv_w"], (3, LANE), lambda j: (0, nb + j)),
            (wl["ffn_conv_b"], (1, LANE), lambda j: (0, j)), (wl["ffn_conv_b"], (1, LANE), lambda j: (0, nb + j))]


def _layer_bwd(dx, sv, cos, sin, bsz, seq, emit, dep):
    n = dx.shape[0]
    gr = {}
    wl = sv["wl"]
    u, x_in, x_mid = sv["u"], sv["x_in"], sv["x_mid"]
    u3 = u.reshape(bsz, seq, U_PAD)
    row1 = lambda a: (a, (1, a.shape[1]), lambda j: (0, 0))

    d_act = mm(dx, wl["w_down"], "nt", "mm_down_dx", dep=dep)
    gr["w_down"] = mm(sv["act"], dx, "tn", "mm_down_dw")
    up3 = sv["up"].reshape(bsz, seq, 2 * D_FF)
    (d_gate, d_val), dps = seqmap_bwd(f_ffn_mid, [(up3, 0), (up3, D_FF // LANE)], _ffn_params(wl),
                                      [d_act.reshape(bsz, seq, D_FF)], D_FF // LANE, "ffn_mid_bwd", din_dtype=bf16)
    gr["ffn_conv_w"] = jnp.concatenate([_cols(dps[0]), _cols(dps[1])], axis=1)
    gr["ffn_conv_b"] = jnp.concatenate([_cols(dps[2]), _cols(dps[3])], axis=1)[0]
    d_up = jnp.concatenate([d_gate, d_val], axis=2).reshape(n, 2 * D_FF)
    gr["w_up"] = mm(sv["h2"], d_up, "tn", "mm_up_dw")
    token = emit("ffn", {k: gr[k] for k in ("w_up", "w_down")})
    d_h2 = mm(d_up, wl["w_up"], "nt", "mm_up_dx", dep=token)
    (dx_mid,), (dg,) = rowmap_bwd(f_norm, [(x_mid, D_MODEL, 0)], [row1(wl["ffn_norm"])], [d_h2], 1, "norm_bwd", add=[dx])
    gr["ffn_norm"] = dg[0, 0]

    d_merged = mm(dx_mid, wl["w_out"], "nt", "mm_out_dx")
    gr["w_out"] = mm(sv["merged"], dx_mid, "tn", "mm_out_dw")
    dm, _ = rowmap_bwd(f_merge, [(u, D_MODEL, i) for i in range(3)] + [(b, D_MODEL, 0) for b in sv["br"]], [], [d_merged], 1,
                       "merge_bwd", din_dtypes=[bf16] * 6)
    d_gl, d_br = dm[:3], dm[3:]
    ys = (sv["y_a"], sv["y_b"], sv["y_c"])
    d_ys = [mm(d_br[i], wl["w_branch"][i], "nt", "mm_branch_dx") for i in range(3)]
    gr["w_branch"] = jnp.stack([mm(ys[i], d_br[i], "tn", "mm_branch_dw") for i in range(3)])

    (d_hc, d_cg), _ = rowmap_bwd(f_lru_post, [(sv["h_c"].reshape(n, 512), 512, 0), (u, 512, U_CG // 512)], [], [d_ys[2]], 1,
                                 "lru_post_bwd", din_dtypes=[f32, bf16])
    d_a, d_b = lru_scan_bwd(sv["a_c"], sv["hp_c"], d_hc.reshape(bsz, seq, 512))
    (d_cx,), dps = seqmap_bwd(f_lru_pre, [(u3, U_CX // LANE)], _lru_params(wl), [d_a, d_b], 4, "lru_pre_bwd", din_dtype=bf16)
    gr["lru_conv_w"], gr["lru_conv_b"] = _cols(dps[0]), _cols(dps[1])[0]
    gr["lru_wa"], gr["lru_ba"], gr["lru_wx"], gr["lru_bx"] = dps[2], dps[3][:, 0], dps[4], dps[5][:, 0]
    gr["lru_lambda"] = _cols(dps[6])[0]

    (d_ob, d_rg), _ = rowmap_bwd(f_ret_post, [(sv["o_b"].reshape(n, 512), LANE, 0), (u, LANE, U_RG // LANE)], [], [d_ys[1]], 4,
                                 "ret_post_bwd", din_dtypes=[f32, bf16])
    crow = CHUNK * CHUNKS_PER_STEP
    d_ret = chunk_scan_bwd(ret_step, sv["ret_seq"], sv["ck_b"], d_ob.reshape(bsz, seq, 512), "ret_scan_bwd")
    d_ret = [t.reshape(n, t.shape[2]) for t in d_ret]
    (d_qb, d_kb, d_rv), _ = rowmap_bwd(ret_prep, sv["ret_in"], [], [d_ret[3]], 1, "ret_prep_bwd", rows=crow,
                                       add=d_ret[:3], din_dtypes=[f32, f32, bf16])
    dr, _ = rowmap_bwd(f_ret_pre, [(u, 256, U_RQ // 256), (u, 256, U_RK // 256), (cos, 256, 0), (sin, 256, 0)], [],
                       [d_qb, d_kb], 1, "ret_pre_bwd", din_dtypes=[bf16, bf16, None, None])
    d_rq, d_rk = dr[0], dr[1]

    (d_oa, d_z), (dnw,) = rowmap_bwd(f_dn_post, [(sv["o_a"].reshape(n, 512), LANE, 0), (u, LANE, U_Z // LANE)],
                                     [(wl["dn_norm_w"], (1, LANE), lambda j: (0, 0))], [d_ys[0]], 4, "dn_post_bwd",
                                     din_dtypes=[f32, bf16])
    gr["dn_norm_w"] = jnp.sum(dnw, axis=0)[0]
    d_prep = chunk_scan_bwd(dn_step, sv["prep_a"], sv["ck_a"], d_oa.reshape(bsz, seq, 512), "dn_scan_bwd")
    (d_q, d_k, d_v, d_gb), _ = rowmap_bwd(dn_prep, sv["dn_in"], [], [t.reshape(n, t.shape[2]) for t in d_prep], 1,
                                          "dn_prep_bwd", rows=crow)
    d_q, d_k, d_v = (t.reshape(bsz, seq, 512) for t in (d_q, d_k, d_v))
    (d_ab,), (dgp,) = rowmap_bwd(f_dn_gates, [(u, LANE, U_AB // LANE)], [(wl["dn_gate_p"], (8, LANE), lambda j: (0, 0))],
                                 [d_gb], 1, "dn_gates_bwd", din_dtypes=[bf16])
    gr["dn_a_log"], gr["dn_dt_bias"] = dgp[0, 0, :DN_HEADS], dgp[0, 1, :DN_HEADS]
    d_qkv, d_cw = [], []
    for kind, d_t in enumerate((d_q, d_k, d_v)):
        cw = (wl["dn_conv_w"], (4, LANE), functools.partial(lambda j, kind: (0, 4 * kind + j), kind=kind))
        (d_in,), (dcw,) = seqmap_bwd(functools.partial(f_dn_pre, kind), [(u3, U_QKV // LANE + 4 * kind)], [cw], [d_t], 4,
                                     "dn_pre%d_bwd" % kind, din_dtype=bf16)
        d_qkv.append(d_in.reshape(n, 512))
        d_cw.append(_cols(dcw))
    gr["dn_conv_w"] = jnp.concatenate(d_cw, axis=1)

    pad = jnp.zeros((n, U_PAD - U_AB - LANE), bf16)
    du = jnp.concatenate(list(d_gl) + d_qkv + [d_rv, d_rg, d_z, d_cx.reshape(n, 512), d_cg, d_rq, d_rk, d_ab, pad], axis=1)
    gr["w_in"] = _unpad_w_in(mm(sv["h"], du, "tn", "mm_in_dw"))
    token = emit("mix", {k: gr[k] for k in ("w_in", "w_branch", "w_out")})
    d_h = mm(du, wl["w_in"], "nt", "mm_in_dx", dep=token)
    (dx_in,), (dg,) = rowmap_bwd(f_norm, [(x_in, D_MODEL, 0)], [row1(wl["attn_norm"])], [d_h], 1, "norm_bwd", add=[dx_mid])
    gr["attn_norm"] = dg[0, 0]
    big = ("w_in", "w_branch", "w_out", "w_up", "w_down")
    return dx_in, emit("small", {k: g for k, g in gr.items() if k not in big})


def _cols(dp):
    ncol, p, _ = dp.shape
    return jnp.transpose(dp, (1, 0, 2)).reshape(p, ncol * LANE)


def _pad_w_in(w):
    segs = sorted(_IN_SEGS, key=lambda s: s[2])
    parts = [lax.slice_in_dim(w, src, src + width, axis=1) for src, width, _ in segs]
    end = segs[-1][2] + segs[-1][1]
    return jnp.concatenate(parts + [jnp.zeros((w.shape[0], U_PAD - end), w.dtype)], axis=1)


def _unpad_w_in(wp):
    return jnp.concatenate([lax.slice_in_dim(wp, dst, dst + width, axis=1) for _, width, dst in _IN_SEGS], axis=1)


def _rope_tables(positions):
    half = RET_DK // 2
    inv = ROPE_BASE ** (-jnp.arange(half, dtype=f32) / half)
    ang = positions.astype(f32).reshape(-1, 1) * inv
    cos, sin = jnp.cos(ang), jnp.sin(ang)
    return jnp.tile(cos, (1, 2 * RET_HEADS)), jnp.tile(sin, (1, 2 * RET_HEADS))


def _layer_weights(lw):
    wl = {}
    wl["w_in"] = _pad_w_in(lw["w_in"])
    for k in ("dn_conv_w", "lru_conv_w", "ffn_conv_w", "lru_wa", "lru_wx"):
        wl[k] = lw[k]
    for k in ("attn_norm", "ffn_norm", "dn_norm_w", "lru_conv_b", "lru_lambda", "ffn_conv_b", "lru_ba", "lru_bx"):
        wl[k] = lw[k].reshape(1, -1)
    gp = jnp.zeros((8, LANE), f32)
    wl["dn_gate_p"] = gp.at[0, :DN_HEADS].set(lw["dn_a_log"]).at[1, :DN_HEADS].set(lw["dn_dt_bias"])
    return wl


REST = ("w_branch", "w_out", "w_up", "w_down")


def forward_backward(x, positions, target, layer_weights, final_norm, on_head, on_grads):
    bsz, seq, d = x.shape
    n = bsz * seq
    cos, sin = _rope_tables(positions)
    xs = x.reshape(n, d)
    saved = []
    for layer in range(DEPTH):
        first, fetch_rest = layer_weights(layer, xs)
        xs, sv = _layer_fwd(xs, _layer_weights(first), fetch_rest, cos, sin, bsz, seq)
        saved.append(sv)
    loss, dx, d_final = final_loss(xs, final_norm.reshape(1, d), target.reshape(n, d))
    on_head(loss[0, 0], d_final[0])
    token = None
    for layer in reversed(range(DEPTH)):
        dx, token = _layer_bwd(dx, saved[layer], cos, sin, bsz, seq, functools.partial(on_grads, layer), token)
    return dx.reshape(bsz, seq, d)


def local_step(x, positions, target, full):
    grads, head = {layer: {} for layer in range(DEPTH)}, {}

    def layer_weights(layer, _):
        return ({k: a[layer] for k, a in full.items() if k != "final_norm" and k not in REST},
                lambda after: {k: full[k][layer] for k in REST})

    gx = forward_backward(x, positions, target, layer_weights, full["final_norm"],
                          lambda loss, d_final: head.update(loss=loss, d_final=d_final),
                          lambda layer, group, gr: grads[layer].update(gr))
    stacked = {k: jnp.stack([grads[layer][k] for layer in range(DEPTH)]) for k in grads[0]}
    stacked["final_norm"] = head["d_final"]
    return head["loss"], gx, stacked


BIG = (("w_in", 2), ("w_branch", 3), ("w_out", 1), ("w_up", 2), ("w_down", 1))
SMALL_SHARDED = (("dn_conv_w", 2), ("lru_conv_w", 2), ("ffn_conv_w", 2))
REPLICATED = ("attn_norm", "dn_a_log", "dn_dt_bias", "dn_norm_w", "lru_conv_b", "lru_wa", "lru_ba", "lru_wx", "lru_bx",
              "lru_lambda", "ffn_norm", "ffn_conv_b", "final_norm")
WEIGHTS = ("attn_norm", "w_in", "dn_conv_w", "dn_a_log", "dn_dt_bias", "dn_norm_w", "lru_conv_w", "lru_conv_b", "lru_wa",
           "lru_ba", "lru_wx", "lru_bx", "lru_lambda", "w_branch", "w_out", "ffn_norm", "w_up", "ffn_conv_w", "ffn_conv_b",
           "w_down", "final_norm")


def _pack(arrs, dtype, align=16 * LANE):
    flat = jnp.concatenate([a.reshape(-1).astype(dtype) for a in arrs])
    pad = (-flat.shape[0]) % align
    return jnp.pad(flat, (0, pad)).reshape(-1, LANE)


def _unpack(rows, shapes):
    flat = rows.reshape(-1)
    out, pos = [], 0
    for shp in shapes:
        size = math.prod(shp)
        out.append(lax.slice_in_dim(flat, pos, pos + size).reshape(shp))
        pos += size
    return out


def _split8(a, axis):
    size = a.shape[axis] // N_DEV
    return [lax.slice_in_dim(a, p * size, (p + 1) * size, axis=axis) for p in range(N_DEV)]


def kernel(x, positions, attn_norm, w_in, dn_conv_w, dn_a_log, dn_dt_bias, dn_norm_w, lru_conv_w, lru_conv_b, lru_wa, lru_ba, lru_wx, lru_bx, lru_lambda, w_branch, w_out, ffn_norm, w_up, ffn_conv_w, ffn_conv_b, w_down, final_norm, loss_target, m_attn_norm, m_w_in, m_dn_conv_w, m_dn_a_log, m_dn_dt_bias, m_dn_norm_w, m_lru_conv_w, m_lru_conv_b, m_lru_wa, m_lru_ba, m_lru_wx, m_lru_bx, m_lru_lambda, m_w_branch, m_w_out, m_ffn_norm, m_w_up, m_ffn_conv_w, m_ffn_conv_b, m_w_down, m_final_norm, v_attn_norm, v_w_in, v_dn_conv_w, v_dn_a_log, v_dn_dt_bias, v_dn_norm_w, v_lru_conv_w, v_lru_conv_b, v_lru_wa, v_lru_ba, v_lru_wx, v_lru_bx, v_lru_lambda, v_w_branch, v_w_out, v_ffn_norm, v_w_up, v_ffn_conv_w, v_ffn_conv_b, v_w_down, v_final_norm):
    w = dict(attn_norm=attn_norm, w_in=w_in, dn_conv_w=dn_conv_w, dn_a_log=dn_a_log, dn_dt_bias=dn_dt_bias, dn_norm_w=dn_norm_w,
             lru_conv_w=lru_conv_w, lru_conv_b=lru_conv_b, lru_wa=lru_wa, lru_ba=lru_ba, lru_wx=lru_wx, lru_bx=lru_bx,
             lru_lambda=lru_lambda, w_branch=w_branch, w_out=w_out, ffn_norm=ffn_norm, w_up=w_up, ffn_conv_w=ffn_conv_w,
             ffn_conv_b=ffn_conv_b, w_down=w_down, final_norm=final_norm)
    m = dict(attn_norm=m_attn_norm, w_in=m_w_in, dn_conv_w=m_dn_conv_w, dn_a_log=m_dn_a_log, dn_dt_bias=m_dn_dt_bias,
             dn_norm_w=m_dn_norm_w, lru_conv_w=m_lru_conv_w, lru_conv_b=m_lru_conv_b, lru_wa=m_lru_wa, lru_ba=m_lru_ba,
             lru_wx=m_lru_wx, lru_bx=m_lru_bx, lru_lambda=m_lru_lambda, w_branch=m_w_branch, w_out=m_w_out, ffn_norm=m_ffn_norm,
             w_up=m_w_up, ffn_conv_w=m_ffn_conv_w, ffn_conv_b=m_ffn_conv_b, w_down=m_w_down, final_norm=m_final_norm)
    v = dict(attn_norm=v_attn_norm, w_in=v_w_in, dn_conv_w=v_dn_conv_w, dn_a_log=v_dn_a_log, dn_dt_bias=v_dn_dt_bias,
             dn_norm_w=v_dn_norm_w, lru_conv_w=v_lru_conv_w, lru_conv_b=v_lru_conv_b, lru_wa=v_lru_wa, lru_ba=v_lru_ba,
             lru_wx=v_lru_wx, lru_bx=v_lru_bx, lru_lambda=v_lru_lambda, w_branch=v_w_branch, w_out=v_w_out, ffn_norm=v_ffn_norm,
             w_up=v_w_up, ffn_conv_w=v_ffn_conv_w, ffn_conv_b=v_ffn_conv_b, w_down=v_w_down, final_norm=v_final_norm)

    me = 4 * lax.axis_index("x") + 2 * lax.axis_index("y") + lax.axis_index("c")
    axes = dict(BIG + SMALL_SHARDED)
    conv_names = [k for k, _ in SMALL_SHARDED]

    gathers, tokens = {}, []
    for layer in range(DEPTH):
        first = [w["w_in"][layer].astype(bf16), _pack([w[k][layer] for k in conv_names], f32)]
        rest = [w[k][layer].astype(bf16) for k in REST]
        for part, srcs in (("in", first), ("rest", rest)):
            gathers[layer, part], token = exchange_start(srcs, [False] * len(srcs), "gather_%s_start%d" % (part, layer))
            tokens.append(token[0:1, 0:1])
    all_started = functools.reduce(lambda a, b: a + b, tokens)

    def joined(own, land, axis):
        return jnp.concatenate([jnp.where(me == p, own, land[p]) for p in range(N_DEV)], axis=axis)

    def layer_weights(layer, x_in):
        own, lands = exchange_wait(gathers[layer, "in"], x_in, "gather_in_wait%d" % layer)
        lw = {"w_in": joined(own[0], lands[0], 1)}
        shapes = [w[k].shape[1:] for k in conv_names]
        per_dev = [_unpack(jnp.where(me == p, own[1], lands[1][p]), shapes) for p in range(N_DEV)]
        lw.update({k: jnp.concatenate([per_dev[p][i] for p in range(N_DEV)], axis=axes[k] - 1) for i, k in enumerate(conv_names)})
        lw.update({k: w[k][layer] for k in REPLICATED if k != "final_norm"})
        if layer == 0:
            lw["attn_norm"] = lw["attn_norm"] + all_started[0]

        def fetch_rest(after):
            own_r, lands_r = exchange_wait(gathers[layer, "rest"], after, "gather_rest_wait%d" % layer)
            return {k: joined(own_r[i], lands_r[i], axes[k] - 1) for i, k in enumerate(REST)}

        return lw, fetch_rest

    small_names = conv_names + [k for k in REPLICATED if k != "final_norm"]
    groups = {"ffn": ("w_up", "w_down"), "mix": ("w_in", "w_branch", "w_out")}
    scatters, small_shapes, head = {}, {}, {}

    def on_grads(layer, group, gr):
        if group == "small":
            small = [gr[k] for k in small_names] + ([head["loss"].reshape(1), head["d_final"]] if layer == DEPTH - 1 else [])
            small_shapes[layer] = [a.shape for a in small]
            srcs, modes = [_pack(small, f32)], [False]
        else:
            srcs = [jnp.stack(_split8(gr[k], axes[k] - 1)).astype(bf16) for k in groups[group]]
            modes = [True] * len(srcs)
        scatters[layer, group], token = exchange_start(srcs, modes, "scatter_%s_start%d" % (group, layer))
        return token

    grad_x = forward_backward(x, positions, loss_target, layer_weights, final_norm,
                              lambda loss_part, d_final: head.update(loss=loss_part, d_final=d_final), on_grads)

    big_sums, small_sums = {}, {}
    for group in ("ffn", "mix"):
        for layer in reversed(range(DEPTH)):
            own, lands = exchange_wait(scatters[layer, group], grad_x, "scatter_%s_wait%d" % (group, layer))
            for i, k in enumerate(groups[group]):
                shard = w[k].shape[1:]
                mine = lax.dynamic_index_in_dim(own[i], me, axis=0, keepdims=False).reshape(-1, shard[-1])
                big_sums[layer, k] = sum_slots(lands[i].reshape(N_DEV, -1, shard[-1]), "sum_" + k, own=mine).reshape(shard)
    for layer in reversed(range(DEPTH)):
        own, lands = exchange_wait(scatters[layer, "small"], grad_x, "scatter_small_wait%d" % layer)
        small_sums[layer] = _unpack(sum_slots(lands[0], "sum_small", own=own[0]), small_shapes[layer])
    grads = {k: jnp.stack([big_sums[layer, k] for layer in range(DEPTH)]) for k, _ in BIG}
    loss, grads["final_norm"] = small_sums[DEPTH - 1][len(small_names)][0], small_sums[DEPTH - 1][len(small_names) + 1]
    for i, k in enumerate(small_names):
        g = jnp.stack([small_sums[layer][i] for layer in range(DEPTH)])
        ax = dict(SMALL_SHARDED).get(k)
        if ax is None:
            grads[k] = g
        else:
            size = g.shape[ax] // N_DEV
            grads[k] = lax.dynamic_slice_in_dim(g, me * size, size, axis=ax)

    upd = {k: adamw(w[k], grads[k], m[k], v[k], "adamw_" + k) for k in WEIGHTS}
    return (loss, grad_x, *[grads[k] for k in WEIGHTS], *[upd[k][0] for k in WEIGHTS], *[upd[k][1] for k in WEIGHTS],
            *[upd[k][2] for k in WEIGHTS])
```

```python
import functools
import math

import jax
import jax.numpy as jnp
from jax import lax
from jax.experimental import pallas as pl
from jax.experimental.pallas import tpu as pltpu

f32 = jnp.float32
bf16 = jnp.bfloat16

D_MODEL = 1024
DEPTH = 4
CHUNK = 64
EPS = 1e-6
DN_HEADS, DN_DK = 4, 128
RET_HEADS, RET_DK, RET_DV = 4, 64, 128
ROPE_BASE = 10000.0
LRU_C = 8.0
D_FF = 2816
N_DEV = 8
LANE = 128
VMEM_LIMIT = 56 * 1024 * 1024

ADAM_LR, ADAM_B1, ADAM_B2, ADAM_EPS, ADAM_WD, ADAM_STEP = 0.001, 0.9, 0.999, 1e-8, 0.01, 10

U_GATES, U_QKV, U_RV, U_RG, U_Z, U_CX, U_CG, U_RQ, U_RK, U_AB = (
    0, 3072, 4608, 5120, 5632, 6144, 6656, 7168, 7424, 7680)
U_PAD = 8192
_IN_SEGS = ((0, 1536, U_QKV), (1536, 8, U_AB), (1544, 512, U_Z), (2056, 256, U_RQ), (2312, 256, U_RK),
            (2568, 512, U_RV), (3080, 512, U_RG), (3592, 512, U_CX), (4104, 512, U_CG), (4616, 3072, U_GATES))
N_IN = 7688


def _params():
    return pltpu.CompilerParams(vmem_limit_bytes=VMEM_LIMIT)


def _pick(dim, pref):
    best = None
    for d in range(LANE, min(dim, pref) + 1, LANE):
        if dim % d == 0:
            best = d
    return best if best is not None else dim


@functools.partial(jax.custom_vjp, nondiff_argnums=(1, 2))
def sroll(x, shift, axis):
    return pltpu.roll(x, shift, axis)


def _sroll_fwd(x, shift, axis):
    return pltpu.roll(x, shift, axis), None


def _sroll_bwd(shift, axis, _, g):
    n = g.shape[axis]
    return (pltpu.roll(g, (n - shift) % n, axis),)


sroll.defvjp(_sroll_fwd, _sroll_bwd)

_DIMS = {"nn": (((1,), (0,)), ((), ())), "nt": (((1,), (1,)), ((), ())), "tn": (((0,), (0,)), ((), ()))}


def _dg(a, b, dims):
    return lax.dot_general(a.astype(bf16), b.astype(bf16), _DIMS[dims], preferred_element_type=f32)


@functools.partial(jax.custom_vjp, nondiff_argnums=(2,))
def bdot(a, b, dims):
    return _dg(a, b, dims)


def _bdot_fwd(a, b, dims):
    return _dg(a, b, dims), (a.astype(bf16), b.astype(bf16))


def _bdot_bwd(dims, res, g):
    a, b = res
    if dims == "nn":
        return _dg(g, b, "nt"), _dg(a, g, "tn")
    if dims == "nt":
        return _dg(g, b, "nn"), _dg(g, a, "tn")
    return _dg(b, g, "nt"), _dg(a, g, "nn")


bdot.defvjp(_bdot_fwd, _bdot_bwd)


def _fdot(a, b, dims):
    return lax.dot_general(a, b, _DIMS[dims], precision=lax.Precision.HIGH, preferred_element_type=f32)


@jax.custom_vjp
def unit_lower_inv_all(mats):
    shape = mats[0].shape
    row = lax.broadcasted_iota(jnp.int32, shape, 0)
    col = lax.broadcasted_iota(jnp.int32, shape, 1)
    eye = jnp.where(row == col, 1.0, 0.0).astype(f32)
    n = [-a for a in mats]
    p = [eye + x for x in n]
    span = 2
    while span < shape[0]:
        n = [_fdot(x, x, "nn") for x in n]
        p = [y + _fdot(y, x, "nn") for y, x in zip(p, n)]
        span *= 2
    return p


def _uli_fwd(mats):
    x = unit_lower_inv_all(mats)
    return x, x


def _uli_bwd(xs, gs):
    t = [_fdot(x, g, "tn") for x, g in zip(xs, gs)]
    return ([-_fdot(y, x, "nt") for y, x in zip(t, xs)],)


unit_lower_inv_all.defvjp(_uli_fwd, _uli_bwd)


def cumsum_rows(x):
    rows = x.shape[0]
    row = lax.broadcasted_iota(jnp.int32, x.shape, 0)
    s = 1
    while s < rows:
        x = x + jnp.where(row >= s, sroll(x, s, 0), 0.0)
        s *= 2
    return x


def _expm1(x):
    return jnp.tanh(0.5 * x) * (jnp.exp(x) + 1.0)


def _lane_pick(x, lane):
    idx = lax.broadcasted_iota(jnp.int32, x.shape, 1)
    return jnp.sum(jnp.where(idx == lane, x, 0.0), axis=1, keepdims=True)


def _row_pick(x, r):
    idx = lax.broadcasted_iota(jnp.int32, x.shape, 0)
    return jnp.sum(jnp.where(idx == r, x, 0.0), axis=0, keepdims=True)


def _causal_conv(x, halo, w, width):
    xe = jnp.concatenate([halo, x], axis=0)
    acc = xe * w[width - 1:width]
    for k in range(width - 1):
        acc = acc + sroll(xe, width - 1 - k, 0) * w[k:k + 1]
    return acc[8:]


def f_norm(ins, ps):
    (x,), (g,) = ins, ps
    return [x * lax.rsqrt(jnp.mean(x * x, axis=-1, keepdims=True) + EPS) * g]


def f_dn_pre(kind, mains, halos, ps):
    y = _causal_conv(mains[0], halos[0], ps[0], 4)
    y = y * jax.nn.sigmoid(y)
    if kind < 2:
        y = y * lax.rsqrt(jnp.sum(y * y, axis=-1, keepdims=True) + EPS)
    if kind == 0:
        y = y * (DN_DK ** -0.5)
    return [y]


def f_dn_gates(ins, ps):
    u, p = ins[0][:, :LANE], ps[0]
    lane = lax.broadcasted_iota(jnp.int32, u.shape, 1)
    g = -jnp.exp(p[0:1]) * jax.nn.softplus(u + p[1:2])
    beta = jax.nn.sigmoid(u)
    return [jnp.where(lane < 4, g, jnp.where(lane < 8, beta, 0.0))]


def per_head(fn):
    def tile_fn(vals, ps):
        heads = [fn([v[:, h * LANE:(h + 1) * LANE] for v in vals], ps) for h in range(vals[0].shape[1] // LANE)]
        return [jnp.concatenate([o[i] for o in heads], axis=1) for i in range(len(heads[0]))]
    return tile_fn


def f_dn_post(ins, ps):
    (o, z), (nw,) = ins, ps
    y = o * lax.rsqrt(jnp.mean(o * o, axis=-1, keepdims=True) + EPS) * nw
    return [y * (z * jax.nn.sigmoid(z))]


def _rot_half(t):
    lane = lax.broadcasted_iota(jnp.int32, t.shape, 1)
    width = t.shape[1]
    first = (lane % RET_DK) < (RET_DK // 2)
    return jnp.where(first, -sroll(t, width - RET_DK // 2, 1), sroll(t, RET_DK // 2, 1))


def f_ret_pre(ins, ps):
    q, k, cos, sin = ins
    qr = q * cos + _rot_half(q) * sin
    kr = (k * cos + _rot_half(k) * sin) * (RET_DK ** -0.5)
    return [qr, kr]


def f_ret_post(ins, ps):
    o, g = ins
    mu = jnp.mean(o, axis=-1, keepdims=True)
    var = jnp.mean(jnp.square(o - mu), axis=-1, keepdims=True)
    return [(o - mu) * lax.rsqrt(var + EPS) * (g * jax.nn.sigmoid(g))]


def f_lru_pre(mains, halos, ps):
    cw, cb, wa, ba, wx, bx, lam = ps
    xc = _causal_conv(mains[0], halos[0], cw, 4) + cb
    r = jax.nn.sigmoid(bdot(xc, wa, "nn") + ba)
    i = jax.nn.sigmoid(bdot(xc, wx, "nn") + bx)
    log_a = -LRU_C * r * jax.nn.softplus(-lam)
    a = jnp.exp(log_a)
    b = jnp.sqrt(-_expm1(2.0 * log_a)) * (i * xc)
    return [a, b]


def f_lru_post(ins, ps):
    h, g = ins
    return [h * jax.nn.gelu(g)]


def f_merge(ins, ps):
    g0, g1, g2, b0, b1, b2 = ins
    return [jax.nn.sigmoid(g0) * b0 + jax.nn.sigmoid(g1) * b1 + jax.nn.sigmoid(g2) * b2]


def f_ffn_mid(mains, halos, ps):
    cwg, cwv, cbg, cbv = ps
    gate = _causal_conv(mains[0], halos[0], cwg, 3) + cbg
    val = _causal_conv(mains[1], halos[1], cwv, 3) + cbv
    return [gate * jax.nn.sigmoid(gate) * val]


def mm(a, b, dims, name, add=None, dep=None, tm=1536, tn=1536, tk=2816):
    if dims == "tn":
        kdim, m = a.shape
        n = b.shape[1]
    else:
        m, kdim = a.shape
        n = b.shape[0] if dims == "nt" else b.shape[1]
    tm, tn, tk = _pick(m, tm), _pick(n, tn), _pick(kdim, tk)
    nk = kdim // tk
    a_spec = pl.BlockSpec((tk, tm), lambda i, j, k: (k, i)) if dims == "tn" else pl.BlockSpec((tm, tk), lambda i, j, k: (i, k))
    b_spec = pl.BlockSpec((tn, tk), lambda i, j, k: (j, k)) if dims == "nt" else pl.BlockSpec((tk, tn), lambda i, j, k: (k, j))
    o_spec = pl.BlockSpec((tm, tn), lambda i, j, k: (i, j))
    has_add, has_dep = add is not None, dep is not None

    def body(*refs):
        a_ref, b_ref = refs[:2]
        add_ref = refs[2] if has_add else None
        o_ref = refs[2 + has_add + has_dep]
        if nk == 1:
            prod = _dg(a_ref[...], b_ref[...], dims)
            o_ref[...] = prod + add_ref[...] if has_add else prod
            return
        acc_ref = refs[-1]
        k = pl.program_id(2)

        @pl.when(k == 0)
        def _():
            acc_ref[...] = jnp.zeros_like(acc_ref)

        acc_ref[...] += _dg(a_ref[...], b_ref[...], dims)

        @pl.when(k == nk - 1)
        def _():
            o_ref[...] = acc_ref[...] + add_ref[...] if has_add else acc_ref[...]

    args = [a, b] + ([add] if has_add else []) + ([dep] if has_dep else [])
    in_specs = [a_spec, b_spec] + ([o_spec] if has_add else [])
    in_specs += [pl.BlockSpec((8, LANE), lambda i, j, k: (0, 0))] if has_dep else []
    return pl.pallas_call(
        body, name=name, grid=(m // tm, n // tn, nk), in_specs=in_specs, out_specs=o_spec,
        out_shape=jax.ShapeDtypeStruct((m, n), f32), scratch_shapes=[pltpu.VMEM((tm, tn), f32)] if nk > 1 else [],
        compiler_params=_params())(*args)


def rowmap(fn, ins, params, outs, ncol, name, rows=256):
    n = ins[0][0].shape[0]
    r = min(rows, n)
    nin, npar = len(ins), len(params)

    def body(*refs):
        vals = [x[...] for x in refs[:nin]]
        pv = [p[...] for p in refs[nin:nin + npar]]
        for o_ref, o in zip(refs[nin + npar:], fn(vals, pv)):
            o_ref[...] = o.astype(o_ref.dtype)

    in_specs = [pl.BlockSpec((r, cb), functools.partial(lambda j, i, off: (i, off + j), off=off)) for _, cb, off in ins]
    in_specs += [pl.BlockSpec(bs, functools.partial(lambda j, i, f: f(j), f=f)) for _, bs, f in params]
    out_specs = [pl.BlockSpec((r, cb), lambda j, i: (i, j)) for cb, _ in outs]
    out_shape = [jax.ShapeDtypeStruct((n, cb * ncol), dt) for cb, dt in outs]
    res = pl.pallas_call(body, name=name, grid=(ncol, n // r), in_specs=in_specs, out_specs=out_specs,
                         out_shape=out_shape, compiler_params=_params())(*[a for a, _, _ in ins], *[a for a, _, _ in params])
    return res


def rowmap_bwd(fn, ins, params, douts, ncol, name, rows=256, add=None, din_dtypes=None, into=None, copy16=None):
    n = ins[0][0].shape[0]
    r = min(rows, n)
    nin, npar, nout = len(ins), len(params), len(douts)
    add = [None] * nin if add is None else list(add)
    add_idx = [i for i in range(nin) if add[i] is not None]
    din_dtypes = [f32] * nin if din_dtypes is None else list(din_dtypes)
    into_buf, into_off, into_idx = into if into is not None else (None, 0, [])
    has_into, has_copy = into is not None, copy16 is not None
    kept = [i for i in range(nin) if din_dtypes[i] is not None and i not in into_idx]

    def body(*refs):
        vals = [x[...] for x in refs[:nin]]
        pv = [p[...] for p in refs[nin:nin + npar]]
        dys = [d[...] for d in refs[nin + npar:nin + npar + nout]]
        k0 = nin + npar + nout
        add_refs = dict(zip(add_idx, refs[k0:k0 + len(add_idx)]))
        k0 += len(add_idx) + has_into
        din_refs = refs[k0:k0 + len(kept)]
        k0 += len(kept)
        copy_ref = refs[k0] if has_copy else None
        into_ref = refs[k0 + has_copy] if has_into else None
        dp_refs = refs[k0 + has_copy + has_into:]
        _, vjp = jax.vjp(fn, vals, pv)
        dvals, dpv = vjp(dys)
        cot = lambda idx: dvals[idx] + add_refs[idx][...] if idx in add_refs else dvals[idx]
        for d_ref, idx in zip(din_refs, kept):
            d_ref[...] = cot(idx).astype(d_ref.dtype)
        if has_copy:
            copy_ref[...] = cot(copy16).astype(copy_ref.dtype)
        if has_into:
            parts = [cot(idx) for idx in into_idx]
            into_ref[...] = (parts[0] if len(parts) == 1 else jnp.concatenate(parts, axis=1)).astype(into_ref.dtype)

        @pl.when(pl.program_id(1) == 0)
        def _():
            for d_ref in dp_refs:
                d_ref[...] = jnp.zeros_like(d_ref)

        for d_ref, d in zip(dp_refs, dpv):
            d_ref[...] += d

    in_specs = [pl.BlockSpec((r, cb), functools.partial(lambda j, i, off: (i, off + j), off=off)) for _, cb, off in ins]
    in_specs += [pl.BlockSpec(bs, functools.partial(lambda j, i, f: f(j), f=f)) for _, bs, f in params]
    in_specs += [pl.BlockSpec((r, d.shape[1] // ncol), lambda j, i: (i, j)) for d in douts]
    in_specs += [pl.BlockSpec((r, ins[i][1]), lambda j, i: (i, j)) for i in add_idx]
    out_specs = [pl.BlockSpec((r, ins[i][1]), lambda j, i: (i, j)) for i in kept]
    out_shape = [jax.ShapeDtypeStruct((n, ins[i][1] * ncol), din_dtypes[i]) for i in kept]
    args = [a for a, _, _ in ins] + [a for a, _, _ in params] + list(douts) + [add[i] for i in add_idx]
    aliases = {}
    if has_copy:
        out_specs += [pl.BlockSpec((r, ins[copy16][1]), lambda j, i: (i, j))]
        out_shape += [jax.ShapeDtypeStruct((n, ins[copy16][1] * ncol), bf16)]
    if has_into:
        assert ncol == 1
        in_specs += [pl.BlockSpec(memory_space=pl.ANY)]
        aliases[len(args)] = len(out_shape)
        args += [into_buf]
        out_specs += [pl.BlockSpec((r, sum(ins[i][1] for i in into_idx)), lambda j, i: (i, into_off))]
        out_shape += [jax.ShapeDtypeStruct(into_buf.shape, into_buf.dtype)]
    pshapes = [tuple(d for d in bs if d is not None) for _, bs, _ in params]
    out_specs += [pl.BlockSpec((None,) + ps, functools.partial(lambda j, i, nd: (j,) + (0,) * nd, nd=len(ps))) for ps in pshapes]
    out_shape += [jax.ShapeDtypeStruct((ncol,) + ps, f32) for ps in pshapes]
    res = pl.pallas_call(body, name=name, grid=(ncol, n // r), in_specs=in_specs, out_specs=out_specs, out_shape=out_shape,
                         input_output_aliases=aliases, compiler_params=_params())(*args)
    dins = [None] * nin
    for pos, i in enumerate(kept):
        dins[i] = res[pos]
    pos = len(kept)
    extras = {}
    if has_copy:
        extras["copy16"] = res[pos]
        pos += 1
    if has_into:
        extras["into"] = res[pos]
        pos += 1
    return dins, res[pos:], extras


SEQ_ROWS = 256


def seqmap(fn, ins, params, nouts, ncol, name, out_dtype=f32):
    bsz, seq, _ = ins[0][0].shape
    r = min(SEQ_ROWS, seq)
    nin, npar = len(ins), len(params)

    def body(*refs):
        in_refs = refs[:nin]
        pv = [p[...] for p in refs[nin:nin + npar]]
        out_refs = refs[nin + npar:]

        def step(i, carry):
            r0 = pl.multiple_of(i * r, r)
            h0 = pl.multiple_of(jnp.maximum(r0 - 8, 0), 8)
            mains = [x[pl.ds(r0, r), :] for x in in_refs]
            halos = [jnp.where(i == 0, 0.0, x[pl.ds(h0, 8), :]) for x in in_refs]
            for o_ref, o in zip(out_refs, fn(mains, halos, pv)):
                o_ref[pl.ds(r0, r), :] = o.astype(o_ref.dtype)
            return carry

        lax.fori_loop(0, seq // r, step, 0)

    in_specs = [pl.BlockSpec((None, seq, LANE), functools.partial(lambda j, b, off: (b, 0, off + j), off=off)) for _, off in ins]
    in_specs += [pl.BlockSpec(bs, functools.partial(lambda j, b, f: f(j), f=f)) for _, bs, f in params]
    out_specs = [pl.BlockSpec((None, seq, LANE), lambda j, b: (b, 0, j)) for _ in range(nouts)]
    out_shape = [jax.ShapeDtypeStruct((bsz, seq, LANE * ncol), out_dtype) for _ in range(nouts)]
    return pl.pallas_call(body, name=name, grid=(ncol, bsz), in_specs=in_specs, out_specs=out_specs,
                          out_shape=out_shape, compiler_params=_params())(*[a for a, _ in ins], *[a for a, _, _ in params])


def seqmap_bwd(fn, ins, params, douts, ncol, name, din_dtype=f32, into=None):
    bsz, seq, _ = ins[0][0].shape
    r = min(SEQ_ROWS, seq)
    nin, npar, nout = len(ins), len(params), len(douts)
    narrow = din_dtype != f32

    def body(*refs):
        in_refs = refs[:nin]
        pv = [p[...] for p in refs[nin:nin + npar]]
        dy_refs = refs[nin + npar:nin + npar + nout]
        k0 = nin + npar + nout + (into is not None)
        dout_refs = refs[k0:k0 + nin]
        dp_refs = refs[k0 + nin:k0 + nin + npar]
        din_refs = refs[k0 + nin + npar:] if narrow else dout_refs

        def step(i, dp_acc):
            r0 = pl.multiple_of(i * r, r)
            h0 = pl.multiple_of(jnp.maximum(r0 - 8, 0), 8)
            mains = [x[pl.ds(r0, r), :] for x in in_refs]
            halos_raw = [x[pl.ds(h0, 8), :] for x in in_refs]

            def tile(mains, halos_raw, pv):
                return fn(mains, [jnp.where(i == 0, 0.0, h) for h in halos_raw], pv)

            _, vjp = jax.vjp(tile, mains, halos_raw, pv)
            dm, dh, dp = vjp([d[pl.ds(r0, r), :] for d in dy_refs])
            for d_ref, m, h in zip(din_refs, dm, dh):
                d_ref[pl.ds(r0, r), :] = m
                d_ref[pl.ds(h0, 8), :] += h
            return [acc + d for acc, d in zip(dp_acc, dp)]

        dp = lax.fori_loop(0, seq // r, step, [jnp.zeros(p.shape, f32) for p in pv])
        if narrow:
            for o_ref, d_ref in zip(dout_refs, din_refs):
                o_ref[...] = d_ref[...].astype(o_ref.dtype)

        @pl.when(pl.program_id(1) == 0)
        def _():
            for d_ref in dp_refs:
                d_ref[...] = jnp.zeros_like(d_ref)

        for d_ref, d in zip(dp_refs, dp):
            d_ref[...] += d

    in_specs = [pl.BlockSpec((None, seq, LANE), functools.partial(lambda j, b, off: (b, 0, off + j), off=off)) for _, off in ins]
    in_specs += [pl.BlockSpec(bs, functools.partial(lambda j, b, f: f(j), f=f)) for _, bs, f in params]
    in_specs += [pl.BlockSpec((None, seq, LANE), lambda j, b: (b, 0, j)) for _ in range(nout)]
    out_specs = [pl.BlockSpec((None, seq, LANE), lambda j, b: (b, 0, j)) for _ in range(nin)]
    pshapes = [tuple(d for d in bs if d is not None) for _, bs, _ in params]
    out_specs += [pl.BlockSpec((None,) + ps, functools.partial(lambda j, b, nd: (j,) + (0,) * nd, nd=len(ps))) for ps in pshapes]
    out_shape = [jax.ShapeDtypeStruct((bsz, seq, LANE * ncol), din_dtype) for _ in range(nin)]
    out_shape += [jax.ShapeDtypeStruct((ncol,) + ps, f32) for ps in pshapes]
    args = [a for a, _ in ins] + [a for a, _, _ in params] + list(douts)
    aliases = {}
    if into is not None:
        assert nin == 1 and into[0].dtype == din_dtype
        in_specs += [pl.BlockSpec(memory_space=pl.ANY)]
        aliases[len(args)] = 0
        args += [into[0]]
        out_specs[0] = pl.BlockSpec((None, seq, LANE), lambda j, b: (b, 0, into[1] + j))
        out_shape[0] = jax.ShapeDtypeStruct(into[0].shape, din_dtype)
    res = pl.pallas_call(body, name=name, grid=(ncol, bsz), in_specs=in_specs, out_specs=out_specs, out_shape=out_shape,
                         scratch_shapes=[pltpu.VMEM((seq, LANE), f32) for _ in range(nin)] if narrow else [],
                         input_output_aliases=aliases, compiler_params=_params())(*args)
    return res[:nin], res[nin:]


def _tri_masks():
    row = lax.broadcasted_iota(jnp.int32, (CHUNK, CHUNK), 0)
    col = lax.broadcasted_iota(jnp.int32, (CHUNK, CHUNK), 1)
    return row >= col, row > col


CHUNKS_PER_STEP = 4


def _by_rows(parts, per_row):
    rows = [jnp.concatenate(parts[i:i + per_row], axis=1) for i in range(0, len(parts), per_row)]
    return jnp.concatenate(rows, axis=0)


def dn_prep(vals, ps):
    q, k, v, gb = vals
    nchunk = q.shape[0] // CHUNK
    causal, strict = _tri_masks()
    gbs = [gb[c * CHUNK:(c + 1) * CHUNK] for c in range(nchunk)]
    gcs = [cumsum_rows(g) for g in gbs]
    gcts = [g.T for g in gcs]
    chains = [(c, h) for c in range(nchunk) for h in range(DN_HEADS)]
    part = lambda t, c, h: t[c * CHUNK:(c + 1) * CHUNK, h * DN_DK:(h + 1) * DN_DK]
    qh = [part(q, c, h) for c, h in chains]
    kh = [part(k, c, h) for c, h in chains]
    vh = [part(v, c, h) for c, h in chains]
    g_col = [_lane_pick(gcs[c], h) for c, h in chains]
    beta = [_lane_pick(gbs[c], DN_HEADS + h) for c, h in chains]
    g_row = [_row_pick(gcts[c], h)[:, :CHUNK] for c, h in chains]
    decay = [jnp.where(causal, jnp.exp(jnp.where(causal, gc - gr, 0.0)), 0.0) for gc, gr in zip(g_col, g_row)]
    k_beta = [a * b for a, b in zip(kh, beta)]
    eg = [jnp.exp(g) for g in g_col]
    kk = [bdot(a, b, "nt") for a, b in zip(k_beta, kh)]
    qk = [bdot(a, b, "nt") for a, b in zip(qh, kh)]
    t_inv = unit_lower_inv_all([jnp.where(strict, a * d, 0.0) for a, d in zip(kk, decay)])
    u = [bdot(t, a * b, "nn") for t, a, b in zip(t_inv, vh, beta)]
    w = [bdot(t, a * e, "nn") for t, a, e in zip(t_inv, k_beta, eg)]
    attn = [jnp.concatenate([a * d, jnp.zeros((CHUNK, DN_DK - CHUNK), f32)], axis=1) for a, d in zip(qk, decay)]
    qd = [a * e for a, e in zip(qh, eg)]
    kd = [a * jnp.exp(_row_pick(g, CHUNK - 1) - g) for a, g in zip(kh, g_col)]
    g_last = jnp.concatenate([jnp.broadcast_to(_row_pick(g, CHUNK - 1), g.shape) for g in gcs], axis=0)
    return [_by_rows(t, DN_HEADS) for t in (u, w, attn, qd, kd)] + [g_last]


def dn_step(state, u, w, attn, qd, kd, g_last):
    bsz = u.shape[0]
    chains = [(b, h) for b in range(bsz) for h in range(DN_HEADS)]
    part = lambda t, b, h: t[b, :, h * DN_DK:(h + 1) * DN_DK]
    ws = [bdot(part(w, b, h), s, "nn") for (b, h), s in zip(chains, state)]
    qs = [bdot(part(qd, b, h), s, "nn") for (b, h), s in zip(chains, state)]
    v_new = [part(u, b, h) - x for (b, h), x in zip(chains, ws)]
    av = [bdot(attn[b, :, h * DN_DK:h * DN_DK + CHUNK], x, "nn") for (b, h), x in zip(chains, v_new)]
    kv = [bdot(part(kd, b, h), x, "tn") for (b, h), x in zip(chains, v_new)]
    ge = [jnp.exp(_row_pick(_lane_pick(g_last[b], h), 0)) for b, h in chains]
    new_state = [s * g + x for s, g, x in zip(state, ge, kv)]
    outs = [a + b for a, b in zip(qs, av)]
    return new_state, jnp.concatenate([jnp.concatenate(outs[b * DN_HEADS:(b + 1) * DN_HEADS], axis=1)[None]
                                       for b in range(bsz)], axis=0)


def _ret_log_gamma(h):
    return math.log(1.0 - 2.0 ** (-5.0 - h))


def ret_prep(vals, ps):
    q, k, v = vals
    nchunk = q.shape[0] // CHUNK
    causal, _ = _tri_masks()
    row = lax.broadcasted_iota(jnp.int32, (CHUNK, CHUNK), 0)
    col = lax.broadcasted_iota(jnp.int32, (CHUNK, CHUNK), 1)
    dist = (row - col).astype(f32)
    lane = lax.broadcasted_iota(jnp.int32, (CHUNK, q.shape[1]), 1)
    dmask = [jnp.where(causal, jnp.exp(jnp.where(causal, dist, 0.0) * _ret_log_gamma(h)), 0.0) for h in range(RET_HEADS)]
    chains = [(c, h) for c in range(nchunk) for h in range(RET_HEADS)]
    rows = lambda t, c: t[c * CHUNK:(c + 1) * CHUNK]
    scores = [bdot(jnp.where((lane // RET_DK) == h, rows(q, c), 0.0), rows(k, c), "nt") * dmask[h] for c, h in chains]
    inner = [bdot(s, rows(v, c)[:, h * RET_DV:(h + 1) * RET_DV], "nn") for s, (c, h) in zip(scores, chains)]
    return [_by_rows(inner, RET_HEADS)]


def ret_step(state, q, k, v, inner):
    bsz = q.shape[0]
    idx = lax.broadcasted_iota(jnp.int32, (CHUNK, 1), 0).astype(f32)
    lane = lax.broadcasted_iota(jnp.int32, (CHUNK, q.shape[2]), 1)
    chains = [(b, h) for b in range(bsz) for h in range(RET_HEADS)]
    part = lambda t, b, h: t[b, :, h * RET_DV:(h + 1) * RET_DV]
    cross = [bdot(q[b], s, "nn") for (b, h), s in zip(chains, state)]
    kz = [jnp.where((lane // RET_DK) == h, k[b], 0.0) * jnp.exp((CHUNK - 1.0 - idx) * _ret_log_gamma(h)) for b, h in chains]
    kv = [bdot(a, part(v, b, h), "tn") for a, (b, h) in zip(kz, chains)]
    outs = [x * jnp.exp((idx + 1.0) * _ret_log_gamma(h)) + part(inner, b, h) for x, (b, h) in zip(cross, chains)]
    new_state = [s * math.exp(CHUNK * _ret_log_gamma(h)) + x for s, x, (b, h) in zip(state, kv, chains)]
    return new_state, jnp.concatenate([jnp.concatenate(outs[b * RET_HEADS:(b + 1) * RET_HEADS], axis=1)[None]
                                       for b in range(bsz)], axis=0)


def chunk_scan(step_fn, ins, state_shape, out_width, name):
    bsz, seq, _ = ins[0].shape
    nchunk = seq // CHUNK
    nin = len(ins)
    nh = state_shape[0]

    def body(*refs):
        in_refs = refs[:nin]
        o_ref, ck_ref, s_ref = refs[nin:]

        @pl.when(pl.program_id(0) == 0)
        def _():
            s_ref[...] = jnp.zeros_like(s_ref)

        state = [s_ref[i] for i in range(bsz * nh)]
        for i in range(bsz * nh):
            ck_ref[i // nh, i % nh] = state[i]
        new_state, out = step_fn(state, *[x[...] for x in in_refs])
        o_ref[...] = out
        for i in range(bsz * nh):
            s_ref[i] = new_state[i]

    in_specs = [pl.BlockSpec((bsz, CHUNK, x.shape[2]), lambda n: (0, n, 0)) for x in ins]
    out_specs = [pl.BlockSpec((bsz, CHUNK, out_width), lambda n: (0, n, 0)),
                 pl.BlockSpec((bsz, None) + tuple(state_shape), lambda n: (0, n, 0, 0, 0))]
    out_shape = [jax.ShapeDtypeStruct((bsz, seq, out_width), f32),
                 jax.ShapeDtypeStruct((bsz, nchunk) + tuple(state_shape), f32)]
    return pl.pallas_call(body, name=name, grid=(nchunk,), in_specs=in_specs, out_specs=out_specs, out_shape=out_shape,
                          scratch_shapes=[pltpu.VMEM((bsz * nh,) + tuple(state_shape[1:]), f32)],
                          compiler_params=_params())(*ins)


def chunk_scan_bwd(step_fn, ins, ckpt, dout, name):
    bsz, seq, _ = ins[0].shape
    nchunk = seq // CHUNK
    nin = len(ins)
    state_shape = ckpt.shape[2:]
    nh = state_shape[0]

    def body(*refs):
        in_refs = refs[:nin]
        ck_ref, do_ref = refs[nin:nin + 2]
        din_refs = refs[nin + 2:nin + 2 + nin]
        ds_ref = refs[-1]

        @pl.when(pl.program_id(0) == 0)
        def _():
            ds_ref[...] = jnp.zeros_like(ds_ref)

        state = [ck_ref[i // nh, i % nh] for i in range(bsz * nh)]
        _, vjp = jax.vjp(step_fn, state, *[x[...] for x in in_refs])
        grads = vjp(([ds_ref[i] for i in range(bsz * nh)], do_ref[...]))
        for i in range(bsz * nh):
            ds_ref[i] = grads[0][i]
        for d_ref, d in zip(din_refs, grads[1:]):
            d_ref[...] = d

    rev = lambda n: (0, nchunk - 1 - n, 0)
    in_specs = [pl.BlockSpec((bsz, CHUNK, x.shape[2]), rev) for x in ins]
    in_specs += [pl.BlockSpec((bsz, None) + tuple(state_shape), lambda n: (0, nchunk - 1 - n, 0, 0, 0)),
                 pl.BlockSpec((bsz, CHUNK, dout.shape[2]), rev)]
    out_specs = [pl.BlockSpec((bsz, CHUNK, x.shape[2]), rev) for x in ins]
    out_shape = [jax.ShapeDtypeStruct(x.shape, f32) for x in ins]
    return pl.pallas_call(body, name=name, grid=(nchunk,), in_specs=in_specs, out_specs=out_specs, out_shape=out_shape,
                          scratch_shapes=[pltpu.VMEM((bsz * nh,) + tuple(state_shape[1:]), f32)],
                          compiler_params=_params())(*ins, ckpt, dout)


LRU_ROWS = 512


def lru_scan(a, b):
    bsz, seq, width = a.shape
    rb = min(LRU_ROWS, seq)

    def body(a_ref, b_ref, h_ref, hp_ref, carry_ref):
        @pl.when(pl.program_id(1) == 0)
        def _():
            carry_ref[...] = jnp.zeros_like(carry_ref)

        row = lax.broadcasted_iota(jnp.int32, (8, width), 0)

        def tile(t, hprev):
            r0 = pl.multiple_of(t * 8, 8)
            ca, cbv = a_ref[pl.ds(r0, 8), :], b_ref[pl.ds(r0, 8), :]
            for s in (1, 2, 4):
                m = row >= s
                cbv = jnp.where(m, ca * pltpu.roll(cbv, s, 0) + cbv, cbv)
                ca = jnp.where(m, ca * pltpu.roll(ca, s, 0), ca)
            h = cbv + ca * hprev
            h_ref[pl.ds(r0, 8), :] = h
            hp_ref[pl.ds(r0, 8), :] = jnp.where(row == 0, hprev, pltpu.roll(h, 1, 0))
            return _row_pick(h, 7)

        carry_ref[0:1, :] = lax.fori_loop(0, rb // 8, tile, carry_ref[0:1, :])

    spec = pl.BlockSpec((None, rb, width), lambda bi, i: (bi, i, 0))
    return pl.pallas_call(body, name="lru_scan", grid=(bsz, seq // rb), in_specs=[spec, spec], out_specs=[spec, spec],
                          out_shape=[jax.ShapeDtypeStruct(a.shape, f32)] * 2,
                          scratch_shapes=[pltpu.VMEM((8, width), f32)], compiler_params=_params())(a, b)


def lru_scan_bwd(a, hp, dh):
    bsz, seq, width = a.shape
    rb = min(LRU_ROWS, seq)
    nblk = seq // rb

    def body(a_ref, hp_ref, dh_ref, da_ref, db_ref, carry_ref):
        @pl.when(pl.program_id(1) == 0)
        def _():
            carry_ref[...] = jnp.zeros_like(carry_ref)

        row = lax.broadcasted_iota(jnp.int32, (8, width), 0)
        ntile = rb // 8

        def tile(t, mu_next):
            r0 = pl.multiple_of((ntile - 1 - t) * 8, 8)
            ca, dh_t = a_ref[pl.ds(r0, 8), :], dh_ref[pl.ds(r0, 8), :]
            cbv = ca * dh_t
            for s in (1, 2, 4):
                m = row < 8 - s
                cbv = jnp.where(m, ca * pltpu.roll(cbv, 8 - s, 0) + cbv, cbv)
                ca = jnp.where(m, ca * pltpu.roll(ca, 8 - s, 0), ca)
            mu = cbv + ca * mu_next
            lam = dh_t + jnp.where(row == 7, mu_next, pltpu.roll(mu, 7, 0))
            db_ref[pl.ds(r0, 8), :] = lam
            da_ref[pl.ds(r0, 8), :] = lam * hp_ref[pl.ds(r0, 8), :]
            return _row_pick(mu, 0)

        carry_ref[0:1, :] = lax.fori_loop(0, ntile, tile, carry_ref[0:1, :])

    spec = pl.BlockSpec((None, rb, width), lambda bi, i: (bi, nblk - 1 - i, 0))
    return pl.pallas_call(body, name="lru_scan_bwd", grid=(bsz, nblk), in_specs=[spec] * 3, out_specs=[spec, spec],
                          out_shape=[jax.ShapeDtypeStruct(a.shape, f32)] * 2,
                          scratch_shapes=[pltpu.VMEM((8, width), f32)], compiler_params=_params())(a, hp, dh)


def final_loss(x, g, target):
    n, d = x.shape
    r = min(256, n)

    def body(x_ref, g_ref, t_ref, loss_ref, dx_ref, dg_ref, dx16_ref):
        @pl.when(pl.program_id(0) == 0)
        def _():
            loss_ref[...] = jnp.zeros_like(loss_ref)
            dg_ref[...] = jnp.zeros_like(dg_ref)

        tgt = t_ref[...]

        def loss_fn(xv, gv):
            y = f_norm([xv], [gv])[0]
            return 0.5 * jnp.sum(jnp.mean(jnp.square(y - tgt), axis=-1, keepdims=True), axis=0, keepdims=True)

        val, vjp = jax.vjp(loss_fn, x_ref[...], g_ref[...])
        dx, dg = vjp(jnp.ones_like(val))
        loss_ref[...] += val
        dx_ref[...] = dx
        dx16_ref[...] = dx.astype(dx16_ref.dtype)
        dg_ref[...] += dg

    row = pl.BlockSpec((r, d), lambda i: (i, 0))
    return pl.pallas_call(
        body, name="final_loss", grid=(n // r,), in_specs=[row, pl.BlockSpec((1, d), lambda i: (0, 0)), row],
        out_specs=[pl.BlockSpec((8, LANE), lambda i: (0, 0)), row, pl.BlockSpec((1, d), lambda i: (0, 0)), row],
        out_shape=[jax.ShapeDtypeStruct((8, LANE), f32), jax.ShapeDtypeStruct((n, d), f32), jax.ShapeDtypeStruct((1, d), f32),
                   jax.ShapeDtypeStruct((n, d), bf16)],
        compiler_params=_params())(x, g, target)


_HBM = pl.BlockSpec(memory_space=pltpu.HBM)
_SEM = pl.BlockSpec(memory_space=pltpu.SEMAPHORE)
_EFFECT = pltpu.SideEffectType.DATAFLOW_SIDE_EFFECTING


def _peer(k):
    mx, my, mc = lax.axis_index("x"), lax.axis_index("y"), lax.axis_index("c")
    px, py, pc = (mx + (k >> 2)) % 2, (my + ((k >> 1) & 1)) % 2, (mc + (k & 1)) % 2
    return (px, py, pc), 4 * px + 2 * py + pc


def _peer_copy(k, i, x_ref, land_ref, send_sems, recv_sems, scatter):
    me = 4 * lax.axis_index("x") + 2 * lax.axis_index("y") + lax.axis_index("c")
    dev, slot = _peer(k)
    sem = i * (N_DEV - 1) + k - 1
    return pltpu.make_async_remote_copy(
        src_ref=x_ref.at[slot] if scatter else x_ref, dst_ref=land_ref.at[me], send_sem=send_sems.at[sem],
        recv_sem=recv_sems.at[sem], device_id=dev, device_id_type=pl.DeviceIdType.MESH)


def exchange_start(xs, scatters, name):
    nx = len(xs)
    lands = [lax.empty((N_DEV,) + tuple(x.shape[1:] if sc else x.shape), x.dtype) for x, sc in zip(xs, scatters)]
    nsem = nx * (N_DEV - 1)

    def body(*refs):
        x_refs, land_refs = refs[:nx], refs[nx:2 * nx]
        send_sems, recv_sems = refs[2 * nx:2 * nx + 2]
        token = refs[-1]
        for i in range(nx):
            for k in range(1, N_DEV):
                _peer_copy(k, i, x_refs[i], land_refs[i], send_sems, recv_sems, scatters[i]).start()
        token[...] = jnp.zeros_like(token)

    hbm = lambda a: pltpu.HBM(a.shape, a.dtype)
    res = pl.pallas_call(
        body, name=name, in_specs=(_HBM,) * (2 * nx),
        out_specs=(_SEM, _SEM) + (_HBM,) * (2 * nx) + (pl.BlockSpec(memory_space=pltpu.VMEM),),
        input_output_aliases={i: 2 + i for i in range(2 * nx)},
        out_shape=(pltpu.SemaphoreType.DMA((nsem,)), pltpu.SemaphoreType.DMA((nsem,)), *[hbm(a) for a in xs],
                   *[hbm(a) for a in lands], jax.ShapeDtypeStruct((8, LANE), f32)),
        compiler_params=pltpu.CompilerParams(has_side_effects=_EFFECT),
    )(*[pltpu.with_memory_space_constraint(a, pltpu.HBM) for a in list(xs) + lands])
    return (res[0], res[1], list(res[2:2 + nx]), list(res[2 + nx:2 + 2 * nx]), tuple(scatters)), res[-1]


def exchange_wait(started, after, name):
    send_sems, recv_sems, x_thrus, land_thrus, scatters = started
    nx = len(x_thrus)

    def body(*refs):
        x_refs, land_refs = refs[:nx], refs[nx:2 * nx]
        send_sems, recv_sems = refs[2 * nx:2 * nx + 2]
        for i in range(nx):
            for k in range(1, N_DEV):
                cp = _peer_copy(k, i, x_refs[i], land_refs[i], send_sems, recv_sems, scatters[i])
                cp.wait_send()
                cp.wait_recv()

    hbm = lambda a: pltpu.HBM(a.shape, a.dtype)
    res = pl.pallas_call(
        body, name=name, in_specs=(_HBM,) * (2 * nx) + (_SEM, _SEM, pl.BlockSpec(memory_space=pl.ANY)),
        out_specs=(_HBM,) * (2 * nx), input_output_aliases={i: i for i in range(2 * nx)},
        out_shape=tuple(hbm(a) for a in list(x_thrus) + list(land_thrus)),
        compiler_params=pltpu.CompilerParams(has_side_effects=_EFFECT),
    )(*x_thrus, *land_thrus, send_sems, recv_sems, after)
    return list(res[:nx]), list(res[nx:])


def sum_slots(x, name, own=None):
    _, rows_total, cols = x.shape
    row_bytes = N_DEV * ((cols + LANE - 1) // LANE) * LANE * x.dtype.itemsize
    r = _pick_rows(rows_total, max(16, (4 * 1024 * 1024) // row_bytes // 16 * 16))
    if own is not None:
        def body_own(x_ref, own_ref, o_ref):
            me = 4 * lax.axis_index("x") + 2 * lax.axis_index("y") + lax.axis_index("c")
            acc = None
            for s in range(N_DEV):
                v = jnp.where(me == s, own_ref[...], x_ref[s]).astype(f32)
                acc = v if acc is None else acc + v
            o_ref[...] = acc

        return pl.pallas_call(body_own, name=name, grid=(rows_total // r,),
                              in_specs=[pl.BlockSpec((N_DEV, r, cols), lambda i: (0, i, 0)), pl.BlockSpec((r, cols), lambda i: (i, 0))],
                              out_specs=pl.BlockSpec((r, cols), lambda i: (i, 0)),
                              out_shape=jax.ShapeDtypeStruct((rows_total, cols), f32), compiler_params=_params())(x, own)

    def body(x_ref, o_ref):
        acc = x_ref[0].astype(f32)
        for s in range(1, N_DEV):
            acc = acc + x_ref[s].astype(f32)
        o_ref[...] = acc

    return pl.pallas_call(body, name=name, grid=(rows_total // r,),
                          in_specs=[pl.BlockSpec((N_DEV, r, cols), lambda i: (0, i, 0))],
                          out_specs=pl.BlockSpec((r, cols), lambda i: (i, 0)),
                          out_shape=jax.ShapeDtypeStruct((rows_total, cols), f32), compiler_params=_params())(x)


def _pick_rows(total, pref):
    best = None
    for d in range(16, min(total, pref) + 1, 16):
        if total % d == 0:
            best = d
    return best if best is not None else total


def adamw(w, g, m, v, name):
    shape = w.shape
    if w.ndim == 1:
        w2, g2, m2, v2 = (t.reshape(1, -1) for t in (w, g, m, v))
    else:
        w2, g2, m2, v2 = (t.reshape(-1, shape[-1]) for t in (w, g, m, v))
    rows_total, cols = w2.shape
    r = _pick_rows(rows_total, max(16, (512 * 1024) // max(cols, 1) // 16 * 16))
    c1, c2 = 1.0 / (1.0 - ADAM_B1 ** ADAM_STEP), 1.0 / (1.0 - ADAM_B2 ** ADAM_STEP)

    def body(w_ref, g_ref, m_ref, v_ref, d_ref, nm_ref, nv_ref):
        gv = g_ref[...]
        nm = ADAM_B1 * m_ref[...] + (1.0 - ADAM_B1) * gv
        nv = ADAM_B2 * v_ref[...] + (1.0 - ADAM_B2) * jnp.square(gv)
        d_ref[...] = -ADAM_LR * ((nm * c1) / (jnp.sqrt(nv * c2) + ADAM_EPS) + ADAM_WD * w_ref[...])
        nm_ref[...] = nm
        nv_ref[...] = nv

    spec = pl.BlockSpec((r, cols), lambda i: (i, 0))
    outs = pl.pallas_call(body, name=name, grid=(rows_total // r,), in_specs=[spec] * 4, out_specs=[spec] * 3,
                          out_shape=[jax.ShapeDtypeStruct((rows_total, cols), f32)] * 3, compiler_params=_params())(w2, g2, m2, v2)
    return tuple(o.reshape(shape) for o in outs)


def _const(j):
    return lambda _: j


def _layer_fwd(x, wl, fetch_rest, cos, sin, bsz, seq):
    n = x.shape[0]
    sv = {"x_in": x}
    row1 = lambda a: (a, (1, a.shape[1]), lambda j: (0, 0))
    h = rowmap(f_norm, [(x, D_MODEL, 0)], [row1(wl["attn_norm"])], [(D_MODEL, bf16)], 1, "norm_fwd")[0]
    u = mm(h, wl["w_in"], "nn", "mm_in")
    sv["h"], sv["u"] = h, u
    u3 = u.reshape(bsz, seq, U_PAD)
    wl = dict(wl)
    wl.update(fetch_rest(u))
    sv["wl"] = wl

    qkv = []
    for kind in range(3):
        cw = (wl["dn_conv_w"], (4, LANE), functools.partial(lambda j, kind: (0, 4 * kind + j), kind=kind))
        qkv.append(seqmap(functools.partial(f_dn_pre, kind), [(u3, U_QKV // LANE + 4 * kind)], [cw], 1, 4, "dn_pre%d" % kind)[0])
    gb = rowmap(f_dn_gates, [(u, 512, U_AB // 512)], [(wl["dn_gate_p"], (8, LANE), lambda j: (0, 0))], [(LANE, f32)], 1,
                "dn_gates", rows=512)[0]
    gb3 = gb.reshape(bsz, seq, LANE)
    crow = CHUNK * CHUNKS_PER_STEP
    dn_in = [(t.reshape(n, 512), 512, 0) for t in qkv] + [(gb, LANE, 0)]
    prep_a = rowmap(dn_prep, dn_in, [], [(512, f32)] * 5 + [(LANE, f32)], 1, "dn_prep", rows=crow)
    prep_a = [t.reshape(bsz, seq, t.shape[1]) for t in prep_a]
    o_a, ck_a = chunk_scan(dn_step, prep_a, (DN_HEADS, DN_DK, DN_DK), 512, "dn_scan")
    y_a = rowmap(per_head(f_dn_post), [(o_a.reshape(n, 512), 512, 0), (u, 512, U_Z // 512)],
                 [(wl["dn_norm_w"], (1, LANE), lambda j: (0, 0))], [(512, bf16)], 1, "dn_post")[0]
    sv.update(dn_in=dn_in, prep_a=prep_a, o_a=o_a, ck_a=ck_a, y_a=y_a)

    q_b, k_b = rowmap(f_ret_pre, [(u, 256, U_RQ // 256), (u, 256, U_RK // 256), (cos, 256, 0), (sin, 256, 0)], [],
                      [(256, f32), (256, f32)], 1, "ret_pre")
    q_b3, k_b3 = q_b.reshape(bsz, seq, 256), k_b.reshape(bsz, seq, 256)
    v_b3 = lax.slice_in_dim(u3, U_RV, U_RV + 512, axis=2)
    ret_in = [(q_b, 256, 0), (k_b, 256, 0), (u, 512, U_RV // 512)]
    inner = rowmap(ret_prep, ret_in, [], [(512, f32)], 1, "ret_prep", rows=crow)[0]
    ret_seq = [q_b3, k_b3, v_b3, inner.reshape(bsz, seq, 512)]
    o_b, ck_b = chunk_scan(ret_step, ret_seq, (RET_HEADS, 256, RET_DV), 512, "ret_scan")
    y_b = rowmap(per_head(f_ret_post), [(o_b.reshape(n, 512), 512, 0), (u, 512, U_RG // 512)], [], [(512, bf16)], 1,
                 "ret_post")[0]
    sv.update(ret_in=ret_in, ret_seq=ret_seq, o_b=o_b, ck_b=ck_b, y_b=y_b)

    lru_params = _lru_params(wl)
    a_c, b_c = seqmap(f_lru_pre, [(u3, U_CX // LANE)], lru_params, 2, 4, "lru_pre")
    h_c, hp_c = lru_scan(a_c, b_c)
    y_c = rowmap(f_lru_post, [(h_c.reshape(n, 512), 512, 0), (u, 512, U_CG // 512)], [], [(512, bf16)], 1, "lru_post")[0]
    sv.update(a_c=a_c, hp_c=hp_c, h_c=h_c, y_c=y_c)

    br = [mm(y, wl["w_branch"][i], "nn", "mm_branch") for i, y in enumerate((y_a, y_b, y_c))]
    merged = rowmap(f_merge, [(u, D_MODEL, i) for i in range(3)] + [(b, D_MODEL, 0) for b in br], [], [(D_MODEL, bf16)], 1,
                    "merge")[0]
    x_mid = mm(merged, wl["w_out"], "nn", "mm_out", add=x)
    sv.update(br=br, merged=merged, x_mid=x_mid)

    h2 = rowmap(f_norm, [(x_mid, D_MODEL, 0)], [row1(wl["ffn_norm"])], [(D_MODEL, bf16)], 1, "norm_fwd")[0]
    up = mm(h2, wl["w_up"], "nn", "mm_up")
    act = seqmap(f_ffn_mid, [(up.reshape(bsz, seq, 2 * D_FF), 0), (up.reshape(bsz, seq, 2 * D_FF), D_FF // LANE)],
                 _ffn_params(wl), 1, D_FF // LANE, "ffn_mid", out_dtype=bf16)[0]
    act = act.reshape(n, D_FF)
    x_out = mm(act, wl["w_down"], "nn", "mm_down", add=x_mid)
    sv.update(h2=h2, up=up, act=act)
    return x_out, sv


def _lru_params(wl):
    col = lambda a: (a, (a.shape[0], LANE), lambda j: (0, j))
    blk = lambda a: (a, (None, LANE, LANE), lambda j: (j, 0, 0))
    return [col(wl["lru_conv_w"]), col(wl["lru_conv_b"]), blk(wl["lru_wa"]), col(wl["lru_ba"]), blk(wl["lru_wx"]),
            col(wl["lru_bx"]), col(wl["lru_lambda"])]


def _ffn_params(wl):
    nb = D_FF // LANE
    return [(wl["ffn_conv_w"], (3, LANE), lambda j: (0, j)), (wl["ffn_conv_w"], (3, LANE), lambda j: (0, nb + j)),
            (wl["ffn_conv_b"], (1, LANE), lambda j: (0, j)), (wl["ffn_conv_b"], (1, LANE), lambda j: (0, nb + j))]


def _layer_bwd(dx, dx16, sv, cos, sin, bsz, seq, emit, dep):
    n = dx.shape[0]
    gr = {}
    wl = sv["wl"]
    u, x_in, x_mid = sv["u"], sv["x_in"], sv["x_mid"]
    u3 = u.reshape(bsz, seq, U_PAD)
    row1 = lambda a: (a, (1, a.shape[1]), lambda j: (0, 0))

    d_act = mm(dx16, wl["w_down"], "nt", "mm_down_dx", dep=dep)
    gr["w_down"] = mm(sv["act"], dx16, "tn", "mm_down_dw")
    up3 = sv["up"].reshape(bsz, seq, 2 * D_FF)
    (d_gate, d_val), dps = seqmap_bwd(f_ffn_mid, [(up3, 0), (up3, D_FF // LANE)], _ffn_params(wl),
                                      [d_act.reshape(bsz, seq, D_FF)], D_FF // LANE, "ffn_mid_bwd", din_dtype=bf16)
    gr["ffn_conv_w"] = jnp.concatenate([_cols(dps[0]), _cols(dps[1])], axis=1)
    gr["ffn_conv_b"] = jnp.concatenate([_cols(dps[2]), _cols(dps[3])], axis=1)[0]
    d_up = jnp.concatenate([d_gate, d_val], axis=2).reshape(n, 2 * D_FF)
    gr["w_up"] = mm(sv["h2"], d_up, "tn", "mm_up_dw")
    token = emit("ffn", {k: gr[k] for k in ("w_up", "w_down")})
    d_h2 = mm(d_up, wl["w_up"], "nt", "mm_up_dx", dep=token)
    (dx_mid,), (dg,), ex = rowmap_bwd(f_norm, [(x_mid, D_MODEL, 0)], [row1(wl["ffn_norm"])], [d_h2], 1, "norm_bwd", add=[dx],
                                      copy16=0)
    dx_mid16 = ex["copy16"]
    gr["ffn_norm"] = dg[0, 0]

    du = lax.empty((n, U_PAD), bf16)
    du3 = lambda: du.reshape(bsz, seq, U_PAD)

    d_merged = mm(dx_mid16, wl["w_out"], "nt", "mm_out_dx")
    gr["w_out"] = mm(sv["merged"], dx_mid16, "tn", "mm_out_dw")
    dm, _, ex = rowmap_bwd(f_merge, [(u, D_MODEL, i) for i in range(3)] + [(b, D_MODEL, 0) for b in sv["br"]], [], [d_merged],
                           1, "merge_bwd", din_dtypes=[bf16] * 6, into=(du, U_GATES // (3 * D_MODEL), [0, 1, 2]))
    du, d_br = ex["into"], dm[3:]
    ys = (sv["y_a"], sv["y_b"], sv["y_c"])
    d_ys = [mm(d_br[i], wl["w_branch"][i], "nt", "mm_branch_dx") for i in range(3)]
    gr["w_branch"] = jnp.stack([mm(ys[i], d_br[i], "tn", "mm_branch_dw") for i in range(3)])

    (d_hc, _), _, ex = rowmap_bwd(f_lru_post, [(sv["h_c"].reshape(n, 512), 512, 0), (u, 512, U_CG // 512)], [], [d_ys[2]], 1,
                                  "lru_post_bwd", into=(du, U_CG // 512, [1]))
    du = ex["into"]
    d_a, d_b = lru_scan_bwd(sv["a_c"], sv["hp_c"], d_hc.reshape(bsz, seq, 512))
    (du_new,), dps = seqmap_bwd(f_lru_pre, [(u3, U_CX // LANE)], _lru_params(wl), [d_a, d_b], 4, "lru_pre_bwd", din_dtype=bf16,
                                into=(du3(), U_CX // LANE))
    du = du_new.reshape(n, U_PAD)
    gr["lru_conv_w"], gr["lru_conv_b"] = _cols(dps[0]), _cols(dps[1])[0]
    gr["lru_wa"], gr["lru_ba"], gr["lru_wx"], gr["lru_bx"] = dps[2], dps[3][:, 0], dps[4], dps[5][:, 0]
    gr["lru_lambda"] = _cols(dps[6])[0]

    (d_ob, _), _, ex = rowmap_bwd(per_head(f_ret_post), [(sv["o_b"].reshape(n, 512), 512, 0), (u, 512, U_RG // 512)], [],
                                  [d_ys[1]], 1, "ret_post_bwd", into=(du, U_RG // 512, [1]))
    du = ex["into"]
    crow = CHUNK * CHUNKS_PER_STEP
    d_ret = chunk_scan_bwd(ret_step, sv["ret_seq"], sv["ck_b"], d_ob.reshape(bsz, seq, 512), "ret_scan_bwd")
    d_ret = [t.reshape(n, t.shape[2]) for t in d_ret]
    (d_qb, d_kb, _), _, ex = rowmap_bwd(ret_prep, sv["ret_in"], [], [d_ret[3]], 1, "ret_prep_bwd", rows=crow, add=d_ret[:3],
                                        into=(du, U_RV // 512, [2]))
    du = ex["into"]
    _, _, ex = rowmap_bwd(f_ret_pre, [(u, 256, U_RQ // 256), (u, 256, U_RK // 256), (cos, 256, 0), (sin, 256, 0)], [],
                          [d_qb, d_kb], 1, "ret_pre_bwd", din_dtypes=[f32, f32, None, None], into=(du, U_RQ // 512, [0, 1]))
    du = ex["into"]

    (d_oa, _), (dnw,), ex = rowmap_bwd(per_head(f_dn_post), [(sv["o_a"].reshape(n, 512), 512, 0), (u, 512, U_Z // 512)],
                                       [(wl["dn_norm_w"], (1, LANE), lambda j: (0, 0))], [d_ys[0]], 1, "dn_post_bwd",
                                       into=(du, U_Z // 512, [1]))
    du = ex["into"]
    gr["dn_norm_w"] = dnw[0, 0]
    d_prep = chunk_scan_bwd(dn_step, sv["prep_a"], sv["ck_a"], d_oa.reshape(bsz, seq, 512), "dn_scan_bwd")
    (d_q, d_k, d_v, d_gb), _, _ = rowmap_bwd(dn_prep, sv["dn_in"], [], [t.reshape(n, t.shape[2]) for t in d_prep], 1,
                                             "dn_prep_bwd", rows=crow)
    d_q, d_k, d_v = (t.reshape(bsz, seq, 512) for t in (d_q, d_k, d_v))
    _, (dgp,), ex = rowmap_bwd(f_dn_gates, [(u, 512, U_AB // 512)], [(wl["dn_gate_p"], (8, LANE), lambda j: (0, 0))],
                               [d_gb], 1, "dn_gates_bwd", rows=512, into=(du, U_AB // 512, [0]))
    du = ex["into"]
    gr["dn_a_log"], gr["dn_dt_bias"] = dgp[0, 0, :DN_HEADS], dgp[0, 1, :DN_HEADS]
    d_cw = []
    for kind, d_t in enumerate((d_q, d_k, d_v)):
        cw = (wl["dn_conv_w"], (4, LANE), functools.partial(lambda j, kind: (0, 4 * kind + j), kind=kind))
        (du_new,), (dcw,) = seqmap_bwd(functools.partial(f_dn_pre, kind), [(u3, U_QKV // LANE + 4 * kind)], [cw], [d_t], 4,
                                       "dn_pre%d_bwd" % kind, din_dtype=bf16, into=(du3(), U_QKV // LANE + 4 * kind))
        du = du_new.reshape(n, U_PAD)
        d_cw.append(_cols(dcw))
    gr["dn_conv_w"] = jnp.concatenate(d_cw, axis=1)

    gr["w_in"] = _unpad_w_in(mm(sv["h"], du, "tn", "mm_in_dw"))
    token = emit("mix", {k: gr[k] for k in ("w_in", "w_branch", "w_out")})
    d_h = mm(du, wl["w_in"], "nt", "mm_in_dx", dep=token)
    (dx_in,), (dg,), ex = rowmap_bwd(f_norm, [(x_in, D_MODEL, 0)], [row1(wl["attn_norm"])], [d_h], 1, "norm_bwd", add=[dx_mid],
                                     copy16=0)
    gr["attn_norm"] = dg[0, 0]
    big = ("w_in", "w_branch", "w_out", "w_up", "w_down")
    return dx_in, ex["copy16"], emit("small", {k: g for k, g in gr.items() if k not in big})


def _cols(dp):
    ncol, p, _ = dp.shape
    return jnp.transpose(dp, (1, 0, 2)).reshape(p, ncol * LANE)


def _pad_w_in(w):
    segs = sorted(_IN_SEGS, key=lambda s: s[2])
    parts = [lax.slice_in_dim(w, src, src + width, axis=1) for src, width, _ in segs]
    end = segs[-1][2] + segs[-1][1]
    return jnp.concatenate(parts + [jnp.zeros((w.shape[0], U_PAD - end), w.dtype)], axis=1)


def _unpad_w_in(wp):
    return jnp.concatenate([lax.slice_in_dim(wp, dst, dst + width, axis=1) for _, width, dst in _IN_SEGS], axis=1)


def _rope_tables(positions):
    half = RET_DK // 2
    inv = ROPE_BASE ** (-jnp.arange(half, dtype=f32) / half)
    ang = positions.astype(f32).reshape(-1, 1) * inv
    cos, sin = jnp.cos(ang), jnp.sin(ang)
    return jnp.tile(cos, (1, 2 * RET_HEADS)), jnp.tile(sin, (1, 2 * RET_HEADS))


def _layer_weights(lw):
    wl = {}
    wl["w_in"] = _pad_w_in(lw["w_in"])
    for k in ("dn_conv_w", "lru_conv_w", "ffn_conv_w", "lru_wa", "lru_wx"):
        wl[k] = lw[k]
    for k in ("attn_norm", "ffn_norm", "dn_norm_w", "lru_conv_b", "lru_lambda", "ffn_conv_b", "lru_ba", "lru_bx"):
        wl[k] = lw[k].reshape(1, -1)
    gp = jnp.zeros((8, LANE), f32)
    wl["dn_gate_p"] = gp.at[0, :DN_HEADS].set(lw["dn_a_log"]).at[1, :DN_HEADS].set(lw["dn_dt_bias"])
    return wl


REST = ("w_branch", "w_out", "w_up", "w_down")


def forward_backward(x, positions, target, layer_weights, final_norm, on_head, on_grads):
    bsz, seq, d = x.shape
    n = bsz * seq
    cos, sin = _rope_tables(positions)
    xs = x.reshape(n, d)
    saved = []
    for layer in range(DEPTH):
        first, fetch_rest = layer_weights(layer, xs)
        xs, sv = _layer_fwd(xs, _layer_weights(first), fetch_rest, cos, sin, bsz, seq)
        saved.append(sv)
    loss, dx, d_final, dx16 = final_loss(xs, final_norm.reshape(1, d), target.reshape(n, d))
    on_head(loss[0, 0], d_final[0])
    token = None
    for layer in reversed(range(DEPTH)):
        dx, dx16, token = _layer_bwd(dx, dx16, saved[layer], cos, sin, bsz, seq, functools.partial(on_grads, layer), token)
    return dx.reshape(bsz, seq, d)


def local_step(x, positions, target, full):
    grads, head = {layer: {} for layer in range(DEPTH)}, {}

    def layer_weights(layer, _):
        return ({k: a[layer] for k, a in full.items() if k != "final_norm" and k not in REST},
                lambda after: {k: full[k][layer] for k in REST})

    gx = forward_backward(x, positions, target, layer_weights, full["final_norm"],
                          lambda loss, d_final: head.update(loss=loss, d_final=d_final),
                          lambda layer, group, gr: grads[layer].update(gr))
    stacked = {k: jnp.stack([grads[layer][k] for layer in range(DEPTH)]) for k in grads[0]}
    stacked["final_norm"] = head["d_final"]
    return head["loss"], gx, stacked


BIG = (("w_in", 2), ("w_branch", 3), ("w_out", 1), ("w_up", 2), ("w_down", 1))
SMALL_SHARDED = (("dn_conv_w", 2), ("lru_conv_w", 2), ("ffn_conv_w", 2))
REPLICATED = ("attn_norm", "dn_a_log", "dn_dt_bias", "dn_norm_w", "lru_conv_b", "lru_wa", "lru_ba", "lru_wx", "lru_bx",
              "lru_lambda", "ffn_norm", "ffn_conv_b", "final_norm")
WEIGHTS = ("attn_norm", "w_in", "dn_conv_w", "dn_a_log", "dn_dt_bias", "dn_norm_w", "lru_conv_w", "lru_conv_b", "lru_wa",
           "lru_ba", "lru_wx", "lru_bx", "lru_lambda", "w_branch", "w_out", "ffn_norm", "w_up", "ffn_conv_w", "ffn_conv_b",
           "w_down", "final_norm")


def _pack(arrs, dtype, align=16 * LANE):
    flat = jnp.concatenate([a.reshape(-1).astype(dtype) for a in arrs])
    pad = (-flat.shape[0]) % align
    return jnp.pad(flat, (0, pad)).reshape(-1, LANE)


def _unpack(rows, shapes):
    flat = rows.reshape(-1)
    out, pos = [], 0
    for shp in shapes:
        size = math.prod(shp)
        out.append(lax.slice_in_dim(flat, pos, pos + size).reshape(shp))
        pos += size
    return out


def _split8(a, axis):
    size = a.shape[axis] // N_DEV
    return [lax.slice_in_dim(a, p * size, (p + 1) * size, axis=axis) for p in range(N_DEV)]


def kernel(x, positions, attn_norm, w_in, dn_conv_w, dn_a_log, dn_dt_bias, dn_norm_w, lru_conv_w, lru_conv_b, lru_wa, lru_ba, lru_wx, lru_bx, lru_lambda, w_branch, w_out, ffn_norm, w_up, ffn_conv_w, ffn_conv_b, w_down, final_norm, loss_target, m_attn_norm, m_w_in, m_dn_conv_w, m_dn_a_log, m_dn_dt_bias, m_dn_norm_w, m_lru_conv_w, m_lru_conv_b, m_lru_wa, m_lru_ba, m_lru_wx, m_lru_bx, m_lru_lambda, m_w_branch, m_w_out, m_ffn_norm, m_w_up, m_ffn_conv_w, m_ffn_conv_b, m_w_down, m_final_norm, v_attn_norm, v_w_in, v_dn_conv_w, v_dn_a_log, v_dn_dt_bias, v_dn_norm_w, v_lru_conv_w, v_lru_conv_b, v_lru_wa, v_lru_ba, v_lru_wx, v_lru_bx, v_lru_lambda, v_w_branch, v_w_out, v_ffn_norm, v_w_up, v_ffn_conv_w, v_ffn_conv_b, v_w_down, v_final_norm):
    w = dict(attn_norm=attn_norm, w_in=w_in, dn_conv_w=dn_conv_w, dn_a_log=dn_a_log, dn_dt_bias=dn_dt_bias, dn_norm_w=dn_norm_w,
             lru_conv_w=lru_conv_w, lru_conv_b=lru_conv_b, lru_wa=lru_wa, lru_ba=lru_ba, lru_wx=lru_wx, lru_bx=lru_bx,
             lru_lambda=lru_lambda, w_branch=w_branch, w_out=w_out, ffn_norm=ffn_norm, w_up=w_up, ffn_conv_w=ffn_conv_w,
             ffn_conv_b=ffn_conv_b, w_down=w_down, final_norm=final_norm)
    m = dict(attn_norm=m_attn_norm, w_in=m_w_in, dn_conv_w=m_dn_conv_w, dn_a_log=m_dn_a_log, dn_dt_bias=m_dn_dt_bias,
             dn_norm_w=m_dn_norm_w, lru_conv_w=m_lru_conv_w, lru_conv_b=m_lru_conv_b, lru_wa=m_lru_wa, lru_ba=m_lru_ba,
             lru_wx=m_lru_wx, lru_bx=m_lru_bx, lru_lambda=m_lru_lambda, w_branch=m_w_branch, w_out=m_w_out, ffn_norm=m_ffn_norm,
             w_up=m_w_up, ffn_conv_w=m_ffn_conv_w, ffn_conv_b=m_ffn_conv_b, w_down=m_w_down, final_norm=m_final_norm)
    v = dict(attn_norm=v_attn_norm, w_in=v_w_in, dn_conv_w=v_dn_conv_w, dn_a_log=v_dn_a_log, dn_dt_bias=v_dn_dt_bias,
             dn_norm_w=v_dn_norm_w, lru_conv_w=v_lru_conv_w, lru_conv_b=v_lru_conv_b, lru_wa=v_lru_wa, lru_ba=v_lru_ba,
             lru_wx=v_lru_wx, lru_bx=v_lru_bx, lru_lambda=v_lru_lambda, w_branch=v_w_branch, w_out=v_w_out, ffn_norm=v_ffn_norm,
             w_up=v_w_up, ffn_conv_w=v_ffn_conv_w, ffn_conv_b=v_ffn_conv_b, w_down=v_w_down, final_norm=v_final_norm)

    me = 4 * lax.axis_index("x") + 2 * lax.axis_index("y") + lax.axis_index("c")
    axes = dict(BIG + SMALL_SHARDED)
    conv_names = [k for k, _ in SMALL_SHARDED]

    gathers, tokens = {}, []
    for layer in range(DEPTH):
        first = [w["w_in"][layer].astype(bf16), _pack([w[k][layer] for k in conv_names], f32)]
        rest = [w[k][layer].astype(bf16) for k in REST]
        for part, srcs in (("in", first), ("rest", rest)):
            gathers[layer, part], token = exchange_start(srcs, [False] * len(srcs), "gather_%s_start%d" % (part, layer))
            tokens.append(token[0:1, 0:1])
    all_started = functools.reduce(lambda a, b: a + b, tokens)

    def joined(own, land, axis):
        return jnp.concatenate([jnp.where(me == p, own, land[p]) for p in range(N_DEV)], axis=axis)

    def layer_weights(layer, x_in):
        own, lands = exchange_wait(gathers[layer, "in"], x_in, "gather_in_wait%d" % layer)
        lw = {"w_in": joined(own[0], lands[0], 1)}
        shapes = [w[k].shape[1:] for k in conv_names]
        per_dev = [_unpack(jnp.where(me == p, own[1], lands[1][p]), shapes) for p in range(N_DEV)]
        lw.update({k: jnp.concatenate([per_dev[p][i] for p in range(N_DEV)], axis=axes[k] - 1) for i, k in enumerate(conv_names)})
        lw.update({k: w[k][layer] for k in REPLICATED if k != "final_norm"})
        if layer == 0:
            lw["attn_norm"] = lw["attn_norm"] + all_started[0]

        def fetch_rest(after):
            own_r, lands_r = exchange_wait(gathers[layer, "rest"], after, "gather_rest_wait%d" % layer)
            return {k: joined(own_r[i], lands_r[i], axes[k] - 1) for i, k in enumerate(REST)}

        return lw, fetch_rest

    small_names = conv_names + [k for k in REPLICATED if k != "final_norm"]
    groups = {"ffn": ("w_up", "w_down"), "mix": ("w_in", "w_branch", "w_out")}
    scatters, small_shapes, head = {}, {}, {}

    def on_grads(layer, group, gr):
        if group == "small":
            small = [gr[k] for k in small_names] + ([head["loss"].reshape(1), head["d_final"]] if layer == DEPTH - 1 else [])
            small_shapes[layer] = [a.shape for a in small]
            srcs, modes = [_pack(small, f32)], [False]
        else:
            srcs = [jnp.stack(_split8(gr[k], axes[k] - 1)).astype(bf16) for k in groups[group]]
            modes = [True] * len(srcs)
        scatters[layer, group], token = exchange_start(srcs, modes, "scatter_%s_start%d" % (group, layer))
        return token

    grad_x = forward_backward(x, positions, loss_target, layer_weights, final_norm,
                              lambda loss_part, d_final: head.update(loss=loss_part, d_final=d_final), on_grads)

    big_sums, small_sums = {}, {}
    for group in ("ffn", "mix"):
        for layer in reversed(range(DEPTH)):
            own, lands = exchange_wait(scatters[layer, group], grad_x, "scatter_%s_wait%d" % (group, layer))
            for i, k in enumerate(groups[group]):
                shard = w[k].shape[1:]
                mine = lax.dynamic_index_in_dim(own[i], me, axis=0, keepdims=False).reshape(-1, shard[-1])
                big_sums[layer, k] = sum_slots(lands[i].reshape(N_DEV, -1, shard[-1]), "sum_" + k, own=mine).reshape(shard)
    for layer in reversed(range(DEPTH)):
        own, lands = exchange_wait(scatters[layer, "small"], grad_x, "scatter_small_wait%d" % layer)
        small_sums[layer] = _unpack(sum_slots(lands[0], "sum_small", own=own[0]), small_shapes[layer])
    grads = {k: jnp.stack([big_sums[layer, k] for layer in range(DEPTH)]) for k, _ in BIG}
    loss, grads["final_norm"] = small_sums[DEPTH - 1][len(small_names)][0], small_sums[DEPTH - 1][len(small_names) + 1]
    for i, k in enumerate(small_names):
        g = jnp.stack([small_sums[layer][i] for layer in range(DEPTH)])
        ax = dict(SMALL_SHARDED).get(k)
        if ax is None:
            grads[k] = g
        else:
            size = g.shape[ax] // N_DEV
            grads[k] = lax.dynamic_slice_in_dim(g, me * size, size, axis=ax)

    upd = {k: adamw(w[k], grads[k], m[k], v[k], "adamw_" + k) for k in WEIGHTS}
    return (loss, grad_x, *[grads[k] for k in WEIGHTS], *[upd[k][0] for k in WEIGHTS], *[upd[k][1] for k in WEIGHTS],
            *[upd[k][2] for k in WEIGHTS])
```

```python
import functools
import math

import jax
import jax.numpy as jnp
from jax import lax
from jax.experimental import pallas as pl
from jax.experimental.pallas import tpu as pltpu

f32 = jnp.float32
bf16 = jnp.bfloat16

D_MODEL = 1024
DEPTH = 4
CHUNK = 64
EPS = 1e-6
DN_HEADS, DN_DK = 4, 128
RET_HEADS, RET_DK, RET_DV = 4, 64, 128
ROPE_BASE = 10000.0
LRU_C = 8.0
D_FF = 2816
N_DEV = 8
LANE = 128
VMEM_LIMIT = 56 * 1024 * 1024

ADAM_LR, ADAM_B1, ADAM_B2, ADAM_EPS, ADAM_WD, ADAM_STEP = 0.001, 0.9, 0.999, 1e-8, 0.01, 10

U_GATES, U_QKV, U_RV, U_RG, U_Z, U_CX, U_CG, U_RQ, U_RK, U_AB = (
    0, 3072, 4608, 5120, 5632, 6144, 6656, 7168, 7424, 7680)
U_PAD = 8192
_IN_SEGS = ((0, 1536, U_QKV), (1536, 8, U_AB), (1544, 512, U_Z), (2056, 256, U_RQ), (2312, 256, U_RK),
            (2568, 512, U_RV), (3080, 512, U_RG), (3592, 512, U_CX), (4104, 512, U_CG), (4616, 3072, U_GATES))
N_IN = 7688


def _params():
    return pltpu.CompilerParams(vmem_limit_bytes=VMEM_LIMIT)


def _pick(dim, pref):
    best = None
    for d in range(LANE, min(dim, pref) + 1, LANE):
        if dim % d == 0:
            best = d
    return best if best is not None else dim


@functools.partial(jax.custom_vjp, nondiff_argnums=(1, 2))
def sroll(x, shift, axis):
    return pltpu.roll(x, shift, axis)


def _sroll_fwd(x, shift, axis):
    return pltpu.roll(x, shift, axis), None


def _sroll_bwd(shift, axis, _, g):
    n = g.shape[axis]
    return (pltpu.roll(g, (n - shift) % n, axis),)


sroll.defvjp(_sroll_fwd, _sroll_bwd)

_DIMS = {"nn": (((1,), (0,)), ((), ())), "nt": (((1,), (1,)), ((), ())), "tn": (((0,), (0,)), ((), ()))}


def _dg(a, b, dims):
    return lax.dot_general(a.astype(bf16), b.astype(bf16), _DIMS[dims], preferred_element_type=f32)


@functools.partial(jax.custom_vjp, nondiff_argnums=(2,))
def bdot(a, b, dims):
    return _dg(a, b, dims)


def _bdot_fwd(a, b, dims):
    return _dg(a, b, dims), (a.astype(bf16), b.astype(bf16))


def _bdot_bwd(dims, res, g):
    a, b = res
    if dims == "nn":
        return _dg(g, b, "nt"), _dg(a, g, "tn")
    if dims == "nt":
        return _dg(g, b, "nn"), _dg(g, a, "tn")
    return _dg(b, g, "nt"), _dg(a, g, "nn")


bdot.defvjp(_bdot_fwd, _bdot_bwd)


def _fdot(a, b, dims):
    return lax.dot_general(a, b, _DIMS[dims], precision=lax.Precision.HIGH, preferred_element_type=f32)


@jax.custom_vjp
def unit_lower_inv_all(mats):
    shape = mats[0].shape
    row = lax.broadcasted_iota(jnp.int32, shape, 0)
    col = lax.broadcasted_iota(jnp.int32, shape, 1)
    eye = jnp.where(row == col, 1.0, 0.0).astype(f32)
    n = [-a for a in mats]
    p = [eye + x for x in n]
    span = 2
    while span < shape[0]:
        n = [_fdot(x, x, "nn") for x in n]
        p = [y + _fdot(y, x, "nn") for y, x in zip(p, n)]
        span *= 2
    return p


def _uli_fwd(mats):
    x = unit_lower_inv_all(mats)
    return x, x


def _uli_bwd(xs, gs):
    t = [_fdot(x, g, "tn") for x, g in zip(xs, gs)]
    return ([-_fdot(y, x, "nt") for y, x in zip(t, xs)],)


unit_lower_inv_all.defvjp(_uli_fwd, _uli_bwd)


@jax.custom_vjp
def known_inverse(invs, mats):
    return invs


def _known_fwd(invs, mats):
    return invs, invs


def _known_bwd(xs, gs):
    return [jnp.zeros_like(x) for x in xs], _uli_bwd(xs, gs)[0]


known_inverse.defvjp(_known_fwd, _known_bwd)


def cumsum_rows(x):
    rows = x.shape[0]
    row = lax.broadcasted_iota(jnp.int32, x.shape, 0)
    s = 1
    while s < rows:
        x = x + jnp.where(row >= s, sroll(x, s, 0), 0.0)
        s *= 2
    return x


def _expm1(x):
    return jnp.tanh(0.5 * x) * (jnp.exp(x) + 1.0)


def _lane_pick(x, lane):
    idx = lax.broadcasted_iota(jnp.int32, x.shape, 1)
    return jnp.sum(jnp.where(idx == lane, x, 0.0), axis=1, keepdims=True)


def _row_pick(x, r):
    idx = lax.broadcasted_iota(jnp.int32, x.shape, 0)
    return jnp.sum(jnp.where(idx == r, x, 0.0), axis=0, keepdims=True)


def _causal_conv(x, halo, w, width):
    xe = jnp.concatenate([halo, x], axis=0)
    acc = xe * w[width - 1:width]
    for k in range(width - 1):
        acc = acc + sroll(xe, width - 1 - k, 0) * w[k:k + 1]
    return acc[8:]


def f_norm(ins, ps):
    (x,), (g,) = ins, ps
    return [x * lax.rsqrt(jnp.mean(x * x, axis=-1, keepdims=True) + EPS) * g]


def f_dn_pre(kind, mains, halos, ps):
    y = _causal_conv(mains[0], halos[0], ps[0], 4)
    y = y * jax.nn.sigmoid(y)
    if kind < 2:
        y = y * lax.rsqrt(jnp.sum(y * y, axis=-1, keepdims=True) + EPS)
    if kind == 0:
        y = y * (DN_DK ** -0.5)
    return [y]


def f_dn_gates(ins, ps):
    u, p = ins[0][:, :LANE], ps[0]
    lane = lax.broadcasted_iota(jnp.int32, u.shape, 1)
    g = -jnp.exp(p[0:1]) * jax.nn.softplus(u + p[1:2])
    beta = jax.nn.sigmoid(u)
    return [jnp.where(lane < 4, g, jnp.where(lane < 8, beta, 0.0))]


def per_head(fn):
    def tile_fn(vals, ps):
        heads = [fn([v[:, h * LANE:(h + 1) * LANE] for v in vals], ps) for h in range(vals[0].shape[1] // LANE)]
        return [jnp.concatenate([o[i] for o in heads], axis=1) for i in range(len(heads[0]))]
    return tile_fn


def f_dn_post(ins, ps):
    (o, z), (nw,) = ins, ps
    y = o * lax.rsqrt(jnp.mean(o * o, axis=-1, keepdims=True) + EPS) * nw
    return [y * (z * jax.nn.sigmoid(z))]


def _rot_half(t):
    lane = lax.broadcasted_iota(jnp.int32, t.shape, 1)
    width = t.shape[1]
    first = (lane % RET_DK) < (RET_DK // 2)
    return jnp.where(first, -sroll(t, width - RET_DK // 2, 1), sroll(t, RET_DK // 2, 1))


def f_ret_pre(ins, ps):
    q, k, cos, sin = ins
    qr = q * cos + _rot_half(q) * sin
    kr = (k * cos + _rot_half(k) * sin) * (RET_DK ** -0.5)
    return [qr, kr]


def f_ret_post(ins, ps):
    o, g = ins
    mu = jnp.mean(o, axis=-1, keepdims=True)
    var = jnp.mean(jnp.square(o - mu), axis=-1, keepdims=True)
    return [(o - mu) * lax.rsqrt(var + EPS) * (g * jax.nn.sigmoid(g))]


def f_lru_pre(mains, halos, ps):
    cw, cb, wa, ba, wx, bx, lam = ps
    xc = _causal_conv(mains[0], halos[0], cw, 4) + cb
    r = jax.nn.sigmoid(bdot(xc, wa, "nn") + ba)
    i = jax.nn.sigmoid(bdot(xc, wx, "nn") + bx)
    log_a = -LRU_C * r * jax.nn.softplus(-lam)
    a = jnp.exp(log_a)
    b = jnp.sqrt(-_expm1(2.0 * log_a)) * (i * xc)
    return [a, b]


def f_lru_post(ins, ps):
    h, g = ins
    return [h * jax.nn.gelu(g)]


def f_merge(ins, ps):
    g0, g1, g2, b0, b1, b2 = ins
    return [jax.nn.sigmoid(g0) * b0 + jax.nn.sigmoid(g1) * b1 + jax.nn.sigmoid(g2) * b2]


def f_ffn_mid(mains, halos, ps):
    cwg, cwv, cbg, cbv = ps
    gate = _causal_conv(mains[0], halos[0], cwg, 3) + cbg
    val = _causal_conv(mains[1], halos[1], cwv, 3) + cbv
    return [gate * jax.nn.sigmoid(gate) * val]


def mm(a, b, dims, name, add=None, dep=None, tm=1536, tn=1536, tk=2816):
    if dims == "tn":
        kdim, m = a.shape
        n = b.shape[1]
    else:
        m, kdim = a.shape
        n = b.shape[0] if dims == "nt" else b.shape[1]
    tm, tn, tk = _pick(m, tm), _pick(n, tn), _pick(kdim, tk)
    nk = kdim // tk
    a_spec = pl.BlockSpec((tk, tm), lambda i, j, k: (k, i)) if dims == "tn" else pl.BlockSpec((tm, tk), lambda i, j, k: (i, k))
    b_spec = pl.BlockSpec((tn, tk), lambda i, j, k: (j, k)) if dims == "nt" else pl.BlockSpec((tk, tn), lambda i, j, k: (k, j))
    o_spec = pl.BlockSpec((tm, tn), lambda i, j, k: (i, j))
    has_add, has_dep = add is not None, dep is not None

    def body(*refs):
        a_ref, b_ref = refs[:2]
        add_ref = refs[2] if has_add else None
        o_ref = refs[2 + has_add + has_dep]
        if nk == 1:
            prod = _dg(a_ref[...], b_ref[...], dims)
            o_ref[...] = prod + add_ref[...] if has_add else prod
            return
        acc_ref = refs[-1]
        k = pl.program_id(2)

        @pl.when(k == 0)
        def _():
            acc_ref[...] = jnp.zeros_like(acc_ref)

        acc_ref[...] += _dg(a_ref[...], b_ref[...], dims)

        @pl.when(k == nk - 1)
        def _():
            o_ref[...] = acc_ref[...] + add_ref[...] if has_add else acc_ref[...]

    args = [a, b] + ([add] if has_add else []) + ([dep] if has_dep else [])
    in_specs = [a_spec, b_spec] + ([o_spec] if has_add else [])
    in_specs += [pl.BlockSpec((8, LANE), lambda i, j, k: (0, 0))] if has_dep else []
    return pl.pallas_call(
        body, name=name, grid=(m // tm, n // tn, nk), in_specs=in_specs, out_specs=o_spec,
        out_shape=jax.ShapeDtypeStruct((m, n), f32), scratch_shapes=[pltpu.VMEM((tm, tn), f32)] if nk > 1 else [],
        compiler_params=_params())(*args)


def rowmap(fn, ins, params, outs, ncol, name, rows=256):
    n = ins[0][0].shape[0]
    r = min(rows, n)
    nin, npar = len(ins), len(params)

    def body(*refs):
        vals = [x[...] for x in refs[:nin]]
        pv = [p[...] for p in refs[nin:nin + npar]]
        for o_ref, o in zip(refs[nin + npar:], fn(vals, pv)):
            o_ref[...] = o.astype(o_ref.dtype)

    in_specs = [pl.BlockSpec((r, cb), functools.partial(lambda j, i, off: (i, off + j), off=off)) for _, cb, off in ins]
    in_specs += [pl.BlockSpec(bs, functools.partial(lambda j, i, f: f(j), f=f)) for _, bs, f in params]
    out_specs = [pl.BlockSpec((r, cb), lambda j, i: (i, j)) for cb, _ in outs]
    out_shape = [jax.ShapeDtypeStruct((n, cb * ncol), dt) for cb, dt in outs]
    res = pl.pallas_call(body, name=name, grid=(ncol, n // r), in_specs=in_specs, out_specs=out_specs,
                         out_shape=out_shape, compiler_params=_params())(*[a for a, _, _ in ins], *[a for a, _, _ in params])
    return res


def rowmap_bwd(fn, ins, params, douts, ncol, name, rows=256, add=None, din_dtypes=None, into=None, copy16=None):
    n = ins[0][0].shape[0]
    r = min(rows, n)
    nin, npar, nout = len(ins), len(params), len(douts)
    add = [None] * nin if add is None else list(add)
    add_idx = [i for i in range(nin) if add[i] is not None]
    din_dtypes = [f32] * nin if din_dtypes is None else list(din_dtypes)
    into_buf, into_off, into_idx = into if into is not None else (None, 0, [])
    has_into, has_copy = into is not None, copy16 is not None
    kept = [i for i in range(nin) if din_dtypes[i] is not None and i not in into_idx]

    def body(*refs):
        vals = [x[...] for x in refs[:nin]]
        pv = [p[...] for p in refs[nin:nin + npar]]
        dys = [d[...] for d in refs[nin + npar:nin + npar + nout]]
        k0 = nin + npar + nout
        add_refs = dict(zip(add_idx, refs[k0:k0 + len(add_idx)]))
        k0 += len(add_idx) + has_into
        din_refs = refs[k0:k0 + len(kept)]
        k0 += len(kept)
        copy_ref = refs[k0] if has_copy else None
        into_ref = refs[k0 + has_copy] if has_into else None
        dp_refs = refs[k0 + has_copy + has_into:]
        _, vjp = jax.vjp(fn, vals, pv)
        dvals, dpv = vjp(dys)
        cot = lambda idx: dvals[idx] + add_refs[idx][...] if idx in add_refs else dvals[idx]
        for d_ref, idx in zip(din_refs, kept):
            d_ref[...] = cot(idx).astype(d_ref.dtype)
        if has_copy:
            copy_ref[...] = cot(copy16).astype(copy_ref.dtype)
        if has_into:
            parts = [cot(idx) for idx in into_idx]
            into_ref[...] = (parts[0] if len(parts) == 1 else jnp.concatenate(parts, axis=1)).astype(into_ref.dtype)

        @pl.when(pl.program_id(1) == 0)
        def _():
            for d_ref in dp_refs:
                d_ref[...] = jnp.zeros_like(d_ref)

        for d_ref, d in zip(dp_refs, dpv):
            d_ref[...] += d

    in_specs = [pl.BlockSpec((r, cb), functools.partial(lambda j, i, off: (i, off + j), off=off)) for _, cb, off in ins]
    in_specs += [pl.BlockSpec(bs, functools.partial(lambda j, i, f: f(j), f=f)) for _, bs, f in params]
    in_specs += [pl.BlockSpec((r, d.shape[1] // ncol), lambda j, i: (i, j)) for d in douts]
    in_specs += [pl.BlockSpec((r, ins[i][1]), lambda j, i: (i, j)) for i in add_idx]
    out_specs = [pl.BlockSpec((r, ins[i][1]), lambda j, i: (i, j)) for i in kept]
    out_shape = [jax.ShapeDtypeStruct((n, ins[i][1] * ncol), din_dtypes[i]) for i in kept]
    args = [a for a, _, _ in ins] + [a for a, _, _ in params] + list(douts) + [add[i] for i in add_idx]
    aliases = {}
    if has_copy:
        out_specs += [pl.BlockSpec((r, ins[copy16][1]), lambda j, i: (i, j))]
        out_shape += [jax.ShapeDtypeStruct((n, ins[copy16][1] * ncol), bf16)]
    if has_into:
        assert ncol == 1
        in_specs += [pl.BlockSpec(memory_space=pl.ANY)]
        aliases[len(args)] = len(out_shape)
        args += [into_buf]
        out_specs += [pl.BlockSpec((r, sum(ins[i][1] for i in into_idx)), lambda j, i: (i, into_off))]
        out_shape += [jax.ShapeDtypeStruct(into_buf.shape, into_buf.dtype)]
    pshapes = [tuple(d for d in bs if d is not None) for _, bs, _ in params]
    out_specs += [pl.BlockSpec((None,) + ps, functools.partial(lambda j, i, nd: (j,) + (0,) * nd, nd=len(ps))) for ps in pshapes]
    out_shape += [jax.ShapeDtypeStruct((ncol,) + ps, f32) for ps in pshapes]
    res = pl.pallas_call(body, name=name, grid=(ncol, n // r), in_specs=in_specs, out_specs=out_specs, out_shape=out_shape,
                         input_output_aliases=aliases, compiler_params=_params())(*args)
    dins = [None] * nin
    for pos, i in enumerate(kept):
        dins[i] = res[pos]
    pos = len(kept)
    extras = {}
    if has_copy:
        extras["copy16"] = res[pos]
        pos += 1
    if has_into:
        extras["into"] = res[pos]
        pos += 1
    return dins, res[pos:], extras


SEQ_ROWS = 2048


def seqmap(fn, ins, params, nouts, ncol, name, out_dtype=f32):
    bsz, seq, _ = ins[0][0].shape
    r = min(SEQ_ROWS, seq)
    nin, npar = len(ins), len(params)

    def body(*refs):
        in_refs = refs[:nin]
        pv = [p[...] for p in refs[nin:nin + npar]]
        out_refs = refs[nin + npar:]

        def step(i, carry):
            r0 = pl.multiple_of(i * r, r)
            h0 = pl.multiple_of(jnp.maximum(r0 - 8, 0), 8)
            mains = [x[pl.ds(r0, r), :] for x in in_refs]
            halos = [jnp.where(i == 0, 0.0, x[pl.ds(h0, 8), :]) for x in in_refs]
            for o_ref, o in zip(out_refs, fn(mains, halos, pv)):
                o_ref[pl.ds(r0, r), :] = o.astype(o_ref.dtype)
            return carry

        lax.fori_loop(0, seq // r, step, 0)

    in_specs = [pl.BlockSpec((None, seq, LANE), functools.partial(lambda j, b, off: (b, 0, off + j), off=off)) for _, off in ins]
    in_specs += [pl.BlockSpec(bs, functools.partial(lambda j, b, f: f(j), f=f)) for _, bs, f in params]
    out_specs = [pl.BlockSpec((None, seq, LANE), lambda j, b: (b, 0, j)) for _ in range(nouts)]
    out_shape = [jax.ShapeDtypeStruct((bsz, seq, LANE * ncol), out_dtype) for _ in range(nouts)]
    return pl.pallas_call(body, name=name, grid=(ncol, bsz), in_specs=in_specs, out_specs=out_specs,
                          out_shape=out_shape, compiler_params=_params())(*[a for a, _ in ins], *[a for a, _, _ in params])


def seqmap_bwd(fn, ins, params, douts, ncol, name, din_dtype=f32, into=None):
    bsz, seq, _ = ins[0][0].shape
    r = min(SEQ_ROWS, seq)
    nin, npar, nout = len(ins), len(params), len(douts)
    narrow = din_dtype != f32

    def body(*refs):
        in_refs = refs[:nin]
        pv = [p[...] for p in refs[nin:nin + npar]]
        dy_refs = refs[nin + npar:nin + npar + nout]
        k0 = nin + npar + nout + (into is not None)
        dout_refs = refs[k0:k0 + nin]
        dp_refs = refs[k0 + nin:k0 + nin + npar]
        din_refs = refs[k0 + nin + npar:] if narrow else dout_refs

        def step(i, dp_acc):
            r0 = pl.multiple_of(i * r, r)
            h0 = pl.multiple_of(jnp.maximum(r0 - 8, 0), 8)
            mains = [x[pl.ds(r0, r), :] for x in in_refs]
            halos_raw = [x[pl.ds(h0, 8), :] for x in in_refs]

            def tile(mains, halos_raw, pv):
                return fn(mains, [jnp.where(i == 0, 0.0, h) for h in halos_raw], pv)

            _, vjp = jax.vjp(tile, mains, halos_raw, pv)
            dm, dh, dp = vjp([d[pl.ds(r0, r), :] for d in dy_refs])
            for d_ref, m, h in zip(din_refs, dm, dh):
                d_ref[pl.ds(r0, r), :] = m
                d_ref[pl.ds(h0, 8), :] += h
            return [acc + d for acc, d in zip(dp_acc, dp)]

        dp = lax.fori_loop(0, seq // r, step, [jnp.zeros(p.shape, f32) for p in pv])
        if narrow:
            for o_ref, d_ref in zip(dout_refs, din_refs):
                o_ref[...] = d_ref[...].astype(o_ref.dtype)

        @pl.when(pl.program_id(1) == 0)
        def _():
            for d_ref in dp_refs:
                d_ref[...] = jnp.zeros_like(d_ref)

        for d_ref, d in zip(dp_refs, dp):
            d_ref[...] += d

    in_specs = [pl.BlockSpec((None, seq, LANE), functools.partial(lambda j, b, off: (b, 0, off + j), off=off)) for _, off in ins]
    in_specs += [pl.BlockSpec(bs, functools.partial(lambda j, b, f: f(j), f=f)) for _, bs, f in params]
    in_specs += [pl.BlockSpec((None, seq, LANE), lambda j, b: (b, 0, j)) for _ in range(nout)]
    out_specs = [pl.BlockSpec((None, seq, LANE), lambda j, b: (b, 0, j)) for _ in range(nin)]
    pshapes = [tuple(d for d in bs if d is not None) for _, bs, _ in params]
    out_specs += [pl.BlockSpec((None,) + ps, functools.partial(lambda j, b, nd: (j,) + (0,) * nd, nd=len(ps))) for ps in pshapes]
    out_shape = [jax.ShapeDtypeStruct((bsz, seq, LANE * ncol), din_dtype) for _ in range(nin)]
    out_shape += [jax.ShapeDtypeStruct((ncol,) + ps, f32) for ps in pshapes]
    args = [a for a, _ in ins] + [a for a, _, _ in params] + list(douts)
    aliases = {}
    if into is not None:
        assert nin == 1 and into[0].dtype == din_dtype
        in_specs += [pl.BlockSpec(memory_space=pl.ANY)]
        aliases[len(args)] = 0
        args += [into[0]]
        out_specs[0] = pl.BlockSpec((None, seq, LANE), lambda j, b: (b, 0, into[1] + j))
        out_shape[0] = jax.ShapeDtypeStruct(into[0].shape, din_dtype)
    res = pl.pallas_call(body, name=name, grid=(ncol, bsz), in_specs=in_specs, out_specs=out_specs, out_shape=out_shape,
                         scratch_shapes=[pltpu.VMEM((seq, LANE), f32) for _ in range(nin)] if narrow else [],
                         input_output_aliases=aliases, compiler_params=_params())(*args)
    return res[:nin], res[nin:]


def _tri_masks():
    row = lax.broadcasted_iota(jnp.int32, (CHUNK, CHUNK), 0)
    col = lax.broadcasted_iota(jnp.int32, (CHUNK, CHUNK), 1)
    return row >= col, row > col


CHUNKS_PER_STEP = 4


def _by_rows(parts, per_row):
    rows = [jnp.concatenate(parts[i:i + per_row], axis=1) for i in range(0, len(parts), per_row)]
    return jnp.concatenate(rows, axis=0)


def dn_prep(vals, ps):
    q, k, v, gb = vals[:4]
    nchunk = q.shape[0] // CHUNK
    causal, strict = _tri_masks()
    gbs = [gb[c * CHUNK:(c + 1) * CHUNK] for c in range(nchunk)]
    gcs = [cumsum_rows(g) for g in gbs]
    gcts = [g.T for g in gcs]
    chains = [(c, h) for c in range(nchunk) for h in range(DN_HEADS)]
    part = lambda t, c, h: t[c * CHUNK:(c + 1) * CHUNK, h * DN_DK:(h + 1) * DN_DK]
    qh = [part(q, c, h) for c, h in chains]
    kh = [part(k, c, h) for c, h in chains]
    vh = [part(v, c, h) for c, h in chains]
    g_col = [_lane_pick(gcs[c], h) for c, h in chains]
    beta = [_lane_pick(gbs[c], DN_HEADS + h) for c, h in chains]
    g_row = [_row_pick(gcts[c], h)[:, :CHUNK] for c, h in chains]
    decay = [jnp.where(causal, jnp.exp(jnp.where(causal, gc - gr, 0.0)), 0.0) for gc, gr in zip(g_col, g_row)]
    k_beta = [a * b for a, b in zip(kh, beta)]
    eg = [jnp.exp(g) for g in g_col]
    kk = [bdot(a, b, "nt") for a, b in zip(k_beta, kh)]
    qk = [bdot(a, b, "nt") for a, b in zip(qh, kh)]
    lower = [jnp.where(strict, a * d, 0.0) for a, d in zip(kk, decay)]
    if len(vals) == 5:
        t_inv = known_inverse([part(vals[4], c, h)[:, :CHUNK] for c, h in chains], lower)
    else:
        t_inv = unit_lower_inv_all(lower)
    u = [bdot(t, a * b, "nn") for t, a, b in zip(t_inv, vh, beta)]
    w = [bdot(t, a * e, "nn") for t, a, e in zip(t_inv, k_beta, eg)]
    attn = [jnp.concatenate([a * d, jnp.zeros((CHUNK, DN_DK - CHUNK), f32)], axis=1) for a, d in zip(qk, decay)]
    qd = [a * e for a, e in zip(qh, eg)]
    kd = [a * jnp.exp(_row_pick(g, CHUNK - 1) - g) for a, g in zip(kh, g_col)]
    g_last = jnp.concatenate([jnp.broadcast_to(_row_pick(g, CHUNK - 1), g.shape) for g in gcs], axis=0)
    outs = [_by_rows(t, DN_HEADS) for t in (u, w, attn, qd, kd)] + [g_last]
    if len(vals) == 4:
        wide = [jnp.concatenate([t, jnp.zeros((CHUNK, DN_DK - CHUNK), f32)], axis=1) for t in t_inv]
        outs.append(_by_rows(wide, DN_HEADS))
    return outs


def dn_step(state, u, w, attn, qd, kd, g_last):
    bsz = u.shape[0]
    chains = [(b, h) for b in range(bsz) for h in range(DN_HEADS)]
    part = lambda t, b, h: t[b, :, h * DN_DK:(h + 1) * DN_DK]
    ws = [bdot(part(w, b, h), s, "nn") for (b, h), s in zip(chains, state)]
    qs = [bdot(part(qd, b, h), s, "nn") for (b, h), s in zip(chains, state)]
    v_new = [part(u, b, h) - x for (b, h), x in zip(chains, ws)]
    av = [bdot(attn[b, :, h * DN_DK:h * DN_DK + CHUNK], x, "nn") for (b, h), x in zip(chains, v_new)]
    kv = [bdot(part(kd, b, h), x, "tn") for (b, h), x in zip(chains, v_new)]
    ge = [jnp.exp(_row_pick(_lane_pick(g_last[b], h), 0)) for b, h in chains]
    new_state = [s * g + x for s, g, x in zip(state, ge, kv)]
    outs = [a + b for a, b in zip(qs, av)]
    return new_state, jnp.concatenate([jnp.concatenate(outs[b * DN_HEADS:(b + 1) * DN_HEADS], axis=1)[None]
                                       for b in range(bsz)], axis=0)


def _ret_log_gamma(h):
    return math.log(1.0 - 2.0 ** (-5.0 - h))


def ret_prep(vals, ps):
    q, k, v = vals
    nchunk = q.shape[0] // CHUNK
    causal, _ = _tri_masks()
    row = lax.broadcasted_iota(jnp.int32, (CHUNK, CHUNK), 0)
    col = lax.broadcasted_iota(jnp.int32, (CHUNK, CHUNK), 1)
    dist = (row - col).astype(f32)
    lane = lax.broadcasted_iota(jnp.int32, (CHUNK, q.shape[1]), 1)
    dmask = [jnp.where(causal, jnp.exp(jnp.where(causal, dist, 0.0) * _ret_log_gamma(h)), 0.0) for h in range(RET_HEADS)]
    chains = [(c, h) for c in range(nchunk) for h in range(RET_HEADS)]
    rows = lambda t, c: t[c * CHUNK:(c + 1) * CHUNK]
    scores = [bdot(jnp.where((lane // RET_DK) == h, rows(q, c), 0.0), rows(k, c), "nt") * dmask[h] for c, h in chains]
    inner = [bdot(s, rows(v, c)[:, h * RET_DV:(h + 1) * RET_DV], "nn") for s, (c, h) in zip(scores, chains)]
    return [_by_rows(inner, RET_HEADS)]


def ret_step(state, q, k, v, inner):
    bsz = q.shape[0]
    idx = lax.broadcasted_iota(jnp.int32, (CHUNK, 1), 0).astype(f32)
    lane = lax.broadcasted_iota(jnp.int32, (CHUNK, q.shape[2]), 1)
    chains = [(b, h) for b in range(bsz) for h in range(RET_HEADS)]
    part = lambda t, b, h: t[b, :, h * RET_DV:(h + 1) * RET_DV]
    cross = [bdot(q[b], s, "nn") for (b, h), s in zip(chains, state)]
    kz = [jnp.where((lane // RET_DK) == h, k[b], 0.0) * jnp.exp((CHUNK - 1.0 - idx) * _ret_log_gamma(h)) for b, h in chains]
    kv = [bdot(a, part(v, b, h), "tn") for a, (b, h) in zip(kz, chains)]
    outs = [x * jnp.exp((idx + 1.0) * _ret_log_gamma(h)) + part(inner, b, h) for x, (b, h) in zip(cross, chains)]
    new_state = [s * math.exp(CHUNK * _ret_log_gamma(h)) + x for s, x, (b, h) in zip(state, kv, chains)]
    return new_state, jnp.concatenate([jnp.concatenate(outs[b * RET_HEADS:(b + 1) * RET_HEADS], axis=1)[None]
                                       for b in range(bsz)], axis=0)


def chunk_scan(step_fn, ins, state_shape, out_width, name):
    bsz, seq, _ = ins[0].shape
    nchunk = seq // CHUNK
    nin = len(ins)
    nh = state_shape[0]

    def body(*refs):
        in_refs = refs[:nin]
        o_ref, ck_ref, s_ref = refs[nin:]

        @pl.when(pl.program_id(0) == 0)
        def _():
            s_ref[...] = jnp.zeros_like(s_ref)

        state = [s_ref[i] for i in range(bsz * nh)]
        for i in range(bsz * nh):
            ck_ref[i // nh, i % nh] = state[i]
        new_state, out = step_fn(state, *[x[...] for x in in_refs])
        o_ref[...] = out
        for i in range(bsz * nh):
            s_ref[i] = new_state[i]

    in_specs = [pl.BlockSpec((bsz, CHUNK, x.shape[2]), lambda n: (0, n, 0)) for x in ins]
    out_specs = [pl.BlockSpec((bsz, CHUNK, out_width), lambda n: (0, n, 0)),
                 pl.BlockSpec((bsz, None) + tuple(state_shape), lambda n: (0, n, 0, 0, 0))]
    out_shape = [jax.ShapeDtypeStruct((bsz, seq, out_width), f32),
                 jax.ShapeDtypeStruct((bsz, nchunk) + tuple(state_shape), f32)]
    return pl.pallas_call(body, name=name, grid=(nchunk,), in_specs=in_specs, out_specs=out_specs, out_shape=out_shape,
                          scratch_shapes=[pltpu.VMEM((bsz * nh,) + tuple(state_shape[1:]), f32)],
                          compiler_params=_params())(*ins)


def chunk_scan_bwd(step_fn, ins, ckpt, dout, name):
    bsz, seq, _ = ins[0].shape
    nchunk = seq // CHUNK
    nin = len(ins)
    state_shape = ckpt.shape[2:]
    nh = state_shape[0]

    def body(*refs):
        in_refs = refs[:nin]
        ck_ref, do_ref = refs[nin:nin + 2]
        din_refs = refs[nin + 2:nin + 2 + nin]
        ds_ref = refs[-1]

        @pl.when(pl.program_id(0) == 0)
        def _():
            ds_ref[...] = jnp.zeros_like(ds_ref)

        state = [ck_ref[i // nh, i % nh] for i in range(bsz * nh)]
        _, vjp = jax.vjp(step_fn, state, *[x[...] for x in in_refs])
        grads = vjp(([ds_ref[i] for i in range(bsz * nh)], do_ref[...]))
        for i in range(bsz * nh):
            ds_ref[i] = grads[0][i]
        for d_ref, d in zip(din_refs, grads[1:]):
            d_ref[...] = d

    rev = lambda n: (0, nchunk - 1 - n, 0)
    in_specs = [pl.BlockSpec((bsz, CHUNK, x.shape[2]), rev) for x in ins]
    in_specs += [pl.BlockSpec((bsz, None) + tuple(state_shape), lambda n: (0, nchunk - 1 - n, 0, 0, 0)),
                 pl.BlockSpec((bsz, CHUNK, dout.shape[2]), rev)]
    out_specs = [pl.BlockSpec((bsz, CHUNK, x.shape[2]), rev) for x in ins]
    out_shape = [jax.ShapeDtypeStruct(x.shape, f32) for x in ins]
    return pl.pallas_call(body, name=name, grid=(nchunk,), in_specs=in_specs, out_specs=out_specs, out_shape=out_shape,
                          scratch_shapes=[pltpu.VMEM((bsz * nh,) + tuple(state_shape[1:]), f32)],
                          compiler_params=_params())(*ins, ckpt, dout)


LRU_ROWS = 512


def lru_scan(a, b):
    bsz, seq, width = a.shape
    rb = min(LRU_ROWS, seq)

    def body(a_ref, b_ref, h_ref, hp_ref, carry_ref):
        @pl.when(pl.program_id(1) == 0)
        def _():
            carry_ref[...] = jnp.zeros_like(carry_ref)

        row = lax.broadcasted_iota(jnp.int32, (8, width), 0)

        def tile(t, hprev):
            r0 = pl.multiple_of(t * 8, 8)
            ca, cbv = a_ref[pl.ds(r0, 8), :], b_ref[pl.ds(r0, 8), :]
            for s in (1, 2, 4):
                m = row >= s
                cbv = jnp.where(m, ca * pltpu.roll(cbv, s, 0) + cbv, cbv)
                ca = jnp.where(m, ca * pltpu.roll(ca, s, 0), ca)
            h = cbv + ca * hprev
            h_ref[pl.ds(r0, 8), :] = h
            hp_ref[pl.ds(r0, 8), :] = jnp.where(row == 0, hprev, pltpu.roll(h, 1, 0))
            return _row_pick(h, 7)

        carry_ref[0:1, :] = lax.fori_loop(0, rb // 8, tile, carry_ref[0:1, :])

    spec = pl.BlockSpec((None, rb, width), lambda bi, i: (bi, i, 0))
    return pl.pallas_call(body, name="lru_scan", grid=(bsz, seq // rb), in_specs=[spec, spec], out_specs=[spec, spec],
                          out_shape=[jax.ShapeDtypeStruct(a.shape, f32)] * 2,
                          scratch_shapes=[pltpu.VMEM((8, width), f32)], compiler_params=_params())(a, b)


def lru_scan_bwd(a, hp, dh):
    bsz, seq, width = a.shape
    rb = min(LRU_ROWS, seq)
    nblk = seq // rb

    def body(a_ref, hp_ref, dh_ref, da_ref, db_ref, carry_ref):
        @pl.when(pl.program_id(1) == 0)
        def _():
            carry_ref[...] = jnp.zeros_like(carry_ref)

        row = lax.broadcasted_iota(jnp.int32, (8, width), 0)
        ntile = rb // 8

        def tile(t, mu_next):
            r0 = pl.multiple_of((ntile - 1 - t) * 8, 8)
            ca, dh_t = a_ref[pl.ds(r0, 8), :], dh_ref[pl.ds(r0, 8), :]
            cbv = ca * dh_t
            for s in (1, 2, 4):
                m = row < 8 - s
                cbv = jnp.where(m, ca * pltpu.roll(cbv, 8 - s, 0) + cbv, cbv)
                ca = jnp.where(m, ca * pltpu.roll(ca, 8 - s, 0), ca)
            mu = cbv + ca * mu_next
            lam = dh_t + jnp.where(row == 7, mu_next, pltpu.roll(mu, 7, 0))
            db_ref[pl.ds(r0, 8), :] = lam
            da_ref[pl.ds(r0, 8), :] = lam * hp_ref[pl.ds(r0, 8), :]
            return _row_pick(mu, 0)

        carry_ref[0:1, :] = lax.fori_loop(0, ntile, tile, carry_ref[0:1, :])

    spec = pl.BlockSpec((None, rb, width), lambda bi, i: (bi, nblk - 1 - i, 0))
    return pl.pallas_call(body, name="lru_scan_bwd", grid=(bsz, nblk), in_specs=[spec] * 3, out_specs=[spec, spec],
                          out_shape=[jax.ShapeDtypeStruct(a.shape, f32)] * 2,
                          scratch_shapes=[pltpu.VMEM((8, width), f32)], compiler_params=_params())(a, hp, dh)


def final_loss(x, g, target):
    n, d = x.shape
    r = min(256, n)

    def body(x_ref, g_ref, t_ref, loss_ref, dx_ref, dg_ref, dx16_ref):
        @pl.when(pl.program_id(0) == 0)
        def _():
            loss_ref[...] = jnp.zeros_like(loss_ref)
            dg_ref[...] = jnp.zeros_like(dg_ref)

        tgt = t_ref[...]

        def loss_fn(xv, gv):
            y = f_norm([xv], [gv])[0]
            return 0.5 * jnp.sum(jnp.mean(jnp.square(y - tgt), axis=-1, keepdims=True), axis=0, keepdims=True)

        val, vjp = jax.vjp(loss_fn, x_ref[...], g_ref[...])
        dx, dg = vjp(jnp.ones_like(val))
        loss_ref[...] += val
        dx_ref[...] = dx
        dx16_ref[...] = dx.astype(dx16_ref.dtype)
        dg_ref[...] += dg

    row = pl.BlockSpec((r, d), lambda i: (i, 0))
    return pl.pallas_call(
        body, name="final_loss", grid=(n // r,), in_specs=[row, pl.BlockSpec((1, d), lambda i: (0, 0)), row],
        out_specs=[pl.BlockSpec((8, LANE), lambda i: (0, 0)), row, pl.BlockSpec((1, d), lambda i: (0, 0)), row],
        out_shape=[jax.ShapeDtypeStruct((8, LANE), f32), jax.ShapeDtypeStruct((n, d), f32), jax.ShapeDtypeStruct((1, d), f32),
                   jax.ShapeDtypeStruct((n, d), bf16)],
        compiler_params=_params())(x, g, target)


_HBM = pl.BlockSpec(memory_space=pltpu.HBM)
_SEM = pl.BlockSpec(memory_space=pltpu.SEMAPHORE)
_EFFECT = pltpu.SideEffectType.DATAFLOW_SIDE_EFFECTING


def _peer(k):
    mx, my, mc = lax.axis_index("x"), lax.axis_index("y"), lax.axis_index("c")
    px, py, pc = (mx + (k >> 2)) % 2, (my + ((k >> 1) & 1)) % 2, (mc + (k & 1)) % 2
    return (px, py, pc), 4 * px + 2 * py + pc


def _peer_copy(k, i, x_ref, land_ref, send_sems, recv_sems, scatter):
    me = 4 * lax.axis_index("x") + 2 * lax.axis_index("y") + lax.axis_index("c")
    dev, slot = _peer(k)
    sem = i * (N_DEV - 1) + k - 1
    return pltpu.make_async_remote_copy(
        src_ref=x_ref.at[slot] if scatter else x_ref, dst_ref=land_ref.at[me], send_sem=send_sems.at[sem],
        recv_sem=recv_sems.at[sem], device_id=dev, device_id_type=pl.DeviceIdType.MESH)


def exchange_start(xs, scatters, name):
    nx = len(xs)
    lands = [lax.empty((N_DEV,) + tuple(x.shape[1:] if sc else x.shape), x.dtype) for x, sc in zip(xs, scatters)]
    nsem = nx * (N_DEV - 1)

    def body(*refs):
        x_refs, land_refs = refs[:nx], refs[nx:2 * nx]
        send_sems, recv_sems = refs[2 * nx:2 * nx + 2]
        token = refs[-1]
        for i in range(nx):
            for k in range(1, N_DEV):
                _peer_copy(k, i, x_refs[i], land_refs[i], send_sems, recv_sems, scatters[i]).start()
        token[...] = jnp.zeros_like(token)

    hbm = lambda a: pltpu.HBM(a.shape, a.dtype)
    res = pl.pallas_call(
        body, name=name, in_specs=(_HBM,) * (2 * nx),
        out_specs=(_SEM, _SEM) + (_HBM,) * (2 * nx) + (pl.BlockSpec(memory_space=pltpu.VMEM),),
        input_output_aliases={i: 2 + i for i in range(2 * nx)},
        out_shape=(pltpu.SemaphoreType.DMA((nsem,)), pltpu.SemaphoreType.DMA((nsem,)), *[hbm(a) for a in xs],
                   *[hbm(a) for a in lands], jax.ShapeDtypeStruct((8, LANE), f32)),
        compiler_params=pltpu.CompilerParams(has_side_effects=_EFFECT),
    )(*[pltpu.with_memory_space_constraint(a, pltpu.HBM) for a in list(xs) + lands])
    return (res[0], res[1], list(res[2:2 + nx]), list(res[2 + nx:2 + 2 * nx]), tuple(scatters)), res[-1]


def exchange_wait(started, after, name):
    send_sems, recv_sems, x_thrus, land_thrus, scatters = started
    nx = len(x_thrus)

    def body(*refs):
        x_refs, land_refs = refs[:nx], refs[nx:2 * nx]
        send_sems, recv_sems = refs[2 * nx:2 * nx + 2]
        for i in range(nx):
            for k in range(1, N_DEV):
                cp = _peer_copy(k, i, x_refs[i], land_refs[i], send_sems, recv_sems, scatters[i])
                cp.wait_send()
                cp.wait_recv()

    hbm = lambda a: pltpu.HBM(a.shape, a.dtype)
    res = pl.pallas_call(
        body, name=name, in_specs=(_HBM,) * (2 * nx) + (_SEM, _SEM, pl.BlockSpec(memory_space=pl.ANY)),
        out_specs=(_HBM,) * (2 * nx), input_output_aliases={i: i for i in range(2 * nx)},
        out_shape=tuple(hbm(a) for a in list(x_thrus) + list(land_thrus)),
        compiler_params=pltpu.CompilerParams(has_side_effects=_EFFECT),
    )(*x_thrus, *land_thrus, send_sems, recv_sems, after)
    return list(res[:nx]), list(res[nx:])


def sum_slots(x, name, own=None):
    _, rows_total, cols = x.shape
    row_bytes = N_DEV * ((cols + LANE - 1) // LANE) * LANE * x.dtype.itemsize
    r = _pick_rows(rows_total, max(16, (4 * 1024 * 1024) // row_bytes // 16 * 16))
    if own is not None:
        def body_own(x_ref, own_ref, o_ref):
            me = 4 * lax.axis_index("x") + 2 * lax.axis_index("y") + lax.axis_index("c")
            acc = None
            for s in range(N_DEV):
                v = jnp.where(me == s, own_ref[...], x_ref[s]).astype(f32)
                acc = v if acc is None else acc + v
            o_ref[...] = acc

        return pl.pallas_call(body_own, name=name, grid=(rows_total // r,),
                              in_specs=[pl.BlockSpec((N_DEV, r, cols), lambda i: (0, i, 0)), pl.BlockSpec((r, cols), lambda i: (i, 0))],
                              out_specs=pl.BlockSpec((r, cols), lambda i: (i, 0)),
                              out_shape=jax.ShapeDtypeStruct((rows_total, cols), f32), compiler_params=_params())(x, own)

    def body(x_ref, o_ref):
        acc = x_ref[0].astype(f32)
        for s in range(1, N_DEV):
            acc = acc + x_ref[s].astype(f32)
        o_ref[...] = acc

    return pl.pallas_call(body, name=name, grid=(rows_total // r,),
                          in_specs=[pl.BlockSpec((N_DEV, r, cols), lambda i: (0, i, 0))],
                          out_specs=pl.BlockSpec((r, cols), lambda i: (i, 0)),
                          out_shape=jax.ShapeDtypeStruct((rows_total, cols), f32), compiler_params=_params())(x)


def _pick_rows(total, pref):
    best = None
    for d in range(16, min(total, pref) + 1, 16):
        if total % d == 0:
            best = d
    return best if best is not None else total


def adamw(w, g, m, v, name):
    shape = w.shape
    if w.ndim == 1:
        w2, g2, m2, v2 = (t.reshape(1, -1) for t in (w, g, m, v))
    else:
        w2, g2, m2, v2 = (t.reshape(-1, shape[-1]) for t in (w, g, m, v))
    rows_total, cols = w2.shape
    r = _pick_rows(rows_total, max(16, (512 * 1024) // max(cols, 1) // 16 * 16))
    c1, c2 = 1.0 / (1.0 - ADAM_B1 ** ADAM_STEP), 1.0 / (1.0 - ADAM_B2 ** ADAM_STEP)

    def body(w_ref, g_ref, m_ref, v_ref, d_ref, nm_ref, nv_ref):
        gv = g_ref[...]
        nm = ADAM_B1 * m_ref[...] + (1.0 - ADAM_B1) * gv
        nv = ADAM_B2 * v_ref[...] + (1.0 - ADAM_B2) * jnp.square(gv)
        d_ref[...] = -ADAM_LR * ((nm * c1) / (jnp.sqrt(nv * c2) + ADAM_EPS) + ADAM_WD * w_ref[...])
        nm_ref[...] = nm
        nv_ref[...] = nv

    spec = pl.BlockSpec((r, cols), lambda i: (i, 0))
    outs = pl.pallas_call(body, name=name, grid=(rows_total // r,), in_specs=[spec] * 4, out_specs=[spec] * 3,
                          out_shape=[jax.ShapeDtypeStruct((rows_total, cols), f32)] * 3, compiler_params=_params())(w2, g2, m2, v2)
    return tuple(o.reshape(shape) for o in outs)


def _const(j):
    return lambda _: j


def _layer_fwd(x, wl, fetch_rest, cos, sin, bsz, seq):
    n = x.shape[0]
    sv = {"x_in": x}
    row1 = lambda a: (a, (1, a.shape[1]), lambda j: (0, 0))
    h = rowmap(f_norm, [(x, D_MODEL, 0)], [row1(wl["attn_norm"])], [(D_MODEL, bf16)], 1, "norm_fwd")[0]
    u = mm(h, wl["w_in"], "nn", "mm_in")
    sv["h"], sv["u"] = h, u
    u3 = u.reshape(bsz, seq, U_PAD)
    wl = dict(wl)
    wl.update(fetch_rest(u))
    sv["wl"] = wl

    qkv = []
    for kind in range(3):
        cw = (wl["dn_conv_w"], (4, LANE), functools.partial(lambda j, kind: (0, 4 * kind + j), kind=kind))
        qkv.append(seqmap(functools.partial(f_dn_pre, kind), [(u3, U_QKV // LANE + 4 * kind)], [cw], 1, 4, "dn_pre%d" % kind)[0])
    gb = rowmap(f_dn_gates, [(u, 512, U_AB // 512)], [(wl["dn_gate_p"], (8, LANE), lambda j: (0, 0))], [(LANE, f32)], 1,
                "dn_gates", rows=512)[0]
    gb3 = gb.reshape(bsz, seq, LANE)
    crow = CHUNK * CHUNKS_PER_STEP
    dn_in = [(t.reshape(n, 512), 512, 0) for t in qkv] + [(gb, LANE, 0)]
    prep_a = rowmap(dn_prep, dn_in, [], [(512, f32)] * 5 + [(LANE, f32), (512, f32)], 1, "dn_prep", rows=crow)
    dn_in = dn_in + [(prep_a[6], 512, 0)]
    prep_a = [t.reshape(bsz, seq, t.shape[1]) for t in prep_a[:6]]
    o_a, ck_a = chunk_scan(dn_step, prep_a, (DN_HEADS, DN_DK, DN_DK), 512, "dn_scan")
    y_a = rowmap(per_head(f_dn_post), [(o_a.reshape(n, 512), 512, 0), (u, 512, U_Z // 512)],
                 [(wl["dn_norm_w"], (1, LANE), lambda j: (0, 0))], [(512, bf16)], 1, "dn_post")[0]
    sv.update(dn_in=dn_in, prep_a=prep_a, o_a=o_a, ck_a=ck_a, y_a=y_a)

    q_b, k_b = rowmap(f_ret_pre, [(u, 256, U_RQ // 256), (u, 256, U_RK // 256), (cos, 256, 0), (sin, 256, 0)], [],
                      [(256, f32), (256, f32)], 1, "ret_pre")
    q_b3, k_b3 = q_b.reshape(bsz, seq, 256), k_b.reshape(bsz, seq, 256)
    v_b3 = lax.slice_in_dim(u3, U_RV, U_RV + 512, axis=2)
    ret_in = [(q_b, 256, 0), (k_b, 256, 0), (u, 512, U_RV // 512)]
    inner = rowmap(ret_prep, ret_in, [], [(512, f32)], 1, "ret_prep", rows=crow)[0]
    ret_seq = [q_b3, k_b3, v_b3, inner.reshape(bsz, seq, 512)]
    o_b, ck_b = chunk_scan(ret_step, ret_seq, (RET_HEADS, 256, RET_DV), 512, "ret_scan")
    y_b = rowmap(per_head(f_ret_post), [(o_b.reshape(n, 512), 512, 0), (u, 512, U_RG // 512)], [], [(512, bf16)], 1,
                 "ret_post")[0]
    sv.update(ret_in=ret_in, ret_seq=ret_seq, o_b=o_b, ck_b=ck_b, y_b=y_b)

    lru_params = _lru_params(wl)
    a_c, b_c = seqmap(f_lru_pre, [(u3, U_CX // LANE)], lru_params, 2, 4, "lru_pre")
    h_c, hp_c = lru_scan(a_c, b_c)
    y_c = rowmap(f_lru_post, [(h_c.reshape(n, 512), 512, 0), (u, 512, U_CG // 512)], [], [(512, bf16)], 1, "lru_post")[0]
    sv.update(a_c=a_c, hp_c=hp_c, h_c=h_c, y_c=y_c)

    br = [mm(y, wl["w_branch"][i], "nn", "mm_branch") for i, y in enumerate((y_a, y_b, y_c))]
    merged = rowmap(f_merge, [(u, D_MODEL, i) for i in range(3)] + [(b, D_MODEL, 0) for b in br], [], [(D_MODEL, bf16)], 1,
                    "merge")[0]
    x_mid = mm(merged, wl["w_out"], "nn", "mm_out", add=x)
    sv.update(br=br, merged=merged, x_mid=x_mid)

    h2 = rowmap(f_norm, [(x_mid, D_MODEL, 0)], [row1(wl["ffn_norm"])], [(D_MODEL, bf16)], 1, "norm_fwd")[0]
    up = mm(h2, wl["w_up"], "nn", "mm_up")
    act = seqmap(f_ffn_mid, [(up.reshape(bsz, seq, 2 * D_FF), 0), (up.reshape(bsz, seq, 2 * D_FF), D_FF // LANE)],
                 _ffn_params(wl), 1, D_FF // LANE, "ffn_mid", out_dtype=bf16)[0]
    act = act.reshape(n, D_FF)
    x_out = mm(act, wl["w_down"], "nn", "mm_down", add=x_mid)
    sv.update(h2=h2, up=up, act=act)
    return x_out, sv


def _lru_params(wl):
    col = lambda a: (a, (a.shape[0], LANE), lambda j: (0, j))
    blk = lambda a: (a, (None, LANE, LANE), lambda j: (j, 0, 0))
    return [col(wl["lru_conv_w"]), col(wl["lru_conv_b"]), blk(wl["lru_wa"]), col(wl["lru_ba"]), blk(wl["lru_wx"]),
            col(wl["lru_bx"]), col(wl["lru_lambda"])]


def _ffn_params(wl):
    nb = D_FF // LANE
    return [(wl["ffn_conv_w"], (3, LANE), lambda j: (0, j)), (wl["ffn_conv_w"], (3, LANE), lambda j: (0, nb + j)),
            (wl["ffn_conv_b"], (1, LANE), lambda j: (0, j)), (wl["ffn_conv_b"], (1, LANE), lambda j: (0, nb + j))]


def _layer_bwd(dx, dx16, sv, cos, sin, bsz, seq, emit, dep):
    n = dx.shape[0]
    gr = {}
    wl = sv["wl"]
    u, x_in, x_mid = sv["u"], sv["x_in"], sv["x_mid"]
    u3 = u.reshape(bsz, seq, U_PAD)
    row1 = lambda a: (a, (1, a.shape[1]), lambda j: (0, 0))

    d_act = mm(dx16, wl["w_down"], "nt", "mm_down_dx", dep=dep)
    gr["w_down"] = mm(sv["act"], dx16, "tn", "mm_down_dw")
    up3 = sv["up"].reshape(bsz, seq, 2 * D_FF)
    (d_gate, d_val), dps = seqmap_bwd(f_ffn_mid, [(up3, 0), (up3, D_FF // LANE)], _ffn_params(wl),
                                      [d_act.reshape(bsz, seq, D_FF)], D_FF // LANE, "ffn_mid_bwd", din_dtype=bf16)
    gr["ffn_conv_w"] = jnp.concatenate([_cols(dps[0]), _cols(dps[1])], axis=1)
    gr["ffn_conv_b"] = jnp.concatenate([_cols(dps[2]), _cols(dps[3])], axis=1)[0]
    d_up = jnp.concatenate([d_gate, d_val], axis=2).reshape(n, 2 * D_FF)
    gr["w_up"] = mm(sv["h2"], d_up, "tn", "mm_up_dw")
    token = emit("ffn", {k: gr[k] for k in ("w_up", "w_down")})
    d_h2 = mm(d_up, wl["w_up"], "nt", "mm_up_dx", dep=token)
    (dx_mid,), (dg,), ex = rowmap_bwd(f_norm, [(x_mid, D_MODEL, 0)], [row1(wl["ffn_norm"])], [d_h2], 1, "norm_bwd", add=[dx],
                                      copy16=0)
    dx_mid16 = ex["copy16"]
    gr["ffn_norm"] = dg[0, 0]

    du = lax.empty((n, U_PAD), bf16)
    du3 = lambda: du.reshape(bsz, seq, U_PAD)

    d_merged = mm(dx_mid16, wl["w_out"], "nt", "mm_out_dx")
    gr["w_out"] = mm(sv["merged"], dx_mid16, "tn", "mm_out_dw")
    dm, _, ex = rowmap_bwd(f_merge, [(u, D_MODEL, i) for i in range(3)] + [(b, D_MODEL, 0) for b in sv["br"]], [], [d_merged],
                           1, "merge_bwd", din_dtypes=[bf16] * 6, into=(du, U_GATES // (3 * D_MODEL), [0, 1, 2]))
    du, d_br = ex["into"], dm[3:]
    ys = (sv["y_a"], sv["y_b"], sv["y_c"])
    d_ys = [mm(d_br[i], wl["w_branch"][i], "nt", "mm_branch_dx") for i in range(3)]
    gr["w_branch"] = jnp.stack([mm(ys[i], d_br[i], "tn", "mm_branch_dw") for i in range(3)])

    (d_hc, _), _, ex = rowmap_bwd(f_lru_post, [(sv["h_c"].reshape(n, 512), 512, 0), (u, 512, U_CG // 512)], [], [d_ys[2]], 1,
                                  "lru_post_bwd", into=(du, U_CG // 512, [1]))
    du = ex["into"]
    d_a, d_b = lru_scan_bwd(sv["a_c"], sv["hp_c"], d_hc.reshape(bsz, seq, 512))
    (du_new,), dps = seqmap_bwd(f_lru_pre, [(u3, U_CX // LANE)], _lru_params(wl), [d_a, d_b], 4, "lru_pre_bwd", din_dtype=bf16,
                                into=(du3(), U_CX // LANE))
    du = du_new.reshape(n, U_PAD)
    gr["lru_conv_w"], gr["lru_conv_b"] = _cols(dps[0]), _cols(dps[1])[0]
    gr["lru_wa"], gr["lru_ba"], gr["lru_wx"], gr["lru_bx"] = dps[2], dps[3][:, 0], dps[4], dps[5][:, 0]
    gr["lru_lambda"] = _cols(dps[6])[0]

    (d_ob, _), _, ex = rowmap_bwd(per_head(f_ret_post), [(sv["o_b"].reshape(n, 512), 512, 0), (u, 512, U_RG // 512)], [],
                                  [d_ys[1]], 1, "ret_post_bwd", into=(du, U_RG // 512, [1]))
    du = ex["into"]
    crow = CHUNK * CHUNKS_PER_STEP
    d_ret = chunk_scan_bwd(ret_step, sv["ret_seq"], sv["ck_b"], d_ob.reshape(bsz, seq, 512), "ret_scan_bwd")
    d_ret = [t.reshape(n, t.shape[2]) for t in d_ret]
    (d_qb, d_kb, _), _, ex = rowmap_bwd(ret_prep, sv["ret_in"], [], [d_ret[3]], 1, "ret_prep_bwd", rows=crow, add=d_ret[:3],
                                        into=(du, U_RV // 512, [2]))
    du = ex["into"]
    _, _, ex = rowmap_bwd(f_ret_pre, [(u, 256, U_RQ // 256), (u, 256, U_RK // 256), (cos, 256, 0), (sin, 256, 0)], [],
                          [d_qb, d_kb], 1, "ret_pre_bwd", din_dtypes=[f32, f32, None, None], into=(du, U_RQ // 512, [0, 1]))
    du = ex["into"]

    (d_oa, _), (dnw,), ex = rowmap_bwd(per_head(f_dn_post), [(sv["o_a"].reshape(n, 512), 512, 0), (u, 512, U_Z // 512)],
                                       [(wl["dn_norm_w"], (1, LANE), lambda j: (0, 0))], [d_ys[0]], 1, "dn_post_bwd",
                                       into=(du, U_Z // 512, [1]))
    du = ex["into"]
    gr["dn_norm_w"] = dnw[0, 0]
    d_prep = chunk_scan_bwd(dn_step, sv["prep_a"], sv["ck_a"], d_oa.reshape(bsz, seq, 512), "dn_scan_bwd")
    (d_q, d_k, d_v, d_gb, _), _, _ = rowmap_bwd(dn_prep, sv["dn_in"], [], [t.reshape(n, t.shape[2]) for t in d_prep], 1,
                                                "dn_prep_bwd", rows=crow, din_dtypes=[f32] * 4 + [None])
    d_q, d_k, d_v = (t.reshape(bsz, seq, 512) for t in (d_q, d_k, d_v))
    _, (dgp,), ex = rowmap_bwd(f_dn_gates, [(u, 512, U_AB // 512)], [(wl["dn_gate_p"], (8, LANE), lambda j: (0, 0))],
                               [d_gb], 1, "dn_gates_bwd", rows=512, into=(du, U_AB // 512, [0]))
    du = ex["into"]
    gr["dn_a_log"], gr["dn_dt_bias"] = dgp[0, 0, :DN_HEADS], dgp[0, 1, :DN_HEADS]
    d_cw = []
    for kind, d_t in enumerate((d_q, d_k, d_v)):
        cw = (wl["dn_conv_w"], (4, LANE), functools.partial(lambda j, kind: (0, 4 * kind + j), kind=kind))
        (du_new,), (dcw,) = seqmap_bwd(functools.partial(f_dn_pre, kind), [(u3, U_QKV // LANE + 4 * kind)], [cw], [d_t], 4,
                                       "dn_pre%d_bwd" % kind, din_dtype=bf16, into=(du3(), U_QKV // LANE + 4 * kind))
        du = du_new.reshape(n, U_PAD)
        d_cw.append(_cols(dcw))
    gr["dn_conv_w"] = jnp.concatenate(d_cw, axis=1)

    gr["w_in"] = _unpad_w_in(mm(sv["h"], du, "tn", "mm_in_dw"))
    token = emit("mix", {k: gr[k] for k in ("w_in", "w_branch", "w_out")})
    d_h = mm(du, wl["w_in"], "nt", "mm_in_dx", dep=token)
    (dx_in,), (dg,), ex = rowmap_bwd(f_norm, [(x_in, D_MODEL, 0)], [row1(wl["attn_norm"])], [d_h], 1, "norm_bwd", add=[dx_mid],
                                     copy16=0)
    gr["attn_norm"] = dg[0, 0]
    big = ("w_in", "w_branch", "w_out", "w_up", "w_down")
    return dx_in, ex["copy16"], emit("small", {k: g for k, g in gr.items() if k not in big})


def _cols(dp):
    ncol, p, _ = dp.shape
    return jnp.transpose(dp, (1, 0, 2)).reshape(p, ncol * LANE)


def _pad_w_in(w):
    segs = sorted(_IN_SEGS, key=lambda s: s[2])
    parts = [lax.slice_in_dim(w, src, src + width, axis=1) for src, width, _ in segs]
    end = segs[-1][2] + segs[-1][1]
    return jnp.concatenate(parts + [jnp.zeros((w.shape[0], U_PAD - end), w.dtype)], axis=1)


def _unpad_w_in(wp):
    return jnp.concatenate([lax.slice_in_dim(wp, dst, dst + width, axis=1) for _, width, dst in _IN_SEGS], axis=1)


def _rope_tables(positions):
    half = RET_DK // 2
    inv = ROPE_BASE ** (-jnp.arange(half, dtype=f32) / half)
    ang = positions.astype(f32).reshape(-1, 1) * inv
    cos, sin = jnp.cos(ang), jnp.sin(ang)
    return jnp.tile(cos, (1, 2 * RET_HEADS)), jnp.tile(sin, (1, 2 * RET_HEADS))


def _layer_weights(lw):
    wl = {}
    wl["w_in"] = _pad_w_in(lw["w_in"])
    for k in ("dn_conv_w", "lru_conv_w", "ffn_conv_w", "lru_wa", "lru_wx"):
        wl[k] = lw[k]
    for k in ("attn_norm", "ffn_norm", "dn_norm_w", "lru_conv_b", "lru_lambda", "ffn_conv_b", "lru_ba", "lru_bx"):
        wl[k] = lw[k].reshape(1, -1)
    gp = jnp.zeros((8, LANE), f32)
    wl["dn_gate_p"] = gp.at[0, :DN_HEADS].set(lw["dn_a_log"]).at[1, :DN_HEADS].set(lw["dn_dt_bias"])
    return wl


REST = ("w_branch", "w_out", "w_up", "w_down")


def forward_backward(x, positions, target, layer_weights, final_norm, on_head, on_grads):
    bsz, seq, d = x.shape
    n = bsz * seq
    cos, sin = _rope_tables(positions)
    xs = x.reshape(n, d)
    saved = []
    for layer in range(DEPTH):
        first, fetch_rest = layer_weights(layer, xs)
        xs, sv = _layer_fwd(xs, _layer_weights(first), fetch_rest, cos, sin, bsz, seq)
        saved.append(sv)
    loss, dx, d_final, dx16 = final_loss(xs, final_norm.reshape(1, d), target.reshape(n, d))
    on_head(loss[0, 0], d_final[0])
    token = None
    for layer in reversed(range(DEPTH)):
        dx, dx16, token = _layer_bwd(dx, dx16, saved[layer], cos, sin, bsz, seq, functools.partial(on_grads, layer), token)
    return dx.reshape(bsz, seq, d)


def local_step(x, positions, target, full):
    grads, head = {layer: {} for layer in range(DEPTH)}, {}

    def layer_weights(layer, _):
        return ({k: a[layer] for k, a in full.items() if k != "final_norm" and k not in REST},
                lambda after: {k: full[k][layer] for k in REST})

    gx = forward_backward(x, positions, target, layer_weights, full["final_norm"],
                          lambda loss, d_final: head.update(loss=loss, d_final=d_final),
                          lambda layer, group, gr: grads[layer].update(gr))
    stacked = {k: jnp.stack([grads[layer][k] for layer in range(DEPTH)]) for k in grads[0]}
    stacked["final_norm"] = head["d_final"]
    return head["loss"], gx, stacked


BIG = (("w_in", 2), ("w_branch", 3), ("w_out", 1), ("w_up", 2), ("w_down", 1))
SMALL_SHARDED = (("dn_conv_w", 2), ("lru_conv_w", 2), ("ffn_conv_w", 2))
REPLICATED = ("attn_norm", "dn_a_log", "dn_dt_bias", "dn_norm_w", "lru_conv_b", "lru_wa", "lru_ba", "lru_wx", "lru_bx",
              "lru_lambda", "ffn_norm", "ffn_conv_b", "final_norm")
WEIGHTS = ("attn_norm", "w_in", "dn_conv_w", "dn_a_log", "dn_dt_bias", "dn_norm_w", "lru_conv_w", "lru_conv_b", "lru_wa",
           "lru_ba", "lru_wx", "lru_bx", "lru_lambda", "w_branch", "w_out", "ffn_norm", "w_up", "ffn_conv_w", "ffn_conv_b",
           "w_down", "final_norm")


def _pack(arrs, dtype, align=16 * LANE):
    flat = jnp.concatenate([a.reshape(-1).astype(dtype) for a in arrs])
    pad = (-flat.shape[0]) % align
    return jnp.pad(flat, (0, pad)).reshape(-1, LANE)


def _unpack(rows, shapes):
    flat = rows.reshape(-1)
    out, pos = [], 0
    for shp in shapes:
        size = math.prod(shp)
        out.append(lax.slice_in_dim(flat, pos, pos + size).reshape(shp))
        pos += size
    return out


def _split8(a, axis):
    size = a.shape[axis] // N_DEV
    return [lax.slice_in_dim(a, p * size, (p + 1) * size, axis=axis) for p in range(N_DEV)]


def kernel(x, positions, attn_norm, w_in, dn_conv_w, dn_a_log, dn_dt_bias, dn_norm_w, lru_conv_w, lru_conv_b, lru_wa, lru_ba, lru_wx, lru_bx, lru_lambda, w_branch, w_out, ffn_norm, w_up, ffn_conv_w, ffn_conv_b, w_down, final_norm, loss_target, m_attn_norm, m_w_in, m_dn_conv_w, m_dn_a_log, m_dn_dt_bias, m_dn_norm_w, m_lru_conv_w, m_lru_conv_b, m_lru_wa, m_lru_ba, m_lru_wx, m_lru_bx, m_lru_lambda, m_w_branch, m_w_out, m_ffn_norm, m_w_up, m_ffn_conv_w, m_ffn_conv_b, m_w_down, m_final_norm, v_attn_norm, v_w_in, v_dn_conv_w, v_dn_a_log, v_dn_dt_bias, v_dn_norm_w, v_lru_conv_w, v_lru_conv_b, v_lru_wa, v_lru_ba, v_lru_wx, v_lru_bx, v_lru_lambda, v_w_branch, v_w_out, v_ffn_norm, v_w_up, v_ffn_conv_w, v_ffn_conv_b, v_w_down, v_final_norm):
    w = dict(attn_norm=attn_norm, w_in=w_in, dn_conv_w=dn_conv_w, dn_a_log=dn_a_log, dn_dt_bias=dn_dt_bias, dn_norm_w=dn_norm_w,
             lru_conv_w=lru_conv_w, lru_conv_b=lru_conv_b, lru_wa=lru_wa, lru_ba=lru_ba, lru_wx=lru_wx, lru_bx=lru_bx,
             lru_lambda=lru_lambda, w_branch=w_branch, w_out=w_out, ffn_norm=ffn_norm, w_up=w_up, ffn_conv_w=ffn_conv_w,
             ffn_conv_b=ffn_conv_b, w_down=w_down, final_norm=final_norm)
    m = dict(attn_norm=m_attn_norm, w_in=m_w_in, dn_conv_w=m_dn_conv_w, dn_a_log=m_dn_a_log, dn_dt_bias=m_dn_dt_bias,
             dn_norm_w=m_dn_norm_w, lru_conv_w=m_lru_conv_w, lru_conv_b=m_lru_conv_b, lru_wa=m_lru_wa, lru_ba=m_lru_ba,
             lru_wx=m_lru_wx, lru_bx=m_lru_bx, lru_lambda=m_lru_lambda, w_branch=m_w_branch, w_out=m_w_out, ffn_norm=m_ffn_norm,
             w_up=m_w_up, ffn_conv_w=m_ffn_conv_w, ffn_conv_b=m_ffn_conv_b, w_down=m_w_down, final_norm=m_final_norm)
    v = dict(attn_norm=v_attn_norm, w_in=v_w_in, dn_conv_w=v_dn_conv_w, dn_a_log=v_dn_a_log, dn_dt_bias=v_dn_dt_bias,
             dn_norm_w=v_dn_norm_w, lru_conv_w=v_lru_conv_w, lru_conv_b=v_lru_conv_b, lru_wa=v_lru_wa, lru_ba=v_lru_ba,
             lru_wx=v_lru_wx, lru_bx=v_lru_bx, lru_lambda=v_lru_lambda, w_branch=v_w_branch, w_out=v_w_out, ffn_norm=v_ffn_norm,
             w_up=v_w_up, ffn_conv_w=v_ffn_conv_w, ffn_conv_b=v_ffn_conv_b, w_down=v_w_down, final_norm=v_final_norm)

    me = 4 * lax.axis_index("x") + 2 * lax.axis_index("y") + lax.axis_index("c")
    axes = dict(BIG + SMALL_SHARDED)
    conv_names = [k for k, _ in SMALL_SHARDED]

    gathers, tokens = {}, []
    for layer in range(DEPTH):
        first = [w["w_in"][layer].astype(bf16), _pack([w[k][layer] for k in conv_names], f32)]
        rest = [w[k][layer].astype(bf16) for k in REST]
        for part, srcs in (("in", first), ("rest", rest)):
            gathers[layer, part], token = exchange_start(srcs, [False] * len(srcs), "gather_%s_start%d" % (part, layer))
            tokens.append(token[0:1, 0:1])
    all_started = functools.reduce(lambda a, b: a + b, tokens)

    def joined(own, land, axis):
        return jnp.concatenate([jnp.where(me == p, own, land[p]) for p in range(N_DEV)], axis=axis)

    def layer_weights(layer, x_in):
        own, lands = exchange_wait(gathers[layer, "in"], x_in, "gather_in_wait%d" % layer)
        lw = {"w_in": joined(own[0], lands[0], 1)}
        shapes = [w[k].shape[1:] for k in conv_names]
        per_dev = [_unpack(jnp.where(me == p, own[1], lands[1][p]), shapes) for p in range(N_DEV)]
        lw.update({k: jnp.concatenate([per_dev[p][i] for p in range(N_DEV)], axis=axes[k] - 1) for i, k in enumerate(conv_names)})
        lw.update({k: w[k][layer] for k in REPLICATED if k != "final_norm"})
        if layer == 0:
            lw["attn_norm"] = lw["attn_norm"] + all_started[0]

        def fetch_rest(after):
            own_r, lands_r = exchange_wait(gathers[layer, "rest"], after, "gather_rest_wait%d" % layer)
            return {k: joined(own_r[i], lands_r[i], axes[k] - 1) for i, k in enumerate(REST)}

        return lw, fetch_rest

    small_names = conv_names + [k for k in REPLICATED if k != "final_norm"]
    groups = {"ffn": ("w_up", "w_down"), "mix": ("w_in", "w_branch", "w_out")}
    scatters, small_shapes, head = {}, {}, {}

    def on_grads(layer, group, gr):
        if group == "small":
            small = [gr[k] for k in small_names] + ([head["loss"].reshape(1), head["d_final"]] if layer == DEPTH - 1 else [])
            small_shapes[layer] = [a.shape for a in small]
            srcs, modes = [_pack(small, f32)], [False]
        else:
            srcs = [jnp.stack(_split8(gr[k], axes[k] - 1)).astype(bf16) for k in groups[group]]
            modes = [True] * len(srcs)
        scatters[layer, group], token = exchange_start(srcs, modes, "scatter_%s_start%d" % (group, layer))
        return token

    grad_x = forward_backward(x, positions, loss_target, layer_weights, final_norm,
                              lambda loss_part, d_final: head.update(loss=loss_part, d_final=d_final), on_grads)

    big_sums, small_sums = {}, {}
    for group in ("ffn", "mix"):
        for layer in reversed(range(DEPTH)):
            own, lands = exchange_wait(scatters[layer, group], grad_x, "scatter_%s_wait%d" % (group, layer))
            for i, k in enumerate(groups[group]):
                shard = w[k].shape[1:]
                mine = lax.dynamic_index_in_dim(own[i], me, axis=0, keepdims=False).reshape(-1, shard[-1])
                big_sums[layer, k] = sum_slots(lands[i].reshape(N_DEV, -1, shard[-1]), "sum_" + k, own=mine).reshape(shard)
    for layer in reversed(range(DEPTH)):
        own, lands = exchange_wait(scatters[layer, "small"], grad_x, "scatter_small_wait%d" % layer)
        small_sums[layer] = _unpack(sum_slots(lands[0], "sum_small", own=own[0]), small_shapes[layer])
    grads = {k: jnp.stack([big_sums[layer, k] for layer in range(DEPTH)]) for k, _ in BIG}
    loss, grads["final_norm"] = small_sums[DEPTH - 1][len(small_names)][0], small_sums[DEPTH - 1][len(small_names) + 1]
    for i, k in enumerate(small_names):
        g = jnp.stack([small_sums[layer][i] for layer in range(DEPTH)])
        ax = dict(SMALL_SHARDED).get(k)
        if ax is None:
            grads[k] = g
        else:
            size = g.shape[ax] // N_DEV
            grads[k] = lax.dynamic_slice_in_dim(g, me * size, size, axis=ax)

    upd = {k: adamw(w[k], grads[k], m[k], v[k], "adamw_" + k) for k in WEIGHTS}
    return (loss, grad_x, *[grads[k] for k in WEIGHTS], *[upd[k][0] for k in WEIGHTS], *[upd[k][1] for k in WEIGHTS],
            *[upd[k][2] for k in WEIGHTS])
```

```python
import functools
import math

import jax
import jax.numpy as jnp
from jax import lax
from jax.experimental import pallas as pl
from jax.experimental.pallas import tpu as pltpu

f32 = jnp.float32
bf16 = jnp.bfloat16

D_MODEL = 1024
DEPTH = 4
CHUNK = 64
EPS = 1e-6
DN_HEADS, DN_DK = 4, 128
RET_HEADS, RET_DK, RET_DV = 4, 64, 128
ROPE_BASE = 10000.0
LRU_C = 8.0
D_FF = 2816
N_DEV = 8
LANE = 128
VMEM_LIMIT = 56 * 1024 * 1024

ADAM_LR, ADAM_B1, ADAM_B2, ADAM_EPS, ADAM_WD, ADAM_STEP = 0.001, 0.9, 0.999, 1e-8, 0.01, 10

U_GATES, U_QKV, U_RV, U_RG, U_Z, U_CX, U_CG, U_RQ, U_RK, U_AB = (
    0, 3072, 4608, 5120, 5632, 6144, 6656, 7168, 7424, 7680)
U_PAD = 8192
_IN_SEGS = ((0, 1536, U_QKV), (1536, 8, U_AB), (1544, 512, U_Z), (2056, 256, U_RQ), (2312, 256, U_RK),
            (2568, 512, U_RV), (3080, 512, U_RG), (3592, 512, U_CX), (4104, 512, U_CG), (4616, 3072, U_GATES))
N_IN = 7688


def _params():
    return pltpu.CompilerParams(vmem_limit_bytes=VMEM_LIMIT)


def _pick(dim, pref):
    best = None
    for d in range(LANE, min(dim, pref) + 1, LANE):
        if dim % d == 0:
            best = d
    return best if best is not None else dim


@functools.partial(jax.custom_vjp, nondiff_argnums=(1, 2))
def sroll(x, shift, axis):
    return pltpu.roll(x, shift, axis)


def _sroll_fwd(x, shift, axis):
    return pltpu.roll(x, shift, axis), None


def _sroll_bwd(shift, axis, _, g):
    n = g.shape[axis]
    return (pltpu.roll(g, (n - shift) % n, axis),)


sroll.defvjp(_sroll_fwd, _sroll_bwd)

_DIMS = {"nn": (((1,), (0,)), ((), ())), "nt": (((1,), (1,)), ((), ())), "tn": (((0,), (0,)), ((), ()))}


def _dg(a, b, dims):
    return lax.dot_general(a.astype(bf16), b.astype(bf16), _DIMS[dims], preferred_element_type=f32)


@functools.partial(jax.custom_vjp, nondiff_argnums=(2,))
def bdot(a, b, dims):
    return _dg(a, b, dims)


def _bdot_fwd(a, b, dims):
    return _dg(a, b, dims), (a.astype(bf16), b.astype(bf16))


def _bdot_bwd(dims, res, g):
    a, b = res
    if dims == "nn":
        return _dg(g, b, "nt"), _dg(a, g, "tn")
    if dims == "nt":
        return _dg(g, b, "nn"), _dg(g, a, "tn")
    return _dg(b, g, "nt"), _dg(a, g, "nn")


bdot.defvjp(_bdot_fwd, _bdot_bwd)


def _fdot(a, b, dims):
    return lax.dot_general(a, b, _DIMS[dims], precision=lax.Precision.HIGH, preferred_element_type=f32)


@jax.custom_vjp
def unit_lower_inv_all(mats):
    shape = mats[0].shape
    row = lax.broadcasted_iota(jnp.int32, shape, 0)
    col = lax.broadcasted_iota(jnp.int32, shape, 1)
    eye = jnp.where(row == col, 1.0, 0.0).astype(f32)
    n = [-a for a in mats]
    p = [eye + x for x in n]
    span = 2
    while span < shape[0]:
        n = [_fdot(x, x, "nn") for x in n]
        p = [y + _fdot(y, x, "nn") for y, x in zip(p, n)]
        span *= 2
    return p


def _uli_fwd(mats):
    x = unit_lower_inv_all(mats)
    return x, x


def _uli_bwd(xs, gs):
    t = [_fdot(x, g, "tn") for x, g in zip(xs, gs)]
    return ([-_fdot(y, x, "nt") for y, x in zip(t, xs)],)


unit_lower_inv_all.defvjp(_uli_fwd, _uli_bwd)


@jax.custom_vjp
def known_inverse(invs, mats):
    return invs


def _known_fwd(invs, mats):
    return invs, invs


def _known_bwd(xs, gs):
    return [jnp.zeros_like(x) for x in xs], _uli_bwd(xs, gs)[0]


known_inverse.defvjp(_known_fwd, _known_bwd)


def cumsum_rows(x):
    rows = x.shape[0]
    row = lax.broadcasted_iota(jnp.int32, x.shape, 0)
    s = 1
    while s < rows:
        x = x + jnp.where(row >= s, sroll(x, s, 0), 0.0)
        s *= 2
    return x


def _expm1(x):
    return jnp.tanh(0.5 * x) * (jnp.exp(x) + 1.0)


def _lane_pick(x, lane):
    idx = lax.broadcasted_iota(jnp.int32, x.shape, 1)
    return jnp.sum(jnp.where(idx == lane, x, 0.0), axis=1, keepdims=True)


def _row_pick(x, r):
    idx = lax.broadcasted_iota(jnp.int32, x.shape, 0)
    return jnp.sum(jnp.where(idx == r, x, 0.0), axis=0, keepdims=True)


def _causal_conv(x, halo, w, width):
    xe = jnp.concatenate([halo, x], axis=0)
    acc = xe * w[width - 1:width]
    for k in range(width - 1):
        acc = acc + sroll(xe, width - 1 - k, 0) * w[k:k + 1]
    return acc[8:]


def f_norm(ins, ps):
    (x,), (g,) = ins, ps
    return [x * lax.rsqrt(jnp.mean(x * x, axis=-1, keepdims=True) + EPS) * g]


def f_dn_pre(kind, mains, halos, ps):
    y = _causal_conv(mains[0], halos[0], ps[0], 4)
    y = y * jax.nn.sigmoid(y)
    if kind < 2:
        y = y * lax.rsqrt(jnp.sum(y * y, axis=-1, keepdims=True) + EPS)
    if kind == 0:
        y = y * (DN_DK ** -0.5)
    return [y]


def f_dn_gates(ins, ps):
    u, p = ins[0][:, :LANE], ps[0]
    lane = lax.broadcasted_iota(jnp.int32, u.shape, 1)
    g = -jnp.exp(p[0:1]) * jax.nn.softplus(u + p[1:2])
    beta = jax.nn.sigmoid(u)
    return [jnp.where(lane < 4, g, jnp.where(lane < 8, beta, 0.0))]


def per_head(fn):
    def tile_fn(vals, ps):
        heads = [fn([v[:, h * LANE:(h + 1) * LANE] for v in vals], ps) for h in range(vals[0].shape[1] // LANE)]
        return [jnp.concatenate([o[i] for o in heads], axis=1) for i in range(len(heads[0]))]
    return tile_fn


def f_dn_post(ins, ps):
    (o, z), (nw,) = ins, ps
    y = o * lax.rsqrt(jnp.mean(o * o, axis=-1, keepdims=True) + EPS) * nw
    return [y * (z * jax.nn.sigmoid(z))]


def _rot_half(t):
    lane = lax.broadcasted_iota(jnp.int32, t.shape, 1)
    width = t.shape[1]
    first = (lane % RET_DK) < (RET_DK // 2)
    return jnp.where(first, -sroll(t, width - RET_DK // 2, 1), sroll(t, RET_DK // 2, 1))


def f_ret_pre(ins, ps):
    q, k, cos, sin = ins
    qr = q * cos + _rot_half(q) * sin
    kr = (k * cos + _rot_half(k) * sin) * (RET_DK ** -0.5)
    return [qr, kr]


def f_ret_post(ins, ps):
    o, g = ins
    mu = jnp.mean(o, axis=-1, keepdims=True)
    var = jnp.mean(jnp.square(o - mu), axis=-1, keepdims=True)
    return [(o - mu) * lax.rsqrt(var + EPS) * (g * jax.nn.sigmoid(g))]


def f_lru_pre(mains, halos, ps):
    cw, cb, wa, ba, wx, bx, lam = ps
    xc = _causal_conv(mains[0], halos[0], cw, 4) + cb
    r = jax.nn.sigmoid(bdot(xc, wa, "nn") + ba)
    i = jax.nn.sigmoid(bdot(xc, wx, "nn") + bx)
    log_a = -LRU_C * r * jax.nn.softplus(-lam)
    a = jnp.exp(log_a)
    b = jnp.sqrt(-_expm1(2.0 * log_a)) * (i * xc)
    return [a, b]


def f_lru_post(ins, ps):
    h, g = ins
    return [h * jax.nn.gelu(g)]


def f_merge(ins, ps):
    g0, g1, g2, b0, b1, b2 = ins
    return [jax.nn.sigmoid(g0) * b0 + jax.nn.sigmoid(g1) * b1 + jax.nn.sigmoid(g2) * b2]


def f_ffn_mid(mains, halos, ps):
    cwg, cwv, cbg, cbv = ps
    gate = _causal_conv(mains[0], halos[0], cwg, 3) + cbg
    val = _causal_conv(mains[1], halos[1], cwv, 3) + cbv
    return [gate * jax.nn.sigmoid(gate) * val]


def mm(a, b, dims, name, add=None, dep=None, tm=1536, tn=1536, tk=2816):
    if dims == "tn":
        kdim, m = a.shape
        n = b.shape[1]
    else:
        m, kdim = a.shape
        n = b.shape[0] if dims == "nt" else b.shape[1]
    tm, tn, tk = _pick(m, tm), _pick(n, tn), _pick(kdim, tk)
    nk = kdim // tk
    a_spec = pl.BlockSpec((tk, tm), lambda i, j, k: (k, i)) if dims == "tn" else pl.BlockSpec((tm, tk), lambda i, j, k: (i, k))
    b_spec = pl.BlockSpec((tn, tk), lambda i, j, k: (j, k)) if dims == "nt" else pl.BlockSpec((tk, tn), lambda i, j, k: (k, j))
    o_spec = pl.BlockSpec((tm, tn), lambda i, j, k: (i, j))
    has_add, has_dep = add is not None, dep is not None

    def body(*refs):
        a_ref, b_ref = refs[:2]
        add_ref = refs[2] if has_add else None
        o_ref = refs[2 + has_add + has_dep]
        if nk == 1:
            prod = _dg(a_ref[...], b_ref[...], dims)
            o_ref[...] = prod + add_ref[...] if has_add else prod
            return
        acc_ref = refs[-1]
        k = pl.program_id(2)

        @pl.when(k == 0)
        def _():
            acc_ref[...] = jnp.zeros_like(acc_ref)

        acc_ref[...] += _dg(a_ref[...], b_ref[...], dims)

        @pl.when(k == nk - 1)
        def _():
            o_ref[...] = acc_ref[...] + add_ref[...] if has_add else acc_ref[...]

    args = [a, b] + ([add] if has_add else []) + ([dep] if has_dep else [])
    in_specs = [a_spec, b_spec] + ([o_spec] if has_add else [])
    in_specs += [pl.BlockSpec((8, LANE), lambda i, j, k: (0, 0))] if has_dep else []
    return pl.pallas_call(
        body, name=name, grid=(m // tm, n // tn, nk), in_specs=in_specs, out_specs=o_spec,
        out_shape=jax.ShapeDtypeStruct((m, n), f32), scratch_shapes=[pltpu.VMEM((tm, tn), f32)] if nk > 1 else [],
        compiler_params=_params())(*args)


def rowmap(fn, ins, params, outs, ncol, name, rows=256):
    n = ins[0][0].shape[0]
    r = min(rows, n)
    nin, npar = len(ins), len(params)

    def body(*refs):
        vals = [x[...] for x in refs[:nin]]
        pv = [p[...] for p in refs[nin:nin + npar]]
        for o_ref, o in zip(refs[nin + npar:], fn(vals, pv)):
            o_ref[...] = o.astype(o_ref.dtype)

    in_specs = [pl.BlockSpec((r, cb), functools.partial(lambda j, i, off: (i, off + j), off=off)) for _, cb, off in ins]
    in_specs += [pl.BlockSpec(bs, functools.partial(lambda j, i, f: f(j), f=f)) for _, bs, f in params]
    out_specs = [pl.BlockSpec((r, cb), lambda j, i: (i, j)) for cb, _ in outs]
    out_shape = [jax.ShapeDtypeStruct((n, cb * ncol), dt) for cb, dt in outs]
    res = pl.pallas_call(body, name=name, grid=(ncol, n // r), in_specs=in_specs, out_specs=out_specs,
                         out_shape=out_shape, compiler_params=_params())(*[a for a, _, _ in ins], *[a for a, _, _ in params])
    return res


def rowmap_bwd(fn, ins, params, douts, ncol, name, rows=256, add=None, din_dtypes=None, into=None, copy16=None):
    n = ins[0][0].shape[0]
    r = min(rows, n)
    nin, npar, nout = len(ins), len(params), len(douts)
    add = [None] * nin if add is None else list(add)
    add_idx = [i for i in range(nin) if add[i] is not None]
    din_dtypes = [f32] * nin if din_dtypes is None else list(din_dtypes)
    into_buf, into_off, into_idx = into if into is not None else (None, 0, [])
    has_into, has_copy = into is not None, copy16 is not None
    kept = [i for i in range(nin) if din_dtypes[i] is not None and i not in into_idx]

    def body(*refs):
        vals = [x[...] for x in refs[:nin]]
        pv = [p[...] for p in refs[nin:nin + npar]]
        dys = [d[...] for d in refs[nin + npar:nin + npar + nout]]
        k0 = nin + npar + nout
        add_refs = dict(zip(add_idx, refs[k0:k0 + len(add_idx)]))
        k0 += len(add_idx) + has_into
        din_refs = refs[k0:k0 + len(kept)]
        k0 += len(kept)
        copy_ref = refs[k0] if has_copy else None
        into_ref = refs[k0 + has_copy] if has_into else None
        dp_refs = refs[k0 + has_copy + has_into:]
        _, vjp = jax.vjp(fn, vals, pv)
        dvals, dpv = vjp(dys)
        cot = lambda idx: dvals[idx] + add_refs[idx][...] if idx in add_refs else dvals[idx]
        for d_ref, idx in zip(din_refs, kept):
            d_ref[...] = cot(idx).astype(d_ref.dtype)
        if has_copy:
            copy_ref[...] = cot(copy16).astype(copy_ref.dtype)
        if has_into:
            parts = [cot(idx) for idx in into_idx]
            into_ref[...] = (parts[0] if len(parts) == 1 else jnp.concatenate(parts, axis=1)).astype(into_ref.dtype)

        @pl.when(pl.program_id(1) == 0)
        def _():
            for d_ref in dp_refs:
                d_ref[...] = jnp.zeros_like(d_ref)

        for d_ref, d in zip(dp_refs, dpv):
            d_ref[...] += d

    in_specs = [pl.BlockSpec((r, cb), functools.partial(lambda j, i, off: (i, off + j), off=off)) for _, cb, off in ins]
    in_specs += [pl.BlockSpec(bs, functools.partial(lambda j, i, f: f(j), f=f)) for _, bs, f in params]
    in_specs += [pl.BlockSpec((r, d.shape[1] // ncol), lambda j, i: (i, j)) for d in douts]
    in_specs += [pl.BlockSpec((r, ins[i][1]), lambda j, i: (i, j)) for i in add_idx]
    out_specs = [pl.BlockSpec((r, ins[i][1]), lambda j, i: (i, j)) for i in kept]
    out_shape = [jax.ShapeDtypeStruct((n, ins[i][1] * ncol), din_dtypes[i]) for i in kept]
    args = [a for a, _, _ in ins] + [a for a, _, _ in params] + list(douts) + [add[i] for i in add_idx]
    aliases = {}
    if has_copy:
        out_specs += [pl.BlockSpec((r, ins[copy16][1]), lambda j, i: (i, j))]
        out_shape += [jax.ShapeDtypeStruct((n, ins[copy16][1] * ncol), bf16)]
    if has_into:
        assert ncol == 1
        in_specs += [pl.BlockSpec(memory_space=pl.ANY)]
        aliases[len(args)] = len(out_shape)
        args += [into_buf]
        out_specs += [pl.BlockSpec((r, sum(ins[i][1] for i in into_idx)), lambda j, i: (i, into_off))]
        out_shape += [jax.ShapeDtypeStruct(into_buf.shape, into_buf.dtype)]
    pshapes = [tuple(d for d in bs if d is not None) for _, bs, _ in params]
    out_specs += [pl.BlockSpec((None,) + ps, functools.partial(lambda j, i, nd: (j,) + (0,) * nd, nd=len(ps))) for ps in pshapes]
    out_shape += [jax.ShapeDtypeStruct((ncol,) + ps, f32) for ps in pshapes]
    res = pl.pallas_call(body, name=name, grid=(ncol, n // r), in_specs=in_specs, out_specs=out_specs, out_shape=out_shape,
                         input_output_aliases=aliases, compiler_params=_params())(*args)
    dins = [None] * nin
    for pos, i in enumerate(kept):
        dins[i] = res[pos]
    pos = len(kept)
    extras = {}
    if has_copy:
        extras["copy16"] = res[pos]
        pos += 1
    if has_into:
        extras["into"] = res[pos]
        pos += 1
    return dins, res[pos:], extras


SEQ_ROWS = 2048


def seqmap(fn, ins, params, nouts, ncol, name, out_dtype=f32):
    bsz, seq, _ = ins[0][0].shape
    r = min(SEQ_ROWS, seq)
    nin, npar = len(ins), len(params)

    def body(*refs):
        in_refs = refs[:nin]
        pv = [p[...] for p in refs[nin:nin + npar]]
        out_refs = refs[nin + npar:]

        def step(i, carry):
            r0 = pl.multiple_of(i * r, r)
            h0 = pl.multiple_of(jnp.maximum(r0 - 8, 0), 8)
            mains = [x[pl.ds(r0, r), :] for x in in_refs]
            halos = [jnp.where(i == 0, 0.0, x[pl.ds(h0, 8), :]) for x in in_refs]
            for o_ref, o in zip(out_refs, fn(mains, halos, pv)):
                o_ref[pl.ds(r0, r), :] = o.astype(o_ref.dtype)
            return carry

        lax.fori_loop(0, seq // r, step, 0)

    in_specs = [pl.BlockSpec((None, seq, LANE), functools.partial(lambda j, b, off: (b, 0, off + j), off=off)) for _, off in ins]
    in_specs += [pl.BlockSpec(bs, functools.partial(lambda j, b, f: f(j), f=f)) for _, bs, f in params]
    out_specs = [pl.BlockSpec((None, seq, LANE), lambda j, b: (b, 0, j)) for _ in range(nouts)]
    out_shape = [jax.ShapeDtypeStruct((bsz, seq, LANE * ncol), out_dtype) for _ in range(nouts)]
    return pl.pallas_call(body, name=name, grid=(ncol, bsz), in_specs=in_specs, out_specs=out_specs,
                          out_shape=out_shape, compiler_params=_params())(*[a for a, _ in ins], *[a for a, _, _ in params])


def seqmap_bwd(fn, ins, params, douts, ncol, name, din_dtype=f32, into=None):
    bsz, seq, _ = ins[0][0].shape
    r = min(SEQ_ROWS, seq)
    nin, npar, nout = len(ins), len(params), len(douts)
    narrow = din_dtype != f32

    def body(*refs):
        in_refs = refs[:nin]
        pv = [p[...] for p in refs[nin:nin + npar]]
        dy_refs = refs[nin + npar:nin + npar + nout]
        k0 = nin + npar + nout + (into is not None)
        dout_refs = refs[k0:k0 + nin]
        dp_refs = refs[k0 + nin:k0 + nin + npar]
        din_refs = refs[k0 + nin + npar:] if narrow else dout_refs

        def step(i, dp_acc):
            r0 = pl.multiple_of(i * r, r)
            h0 = pl.multiple_of(jnp.maximum(r0 - 8, 0), 8)
            mains = [x[pl.ds(r0, r), :] for x in in_refs]
            halos_raw = [x[pl.ds(h0, 8), :] for x in in_refs]

            def tile(mains, halos_raw, pv):
                return fn(mains, [jnp.where(i == 0, 0.0, h) for h in halos_raw], pv)

            _, vjp = jax.vjp(tile, mains, halos_raw, pv)
            dm, dh, dp = vjp([d[pl.ds(r0, r), :] for d in dy_refs])
            for d_ref, m, h in zip(din_refs, dm, dh):
                d_ref[pl.ds(r0, r), :] = m
                d_ref[pl.ds(h0, 8), :] += h
            return [acc + d for acc, d in zip(dp_acc, dp)]

        dp = lax.fori_loop(0, seq // r, step, [jnp.zeros(p.shape, f32) for p in pv])
        if narrow:
            for o_ref, d_ref in zip(dout_refs, din_refs):
                o_ref[...] = d_ref[...].astype(o_ref.dtype)

        @pl.when(pl.program_id(1) == 0)
        def _():
            for d_ref in dp_refs:
                d_ref[...] = jnp.zeros_like(d_ref)

        for d_ref, d in zip(dp_refs, dp):
            d_ref[...] += d

    in_specs = [pl.BlockSpec((None, seq, LANE), functools.partial(lambda j, b, off: (b, 0, off + j), off=off)) for _, off in ins]
    in_specs += [pl.BlockSpec(bs, functools.partial(lambda j, b, f: f(j), f=f)) for _, bs, f in params]
    in_specs += [pl.BlockSpec((None, seq, LANE), lambda j, b: (b, 0, j)) for _ in range(nout)]
    out_specs = [pl.BlockSpec((None, seq, LANE), lambda j, b: (b, 0, j)) for _ in range(nin)]
    pshapes = [tuple(d for d in bs if d is not None) for _, bs, _ in params]
    out_specs += [pl.BlockSpec((None,) + ps, functools.partial(lambda j, b, nd: (j,) + (0,) * nd, nd=len(ps))) for ps in pshapes]
    out_shape = [jax.ShapeDtypeStruct((bsz, seq, LANE * ncol), din_dtype) for _ in range(nin)]
    out_shape += [jax.ShapeDtypeStruct((ncol,) + ps, f32) for ps in pshapes]
    args = [a for a, _ in ins] + [a for a, _, _ in params] + list(douts)
    aliases = {}
    if into is not None:
        assert nin == 1 and into[0].dtype == din_dtype
        in_specs += [pl.BlockSpec(memory_space=pl.ANY)]
        aliases[len(args)] = 0
        args += [into[0]]
        out_specs[0] = pl.BlockSpec((None, seq, LANE), lambda j, b: (b, 0, into[1] + j))
        out_shape[0] = jax.ShapeDtypeStruct(into[0].shape, din_dtype)
    res = pl.pallas_call(body, name=name, grid=(ncol, bsz), in_specs=in_specs, out_specs=out_specs, out_shape=out_shape,
                         scratch_shapes=[pltpu.VMEM((seq, LANE), f32) for _ in range(nin)] if narrow else [],
                         input_output_aliases=aliases, compiler_params=_params())(*args)
    return res[:nin], res[nin:]


def _tri_masks():
    row = lax.broadcasted_iota(jnp.int32, (CHUNK, CHUNK), 0)
    col = lax.broadcasted_iota(jnp.int32, (CHUNK, CHUNK), 1)
    return row >= col, row > col


CHUNKS_PER_STEP = 4


def _by_rows(parts, per_row):
    rows = [jnp.concatenate(parts[i:i + per_row], axis=1) for i in range(0, len(parts), per_row)]
    return jnp.concatenate(rows, axis=0)


def dn_prep(vals, ps):
    q, k, v, gb = vals[:4]
    nchunk = q.shape[0] // CHUNK
    causal, strict = _tri_masks()
    gbs = [gb[c * CHUNK:(c + 1) * CHUNK] for c in range(nchunk)]
    gcs = [cumsum_rows(g) for g in gbs]
    gcts = [g.T for g in gcs]
    chains = [(c, h) for c in range(nchunk) for h in range(DN_HEADS)]
    part = lambda t, c, h: t[c * CHUNK:(c + 1) * CHUNK, h * DN_DK:(h + 1) * DN_DK]
    qh = [part(q, c, h) for c, h in chains]
    kh = [part(k, c, h) for c, h in chains]
    vh = [part(v, c, h) for c, h in chains]
    g_col = [_lane_pick(gcs[c], h) for c, h in chains]
    beta = [_lane_pick(gbs[c], DN_HEADS + h) for c, h in chains]
    g_row = [_row_pick(gcts[c], h)[:, :CHUNK] for c, h in chains]
    decay = [jnp.where(causal, jnp.exp(jnp.where(causal, gc - gr, 0.0)), 0.0) for gc, gr in zip(g_col, g_row)]
    k_beta = [a * b for a, b in zip(kh, beta)]
    eg = [jnp.exp(g) for g in g_col]
    kk = [bdot(a, b, "nt") for a, b in zip(k_beta, kh)]
    qk = [bdot(a, b, "nt") for a, b in zip(qh, kh)]
    lower = [jnp.where(strict, a * d, 0.0) for a, d in zip(kk, decay)]
    if len(vals) == 5:
        t_inv = known_inverse([part(vals[4], c, h)[:, :CHUNK] for c, h in chains], lower)
    else:
        t_inv = unit_lower_inv_all(lower)
    u = [bdot(t, a * b, "nn") for t, a, b in zip(t_inv, vh, beta)]
    w = [bdot(t, a * e, "nn") for t, a, e in zip(t_inv, k_beta, eg)]
    attn = [jnp.concatenate([a * d, jnp.zeros((CHUNK, DN_DK - CHUNK), f32)], axis=1) for a, d in zip(qk, decay)]
    qd = [a * e for a, e in zip(qh, eg)]
    kd = [a * jnp.exp(_row_pick(g, CHUNK - 1) - g) for a, g in zip(kh, g_col)]
    g_last = jnp.concatenate([jnp.broadcast_to(_row_pick(g, CHUNK - 1), g.shape) for g in gcs], axis=0)
    outs = [_by_rows(t, DN_HEADS) for t in (u, w, attn, qd, kd)] + [g_last]
    if len(vals) == 4:
        wide = [jnp.concatenate([t, jnp.zeros((CHUNK, DN_DK - CHUNK), f32)], axis=1) for t in t_inv]
        outs.append(_by_rows(wide, DN_HEADS))
    return outs


def dn_step(state, u, w, attn, qd, kd, g_last):
    bsz = u.shape[0]
    chains = [(b, h) for b in range(bsz) for h in range(DN_HEADS)]
    part = lambda t, b, h: t[b, :, h * DN_DK:(h + 1) * DN_DK]
    ws = [bdot(part(w, b, h), s, "nn") for (b, h), s in zip(chains, state)]
    qs = [bdot(part(qd, b, h), s, "nn") for (b, h), s in zip(chains, state)]
    v_new = [part(u, b, h) - x for (b, h), x in zip(chains, ws)]
    av = [bdot(attn[b, :, h * DN_DK:h * DN_DK + CHUNK], x, "nn") for (b, h), x in zip(chains, v_new)]
    kv = [bdot(part(kd, b, h), x, "tn") for (b, h), x in zip(chains, v_new)]
    ge = [jnp.exp(_row_pick(_lane_pick(g_last[b], h), 0)) for b, h in chains]
    new_state = [s * g + x for s, g, x in zip(state, ge, kv)]
    outs = [a + b for a, b in zip(qs, av)]
    return new_state, jnp.concatenate([jnp.concatenate(outs[b * DN_HEADS:(b + 1) * DN_HEADS], axis=1)[None]
                                       for b in range(bsz)], axis=0)


def _ret_log_gamma(h):
    return math.log(1.0 - 2.0 ** (-5.0 - h))


def ret_prep(vals, ps):
    q, k, v = vals
    nchunk = q.shape[0] // CHUNK
    causal, _ = _tri_masks()
    row = lax.broadcasted_iota(jnp.int32, (CHUNK, CHUNK), 0)
    col = lax.broadcasted_iota(jnp.int32, (CHUNK, CHUNK), 1)
    dist = (row - col).astype(f32)
    lane = lax.broadcasted_iota(jnp.int32, (CHUNK, q.shape[1]), 1)
    dmask = [jnp.where(causal, jnp.exp(jnp.where(causal, dist, 0.0) * _ret_log_gamma(h)), 0.0) for h in range(RET_HEADS)]
    chains = [(c, h) for c in range(nchunk) for h in range(RET_HEADS)]
    rows = lambda t, c: t[c * CHUNK:(c + 1) * CHUNK]
    scores = [bdot(jnp.where((lane // RET_DK) == h, rows(q, c), 0.0), rows(k, c), "nt") * dmask[h] for c, h in chains]
    inner = [bdot(s, rows(v, c)[:, h * RET_DV:(h + 1) * RET_DV], "nn") for s, (c, h) in zip(scores, chains)]
    return [_by_rows(inner, RET_HEADS)]


def ret_step(state, q, k, v, inner):
    bsz = q.shape[0]
    idx = lax.broadcasted_iota(jnp.int32, (CHUNK, 1), 0).astype(f32)
    lane = lax.broadcasted_iota(jnp.int32, (CHUNK, q.shape[2]), 1)
    chains = [(b, h) for b in range(bsz) for h in range(RET_HEADS)]
    part = lambda t, b, h: t[b, :, h * RET_DV:(h + 1) * RET_DV]
    cross = [bdot(q[b], s, "nn") for (b, h), s in zip(chains, state)]
    kz = [jnp.where((lane // RET_DK) == h, k[b], 0.0) * jnp.exp((CHUNK - 1.0 - idx) * _ret_log_gamma(h)) for b, h in chains]
    kv = [bdot(a, part(v, b, h), "tn") for a, (b, h) in zip(kz, chains)]
    outs = [x * jnp.exp((idx + 1.0) * _ret_log_gamma(h)) + part(inner, b, h) for x, (b, h) in zip(cross, chains)]
    new_state = [s * math.exp(CHUNK * _ret_log_gamma(h)) + x for s, x, (b, h) in zip(state, kv, chains)]
    return new_state, jnp.concatenate([jnp.concatenate(outs[b * RET_HEADS:(b + 1) * RET_HEADS], axis=1)[None]
                                       for b in range(bsz)], axis=0)


def chunk_scan(step_fn, ins, state_shape, out_width, name):
    bsz, seq, _ = ins[0].shape
    nchunk = seq // CHUNK
    nin = len(ins)
    nh = state_shape[0]

    def body(*refs):
        in_refs = refs[:nin]
        o_ref, ck_ref, s_ref = refs[nin:]

        @pl.when(pl.program_id(0) == 0)
        def _():
            s_ref[...] = jnp.zeros_like(s_ref)

        state = [s_ref[i] for i in range(bsz * nh)]
        for i in range(bsz * nh):
            ck_ref[i // nh, i % nh] = state[i]
        new_state, out = step_fn(state, *[x[...] for x in in_refs])
        o_ref[...] = out
        for i in range(bsz * nh):
            s_ref[i] = new_state[i]

    in_specs = [pl.BlockSpec((bsz, CHUNK, x.shape[2]), lambda n: (0, n, 0)) for x in ins]
    out_specs = [pl.BlockSpec((bsz, CHUNK, out_width), lambda n: (0, n, 0)),
                 pl.BlockSpec((bsz, None) + tuple(state_shape), lambda n: (0, n, 0, 0, 0))]
    out_shape = [jax.ShapeDtypeStruct((bsz, seq, out_width), f32),
                 jax.ShapeDtypeStruct((bsz, nchunk) + tuple(state_shape), f32)]
    return pl.pallas_call(body, name=name, grid=(nchunk,), in_specs=in_specs, out_specs=out_specs, out_shape=out_shape,
                          scratch_shapes=[pltpu.VMEM((bsz * nh,) + tuple(state_shape[1:]), f32)],
                          compiler_params=_params())(*ins)


def chunk_scan_bwd(step_fn, ins, ckpt, dout, name):
    bsz, seq, _ = ins[0].shape
    nchunk = seq // CHUNK
    nin = len(ins)
    state_shape = ckpt.shape[2:]
    nh = state_shape[0]

    def body(*refs):
        in_refs = refs[:nin]
        ck_ref, do_ref = refs[nin:nin + 2]
        din_refs = refs[nin + 2:nin + 2 + nin]
        ds_ref = refs[-1]

        @pl.when(pl.program_id(0) == 0)
        def _():
            ds_ref[...] = jnp.zeros_like(ds_ref)

        state = [ck_ref[i // nh, i % nh] for i in range(bsz * nh)]
        _, vjp = jax.vjp(step_fn, state, *[x[...] for x in in_refs])
        grads = vjp(([ds_ref[i] for i in range(bsz * nh)], do_ref[...]))
        for i in range(bsz * nh):
            ds_ref[i] = grads[0][i]
        for d_ref, d in zip(din_refs, grads[1:]):
            d_ref[...] = d

    rev = lambda n: (0, nchunk - 1 - n, 0)
    in_specs = [pl.BlockSpec((bsz, CHUNK, x.shape[2]), rev) for x in ins]
    in_specs += [pl.BlockSpec((bsz, None) + tuple(state_shape), lambda n: (0, nchunk - 1 - n, 0, 0, 0)),
                 pl.BlockSpec((bsz, CHUNK, dout.shape[2]), rev)]
    out_specs = [pl.BlockSpec((bsz, CHUNK, x.shape[2]), rev) for x in ins]
    out_shape = [jax.ShapeDtypeStruct(x.shape, f32) for x in ins]
    return pl.pallas_call(body, name=name, grid=(nchunk,), in_specs=in_specs, out_specs=out_specs, out_shape=out_shape,
                          scratch_shapes=[pltpu.VMEM((bsz * nh,) + tuple(state_shape[1:]), f32)],
                          compiler_params=_params())(*ins, ckpt, dout)


LRU_ROWS = 512


def lru_scan(a, b):
    bsz, seq, width = a.shape
    rb = min(LRU_ROWS, seq)

    def body(a_ref, b_ref, h_ref, hp_ref, carry_ref):
        @pl.when(pl.program_id(1) == 0)
        def _():
            carry_ref[...] = jnp.zeros_like(carry_ref)

        row = lax.broadcasted_iota(jnp.int32, (8, width), 0)

        def tile(t, hprev):
            r0 = pl.multiple_of(t * 8, 8)
            ca, cbv = a_ref[pl.ds(r0, 8), :], b_ref[pl.ds(r0, 8), :]
            for s in (1, 2, 4):
                m = row >= s
                cbv = jnp.where(m, ca * pltpu.roll(cbv, s, 0) + cbv, cbv)
                ca = jnp.where(m, ca * pltpu.roll(ca, s, 0), ca)
            h = cbv + ca * hprev
            h_ref[pl.ds(r0, 8), :] = h
            hp_ref[pl.ds(r0, 8), :] = jnp.where(row == 0, hprev, pltpu.roll(h, 1, 0))
            return _row_pick(h, 7)

        carry_ref[0:1, :] = lax.fori_loop(0, rb // 8, tile, carry_ref[0:1, :])

    spec = pl.BlockSpec((None, rb, width), lambda bi, i: (bi, i, 0))
    return pl.pallas_call(body, name="lru_scan", grid=(bsz, seq // rb), in_specs=[spec, spec], out_specs=[spec, spec],
                          out_shape=[jax.ShapeDtypeStruct(a.shape, f32)] * 2,
                          scratch_shapes=[pltpu.VMEM((8, width), f32)], compiler_params=_params())(a, b)


def lru_scan_bwd(a, hp, dh):
    bsz, seq, width = a.shape
    rb = min(LRU_ROWS, seq)
    nblk = seq // rb

    def body(a_ref, hp_ref, dh_ref, da_ref, db_ref, carry_ref):
        @pl.when(pl.program_id(1) == 0)
        def _():
            carry_ref[...] = jnp.zeros_like(carry_ref)

        row = lax.broadcasted_iota(jnp.int32, (8, width), 0)
        ntile = rb // 8

        def tile(t, mu_next):
            r0 = pl.multiple_of((ntile - 1 - t) * 8, 8)
            ca, dh_t = a_ref[pl.ds(r0, 8), :], dh_ref[pl.ds(r0, 8), :]
            cbv = ca * dh_t
            for s in (1, 2, 4):
                m = row < 8 - s
                cbv = jnp.where(m, ca * pltpu.roll(cbv, 8 - s, 0) + cbv, cbv)
                ca = jnp.where(m, ca * pltpu.roll(ca, 8 - s, 0), ca)
            mu = cbv + ca * mu_next
            lam = dh_t + jnp.where(row == 7, mu_next, pltpu.roll(mu, 7, 0))
            db_ref[pl.ds(r0, 8), :] = lam
            da_ref[pl.ds(r0, 8), :] = lam * hp_ref[pl.ds(r0, 8), :]
            return _row_pick(mu, 0)

        carry_ref[0:1, :] = lax.fori_loop(0, ntile, tile, carry_ref[0:1, :])

    spec = pl.BlockSpec((None, rb, width), lambda bi, i: (bi, nblk - 1 - i, 0))
    return pl.pallas_call(body, name="lru_scan_bwd", grid=(bsz, nblk), in_specs=[spec] * 3, out_specs=[spec, spec],
                          out_shape=[jax.ShapeDtypeStruct(a.shape, f32)] * 2,
                          scratch_shapes=[pltpu.VMEM((8, width), f32)], compiler_params=_params())(a, hp, dh)


def final_loss(x, g, target):
    n, d = x.shape
    r = min(256, n)

    def body(x_ref, g_ref, t_ref, loss_ref, dx_ref, dg_ref, dx16_ref):
        @pl.when(pl.program_id(0) == 0)
        def _():
            loss_ref[...] = jnp.zeros_like(loss_ref)
            dg_ref[...] = jnp.zeros_like(dg_ref)

        tgt = t_ref[...]

        def loss_fn(xv, gv):
            y = f_norm([xv], [gv])[0]
            return 0.5 * jnp.sum(jnp.mean(jnp.square(y - tgt), axis=-1, keepdims=True), axis=0, keepdims=True)

        val, vjp = jax.vjp(loss_fn, x_ref[...], g_ref[...])
        dx, dg = vjp(jnp.ones_like(val))
        loss_ref[...] += val
        dx_ref[...] = dx
        dx16_ref[...] = dx.astype(dx16_ref.dtype)
        dg_ref[...] += dg

    row = pl.BlockSpec((r, d), lambda i: (i, 0))
    return pl.pallas_call(
        body, name="final_loss", grid=(n // r,), in_specs=[row, pl.BlockSpec((1, d), lambda i: (0, 0)), row],
        out_specs=[pl.BlockSpec((8, LANE), lambda i: (0, 0)), row, pl.BlockSpec((1, d), lambda i: (0, 0)), row],
        out_shape=[jax.ShapeDtypeStruct((8, LANE), f32), jax.ShapeDtypeStruct((n, d), f32), jax.ShapeDtypeStruct((1, d), f32),
                   jax.ShapeDtypeStruct((n, d), bf16)],
        compiler_params=_params())(x, g, target)


_HBM = pl.BlockSpec(memory_space=pltpu.HBM)
_SEM = pl.BlockSpec(memory_space=pltpu.SEMAPHORE)
_EFFECT = pltpu.SideEffectType.DATAFLOW_SIDE_EFFECTING


def _peer(k):
    mx, my, mc = lax.axis_index("x"), lax.axis_index("y"), lax.axis_index("c")
    px, py, pc = (mx + (k >> 2)) % 2, (my + ((k >> 1) & 1)) % 2, (mc + (k & 1)) % 2
    return (px, py, pc), 4 * px + 2 * py + pc


def _peer_copy(k, i, x_ref, land_ref, send_sems, recv_sems, scatter):
    me = 4 * lax.axis_index("x") + 2 * lax.axis_index("y") + lax.axis_index("c")
    dev, slot = _peer(k)
    sem = i * (N_DEV - 1) + k - 1
    return pltpu.make_async_remote_copy(
        src_ref=x_ref.at[slot] if scatter else x_ref, dst_ref=land_ref.at[me], send_sem=send_sems.at[sem],
        recv_sem=recv_sems.at[sem], device_id=dev, device_id_type=pl.DeviceIdType.MESH)


def _own_copy(i, x_ref, land_ref, own_sems, scatter):
    me = 4 * lax.axis_index("x") + 2 * lax.axis_index("y") + lax.axis_index("c")
    return pltpu.make_async_copy(x_ref.at[me] if scatter else x_ref, land_ref.at[me], own_sems.at[i])


def exchange_start(xs, scatters, name):
    nx = len(xs)
    lands = [lax.empty((N_DEV,) + tuple(x.shape[1:] if sc else x.shape), x.dtype) for x, sc in zip(xs, scatters)]
    nsem = nx * (N_DEV - 1)

    def body(*refs):
        x_refs, land_refs = refs[:nx], refs[nx:2 * nx]
        send_sems, recv_sems, own_sems = refs[2 * nx:2 * nx + 3]
        token = refs[-1]
        for i in range(nx):
            for k in range(1, N_DEV):
                _peer_copy(k, i, x_refs[i], land_refs[i], send_sems, recv_sems, scatters[i]).start()
            _own_copy(i, x_refs[i], land_refs[i], own_sems, scatters[i]).start()
        token[...] = jnp.zeros_like(token)

    hbm = lambda a: pltpu.HBM(a.shape, a.dtype)
    res = pl.pallas_call(
        body, name=name, in_specs=(_HBM,) * (2 * nx),
        out_specs=(_SEM, _SEM, _SEM) + (_HBM,) * (2 * nx) + (pl.BlockSpec(memory_space=pltpu.VMEM),),
        input_output_aliases={i: 3 + i for i in range(2 * nx)},
        out_shape=(pltpu.SemaphoreType.DMA((nsem,)), pltpu.SemaphoreType.DMA((nsem,)), pltpu.SemaphoreType.DMA((nx,)),
                   *[hbm(a) for a in xs], *[hbm(a) for a in lands], jax.ShapeDtypeStruct((8, LANE), f32)),
        compiler_params=pltpu.CompilerParams(has_side_effects=_EFFECT),
    )(*[pltpu.with_memory_space_constraint(a, pltpu.HBM) for a in list(xs) + lands])
    return (res[0], res[1], res[2], list(res[3:3 + nx]), list(res[3 + nx:3 + 2 * nx]), tuple(scatters)), res[-1]


def exchange_wait(started, after, name):
    send_sems, recv_sems, own_sems, x_thrus, land_thrus, scatters = started
    nx = len(x_thrus)

    def body(*refs):
        x_refs, land_refs = refs[:nx], refs[nx:2 * nx]
        send_sems, recv_sems, own_sems = refs[2 * nx:2 * nx + 3]
        for i in range(nx):
            for k in range(1, N_DEV):
                cp = _peer_copy(k, i, x_refs[i], land_refs[i], send_sems, recv_sems, scatters[i])
                cp.wait_send()
                cp.wait_recv()
            _own_copy(i, x_refs[i], land_refs[i], own_sems, scatters[i]).wait()

    hbm = lambda a: pltpu.HBM(a.shape, a.dtype)
    res = pl.pallas_call(
        body, name=name, in_specs=(_HBM,) * (2 * nx) + (_SEM, _SEM, _SEM, pl.BlockSpec(memory_space=pl.ANY)),
        out_specs=(_HBM,) * (2 * nx), input_output_aliases={i: i for i in range(2 * nx)},
        out_shape=tuple(hbm(a) for a in list(x_thrus) + list(land_thrus)),
        compiler_params=pltpu.CompilerParams(has_side_effects=_EFFECT),
    )(*x_thrus, *land_thrus, send_sems, recv_sems, own_sems, after)
    return list(res[nx:])


def sum_slots(x, name):
    _, rows_total, cols = x.shape
    row_bytes = N_DEV * ((cols + LANE - 1) // LANE) * LANE * x.dtype.itemsize
    r = _pick_rows(rows_total, max(16, (4 * 1024 * 1024) // row_bytes // 16 * 16))

    def body(x_ref, o_ref):
        acc = x_ref[0].astype(f32)
        for s in range(1, N_DEV):
            acc = acc + x_ref[s].astype(f32)
        o_ref[...] = acc

    return pl.pallas_call(body, name=name, grid=(rows_total // r,),
                          in_specs=[pl.BlockSpec((N_DEV, r, cols), lambda i: (0, i, 0))],
                          out_specs=pl.BlockSpec((r, cols), lambda i: (i, 0)),
                          out_shape=jax.ShapeDtypeStruct((rows_total, cols), f32), compiler_params=_params())(x)


def _pick_rows(total, pref):
    best = None
    for d in range(16, min(total, pref) + 1, 16):
        if total % d == 0:
            best = d
    return best if best is not None else total


def adamw(w, g, m, v, name):
    shape = w.shape
    if w.ndim == 1:
        w2, g2, m2, v2 = (t.reshape(1, -1) for t in (w, g, m, v))
    else:
        w2, g2, m2, v2 = (t.reshape(-1, shape[-1]) for t in (w, g, m, v))
    rows_total, cols = w2.shape
    r = _pick_rows(rows_total, max(16, (512 * 1024) // max(cols, 1) // 16 * 16))
    c1, c2 = 1.0 / (1.0 - ADAM_B1 ** ADAM_STEP), 1.0 / (1.0 - ADAM_B2 ** ADAM_STEP)

    def body(w_ref, g_ref, m_ref, v_ref, d_ref, nm_ref, nv_ref):
        gv = g_ref[...]
        nm = ADAM_B1 * m_ref[...] + (1.0 - ADAM_B1) * gv
        nv = ADAM_B2 * v_ref[...] + (1.0 - ADAM_B2) * jnp.square(gv)
        d_ref[...] = -ADAM_LR * ((nm * c1) / (jnp.sqrt(nv * c2) + ADAM_EPS) + ADAM_WD * w_ref[...])
        nm_ref[...] = nm
        nv_ref[...] = nv

    spec = pl.BlockSpec((r, cols), lambda i: (i, 0))
    outs = pl.pallas_call(body, name=name, grid=(rows_total // r,), in_specs=[spec] * 4, out_specs=[spec] * 3,
                          out_shape=[jax.ShapeDtypeStruct((rows_total, cols), f32)] * 3, compiler_params=_params())(w2, g2, m2, v2)
    return tuple(o.reshape(shape) for o in outs)


def _const(j):
    return lambda _: j


def _layer_fwd(x, wl, fetch_rest, cos, sin, bsz, seq):
    n = x.shape[0]
    sv = {"x_in": x}
    row1 = lambda a: (a, (1, a.shape[1]), lambda j: (0, 0))
    h = rowmap(f_norm, [(x, D_MODEL, 0)], [row1(wl["attn_norm"])], [(D_MODEL, bf16)], 1, "norm_fwd")[0]
    u = mm(h, wl["w_in"], "nn", "mm_in")
    sv["h"], sv["u"] = h, u
    u3 = u.reshape(bsz, seq, U_PAD)
    wl = dict(wl)
    wl.update(fetch_rest(u))
    sv["wl"] = wl

    qkv = []
    for kind in range(3):
        cw = (wl["dn_conv_w"], (4, LANE), functools.partial(lambda j, kind: (0, 4 * kind + j), kind=kind))
        qkv.append(seqmap(functools.partial(f_dn_pre, kind), [(u3, U_QKV // LANE + 4 * kind)], [cw], 1, 4, "dn_pre%d" % kind)[0])
    gb = rowmap(f_dn_gates, [(u, 512, U_AB // 512)], [(wl["dn_gate_p"], (8, LANE), lambda j: (0, 0))], [(LANE, f32)], 1,
                "dn_gates", rows=512)[0]
    gb3 = gb.reshape(bsz, seq, LANE)
    crow = CHUNK * CHUNKS_PER_STEP
    dn_in = [(t.reshape(n, 512), 512, 0) for t in qkv] + [(gb, LANE, 0)]
    prep_a = rowmap(dn_prep, dn_in, [], [(512, f32)] * 5 + [(LANE, f32), (512, f32)], 1, "dn_prep", rows=crow)
    dn_in = dn_in + [(prep_a[6], 512, 0)]
    prep_a = [t.reshape(bsz, seq, t.shape[1]) for t in prep_a[:6]]
    o_a, ck_a = chunk_scan(dn_step, prep_a, (DN_HEADS, DN_DK, DN_DK), 512, "dn_scan")
    y_a = rowmap(per_head(f_dn_post), [(o_a.reshape(n, 512), 512, 0), (u, 512, U_Z // 512)],
                 [(wl["dn_norm_w"], (1, LANE), lambda j: (0, 0))], [(512, bf16)], 1, "dn_post")[0]
    sv.update(dn_in=dn_in, prep_a=prep_a, o_a=o_a, ck_a=ck_a, y_a=y_a)

    q_b, k_b = rowmap(f_ret_pre, [(u, 256, U_RQ // 256), (u, 256, U_RK // 256), (cos, 256, 0), (sin, 256, 0)], [],
                      [(256, f32), (256, f32)], 1, "ret_pre")
    q_b3, k_b3 = q_b.reshape(bsz, seq, 256), k_b.reshape(bsz, seq, 256)
    v_b3 = lax.slice_in_dim(u3, U_RV, U_RV + 512, axis=2)
    ret_in = [(q_b, 256, 0), (k_b, 256, 0), (u, 512, U_RV // 512)]
    inner = rowmap(ret_prep, ret_in, [], [(512, f32)], 1, "ret_prep", rows=crow)[0]
    ret_seq = [q_b3, k_b3, v_b3, inner.reshape(bsz, seq, 512)]
    o_b, ck_b = chunk_scan(ret_step, ret_seq, (RET_HEADS, 256, RET_DV), 512, "ret_scan")
    y_b = rowmap(per_head(f_ret_post), [(o_b.reshape(n, 512), 512, 0), (u, 512, U_RG // 512)], [], [(512, bf16)], 1,
                 "ret_post")[0]
    sv.update(ret_in=ret_in, ret_seq=ret_seq, o_b=o_b, ck_b=ck_b, y_b=y_b)

    lru_params = _lru_params(wl)
    a_c, b_c = seqmap(f_lru_pre, [(u3, U_CX // LANE)], lru_params, 2, 4, "lru_pre")
    h_c, hp_c = lru_scan(a_c, b_c)
    y_c = rowmap(f_lru_post, [(h_c.reshape(n, 512), 512, 0), (u, 512, U_CG // 512)], [], [(512, bf16)], 1, "lru_post")[0]
    sv.update(a_c=a_c, hp_c=hp_c, h_c=h_c, y_c=y_c)

    br = [mm(y, wl["w_branch"][i], "nn", "mm_branch") for i, y in enumerate((y_a, y_b, y_c))]
    merged = rowmap(f_merge, [(u, D_MODEL, i) for i in range(3)] + [(b, D_MODEL, 0) for b in br], [], [(D_MODEL, bf16)], 1,
                    "merge")[0]
    x_mid = mm(merged, wl["w_out"], "nn", "mm_out", add=x)
    sv.update(br=br, merged=merged, x_mid=x_mid)

    h2 = rowmap(f_norm, [(x_mid, D_MODEL, 0)], [row1(wl["ffn_norm"])], [(D_MODEL, bf16)], 1, "norm_fwd")[0]
    up = mm(h2, wl["w_up"], "nn", "mm_up")
    act = seqmap(f_ffn_mid, [(up.reshape(bsz, seq, 2 * D_FF), 0), (up.reshape(bsz, seq, 2 * D_FF), D_FF // LANE)],
                 _ffn_params(wl), 1, D_FF // LANE, "ffn_mid", out_dtype=bf16)[0]
    act = act.reshape(n, D_FF)
    x_out = mm(act, wl["w_down"], "nn", "mm_down", add=x_mid)
    sv.update(h2=h2, up=up, act=act)
    return x_out, sv


def _lru_params(wl):
    col = lambda a: (a, (a.shape[0], LANE), lambda j: (0, j))
    blk = lambda a: (a, (None, LANE, LANE), lambda j: (j, 0, 0))
    return [col(wl["lru_conv_w"]), col(wl["lru_conv_b"]), blk(wl["lru_wa"]), col(wl["lru_ba"]), blk(wl["lru_wx"]),
            col(wl["lru_bx"]), col(wl["lru_lambda"])]


def _ffn_params(wl):
    nb = D_FF // LANE
    return [(wl["ffn_conv_w"], (3, LANE), lambda j: (0, j)), (wl["ffn_conv_w"], (3, LANE), lambda j: (0, nb + j)),
            (wl["ffn_conv_b"], (1, LANE), lambda j: (0, j)), (wl["ffn_conv_b"], (1, LANE), lambda j: (0, nb + j))]


def _layer_bwd(dx, dx16, sv, cos, sin, bsz, seq, emit, dep):
    n = dx.shape[0]
    gr = {}
    wl = sv["wl"]
    u, x_in, x_mid = sv["u"], sv["x_in"], sv["x_mid"]
    u3 = u.reshape(bsz, seq, U_PAD)
    row1 = lambda a: (a, (1, a.shape[1]), lambda j: (0, 0))

    d_act = mm(dx16, wl["w_down"], "nt", "mm_down_dx", dep=dep)
    gr["w_down"] = mm(sv["act"], dx16, "tn", "mm_down_dw")
    up3 = sv["up"].reshape(bsz, seq, 2 * D_FF)
    (d_gate, d_val), dps = seqmap_bwd(f_ffn_mid, [(up3, 0), (up3, D_FF // LANE)], _ffn_params(wl),
                                      [d_act.reshape(bsz, seq, D_FF)], D_FF // LANE, "ffn_mid_bwd", din_dtype=bf16)
    gr["ffn_conv_w"] = jnp.concatenate([_cols(dps[0]), _cols(dps[1])], axis=1)
    gr["ffn_conv_b"] = jnp.concatenate([_cols(dps[2]), _cols(dps[3])], axis=1)[0]
    d_up = jnp.concatenate([d_gate, d_val], axis=2).reshape(n, 2 * D_FF)
    gr["w_up"] = mm(sv["h2"], d_up, "tn", "mm_up_dw")
    token = emit("ffn", {k: gr[k] for k in ("w_up", "w_down")})
    d_h2 = mm(d_up, wl["w_up"], "nt", "mm_up_dx", dep=token)
    (dx_mid,), (dg,), ex = rowmap_bwd(f_norm, [(x_mid, D_MODEL, 0)], [row1(wl["ffn_norm"])], [d_h2], 1, "norm_bwd", add=[dx],
                                      copy16=0)
    dx_mid16 = ex["copy16"]
    gr["ffn_norm"] = dg[0, 0]

    du = lax.empty((n, U_PAD), bf16)
    du3 = lambda: du.reshape(bsz, seq, U_PAD)

    d_merged = mm(dx_mid16, wl["w_out"], "nt", "mm_out_dx")
    gr["w_out"] = mm(sv["merged"], dx_mid16, "tn", "mm_out_dw")
    dm, _, ex = rowmap_bwd(f_merge, [(u, D_MODEL, i) for i in range(3)] + [(b, D_MODEL, 0) for b in sv["br"]], [], [d_merged],
                           1, "merge_bwd", din_dtypes=[bf16] * 6, into=(du, U_GATES // (3 * D_MODEL), [0, 1, 2]))
    du, d_br = ex["into"], dm[3:]
    ys = (sv["y_a"], sv["y_b"], sv["y_c"])
    d_ys = [mm(d_br[i], wl["w_branch"][i], "nt", "mm_branch_dx") for i in range(3)]
    gr["w_branch"] = jnp.stack([mm(ys[i], d_br[i], "tn", "mm_branch_dw") for i in range(3)])

    (d_hc, _), _, ex = rowmap_bwd(f_lru_post, [(sv["h_c"].reshape(n, 512), 512, 0), (u, 512, U_CG // 512)], [], [d_ys[2]], 1,
                                  "lru_post_bwd", into=(du, U_CG // 512, [1]))
    du = ex["into"]
    d_a, d_b = lru_scan_bwd(sv["a_c"], sv["hp_c"], d_hc.reshape(bsz, seq, 512))
    (du_new,), dps = seqmap_bwd(f_lru_pre, [(u3, U_CX // LANE)], _lru_params(wl), [d_a, d_b], 4, "lru_pre_bwd", din_dtype=bf16,
                                into=(du3(), U_CX // LANE))
    du = du_new.reshape(n, U_PAD)
    gr["lru_conv_w"], gr["lru_conv_b"] = _cols(dps[0]), _cols(dps[1])[0]
    gr["lru_wa"], gr["lru_ba"], gr["lru_wx"], gr["lru_bx"] = dps[2], dps[3][:, 0], dps[4], dps[5][:, 0]
    gr["lru_lambda"] = _cols(dps[6])[0]

    (d_ob, _), _, ex = rowmap_bwd(per_head(f_ret_post), [(sv["o_b"].reshape(n, 512), 512, 0), (u, 512, U_RG // 512)], [],
                                  [d_ys[1]], 1, "ret_post_bwd", into=(du, U_RG // 512, [1]))
    du = ex["into"]
    crow = CHUNK * CHUNKS_PER_STEP
    d_ret = chunk_scan_bwd(ret_step, sv["ret_seq"], sv["ck_b"], d_ob.reshape(bsz, seq, 512), "ret_scan_bwd")
    d_ret = [t.reshape(n, t.shape[2]) for t in d_ret]
    (d_qb, d_kb, _), _, ex = rowmap_bwd(ret_prep, sv["ret_in"], [], [d_ret[3]], 1, "ret_prep_bwd", rows=crow, add=d_ret[:3],
                                        into=(du, U_RV // 512, [2]))
    du = ex["into"]
    _, _, ex = rowmap_bwd(f_ret_pre, [(u, 256, U_RQ // 256), (u, 256, U_RK // 256), (cos, 256, 0), (sin, 256, 0)], [],
                          [d_qb, d_kb], 1, "ret_pre_bwd", din_dtypes=[f32, f32, None, None], into=(du, U_RQ // 512, [0, 1]))
    du = ex["into"]

    (d_oa, _), (dnw,), ex = rowmap_bwd(per_head(f_dn_post), [(sv["o_a"].reshape(n, 512), 512, 0), (u, 512, U_Z // 512)],
                                       [(wl["dn_norm_w"], (1, LANE), lambda j: (0, 0))], [d_ys[0]], 1, "dn_post_bwd",
                                       into=(du, U_Z // 512, [1]))
    du = ex["into"]
    gr["dn_norm_w"] = dnw[0, 0]
    d_prep = chunk_scan_bwd(dn_step, sv["prep_a"], sv["ck_a"], d_oa.reshape(bsz, seq, 512), "dn_scan_bwd")
    (d_q, d_k, d_v, d_gb, _), _, _ = rowmap_bwd(dn_prep, sv["dn_in"], [], [t.reshape(n, t.shape[2]) for t in d_prep], 1,
                                                "dn_prep_bwd", rows=crow, din_dtypes=[f32] * 4 + [None])
    d_q, d_k, d_v = (t.reshape(bsz, seq, 512) for t in (d_q, d_k, d_v))
    _, (dgp,), ex = rowmap_bwd(f_dn_gates, [(u, 512, U_AB // 512)], [(wl["dn_gate_p"], (8, LANE), lambda j: (0, 0))],
                               [d_gb], 1, "dn_gates_bwd", rows=512, into=(du, U_AB // 512, [0]))
    du = ex["into"]
    gr["dn_a_log"], gr["dn_dt_bias"] = dgp[0, 0, :DN_HEADS], dgp[0, 1, :DN_HEADS]
    d_cw = []
    for kind, d_t in enumerate((d_q, d_k, d_v)):
        cw = (wl["dn_conv_w"], (4, LANE), functools.partial(lambda j, kind: (0, 4 * kind + j), kind=kind))
        (du_new,), (dcw,) = seqmap_bwd(functools.partial(f_dn_pre, kind), [(u3, U_QKV // LANE + 4 * kind)], [cw], [d_t], 4,
                                       "dn_pre%d_bwd" % kind, din_dtype=bf16, into=(du3(), U_QKV // LANE + 4 * kind))
        du = du_new.reshape(n, U_PAD)
        d_cw.append(_cols(dcw))
    gr["dn_conv_w"] = jnp.concatenate(d_cw, axis=1)

    gr["w_in"] = _unpad_w_in(mm(sv["h"], du, "tn", "mm_in_dw"))
    token = emit("mix", {k: gr[k] for k in ("w_in", "w_branch", "w_out")})
    d_h = mm(du, wl["w_in"], "nt", "mm_in_dx", dep=token)
    (dx_in,), (dg,), ex = rowmap_bwd(f_norm, [(x_in, D_MODEL, 0)], [row1(wl["attn_norm"])], [d_h], 1, "norm_bwd", add=[dx_mid],
                                     copy16=0)
    gr["attn_norm"] = dg[0, 0]
    big = ("w_in", "w_branch", "w_out", "w_up", "w_down")
    return dx_in, ex["copy16"], emit("small", {k: g for k, g in gr.items() if k not in big})


def _cols(dp):
    ncol, p, _ = dp.shape
    return jnp.transpose(dp, (1, 0, 2)).reshape(p, ncol * LANE)


def _pad_w_in(w):
    segs = sorted(_IN_SEGS, key=lambda s: s[2])
    parts = [lax.slice_in_dim(w, src, src + width, axis=1) for src, width, _ in segs]
    end = segs[-1][2] + segs[-1][1]
    return jnp.concatenate(parts + [jnp.zeros((w.shape[0], U_PAD - end), w.dtype)], axis=1)


def _unpad_w_in(wp):
    return jnp.concatenate([lax.slice_in_dim(wp, dst, dst + width, axis=1) for _, width, dst in _IN_SEGS], axis=1)


def _rope_tables(positions):
    half = RET_DK // 2
    inv = ROPE_BASE ** (-jnp.arange(half, dtype=f32) / half)
    ang = positions.astype(f32).reshape(-1, 1) * inv
    cos, sin = jnp.cos(ang), jnp.sin(ang)
    return jnp.tile(cos, (1, 2 * RET_HEADS)), jnp.tile(sin, (1, 2 * RET_HEADS))


def _layer_weights(lw):
    wl = {}
    wl["w_in"] = _pad_w_in(lw["w_in"])
    for k in ("dn_conv_w", "lru_conv_w", "ffn_conv_w", "lru_wa", "lru_wx"):
        wl[k] = lw[k]
    for k in ("attn_norm", "ffn_norm", "dn_norm_w", "lru_conv_b", "lru_lambda", "ffn_conv_b", "lru_ba", "lru_bx"):
        wl[k] = lw[k].reshape(1, -1)
    gp = jnp.zeros((8, LANE), f32)
    wl["dn_gate_p"] = gp.at[0, :DN_HEADS].set(lw["dn_a_log"]).at[1, :DN_HEADS].set(lw["dn_dt_bias"])
    return wl


REST = ("w_branch", "w_out", "w_up", "w_down")


def forward_backward(x, positions, target, layer_weights, final_norm, on_head, on_grads):
    bsz, seq, d = x.shape
    n = bsz * seq
    cos, sin = _rope_tables(positions)
    xs = x.reshape(n, d)
    saved = []
    for layer in range(DEPTH):
        first, fetch_rest = layer_weights(layer, xs)
        xs, sv = _layer_fwd(xs, _layer_weights(first), fetch_rest, cos, sin, bsz, seq)
        saved.append(sv)
    loss, dx, d_final, dx16 = final_loss(xs, final_norm.reshape(1, d), target.reshape(n, d))
    on_head(loss[0, 0], d_final[0])
    token = None
    for layer in reversed(range(DEPTH)):
        dx, dx16, token = _layer_bwd(dx, dx16, saved[layer], cos, sin, bsz, seq, functools.partial(on_grads, layer), token)
    return dx.reshape(bsz, seq, d)


def local_step(x, positions, target, full):
    grads, head = {layer: {} for layer in range(DEPTH)}, {}

    def layer_weights(layer, _):
        return ({k: a[layer] for k, a in full.items() if k != "final_norm" and k not in REST},
                lambda after: {k: full[k][layer] for k in REST})

    gx = forward_backward(x, positions, target, layer_weights, full["final_norm"],
                          lambda loss, d_final: head.update(loss=loss, d_final=d_final),
                          lambda layer, group, gr: grads[layer].update(gr))
    stacked = {k: jnp.stack([grads[layer][k] for layer in range(DEPTH)]) for k in grads[0]}
    stacked["final_norm"] = head["d_final"]
    return head["loss"], gx, stacked


BIG = (("w_in", 2), ("w_branch", 3), ("w_out", 1), ("w_up", 2), ("w_down", 1))
SMALL_SHARDED = (("dn_conv_w", 2), ("lru_conv_w", 2), ("ffn_conv_w", 2))
REPLICATED = ("attn_norm", "dn_a_log", "dn_dt_bias", "dn_norm_w", "lru_conv_b", "lru_wa", "lru_ba", "lru_wx", "lru_bx",
              "lru_lambda", "ffn_norm", "ffn_conv_b", "final_norm")
WEIGHTS = ("attn_norm", "w_in", "dn_conv_w", "dn_a_log", "dn_dt_bias", "dn_norm_w", "lru_conv_w", "lru_conv_b", "lru_wa",
           "lru_ba", "lru_wx", "lru_bx", "lru_lambda", "w_branch", "w_out", "ffn_norm", "w_up", "ffn_conv_w", "ffn_conv_b",
           "w_down", "final_norm")


def _pack(arrs, dtype, align=16 * LANE):
    flat = jnp.concatenate([a.reshape(-1).astype(dtype) for a in arrs])
    pad = (-flat.shape[0]) % align
    return jnp.pad(flat, (0, pad)).reshape(-1, LANE)


def _unpack(rows, shapes):
    flat = rows.reshape(-1)
    out, pos = [], 0
    for shp in shapes:
        size = math.prod(shp)
        out.append(lax.slice_in_dim(flat, pos, pos + size).reshape(shp))
        pos += size
    return out


def kernel(x, positions, attn_norm, w_in, dn_conv_w, dn_a_log, dn_dt_bias, dn_norm_w, lru_conv_w, lru_conv_b, lru_wa, lru_ba, lru_wx, lru_bx, lru_lambda, w_branch, w_out, ffn_norm, w_up, ffn_conv_w, ffn_conv_b, w_down, final_norm, loss_target, m_attn_norm, m_w_in, m_dn_conv_w, m_dn_a_log, m_dn_dt_bias, m_dn_norm_w, m_lru_conv_w, m_lru_conv_b, m_lru_wa, m_lru_ba, m_lru_wx, m_lru_bx, m_lru_lambda, m_w_branch, m_w_out, m_ffn_norm, m_w_up, m_ffn_conv_w, m_ffn_conv_b, m_w_down, m_final_norm, v_attn_norm, v_w_in, v_dn_conv_w, v_dn_a_log, v_dn_dt_bias, v_dn_norm_w, v_lru_conv_w, v_lru_conv_b, v_lru_wa, v_lru_ba, v_lru_wx, v_lru_bx, v_lru_lambda, v_w_branch, v_w_out, v_ffn_norm, v_w_up, v_ffn_conv_w, v_ffn_conv_b, v_w_down, v_final_norm):
    w = dict(attn_norm=attn_norm, w_in=w_in, dn_conv_w=dn_conv_w, dn_a_log=dn_a_log, dn_dt_bias=dn_dt_bias, dn_norm_w=dn_norm_w,
             lru_conv_w=lru_conv_w, lru_conv_b=lru_conv_b, lru_wa=lru_wa, lru_ba=lru_ba, lru_wx=lru_wx, lru_bx=lru_bx,
             lru_lambda=lru_lambda, w_branch=w_branch, w_out=w_out, ffn_norm=ffn_norm, w_up=w_up, ffn_conv_w=ffn_conv_w,
             ffn_conv_b=ffn_conv_b, w_down=w_down, final_norm=final_norm)
    m = dict(attn_norm=m_attn_norm, w_in=m_w_in, dn_conv_w=m_dn_conv_w, dn_a_log=m_dn_a_log, dn_dt_bias=m_dn_dt_bias,
             dn_norm_w=m_dn_norm_w, lru_conv_w=m_lru_conv_w, lru_conv_b=m_lru_conv_b, lru_wa=m_lru_wa, lru_ba=m_lru_ba,
             lru_wx=m_lru_wx, lru_bx=m_lru_bx, lru_lambda=m_lru_lambda, w_branch=m_w_branch, w_out=m_w_out, ffn_norm=m_ffn_norm,
             w_up=m_w_up, ffn_conv_w=m_ffn_conv_w, ffn_conv_b=m_ffn_conv_b, w_down=m_w_down, final_norm=m_final_norm)
    v = dict(attn_norm=v_attn_norm, w_in=v_w_in, dn_conv_w=v_dn_conv_w, dn_a_log=v_dn_a_log, dn_dt_bias=v_dn_dt_bias,
             dn_norm_w=v_dn_norm_w, lru_conv_w=v_lru_conv_w, lru_conv_b=v_lru_conv_b, lru_wa=v_lru_wa, lru_ba=v_lru_ba,
             lru_wx=v_lru_wx, lru_bx=v_lru_bx, lru_lambda=v_lru_lambda, w_branch=v_w_branch, w_out=v_w_out, ffn_norm=v_ffn_norm,
             w_up=v_w_up, ffn_conv_w=v_ffn_conv_w, ffn_conv_b=v_ffn_conv_b, w_down=v_w_down, final_norm=v_final_norm)

    me = 4 * lax.axis_index("x") + 2 * lax.axis_index("y") + lax.axis_index("c")
    axes = dict(BIG + SMALL_SHARDED)
    conv_names = [k for k, _ in SMALL_SHARDED]

    gathers, tokens, conv_full = {}, [], {}
    for layer in range(DEPTH):
        first = [w["w_in"][layer].astype(bf16)] + ([w[k] for k in conv_names] if layer == 0 else [])
        rest = [w[k][layer].astype(bf16) for k in REST]
        for part, srcs in (("in", first), ("rest", rest)):
            gathers[layer, part], token = exchange_start(srcs, [False] * len(srcs), "gather_%s_start%d" % (part, layer))
            tokens.append(token[0:1, 0:1])
    all_started = functools.reduce(lambda a, b: a + b, tokens)

    def join(land, axis):
        if axis == 0:
            return land.reshape((N_DEV * land.shape[1],) + land.shape[2:])
        return jnp.concatenate([land[p] for p in range(N_DEV)], axis=axis)

    def split(g, axis):
        size = g.shape[axis] // N_DEV
        if axis == 0:
            return g.reshape((N_DEV, size) + g.shape[1:])
        return jnp.stack([lax.slice_in_dim(g, p * size, (p + 1) * size, axis=axis) for p in range(N_DEV)])

    def layer_weights(layer, x_in):
        lands = exchange_wait(gathers[layer, "in"], x_in, "gather_in_wait%d" % layer)
        lw = {"w_in": join(lands[0], 1)}
        if layer == 0:
            conv_full.update({k: join(lands[1 + i], axes[k]) for i, k in enumerate(conv_names)})
        lw.update({k: conv_full[k][layer] for k in conv_names})
        lw.update({k: w[k][layer] for k in REPLICATED if k != "final_norm"})
        if layer == 0:
            lw["attn_norm"] = lw["attn_norm"] + all_started[0]

        def fetch_rest(after):
            lands_r = exchange_wait(gathers[layer, "rest"], after, "gather_rest_wait%d" % layer)
            return {k: join(lands_r[i], axes[k] - 1) for i, k in enumerate(REST)}

        return lw, fetch_rest

    small_names = conv_names + [k for k in REPLICATED if k != "final_norm"]
    groups = {"ffn": ("w_up", "w_down"), "mix": ("w_in", "w_branch", "w_out")}
    scatters, small_shapes, head = {}, {}, {}

    def on_grads(layer, group, gr):
        if group == "small":
            small_shapes.update({k: gr[k].shape for k in small_names})
            srcs = [_pack([gr[k] for k in small_names], f32)]
            srcs += [_pack([head["loss"].reshape(1), head["d_final"]], f32)] if layer == DEPTH - 1 else []
            modes = [False] * len(srcs)
        else:
            srcs = [split(gr[k], axes[k] - 1).astype(bf16) for k in groups[group]]
            modes = [True] * len(srcs)
        scatters[layer, group], token = exchange_start(srcs, modes, "scatter_%s_start%d" % (group, layer))
        return token

    grad_x = forward_backward(x, positions, loss_target, layer_weights, final_norm,
                              lambda loss_part, d_final: head.update(loss=loss_part, d_final=d_final), on_grads)

    big_sums, small_sums = {}, {}
    for group in ("ffn", "mix"):
        for layer in reversed(range(DEPTH)):
            lands = exchange_wait(scatters[layer, group], grad_x, "scatter_%s_wait%d" % (group, layer))
            for i, k in enumerate(groups[group]):
                shard = w[k].shape[1:]
                big_sums[layer, k] = sum_slots(lands[i].reshape(N_DEV, -1, shard[-1]), "sum_" + k).reshape(shard)
    for layer in reversed(range(DEPTH)):
        lands = exchange_wait(scatters[layer, "small"], grad_x, "scatter_small_wait%d" % layer)
        small_sums[layer] = sum_slots(lands[0], "sum_small")
        if layer == DEPTH - 1:
            head_sum = _unpack(sum_slots(lands[1], "sum_head"), [(1,), final_norm.shape])
    grads = {k: jnp.stack([big_sums[layer, k] for layer in range(DEPTH)]) for k, _ in BIG}
    loss, grads["final_norm"] = head_sum[0][0], head_sum[1]
    small_flat = jnp.stack([small_sums[layer] for layer in range(DEPTH)]).reshape(DEPTH, -1)
    pos = 0
    for k in small_names:
        size = math.prod(small_shapes[k])
        g = lax.slice_in_dim(small_flat, pos, pos + size, axis=1).reshape((DEPTH,) + small_shapes[k])
        pos += size
        ax = dict(SMALL_SHARDED).get(k)
        if ax is None:
            grads[k] = g
        else:
            size = g.shape[ax] // N_DEV
            grads[k] = lax.dynamic_slice_in_dim(g, me * size, size, axis=ax)

    upd = {k: adamw(w[k], grads[k], m[k], v[k], "adamw_" + k) for k in WEIGHTS}
    return (loss, grad_x, *[grads[k] for k in WEIGHTS], *[upd[k][0] for k in WEIGHTS], *[upd[k][1] for k in WEIGHTS],
            *[upd[k][2] for k in WEIGHTS])
```

```python
import functools
import math

import jax
import jax.numpy as jnp
from jax import lax
from jax.experimental import pallas as pl
from jax.experimental.pallas import tpu as pltpu

f32 = jnp.float32
bf16 = jnp.bfloat16

D_MODEL = 1024
DEPTH = 4
CHUNK = 64
EPS = 1e-6
DN_HEADS, DN_DK = 4, 128
RET_HEADS, RET_DK, RET_DV = 4, 64, 128
ROPE_BASE = 10000.0
LRU_C = 8.0
D_FF = 2816
N_DEV = 8
LANE = 128
VMEM_LIMIT = 56 * 1024 * 1024

ADAM_LR, ADAM_B1, ADAM_B2, ADAM_EPS, ADAM_WD, ADAM_STEP = 0.001, 0.9, 0.999, 1e-8, 0.01, 10

U_GATES, U_QKV, U_RV, U_RG, U_Z, U_CX, U_CG, U_RQ, U_RK, U_AB = (
    0, 3072, 4608, 5120, 5632, 6144, 6656, 7168, 7424, 7680)
U_PAD = 8192
_IN_SEGS = ((0, 1536, U_QKV), (1536, 8, U_AB), (1544, 512, U_Z), (2056, 256, U_RQ), (2312, 256, U_RK),
            (2568, 512, U_RV), (3080, 512, U_RG), (3592, 512, U_CX), (4104, 512, U_CG), (4616, 3072, U_GATES))
N_IN = 7688


def _params():
    return pltpu.CompilerParams(vmem_limit_bytes=VMEM_LIMIT)


def _pick(dim, pref):
    best = None
    for d in range(LANE, min(dim, pref) + 1, LANE):
        if dim % d == 0:
            best = d
    return best if best is not None else dim


@functools.partial(jax.custom_vjp, nondiff_argnums=(1, 2))
def sroll(x, shift, axis):
    return pltpu.roll(x, shift, axis)


def _sroll_fwd(x, shift, axis):
    return pltpu.roll(x, shift, axis), None


def _sroll_bwd(shift, axis, _, g):
    n = g.shape[axis]
    return (pltpu.roll(g, (n - shift) % n, axis),)


sroll.defvjp(_sroll_fwd, _sroll_bwd)

_DIMS = {"nn": (((1,), (0,)), ((), ())), "nt": (((1,), (1,)), ((), ())), "tn": (((0,), (0,)), ((), ()))}


def _dg(a, b, dims):
    return lax.dot_general(a.astype(bf16), b.astype(bf16), _DIMS[dims], preferred_element_type=f32)


@functools.partial(jax.custom_vjp, nondiff_argnums=(2,))
def bdot(a, b, dims):
    return _dg(a, b, dims)


def _bdot_fwd(a, b, dims):
    return _dg(a, b, dims), (a.astype(bf16), b.astype(bf16))


def _bdot_bwd(dims, res, g):
    a, b = res
    if dims == "nn":
        return _dg(g, b, "nt"), _dg(a, g, "tn")
    if dims == "nt":
        return _dg(g, b, "nn"), _dg(g, a, "tn")
    return _dg(b, g, "nt"), _dg(a, g, "nn")


bdot.defvjp(_bdot_fwd, _bdot_bwd)


def _fdot(a, b, dims):
    return lax.dot_general(a, b, _DIMS[dims], precision=lax.Precision.HIGH, preferred_element_type=f32)


@jax.custom_vjp
def unit_lower_inv_all(mats):
    shape = mats[0].shape
    row = lax.broadcasted_iota(jnp.int32, shape, 0)
    col = lax.broadcasted_iota(jnp.int32, shape, 1)
    eye = jnp.where(row == col, 1.0, 0.0).astype(f32)
    n = [-a for a in mats]
    p = [eye + x for x in n]
    span = 2
    while span < shape[0]:
        n = [_fdot(x, x, "nn") for x in n]
        p = [y + _fdot(y, x, "nn") for y, x in zip(p, n)]
        span *= 2
    return p


def _uli_fwd(mats):
    x = unit_lower_inv_all(mats)
    return x, x


def _uli_bwd(xs, gs):
    t = [_fdot(x, g, "tn") for x, g in zip(xs, gs)]
    return ([-_fdot(y, x, "nt") for y, x in zip(t, xs)],)


unit_lower_inv_all.defvjp(_uli_fwd, _uli_bwd)


@jax.custom_vjp
def known_inverse(invs, mats):
    return invs


def _known_fwd(invs, mats):
    return invs, invs


def _known_bwd(xs, gs):
    return [jnp.zeros_like(x) for x in xs], _uli_bwd(xs, gs)[0]


known_inverse.defvjp(_known_fwd, _known_bwd)


def cumsum_rows(x):
    rows = x.shape[0]
    row = lax.broadcasted_iota(jnp.int32, x.shape, 0)
    s = 1
    while s < rows:
        x = x + jnp.where(row >= s, sroll(x, s, 0), 0.0)
        s *= 2
    return x


def _expm1(x):
    return jnp.tanh(0.5 * x) * (jnp.exp(x) + 1.0)


def _lane_pick(x, lane):
    idx = lax.broadcasted_iota(jnp.int32, x.shape, 1)
    return jnp.sum(jnp.where(idx == lane, x, 0.0), axis=1, keepdims=True)


def _row_pick(x, r):
    idx = lax.broadcasted_iota(jnp.int32, x.shape, 0)
    return jnp.sum(jnp.where(idx == r, x, 0.0), axis=0, keepdims=True)


def _causal_conv(x, halo, w, width):
    xe = jnp.concatenate([halo, x], axis=0)
    acc = xe * w[width - 1:width]
    for k in range(width - 1):
        acc = acc + sroll(xe, width - 1 - k, 0) * w[k:k + 1]
    return acc[8:]


def f_norm(ins, ps):
    (x,), (g,) = ins, ps
    return [x * lax.rsqrt(jnp.mean(x * x, axis=-1, keepdims=True) + EPS) * g]


def f_dn_pre(kind, mains, halos, ps):
    y = _causal_conv(mains[0], halos[0], ps[0], 4)
    y = y * jax.nn.sigmoid(y)
    if kind < 2:
        y = y * lax.rsqrt(jnp.sum(y * y, axis=-1, keepdims=True) + EPS)
    if kind == 0:
        y = y * (DN_DK ** -0.5)
    return [y]


def f_dn_gates(ins, ps):
    u, p = ins[0][:, :LANE], ps[0]
    lane = lax.broadcasted_iota(jnp.int32, u.shape, 1)
    g = -jnp.exp(p[0:1]) * jax.nn.softplus(u + p[1:2])
    beta = jax.nn.sigmoid(u)
    return [jnp.where(lane < 4, g, jnp.where(lane < 8, beta, 0.0))]


def per_head(fn):
    def tile_fn(vals, ps):
        heads = [fn([v[:, h * LANE:(h + 1) * LANE] for v in vals], ps) for h in range(vals[0].shape[1] // LANE)]
        return [jnp.concatenate([o[i] for o in heads], axis=1) for i in range(len(heads[0]))]
    return tile_fn


def f_dn_post(ins, ps):
    (o, z), (nw,) = ins, ps
    y = o * lax.rsqrt(jnp.mean(o * o, axis=-1, keepdims=True) + EPS) * nw
    return [y * (z * jax.nn.sigmoid(z))]


def _rot_half(t):
    lane = lax.broadcasted_iota(jnp.int32, t.shape, 1)
    width = t.shape[1]
    first = (lane % RET_DK) < (RET_DK // 2)
    return jnp.where(first, -sroll(t, width - RET_DK // 2, 1), sroll(t, RET_DK // 2, 1))


def f_ret_pre(ins, ps):
    q, k, cos, sin = ins
    qr = q * cos + _rot_half(q) * sin
    kr = (k * cos + _rot_half(k) * sin) * (RET_DK ** -0.5)
    return [qr, kr]


def f_ret_post(ins, ps):
    o, g = ins
    mu = jnp.mean(o, axis=-1, keepdims=True)
    var = jnp.mean(jnp.square(o - mu), axis=-1, keepdims=True)
    return [(o - mu) * lax.rsqrt(var + EPS) * (g * jax.nn.sigmoid(g))]


def f_lru_pre(mains, halos, ps):
    cw, cb, wa, ba, wx, bx, lam = ps
    xc = _causal_conv(mains[0], halos[0], cw, 4) + cb
    r = jax.nn.sigmoid(bdot(xc, wa, "nn") + ba)
    i = jax.nn.sigmoid(bdot(xc, wx, "nn") + bx)
    log_a = -LRU_C * r * jax.nn.softplus(-lam)
    a = jnp.exp(log_a)
    b = jnp.sqrt(-_expm1(2.0 * log_a)) * (i * xc)
    return [a, b]


def f_lru_post(ins, ps):
    h, g = ins
    return [h * jax.nn.gelu(g)]


def f_ffn_mid(mains, halos, ps):
    cwg, cwv, cbg, cbv = ps
    gate = _causal_conv(mains[0], halos[0], cwg, 3) + cbg
    val = _causal_conv(mains[1], halos[1], cwv, 3) + cbv
    return [gate * jax.nn.sigmoid(gate) * val]


def mm(a, b, dims, name, add=None, dep=None, tm=1536, tn=1536, tk=2816):
    if dims == "tn":
        kdim, m = a.shape
        n = b.shape[1]
    else:
        m, kdim = a.shape
        n = b.shape[0] if dims == "nt" else b.shape[1]
    tm, tn, tk = _pick(m, tm), _pick(n, tn), _pick(kdim, tk)
    nk = kdim // tk
    a_spec = pl.BlockSpec((tk, tm), lambda i, j, k: (k, i)) if dims == "tn" else pl.BlockSpec((tm, tk), lambda i, j, k: (i, k))
    b_spec = pl.BlockSpec((tn, tk), lambda i, j, k: (j, k)) if dims == "nt" else pl.BlockSpec((tk, tn), lambda i, j, k: (k, j))
    o_spec = pl.BlockSpec((tm, tn), lambda i, j, k: (i, j))
    has_add, has_dep = add is not None, dep is not None

    def body(*refs):
        a_ref, b_ref = refs[:2]
        add_ref = refs[2] if has_add else None
        o_ref = refs[2 + has_add + has_dep]
        if nk == 1:
            prod = _dg(a_ref[...], b_ref[...], dims)
            o_ref[...] = prod + add_ref[...] if has_add else prod
            return
        acc_ref = refs[-1]
        k = pl.program_id(2)

        @pl.when(k == 0)
        def _():
            acc_ref[...] = jnp.zeros_like(acc_ref)

        acc_ref[...] += _dg(a_ref[...], b_ref[...], dims)

        @pl.when(k == nk - 1)
        def _():
            o_ref[...] = acc_ref[...] + add_ref[...] if has_add else acc_ref[...]

    args = [a, b] + ([add] if has_add else []) + ([dep] if has_dep else [])
    in_specs = [a_spec, b_spec] + ([o_spec] if has_add else [])
    in_specs += [pl.BlockSpec((8, LANE), lambda i, j, k: (0, 0))] if has_dep else []
    return pl.pallas_call(
        body, name=name, grid=(m // tm, n // tn, nk), in_specs=in_specs, out_specs=o_spec,
        out_shape=jax.ShapeDtypeStruct((m, n), f32), scratch_shapes=[pltpu.VMEM((tm, tn), f32)] if nk > 1 else [],
        compiler_params=_params())(*args)


def rowmap(fn, ins, params, outs, ncol, name, rows=512):
    n = ins[0][0].shape[0]
    r = min(rows, n)
    nin, npar = len(ins), len(params)

    def body(*refs):
        vals = [x[...] for x in refs[:nin]]
        pv = [p[...] for p in refs[nin:nin + npar]]
        for o_ref, o in zip(refs[nin + npar:], fn(vals, pv)):
            o_ref[...] = o.astype(o_ref.dtype)

    in_specs = [pl.BlockSpec((r, cb), functools.partial(lambda j, i, off: (i, off + j), off=off)) for _, cb, off in ins]
    in_specs += [pl.BlockSpec(bs, functools.partial(lambda j, i, f: f(j), f=f)) for _, bs, f in params]
    out_specs = [pl.BlockSpec((r, cb), lambda j, i: (i, j)) for cb, _ in outs]
    out_shape = [jax.ShapeDtypeStruct((n, cb * ncol), dt) for cb, dt in outs]
    res = pl.pallas_call(body, name=name, grid=(ncol, n // r), in_specs=in_specs, out_specs=out_specs,
                         out_shape=out_shape, compiler_params=_params())(*[a for a, _, _ in ins], *[a for a, _, _ in params])
    return res


def rowmap_bwd(fn, ins, params, douts, ncol, name, rows=512, add=None, din_dtypes=None, into=None, copy16=None):
    n = ins[0][0].shape[0]
    r = min(rows, n)
    nin, npar, nout = len(ins), len(params), len(douts)
    add = [None] * nin if add is None else list(add)
    add_idx = [i for i in range(nin) if add[i] is not None]
    din_dtypes = [f32] * nin if din_dtypes is None else list(din_dtypes)
    into_buf, into_off, into_idx = into if into is not None else (None, 0, [])
    has_into, has_copy = into is not None, copy16 is not None
    kept = [i for i in range(nin) if din_dtypes[i] is not None and i not in into_idx]

    def body(*refs):
        vals = [x[...] for x in refs[:nin]]
        pv = [p[...] for p in refs[nin:nin + npar]]
        dys = [d[...] for d in refs[nin + npar:nin + npar + nout]]
        k0 = nin + npar + nout
        add_refs = dict(zip(add_idx, refs[k0:k0 + len(add_idx)]))
        k0 += len(add_idx) + has_into
        din_refs = refs[k0:k0 + len(kept)]
        k0 += len(kept)
        copy_ref = refs[k0] if has_copy else None
        into_ref = refs[k0 + has_copy] if has_into else None
        dp_refs = refs[k0 + has_copy + has_into:]
        _, vjp = jax.vjp(fn, vals, pv)
        dvals, dpv = vjp(dys)
        cot = lambda idx: dvals[idx] + add_refs[idx][...] if idx in add_refs else dvals[idx]
        for d_ref, idx in zip(din_refs, kept):
            d_ref[...] = cot(idx).astype(d_ref.dtype)
        if has_copy:
            copy_ref[...] = cot(copy16).astype(copy_ref.dtype)
        if has_into:
            parts = [cot(idx) for idx in into_idx]
            into_ref[...] = (parts[0] if len(parts) == 1 else jnp.concatenate(parts, axis=1)).astype(into_ref.dtype)

        @pl.when(pl.program_id(1) == 0)
        def _():
            for d_ref in dp_refs:
                d_ref[...] = jnp.zeros_like(d_ref)

        for d_ref, d in zip(dp_refs, dpv):
            d_ref[...] += d

    in_specs = [pl.BlockSpec((r, cb), functools.partial(lambda j, i, off: (i, off + j), off=off)) for _, cb, off in ins]
    in_specs += [pl.BlockSpec(bs, functools.partial(lambda j, i, f: f(j), f=f)) for _, bs, f in params]
    in_specs += [pl.BlockSpec((r, d.shape[1] // ncol), lambda j, i: (i, j)) for d in douts]
    in_specs += [pl.BlockSpec((r, ins[i][1]), lambda j, i: (i, j)) for i in add_idx]
    out_specs = [pl.BlockSpec((r, ins[i][1]), lambda j, i: (i, j)) for i in kept]
    out_shape = [jax.ShapeDtypeStruct((n, ins[i][1] * ncol), din_dtypes[i]) for i in kept]
    args = [a for a, _, _ in ins] + [a for a, _, _ in params] + list(douts) + [add[i] for i in add_idx]
    aliases = {}
    if has_copy:
        out_specs += [pl.BlockSpec((r, ins[copy16][1]), lambda j, i: (i, j))]
        out_shape += [jax.ShapeDtypeStruct((n, ins[copy16][1] * ncol), bf16)]
    if has_into:
        assert ncol == 1
        in_specs += [pl.BlockSpec(memory_space=pl.ANY)]
        aliases[len(args)] = len(out_shape)
        args += [into_buf]
        out_specs += [pl.BlockSpec((r, sum(ins[i][1] for i in into_idx)), lambda j, i: (i, into_off))]
        out_shape += [jax.ShapeDtypeStruct(into_buf.shape, into_buf.dtype)]
    pshapes = [tuple(d for d in bs if d is not None) for _, bs, _ in params]
    out_specs += [pl.BlockSpec((None,) + ps, functools.partial(lambda j, i, nd: (j,) + (0,) * nd, nd=len(ps))) for ps in pshapes]
    out_shape += [jax.ShapeDtypeStruct((ncol,) + ps, f32) for ps in pshapes]
    res = pl.pallas_call(body, name=name, grid=(ncol, n // r), in_specs=in_specs, out_specs=out_specs, out_shape=out_shape,
                         input_output_aliases=aliases, compiler_params=_params())(*args)
    dins = [None] * nin
    for pos, i in enumerate(kept):
        dins[i] = res[pos]
    pos = len(kept)
    extras = {}
    if has_copy:
        extras["copy16"] = res[pos]
        pos += 1
    if has_into:
        extras["into"] = res[pos]
        pos += 1
    return dins, res[pos:], extras


SEQ_ROWS = 2048


def seqmap(fn, ins, params, nouts, ncol, name, out_dtype=f32):
    bsz, seq, _ = ins[0][0].shape
    r = min(SEQ_ROWS, seq)
    nin, npar = len(ins), len(params)

    def body(*refs):
        in_refs = refs[:nin]
        pv = [p[...] for p in refs[nin:nin + npar]]
        out_refs = refs[nin + npar:]

        def step(i, carry):
            r0 = pl.multiple_of(i * r, r)
            h0 = pl.multiple_of(jnp.maximum(r0 - 8, 0), 8)
            mains = [x[pl.ds(r0, r), :] for x in in_refs]
            halos = [jnp.where(i == 0, 0.0, x[pl.ds(h0, 8), :]) for x in in_refs]
            for o_ref, o in zip(out_refs, fn(mains, halos, pv)):
                o_ref[pl.ds(r0, r), :] = o.astype(o_ref.dtype)
            return carry

        lax.fori_loop(0, seq // r, step, 0)

    in_specs = [pl.BlockSpec((None, seq, LANE), functools.partial(lambda j, b, off: (b, 0, off + j), off=off)) for _, off in ins]
    in_specs += [pl.BlockSpec(bs, functools.partial(lambda j, b, f: f(j), f=f)) for _, bs, f in params]
    out_specs = [pl.BlockSpec((None, seq, LANE), lambda j, b: (b, 0, j)) for _ in range(nouts)]
    out_shape = [jax.ShapeDtypeStruct((bsz, seq, LANE * ncol), out_dtype) for _ in range(nouts)]
    return pl.pallas_call(body, name=name, grid=(ncol, bsz), in_specs=in_specs, out_specs=out_specs,
                          out_shape=out_shape, compiler_params=_params())(*[a for a, _ in ins], *[a for a, _, _ in params])


def seqmap_bwd(fn, ins, params, douts, ncol, name, din_dtype=f32, into=None):
    bsz, seq, _ = ins[0][0].shape
    r = min(SEQ_ROWS, seq)
    nin, npar, nout = len(ins), len(params), len(douts)
    narrow = din_dtype != f32

    def body(*refs):
        in_refs = refs[:nin]
        pv = [p[...] for p in refs[nin:nin + npar]]
        dy_refs = refs[nin + npar:nin + npar + nout]
        k0 = nin + npar + nout + (into is not None)
        dout_refs = refs[k0:k0 + nin]
        dp_refs = refs[k0 + nin:k0 + nin + npar]
        din_refs = refs[k0 + nin + npar:] if narrow else dout_refs

        def step(i, dp_acc):
            r0 = pl.multiple_of(i * r, r)
            h0 = pl.multiple_of(jnp.maximum(r0 - 8, 0), 8)
            mains = [x[pl.ds(r0, r), :] for x in in_refs]
            halos_raw = [x[pl.ds(h0, 8), :] for x in in_refs]

            def tile(mains, halos_raw, pv):
                return fn(mains, [jnp.where(i == 0, 0.0, h) for h in halos_raw], pv)

            _, vjp = jax.vjp(tile, mains, halos_raw, pv)
            dm, dh, dp = vjp([d[pl.ds(r0, r), :] for d in dy_refs])
            for d_ref, m, h in zip(din_refs, dm, dh):
                d_ref[pl.ds(r0, r), :] = m
                d_ref[pl.ds(h0, 8), :] += h
            return [acc + d for acc, d in zip(dp_acc, dp)]

        dp = lax.fori_loop(0, seq // r, step, [jnp.zeros(p.shape, f32) for p in pv])
        if narrow:
            for o_ref, d_ref in zip(dout_refs, din_refs):
                o_ref[...] = d_ref[...].astype(o_ref.dtype)

        @pl.when(pl.program_id(1) == 0)
        def _():
            for d_ref in dp_refs:
                d_ref[...] = jnp.zeros_like(d_ref)

        for d_ref, d in zip(dp_refs, dp):
            d_ref[...] += d

    in_specs = [pl.BlockSpec((None, seq, LANE), functools.partial(lambda j, b, off: (b, 0, off + j), off=off)) for _, off in ins]
    in_specs += [pl.BlockSpec(bs, functools.partial(lambda j, b, f: f(j), f=f)) for _, bs, f in params]
    in_specs += [pl.BlockSpec((None, seq, LANE), lambda j, b: (b, 0, j)) for _ in range(nout)]
    out_specs = [pl.BlockSpec((None, seq, LANE), lambda j, b: (b, 0, j)) for _ in range(nin)]
    pshapes = [tuple(d for d in bs if d is not None) for _, bs, _ in params]
    out_specs += [pl.BlockSpec((None,) + ps, functools.partial(lambda j, b, nd: (j,) + (0,) * nd, nd=len(ps))) for ps in pshapes]
    out_shape = [jax.ShapeDtypeStruct((bsz, seq, LANE * ncol), din_dtype) for _ in range(nin)]
    out_shape += [jax.ShapeDtypeStruct((ncol,) + ps, f32) for ps in pshapes]
    args = [a for a, _ in ins] + [a for a, _, _ in params] + list(douts)
    aliases = {}
    if into is not None:
        assert nin == 1 and into[0].dtype == din_dtype
        in_specs += [pl.BlockSpec(memory_space=pl.ANY)]
        aliases[len(args)] = 0
        args += [into[0]]
        out_specs[0] = pl.BlockSpec((None, seq, LANE), lambda j, b: (b, 0, into[1] + j))
        out_shape[0] = jax.ShapeDtypeStruct(into[0].shape, din_dtype)
    res = pl.pallas_call(body, name=name, grid=(ncol, bsz), in_specs=in_specs, out_specs=out_specs, out_shape=out_shape,
                         scratch_shapes=[pltpu.VMEM((seq, LANE), f32) for _ in range(nin)] if narrow else [],
                         input_output_aliases=aliases, compiler_params=_params())(*args)
    return res[:nin], res[nin:]


def _tri_masks():
    row = lax.broadcasted_iota(jnp.int32, (CHUNK, CHUNK), 0)
    col = lax.broadcasted_iota(jnp.int32, (CHUNK, CHUNK), 1)
    return row >= col, row > col


CHUNKS_PER_STEP = 4


def _by_rows(parts, per_row):
    rows = [jnp.concatenate(parts[i:i + per_row], axis=1) for i in range(0, len(parts), per_row)]
    return jnp.concatenate(rows, axis=0)


def dn_prep(vals, ps):
    q, k, v, gb = vals[:4]
    nchunk = q.shape[0] // CHUNK
    causal, strict = _tri_masks()
    gbs = [gb[c * CHUNK:(c + 1) * CHUNK] for c in range(nchunk)]
    gcs = [cumsum_rows(g) for g in gbs]
    gcts = [g.T for g in gcs]
    chains = [(c, h) for c in range(nchunk) for h in range(DN_HEADS)]
    part = lambda t, c, h: t[c * CHUNK:(c + 1) * CHUNK, h * DN_DK:(h + 1) * DN_DK]
    qh = [part(q, c, h) for c, h in chains]
    kh = [part(k, c, h) for c, h in chains]
    vh = [part(v, c, h) for c, h in chains]
    g_col = [_lane_pick(gcs[c], h) for c, h in chains]
    beta = [_lane_pick(gbs[c], DN_HEADS + h) for c, h in chains]
    g_row = [_row_pick(gcts[c], h)[:, :CHUNK] for c, h in chains]
    decay = [jnp.where(causal, jnp.exp(jnp.where(causal, gc - gr, 0.0)), 0.0) for gc, gr in zip(g_col, g_row)]
    k_beta = [a * b for a, b in zip(kh, beta)]
    eg = [jnp.exp(g) for g in g_col]
    kk = [bdot(a, b, "nt") for a, b in zip(k_beta, kh)]
    qk = [bdot(a, b, "nt") for a, b in zip(qh, kh)]
    lower = [jnp.where(strict, a * d, 0.0) for a, d in zip(kk, decay)]
    if len(vals) == 5:
        t_inv = known_inverse([part(vals[4], c, h)[:, :CHUNK] for c, h in chains], lower)
    else:
        t_inv = unit_lower_inv_all(lower)
    u = [bdot(t, a * b, "nn") for t, a, b in zip(t_inv, vh, beta)]
    w = [bdot(t, a * e, "nn") for t, a, e in zip(t_inv, k_beta, eg)]
    attn = [jnp.concatenate([a * d, jnp.zeros((CHUNK, DN_DK - CHUNK), f32)], axis=1) for a, d in zip(qk, decay)]
    qd = [a * e for a, e in zip(qh, eg)]
    kd = [a * jnp.exp(_row_pick(g, CHUNK - 1) - g) for a, g in zip(kh, g_col)]
    g_last = jnp.concatenate([jnp.broadcast_to(_row_pick(g, CHUNK - 1), g.shape) for g in gcs], axis=0)
    outs = [_by_rows(t, DN_HEADS) for t in (u, w, attn, qd, kd)] + [g_last]
    if len(vals) == 4:
        wide = [jnp.concatenate([t, jnp.zeros((CHUNK, DN_DK - CHUNK), f32)], axis=1) for t in t_inv]
        outs.append(_by_rows(wide, DN_HEADS))
    return outs


def dn_step(state, u, w, attn, qd, kd, g_last):
    bsz = u.shape[0]
    chains = [(b, h) for b in range(bsz) for h in range(DN_HEADS)]
    part = lambda t, b, h: t[b, :, h * DN_DK:(h + 1) * DN_DK]
    ws = [bdot(part(w, b, h), s, "nn") for (b, h), s in zip(chains, state)]
    qs = [bdot(part(qd, b, h), s, "nn") for (b, h), s in zip(chains, state)]
    v_new = [part(u, b, h) - x for (b, h), x in zip(chains, ws)]
    av = [bdot(attn[b, :, h * DN_DK:h * DN_DK + CHUNK], x, "nn") for (b, h), x in zip(chains, v_new)]
    kv = [bdot(part(kd, b, h), x, "tn") for (b, h), x in zip(chains, v_new)]
    ge = [jnp.exp(_row_pick(_lane_pick(g_last[b], h), 0)) for b, h in chains]
    new_state = [s * g + x for s, g, x in zip(state, ge, kv)]
    outs = [a + b for a, b in zip(qs, av)]
    return new_state, jnp.concatenate([jnp.concatenate(outs[b * DN_HEADS:(b + 1) * DN_HEADS], axis=1)[None]
                                       for b in range(bsz)], axis=0)


def _ret_log_gamma(h):
    return math.log(1.0 - 2.0 ** (-5.0 - h))


def ret_prep(vals, ps):
    q, k, v = vals
    nchunk = q.shape[0] // CHUNK
    causal, _ = _tri_masks()
    row = lax.broadcasted_iota(jnp.int32, (CHUNK, CHUNK), 0)
    col = lax.broadcasted_iota(jnp.int32, (CHUNK, CHUNK), 1)
    dist = (row - col).astype(f32)
    lane = lax.broadcasted_iota(jnp.int32, (CHUNK, q.shape[1]), 1)
    dmask = [jnp.where(causal, jnp.exp(jnp.where(causal, dist, 0.0) * _ret_log_gamma(h)), 0.0) for h in range(RET_HEADS)]
    chains = [(c, h) for c in range(nchunk) for h in range(RET_HEADS)]
    rows = lambda t, c: t[c * CHUNK:(c + 1) * CHUNK]
    scores = [bdot(jnp.where((lane // RET_DK) == h, rows(q, c), 0.0), rows(k, c), "nt") * dmask[h] for c, h in chains]
    inner = [bdot(s, rows(v, c)[:, h * RET_DV:(h + 1) * RET_DV], "nn") for s, (c, h) in zip(scores, chains)]
    return [_by_rows(inner, RET_HEADS)]


def ret_step(state, q, k, v, inner):
    bsz = q.shape[0]
    idx = lax.broadcasted_iota(jnp.int32, (CHUNK, 1), 0).astype(f32)
    lane = lax.broadcasted_iota(jnp.int32, (CHUNK, q.shape[2]), 1)
    chains = [(b, h) for b in range(bsz) for h in range(RET_HEADS)]
    part = lambda t, b, h: t[b, :, h * RET_DV:(h + 1) * RET_DV]
    cross = [bdot(q[b], s, "nn") for (b, h), s in zip(chains, state)]
    kz = [jnp.where((lane // RET_DK) == h, k[b], 0.0) * jnp.exp((CHUNK - 1.0 - idx) * _ret_log_gamma(h)) for b, h in chains]
    kv = [bdot(a, part(v, b, h), "tn") for a, (b, h) in zip(kz, chains)]
    outs = [x * jnp.exp((idx + 1.0) * _ret_log_gamma(h)) + part(inner, b, h) for x, (b, h) in zip(cross, chains)]
    new_state = [s * math.exp(CHUNK * _ret_log_gamma(h)) + x for s, x, (b, h) in zip(state, kv, chains)]
    return new_state, jnp.concatenate([jnp.concatenate(outs[b * RET_HEADS:(b + 1) * RET_HEADS], axis=1)[None]
                                       for b in range(bsz)], axis=0)


SCAN_CHUNKS = 2


def chunk_scan(step_fn, ins, state_shape, out_width, name):
    bsz, seq, _ = ins[0].shape
    nchunk = seq // CHUNK
    nin = len(ins)
    nh = state_shape[0]
    per = SCAN_CHUNKS if nchunk % SCAN_CHUNKS == 0 else 1

    def body(*refs):
        in_refs = refs[:nin]
        o_ref, ck_ref, s_ref = refs[nin:]

        @pl.when(pl.program_id(0) == 0)
        def _():
            s_ref[...] = jnp.zeros_like(s_ref)

        state = [s_ref[i] for i in range(bsz * nh)]
        for c in range(per):
            rows = slice(c * CHUNK, (c + 1) * CHUNK)
            for i in range(bsz * nh):
                ck_ref[i // nh, c, i % nh] = state[i]
            state, out = step_fn(state, *[x[:, rows, :].astype(f32) for x in in_refs])
            o_ref[:, rows, :] = out
        for i in range(bsz * nh):
            s_ref[i] = state[i]

    in_specs = [pl.BlockSpec((bsz, per * CHUNK, x.shape[2]), lambda n: (0, n, 0)) for x in ins]
    out_specs = [pl.BlockSpec((bsz, per * CHUNK, out_width), lambda n: (0, n, 0)),
                 pl.BlockSpec((bsz, per) + tuple(state_shape), lambda n: (0, n, 0, 0, 0))]
    out_shape = [jax.ShapeDtypeStruct((bsz, seq, out_width), f32),
                 jax.ShapeDtypeStruct((bsz, nchunk) + tuple(state_shape), f32)]
    return pl.pallas_call(body, name=name, grid=(nchunk // per,), in_specs=in_specs, out_specs=out_specs, out_shape=out_shape,
                          scratch_shapes=[pltpu.VMEM((bsz * nh,) + tuple(state_shape[1:]), f32)],
                          compiler_params=_params())(*ins)


def chunk_scan_bwd(step_fn, ins, ckpt, dout, name):
    bsz, seq, _ = ins[0].shape
    nchunk = seq // CHUNK
    nin = len(ins)
    state_shape = ckpt.shape[2:]
    nh = state_shape[0]
    per = SCAN_CHUNKS if nchunk % SCAN_CHUNKS == 0 else 1
    nstep = nchunk // per

    def body(*refs):
        in_refs = refs[:nin]
        ck_ref, do_ref = refs[nin:nin + 2]
        din_refs = refs[nin + 2:nin + 2 + nin]
        ds_ref = refs[-1]

        @pl.when(pl.program_id(0) == 0)
        def _():
            ds_ref[...] = jnp.zeros_like(ds_ref)

        dstate = [ds_ref[i] for i in range(bsz * nh)]
        for c in reversed(range(per)):
            rows = slice(c * CHUNK, (c + 1) * CHUNK)
            state = [ck_ref[i // nh, c, i % nh] for i in range(bsz * nh)]
            _, vjp = jax.vjp(step_fn, state, *[x[:, rows, :].astype(f32) for x in in_refs])
            grads = vjp((dstate, do_ref[:, rows, :]))
            dstate = grads[0]
            for d_ref, d in zip(din_refs, grads[1:]):
                d_ref[:, rows, :] = d
        for i in range(bsz * nh):
            ds_ref[i] = dstate[i]

    rev = lambda n: (0, nstep - 1 - n, 0)
    in_specs = [pl.BlockSpec((bsz, per * CHUNK, x.shape[2]), rev) for x in ins]
    in_specs += [pl.BlockSpec((bsz, per) + tuple(state_shape), lambda n: (0, nstep - 1 - n, 0, 0, 0)),
                 pl.BlockSpec((bsz, per * CHUNK, dout.shape[2]), rev)]
    out_specs = [pl.BlockSpec((bsz, per * CHUNK, x.shape[2]), rev) for x in ins]
    out_shape = [jax.ShapeDtypeStruct(x.shape, f32) for x in ins]
    return pl.pallas_call(body, name=name, grid=(nstep,), in_specs=in_specs, out_specs=out_specs, out_shape=out_shape,
                          scratch_shapes=[pltpu.VMEM((bsz * nh,) + tuple(state_shape[1:]), f32)],
                          compiler_params=_params())(*ins, ckpt, dout)


LRU_ROWS = 512


def lru_scan(a, b):
    bsz, seq, width = a.shape
    rb = min(LRU_ROWS, seq)

    def body(a_ref, b_ref, h_ref, hp_ref, carry_ref):
        @pl.when(pl.program_id(1) == 0)
        def _():
            carry_ref[...] = jnp.zeros_like(carry_ref)

        row = lax.broadcasted_iota(jnp.int32, (8, width), 0)

        def tile(t, hprev):
            r0 = pl.multiple_of(t * 8, 8)
            ca, cbv = a_ref[pl.ds(r0, 8), :], b_ref[pl.ds(r0, 8), :]
            for s in (1, 2, 4):
                m = row >= s
                cbv = jnp.where(m, ca * pltpu.roll(cbv, s, 0) + cbv, cbv)
                ca = jnp.where(m, ca * pltpu.roll(ca, s, 0), ca)
            h = cbv + ca * hprev
            h_ref[pl.ds(r0, 8), :] = h
            hp_ref[pl.ds(r0, 8), :] = jnp.where(row == 0, hprev, pltpu.roll(h, 1, 0))
            return _row_pick(h, 7)

        carry_ref[0:1, :] = lax.fori_loop(0, rb // 8, tile, carry_ref[0:1, :])

    spec = pl.BlockSpec((None, rb, width), lambda bi, i: (bi, i, 0))
    return pl.pallas_call(body, name="lru_scan", grid=(bsz, seq // rb), in_specs=[spec, spec], out_specs=[spec, spec],
                          out_shape=[jax.ShapeDtypeStruct(a.shape, f32)] * 2,
                          scratch_shapes=[pltpu.VMEM((8, width), f32)], compiler_params=_params())(a, b)


def lru_scan_bwd(a, hp, dh):
    bsz, seq, width = a.shape
    rb = min(LRU_ROWS, seq)
    nblk = seq // rb

    def body(a_ref, hp_ref, dh_ref, da_ref, db_ref, carry_ref):
        @pl.when(pl.program_id(1) == 0)
        def _():
            carry_ref[...] = jnp.zeros_like(carry_ref)

        row = lax.broadcasted_iota(jnp.int32, (8, width), 0)
        ntile = rb // 8

        def tile(t, mu_next):
            r0 = pl.multiple_of((ntile - 1 - t) * 8, 8)
            ca, dh_t = a_ref[pl.ds(r0, 8), :], dh_ref[pl.ds(r0, 8), :]
            cbv = ca * dh_t
            for s in (1, 2, 4):
                m = row < 8 - s
                cbv = jnp.where(m, ca * pltpu.roll(cbv, 8 - s, 0) + cbv, cbv)
                ca = jnp.where(m, ca * pltpu.roll(ca, 8 - s, 0), ca)
            mu = cbv + ca * mu_next
            lam = dh_t + jnp.where(row == 7, mu_next, pltpu.roll(mu, 7, 0))
            db_ref[pl.ds(r0, 8), :] = lam
            da_ref[pl.ds(r0, 8), :] = lam * hp_ref[pl.ds(r0, 8), :]
            return _row_pick(mu, 0)

        carry_ref[0:1, :] = lax.fori_loop(0, ntile, tile, carry_ref[0:1, :])

    spec = pl.BlockSpec((None, rb, width), lambda bi, i: (bi, nblk - 1 - i, 0))
    return pl.pallas_call(body, name="lru_scan_bwd", grid=(bsz, nblk), in_specs=[spec] * 3, out_specs=[spec, spec],
                          out_shape=[jax.ShapeDtypeStruct(a.shape, f32)] * 2,
                          scratch_shapes=[pltpu.VMEM((8, width), f32)], compiler_params=_params())(a, hp, dh)


MERGE_ROWS = 512


def branch_merge(ys, w_branch, u):
    n = ys[0].shape[0]
    tm = min(MERGE_ROWS, n)

    def body(ya, yb, yc, w_ref, g0, g1, g2, o_ref):
        acc = None
        for i, (y_ref, g_ref) in enumerate(((ya, g0), (yb, g1), (yc, g2))):
            term = jax.nn.sigmoid(g_ref[...]) * _dg(y_ref[...], w_ref[i], "nn")
            acc = term if acc is None else acc + term
        o_ref[...] = acc.astype(o_ref.dtype)

    y_spec = pl.BlockSpec((tm, ys[0].shape[1]), lambda i: (i, 0))
    g_specs = [pl.BlockSpec((tm, D_MODEL), functools.partial(lambda i, k: (i, k), k=k)) for k in range(3)]
    return pl.pallas_call(
        body, name="branch_merge", grid=(n // tm,),
        in_specs=[y_spec] * 3 + [pl.BlockSpec(w_branch.shape, lambda i: (0, 0, 0))] + g_specs,
        out_specs=pl.BlockSpec((tm, D_MODEL), lambda i: (i, 0)), out_shape=jax.ShapeDtypeStruct((n, D_MODEL), bf16),
        compiler_params=_params())(*ys, w_branch, u, u, u)


def branch_merge_bwd(ys, w_branch, u, d_merged, du):
    n = ys[0].shape[0]
    tm = min(MERGE_ROWS, n)

    def body(ya, yb, yc, w_ref, g0, g1, g2, dm_ref, du_in, db0, db1, db2, du_ref):
        dm = dm_ref[...]
        d_gates = []
        for i, (y_ref, g_ref, db_ref) in enumerate(((ya, g0, db0), (yb, g1, db1), (yc, g2, db2))):
            s = jax.nn.sigmoid(g_ref[...])
            db_ref[...] = (dm * s).astype(db_ref.dtype)
            d_gates.append(dm * _dg(y_ref[...], w_ref[i], "nn") * (s * (1.0 - s)))
        du_ref[...] = jnp.concatenate(d_gates, axis=1).astype(du_ref.dtype)

    y_spec = pl.BlockSpec((tm, ys[0].shape[1]), lambda i: (i, 0))
    row = pl.BlockSpec((tm, D_MODEL), lambda i: (i, 0))
    g_specs = [pl.BlockSpec((tm, D_MODEL), functools.partial(lambda i, k: (i, k), k=k)) for k in range(3)]
    res = pl.pallas_call(
        body, name="branch_merge_bwd", grid=(n // tm,),
        in_specs=[y_spec] * 3 + [pl.BlockSpec(w_branch.shape, lambda i: (0, 0, 0))] + g_specs + [row, pl.BlockSpec(memory_space=pl.ANY)],
        out_specs=[row] * 3 + [pl.BlockSpec((tm, 3 * D_MODEL), lambda i: (i, 0))],
        out_shape=[jax.ShapeDtypeStruct((n, D_MODEL), bf16)] * 3 + [jax.ShapeDtypeStruct(du.shape, du.dtype)],
        input_output_aliases={8: 3}, compiler_params=_params())(*ys, w_branch, u, u, u, d_merged, du)
    return list(res[:3]), res[3]


def final_loss(x, g, target):
    n, d = x.shape
    r = min(256, n)

    def body(x_ref, g_ref, t_ref, loss_ref, dx_ref, dg_ref, dx16_ref):
        @pl.when(pl.program_id(0) == 0)
        def _():
            loss_ref[...] = jnp.zeros_like(loss_ref)
            dg_ref[...] = jnp.zeros_like(dg_ref)

        tgt = t_ref[...]

        def loss_fn(xv, gv):
            y = f_norm([xv], [gv])[0]
            return 0.5 * jnp.sum(jnp.mean(jnp.square(y - tgt), axis=-1, keepdims=True), axis=0, keepdims=True)

        val, vjp = jax.vjp(loss_fn, x_ref[...], g_ref[...])
        dx, dg = vjp(jnp.ones_like(val))
        loss_ref[...] += val
        dx_ref[...] = dx
        dx16_ref[...] = dx.astype(dx16_ref.dtype)
        dg_ref[...] += dg

    row = pl.BlockSpec((r, d), lambda i: (i, 0))
    return pl.pallas_call(
        body, name="final_loss", grid=(n // r,), in_specs=[row, pl.BlockSpec((1, d), lambda i: (0, 0)), row],
        out_specs=[pl.BlockSpec((8, LANE), lambda i: (0, 0)), row, pl.BlockSpec((1, d), lambda i: (0, 0)), row],
        out_shape=[jax.ShapeDtypeStruct((8, LANE), f32), jax.ShapeDtypeStruct((n, d), f32), jax.ShapeDtypeStruct((1, d), f32),
                   jax.ShapeDtypeStruct((n, d), bf16)],
        compiler_params=_params())(x, g, target)


_HBM = pl.BlockSpec(memory_space=pltpu.HBM)
_SEM = pl.BlockSpec(memory_space=pltpu.SEMAPHORE)
_EFFECT = pltpu.SideEffectType.DATAFLOW_SIDE_EFFECTING


def _peer(k):
    mx, my, mc = lax.axis_index("x"), lax.axis_index("y"), lax.axis_index("c")
    px, py, pc = (mx + (k >> 2)) % 2, (my + ((k >> 1) & 1)) % 2, (mc + (k & 1)) % 2
    return (px, py, pc), 4 * px + 2 * py + pc


def _peer_copy(k, i, x_ref, land_ref, send_sems, recv_sems, scatter):
    me = 4 * lax.axis_index("x") + 2 * lax.axis_index("y") + lax.axis_index("c")
    dev, slot = _peer(k)
    sem = i * (N_DEV - 1) + k - 1
    return pltpu.make_async_remote_copy(
        src_ref=x_ref.at[slot] if scatter else x_ref, dst_ref=land_ref.at[me], send_sem=send_sems.at[sem],
        recv_sem=recv_sems.at[sem], device_id=dev, device_id_type=pl.DeviceIdType.MESH)


def _own_copy(i, x_ref, land_ref, own_sems, scatter):
    me = 4 * lax.axis_index("x") + 2 * lax.axis_index("y") + lax.axis_index("c")
    return pltpu.make_async_copy(x_ref.at[me] if scatter else x_ref, land_ref.at[me], own_sems.at[i])


def exchange_start(xs, scatters, name):
    nx = len(xs)
    lands = [lax.empty((N_DEV,) + tuple(x.shape[1:] if sc else x.shape), x.dtype) for x, sc in zip(xs, scatters)]
    nsem = nx * (N_DEV - 1)

    def body(*refs):
        x_refs, land_refs = refs[:nx], refs[nx:2 * nx]
        send_sems, recv_sems, own_sems = refs[2 * nx:2 * nx + 3]
        token = refs[-1]
        for i in range(nx):
            for k in range(1, N_DEV):
                _peer_copy(k, i, x_refs[i], land_refs[i], send_sems, recv_sems, scatters[i]).start()
            _own_copy(i, x_refs[i], land_refs[i], own_sems, scatters[i]).start()
        token[...] = jnp.zeros_like(token)

    hbm = lambda a: pltpu.HBM(a.shape, a.dtype)
    res = pl.pallas_call(
        body, name=name, in_specs=(_HBM,) * (2 * nx),
        out_specs=(_SEM, _SEM, _SEM) + (_HBM,) * (2 * nx) + (pl.BlockSpec(memory_space=pltpu.VMEM),),
        input_output_aliases={i: 3 + i for i in range(2 * nx)},
        out_shape=(pltpu.SemaphoreType.DMA((nsem,)), pltpu.SemaphoreType.DMA((nsem,)), pltpu.SemaphoreType.DMA((nx,)),
                   *[hbm(a) for a in xs], *[hbm(a) for a in lands], jax.ShapeDtypeStruct((8, LANE), f32)),
        compiler_params=pltpu.CompilerParams(has_side_effects=_EFFECT),
    )(*[pltpu.with_memory_space_constraint(a, pltpu.HBM) for a in list(xs) + lands])
    return (res[0], res[1], res[2], list(res[3:3 + nx]), list(res[3 + nx:3 + 2 * nx]), tuple(scatters)), res[-1]


def exchange_wait(started, after, name):
    send_sems, recv_sems, own_sems, x_thrus, land_thrus, scatters = started
    nx = len(x_thrus)

    def body(*refs):
        x_refs, land_refs = refs[:nx], refs[nx:2 * nx]
        send_sems, recv_sems, own_sems = refs[2 * nx:2 * nx + 3]
        for i in range(nx):
            for k in range(1, N_DEV):
                cp = _peer_copy(k, i, x_refs[i], land_refs[i], send_sems, recv_sems, scatters[i])
                cp.wait_send()
                cp.wait_recv()
            _own_copy(i, x_refs[i], land_refs[i], own_sems, scatters[i]).wait()

    hbm = lambda a: pltpu.HBM(a.shape, a.dtype)
    res = pl.pallas_call(
        body, name=name, in_specs=(_HBM,) * (2 * nx) + (_SEM, _SEM, _SEM, pl.BlockSpec(memory_space=pl.ANY)),
        out_specs=(_HBM,) * (2 * nx), input_output_aliases={i: i for i in range(2 * nx)},
        out_shape=tuple(hbm(a) for a in list(x_thrus) + list(land_thrus)),
        compiler_params=pltpu.CompilerParams(has_side_effects=_EFFECT),
    )(*x_thrus, *land_thrus, send_sems, recv_sems, own_sems, after)
    return list(res[nx:])


def sum_slots(x, name):
    _, rows_total, cols = x.shape
    row_bytes = N_DEV * ((cols + LANE - 1) // LANE) * LANE * x.dtype.itemsize
    r = _pick_rows(rows_total, max(16, (4 * 1024 * 1024) // row_bytes // 16 * 16))

    def body(x_ref, o_ref):
        acc = x_ref[0].astype(f32)
        for s in range(1, N_DEV):
            acc = acc + x_ref[s].astype(f32)
        o_ref[...] = acc

    return pl.pallas_call(body, name=name, grid=(rows_total // r,),
                          in_specs=[pl.BlockSpec((N_DEV, r, cols), lambda i: (0, i, 0))],
                          out_specs=pl.BlockSpec((r, cols), lambda i: (i, 0)),
                          out_shape=jax.ShapeDtypeStruct((rows_total, cols), f32), compiler_params=_params())(x)


def _pick_rows(total, pref):
    best = None
    for d in range(16, min(total, pref) + 1, 16):
        if total % d == 0:
            best = d
    return best if best is not None else total


def adamw(w, g, m, v, name):
    shape = w.shape
    if w.ndim == 1:
        w2, g2, m2, v2 = (t.reshape(1, -1) for t in (w, g, m, v))
    else:
        w2, g2, m2, v2 = (t.reshape(-1, shape[-1]) for t in (w, g, m, v))
    rows_total, cols = w2.shape
    r = _pick_rows(rows_total, max(16, (512 * 1024) // max(cols, 1) // 16 * 16))
    c1, c2 = 1.0 / (1.0 - ADAM_B1 ** ADAM_STEP), 1.0 / (1.0 - ADAM_B2 ** ADAM_STEP)

    def body(w_ref, g_ref, m_ref, v_ref, d_ref, nm_ref, nv_ref):
        gv = g_ref[...]
        nm = ADAM_B1 * m_ref[...] + (1.0 - ADAM_B1) * gv
        nv = ADAM_B2 * v_ref[...] + (1.0 - ADAM_B2) * jnp.square(gv)
        d_ref[...] = -ADAM_LR * ((nm * c1) / (jnp.sqrt(nv * c2) + ADAM_EPS) + ADAM_WD * w_ref[...])
        nm_ref[...] = nm
        nv_ref[...] = nv

    spec = pl.BlockSpec((r, cols), lambda i: (i, 0))
    outs = pl.pallas_call(body, name=name, grid=(rows_total // r,), in_specs=[spec] * 4, out_specs=[spec] * 3,
                          out_shape=[jax.ShapeDtypeStruct((rows_total, cols), f32)] * 3, compiler_params=_params())(w2, g2, m2, v2)
    return tuple(o.reshape(shape) for o in outs)


def _const(j):
    return lambda _: j


def _layer_fwd(x, wl, fetch_rest, cos, sin, bsz, seq):
    n = x.shape[0]
    sv = {"x_in": x}
    row1 = lambda a: (a, (1, a.shape[1]), lambda j: (0, 0))
    h = rowmap(f_norm, [(x, D_MODEL, 0)], [row1(wl["attn_norm"])], [(D_MODEL, bf16)], 1, "norm_fwd")[0]
    u = mm(h, wl["w_in"], "nn", "mm_in")
    sv["h"], sv["u"] = h, u
    u3 = u.reshape(bsz, seq, U_PAD)
    wl = dict(wl)
    wl.update(fetch_rest(u))
    sv["wl"] = wl

    qkv = []
    for kind in range(3):
        cw = (wl["dn_conv_w"], (4, LANE), functools.partial(lambda j, kind: (0, 4 * kind + j), kind=kind))
        qkv.append(seqmap(functools.partial(f_dn_pre, kind), [(u3, U_QKV // LANE + 4 * kind)], [cw], 1, 4, "dn_pre%d" % kind)[0])
    gb = rowmap(f_dn_gates, [(u, 512, U_AB // 512)], [(wl["dn_gate_p"], (8, LANE), lambda j: (0, 0))], [(LANE, f32)], 1,
                "dn_gates", rows=512)[0]
    gb3 = gb.reshape(bsz, seq, LANE)
    crow = CHUNK * CHUNKS_PER_STEP
    dn_in = [(t.reshape(n, 512), 512, 0) for t in qkv] + [(gb, LANE, 0)]
    prep_a = rowmap(dn_prep, dn_in, [], [(512, f32)] + [(512, bf16)] * 4 + [(LANE, f32), (512, f32)], 1, "dn_prep", rows=crow)
    dn_in = dn_in + [(prep_a[6], 512, 0)]
    prep_a = [t.reshape(bsz, seq, t.shape[1]) for t in prep_a[:6]]
    o_a, ck_a = chunk_scan(dn_step, prep_a, (DN_HEADS, DN_DK, DN_DK), 512, "dn_scan")
    y_a = rowmap(per_head(f_dn_post), [(o_a.reshape(n, 512), 512, 0), (u, 512, U_Z // 512)],
                 [(wl["dn_norm_w"], (1, LANE), lambda j: (0, 0))], [(512, bf16)], 1, "dn_post")[0]
    sv.update(dn_in=dn_in, prep_a=prep_a, o_a=o_a, ck_a=ck_a, y_a=y_a)

    q_b, k_b = rowmap(f_ret_pre, [(u, 256, U_RQ // 256), (u, 256, U_RK // 256), (cos, 256, 0), (sin, 256, 0)], [],
                      [(256, f32), (256, f32)], 1, "ret_pre")
    q_b3, k_b3 = q_b.reshape(bsz, seq, 256), k_b.reshape(bsz, seq, 256)
    v_b3 = lax.slice_in_dim(u3, U_RV, U_RV + 512, axis=2)
    ret_in = [(q_b, 256, 0), (k_b, 256, 0), (u, 512, U_RV // 512)]
    inner = rowmap(ret_prep, ret_in, [], [(512, f32)], 1, "ret_prep", rows=crow)[0]
    ret_seq = [q_b3, k_b3, v_b3, inner.reshape(bsz, seq, 512)]
    o_b, ck_b = chunk_scan(ret_step, ret_seq, (RET_HEADS, 256, RET_DV), 512, "ret_scan")
    y_b = rowmap(per_head(f_ret_post), [(o_b.reshape(n, 512), 512, 0), (u, 512, U_RG // 512)], [], [(512, bf16)], 1,
                 "ret_post")[0]
    sv.update(ret_in=ret_in, ret_seq=ret_seq, o_b=o_b, ck_b=ck_b, y_b=y_b)

    lru_params = _lru_params(wl)
    a_c, b_c = seqmap(f_lru_pre, [(u3, U_CX // LANE)], lru_params, 2, 4, "lru_pre")
    h_c, hp_c = lru_scan(a_c, b_c)
    y_c = rowmap(f_lru_post, [(h_c.reshape(n, 512), 512, 0), (u, 512, U_CG // 512)], [], [(512, bf16)], 1, "lru_post")[0]
    sv.update(a_c=a_c, hp_c=hp_c, h_c=h_c, y_c=y_c)

    merged = branch_merge((y_a, y_b, y_c), wl["w_branch"], u)
    x_mid = mm(merged, wl["w_out"], "nn", "mm_out", add=x)
    sv.update(merged=merged, x_mid=x_mid)

    h2 = rowmap(f_norm, [(x_mid, D_MODEL, 0)], [row1(wl["ffn_norm"])], [(D_MODEL, bf16)], 1, "norm_fwd")[0]
    up = mm(h2, wl["w_up"], "nn", "mm_up")
    act = seqmap(f_ffn_mid, [(up.reshape(bsz, seq, 2 * D_FF), 0), (up.reshape(bsz, seq, 2 * D_FF), D_FF // LANE)],
                 _ffn_params(wl), 1, D_FF // LANE, "ffn_mid", out_dtype=bf16)[0]
    act = act.reshape(n, D_FF)
    x_out = mm(act, wl["w_down"], "nn", "mm_down", add=x_mid)
    sv.update(h2=h2, up=up, act=act)
    return x_out, sv


def _lru_params(wl):
    col = lambda a: (a, (a.shape[0], LANE), lambda j: (0, j))
    blk = lambda a: (a, (None, LANE, LANE), lambda j: (j, 0, 0))
    return [col(wl["lru_conv_w"]), col(wl["lru_conv_b"]), blk(wl["lru_wa"]), col(wl["lru_ba"]), blk(wl["lru_wx"]),
            col(wl["lru_bx"]), col(wl["lru_lambda"])]


def _ffn_params(wl):
    nb = D_FF // LANE
    return [(wl["ffn_conv_w"], (3, LANE), lambda j: (0, j)), (wl["ffn_conv_w"], (3, LANE), lambda j: (0, nb + j)),
            (wl["ffn_conv_b"], (1, LANE), lambda j: (0, j)), (wl["ffn_conv_b"], (1, LANE), lambda j: (0, nb + j))]


def _layer_bwd(dx, dx16, sv, cos, sin, bsz, seq, emit, dep):
    n = dx.shape[0]
    gr = {}
    wl = sv["wl"]
    u, x_in, x_mid = sv["u"], sv["x_in"], sv["x_mid"]
    u3 = u.reshape(bsz, seq, U_PAD)
    row1 = lambda a: (a, (1, a.shape[1]), lambda j: (0, 0))

    d_act = mm(dx16, wl["w_down"], "nt", "mm_down_dx", dep=dep)
    gr["w_down"] = mm(sv["act"], dx16, "tn", "mm_down_dw")
    up3 = sv["up"].reshape(bsz, seq, 2 * D_FF)
    (d_gate, d_val), dps = seqmap_bwd(f_ffn_mid, [(up3, 0), (up3, D_FF // LANE)], _ffn_params(wl),
                                      [d_act.reshape(bsz, seq, D_FF)], D_FF // LANE, "ffn_mid_bwd", din_dtype=bf16)
    gr["ffn_conv_w"] = jnp.concatenate([_cols(dps[0]), _cols(dps[1])], axis=1)
    gr["ffn_conv_b"] = jnp.concatenate([_cols(dps[2]), _cols(dps[3])], axis=1)[0]
    d_up = jnp.concatenate([d_gate, d_val], axis=2).reshape(n, 2 * D_FF)
    gr["w_up"] = mm(sv["h2"], d_up, "tn", "mm_up_dw")
    token = emit("ffn", {k: gr[k] for k in ("w_up", "w_down")})
    d_h2 = mm(d_up, wl["w_up"], "nt", "mm_up_dx", dep=token)
    (dx_mid,), (dg,), ex = rowmap_bwd(f_norm, [(x_mid, D_MODEL, 0)], [row1(wl["ffn_norm"])], [d_h2], 1, "norm_bwd", add=[dx],
                                      copy16=0)
    dx_mid16 = ex["copy16"]
    gr["ffn_norm"] = dg[0, 0]

    du = lax.empty((n, U_PAD), bf16)
    du3 = lambda: du.reshape(bsz, seq, U_PAD)

    d_merged = mm(dx_mid16, wl["w_out"], "nt", "mm_out_dx")
    gr["w_out"] = mm(sv["merged"], dx_mid16, "tn", "mm_out_dw")
    ys = (sv["y_a"], sv["y_b"], sv["y_c"])
    d_br, du = branch_merge_bwd(ys, wl["w_branch"], u, d_merged, du)
    d_ys = [mm(d_br[i], wl["w_branch"][i], "nt", "mm_branch_dx") for i in range(3)]
    gr["w_branch"] = jnp.stack([mm(ys[i], d_br[i], "tn", "mm_branch_dw") for i in range(3)])

    (d_hc, _), _, ex = rowmap_bwd(f_lru_post, [(sv["h_c"].reshape(n, 512), 512, 0), (u, 512, U_CG // 512)], [], [d_ys[2]], 1,
                                  "lru_post_bwd", into=(du, U_CG // 512, [1]))
    du = ex["into"]
    d_a, d_b = lru_scan_bwd(sv["a_c"], sv["hp_c"], d_hc.reshape(bsz, seq, 512))
    (du_new,), dps = seqmap_bwd(f_lru_pre, [(u3, U_CX // LANE)], _lru_params(wl), [d_a, d_b], 4, "lru_pre_bwd", din_dtype=bf16,
                                into=(du3(), U_CX // LANE))
    du = du_new.reshape(n, U_PAD)
    gr["lru_conv_w"], gr["lru_conv_b"] = _cols(dps[0]), _cols(dps[1])[0]
    gr["lru_wa"], gr["lru_ba"], gr["lru_wx"], gr["lru_bx"] = dps[2], dps[3][:, 0], dps[4], dps[5][:, 0]
    gr["lru_lambda"] = _cols(dps[6])[0]

    (d_ob, _), _, ex = rowmap_bwd(per_head(f_ret_post), [(sv["o_b"].reshape(n, 512), 512, 0), (u, 512, U_RG // 512)], [],
                                  [d_ys[1]], 1, "ret_post_bwd", into=(du, U_RG // 512, [1]))
    du = ex["into"]
    crow = CHUNK * CHUNKS_PER_STEP
    d_ret = chunk_scan_bwd(ret_step, sv["ret_seq"], sv["ck_b"], d_ob.reshape(bsz, seq, 512), "ret_scan_bwd")
    d_ret = [t.reshape(n, t.shape[2]) for t in d_ret]
    (d_qb, d_kb, _), _, ex = rowmap_bwd(ret_prep, sv["ret_in"], [], [d_ret[3]], 1, "ret_prep_bwd", rows=crow, add=d_ret[:3],
                                        into=(du, U_RV // 512, [2]))
    du = ex["into"]
    _, _, ex = rowmap_bwd(f_ret_pre, [(u, 256, U_RQ // 256), (u, 256, U_RK // 256), (cos, 256, 0), (sin, 256, 0)], [],
                          [d_qb, d_kb], 1, "ret_pre_bwd", din_dtypes=[f32, f32, None, None], into=(du, U_RQ // 512, [0, 1]))
    du = ex["into"]

    (d_oa, _), (dnw,), ex = rowmap_bwd(per_head(f_dn_post), [(sv["o_a"].reshape(n, 512), 512, 0), (u, 512, U_Z // 512)],
                                       [(wl["dn_norm_w"], (1, LANE), lambda j: (0, 0))], [d_ys[0]], 1, "dn_post_bwd",
                                       into=(du, U_Z // 512, [1]))
    du = ex["into"]
    gr["dn_norm_w"] = dnw[0, 0]
    d_prep = chunk_scan_bwd(dn_step, sv["prep_a"], sv["ck_a"], d_oa.reshape(bsz, seq, 512), "dn_scan_bwd")
    (d_q, d_k, d_v, d_gb, _), _, _ = rowmap_bwd(dn_prep, sv["dn_in"], [], [t.reshape(n, t.shape[2]) for t in d_prep], 1,
                                                "dn_prep_bwd", rows=crow, din_dtypes=[f32] * 4 + [None])
    d_q, d_k, d_v = (t.reshape(bsz, seq, 512) for t in (d_q, d_k, d_v))
    _, (dgp,), ex = rowmap_bwd(f_dn_gates, [(u, 512, U_AB // 512)], [(wl["dn_gate_p"], (8, LANE), lambda j: (0, 0))],
                               [d_gb], 1, "dn_gates_bwd", rows=512, into=(du, U_AB // 512, [0]))
    du = ex["into"]
    gr["dn_a_log"], gr["dn_dt_bias"] = dgp[0, 0, :DN_HEADS], dgp[0, 1, :DN_HEADS]
    d_cw = []
    for kind, d_t in enumerate((d_q, d_k, d_v)):
        cw = (wl["dn_conv_w"], (4, LANE), functools.partial(lambda j, kind: (0, 4 * kind + j), kind=kind))
        (du_new,), (dcw,) = seqmap_bwd(functools.partial(f_dn_pre, kind), [(u3, U_QKV // LANE + 4 * kind)], [cw], [d_t], 4,
                                       "dn_pre%d_bwd" % kind, din_dtype=bf16, into=(du3(), U_QKV // LANE + 4 * kind))
        du = du_new.reshape(n, U_PAD)
        d_cw.append(_cols(dcw))
    gr["dn_conv_w"] = jnp.concatenate(d_cw, axis=1)

    gr["w_in"] = _unpad_w_in(mm(sv["h"], du, "tn", "mm_in_dw"))
    token = emit("mix", {k: gr[k] for k in ("w_in", "w_branch", "w_out")})
    d_h = mm(du, wl["w_in"], "nt", "mm_in_dx", dep=token)
    (dx_in,), (dg,), ex = rowmap_bwd(f_norm, [(x_in, D_MODEL, 0)], [row1(wl["attn_norm"])], [d_h], 1, "norm_bwd", add=[dx_mid],
                                     copy16=0)
    gr["attn_norm"] = dg[0, 0]
    big = ("w_in", "w_branch", "w_out", "w_up", "w_down")
    return dx_in, ex["copy16"], emit("small", {k: g for k, g in gr.items() if k not in big})


def _cols(dp):
    ncol, p, _ = dp.shape
    return jnp.transpose(dp, (1, 0, 2)).reshape(p, ncol * LANE)


def _pad_w_in(w):
    segs = sorted(_IN_SEGS, key=lambda s: s[2])
    parts = [lax.slice_in_dim(w, src, src + width, axis=1) for src, width, _ in segs]
    end = segs[-1][2] + segs[-1][1]
    return jnp.concatenate(parts + [jnp.zeros((w.shape[0], U_PAD - end), w.dtype)], axis=1)


def _unpad_w_in(wp):
    return jnp.concatenate([lax.slice_in_dim(wp, dst, dst + width, axis=1) for _, width, dst in _IN_SEGS], axis=1)


def _rope_tables(positions):
    half = RET_DK // 2
    inv = ROPE_BASE ** (-jnp.arange(half, dtype=f32) / half)
    ang = positions.astype(f32).reshape(-1, 1) * inv
    cos, sin = jnp.cos(ang), jnp.sin(ang)
    return jnp.tile(cos, (1, 2 * RET_HEADS)), jnp.tile(sin, (1, 2 * RET_HEADS))


def _layer_weights(lw):
    wl = {}
    wl["w_in"] = _pad_w_in(lw["w_in"])
    for k in ("dn_conv_w", "lru_conv_w", "ffn_conv_w", "lru_wa", "lru_wx"):
        wl[k] = lw[k]
    for k in ("attn_norm", "ffn_norm", "dn_norm_w", "lru_conv_b", "lru_lambda", "ffn_conv_b", "lru_ba", "lru_bx"):
        wl[k] = lw[k].reshape(1, -1)
    gp = jnp.zeros((8, LANE), f32)
    wl["dn_gate_p"] = gp.at[0, :DN_HEADS].set(lw["dn_a_log"]).at[1, :DN_HEADS].set(lw["dn_dt_bias"])
    return wl


REST = ("w_branch", "w_out", "w_up", "w_down")


def forward_backward(x, positions, target, layer_weights, final_norm, on_head, on_grads):
    bsz, seq, d = x.shape
    n = bsz * seq
    cos, sin = _rope_tables(positions)
    xs = x.reshape(n, d)
    saved = []
    for layer in range(DEPTH):
        first, fetch_rest = layer_weights(layer, xs)
        xs, sv = _layer_fwd(xs, _layer_weights(first), fetch_rest, cos, sin, bsz, seq)
        saved.append(sv)
    loss, dx, d_final, dx16 = final_loss(xs, final_norm.reshape(1, d), target.reshape(n, d))
    on_head(loss[0, 0], d_final[0])
    token = None
    for layer in reversed(range(DEPTH)):
        dx, dx16, token = _layer_bwd(dx, dx16, saved[layer], cos, sin, bsz, seq, functools.partial(on_grads, layer), token)
    return dx.reshape(bsz, seq, d)


def local_step(x, positions, target, full):
    grads, head = {layer: {} for layer in range(DEPTH)}, {}

    def layer_weights(layer, _):
        return ({k: a[layer] for k, a in full.items() if k != "final_norm" and k not in REST},
                lambda after: {k: full[k][layer] for k in REST})

    gx = forward_backward(x, positions, target, layer_weights, full["final_norm"],
                          lambda loss, d_final: head.update(loss=loss, d_final=d_final),
                          lambda layer, group, gr: grads[layer].update(gr))
    stacked = {k: jnp.stack([grads[layer][k] for layer in range(DEPTH)]) for k in grads[0]}
    stacked["final_norm"] = head["d_final"]
    return head["loss"], gx, stacked


BIG = (("w_in", 2), ("w_branch", 3), ("w_out", 1), ("w_up", 2), ("w_down", 1))
SMALL_SHARDED = (("dn_conv_w", 2), ("lru_conv_w", 2), ("ffn_conv_w", 2))
REPLICATED = ("attn_norm", "dn_a_log", "dn_dt_bias", "dn_norm_w", "lru_conv_b", "lru_wa", "lru_ba", "lru_wx", "lru_bx",
              "lru_lambda", "ffn_norm", "ffn_conv_b", "final_norm")
WEIGHTS = ("attn_norm", "w_in", "dn_conv_w", "dn_a_log", "dn_dt_bias", "dn_norm_w", "lru_conv_w", "lru_conv_b", "lru_wa",
           "lru_ba", "lru_wx", "lru_bx", "lru_lambda", "w_branch", "w_out", "ffn_norm", "w_up", "ffn_conv_w", "ffn_conv_b",
           "w_down", "final_norm")


def _pack(arrs, dtype, align=16 * LANE):
    flat = jnp.concatenate([a.reshape(-1).astype(dtype) for a in arrs])
    pad = (-flat.shape[0]) % align
    return jnp.pad(flat, (0, pad)).reshape(-1, LANE)


def _unpack(rows, shapes):
    flat = rows.reshape(-1)
    out, pos = [], 0
    for shp in shapes:
        size = math.prod(shp)
        out.append(lax.slice_in_dim(flat, pos, pos + size).reshape(shp))
        pos += size
    return out


def kernel(x, positions, attn_norm, w_in, dn_conv_w, dn_a_log, dn_dt_bias, dn_norm_w, lru_conv_w, lru_conv_b, lru_wa, lru_ba, lru_wx, lru_bx, lru_lambda, w_branch, w_out, ffn_norm, w_up, ffn_conv_w, ffn_conv_b, w_down, final_norm, loss_target, m_attn_norm, m_w_in, m_dn_conv_w, m_dn_a_log, m_dn_dt_bias, m_dn_norm_w, m_lru_conv_w, m_lru_conv_b, m_lru_wa, m_lru_ba, m_lru_wx, m_lru_bx, m_lru_lambda, m_w_branch, m_w_out, m_ffn_norm, m_w_up, m_ffn_conv_w, m_ffn_conv_b, m_w_down, m_final_norm, v_attn_norm, v_w_in, v_dn_conv_w, v_dn_a_log, v_dn_dt_bias, v_dn_norm_w, v_lru_conv_w, v_lru_conv_b, v_lru_wa, v_lru_ba, v_lru_wx, v_lru_bx, v_lru_lambda, v_w_branch, v_w_out, v_ffn_norm, v_w_up, v_ffn_conv_w, v_ffn_conv_b, v_w_down, v_final_norm):
    w = dict(attn_norm=attn_norm, w_in=w_in, dn_conv_w=dn_conv_w, dn_a_log=dn_a_log, dn_dt_bias=dn_dt_bias, dn_norm_w=dn_norm_w,
             lru_conv_w=lru_conv_w, lru_conv_b=lru_conv_b, lru_wa=lru_wa, lru_ba=lru_ba, lru_wx=lru_wx, lru_bx=lru_bx,
             lru_lambda=lru_lambda, w_branch=w_branch, w_out=w_out, ffn_norm=ffn_norm, w_up=w_up, ffn_conv_w=ffn_conv_w,
             ffn_conv_b=ffn_conv_b, w_down=w_down, final_norm=final_norm)
    m = dict(attn_norm=m_attn_norm, w_in=m_w_in, dn_conv_w=m_dn_conv_w, dn_a_log=m_dn_a_log, dn_dt_bias=m_dn_dt_bias,
             dn_norm_w=m_dn_norm_w, lru_conv_w=m_lru_conv_w, lru_conv_b=m_lru_conv_b, lru_wa=m_lru_wa, lru_ba=m_lru_ba,
             lru_wx=m_lru_wx, lru_bx=m_lru_bx, lru_lambda=m_lru_lambda, w_branch=m_w_branch, w_out=m_w_out, ffn_norm=m_ffn_norm,
             w_up=m_w_up, ffn_conv_w=m_ffn_conv_w, ffn_conv_b=m_ffn_conv_b, w_down=m_w_down, final_norm=m_final_norm)
    v = dict(attn_norm=v_attn_norm, w_in=v_w_in, dn_conv_w=v_dn_conv_w, dn_a_log=v_dn_a_log, dn_dt_bias=v_dn_dt_bias,
             dn_norm_w=v_dn_norm_w, lru_conv_w=v_lru_conv_w, lru_conv_b=v_lru_conv_b, lru_wa=v_lru_wa, lru_ba=v_lru_ba,
             lru_wx=v_lru_wx, lru_bx=v_lru_bx, lru_lambda=v_lru_lambda, w_branch=v_w_branch, w_out=v_w_out, ffn_norm=v_ffn_norm,
             w_up=v_w_up, ffn_conv_w=v_ffn_conv_w, ffn_conv_b=v_ffn_conv_b, w_down=v_w_down, final_norm=v_final_norm)

    me = 4 * lax.axis_index("x") + 2 * lax.axis_index("y") + lax.axis_index("c")
    axes = dict(BIG + SMALL_SHARDED)
    conv_names = [k for k, _ in SMALL_SHARDED]

    gathers, tokens, conv_full = {}, [], {}
    for layer in range(DEPTH):
        first = [w["w_in"][layer].astype(bf16)] + ([w[k] for k in conv_names] if layer == 0 else [])
        rest = [w[k][layer].astype(bf16) for k in REST]
        for part, srcs in (("in", first), ("rest", rest)):
            gathers[layer, part], token = exchange_start(srcs, [False] * len(srcs), "gather_%s_start%d" % (part, layer))
            tokens.append(token[0:1, 0:1])
    all_started = functools.reduce(lambda a, b: a + b, tokens)

    def join(land, axis):
        if axis == 0:
            return land.reshape((N_DEV * land.shape[1],) + land.shape[2:])
        return jnp.concatenate([land[p] for p in range(N_DEV)], axis=axis)

    def split(g, axis):
        size = g.shape[axis] // N_DEV
        if axis == 0:
            return g.reshape((N_DEV, size) + g.shape[1:])
        return jnp.stack([lax.slice_in_dim(g, p * size, (p + 1) * size, axis=axis) for p in range(N_DEV)])

    def layer_weights(layer, x_in):
        lands = exchange_wait(gathers[layer, "in"], x_in, "gather_in_wait%d" % layer)
        lw = {"w_in": join(lands[0], 1)}
        if layer == 0:
            conv_full.update({k: join(lands[1 + i], axes[k]) for i, k in enumerate(conv_names)})
        lw.update({k: conv_full[k][layer] for k in conv_names})
        lw.update({k: w[k][layer] for k in REPLICATED if k != "final_norm"})
        if layer == 0:
            lw["attn_norm"] = lw["attn_norm"] + all_started[0]

        def fetch_rest(after):
            lands_r = exchange_wait(gathers[layer, "rest"], after, "gather_rest_wait%d" % layer)
            return {k: join(lands_r[i], axes[k] - 1) for i, k in enumerate(REST)}

        return lw, fetch_rest

    small_names = conv_names + [k for k in REPLICATED if k != "final_norm"]
    groups = {"ffn": ("w_up", "w_down"), "mix": ("w_in", "w_branch", "w_out")}
    scatters, small_shapes, head = {}, {}, {}

    def on_grads(layer, group, gr):
        if group == "small":
            small_shapes.update({k: gr[k].shape for k in small_names})
            srcs = [_pack([gr[k] for k in small_names], f32)]
            srcs += [_pack([head["loss"].reshape(1), head["d_final"]], f32)] if layer == DEPTH - 1 else []
            modes = [False] * len(srcs)
        else:
            srcs = [split(gr[k], axes[k] - 1).astype(bf16) for k in groups[group]]
            modes = [True] * len(srcs)
        scatters[layer, group], token = exchange_start(srcs, modes, "scatter_%s_start%d" % (group, layer))
        return token

    grad_x = forward_backward(x, positions, loss_target, layer_weights, final_norm,
                              lambda loss_part, d_final: head.update(loss=loss_part, d_final=d_final), on_grads)

    big_sums, small_sums = {}, {}
    for group in ("ffn", "mix"):
        for layer in reversed(range(DEPTH)):
            lands = exchange_wait(scatters[layer, group], grad_x, "scatter_%s_wait%d" % (group, layer))
            for i, k in enumerate(groups[group]):
                shard = w[k].shape[1:]
                big_sums[layer, k] = sum_slots(lands[i].reshape(N_DEV, -1, shard[-1]), "sum_" + k).reshape(shard)
    for layer in reversed(range(DEPTH)):
        lands = exchange_wait(scatters[layer, "small"], grad_x, "scatter_small_wait%d" % layer)
        small_sums[layer] = sum_slots(lands[0], "sum_small")
        if layer == DEPTH - 1:
            head_sum = _unpack(sum_slots(lands[1], "sum_head"), [(1,), final_norm.shape])
    grads = {k: jnp.stack([big_sums[layer, k] for layer in range(DEPTH)]) for k, _ in BIG}
    loss, grads["final_norm"] = head_sum[0][0], head_sum[1]
    small_flat = jnp.stack([small_sums[layer] for layer in range(DEPTH)]).reshape(DEPTH, -1)
    pos = 0
    for k in small_names:
        size = math.prod(small_shapes[k])
        g = lax.slice_in_dim(small_flat, pos, pos + size, axis=1).reshape((DEPTH,) + small_shapes[k])
        pos += size
        ax = dict(SMALL_SHARDED).get(k)
        if ax is None:
            grads[k] = g
        else:
            size = g.shape[ax] // N_DEV
            grads[k] = lax.dynamic_slice_in_dim(g, me * size, size, axis=ax)

    upd = {k: adamw(w[k], grads[k], m[k], v[k], "adamw_" + k) for k in WEIGHTS}
    return (loss, grad_x, *[grads[k] for k in WEIGHTS], *[upd[k][0] for k in WEIGHTS], *[upd[k][1] for k in WEIGHTS],
            *[upd[k][2] for k in WEIGHTS])
```

```python
import functools
import math

import jax
import jax.numpy as jnp
from jax import lax
from jax.experimental import pallas as pl
from jax.experimental.pallas import tpu as pltpu

f32 = jnp.float32
bf16 = jnp.bfloat16

D_MODEL = 1024
DEPTH = 4
CHUNK = 64
EPS = 1e-6
DN_HEADS, DN_DK = 4, 128
RET_HEADS, RET_DK, RET_DV = 4, 64, 128
ROPE_BASE = 10000.0
LRU_C = 8.0
D_FF = 2816
N_DEV = 8
LANE = 128
VMEM_LIMIT = 56 * 1024 * 1024

ADAM_LR, ADAM_B1, ADAM_B2, ADAM_EPS, ADAM_WD, ADAM_STEP = 0.001, 0.9, 0.999, 1e-8, 0.01, 10

U_GATES, U_QKV, U_RV, U_RG, U_Z, U_CX, U_CG, U_RQ, U_RK, U_AB = (
    0, 3072, 4608, 5120, 5632, 6144, 6656, 7168, 7424, 7680)
U_PAD = 8192
_IN_SEGS = ((0, 1536, U_QKV), (1536, 8, U_AB), (1544, 512, U_Z), (2056, 256, U_RQ), (2312, 256, U_RK),
            (2568, 512, U_RV), (3080, 512, U_RG), (3592, 512, U_CX), (4104, 512, U_CG), (4616, 3072, U_GATES))
N_IN = 7688


def _params():
    return pltpu.CompilerParams(vmem_limit_bytes=VMEM_LIMIT)


def _pick(dim, pref):
    best = None
    for d in range(LANE, min(dim, pref) + 1, LANE):
        if dim % d == 0:
            best = d
    return best if best is not None else dim


@functools.partial(jax.custom_vjp, nondiff_argnums=(1, 2))
def sroll(x, shift, axis):
    return pltpu.roll(x, shift, axis)


def _sroll_fwd(x, shift, axis):
    return pltpu.roll(x, shift, axis), None


def _sroll_bwd(shift, axis, _, g):
    n = g.shape[axis]
    return (pltpu.roll(g, (n - shift) % n, axis),)


sroll.defvjp(_sroll_fwd, _sroll_bwd)

_DIMS = {"nn": (((1,), (0,)), ((), ())), "nt": (((1,), (1,)), ((), ())), "tn": (((0,), (0,)), ((), ()))}


def _dg(a, b, dims):
    return lax.dot_general(a.astype(bf16), b.astype(bf16), _DIMS[dims], preferred_element_type=f32)


@functools.partial(jax.custom_vjp, nondiff_argnums=(2,))
def bdot(a, b, dims):
    return _dg(a, b, dims)


def _bdot_fwd(a, b, dims):
    return _dg(a, b, dims), (a.astype(bf16), b.astype(bf16))


def _bdot_bwd(dims, res, g):
    a, b = res
    if dims == "nn":
        return _dg(g, b, "nt"), _dg(a, g, "tn")
    if dims == "nt":
        return _dg(g, b, "nn"), _dg(g, a, "tn")
    return _dg(b, g, "nt"), _dg(a, g, "nn")


bdot.defvjp(_bdot_fwd, _bdot_bwd)


def _fdot(a, b, dims):
    return lax.dot_general(a, b, _DIMS[dims], precision=lax.Precision.HIGH, preferred_element_type=f32)


@jax.custom_vjp
def unit_lower_inv_all(mats):
    shape = mats[0].shape
    row = lax.broadcasted_iota(jnp.int32, shape, 0)
    col = lax.broadcasted_iota(jnp.int32, shape, 1)
    eye = jnp.where(row == col, 1.0, 0.0).astype(f32)
    n = [-a for a in mats]
    p = [eye + x for x in n]
    span = 2
    while span < shape[0]:
        n = [_fdot(x, x, "nn") for x in n]
        p = [y + _fdot(y, x, "nn") for y, x in zip(p, n)]
        span *= 2
    return p


def _uli_fwd(mats):
    x = unit_lower_inv_all(mats)
    return x, x


def _uli_bwd(xs, gs):
    t = [_fdot(x, g, "tn") for x, g in zip(xs, gs)]
    return ([-_fdot(y, x, "nt") for y, x in zip(t, xs)],)


unit_lower_inv_all.defvjp(_uli_fwd, _uli_bwd)


@jax.custom_vjp
def known_inverse(invs, mats):
    return invs


def _known_fwd(invs, mats):
    return invs, invs


def _known_bwd(xs, gs):
    return [jnp.zeros_like(x) for x in xs], _uli_bwd(xs, gs)[0]


known_inverse.defvjp(_known_fwd, _known_bwd)


def cumsum_rows(x):
    rows = x.shape[0]
    row = lax.broadcasted_iota(jnp.int32, x.shape, 0)
    s = 1
    while s < rows:
        x = x + jnp.where(row >= s, sroll(x, s, 0), 0.0)
        s *= 2
    return x


def _expm1(x):
    return jnp.tanh(0.5 * x) * (jnp.exp(x) + 1.0)


def _lane_pick(x, lane):
    idx = lax.broadcasted_iota(jnp.int32, x.shape, 1)
    return jnp.sum(jnp.where(idx == lane, x, 0.0), axis=1, keepdims=True)


def _row_pick(x, r):
    idx = lax.broadcasted_iota(jnp.int32, x.shape, 0)
    return jnp.sum(jnp.where(idx == r, x, 0.0), axis=0, keepdims=True)


def _causal_conv(x, halo, w, width):
    xe = jnp.concatenate([halo, x], axis=0)
    acc = xe * w[width - 1:width]
    for k in range(width - 1):
        acc = acc + sroll(xe, width - 1 - k, 0) * w[k:k + 1]
    return acc[8:]


def f_norm(ins, ps):
    (x,), (g,) = ins, ps
    return [x * lax.rsqrt(jnp.mean(x * x, axis=-1, keepdims=True) + EPS) * g]


def f_dn_pre(kind, mains, halos, ps):
    y = _causal_conv(mains[0], halos[0], ps[0], 4)
    y = y * jax.nn.sigmoid(y)
    if kind < 2:
        y = y * lax.rsqrt(jnp.sum(y * y, axis=-1, keepdims=True) + EPS)
    if kind == 0:
        y = y * (DN_DK ** -0.5)
    return [y]


def f_dn_gates(ins, ps):
    u, p = ins[0][:, :LANE], ps[0]
    lane = lax.broadcasted_iota(jnp.int32, u.shape, 1)
    g = -jnp.exp(p[0:1]) * jax.nn.softplus(u + p[1:2])
    beta = jax.nn.sigmoid(u)
    return [jnp.where(lane < 4, g, jnp.where(lane < 8, beta, 0.0))]


def per_head(fn):
    def tile_fn(vals, ps):
        heads = [fn([v[:, h * LANE:(h + 1) * LANE] for v in vals], ps) for h in range(vals[0].shape[1] // LANE)]
        return [jnp.concatenate([o[i] for o in heads], axis=1) for i in range(len(heads[0]))]
    return tile_fn


def f_dn_post(ins, ps):
    (o, z), (nw,) = ins, ps
    y = o * lax.rsqrt(jnp.mean(o * o, axis=-1, keepdims=True) + EPS) * nw
    return [y * (z * jax.nn.sigmoid(z))]


def _rot_half(t):
    lane = lax.broadcasted_iota(jnp.int32, t.shape, 1)
    width = t.shape[1]
    first = (lane % RET_DK) < (RET_DK // 2)
    return jnp.where(first, -sroll(t, width - RET_DK // 2, 1), sroll(t, RET_DK // 2, 1))


def f_ret_pre(ins, ps):
    q, k, cos, sin = ins
    qr = q * cos + _rot_half(q) * sin
    kr = (k * cos + _rot_half(k) * sin) * (RET_DK ** -0.5)
    return [qr, kr]


def f_ret_post(ins, ps):
    o, g = ins
    mu = jnp.mean(o, axis=-1, keepdims=True)
    var = jnp.mean(jnp.square(o - mu), axis=-1, keepdims=True)
    return [(o - mu) * lax.rsqrt(var + EPS) * (g * jax.nn.sigmoid(g))]


def f_lru_pre(mains, halos, ps):
    cw, cb, wa, ba, wx, bx, lam = ps
    xc = _causal_conv(mains[0], halos[0], cw, 4) + cb
    r = jax.nn.sigmoid(bdot(xc, wa, "nn") + ba)
    i = jax.nn.sigmoid(bdot(xc, wx, "nn") + bx)
    log_a = -LRU_C * r * jax.nn.softplus(-lam)
    a = jnp.exp(log_a)
    b = jnp.sqrt(-_expm1(2.0 * log_a)) * (i * xc)
    return [a, b]


def f_lru_post(ins, ps):
    h, g = ins
    return [h * jax.nn.gelu(g)]


def f_ffn_mid(mains, halos, ps):
    cwg, cwv, cbg, cbv = ps
    gate = _causal_conv(mains[0], halos[0], cwg, 3) + cbg
    val = _causal_conv(mains[1], halos[1], cwv, 3) + cbv
    return [gate * jax.nn.sigmoid(gate) * val]


def mm(a, b, dims, name, add=None, dep=None, b_koff=0, tm=1536, tn=1536, tk=2816):
    if dims == "tn":
        kdim, m = a.shape
        n = b.shape[1]
    else:
        m, kdim = a.shape
        n = b.shape[0] if dims == "nt" else b.shape[1]
    tm, tn, tk = _pick(m, tm), _pick(n, tn), _pick(kdim, tk)
    nk = kdim // tk
    a_spec = pl.BlockSpec((tk, tm), lambda i, j, k: (k, i)) if dims == "tn" else pl.BlockSpec((tm, tk), lambda i, j, k: (i, k))
    b_spec = (pl.BlockSpec((tn, tk), lambda i, j, k: (j, k + b_koff * nk)) if dims == "nt"
              else pl.BlockSpec((tk, tn), lambda i, j, k: (k, j)))
    o_spec = pl.BlockSpec((tm, tn), lambda i, j, k: (i, j))
    has_add, has_dep = add is not None, dep is not None

    def body(*refs):
        a_ref, b_ref = refs[:2]
        add_ref = refs[2] if has_add else None
        o_ref = refs[2 + has_add + has_dep]
        if nk == 1:
            prod = _dg(a_ref[...], b_ref[...], dims)
            o_ref[...] = prod + add_ref[...] if has_add else prod
            return
        acc_ref = refs[-1]
        k = pl.program_id(2)

        @pl.when(k == 0)
        def _():
            acc_ref[...] = jnp.zeros_like(acc_ref)

        acc_ref[...] += _dg(a_ref[...], b_ref[...], dims)

        @pl.when(k == nk - 1)
        def _():
            o_ref[...] = acc_ref[...] + add_ref[...] if has_add else acc_ref[...]

    args = [a, b] + ([add] if has_add else []) + ([dep] if has_dep else [])
    in_specs = [a_spec, b_spec] + ([o_spec] if has_add else [])
    in_specs += [pl.BlockSpec((8, LANE), lambda i, j, k: (0, 0))] if has_dep else []
    return pl.pallas_call(
        body, name=name, grid=(m // tm, n // tn, nk), in_specs=in_specs, out_specs=o_spec,
        out_shape=jax.ShapeDtypeStruct((m, n), f32), scratch_shapes=[pltpu.VMEM((tm, tn), f32)] if nk > 1 else [],
        compiler_params=_params())(*args)


def rowmap(fn, ins, params, outs, ncol, name, rows=512):
    n = ins[0][0].shape[0]
    r = min(rows, n)
    nin, npar = len(ins), len(params)

    def body(*refs):
        vals = [x[...] for x in refs[:nin]]
        pv = [p[...] for p in refs[nin:nin + npar]]
        for o_ref, o in zip(refs[nin + npar:], fn(vals, pv)):
            o_ref[...] = o.astype(o_ref.dtype)

    in_specs = [pl.BlockSpec((r, cb), functools.partial(lambda j, i, off: (i, off + j), off=off)) for _, cb, off in ins]
    in_specs += [pl.BlockSpec(bs, functools.partial(lambda j, i, f: f(j), f=f)) for _, bs, f in params]
    out_specs = [pl.BlockSpec((r, cb), lambda j, i: (i, j)) for cb, _ in outs]
    out_shape = [jax.ShapeDtypeStruct((n, cb * ncol), dt) for cb, dt in outs]
    res = pl.pallas_call(body, name=name, grid=(ncol, n // r), in_specs=in_specs, out_specs=out_specs,
                         out_shape=out_shape, compiler_params=_params())(*[a for a, _, _ in ins], *[a for a, _, _ in params])
    return res


def rowmap_bwd(fn, ins, params, douts, ncol, name, rows=512, add=None, din_dtypes=None, into=None, copy16=None):
    n = ins[0][0].shape[0]
    r = min(rows, n)
    nin, npar, nout = len(ins), len(params), len(douts)
    add = [None] * nin if add is None else list(add)
    add_idx = [i for i in range(nin) if add[i] is not None]
    din_dtypes = [f32] * nin if din_dtypes is None else list(din_dtypes)
    into_buf, into_off, into_idx = into if into is not None else (None, 0, [])
    has_into, has_copy = into is not None, copy16 is not None
    kept = [i for i in range(nin) if din_dtypes[i] is not None and i not in into_idx]

    def body(*refs):
        vals = [x[...] for x in refs[:nin]]
        pv = [p[...] for p in refs[nin:nin + npar]]
        dys = [d[...] for d in refs[nin + npar:nin + npar + nout]]
        k0 = nin + npar + nout
        add_refs = dict(zip(add_idx, refs[k0:k0 + len(add_idx)]))
        k0 += len(add_idx) + has_into
        din_refs = refs[k0:k0 + len(kept)]
        k0 += len(kept)
        copy_ref = refs[k0] if has_copy else None
        into_ref = refs[k0 + has_copy] if has_into else None
        dp_refs = refs[k0 + has_copy + has_into:]
        _, vjp = jax.vjp(fn, vals, pv)
        dvals, dpv = vjp(dys)
        cot = lambda idx: dvals[idx] + add_refs[idx][...] if idx in add_refs else dvals[idx]
        for d_ref, idx in zip(din_refs, kept):
            d_ref[...] = cot(idx).astype(d_ref.dtype)
        if has_copy:
            copy_ref[...] = cot(copy16).astype(copy_ref.dtype)
        if has_into:
            parts = [cot(idx) for idx in into_idx]
            into_ref[...] = (parts[0] if len(parts) == 1 else jnp.concatenate(parts, axis=1)).astype(into_ref.dtype)

        @pl.when(pl.program_id(1) == 0)
        def _():
            for d_ref in dp_refs:
                d_ref[...] = jnp.zeros_like(d_ref)

        for d_ref, d in zip(dp_refs, dpv):
            d_ref[...] += d

    in_specs = [pl.BlockSpec((r, cb), functools.partial(lambda j, i, off: (i, off + j), off=off)) for _, cb, off in ins]
    in_specs += [pl.BlockSpec(bs, functools.partial(lambda j, i, f: f(j), f=f)) for _, bs, f in params]
    in_specs += [pl.BlockSpec((r, d.shape[1] // ncol), lambda j, i: (i, j)) for d in douts]
    in_specs += [pl.BlockSpec((r, ins[i][1]), lambda j, i: (i, j)) for i in add_idx]
    out_specs = [pl.BlockSpec((r, ins[i][1]), lambda j, i: (i, j)) for i in kept]
    out_shape = [jax.ShapeDtypeStruct((n, ins[i][1] * ncol), din_dtypes[i]) for i in kept]
    args = [a for a, _, _ in ins] + [a for a, _, _ in params] + list(douts) + [add[i] for i in add_idx]
    aliases = {}
    if has_copy:
        out_specs += [pl.BlockSpec((r, ins[copy16][1]), lambda j, i: (i, j))]
        out_shape += [jax.ShapeDtypeStruct((n, ins[copy16][1] * ncol), bf16)]
    if has_into:
        assert ncol == 1
        in_specs += [pl.BlockSpec(memory_space=pl.ANY)]
        aliases[len(args)] = len(out_shape)
        args += [into_buf]
        out_specs += [pl.BlockSpec((r, sum(ins[i][1] for i in into_idx)), lambda j, i: (i, into_off))]
        out_shape += [jax.ShapeDtypeStruct(into_buf.shape, into_buf.dtype)]
    pshapes = [tuple(d for d in bs if d is not None) for _, bs, _ in params]
    out_specs += [pl.BlockSpec((None,) + ps, functools.partial(lambda j, i, nd: (j,) + (0,) * nd, nd=len(ps))) for ps in pshapes]
    out_shape += [jax.ShapeDtypeStruct((ncol,) + ps, f32) for ps in pshapes]
    res = pl.pallas_call(body, name=name, grid=(ncol, n // r), in_specs=in_specs, out_specs=out_specs, out_shape=out_shape,
                         input_output_aliases=aliases, compiler_params=_params())(*args)
    dins = [None] * nin
    for pos, i in enumerate(kept):
        dins[i] = res[pos]
    pos = len(kept)
    extras = {}
    if has_copy:
        extras["copy16"] = res[pos]
        pos += 1
    if has_into:
        extras["into"] = res[pos]
        pos += 1
    return dins, res[pos:], extras


SEQ_ROWS = 2048


def seqmap(fn, ins, params, nouts, ncol, name, out_dtype=f32):
    bsz, seq, _ = ins[0][0].shape
    r = min(SEQ_ROWS, seq)
    nin, npar = len(ins), len(params)

    def body(*refs):
        in_refs = refs[:nin]
        pv = [p[...] for p in refs[nin:nin + npar]]
        out_refs = refs[nin + npar:]

        def step(i, carry):
            r0 = pl.multiple_of(i * r, r)
            h0 = pl.multiple_of(jnp.maximum(r0 - 8, 0), 8)
            mains = [x[pl.ds(r0, r), :] for x in in_refs]
            halos = [jnp.where(i == 0, 0.0, x[pl.ds(h0, 8), :]) for x in in_refs]
            for o_ref, o in zip(out_refs, fn(mains, halos, pv)):
                o_ref[pl.ds(r0, r), :] = o.astype(o_ref.dtype)
            return carry

        lax.fori_loop(0, seq // r, step, 0)

    in_specs = [pl.BlockSpec((None, seq, LANE), functools.partial(lambda j, b, off: (b, 0, off + j), off=off)) for _, off in ins]
    in_specs += [pl.BlockSpec(bs, functools.partial(lambda j, b, f: f(j), f=f)) for _, bs, f in params]
    out_specs = [pl.BlockSpec((None, seq, LANE), lambda j, b: (b, 0, j)) for _ in range(nouts)]
    out_shape = [jax.ShapeDtypeStruct((bsz, seq, LANE * ncol), out_dtype) for _ in range(nouts)]
    return pl.pallas_call(body, name=name, grid=(ncol, bsz), in_specs=in_specs, out_specs=out_specs,
                          out_shape=out_shape, compiler_params=_params())(*[a for a, _ in ins], *[a for a, _, _ in params])


def seqmap_bwd(fn, ins, params, douts, ncol, name, din_dtype=f32, into=None):
    bsz, seq, _ = ins[0][0].shape
    r = min(SEQ_ROWS, seq)
    nin, npar, nout = len(ins), len(params), len(douts)
    narrow = din_dtype != f32

    def body(*refs):
        in_refs = refs[:nin]
        pv = [p[...] for p in refs[nin:nin + npar]]
        dy_refs = refs[nin + npar:nin + npar + nout]
        k0 = nin + npar + nout + (into is not None)
        dout_refs = refs[k0:k0 + nin]
        dp_refs = refs[k0 + nin:k0 + nin + npar]
        din_refs = refs[k0 + nin + npar:] if narrow else dout_refs

        def step(i, dp_acc):
            r0 = pl.multiple_of(i * r, r)
            h0 = pl.multiple_of(jnp.maximum(r0 - 8, 0), 8)
            mains = [x[pl.ds(r0, r), :] for x in in_refs]
            halos_raw = [x[pl.ds(h0, 8), :] for x in in_refs]

            def tile(mains, halos_raw, pv):
                return fn(mains, [jnp.where(i == 0, 0.0, h) for h in halos_raw], pv)

            _, vjp = jax.vjp(tile, mains, halos_raw, pv)
            dm, dh, dp = vjp([d[pl.ds(r0, r), :] for d in dy_refs])
            for d_ref, m, h in zip(din_refs, dm, dh):
                d_ref[pl.ds(r0, r), :] = m
                d_ref[pl.ds(h0, 8), :] += h
            return [acc + d for acc, d in zip(dp_acc, dp)]

        dp = lax.fori_loop(0, seq // r, step, [jnp.zeros(p.shape, f32) for p in pv])
        if narrow:
            for o_ref, d_ref in zip(dout_refs, din_refs):
                o_ref[...] = d_ref[...].astype(o_ref.dtype)

        @pl.when(pl.program_id(1) == 0)
        def _():
            for d_ref in dp_refs:
                d_ref[...] = jnp.zeros_like(d_ref)

        for d_ref, d in zip(dp_refs, dp):
            d_ref[...] += d

    in_specs = [pl.BlockSpec((None, seq, LANE), functools.partial(lambda j, b, off: (b, 0, off + j), off=off)) for _, off in ins]
    in_specs += [pl.BlockSpec(bs, functools.partial(lambda j, b, f: f(j), f=f)) for _, bs, f in params]
    in_specs += [pl.BlockSpec((None, seq, LANE), lambda j, b: (b, 0, j)) for _ in range(nout)]
    out_specs = [pl.BlockSpec((None, seq, LANE), lambda j, b: (b, 0, j)) for _ in range(nin)]
    pshapes = [tuple(d for d in bs if d is not None) for _, bs, _ in params]
    out_specs += [pl.BlockSpec((None,) + ps, functools.partial(lambda j, b, nd: (j,) + (0,) * nd, nd=len(ps))) for ps in pshapes]
    out_shape = [jax.ShapeDtypeStruct((bsz, seq, LANE * ncol), din_dtype) for _ in range(nin)]
    out_shape += [jax.ShapeDtypeStruct((ncol,) + ps, f32) for ps in pshapes]
    args = [a for a, _ in ins] + [a for a, _, _ in params] + list(douts)
    aliases = {}
    if into is not None:
        assert nin == 1 and into[0].dtype == din_dtype
        in_specs += [pl.BlockSpec(memory_space=pl.ANY)]
        aliases[len(args)] = 0
        args += [into[0]]
        out_specs[0] = pl.BlockSpec((None, seq, LANE), lambda j, b: (b, 0, into[1] + j))
        out_shape[0] = jax.ShapeDtypeStruct(into[0].shape, din_dtype)
    res = pl.pallas_call(body, name=name, grid=(ncol, bsz), in_specs=in_specs, out_specs=out_specs, out_shape=out_shape,
                         scratch_shapes=[pltpu.VMEM((seq, LANE), f32) for _ in range(nin)] if narrow else [],
                         input_output_aliases=aliases, compiler_params=_params())(*args)
    return res[:nin], res[nin:]


def _tri_masks():
    row = lax.broadcasted_iota(jnp.int32, (CHUNK, CHUNK), 0)
    col = lax.broadcasted_iota(jnp.int32, (CHUNK, CHUNK), 1)
    return row >= col, row > col


CHUNKS_PER_STEP = 4


def _by_rows(parts, per_row):
    rows = [jnp.concatenate(parts[i:i + per_row], axis=1) for i in range(0, len(parts), per_row)]
    return jnp.concatenate(rows, axis=0)


def dn_prep(vals, ps):
    q, k, v, gb = vals[:4]
    nchunk = q.shape[0] // CHUNK
    causal, strict = _tri_masks()
    gbs = [gb[c * CHUNK:(c + 1) * CHUNK] for c in range(nchunk)]
    gcs = [cumsum_rows(g) for g in gbs]
    gcts = [g.T for g in gcs]
    chains = [(c, h) for c in range(nchunk) for h in range(DN_HEADS)]
    part = lambda t, c, h: t[c * CHUNK:(c + 1) * CHUNK, h * DN_DK:(h + 1) * DN_DK]
    qh = [part(q, c, h) for c, h in chains]
    kh = [part(k, c, h) for c, h in chains]
    vh = [part(v, c, h) for c, h in chains]
    g_col = [_lane_pick(gcs[c], h) for c, h in chains]
    beta = [_lane_pick(gbs[c], DN_HEADS + h) for c, h in chains]
    g_row = [_row_pick(gcts[c], h)[:, :CHUNK] for c, h in chains]
    decay = [jnp.where(causal, jnp.exp(jnp.where(causal, gc - gr, 0.0)), 0.0) for gc, gr in zip(g_col, g_row)]
    k_beta = [a * b for a, b in zip(kh, beta)]
    eg = [jnp.exp(g) for g in g_col]
    kk = [bdot(a, b, "nt") for a, b in zip(k_beta, kh)]
    qk = [bdot(a, b, "nt") for a, b in zip(qh, kh)]
    lower = [jnp.where(strict, a * d, 0.0) for a, d in zip(kk, decay)]
    if len(vals) == 5:
        t_inv = known_inverse([part(vals[4], c, h)[:, :CHUNK] for c, h in chains], lower)
    else:
        t_inv = unit_lower_inv_all(lower)
    u = [bdot(t, a * b, "nn") for t, a, b in zip(t_inv, vh, beta)]
    w = [bdot(t, a * e, "nn") for t, a, e in zip(t_inv, k_beta, eg)]
    attn = [jnp.concatenate([a * d, jnp.zeros((CHUNK, DN_DK - CHUNK), f32)], axis=1) for a, d in zip(qk, decay)]
    qd = [a * e for a, e in zip(qh, eg)]
    kd = [a * jnp.exp(_row_pick(g, CHUNK - 1) - g) for a, g in zip(kh, g_col)]
    g_last = jnp.concatenate([jnp.broadcast_to(_row_pick(g, CHUNK - 1), g.shape) for g in gcs], axis=0)
    outs = [_by_rows(t, DN_HEADS) for t in (u, w, attn, qd, kd)] + [g_last]
    if len(vals) == 4:
        wide = [jnp.concatenate([t, jnp.zeros((CHUNK, DN_DK - CHUNK), f32)], axis=1) for t in t_inv]
        outs.append(_by_rows(wide, DN_HEADS))
    return outs


def dn_step(state, u, w, attn, qd, kd, g_last):
    bsz = u.shape[0]
    chains = [(b, h) for b in range(bsz) for h in range(DN_HEADS)]
    part = lambda t, b, h: t[b, :, h * DN_DK:(h + 1) * DN_DK]
    ws = [bdot(part(w, b, h), s, "nn") for (b, h), s in zip(chains, state)]
    qs = [bdot(part(qd, b, h), s, "nn") for (b, h), s in zip(chains, state)]
    v_new = [part(u, b, h) - x for (b, h), x in zip(chains, ws)]
    av = [bdot(attn[b, :, h * DN_DK:h * DN_DK + CHUNK], x, "nn") for (b, h), x in zip(chains, v_new)]
    kv = [bdot(part(kd, b, h), x, "tn") for (b, h), x in zip(chains, v_new)]
    ge = [jnp.exp(_row_pick(_lane_pick(g_last[b], h), 0)) for b, h in chains]
    new_state = [s * g + x for s, g, x in zip(state, ge, kv)]
    outs = [a + b for a, b in zip(qs, av)]
    return new_state, jnp.concatenate([jnp.concatenate(outs[b * DN_HEADS:(b + 1) * DN_HEADS], axis=1)[None]
                                       for b in range(bsz)], axis=0)


def _ret_log_gamma(h):
    return math.log(1.0 - 2.0 ** (-5.0 - h))


def ret_prep(vals, ps):
    q, k, v = vals
    nchunk = q.shape[0] // CHUNK
    causal, _ = _tri_masks()
    row = lax.broadcasted_iota(jnp.int32, (CHUNK, CHUNK), 0)
    col = lax.broadcasted_iota(jnp.int32, (CHUNK, CHUNK), 1)
    dist = (row - col).astype(f32)
    lane = lax.broadcasted_iota(jnp.int32, (CHUNK, q.shape[1]), 1)
    dmask = [jnp.where(causal, jnp.exp(jnp.where(causal, dist, 0.0) * _ret_log_gamma(h)), 0.0) for h in range(RET_HEADS)]
    chains = [(c, h) for c in range(nchunk) for h in range(RET_HEADS)]
    rows = lambda t, c: t[c * CHUNK:(c + 1) * CHUNK]
    scores = [bdot(jnp.where((lane // RET_DK) == h, rows(q, c), 0.0), rows(k, c), "nt") * dmask[h] for c, h in chains]
    inner = [bdot(s, rows(v, c)[:, h * RET_DV:(h + 1) * RET_DV], "nn") for s, (c, h) in zip(scores, chains)]
    return [_by_rows(inner, RET_HEADS)]


def ret_step(state, q, k, v, inner):
    bsz = q.shape[0]
    idx = lax.broadcasted_iota(jnp.int32, (CHUNK, 1), 0).astype(f32)
    lane = lax.broadcasted_iota(jnp.int32, (CHUNK, q.shape[2]), 1)
    chains = [(b, h) for b in range(bsz) for h in range(RET_HEADS)]
    part = lambda t, b, h: t[b, :, h * RET_DV:(h + 1) * RET_DV]
    cross = [bdot(q[b], s, "nn") for (b, h), s in zip(chains, state)]
    kz = [jnp.where((lane // RET_DK) == h, k[b], 0.0) * jnp.exp((CHUNK - 1.0 - idx) * _ret_log_gamma(h)) for b, h in chains]
    kv = [bdot(a, part(v, b, h), "tn") for a, (b, h) in zip(kz, chains)]
    outs = [x * jnp.exp((idx + 1.0) * _ret_log_gamma(h)) + part(inner, b, h) for x, (b, h) in zip(cross, chains)]
    new_state = [s * math.exp(CHUNK * _ret_log_gamma(h)) + x for s, x, (b, h) in zip(state, kv, chains)]
    return new_state, jnp.concatenate([jnp.concatenate(outs[b * RET_HEADS:(b + 1) * RET_HEADS], axis=1)[None]
                                       for b in range(bsz)], axis=0)


SCAN_CHUNKS = 2


def chunk_scan(step_fn, ins, state_shape, out_width, name):
    bsz, seq, _ = ins[0].shape
    nchunk = seq // CHUNK
    nin = len(ins)
    nh = state_shape[0]
    per = SCAN_CHUNKS if nchunk % SCAN_CHUNKS == 0 else 1

    def body(*refs):
        in_refs = refs[:nin]
        o_ref, ck_ref, s_ref = refs[nin:]

        @pl.when(pl.program_id(0) == 0)
        def _():
            s_ref[...] = jnp.zeros_like(s_ref)

        state = [s_ref[i] for i in range(bsz * nh)]
        for c in range(per):
            rows = slice(c * CHUNK, (c + 1) * CHUNK)
            for i in range(bsz * nh):
                ck_ref[i // nh, c, i % nh] = state[i]
            state, out = step_fn(state, *[x[:, rows, :].astype(f32) for x in in_refs])
            o_ref[:, rows, :] = out
        for i in range(bsz * nh):
            s_ref[i] = state[i]

    in_specs = [pl.BlockSpec((bsz, per * CHUNK, x.shape[2]), lambda n: (0, n, 0)) for x in ins]
    out_specs = [pl.BlockSpec((bsz, per * CHUNK, out_width), lambda n: (0, n, 0)),
                 pl.BlockSpec((bsz, per) + tuple(state_shape), lambda n: (0, n, 0, 0, 0))]
    out_shape = [jax.ShapeDtypeStruct((bsz, seq, out_width), f32),
                 jax.ShapeDtypeStruct((bsz, nchunk) + tuple(state_shape), f32)]
    return pl.pallas_call(body, name=name, grid=(nchunk // per,), in_specs=in_specs, out_specs=out_specs, out_shape=out_shape,
                          scratch_shapes=[pltpu.VMEM((bsz * nh,) + tuple(state_shape[1:]), f32)],
                          compiler_params=_params())(*ins)


def chunk_scan_bwd(step_fn, ins, ckpt, dout, name):
    bsz, seq, _ = ins[0].shape
    nchunk = seq // CHUNK
    nin = len(ins)
    state_shape = ckpt.shape[2:]
    nh = state_shape[0]
    per = SCAN_CHUNKS if nchunk % SCAN_CHUNKS == 0 else 1
    nstep = nchunk // per

    def body(*refs):
        in_refs = refs[:nin]
        ck_ref, do_ref = refs[nin:nin + 2]
        din_refs = refs[nin + 2:nin + 2 + nin]
        ds_ref = refs[-1]

        @pl.when(pl.program_id(0) == 0)
        def _():
            ds_ref[...] = jnp.zeros_like(ds_ref)

        dstate = [ds_ref[i] for i in range(bsz * nh)]
        for c in reversed(range(per)):
            rows = slice(c * CHUNK, (c + 1) * CHUNK)
            state = [ck_ref[i // nh, c, i % nh] for i in range(bsz * nh)]
            _, vjp = jax.vjp(step_fn, state, *[x[:, rows, :].astype(f32) for x in in_refs])
            grads = vjp((dstate, do_ref[:, rows, :]))
            dstate = grads[0]
            for d_ref, d in zip(din_refs, grads[1:]):
                d_ref[:, rows, :] = d
        for i in range(bsz * nh):
            ds_ref[i] = dstate[i]

    rev = lambda n: (0, nstep - 1 - n, 0)
    in_specs = [pl.BlockSpec((bsz, per * CHUNK, x.shape[2]), rev) for x in ins]
    in_specs += [pl.BlockSpec((bsz, per) + tuple(state_shape), lambda n: (0, nstep - 1 - n, 0, 0, 0)),
                 pl.BlockSpec((bsz, per * CHUNK, dout.shape[2]), rev)]
    out_specs = [pl.BlockSpec((bsz, per * CHUNK, x.shape[2]), rev) for x in ins]
    out_shape = [jax.ShapeDtypeStruct(x.shape, f32) for x in ins]
    return pl.pallas_call(body, name=name, grid=(nstep,), in_specs=in_specs, out_specs=out_specs, out_shape=out_shape,
                          scratch_shapes=[pltpu.VMEM((bsz * nh,) + tuple(state_shape[1:]), f32)],
                          compiler_params=_params())(*ins, ckpt, dout)


LRU_ROWS = 512


def lru_scan(a, b):
    bsz, seq, width = a.shape
    rb = min(LRU_ROWS, seq)
    seqs = range(bsz)

    def body(a_ref, b_ref, h_ref, hp_ref, carry_ref):
        @pl.when(pl.program_id(0) == 0)
        def _():
            carry_ref[...] = jnp.zeros_like(carry_ref)

        row = lax.broadcasted_iota(jnp.int32, (8, width), 0)

        def tile(t, hprev):
            r0 = pl.multiple_of(t * 8, 8)
            ca = [a_ref[i, pl.ds(r0, 8), :] for i in seqs]
            cb = [b_ref[i, pl.ds(r0, 8), :] for i in seqs]
            for s in (1, 2, 4):
                m = row >= s
                cb = [jnp.where(m, x * pltpu.roll(y, s, 0) + y, y) for x, y in zip(ca, cb)]
                ca = [jnp.where(m, x * pltpu.roll(x, s, 0), x) for x in ca]
            h = [y + x * p for x, y, p in zip(ca, cb, hprev)]
            for i in seqs:
                h_ref[i, pl.ds(r0, 8), :] = h[i]
                hp_ref[i, pl.ds(r0, 8), :] = jnp.where(row == 0, hprev[i], pltpu.roll(h[i], 1, 0))
            return tuple(_row_pick(x, 7) for x in h)

        last = lax.fori_loop(0, rb // 8, tile, tuple(carry_ref[i:i + 1, :] for i in seqs))
        for i in seqs:
            carry_ref[i:i + 1, :] = last[i]

    spec = pl.BlockSpec((bsz, rb, width), lambda i: (0, i, 0))
    return pl.pallas_call(body, name="lru_scan", grid=(seq // rb,), in_specs=[spec, spec], out_specs=[spec, spec],
                          out_shape=[jax.ShapeDtypeStruct(a.shape, f32)] * 2,
                          scratch_shapes=[pltpu.VMEM((max(8, bsz), width), f32)], compiler_params=_params())(a, b)


def lru_scan_bwd(a, hp, dh):
    bsz, seq, width = a.shape
    rb = min(LRU_ROWS, seq)
    nblk = seq // rb
    seqs = range(bsz)

    def body(a_ref, hp_ref, dh_ref, da_ref, db_ref, carry_ref):
        @pl.when(pl.program_id(0) == 0)
        def _():
            carry_ref[...] = jnp.zeros_like(carry_ref)

        row = lax.broadcasted_iota(jnp.int32, (8, width), 0)
        ntile = rb // 8

        def tile(t, mu_next):
            r0 = pl.multiple_of((ntile - 1 - t) * 8, 8)
            ca = [a_ref[i, pl.ds(r0, 8), :] for i in seqs]
            dh_t = [dh_ref[i, pl.ds(r0, 8), :] for i in seqs]
            cb = [x * y for x, y in zip(ca, dh_t)]
            for s in (1, 2, 4):
                m = row < 8 - s
                cb = [jnp.where(m, x * pltpu.roll(y, 8 - s, 0) + y, y) for x, y in zip(ca, cb)]
                ca = [jnp.where(m, x * pltpu.roll(x, 8 - s, 0), x) for x in ca]
            mu = [y + x * p for x, y, p in zip(ca, cb, mu_next)]
            for i in seqs:
                lam = dh_t[i] + jnp.where(row == 7, mu_next[i], pltpu.roll(mu[i], 7, 0))
                db_ref[i, pl.ds(r0, 8), :] = lam
                da_ref[i, pl.ds(r0, 8), :] = lam * hp_ref[i, pl.ds(r0, 8), :]
            return tuple(_row_pick(x, 0) for x in mu)

        last = lax.fori_loop(0, ntile, tile, tuple(carry_ref[i:i + 1, :] for i in seqs))
        for i in seqs:
            carry_ref[i:i + 1, :] = last[i]

    spec = pl.BlockSpec((bsz, rb, width), lambda i: (0, nblk - 1 - i, 0))
    return pl.pallas_call(body, name="lru_scan_bwd", grid=(nblk,), in_specs=[spec] * 3, out_specs=[spec, spec],
                          out_shape=[jax.ShapeDtypeStruct(a.shape, f32)] * 2,
                          scratch_shapes=[pltpu.VMEM((max(8, bsz), width), f32)], compiler_params=_params())(a, hp, dh)


MERGE_ROWS = 512


def branch_merge(ys, w_branch, u):
    n = ys[0].shape[0]
    tm = min(MERGE_ROWS, n)

    def body(ya, yb, yc, w_ref, g0, g1, g2, o_ref):
        acc = None
        for i, (y_ref, g_ref) in enumerate(((ya, g0), (yb, g1), (yc, g2))):
            term = jax.nn.sigmoid(g_ref[...]) * _dg(y_ref[...], w_ref[i], "nn")
            acc = term if acc is None else acc + term
        o_ref[...] = acc.astype(o_ref.dtype)

    y_spec = pl.BlockSpec((tm, ys[0].shape[1]), lambda i: (i, 0))
    g_specs = [pl.BlockSpec((tm, D_MODEL), functools.partial(lambda i, k: (i, k), k=k)) for k in range(3)]
    return pl.pallas_call(
        body, name="branch_merge", grid=(n // tm,),
        in_specs=[y_spec] * 3 + [pl.BlockSpec(w_branch.shape, lambda i: (0, 0, 0))] + g_specs,
        out_specs=pl.BlockSpec((tm, D_MODEL), lambda i: (i, 0)), out_shape=jax.ShapeDtypeStruct((n, D_MODEL), bf16),
        compiler_params=_params())(*ys, w_branch, u, u, u)


def branch_merge_bwd(ys, w_branch, u, d_merged, du):
    n = ys[0].shape[0]
    tm = min(MERGE_ROWS, n)

    def body(ya, yb, yc, w_ref, g0, g1, g2, dm_ref, du_in, db0, db1, db2, du_ref):
        dm = dm_ref[...]
        d_gates = []
        for i, (y_ref, g_ref, db_ref) in enumerate(((ya, g0, db0), (yb, g1, db1), (yc, g2, db2))):
            s = jax.nn.sigmoid(g_ref[...])
            db_ref[...] = (dm * s).astype(db_ref.dtype)
            d_gates.append(dm * _dg(y_ref[...], w_ref[i], "nn") * (s * (1.0 - s)))
        du_ref[...] = jnp.concatenate(d_gates, axis=1).astype(du_ref.dtype)

    y_spec = pl.BlockSpec((tm, ys[0].shape[1]), lambda i: (i, 0))
    row = pl.BlockSpec((tm, D_MODEL), lambda i: (i, 0))
    g_specs = [pl.BlockSpec((tm, D_MODEL), functools.partial(lambda i, k: (i, k), k=k)) for k in range(3)]
    res = pl.pallas_call(
        body, name="branch_merge_bwd", grid=(n // tm,),
        in_specs=[y_spec] * 3 + [pl.BlockSpec(w_branch.shape, lambda i: (0, 0, 0))] + g_specs + [row, pl.BlockSpec(memory_space=pl.ANY)],
        out_specs=[row] * 3 + [pl.BlockSpec((tm, 3 * D_MODEL), lambda i: (i, 0))],
        out_shape=[jax.ShapeDtypeStruct((n, D_MODEL), bf16)] * 3 + [jax.ShapeDtypeStruct(du.shape, du.dtype)],
        input_output_aliases={8: 3}, compiler_params=_params())(*ys, w_branch, u, u, u, d_merged, du)
    return list(res[:3]), res[3]


def final_loss(x, g, target):
    n, d = x.shape
    r = min(256, n)

    def body(x_ref, g_ref, t_ref, loss_ref, dx_ref, dg_ref, dx16_ref):
        @pl.when(pl.program_id(0) == 0)
        def _():
            loss_ref[...] = jnp.zeros_like(loss_ref)
            dg_ref[...] = jnp.zeros_like(dg_ref)

        tgt = t_ref[...]

        def loss_fn(xv, gv):
            y = f_norm([xv], [gv])[0]
            return 0.5 * jnp.sum(jnp.mean(jnp.square(y - tgt), axis=-1, keepdims=True), axis=0, keepdims=True)

        val, vjp = jax.vjp(loss_fn, x_ref[...], g_ref[...])
        dx, dg = vjp(jnp.ones_like(val))
        loss_ref[...] += val
        dx_ref[...] = dx
        dx16_ref[...] = dx.astype(dx16_ref.dtype)
        dg_ref[...] += dg

    row = pl.BlockSpec((r, d), lambda i: (i, 0))
    return pl.pallas_call(
        body, name="final_loss", grid=(n // r,), in_specs=[row, pl.BlockSpec((1, d), lambda i: (0, 0)), row],
        out_specs=[pl.BlockSpec((8, LANE), lambda i: (0, 0)), row, pl.BlockSpec((1, d), lambda i: (0, 0)), row],
        out_shape=[jax.ShapeDtypeStruct((8, LANE), f32), jax.ShapeDtypeStruct((n, d), f32), jax.ShapeDtypeStruct((1, d), f32),
                   jax.ShapeDtypeStruct((n, d), bf16)],
        compiler_params=_params())(x, g, target)


_HBM = pl.BlockSpec(memory_space=pltpu.HBM)
_SEM = pl.BlockSpec(memory_space=pltpu.SEMAPHORE)
_EFFECT = pltpu.SideEffectType.DATAFLOW_SIDE_EFFECTING


def _peer(k):
    mx, my, mc = lax.axis_index("x"), lax.axis_index("y"), lax.axis_index("c")
    px, py, pc = (mx + (k >> 2)) % 2, (my + ((k >> 1) & 1)) % 2, (mc + (k & 1)) % 2
    return (px, py, pc), 4 * px + 2 * py + pc


def _peer_copy(k, i, x_ref, land_ref, send_sems, recv_sems, scatter):
    me = 4 * lax.axis_index("x") + 2 * lax.axis_index("y") + lax.axis_index("c")
    dev, slot = _peer(k)
    sem = i * (N_DEV - 1) + k - 1
    return pltpu.make_async_remote_copy(
        src_ref=x_ref.at[slot] if scatter else x_ref, dst_ref=land_ref.at[me], send_sem=send_sems.at[sem],
        recv_sem=recv_sems.at[sem], device_id=dev, device_id_type=pl.DeviceIdType.MESH)


def _own_copy(i, x_ref, land_ref, own_sems, scatter):
    me = 4 * lax.axis_index("x") + 2 * lax.axis_index("y") + lax.axis_index("c")
    return pltpu.make_async_copy(x_ref.at[me] if scatter else x_ref, land_ref.at[me], own_sems.at[i])


def exchange_start(xs, scatters, name):
    nx = len(xs)
    lands = [lax.empty((N_DEV,) + tuple(x.shape[1:] if sc else x.shape), x.dtype) for x, sc in zip(xs, scatters)]
    nsem = nx * (N_DEV - 1)

    def body(*refs):
        x_refs, land_refs = refs[:nx], refs[nx:2 * nx]
        send_sems, recv_sems, own_sems = refs[2 * nx:2 * nx + 3]
        token = refs[-1]
        for i in range(nx):
            for k in range(1, N_DEV):
                _peer_copy(k, i, x_refs[i], land_refs[i], send_sems, recv_sems, scatters[i]).start()
            _own_copy(i, x_refs[i], land_refs[i], own_sems, scatters[i]).start()
        token[...] = jnp.zeros_like(token)

    hbm = lambda a: pltpu.HBM(a.shape, a.dtype)
    res = pl.pallas_call(
        body, name=name, in_specs=(_HBM,) * (2 * nx),
        out_specs=(_SEM, _SEM, _SEM) + (_HBM,) * (2 * nx) + (pl.BlockSpec(memory_space=pltpu.VMEM),),
        input_output_aliases={i: 3 + i for i in range(2 * nx)},
        out_shape=(pltpu.SemaphoreType.DMA((nsem,)), pltpu.SemaphoreType.DMA((nsem,)), pltpu.SemaphoreType.DMA((nx,)),
                   *[hbm(a) for a in xs], *[hbm(a) for a in lands], jax.ShapeDtypeStruct((8, LANE), f32)),
        compiler_params=pltpu.CompilerParams(has_side_effects=_EFFECT),
    )(*[pltpu.with_memory_space_constraint(a, pltpu.HBM) for a in list(xs) + lands])
    return (res[0], res[1], res[2], list(res[3:3 + nx]), list(res[3 + nx:3 + 2 * nx]), tuple(scatters)), res[-1]


def exchange_wait(started, after, name):
    send_sems, recv_sems, own_sems, x_thrus, land_thrus, scatters = started
    nx = len(x_thrus)

    def body(*refs):
        x_refs, land_refs = refs[:nx], refs[nx:2 * nx]
        send_sems, recv_sems, own_sems = refs[2 * nx:2 * nx + 3]
        for i in range(nx):
            for k in range(1, N_DEV):
                cp = _peer_copy(k, i, x_refs[i], land_refs[i], send_sems, recv_sems, scatters[i])
                cp.wait_send()
                cp.wait_recv()
            _own_copy(i, x_refs[i], land_refs[i], own_sems, scatters[i]).wait()

    hbm = lambda a: pltpu.HBM(a.shape, a.dtype)
    res = pl.pallas_call(
        body, name=name, in_specs=(_HBM,) * (2 * nx) + (_SEM, _SEM, _SEM, pl.BlockSpec(memory_space=pl.ANY)),
        out_specs=(_HBM,) * (2 * nx), input_output_aliases={i: i for i in range(2 * nx)},
        out_shape=tuple(hbm(a) for a in list(x_thrus) + list(land_thrus)),
        compiler_params=pltpu.CompilerParams(has_side_effects=_EFFECT),
    )(*x_thrus, *land_thrus, send_sems, recv_sems, own_sems, after)
    return list(res[nx:])


def sum_slots(x, name):
    _, rows_total, cols = x.shape
    row_bytes = N_DEV * ((cols + LANE - 1) // LANE) * LANE * x.dtype.itemsize
    r = _pick_rows(rows_total, max(16, (4 * 1024 * 1024) // row_bytes // 16 * 16))

    def body(x_ref, o_ref):
        acc = x_ref[0].astype(f32)
        for s in range(1, N_DEV):
            acc = acc + x_ref[s].astype(f32)
        o_ref[...] = acc

    return pl.pallas_call(body, name=name, grid=(rows_total // r,),
                          in_specs=[pl.BlockSpec((N_DEV, r, cols), lambda i: (0, i, 0))],
                          out_specs=pl.BlockSpec((r, cols), lambda i: (i, 0)),
                          out_shape=jax.ShapeDtypeStruct((rows_total, cols), f32), compiler_params=_params())(x)


def _pick_rows(total, pref):
    best = None
    for d in range(16, min(total, pref) + 1, 16):
        if total % d == 0:
            best = d
    return best if best is not None else total


def adamw(w, g, m, v, name):
    shape = w.shape
    if w.ndim == 1:
        w2, g2, m2, v2 = (t.reshape(1, -1) for t in (w, g, m, v))
    else:
        w2, g2, m2, v2 = (t.reshape(-1, shape[-1]) for t in (w, g, m, v))
    rows_total, cols = w2.shape
    r = _pick_rows(rows_total, max(16, (512 * 1024) // max(cols, 1) // 16 * 16))
    c1, c2 = 1.0 / (1.0 - ADAM_B1 ** ADAM_STEP), 1.0 / (1.0 - ADAM_B2 ** ADAM_STEP)

    def body(w_ref, g_ref, m_ref, v_ref, d_ref, nm_ref, nv_ref):
        gv = g_ref[...]
        nm = ADAM_B1 * m_ref[...] + (1.0 - ADAM_B1) * gv
        nv = ADAM_B2 * v_ref[...] + (1.0 - ADAM_B2) * jnp.square(gv)
        d_ref[...] = -ADAM_LR * ((nm * c1) / (jnp.sqrt(nv * c2) + ADAM_EPS) + ADAM_WD * w_ref[...])
        nm_ref[...] = nm
        nv_ref[...] = nv

    spec = pl.BlockSpec((r, cols), lambda i: (i, 0))
    outs = pl.pallas_call(body, name=name, grid=(rows_total // r,), in_specs=[spec] * 4, out_specs=[spec] * 3,
                          out_shape=[jax.ShapeDtypeStruct((rows_total, cols), f32)] * 3, compiler_params=_params())(w2, g2, m2, v2)
    return tuple(o.reshape(shape) for o in outs)


def _const(j):
    return lambda _: j


def _layer_fwd(x, wl, fetch_rest, cos, sin, bsz, seq):
    n = x.shape[0]
    sv = {"x_in": x}
    row1 = lambda a: (a, (1, a.shape[1]), lambda j: (0, 0))
    h = rowmap(f_norm, [(x, D_MODEL, 0)], [row1(wl["attn_norm"])], [(D_MODEL, bf16)], 1, "norm_fwd")[0]
    u = mm(h, wl["w_in"], "nn", "mm_in")
    sv["h"], sv["u"] = h, u
    u3 = u.reshape(bsz, seq, U_PAD)
    wl = dict(wl)
    wl.update(fetch_rest(u))
    sv["wl"] = wl

    qkv = []
    for kind in range(3):
        cw = (wl["dn_conv_w"], (4, LANE), functools.partial(lambda j, kind: (0, 4 * kind + j), kind=kind))
        qkv.append(seqmap(functools.partial(f_dn_pre, kind), [(u3, U_QKV // LANE + 4 * kind)], [cw], 1, 4, "dn_pre%d" % kind)[0])
    gb = rowmap(f_dn_gates, [(u, 512, U_AB // 512)], [(wl["dn_gate_p"], (8, LANE), lambda j: (0, 0))], [(LANE, f32)], 1,
                "dn_gates", rows=512)[0]
    gb3 = gb.reshape(bsz, seq, LANE)
    crow = CHUNK * CHUNKS_PER_STEP
    dn_in = [(t.reshape(n, 512), 512, 0) for t in qkv] + [(gb, LANE, 0)]
    prep_a = rowmap(dn_prep, dn_in, [], [(512, f32)] + [(512, bf16)] * 4 + [(LANE, f32), (512, f32)], 1, "dn_prep", rows=crow)
    dn_in = dn_in + [(prep_a[6], 512, 0)]
    prep_a = [t.reshape(bsz, seq, t.shape[1]) for t in prep_a[:6]]
    o_a, ck_a = chunk_scan(dn_step, prep_a, (DN_HEADS, DN_DK, DN_DK), 512, "dn_scan")
    y_a = rowmap(per_head(f_dn_post), [(o_a.reshape(n, 512), 512, 0), (u, 512, U_Z // 512)],
                 [(wl["dn_norm_w"], (1, LANE), lambda j: (0, 0))], [(512, bf16)], 1, "dn_post")[0]
    sv.update(dn_in=dn_in, prep_a=prep_a, o_a=o_a, ck_a=ck_a, y_a=y_a)

    q_b, k_b = rowmap(f_ret_pre, [(u, 256, U_RQ // 256), (u, 256, U_RK // 256), (cos, 256, 0), (sin, 256, 0)], [],
                      [(256, f32), (256, f32)], 1, "ret_pre")
    q_b3, k_b3 = q_b.reshape(bsz, seq, 256), k_b.reshape(bsz, seq, 256)
    v_b3 = lax.slice_in_dim(u3, U_RV, U_RV + 512, axis=2)
    ret_in = [(q_b, 256, 0), (k_b, 256, 0), (u, 512, U_RV // 512)]
    inner = rowmap(ret_prep, ret_in, [], [(512, f32)], 1, "ret_prep", rows=crow)[0]
    ret_seq = [q_b3, k_b3, v_b3, inner.reshape(bsz, seq, 512)]
    o_b, ck_b = chunk_scan(ret_step, ret_seq, (RET_HEADS, 256, RET_DV), 512, "ret_scan")
    y_b = rowmap(per_head(f_ret_post), [(o_b.reshape(n, 512), 512, 0), (u, 512, U_RG // 512)], [], [(512, bf16)], 1,
                 "ret_post")[0]
    sv.update(ret_in=ret_in, ret_seq=ret_seq, o_b=o_b, ck_b=ck_b, y_b=y_b)

    lru_params = _lru_params(wl)
    a_c, b_c = seqmap(f_lru_pre, [(u3, U_CX // LANE)], lru_params, 2, 4, "lru_pre")
    h_c, hp_c = lru_scan(a_c, b_c)
    y_c = rowmap(f_lru_post, [(h_c.reshape(n, 512), 512, 0), (u, 512, U_CG // 512)], [], [(512, bf16)], 1, "lru_post")[0]
    sv.update(a_c=a_c, hp_c=hp_c, h_c=h_c, y_c=y_c)

    merged = branch_merge((y_a, y_b, y_c), wl["w_branch"], u)
    x_mid = mm(merged, wl["w_out"], "nn", "mm_out", add=x)
    sv.update(merged=merged, x_mid=x_mid)

    h2 = rowmap(f_norm, [(x_mid, D_MODEL, 0)], [row1(wl["ffn_norm"])], [(D_MODEL, bf16)], 1, "norm_fwd")[0]
    up = mm(h2, wl["w_up"], "nn", "mm_up")
    act = seqmap(f_ffn_mid, [(up.reshape(bsz, seq, 2 * D_FF), 0), (up.reshape(bsz, seq, 2 * D_FF), D_FF // LANE)],
                 _ffn_params(wl), 1, D_FF // LANE, "ffn_mid", out_dtype=bf16)[0]
    act = act.reshape(n, D_FF)
    x_out = mm(act, wl["w_down"], "nn", "mm_down", add=x_mid)
    sv.update(h2=h2, up=up, act=act)
    return x_out, sv


def _lru_params(wl):
    col = lambda a: (a, (a.shape[0], LANE), lambda j: (0, j))
    blk = lambda a: (a, (None, LANE, LANE), lambda j: (j, 0, 0))
    return [col(wl["lru_conv_w"]), col(wl["lru_conv_b"]), blk(wl["lru_wa"]), col(wl["lru_ba"]), blk(wl["lru_wx"]),
            col(wl["lru_bx"]), col(wl["lru_lambda"])]


def _ffn_params(wl):
    nb = D_FF // LANE
    return [(wl["ffn_conv_w"], (3, LANE), lambda j: (0, j)), (wl["ffn_conv_w"], (3, LANE), lambda j: (0, nb + j)),
            (wl["ffn_conv_b"], (1, LANE), lambda j: (0, j)), (wl["ffn_conv_b"], (1, LANE), lambda j: (0, nb + j))]


def _layer_bwd(dx, dx16, sv, cos, sin, bsz, seq, emit, dep):
    n = dx.shape[0]
    gr = {}
    wl = sv["wl"]
    u, x_in, x_mid = sv["u"], sv["x_in"], sv["x_mid"]
    u3 = u.reshape(bsz, seq, U_PAD)
    row1 = lambda a: (a, (1, a.shape[1]), lambda j: (0, 0))

    d_act = mm(dx16, wl["w_down"], "nt", "mm_down_dx", dep=dep)
    gr["w_down"] = mm(sv["act"], dx16, "tn", "mm_down_dw")
    up3 = sv["up"].reshape(bsz, seq, 2 * D_FF)
    (d_gate, d_val), dps = seqmap_bwd(f_ffn_mid, [(up3, 0), (up3, D_FF // LANE)], _ffn_params(wl),
                                      [d_act.reshape(bsz, seq, D_FF)], D_FF // LANE, "ffn_mid_bwd", din_dtype=bf16)
    gr["ffn_conv_w"] = jnp.concatenate([_cols(dps[0]), _cols(dps[1])], axis=1)
    gr["ffn_conv_b"] = jnp.concatenate([_cols(dps[2]), _cols(dps[3])], axis=1)[0]
    d_gate, d_val = d_gate.reshape(n, D_FF), d_val.reshape(n, D_FF)
    gr["w_up"] = (mm(sv["h2"], d_gate, "tn", "mm_up_dw"), mm(sv["h2"], d_val, "tn", "mm_up_dw"))
    token = emit("ffn", {k: gr[k] for k in ("w_up", "w_down")})
    d_h2 = mm(d_gate, wl["w_up"], "nt", "mm_up_dx", dep=token)
    d_h2 = mm(d_val, wl["w_up"], "nt", "mm_up_dx", add=d_h2, b_koff=1)
    (dx_mid,), (dg,), ex = rowmap_bwd(f_norm, [(x_mid, D_MODEL, 0)], [row1(wl["ffn_norm"])], [d_h2], 1, "norm_bwd", add=[dx],
                                      copy16=0)
    dx_mid16 = ex["copy16"]
    gr["ffn_norm"] = dg[0, 0]

    du = lax.empty((n, U_PAD), bf16)
    du3 = lambda: du.reshape(bsz, seq, U_PAD)

    d_merged = mm(dx_mid16, wl["w_out"], "nt", "mm_out_dx")
    gr["w_out"] = mm(sv["merged"], dx_mid16, "tn", "mm_out_dw")
    ys = (sv["y_a"], sv["y_b"], sv["y_c"])
    d_br, du = branch_merge_bwd(ys, wl["w_branch"], u, d_merged, du)
    d_ys = [mm(d_br[i], wl["w_branch"][i], "nt", "mm_branch_dx") for i in range(3)]
    gr["w_branch"] = jnp.stack([mm(ys[i], d_br[i], "tn", "mm_branch_dw") for i in range(3)])

    (d_hc, _), _, ex = rowmap_bwd(f_lru_post, [(sv["h_c"].reshape(n, 512), 512, 0), (u, 512, U_CG // 512)], [], [d_ys[2]], 1,
                                  "lru_post_bwd", into=(du, U_CG // 512, [1]))
    du = ex["into"]
    d_a, d_b = lru_scan_bwd(sv["a_c"], sv["hp_c"], d_hc.reshape(bsz, seq, 512))
    (du_new,), dps = seqmap_bwd(f_lru_pre, [(u3, U_CX // LANE)], _lru_params(wl), [d_a, d_b], 4, "lru_pre_bwd", din_dtype=bf16,
                                into=(du3(), U_CX // LANE))
    du = du_new.reshape(n, U_PAD)
    gr["lru_conv_w"], gr["lru_conv_b"] = _cols(dps[0]), _cols(dps[1])[0]
    gr["lru_wa"], gr["lru_ba"], gr["lru_wx"], gr["lru_bx"] = dps[2], dps[3][:, 0], dps[4], dps[5][:, 0]
    gr["lru_lambda"] = _cols(dps[6])[0]

    (d_ob, _), _, ex = rowmap_bwd(per_head(f_ret_post), [(sv["o_b"].reshape(n, 512), 512, 0), (u, 512, U_RG // 512)], [],
                                  [d_ys[1]], 1, "ret_post_bwd", into=(du, U_RG // 512, [1]))
    du = ex["into"]
    crow = CHUNK * CHUNKS_PER_STEP
    d_ret = chunk_scan_bwd(ret_step, sv["ret_seq"], sv["ck_b"], d_ob.reshape(bsz, seq, 512), "ret_scan_bwd")
    d_ret = [t.reshape(n, t.shape[2]) for t in d_ret]
    (d_qb, d_kb, _), _, ex = rowmap_bwd(ret_prep, sv["ret_in"], [], [d_ret[3]], 1, "ret_prep_bwd", rows=crow, add=d_ret[:3],
                                        into=(du, U_RV // 512, [2]))
    du = ex["into"]
    _, _, ex = rowmap_bwd(f_ret_pre, [(u, 256, U_RQ // 256), (u, 256, U_RK // 256), (cos, 256, 0), (sin, 256, 0)], [],
                          [d_qb, d_kb], 1, "ret_pre_bwd", din_dtypes=[f32, f32, None, None], into=(du, U_RQ // 512, [0, 1]))
    du = ex["into"]

    (d_oa, _), (dnw,), ex = rowmap_bwd(per_head(f_dn_post), [(sv["o_a"].reshape(n, 512), 512, 0), (u, 512, U_Z // 512)],
                                       [(wl["dn_norm_w"], (1, LANE), lambda j: (0, 0))], [d_ys[0]], 1, "dn_post_bwd",
                                       into=(du, U_Z // 512, [1]))
    du = ex["into"]
    gr["dn_norm_w"] = dnw[0, 0]
    d_prep = chunk_scan_bwd(dn_step, sv["prep_a"], sv["ck_a"], d_oa.reshape(bsz, seq, 512), "dn_scan_bwd")
    (d_q, d_k, d_v, d_gb, _), _, _ = rowmap_bwd(dn_prep, sv["dn_in"], [], [t.reshape(n, t.shape[2]) for t in d_prep], 1,
                                                "dn_prep_bwd", rows=crow, din_dtypes=[f32] * 4 + [None])
    d_q, d_k, d_v = (t.reshape(bsz, seq, 512) for t in (d_q, d_k, d_v))
    _, (dgp,), ex = rowmap_bwd(f_dn_gates, [(u, 512, U_AB // 512)], [(wl["dn_gate_p"], (8, LANE), lambda j: (0, 0))],
                               [d_gb], 1, "dn_gates_bwd", rows=512, into=(du, U_AB // 512, [0]))
    du = ex["into"]
    gr["dn_a_log"], gr["dn_dt_bias"] = dgp[0, 0, :DN_HEADS], dgp[0, 1, :DN_HEADS]
    d_cw = []
    for kind, d_t in enumerate((d_q, d_k, d_v)):
        cw = (wl["dn_conv_w"], (4, LANE), functools.partial(lambda j, kind: (0, 4 * kind + j), kind=kind))
        (du_new,), (dcw,) = seqmap_bwd(functools.partial(f_dn_pre, kind), [(u3, U_QKV // LANE + 4 * kind)], [cw], [d_t], 4,
                                       "dn_pre%d_bwd" % kind, din_dtype=bf16, into=(du3(), U_QKV // LANE + 4 * kind))
        du = du_new.reshape(n, U_PAD)
        d_cw.append(_cols(dcw))
    gr["dn_conv_w"] = jnp.concatenate(d_cw, axis=1)

    gr["w_in"] = _unpad_w_in(mm(sv["h"], du, "tn", "mm_in_dw"))
    token = emit("mix", {k: gr[k] for k in ("w_in", "w_branch", "w_out")})
    d_h = mm(du, wl["w_in"], "nt", "mm_in_dx", dep=token)
    (dx_in,), (dg,), ex = rowmap_bwd(f_norm, [(x_in, D_MODEL, 0)], [row1(wl["attn_norm"])], [d_h], 1, "norm_bwd", add=[dx_mid],
                                     copy16=0)
    gr["attn_norm"] = dg[0, 0]
    big = ("w_in", "w_branch", "w_out", "w_up", "w_down")
    return dx_in, ex["copy16"], emit("small", {k: g for k, g in gr.items() if k not in big})


def _cols(dp):
    ncol, p, _ = dp.shape
    return jnp.transpose(dp, (1, 0, 2)).reshape(p, ncol * LANE)


def _pad_w_in(w):
    segs = sorted(_IN_SEGS, key=lambda s: s[2])
    parts = [lax.slice_in_dim(w, src, src + width, axis=1) for src, width, _ in segs]
    end = segs[-1][2] + segs[-1][1]
    return jnp.concatenate(parts + [jnp.zeros((w.shape[0], U_PAD - end), w.dtype)], axis=1)


def _unpad_w_in(wp):
    return jnp.concatenate([lax.slice_in_dim(wp, dst, dst + width, axis=1) for _, width, dst in _IN_SEGS], axis=1)


def _rope_tables(positions):
    half = RET_DK // 2
    inv = ROPE_BASE ** (-jnp.arange(half, dtype=f32) / half)
    ang = positions.astype(f32).reshape(-1, 1) * inv
    cos, sin = jnp.cos(ang), jnp.sin(ang)
    return jnp.tile(cos, (1, 2 * RET_HEADS)), jnp.tile(sin, (1, 2 * RET_HEADS))


def _layer_weights(lw):
    wl = {}
    wl["w_in"] = _pad_w_in(lw["w_in"])
    for k in ("dn_conv_w", "lru_conv_w", "ffn_conv_w", "lru_wa", "lru_wx"):
        wl[k] = lw[k]
    for k in ("attn_norm", "ffn_norm", "dn_norm_w", "lru_conv_b", "lru_lambda", "ffn_conv_b", "lru_ba", "lru_bx"):
        wl[k] = lw[k].reshape(1, -1)
    gp = jnp.zeros((8, LANE), f32)
    wl["dn_gate_p"] = gp.at[0, :DN_HEADS].set(lw["dn_a_log"]).at[1, :DN_HEADS].set(lw["dn_dt_bias"])
    return wl


REST = ("w_branch", "w_out", "w_up", "w_down")


def forward_backward(x, positions, target, layer_weights, final_norm, on_head, on_grads):
    bsz, seq, d = x.shape
    n = bsz * seq
    cos, sin = _rope_tables(positions)
    xs = x.reshape(n, d)
    saved = []
    for layer in range(DEPTH):
        first, fetch_rest = layer_weights(layer, xs)
        xs, sv = _layer_fwd(xs, _layer_weights(first), fetch_rest, cos, sin, bsz, seq)
        saved.append(sv)
    loss, dx, d_final, dx16 = final_loss(xs, final_norm.reshape(1, d), target.reshape(n, d))
    on_head(loss[0, 0], d_final[0])
    token = None
    for layer in reversed(range(DEPTH)):
        dx, dx16, token = _layer_bwd(dx, dx16, saved[layer], cos, sin, bsz, seq, functools.partial(on_grads, layer), token)
    return dx.reshape(bsz, seq, d)


def local_step(x, positions, target, full):
    grads, head = {layer: {} for layer in range(DEPTH)}, {}

    def layer_weights(layer, _):
        return ({k: a[layer] for k, a in full.items() if k != "final_norm" and k not in REST},
                lambda after: {k: full[k][layer] for k in REST})

    gx = forward_backward(x, positions, target, layer_weights, full["final_norm"],
                          lambda loss, d_final: head.update(loss=loss, d_final=d_final),
                          lambda layer, group, gr: grads[layer].update(
                              {k: jnp.concatenate(g, axis=1) if isinstance(g, tuple) else g for k, g in gr.items()}))
    stacked = {k: jnp.stack([grads[layer][k] for layer in range(DEPTH)]) for k in grads[0]}
    stacked["final_norm"] = head["d_final"]
    return head["loss"], gx, stacked


BIG = (("w_in", 2), ("w_branch", 3), ("w_out", 1), ("w_up", 2), ("w_down", 1))
SMALL_SHARDED = (("dn_conv_w", 2), ("lru_conv_w", 2), ("ffn_conv_w", 2))
REPLICATED = ("attn_norm", "dn_a_log", "dn_dt_bias", "dn_norm_w", "lru_conv_b", "lru_wa", "lru_ba", "lru_wx", "lru_bx",
              "lru_lambda", "ffn_norm", "ffn_conv_b", "final_norm")
WEIGHTS = ("attn_norm", "w_in", "dn_conv_w", "dn_a_log", "dn_dt_bias", "dn_norm_w", "lru_conv_w", "lru_conv_b", "lru_wa",
           "lru_ba", "lru_wx", "lru_bx", "lru_lambda", "w_branch", "w_out", "ffn_norm", "w_up", "ffn_conv_w", "ffn_conv_b",
           "w_down", "final_norm")


def _pack(arrs, dtype, align=16 * LANE):
    flat = jnp.concatenate([a.reshape(-1).astype(dtype) for a in arrs])
    pad = (-flat.shape[0]) % align
    return jnp.pad(flat, (0, pad)).reshape(-1, LANE)


def _unpack(rows, shapes):
    flat = rows.reshape(-1)
    out, pos = [], 0
    for shp in shapes:
        size = math.prod(shp)
        out.append(lax.slice_in_dim(flat, pos, pos + size).reshape(shp))
        pos += size
    return out


def kernel(x, positions, attn_norm, w_in, dn_conv_w, dn_a_log, dn_dt_bias, dn_norm_w, lru_conv_w, lru_conv_b, lru_wa, lru_ba, lru_wx, lru_bx, lru_lambda, w_branch, w_out, ffn_norm, w_up, ffn_conv_w, ffn_conv_b, w_down, final_norm, loss_target, m_attn_norm, m_w_in, m_dn_conv_w, m_dn_a_log, m_dn_dt_bias, m_dn_norm_w, m_lru_conv_w, m_lru_conv_b, m_lru_wa, m_lru_ba, m_lru_wx, m_lru_bx, m_lru_lambda, m_w_branch, m_w_out, m_ffn_norm, m_w_up, m_ffn_conv_w, m_ffn_conv_b, m_w_down, m_final_norm, v_attn_norm, v_w_in, v_dn_conv_w, v_dn_a_log, v_dn_dt_bias, v_dn_norm_w, v_lru_conv_w, v_lru_conv_b, v_lru_wa, v_lru_ba, v_lru_wx, v_lru_bx, v_lru_lambda, v_w_branch, v_w_out, v_ffn_norm, v_w_up, v_ffn_conv_w, v_ffn_conv_b, v_w_down, v_final_norm):
    w = dict(attn_norm=attn_norm, w_in=w_in, dn_conv_w=dn_conv_w, dn_a_log=dn_a_log, dn_dt_bias=dn_dt_bias, dn_norm_w=dn_norm_w,
             lru_conv_w=lru_conv_w, lru_conv_b=lru_conv_b, lru_wa=lru_wa, lru_ba=lru_ba, lru_wx=lru_wx, lru_bx=lru_bx,
             lru_lambda=lru_lambda, w_branch=w_branch, w_out=w_out, ffn_norm=ffn_norm, w_up=w_up, ffn_conv_w=ffn_conv_w,
             ffn_conv_b=ffn_conv_b, w_down=w_down, final_norm=final_norm)
    m = dict(attn_norm=m_attn_norm, w_in=m_w_in, dn_conv_w=m_dn_conv_w, dn_a_log=m_dn_a_log, dn_dt_bias=m_dn_dt_bias,
             dn_norm_w=m_dn_norm_w, lru_conv_w=m_lru_conv_w, lru_conv_b=m_lru_conv_b, lru_wa=m_lru_wa, lru_ba=m_lru_ba,
             lru_wx=m_lru_wx, lru_bx=m_lru_bx, lru_lambda=m_lru_lambda, w_branch=m_w_branch, w_out=m_w_out, ffn_norm=m_ffn_norm,
             w_up=m_w_up, ffn_conv_w=m_ffn_conv_w, ffn_conv_b=m_ffn_conv_b, w_down=m_w_down, final_norm=m_final_norm)
    v = dict(attn_norm=v_attn_norm, w_in=v_w_in, dn_conv_w=v_dn_conv_w, dn_a_log=v_dn_a_log, dn_dt_bias=v_dn_dt_bias,
             dn_norm_w=v_dn_norm_w, lru_conv_w=v_lru_conv_w, lru_conv_b=v_lru_conv_b, lru_wa=v_lru_wa, lru_ba=v_lru_ba,
             lru_wx=v_lru_wx, lru_bx=v_lru_bx, lru_lambda=v_lru_lambda, w_branch=v_w_branch, w_out=v_w_out, ffn_norm=v_ffn_norm,
             w_up=v_w_up, ffn_conv_w=v_ffn_conv_w, ffn_conv_b=v_ffn_conv_b, w_down=v_w_down, final_norm=v_final_norm)

    me = 4 * lax.axis_index("x") + 2 * lax.axis_index("y") + lax.axis_index("c")
    axes = dict(BIG + SMALL_SHARDED)
    conv_names = [k for k, _ in SMALL_SHARDED]

    gathers, tokens, conv_full = {}, [], {}
    for layer in range(DEPTH):
        first = [w["w_in"][layer].astype(bf16)] + ([w[k] for k in conv_names] if layer == 0 else [])
        rest = [w[k][layer].astype(bf16) for k in REST]
        for part, srcs in (("in", first), ("rest", rest)):
            gathers[layer, part], token = exchange_start(srcs, [False] * len(srcs), "gather_%s_start%d" % (part, layer))
            tokens.append(token[0:1, 0:1])
    all_started = functools.reduce(lambda a, b: a + b, tokens)

    def join(land, axis):
        if axis == 0:
            return land.reshape((N_DEV * land.shape[1],) + land.shape[2:])
        return jnp.concatenate([land[p] for p in range(N_DEV)], axis=axis)

    def split(g, axis):
        if isinstance(g, tuple):
            each = N_DEV // len(g)
            size = g[0].shape[axis] // each
            return jnp.stack([lax.slice_in_dim(piece, p * size, (p + 1) * size, axis=axis) for piece in g for p in range(each)])
        size = g.shape[axis] // N_DEV
        if axis == 0:
            return g.reshape((N_DEV, size) + g.shape[1:])
        return jnp.stack([lax.slice_in_dim(g, p * size, (p + 1) * size, axis=axis) for p in range(N_DEV)])

    def layer_weights(layer, x_in):
        lands = exchange_wait(gathers[layer, "in"], x_in, "gather_in_wait%d" % layer)
        lw = {"w_in": join(lands[0], 1)}
        if layer == 0:
            conv_full.update({k: join(lands[1 + i], axes[k]) for i, k in enumerate(conv_names)})
        lw.update({k: conv_full[k][layer] for k in conv_names})
        lw.update({k: w[k][layer] for k in REPLICATED if k != "final_norm"})
        if layer == 0:
            lw["attn_norm"] = lw["attn_norm"] + all_started[0]

        def fetch_rest(after):
            lands_r = exchange_wait(gathers[layer, "rest"], after, "gather_rest_wait%d" % layer)
            return {k: join(lands_r[i], axes[k] - 1) for i, k in enumerate(REST)}

        return lw, fetch_rest

    small_names = conv_names + [k for k in REPLICATED if k != "final_norm"]
    groups = {"ffn": ("w_up", "w_down"), "mix": ("w_in", "w_branch", "w_out")}
    scatters, small_shapes, head = {}, {}, {}

    def on_grads(layer, group, gr):
        if group == "small":
            small_shapes.update({k: gr[k].shape for k in small_names})
            srcs = [_pack([gr[k] for k in small_names], f32)]
            srcs += [_pack([head["loss"].reshape(1), head["d_final"]], f32)] if layer == DEPTH - 1 else []
            modes = [False] * len(srcs)
        else:
            srcs = [split(gr[k], axes[k] - 1).astype(bf16) for k in groups[group]]
            modes = [True] * len(srcs)
        scatters[layer, group], token = exchange_start(srcs, modes, "scatter_%s_start%d" % (group, layer))
        return token

    grad_x = forward_backward(x, positions, loss_target, layer_weights, final_norm,
                              lambda loss_part, d_final: head.update(loss=loss_part, d_final=d_final), on_grads)

    big_sums, small_sums = {}, {}
    for group in ("ffn", "mix"):
        for layer in reversed(range(DEPTH)):
            lands = exchange_wait(scatters[layer, group], grad_x, "scatter_%s_wait%d" % (group, layer))
            for i, k in enumerate(groups[group]):
                shard = w[k].shape[1:]
                big_sums[layer, k] = sum_slots(lands[i].reshape(N_DEV, -1, shard[-1]), "sum_" + k).reshape(shard)
    for layer in reversed(range(DEPTH)):
        lands = exchange_wait(scatters[layer, "small"], grad_x, "scatter_small_wait%d" % layer)
        small_sums[layer] = sum_slots(lands[0], "sum_small")
        if layer == DEPTH - 1:
            head_sum = _unpack(sum_slots(lands[1], "sum_head"), [(1,), final_norm.shape])
    grads = {k: jnp.stack([big_sums[layer, k] for layer in range(DEPTH)]) for k, _ in BIG}
    loss, grads["final_norm"] = head_sum[0][0], head_sum[1]
    small_flat = jnp.stack([small_sums[layer] for layer in range(DEPTH)]).reshape(DEPTH, -1)
    pos = 0
    for k in small_names:
        size = math.prod(small_shapes[k])
        g = lax.slice_in_dim(small_flat, pos, pos + size, axis=1).reshape((DEPTH,) + small_shapes[k])
        pos += size
        ax = dict(SMALL_SHARDED).get(k)
        if ax is None:
            grads[k] = g
        else:
            size = g.shape[ax] // N_DEV
            grads[k] = lax.dynamic_slice_in_dim(g, me * size, size, axis=ax)

    upd = {k: adamw(w[k], grads[k], m[k], v[k], "adamw_" + k) for k in WEIGHTS}
    return (loss, grad_x, *[grads[k] for k in WEIGHTS], *[upd[k][0] for k in WEIGHTS], *[upd[k][1] for k in WEIGHTS],
            *[upd[k][2] for k in WEIGHTS])
```

```python
import functools
import math

import jax
import jax.numpy as jnp
from jax import lax
from jax.experimental import pallas as pl
from jax.experimental.pallas import tpu as pltpu

f32 = jnp.float32
bf16 = jnp.bfloat16

D_MODEL = 1024
DEPTH = 4
CHUNK = 64
EPS = 1e-6
DN_HEADS, DN_DK = 4, 128
RET_HEADS, RET_DK, RET_DV = 4, 64, 128
ROPE_BASE = 10000.0
LRU_C = 8.0
D_FF = 2816
N_DEV = 8
LANE = 128
VMEM_LIMIT = 56 * 1024 * 1024

ADAM_LR, ADAM_B1, ADAM_B2, ADAM_EPS, ADAM_WD, ADAM_STEP = 0.001, 0.9, 0.999, 1e-8, 0.01, 10

U_GATES, U_QKV, U_RV, U_RG, U_Z, U_CX, U_CG, U_RQ, U_RK, U_AB = (
    0, 3072, 4608, 5120, 5632, 6144, 6656, 7168, 7424, 7680)
U_PAD = 8192
_IN_SEGS = ((0, 1536, U_QKV), (1536, 8, U_AB), (1544, 512, U_Z), (2056, 256, U_RQ), (2312, 256, U_RK),
            (2568, 512, U_RV), (3080, 512, U_RG), (3592, 512, U_CX), (4104, 512, U_CG), (4616, 3072, U_GATES))
N_IN = 7688


def _params():
    return pltpu.CompilerParams(vmem_limit_bytes=VMEM_LIMIT)


def _pick(dim, pref):
    best = None
    for d in range(LANE, min(dim, pref) + 1, LANE):
        if dim % d == 0:
            best = d
    return best if best is not None else dim


@functools.partial(jax.custom_vjp, nondiff_argnums=(1, 2))
def sroll(x, shift, axis):
    return pltpu.roll(x, shift, axis)


def _sroll_fwd(x, shift, axis):
    return pltpu.roll(x, shift, axis), None


def _sroll_bwd(shift, axis, _, g):
    n = g.shape[axis]
    return (pltpu.roll(g, (n - shift) % n, axis),)


sroll.defvjp(_sroll_fwd, _sroll_bwd)

_DIMS = {"nn": (((1,), (0,)), ((), ())), "nt": (((1,), (1,)), ((), ())), "tn": (((0,), (0,)), ((), ()))}


def _dg(a, b, dims):
    return lax.dot_general(a.astype(bf16), b.astype(bf16), _DIMS[dims], preferred_element_type=f32)


@functools.partial(jax.custom_vjp, nondiff_argnums=(2,))
def bdot(a, b, dims):
    return _dg(a, b, dims)


def _bdot_fwd(a, b, dims):
    return _dg(a, b, dims), (a.astype(bf16), b.astype(bf16))


def _bdot_bwd(dims, res, g):
    a, b = res
    if dims == "nn":
        return _dg(g, b, "nt"), _dg(a, g, "tn")
    if dims == "nt":
        return _dg(g, b, "nn"), _dg(g, a, "tn")
    return _dg(b, g, "nt"), _dg(a, g, "nn")


bdot.defvjp(_bdot_fwd, _bdot_bwd)


def _fdot(a, b, dims):
    return lax.dot_general(a, b, _DIMS[dims], precision=lax.Precision.HIGH, preferred_element_type=f32)


@jax.custom_vjp
def unit_lower_inv_all(mats):
    shape = mats[0].shape
    row = lax.broadcasted_iota(jnp.int32, shape, 0)
    col = lax.broadcasted_iota(jnp.int32, shape, 1)
    eye = jnp.where(row == col, 1.0, 0.0).astype(f32)
    n = [-a for a in mats]
    p = [eye + x for x in n]
    span = 2
    while span < shape[0]:
        n = [_fdot(x, x, "nn") for x in n]
        p = [y + _fdot(y, x, "nn") for y, x in zip(p, n)]
        span *= 2
    return p


def _uli_fwd(mats):
    x = unit_lower_inv_all(mats)
    return x, x


def _uli_bwd(xs, gs):
    t = [_fdot(x, g, "tn") for x, g in zip(xs, gs)]
    return ([-_fdot(y, x, "nt") for y, x in zip(t, xs)],)


unit_lower_inv_all.defvjp(_uli_fwd, _uli_bwd)


@jax.custom_vjp
def known_inverse(invs, mats):
    return invs


def _known_fwd(invs, mats):
    return invs, invs


def _known_bwd(xs, gs):
    return [jnp.zeros_like(x) for x in xs], _uli_bwd(xs, gs)[0]


known_inverse.defvjp(_known_fwd, _known_bwd)


def cumsum_rows(x):
    rows = x.shape[0]
    row = lax.broadcasted_iota(jnp.int32, x.shape, 0)
    s = 1
    while s < rows:
        x = x + jnp.where(row >= s, sroll(x, s, 0), 0.0)
        s *= 2
    return x


def _expm1(x):
    return jnp.tanh(0.5 * x) * (jnp.exp(x) + 1.0)


def _lane_pick(x, lane):
    idx = lax.broadcasted_iota(jnp.int32, x.shape, 1)
    return jnp.sum(jnp.where(idx == lane, x, 0.0), axis=1, keepdims=True)


def _row_pick(x, r):
    idx = lax.broadcasted_iota(jnp.int32, x.shape, 0)
    return jnp.sum(jnp.where(idx == r, x, 0.0), axis=0, keepdims=True)


def _causal_conv(x, halo, w, width):
    if halo is None:
        row = lax.broadcasted_iota(jnp.int32, x.shape, 0)
        acc = x * w[width - 1:width]
        for k in range(width - 1):
            shift = width - 1 - k
            acc = acc + jnp.where(row >= shift, sroll(x, shift, 0), 0.0) * w[k:k + 1]
        return acc
    xe = jnp.concatenate([halo, x], axis=0)
    acc = xe * w[width - 1:width]
    for k in range(width - 1):
        acc = acc + sroll(xe, width - 1 - k, 0) * w[k:k + 1]
    return acc[8:]


def f_norm(ins, ps):
    (x,), (g,) = ins, ps
    return [x * lax.rsqrt(jnp.mean(x * x, axis=-1, keepdims=True) + EPS) * g]


def f_dn_pre(kind, mains, halos, ps):
    y = _causal_conv(mains[0], halos[0], ps[0], 4)
    y = y * jax.nn.sigmoid(y)
    if kind < 2:
        y = y * lax.rsqrt(jnp.sum(y * y, axis=-1, keepdims=True) + EPS)
    if kind == 0:
        y = y * (DN_DK ** -0.5)
    return [y]


def f_dn_gates(ins, ps):
    u, p = ins[0][:, :LANE], ps[0]
    lane = lax.broadcasted_iota(jnp.int32, u.shape, 1)
    g = -jnp.exp(p[0:1]) * jax.nn.softplus(u + p[1:2])
    beta = jax.nn.sigmoid(u)
    return [jnp.where(lane < 4, g, jnp.where(lane < 8, beta, 0.0))]


def per_head(fn):
    def tile_fn(vals, ps):
        heads = [fn([v[:, h * LANE:(h + 1) * LANE] for v in vals], ps) for h in range(vals[0].shape[1] // LANE)]
        return [jnp.concatenate([o[i] for o in heads], axis=1) for i in range(len(heads[0]))]
    return tile_fn


def f_dn_post(ins, ps):
    (o, z), (nw,) = ins, ps
    y = o * lax.rsqrt(jnp.mean(o * o, axis=-1, keepdims=True) + EPS) * nw
    return [y * (z * jax.nn.sigmoid(z))]


def _rot_half(t):
    lane = lax.broadcasted_iota(jnp.int32, t.shape, 1)
    width = t.shape[1]
    first = (lane % RET_DK) < (RET_DK // 2)
    return jnp.where(first, -sroll(t, width - RET_DK // 2, 1), sroll(t, RET_DK // 2, 1))


def f_ret_pre(ins, ps):
    q, k, cos, sin = ins
    qr = q * cos + _rot_half(q) * sin
    kr = (k * cos + _rot_half(k) * sin) * (RET_DK ** -0.5)
    return [qr, kr]


def f_ret_post(ins, ps):
    o, g = ins
    mu = jnp.mean(o, axis=-1, keepdims=True)
    var = jnp.mean(jnp.square(o - mu), axis=-1, keepdims=True)
    return [(o - mu) * lax.rsqrt(var + EPS) * (g * jax.nn.sigmoid(g))]


def f_lru_pre(mains, halos, ps):
    cw, cb, wa, ba, wx, bx, lam = ps
    xc = _causal_conv(mains[0], halos[0], cw, 4) + cb
    r = jax.nn.sigmoid(bdot(xc, wa, "nn") + ba)
    i = jax.nn.sigmoid(bdot(xc, wx, "nn") + bx)
    log_a = -LRU_C * r * jax.nn.softplus(-lam)
    a = jnp.exp(log_a)
    b = jnp.sqrt(-_expm1(2.0 * log_a)) * (i * xc)
    return [a, b]


def f_lru_post(ins, ps):
    h, g = ins
    return [h * jax.nn.gelu(g)]


def f_ffn_mid(mains, halos, ps):
    cwg, cwv, cbg, cbv = ps
    gate = _causal_conv(mains[0], halos[0], cwg, 3) + cbg
    val = _causal_conv(mains[1], halos[1], cwv, 3) + cbv
    return [gate * jax.nn.sigmoid(gate) * val]


def mm(a, b, dims, name, add=None, dep=None, b_koff=0, tm=1536, tn=1536, tk=2816):
    if dims == "tn":
        kdim, m = a.shape
        n = b.shape[1]
    else:
        m, kdim = a.shape
        n = b.shape[0] if dims == "nt" else b.shape[1]
    tm, tn, tk = _pick(m, tm), _pick(n, tn), _pick(kdim, tk)
    nk = kdim // tk
    a_spec = pl.BlockSpec((tk, tm), lambda i, j, k: (k, i)) if dims == "tn" else pl.BlockSpec((tm, tk), lambda i, j, k: (i, k))
    b_spec = (pl.BlockSpec((tn, tk), lambda i, j, k: (j, k + b_koff * nk)) if dims == "nt"
              else pl.BlockSpec((tk, tn), lambda i, j, k: (k, j)))
    o_spec = pl.BlockSpec((tm, tn), lambda i, j, k: (i, j))
    has_add, has_dep = add is not None, dep is not None

    def body(*refs):
        a_ref, b_ref = refs[:2]
        add_ref = refs[2] if has_add else None
        o_ref = refs[2 + has_add + has_dep]
        if nk == 1:
            prod = _dg(a_ref[...], b_ref[...], dims)
            o_ref[...] = prod + add_ref[...] if has_add else prod
            return
        acc_ref = refs[-1]
        k = pl.program_id(2)

        @pl.when(k == 0)
        def _():
            acc_ref[...] = jnp.zeros_like(acc_ref)

        acc_ref[...] += _dg(a_ref[...], b_ref[...], dims)

        @pl.when(k == nk - 1)
        def _():
            o_ref[...] = acc_ref[...] + add_ref[...] if has_add else acc_ref[...]

    args = [a, b] + ([add] if has_add else []) + ([dep] if has_dep else [])
    in_specs = [a_spec, b_spec] + ([o_spec] if has_add else [])
    in_specs += [pl.BlockSpec((8, LANE), lambda i, j, k: (0, 0))] if has_dep else []
    return pl.pallas_call(
        body, name=name, grid=(m // tm, n // tn, nk), in_specs=in_specs, out_specs=o_spec,
        out_shape=jax.ShapeDtypeStruct((m, n), f32), scratch_shapes=[pltpu.VMEM((tm, tn), f32)] if nk > 1 else [],
        compiler_params=_params())(*args)


def rowmap(fn, ins, params, outs, ncol, name, rows=512):
    n = ins[0][0].shape[0]
    r = min(rows, n)
    nin, npar = len(ins), len(params)

    def body(*refs):
        vals = [x[...] for x in refs[:nin]]
        pv = [p[...] for p in refs[nin:nin + npar]]
        for o_ref, o in zip(refs[nin + npar:], fn(vals, pv)):
            o_ref[...] = o.astype(o_ref.dtype)

    in_specs = [pl.BlockSpec((r, cb), functools.partial(lambda j, i, off: (i, off + j), off=off)) for _, cb, off in ins]
    in_specs += [pl.BlockSpec(bs, functools.partial(lambda j, i, f: f(j), f=f)) for _, bs, f in params]
    out_specs = [pl.BlockSpec((r, cb), lambda j, i: (i, j)) for cb, _ in outs]
    out_shape = [jax.ShapeDtypeStruct((n, cb * ncol), dt) for cb, dt in outs]
    res = pl.pallas_call(body, name=name, grid=(ncol, n // r), in_specs=in_specs, out_specs=out_specs,
                         out_shape=out_shape, compiler_params=_params())(*[a for a, _, _ in ins], *[a for a, _, _ in params])
    return res


def rowmap_bwd(fn, ins, params, douts, ncol, name, rows=512, add=None, din_dtypes=None, into=None, copy16=None):
    n = ins[0][0].shape[0]
    r = min(rows, n)
    nin, npar, nout = len(ins), len(params), len(douts)
    add = [None] * nin if add is None else list(add)
    add_idx = [i for i in range(nin) if add[i] is not None]
    din_dtypes = [f32] * nin if din_dtypes is None else list(din_dtypes)
    into_buf, into_off, into_idx = into if into is not None else (None, 0, [])
    has_into, has_copy = into is not None, copy16 is not None
    kept = [i for i in range(nin) if din_dtypes[i] is not None and i not in into_idx]

    def body(*refs):
        vals = [x[...] for x in refs[:nin]]
        pv = [p[...] for p in refs[nin:nin + npar]]
        dys = [d[...] for d in refs[nin + npar:nin + npar + nout]]
        k0 = nin + npar + nout
        add_refs = dict(zip(add_idx, refs[k0:k0 + len(add_idx)]))
        k0 += len(add_idx) + has_into
        din_refs = refs[k0:k0 + len(kept)]
        k0 += len(kept)
        copy_ref = refs[k0] if has_copy else None
        into_ref = refs[k0 + has_copy] if has_into else None
        dp_refs = refs[k0 + has_copy + has_into:]
        _, vjp = jax.vjp(fn, vals, pv)
        dvals, dpv = vjp(dys)
        cot = lambda idx: dvals[idx] + add_refs[idx][...] if idx in add_refs else dvals[idx]
        for d_ref, idx in zip(din_refs, kept):
            d_ref[...] = cot(idx).astype(d_ref.dtype)
        if has_copy:
            copy_ref[...] = cot(copy16).astype(copy_ref.dtype)
        if has_into:
            parts = [cot(idx) for idx in into_idx]
            into_ref[...] = (parts[0] if len(parts) == 1 else jnp.concatenate(parts, axis=1)).astype(into_ref.dtype)

        @pl.when(pl.program_id(1) == 0)
        def _():
            for d_ref in dp_refs:
                d_ref[...] = jnp.zeros_like(d_ref)

        for d_ref, d in zip(dp_refs, dpv):
            d_ref[...] += d

    in_specs = [pl.BlockSpec((r, cb), functools.partial(lambda j, i, off: (i, off + j), off=off)) for _, cb, off in ins]
    in_specs += [pl.BlockSpec(bs, functools.partial(lambda j, i, f: f(j), f=f)) for _, bs, f in params]
    in_specs += [pl.BlockSpec((r, d.shape[1] // ncol), lambda j, i: (i, j)) for d in douts]
    in_specs += [pl.BlockSpec((r, ins[i][1]), lambda j, i: (i, j)) for i in add_idx]
    out_specs = [pl.BlockSpec((r, ins[i][1]), lambda j, i: (i, j)) for i in kept]
    out_shape = [jax.ShapeDtypeStruct((n, ins[i][1] * ncol), din_dtypes[i]) for i in kept]
    args = [a for a, _, _ in ins] + [a for a, _, _ in params] + list(douts) + [add[i] for i in add_idx]
    aliases = {}
    if has_copy:
        out_specs += [pl.BlockSpec((r, ins[copy16][1]), lambda j, i: (i, j))]
        out_shape += [jax.ShapeDtypeStruct((n, ins[copy16][1] * ncol), bf16)]
    if has_into:
        assert ncol == 1
        in_specs += [pl.BlockSpec(memory_space=pl.ANY)]
        aliases[len(args)] = len(out_shape)
        args += [into_buf]
        out_specs += [pl.BlockSpec((r, sum(ins[i][1] for i in into_idx)), lambda j, i: (i, into_off))]
        out_shape += [jax.ShapeDtypeStruct(into_buf.shape, into_buf.dtype)]
    pshapes = [tuple(d for d in bs if d is not None) for _, bs, _ in params]
    out_specs += [pl.BlockSpec((None,) + ps, functools.partial(lambda j, i, nd: (j,) + (0,) * nd, nd=len(ps))) for ps in pshapes]
    out_shape += [jax.ShapeDtypeStruct((ncol,) + ps, f32) for ps in pshapes]
    res = pl.pallas_call(body, name=name, grid=(ncol, n // r), in_specs=in_specs, out_specs=out_specs, out_shape=out_shape,
                         input_output_aliases=aliases, compiler_params=_params())(*args)
    dins = [None] * nin
    for pos, i in enumerate(kept):
        dins[i] = res[pos]
    pos = len(kept)
    extras = {}
    if has_copy:
        extras["copy16"] = res[pos]
        pos += 1
    if has_into:
        extras["into"] = res[pos]
        pos += 1
    return dins, res[pos:], extras


SEQ_ROWS = 2048


def seqmap(fn, ins, params, nouts, ncol, name, out_dtype=f32):
    bsz, seq, _ = ins[0][0].shape
    r = min(SEQ_ROWS, seq)
    nin, npar = len(ins), len(params)

    def body(*refs):
        in_refs = refs[:nin]
        pv = [p[...] for p in refs[nin:nin + npar]]
        out_refs = refs[nin + npar:]

        def step(i, carry):
            r0 = pl.multiple_of(i * r, r)
            h0 = pl.multiple_of(jnp.maximum(r0 - 8, 0), 8)
            mains = [x[pl.ds(r0, r), :] for x in in_refs]
            halos = [jnp.where(i == 0, 0.0, x[pl.ds(h0, 8), :]) for x in in_refs]
            for o_ref, o in zip(out_refs, fn(mains, halos, pv)):
                o_ref[pl.ds(r0, r), :] = o.astype(o_ref.dtype)
            return carry

        if r == seq:
            for o_ref, o in zip(out_refs, fn([x[...] for x in in_refs], [None] * nin, pv)):
                o_ref[...] = o.astype(o_ref.dtype)
        else:
            lax.fori_loop(0, seq // r, step, 0)

    in_specs = [pl.BlockSpec((None, seq, LANE), functools.partial(lambda j, b, off: (b, 0, off + j), off=off)) for _, off in ins]
    in_specs += [pl.BlockSpec(bs, functools.partial(lambda j, b, f: f(j), f=f)) for _, bs, f in params]
    out_specs = [pl.BlockSpec((None, seq, LANE), lambda j, b: (b, 0, j)) for _ in range(nouts)]
    out_shape = [jax.ShapeDtypeStruct((bsz, seq, LANE * ncol), out_dtype) for _ in range(nouts)]
    return pl.pallas_call(body, name=name, grid=(ncol, bsz), in_specs=in_specs, out_specs=out_specs,
                          out_shape=out_shape, compiler_params=_params())(*[a for a, _ in ins], *[a for a, _, _ in params])


def seqmap_bwd(fn, ins, params, douts, ncol, name, din_dtype=f32, into=None):
    bsz, seq, _ = ins[0][0].shape
    r = min(SEQ_ROWS, seq)
    nin, npar, nout = len(ins), len(params), len(douts)
    narrow = din_dtype != f32

    def body(*refs):
        in_refs = refs[:nin]
        pv = [p[...] for p in refs[nin:nin + npar]]
        dy_refs = refs[nin + npar:nin + npar + nout]
        k0 = nin + npar + nout + (into is not None)
        dout_refs = refs[k0:k0 + nin]
        dp_refs = refs[k0 + nin:k0 + nin + npar]
        din_refs = refs[k0 + nin + npar:] if narrow else dout_refs

        def step(i, dp_acc):
            r0 = pl.multiple_of(i * r, r)
            h0 = pl.multiple_of(jnp.maximum(r0 - 8, 0), 8)
            mains = [x[pl.ds(r0, r), :] for x in in_refs]
            halos_raw = [x[pl.ds(h0, 8), :] for x in in_refs]

            def tile(mains, halos_raw, pv):
                return fn(mains, [jnp.where(i == 0, 0.0, h) for h in halos_raw], pv)

            _, vjp = jax.vjp(tile, mains, halos_raw, pv)
            dm, dh, dp = vjp([d[pl.ds(r0, r), :] for d in dy_refs])
            for d_ref, m, h in zip(din_refs, dm, dh):
                d_ref[pl.ds(r0, r), :] = m
                d_ref[pl.ds(h0, 8), :] += h
            return [acc + d for acc, d in zip(dp_acc, dp)]

        if r == seq:
            _, vjp = jax.vjp(lambda mains, pv: fn(mains, [None] * nin, pv), [x[...] for x in in_refs], pv)
            dm, dp = vjp([d[...] for d in dy_refs])
            for o_ref, m in zip(dout_refs, dm):
                o_ref[...] = m.astype(o_ref.dtype)
        else:
            dp = lax.fori_loop(0, seq // r, step, [jnp.zeros(p.shape, f32) for p in pv])
            if narrow:
                for o_ref, d_ref in zip(dout_refs, din_refs):
                    o_ref[...] = d_ref[...].astype(o_ref.dtype)

        @pl.when(pl.program_id(1) == 0)
        def _():
            for d_ref in dp_refs:
                d_ref[...] = jnp.zeros_like(d_ref)

        for d_ref, d in zip(dp_refs, dp):
            d_ref[...] += d

    in_specs = [pl.BlockSpec((None, seq, LANE), functools.partial(lambda j, b, off: (b, 0, off + j), off=off)) for _, off in ins]
    in_specs += [pl.BlockSpec(bs, functools.partial(lambda j, b, f: f(j), f=f)) for _, bs, f in params]
    in_specs += [pl.BlockSpec((None, seq, LANE), lambda j, b: (b, 0, j)) for _ in range(nout)]
    out_specs = [pl.BlockSpec((None, seq, LANE), lambda j, b: (b, 0, j)) for _ in range(nin)]
    pshapes = [tuple(d for d in bs if d is not None) for _, bs, _ in params]
    out_specs += [pl.BlockSpec((None,) + ps, functools.partial(lambda j, b, nd: (j,) + (0,) * nd, nd=len(ps))) for ps in pshapes]
    out_shape = [jax.ShapeDtypeStruct((bsz, seq, LANE * ncol), din_dtype) for _ in range(nin)]
    out_shape += [jax.ShapeDtypeStruct((ncol,) + ps, f32) for ps in pshapes]
    args = [a for a, _ in ins] + [a for a, _, _ in params] + list(douts)
    aliases = {}
    if into is not None:
        assert nin == 1 and into[0].dtype == din_dtype
        in_specs += [pl.BlockSpec(memory_space=pl.ANY)]
        aliases[len(args)] = 0
        args += [into[0]]
        out_specs[0] = pl.BlockSpec((None, seq, LANE), lambda j, b: (b, 0, into[1] + j))
        out_shape[0] = jax.ShapeDtypeStruct(into[0].shape, din_dtype)
    res = pl.pallas_call(body, name=name, grid=(ncol, bsz), in_specs=in_specs, out_specs=out_specs, out_shape=out_shape,
                         scratch_shapes=[pltpu.VMEM((seq, LANE), f32) for _ in range(nin)] if narrow and r != seq else [],
                         input_output_aliases=aliases, compiler_params=_params())(*args)
    return res[:nin], res[nin:]


def _tri_masks():
    row = lax.broadcasted_iota(jnp.int32, (CHUNK, CHUNK), 0)
    col = lax.broadcasted_iota(jnp.int32, (CHUNK, CHUNK), 1)
    return row >= col, row > col


CHUNKS_PER_STEP = 4


def _by_rows(parts, per_row):
    rows = [jnp.concatenate(parts[i:i + per_row], axis=1) for i in range(0, len(parts), per_row)]
    return jnp.concatenate(rows, axis=0)


def dn_prep(vals, ps):
    q, k, v, gb = vals[:4]
    nchunk = q.shape[0] // CHUNK
    causal, strict = _tri_masks()
    gbs = [gb[c * CHUNK:(c + 1) * CHUNK] for c in range(nchunk)]
    gcs = [cumsum_rows(g) for g in gbs]
    gcts = [g.T for g in gcs]
    chains = [(c, h) for c in range(nchunk) for h in range(DN_HEADS)]
    part = lambda t, c, h: t[c * CHUNK:(c + 1) * CHUNK, h * DN_DK:(h + 1) * DN_DK]
    qh = [part(q, c, h) for c, h in chains]
    kh = [part(k, c, h) for c, h in chains]
    vh = [part(v, c, h) for c, h in chains]
    g_col = [_lane_pick(gcs[c], h) for c, h in chains]
    beta = [_lane_pick(gbs[c], DN_HEADS + h) for c, h in chains]
    g_row = [_row_pick(gcts[c], h)[:, :CHUNK] for c, h in chains]
    decay = [jnp.where(causal, jnp.exp(jnp.where(causal, gc - gr, 0.0)), 0.0) for gc, gr in zip(g_col, g_row)]
    k_beta = [a * b for a, b in zip(kh, beta)]
    eg = [jnp.exp(g) for g in g_col]
    kk = [bdot(a, b, "nt") for a, b in zip(k_beta, kh)]
    qk = [bdot(a, b, "nt") for a, b in zip(qh, kh)]
    lower = [jnp.where(strict, a * d, 0.0) for a, d in zip(kk, decay)]
    if len(vals) == 5:
        t_inv = known_inverse([part(vals[4], c, h)[:, :CHUNK] for c, h in chains], lower)
    else:
        t_inv = unit_lower_inv_all(lower)
    u = [bdot(t, a * b, "nn") for t, a, b in zip(t_inv, vh, beta)]
    w = [bdot(t, a * e, "nn") for t, a, e in zip(t_inv, k_beta, eg)]
    attn = [jnp.concatenate([a * d, jnp.zeros((CHUNK, DN_DK - CHUNK), f32)], axis=1) for a, d in zip(qk, decay)]
    qd = [a * e for a, e in zip(qh, eg)]
    kd = [a * jnp.exp(_row_pick(g, CHUNK - 1) - g) for a, g in zip(kh, g_col)]
    g_last = jnp.concatenate([jnp.broadcast_to(_row_pick(g, CHUNK - 1), g.shape) for g in gcs], axis=0)
    outs = [_by_rows(t, DN_HEADS) for t in (u, w, attn, qd, kd)] + [g_last]
    if len(vals) == 4:
        wide = [jnp.concatenate([t, jnp.zeros((CHUNK, DN_DK - CHUNK), f32)], axis=1) for t in t_inv]
        outs.append(_by_rows(wide, DN_HEADS))
    return outs


def dn_step(state, u, w, attn, qd, kd, g_last):
    bsz = u.shape[0]
    chains = [(b, h) for b in range(bsz) for h in range(DN_HEADS)]
    part = lambda t, b, h: t[b, :, h * DN_DK:(h + 1) * DN_DK]
    ws = [bdot(part(w, b, h), s, "nn") for (b, h), s in zip(chains, state)]
    qs = [bdot(part(qd, b, h), s, "nn") for (b, h), s in zip(chains, state)]
    v_new = [part(u, b, h) - x for (b, h), x in zip(chains, ws)]
    av = [bdot(attn[b, :, h * DN_DK:h * DN_DK + CHUNK], x, "nn") for (b, h), x in zip(chains, v_new)]
    kv = [bdot(part(kd, b, h), x, "tn") for (b, h), x in zip(chains, v_new)]
    ge = [jnp.exp(_row_pick(_lane_pick(g_last[b], h), 0)) for b, h in chains]
    new_state = [s * g + x for s, g, x in zip(state, ge, kv)]
    outs = [a + b for a, b in zip(qs, av)]
    return new_state, jnp.concatenate([jnp.concatenate(outs[b * DN_HEADS:(b + 1) * DN_HEADS], axis=1)[None]
                                       for b in range(bsz)], axis=0)


def _ret_log_gamma(h):
    return math.log(1.0 - 2.0 ** (-5.0 - h))


def ret_prep(vals, ps):
    q, k, v = vals
    nchunk = q.shape[0] // CHUNK
    causal, _ = _tri_masks()
    row = lax.broadcasted_iota(jnp.int32, (CHUNK, CHUNK), 0)
    col = lax.broadcasted_iota(jnp.int32, (CHUNK, CHUNK), 1)
    dist = (row - col).astype(f32)
    lane = lax.broadcasted_iota(jnp.int32, (CHUNK, q.shape[1]), 1)
    dmask = [jnp.where(causal, jnp.exp(jnp.where(causal, dist, 0.0) * _ret_log_gamma(h)), 0.0) for h in range(RET_HEADS)]
    chains = [(c, h) for c in range(nchunk) for h in range(RET_HEADS)]
    rows = lambda t, c: t[c * CHUNK:(c + 1) * CHUNK]
    scores = [bdot(jnp.where((lane // RET_DK) == h, rows(q, c), 0.0), rows(k, c), "nt") * dmask[h] for c, h in chains]
    inner = [bdot(s, rows(v, c)[:, h * RET_DV:(h + 1) * RET_DV], "nn") for s, (c, h) in zip(scores, chains)]
    return [_by_rows(inner, RET_HEADS)]


def ret_step(state, q, k, v, inner):
    bsz = q.shape[0]
    idx = lax.broadcasted_iota(jnp.int32, (CHUNK, 1), 0).astype(f32)
    lane = lax.broadcasted_iota(jnp.int32, (CHUNK, q.shape[2]), 1)
    chains = [(b, h) for b in range(bsz) for h in range(RET_HEADS)]
    part = lambda t, b, h: t[b, :, h * RET_DV:(h + 1) * RET_DV]
    cross = [bdot(q[b], s, "nn") for (b, h), s in zip(chains, state)]
    kz = [jnp.where((lane // RET_DK) == h, k[b], 0.0) * jnp.exp((CHUNK - 1.0 - idx) * _ret_log_gamma(h)) for b, h in chains]
    kv = [bdot(a, part(v, b, h), "tn") for a, (b, h) in zip(kz, chains)]
    outs = [x * jnp.exp((idx + 1.0) * _ret_log_gamma(h)) + part(inner, b, h) for x, (b, h) in zip(cross, chains)]
    new_state = [s * math.exp(CHUNK * _ret_log_gamma(h)) + x for s, x, (b, h) in zip(state, kv, chains)]
    return new_state, jnp.concatenate([jnp.concatenate(outs[b * RET_HEADS:(b + 1) * RET_HEADS], axis=1)[None]
                                       for b in range(bsz)], axis=0)


SCAN_CHUNKS = 4


def chunk_scan(step_fn, ins, state_shape, out_width, name):
    bsz, seq, _ = ins[0].shape
    nchunk = seq // CHUNK
    nin = len(ins)
    nh = state_shape[0]
    per = SCAN_CHUNKS if nchunk % SCAN_CHUNKS == 0 else 1

    def body(*refs):
        in_refs = refs[:nin]
        o_ref, ck_ref, s_ref = refs[nin:]

        @pl.when(pl.program_id(0) == 0)
        def _():
            s_ref[...] = jnp.zeros_like(s_ref)

        state = [s_ref[i] for i in range(bsz * nh)]
        for c in range(per):
            rows = slice(c * CHUNK, (c + 1) * CHUNK)
            for i in range(bsz * nh):
                ck_ref[i // nh, c, i % nh] = state[i]
            state, out = step_fn(state, *[x[:, rows, :].astype(f32) for x in in_refs])
            o_ref[:, rows, :] = out
        for i in range(bsz * nh):
            s_ref[i] = state[i]

    in_specs = [pl.BlockSpec((bsz, per * CHUNK, x.shape[2]), lambda n: (0, n, 0)) for x in ins]
    out_specs = [pl.BlockSpec((bsz, per * CHUNK, out_width), lambda n: (0, n, 0)),
                 pl.BlockSpec((bsz, per) + tuple(state_shape), lambda n: (0, n, 0, 0, 0))]
    out_shape = [jax.ShapeDtypeStruct((bsz, seq, out_width), f32),
                 jax.ShapeDtypeStruct((bsz, nchunk) + tuple(state_shape), f32)]
    return pl.pallas_call(body, name=name, grid=(nchunk // per,), in_specs=in_specs, out_specs=out_specs, out_shape=out_shape,
                          scratch_shapes=[pltpu.VMEM((bsz * nh,) + tuple(state_shape[1:]), f32)],
                          compiler_params=_params())(*ins)


def chunk_scan_bwd(step_fn, ins, ckpt, dout, name):
    bsz, seq, _ = ins[0].shape
    nchunk = seq // CHUNK
    nin = len(ins)
    state_shape = ckpt.shape[2:]
    nh = state_shape[0]
    per = SCAN_CHUNKS if nchunk % SCAN_CHUNKS == 0 else 1
    nstep = nchunk // per

    def body(*refs):
        in_refs = refs[:nin]
        ck_ref, do_ref = refs[nin:nin + 2]
        din_refs = refs[nin + 2:nin + 2 + nin]
        ds_ref = refs[-1]

        @pl.when(pl.program_id(0) == 0)
        def _():
            ds_ref[...] = jnp.zeros_like(ds_ref)

        dstate = [ds_ref[i] for i in range(bsz * nh)]
        for c in reversed(range(per)):
            rows = slice(c * CHUNK, (c + 1) * CHUNK)
            state = [ck_ref[i // nh, c, i % nh] for i in range(bsz * nh)]
            _, vjp = jax.vjp(step_fn, state, *[x[:, rows, :].astype(f32) for x in in_refs])
            grads = vjp((dstate, do_ref[:, rows, :]))
            dstate = grads[0]
            for d_ref, d in zip(din_refs, grads[1:]):
                d_ref[:, rows, :] = d
        for i in range(bsz * nh):
            ds_ref[i] = dstate[i]

    rev = lambda n: (0, nstep - 1 - n, 0)
    in_specs = [pl.BlockSpec((bsz, per * CHUNK, x.shape[2]), rev) for x in ins]
    in_specs += [pl.BlockSpec((bsz, per) + tuple(state_shape), lambda n: (0, nstep - 1 - n, 0, 0, 0)),
                 pl.BlockSpec((bsz, per * CHUNK, dout.shape[2]), rev)]
    out_specs = [pl.BlockSpec((bsz, per * CHUNK, x.shape[2]), rev) for x in ins]
    out_shape = [jax.ShapeDtypeStruct(x.shape, f32) for x in ins]
    return pl.pallas_call(body, name=name, grid=(nstep,), in_specs=in_specs, out_specs=out_specs, out_shape=out_shape,
                          scratch_shapes=[pltpu.VMEM((bsz * nh,) + tuple(state_shape[1:]), f32)],
                          compiler_params=_params())(*ins, ckpt, dout)


LRU_ROWS = 512


def lru_scan(a, b):
    bsz, seq, width = a.shape
    rb = min(LRU_ROWS, seq)
    seqs = range(bsz)

    def body(a_ref, b_ref, h_ref, hp_ref, carry_ref):
        @pl.when(pl.program_id(0) == 0)
        def _():
            carry_ref[...] = jnp.zeros_like(carry_ref)

        row = lax.broadcasted_iota(jnp.int32, (8, width), 0)

        def tile(t, hprev):
            r0 = pl.multiple_of(t * 8, 8)
            ca = [a_ref[i, pl.ds(r0, 8), :] for i in seqs]
            cb = [b_ref[i, pl.ds(r0, 8), :] for i in seqs]
            for s in (1, 2, 4):
                m = row >= s
                cb = [jnp.where(m, x * pltpu.roll(y, s, 0) + y, y) for x, y in zip(ca, cb)]
                ca = [jnp.where(m, x * pltpu.roll(x, s, 0), x) for x in ca]
            h = [y + x * p for x, y, p in zip(ca, cb, hprev)]
            for i in seqs:
                h_ref[i, pl.ds(r0, 8), :] = h[i]
                hp_ref[i, pl.ds(r0, 8), :] = jnp.where(row == 0, hprev[i], pltpu.roll(h[i], 1, 0))
            return tuple(_row_pick(x, 7) for x in h)

        last = lax.fori_loop(0, rb // 8, tile, tuple(carry_ref[i:i + 1, :] for i in seqs))
        for i in seqs:
            carry_ref[i:i + 1, :] = last[i]

    spec = pl.BlockSpec((bsz, rb, width), lambda i: (0, i, 0))
    return pl.pallas_call(body, name="lru_scan", grid=(seq // rb,), in_specs=[spec, spec], out_specs=[spec, spec],
                          out_shape=[jax.ShapeDtypeStruct(a.shape, f32)] * 2,
                          scratch_shapes=[pltpu.VMEM((max(8, bsz), width), f32)], compiler_params=_params())(a, b)


def lru_scan_bwd(a, hp, dh):
    bsz, seq, width = a.shape
    rb = min(LRU_ROWS, seq)
    nblk = seq // rb
    seqs = range(bsz)

    def body(a_ref, hp_ref, dh_ref, da_ref, db_ref, carry_ref):
        @pl.when(pl.program_id(0) == 0)
        def _():
            carry_ref[...] = jnp.zeros_like(carry_ref)

        row = lax.broadcasted_iota(jnp.int32, (8, width), 0)
        ntile = rb // 8

        def tile(t, mu_next):
            r0 = pl.multiple_of((ntile - 1 - t) * 8, 8)
            ca = [a_ref[i, pl.ds(r0, 8), :] for i in seqs]
            dh_t = [dh_ref[i, pl.ds(r0, 8), :] for i in seqs]
            cb = [x * y for x, y in zip(ca, dh_t)]
            for s in (1, 2, 4):
                m = row < 8 - s
                cb = [jnp.where(m, x * pltpu.roll(y, 8 - s, 0) + y, y) for x, y in zip(ca, cb)]
                ca = [jnp.where(m, x * pltpu.roll(x, 8 - s, 0), x) for x in ca]
            mu = [y + x * p for x, y, p in zip(ca, cb, mu_next)]
            for i in seqs:
                lam = dh_t[i] + jnp.where(row == 7, mu_next[i], pltpu.roll(mu[i], 7, 0))
                db_ref[i, pl.ds(r0, 8), :] = lam
                da_ref[i, pl.ds(r0, 8), :] = lam * hp_ref[i, pl.ds(r0, 8), :]
            return tuple(_row_pick(x, 0) for x in mu)

        last = lax.fori_loop(0, ntile, tile, tuple(carry_ref[i:i + 1, :] for i in seqs))
        for i in seqs:
            carry_ref[i:i + 1, :] = last[i]

    spec = pl.BlockSpec((bsz, rb, width), lambda i: (0, nblk - 1 - i, 0))
    return pl.pallas_call(body, name="lru_scan_bwd", grid=(nblk,), in_specs=[spec] * 3, out_specs=[spec, spec],
                          out_shape=[jax.ShapeDtypeStruct(a.shape, f32)] * 2,
                          scratch_shapes=[pltpu.VMEM((max(8, bsz), width), f32)], compiler_params=_params())(a, hp, dh)


MERGE_ROWS = 512


def branch_merge(ys, w_branch, u):
    n = ys[0].shape[0]
    tm = min(MERGE_ROWS, n)

    def body(ya, yb, yc, w_ref, g0, g1, g2, o_ref):
        acc = None
        for i, (y_ref, g_ref) in enumerate(((ya, g0), (yb, g1), (yc, g2))):
            term = jax.nn.sigmoid(g_ref[...]) * _dg(y_ref[...], w_ref[i], "nn")
            acc = term if acc is None else acc + term
        o_ref[...] = acc.astype(o_ref.dtype)

    y_spec = pl.BlockSpec((tm, ys[0].shape[1]), lambda i: (i, 0))
    g_specs = [pl.BlockSpec((tm, D_MODEL), functools.partial(lambda i, k: (i, k), k=k)) for k in range(3)]
    return pl.pallas_call(
        body, name="branch_merge", grid=(n // tm,),
        in_specs=[y_spec] * 3 + [pl.BlockSpec(w_branch.shape, lambda i: (0, 0, 0))] + g_specs,
        out_specs=pl.BlockSpec((tm, D_MODEL), lambda i: (i, 0)), out_shape=jax.ShapeDtypeStruct((n, D_MODEL), bf16),
        compiler_params=_params())(*ys, w_branch, u, u, u)


def branch_merge_bwd(ys, w_branch, u, d_merged, du):
    n = ys[0].shape[0]
    tm = min(MERGE_ROWS, n)

    def body(ya, yb, yc, w_ref, g0, g1, g2, dm_ref, du_in, db0, db1, db2, du_ref):
        dm = dm_ref[...]
        d_gates = []
        for i, (y_ref, g_ref, db_ref) in enumerate(((ya, g0, db0), (yb, g1, db1), (yc, g2, db2))):
            s = jax.nn.sigmoid(g_ref[...])
            db_ref[...] = (dm * s).astype(db_ref.dtype)
            d_gates.append(dm * _dg(y_ref[...], w_ref[i], "nn") * (s * (1.0 - s)))
        du_ref[...] = jnp.concatenate(d_gates, axis=1).astype(du_ref.dtype)

    y_spec = pl.BlockSpec((tm, ys[0].shape[1]), lambda i: (i, 0))
    row = pl.BlockSpec((tm, D_MODEL), lambda i: (i, 0))
    g_specs = [pl.BlockSpec((tm, D_MODEL), functools.partial(lambda i, k: (i, k), k=k)) for k in range(3)]
    res = pl.pallas_call(
        body, name="branch_merge_bwd", grid=(n // tm,),
        in_specs=[y_spec] * 3 + [pl.BlockSpec(w_branch.shape, lambda i: (0, 0, 0))] + g_specs + [row, pl.BlockSpec(memory_space=pl.ANY)],
        out_specs=[row] * 3 + [pl.BlockSpec((tm, 3 * D_MODEL), lambda i: (i, 0))],
        out_shape=[jax.ShapeDtypeStruct((n, D_MODEL), bf16)] * 3 + [jax.ShapeDtypeStruct(du.shape, du.dtype)],
        input_output_aliases={8: 3}, compiler_params=_params())(*ys, w_branch, u, u, u, d_merged, du)
    return list(res[:3]), res[3]


def final_loss(x, g, target):
    n, d = x.shape
    r = min(256, n)

    def body(x_ref, g_ref, t_ref, loss_ref, dx_ref, dg_ref, dx16_ref):
        @pl.when(pl.program_id(0) == 0)
        def _():
            loss_ref[...] = jnp.zeros_like(loss_ref)
            dg_ref[...] = jnp.zeros_like(dg_ref)

        tgt = t_ref[...]

        def loss_fn(xv, gv):
            y = f_norm([xv], [gv])[0]
            return 0.5 * jnp.sum(jnp.mean(jnp.square(y - tgt), axis=-1, keepdims=True), axis=0, keepdims=True)

        val, vjp = jax.vjp(loss_fn, x_ref[...], g_ref[...])
        dx, dg = vjp(jnp.ones_like(val))
        loss_ref[...] += val
        dx_ref[...] = dx
        dx16_ref[...] = dx.astype(dx16_ref.dtype)
        dg_ref[...] += dg

    row = pl.BlockSpec((r, d), lambda i: (i, 0))
    return pl.pallas_call(
        body, name="final_loss", grid=(n // r,), in_specs=[row, pl.BlockSpec((1, d), lambda i: (0, 0)), row],
        out_specs=[pl.BlockSpec((8, LANE), lambda i: (0, 0)), row, pl.BlockSpec((1, d), lambda i: (0, 0)), row],
        out_shape=[jax.ShapeDtypeStruct((8, LANE), f32), jax.ShapeDtypeStruct((n, d), f32), jax.ShapeDtypeStruct((1, d), f32),
                   jax.ShapeDtypeStruct((n, d), bf16)],
        compiler_params=_params())(x, g, target)


_HBM = pl.BlockSpec(memory_space=pltpu.HBM)
_SEM = pl.BlockSpec(memory_space=pltpu.SEMAPHORE)
_EFFECT = pltpu.SideEffectType.DATAFLOW_SIDE_EFFECTING


def _peer(k):
    mx, my, mc = lax.axis_index("x"), lax.axis_index("y"), lax.axis_index("c")
    px, py, pc = (mx + (k >> 2)) % 2, (my + ((k >> 1) & 1)) % 2, (mc + (k & 1)) % 2
    return (px, py, pc), 4 * px + 2 * py + pc


def _peer_copy(k, i, x_ref, land_ref, send_sems, recv_sems, scatter):
    me = 4 * lax.axis_index("x") + 2 * lax.axis_index("y") + lax.axis_index("c")
    dev, slot = _peer(k)
    sem = i * (N_DEV - 1) + k - 1
    return pltpu.make_async_remote_copy(
        src_ref=x_ref.at[slot] if scatter else x_ref, dst_ref=land_ref.at[me], send_sem=send_sems.at[sem],
        recv_sem=recv_sems.at[sem], device_id=dev, device_id_type=pl.DeviceIdType.MESH)


def _own_copy(i, x_ref, land_ref, own_sems, scatter):
    me = 4 * lax.axis_index("x") + 2 * lax.axis_index("y") + lax.axis_index("c")
    return pltpu.make_async_copy(x_ref.at[me] if scatter else x_ref, land_ref.at[me], own_sems.at[i])


def exchange_start(xs, scatters, name):
    nx = len(xs)
    lands = [lax.empty((N_DEV,) + tuple(x.shape[1:] if sc else x.shape), x.dtype) for x, sc in zip(xs, scatters)]
    nsem = nx * (N_DEV - 1)

    def body(*refs):
        x_refs, land_refs = refs[:nx], refs[nx:2 * nx]
        send_sems, recv_sems, own_sems = refs[2 * nx:2 * nx + 3]
        token = refs[-1]
        for i in range(nx):
            for k in range(1, N_DEV):
                _peer_copy(k, i, x_refs[i], land_refs[i], send_sems, recv_sems, scatters[i]).start()
            _own_copy(i, x_refs[i], land_refs[i], own_sems, scatters[i]).start()
        token[...] = jnp.zeros_like(token)

    hbm = lambda a: pltpu.HBM(a.shape, a.dtype)
    res = pl.pallas_call(
        body, name=name, in_specs=(_HBM,) * (2 * nx),
        out_specs=(_SEM, _SEM, _SEM) + (_HBM,) * (2 * nx) + (pl.BlockSpec(memory_space=pltpu.VMEM),),
        input_output_aliases={i: 3 + i for i in range(2 * nx)},
        out_shape=(pltpu.SemaphoreType.DMA((nsem,)), pltpu.SemaphoreType.DMA((nsem,)), pltpu.SemaphoreType.DMA((nx,)),
                   *[hbm(a) for a in xs], *[hbm(a) for a in lands], jax.ShapeDtypeStruct((8, LANE), f32)),
        compiler_params=pltpu.CompilerParams(has_side_effects=_EFFECT),
    )(*[pltpu.with_memory_space_constraint(a, pltpu.HBM) for a in list(xs) + lands])
    return (res[0], res[1], res[2], list(res[3:3 + nx]), list(res[3 + nx:3 + 2 * nx]), tuple(scatters)), res[-1]


def exchange_wait(started, after, name):
    send_sems, recv_sems, own_sems, x_thrus, land_thrus, scatters = started
    nx = len(x_thrus)

    def body(*refs):
        x_refs, land_refs = refs[:nx], refs[nx:2 * nx]
        send_sems, recv_sems, own_sems = refs[2 * nx:2 * nx + 3]
        for i in range(nx):
            for k in range(1, N_DEV):
                cp = _peer_copy(k, i, x_refs[i], land_refs[i], send_sems, recv_sems, scatters[i])
                cp.wait_send()
                cp.wait_recv()
            _own_copy(i, x_refs[i], land_refs[i], own_sems, scatters[i]).wait()

    hbm = lambda a: pltpu.HBM(a.shape, a.dtype)
    res = pl.pallas_call(
        body, name=name, in_specs=(_HBM,) * (2 * nx) + (_SEM, _SEM, _SEM, pl.BlockSpec(memory_space=pl.ANY)),
        out_specs=(_HBM,) * (2 * nx), input_output_aliases={i: i for i in range(2 * nx)},
        out_shape=tuple(hbm(a) for a in list(x_thrus) + list(land_thrus)),
        compiler_params=pltpu.CompilerParams(has_side_effects=_EFFECT),
    )(*x_thrus, *land_thrus, send_sems, recv_sems, own_sems, after)
    return list(res[nx:])


def sum_slots(x, name):
    _, rows_total, cols = x.shape
    row_bytes = N_DEV * ((cols + LANE - 1) // LANE) * LANE * x.dtype.itemsize
    r = _pick_rows(rows_total, max(16, (4 * 1024 * 1024) // row_bytes // 16 * 16))

    def body(x_ref, o_ref):
        acc = x_ref[0].astype(f32)
        for s in range(1, N_DEV):
            acc = acc + x_ref[s].astype(f32)
        o_ref[...] = acc

    return pl.pallas_call(body, name=name, grid=(rows_total // r,),
                          in_specs=[pl.BlockSpec((N_DEV, r, cols), lambda i: (0, i, 0))],
                          out_specs=pl.BlockSpec((r, cols), lambda i: (i, 0)),
                          out_shape=jax.ShapeDtypeStruct((rows_total, cols), f32), compiler_params=_params())(x)


def _pick_rows(total, pref):
    best = None
    for d in range(16, min(total, pref) + 1, 16):
        if total % d == 0:
            best = d
    return best if best is not None else total


def adamw(w, g, m, v, name):
    shape = w.shape
    if w.ndim == 1:
        w2, g2, m2, v2 = (t.reshape(1, -1) for t in (w, g, m, v))
    else:
        w2, g2, m2, v2 = (t.reshape(-1, shape[-1]) for t in (w, g, m, v))
    rows_total, cols = w2.shape
    r = _pick_rows(rows_total, max(16, (512 * 1024) // max(cols, 1) // 16 * 16))
    c1, c2 = 1.0 / (1.0 - ADAM_B1 ** ADAM_STEP), 1.0 / (1.0 - ADAM_B2 ** ADAM_STEP)

    def body(w_ref, g_ref, m_ref, v_ref, d_ref, nm_ref, nv_ref):
        gv = g_ref[...]
        nm = ADAM_B1 * m_ref[...] + (1.0 - ADAM_B1) * gv
        nv = ADAM_B2 * v_ref[...] + (1.0 - ADAM_B2) * jnp.square(gv)
        d_ref[...] = -ADAM_LR * ((nm * c1) / (jnp.sqrt(nv * c2) + ADAM_EPS) + ADAM_WD * w_ref[...])
        nm_ref[...] = nm
        nv_ref[...] = nv

    spec = pl.BlockSpec((r, cols), lambda i: (i, 0))
    outs = pl.pallas_call(body, name=name, grid=(rows_total // r,), in_specs=[spec] * 4, out_specs=[spec] * 3,
                          out_shape=[jax.ShapeDtypeStruct((rows_total, cols), f32)] * 3, compiler_params=_params())(w2, g2, m2, v2)
    return tuple(o.reshape(shape) for o in outs)


def _const(j):
    return lambda _: j


def _layer_fwd(x, wl, fetch_rest, cos, sin, bsz, seq):
    n = x.shape[0]
    sv = {"x_in": x}
    row1 = lambda a: (a, (1, a.shape[1]), lambda j: (0, 0))
    h = rowmap(f_norm, [(x, D_MODEL, 0)], [row1(wl["attn_norm"])], [(D_MODEL, bf16)], 1, "norm_fwd")[0]
    u = mm(h, wl["w_in"], "nn", "mm_in", tn=2048)
    sv["h"], sv["u"] = h, u
    u3 = u.reshape(bsz, seq, U_PAD)
    wl = dict(wl)
    wl.update(fetch_rest(u))
    sv["wl"] = wl

    qkv = []
    for kind in range(3):
        cw = (wl["dn_conv_w"], (4, LANE), functools.partial(lambda j, kind: (0, 4 * kind + j), kind=kind))
        qkv.append(seqmap(functools.partial(f_dn_pre, kind), [(u3, U_QKV // LANE + 4 * kind)], [cw], 1, 4, "dn_pre%d" % kind)[0])
    gb = rowmap(f_dn_gates, [(u, 512, U_AB // 512)], [(wl["dn_gate_p"], (8, LANE), lambda j: (0, 0))], [(LANE, f32)], 1,
                "dn_gates", rows=512)[0]
    gb3 = gb.reshape(bsz, seq, LANE)
    crow = CHUNK * CHUNKS_PER_STEP
    dn_in = [(t.reshape(n, 512), 512, 0) for t in qkv] + [(gb, LANE, 0)]
    prep_a = rowmap(dn_prep, dn_in, [], [(512, f32)] + [(512, bf16)] * 4 + [(LANE, f32), (512, f32)], 1, "dn_prep", rows=crow)
    dn_in = dn_in + [(prep_a[6], 512, 0)]
    prep_a = [t.reshape(bsz, seq, t.shape[1]) for t in prep_a[:6]]
    o_a, ck_a = chunk_scan(dn_step, prep_a, (DN_HEADS, DN_DK, DN_DK), 512, "dn_scan")
    y_a = rowmap(per_head(f_dn_post), [(o_a.reshape(n, 512), 512, 0), (u, 512, U_Z // 512)],
                 [(wl["dn_norm_w"], (1, LANE), lambda j: (0, 0))], [(512, bf16)], 1, "dn_post")[0]
    sv.update(dn_in=dn_in, prep_a=prep_a, o_a=o_a, ck_a=ck_a, y_a=y_a)

    q_b, k_b = rowmap(f_ret_pre, [(u, 256, U_RQ // 256), (u, 256, U_RK // 256), (cos, 256, 0), (sin, 256, 0)], [],
                      [(256, f32), (256, f32)], 1, "ret_pre")
    q_b3, k_b3 = q_b.reshape(bsz, seq, 256), k_b.reshape(bsz, seq, 256)
    v_b3 = lax.slice_in_dim(u3, U_RV, U_RV + 512, axis=2)
    ret_in = [(q_b, 256, 0), (k_b, 256, 0), (u, 512, U_RV // 512)]
    inner = rowmap(ret_prep, ret_in, [], [(512, f32)], 1, "ret_prep", rows=crow)[0]
    ret_seq = [q_b3, k_b3, v_b3, inner.reshape(bsz, seq, 512)]
    o_b, ck_b = chunk_scan(ret_step, ret_seq, (RET_HEADS, 256, RET_DV), 512, "ret_scan")
    y_b = rowmap(per_head(f_ret_post), [(o_b.reshape(n, 512), 512, 0), (u, 512, U_RG // 512)], [], [(512, bf16)], 1,
                 "ret_post")[0]
    sv.update(ret_in=ret_in, ret_seq=ret_seq, o_b=o_b, ck_b=ck_b, y_b=y_b)

    lru_params = _lru_params(wl)
    a_c, b_c = seqmap(f_lru_pre, [(u3, U_CX // LANE)], lru_params, 2, 4, "lru_pre")
    h_c, hp_c = lru_scan(a_c, b_c)
    y_c = rowmap(f_lru_post, [(h_c.reshape(n, 512), 512, 0), (u, 512, U_CG // 512)], [], [(512, bf16)], 1, "lru_post")[0]
    sv.update(a_c=a_c, hp_c=hp_c, h_c=h_c, y_c=y_c)

    merged = branch_merge((y_a, y_b, y_c), wl["w_branch"], u)
    x_mid = mm(merged, wl["w_out"], "nn", "mm_out", add=x)
    sv.update(merged=merged, x_mid=x_mid)

    h2 = rowmap(f_norm, [(x_mid, D_MODEL, 0)], [row1(wl["ffn_norm"])], [(D_MODEL, bf16)], 1, "norm_fwd")[0]
    up = mm(h2, wl["w_up"], "nn", "mm_up")
    act = seqmap(f_ffn_mid, [(up.reshape(bsz, seq, 2 * D_FF), 0), (up.reshape(bsz, seq, 2 * D_FF), D_FF // LANE)],
                 _ffn_params(wl), 1, D_FF // LANE, "ffn_mid", out_dtype=bf16)[0]
    act = act.reshape(n, D_FF)
    x_out = mm(act, wl["w_down"], "nn", "mm_down", add=x_mid)
    sv.update(h2=h2, up=up, act=act)
    return x_out, sv


def _lru_params(wl):
    col = lambda a: (a, (a.shape[0], LANE), lambda j: (0, j))
    blk = lambda a: (a, (None, LANE, LANE), lambda j: (j, 0, 0))
    return [col(wl["lru_conv_w"]), col(wl["lru_conv_b"]), blk(wl["lru_wa"]), col(wl["lru_ba"]), blk(wl["lru_wx"]),
            col(wl["lru_bx"]), col(wl["lru_lambda"])]


def _ffn_params(wl):
    nb = D_FF // LANE
    return [(wl["ffn_conv_w"], (3, LANE), lambda j: (0, j)), (wl["ffn_conv_w"], (3, LANE), lambda j: (0, nb + j)),
            (wl["ffn_conv_b"], (1, LANE), lambda j: (0, j)), (wl["ffn_conv_b"], (1, LANE), lambda j: (0, nb + j))]


def _layer_bwd(dx, dx16, sv, cos, sin, bsz, seq, emit, dep):
    n = dx.shape[0]
    gr = {}
    wl = sv["wl"]
    u, x_in, x_mid = sv["u"], sv["x_in"], sv["x_mid"]
    u3 = u.reshape(bsz, seq, U_PAD)
    row1 = lambda a: (a, (1, a.shape[1]), lambda j: (0, 0))

    d_act = mm(dx16, wl["w_down"], "nt", "mm_down_dx", dep=dep)
    gr["w_down"] = mm(sv["act"], dx16, "tn", "mm_down_dw")
    up3 = sv["up"].reshape(bsz, seq, 2 * D_FF)
    (d_gate, d_val), dps = seqmap_bwd(f_ffn_mid, [(up3, 0), (up3, D_FF // LANE)], _ffn_params(wl),
                                      [d_act.reshape(bsz, seq, D_FF)], D_FF // LANE, "ffn_mid_bwd", din_dtype=bf16)
    gr["ffn_conv_w"] = jnp.concatenate([_cols(dps[0]), _cols(dps[1])], axis=1)
    gr["ffn_conv_b"] = jnp.concatenate([_cols(dps[2]), _cols(dps[3])], axis=1)[0]
    d_gate, d_val = d_gate.reshape(n, D_FF), d_val.reshape(n, D_FF)
    gr["w_up"] = (mm(sv["h2"], d_gate, "tn", "mm_up_dw"), mm(sv["h2"], d_val, "tn", "mm_up_dw"))
    token = emit("ffn", {k: gr[k] for k in ("w_up", "w_down")})
    d_h2 = mm(d_gate, wl["w_up"], "nt", "mm_up_dx", dep=token)
    d_h2 = mm(d_val, wl["w_up"], "nt", "mm_up_dx", add=d_h2, b_koff=1)
    (dx_mid,), (dg,), ex = rowmap_bwd(f_norm, [(x_mid, D_MODEL, 0)], [row1(wl["ffn_norm"])], [d_h2], 1, "norm_bwd", add=[dx],
                                      copy16=0)
    dx_mid16 = ex["copy16"]
    gr["ffn_norm"] = dg[0, 0]

    du = lax.empty((n, U_PAD), bf16)
    du3 = lambda: du.reshape(bsz, seq, U_PAD)

    d_merged = mm(dx_mid16, wl["w_out"], "nt", "mm_out_dx")
    gr["w_out"] = mm(sv["merged"], dx_mid16, "tn", "mm_out_dw")
    ys = (sv["y_a"], sv["y_b"], sv["y_c"])
    d_br, du = branch_merge_bwd(ys, wl["w_branch"], u, d_merged, du)
    d_ys = [mm(d_br[i], wl["w_branch"][i], "nt", "mm_branch_dx") for i in range(3)]
    gr["w_branch"] = jnp.stack([mm(ys[i], d_br[i], "tn", "mm_branch_dw") for i in range(3)])

    (d_hc, _), _, ex = rowmap_bwd(f_lru_post, [(sv["h_c"].reshape(n, 512), 512, 0), (u, 512, U_CG // 512)], [], [d_ys[2]], 1,
                                  "lru_post_bwd", into=(du, U_CG // 512, [1]))
    du = ex["into"]
    d_a, d_b = lru_scan_bwd(sv["a_c"], sv["hp_c"], d_hc.reshape(bsz, seq, 512))
    (du_new,), dps = seqmap_bwd(f_lru_pre, [(u3, U_CX // LANE)], _lru_params(wl), [d_a, d_b], 4, "lru_pre_bwd", din_dtype=bf16,
                                into=(du3(), U_CX // LANE))
    du = du_new.reshape(n, U_PAD)
    gr["lru_conv_w"], gr["lru_conv_b"] = _cols(dps[0]), _cols(dps[1])[0]
    gr["lru_wa"], gr["lru_ba"], gr["lru_wx"], gr["lru_bx"] = dps[2], dps[3][:, 0], dps[4], dps[5][:, 0]
    gr["lru_lambda"] = _cols(dps[6])[0]

    (d_ob, _), _, ex = rowmap_bwd(per_head(f_ret_post), [(sv["o_b"].reshape(n, 512), 512, 0), (u, 512, U_RG // 512)], [],
                                  [d_ys[1]], 1, "ret_post_bwd", into=(du, U_RG // 512, [1]))
    du = ex["into"]
    crow = CHUNK * CHUNKS_PER_STEP
    d_ret = chunk_scan_bwd(ret_step, sv["ret_seq"], sv["ck_b"], d_ob.reshape(bsz, seq, 512), "ret_scan_bwd")
    d_ret = [t.reshape(n, t.shape[2]) for t in d_ret]
    (d_qb, d_kb, _), _, ex = rowmap_bwd(ret_prep, sv["ret_in"], [], [d_ret[3]], 1, "ret_prep_bwd", rows=crow, add=d_ret[:3],
                                        into=(du, U_RV // 512, [2]))
    du = ex["into"]
    _, _, ex = rowmap_bwd(f_ret_pre, [(u, 256, U_RQ // 256), (u, 256, U_RK // 256), (cos, 256, 0), (sin, 256, 0)], [],
                          [d_qb, d_kb], 1, "ret_pre_bwd", din_dtypes=[f32, f32, None, None], into=(du, U_RQ // 512, [0, 1]))
    du = ex["into"]

    (d_oa, _), (dnw,), ex = rowmap_bwd(per_head(f_dn_post), [(sv["o_a"].reshape(n, 512), 512, 0), (u, 512, U_Z // 512)],
                                       [(wl["dn_norm_w"], (1, LANE), lambda j: (0, 0))], [d_ys[0]], 1, "dn_post_bwd",
                                       into=(du, U_Z // 512, [1]))
    du = ex["into"]
    gr["dn_norm_w"] = dnw[0, 0]
    d_prep = chunk_scan_bwd(dn_step, sv["prep_a"], sv["ck_a"], d_oa.reshape(bsz, seq, 512), "dn_scan_bwd")
    (d_q, d_k, d_v, d_gb, _), _, _ = rowmap_bwd(dn_prep, sv["dn_in"], [], [t.reshape(n, t.shape[2]) for t in d_prep], 1,
                                                "dn_prep_bwd", rows=crow, din_dtypes=[f32] * 4 + [None])
    d_q, d_k, d_v = (t.reshape(bsz, seq, 512) for t in (d_q, d_k, d_v))
    _, (dgp,), ex = rowmap_bwd(f_dn_gates, [(u, 512, U_AB // 512)], [(wl["dn_gate_p"], (8, LANE), lambda j: (0, 0))],
                               [d_gb], 1, "dn_gates_bwd", rows=512, into=(du, U_AB // 512, [0]))
    du = ex["into"]
    gr["dn_a_log"], gr["dn_dt_bias"] = dgp[0, 0, :DN_HEADS], dgp[0, 1, :DN_HEADS]
    d_cw = []
    for kind, d_t in enumerate((d_q, d_k, d_v)):
        cw = (wl["dn_conv_w"], (4, LANE), functools.partial(lambda j, kind: (0, 4 * kind + j), kind=kind))
        (du_new,), (dcw,) = seqmap_bwd(functools.partial(f_dn_pre, kind), [(u3, U_QKV // LANE + 4 * kind)], [cw], [d_t], 4,
                                       "dn_pre%d_bwd" % kind, din_dtype=bf16, into=(du3(), U_QKV // LANE + 4 * kind))
        du = du_new.reshape(n, U_PAD)
        d_cw.append(_cols(dcw))
    gr["dn_conv_w"] = jnp.concatenate(d_cw, axis=1)

    gr["w_in"] = _unpad_w_in(mm(sv["h"], du, "tn", "mm_in_dw"))
    token = emit("mix", {k: gr[k] for k in ("w_in", "w_branch", "w_out")})
    d_h = mm(du, wl["w_in"], "nt", "mm_in_dx", dep=token)
    (dx_in,), (dg,), ex = rowmap_bwd(f_norm, [(x_in, D_MODEL, 0)], [row1(wl["attn_norm"])], [d_h], 1, "norm_bwd", add=[dx_mid],
                                     copy16=0)
    gr["attn_norm"] = dg[0, 0]
    big = ("w_in", "w_branch", "w_out", "w_up", "w_down")
    return dx_in, ex["copy16"], emit("small", {k: g for k, g in gr.items() if k not in big})


def _cols(dp):
    ncol, p, _ = dp.shape
    return jnp.transpose(dp, (1, 0, 2)).reshape(p, ncol * LANE)


def _pad_w_in(w):
    segs = sorted(_IN_SEGS, key=lambda s: s[2])
    parts = [lax.slice_in_dim(w, src, src + width, axis=1) for src, width, _ in segs]
    end = segs[-1][2] + segs[-1][1]
    return jnp.concatenate(parts + [jnp.zeros((w.shape[0], U_PAD - end), w.dtype)], axis=1)


def _unpad_w_in(wp):
    return jnp.concatenate([lax.slice_in_dim(wp, dst, dst + width, axis=1) for _, width, dst in _IN_SEGS], axis=1)


def _rope_tables(positions):
    half = RET_DK // 2
    inv = ROPE_BASE ** (-jnp.arange(half, dtype=f32) / half)
    ang = positions.astype(f32).reshape(-1, 1) * inv
    cos, sin = jnp.cos(ang), jnp.sin(ang)
    return jnp.tile(cos, (1, 2 * RET_HEADS)), jnp.tile(sin, (1, 2 * RET_HEADS))


def _layer_weights(lw):
    wl = {}
    wl["w_in"] = _pad_w_in(lw["w_in"])
    for k in ("dn_conv_w", "lru_conv_w", "ffn_conv_w", "lru_wa", "lru_wx"):
        wl[k] = lw[k]
    for k in ("attn_norm", "ffn_norm", "dn_norm_w", "lru_conv_b", "lru_lambda", "ffn_conv_b", "lru_ba", "lru_bx"):
        wl[k] = lw[k].reshape(1, -1)
    gp = jnp.zeros((8, LANE), f32)
    wl["dn_gate_p"] = gp.at[0, :DN_HEADS].set(lw["dn_a_log"]).at[1, :DN_HEADS].set(lw["dn_dt_bias"])
    return wl


REST = ("w_branch", "w_out", "w_up", "w_down")


def forward_backward(x, positions, target, layer_weights, final_norm, on_head, on_grads):
    bsz, seq, d = x.shape
    n = bsz * seq
    cos, sin = _rope_tables(positions)
    xs = x.reshape(n, d)
    saved = []
    for layer in range(DEPTH):
        first, fetch_rest = layer_weights(layer, xs)
        xs, sv = _layer_fwd(xs, _layer_weights(first), fetch_rest, cos, sin, bsz, seq)
        saved.append(sv)
    loss, dx, d_final, dx16 = final_loss(xs, final_norm.reshape(1, d), target.reshape(n, d))
    on_head(loss[0, 0], d_final[0])
    token = None
    for layer in reversed(range(DEPTH)):
        dx, dx16, token = _layer_bwd(dx, dx16, saved[layer], cos, sin, bsz, seq, functools.partial(on_grads, layer), token)
    return dx.reshape(bsz, seq, d)


def local_step(x, positions, target, full):
    grads, head = {layer: {} for layer in range(DEPTH)}, {}

    def layer_weights(layer, _):
        return ({k: a[layer] for k, a in full.items() if k != "final_norm" and k not in REST},
                lambda after: {k: full[k][layer] for k in REST})

    gx = forward_backward(x, positions, target, layer_weights, full["final_norm"],
                          lambda loss, d_final: head.update(loss=loss, d_final=d_final),
                          lambda layer, group, gr: grads[layer].update(
                              {k: jnp.concatenate(g, axis=1) if isinstance(g, tuple) else g for k, g in gr.items()}))
    stacked = {k: jnp.stack([grads[layer][k] for layer in range(DEPTH)]) for k in grads[0]}
    stacked["final_norm"] = head["d_final"]
    return head["loss"], gx, stacked


BIG = (("w_in", 2), ("w_branch", 3), ("w_out", 1), ("w_up", 2), ("w_down", 1))
SMALL_SHARDED = (("dn_conv_w", 2), ("lru_conv_w", 2), ("ffn_conv_w", 2))
REPLICATED = ("attn_norm", "dn_a_log", "dn_dt_bias", "dn_norm_w", "lru_conv_b", "lru_wa", "lru_ba", "lru_wx", "lru_bx",
              "lru_lambda", "ffn_norm", "ffn_conv_b", "final_norm")
WEIGHTS = ("attn_norm", "w_in", "dn_conv_w", "dn_a_log", "dn_dt_bias", "dn_norm_w", "lru_conv_w", "lru_conv_b", "lru_wa",
           "lru_ba", "lru_wx", "lru_bx", "lru_lambda", "w_branch", "w_out", "ffn_norm", "w_up", "ffn_conv_w", "ffn_conv_b",
           "w_down", "final_norm")


def _pack(arrs, dtype, align=16 * LANE):
    flat = jnp.concatenate([a.reshape(-1).astype(dtype) for a in arrs])
    pad = (-flat.shape[0]) % align
    return jnp.pad(flat, (0, pad)).reshape(-1, LANE)


def _unpack(rows, shapes):
    flat = rows.reshape(-1)
    out, pos = [], 0
    for shp in shapes:
        size = math.prod(shp)
        out.append(lax.slice_in_dim(flat, pos, pos + size).reshape(shp))
        pos += size
    return out


def kernel(x, positions, attn_norm, w_in, dn_conv_w, dn_a_log, dn_dt_bias, dn_norm_w, lru_conv_w, lru_conv_b, lru_wa, lru_ba, lru_wx, lru_bx, lru_lambda, w_branch, w_out, ffn_norm, w_up, ffn_conv_w, ffn_conv_b, w_down, final_norm, loss_target, m_attn_norm, m_w_in, m_dn_conv_w, m_dn_a_log, m_dn_dt_bias, m_dn_norm_w, m_lru_conv_w, m_lru_conv_b, m_lru_wa, m_lru_ba, m_lru_wx, m_lru_bx, m_lru_lambda, m_w_branch, m_w_out, m_ffn_norm, m_w_up, m_ffn_conv_w, m_ffn_conv_b, m_w_down, m_final_norm, v_attn_norm, v_w_in, v_dn_conv_w, v_dn_a_log, v_dn_dt_bias, v_dn_norm_w, v_lru_conv_w, v_lru_conv_b, v_lru_wa, v_lru_ba, v_lru_wx, v_lru_bx, v_lru_lambda, v_w_branch, v_w_out, v_ffn_norm, v_w_up, v_ffn_conv_w, v_ffn_conv_b, v_w_down, v_final_norm):
    w = dict(attn_norm=attn_norm, w_in=w_in, dn_conv_w=dn_conv_w, dn_a_log=dn_a_log, dn_dt_bias=dn_dt_bias, dn_norm_w=dn_norm_w,
             lru_conv_w=lru_conv_w, lru_conv_b=lru_conv_b, lru_wa=lru_wa, lru_ba=lru_ba, lru_wx=lru_wx, lru_bx=lru_bx,
             lru_lambda=lru_lambda, w_branch=w_branch, w_out=w_out, ffn_norm=ffn_norm, w_up=w_up, ffn_conv_w=ffn_conv_w,
             ffn_conv_b=ffn_conv_b, w_down=w_down, final_norm=final_norm)
    m = dict(attn_norm=m_attn_norm, w_in=m_w_in, dn_conv_w=m_dn_conv_w, dn_a_log=m_dn_a_log, dn_dt_bias=m_dn_dt_bias,
             dn_norm_w=m_dn_norm_w, lru_conv_w=m_lru_conv_w, lru_conv_b=m_lru_conv_b, lru_wa=m_lru_wa, lru_ba=m_lru_ba,
             lru_wx=m_lru_wx, lru_bx=m_lru_bx, lru_lambda=m_lru_lambda, w_branch=m_w_branch, w_out=m_w_out, ffn_norm=m_ffn_norm,
             w_up=m_w_up, ffn_conv_w=m_ffn_conv_w, ffn_conv_b=m_ffn_conv_b, w_down=m_w_down, final_norm=m_final_norm)
    v = dict(attn_norm=v_attn_norm, w_in=v_w_in, dn_conv_w=v_dn_conv_w, dn_a_log=v_dn_a_log, dn_dt_bias=v_dn_dt_bias,
             dn_norm_w=v_dn_norm_w, lru_conv_w=v_lru_conv_w, lru_conv_b=v_lru_conv_b, lru_wa=v_lru_wa, lru_ba=v_lru_ba,
             lru_wx=v_lru_wx, lru_bx=v_lru_bx, lru_lambda=v_lru_lambda, w_branch=v_w_branch, w_out=v_w_out, ffn_norm=v_ffn_norm,
             w_up=v_w_up, ffn_conv_w=v_ffn_conv_w, ffn_conv_b=v_ffn_conv_b, w_down=v_w_down, final_norm=v_final_norm)

    me = 4 * lax.axis_index("x") + 2 * lax.axis_index("y") + lax.axis_index("c")
    axes = dict(BIG + SMALL_SHARDED)
    conv_names = [k for k, _ in SMALL_SHARDED]

    gathers, tokens, conv_full = {}, [], {}
    for layer in range(DEPTH):
        first = [w["w_in"][layer].astype(bf16)] + ([w[k] for k in conv_names] if layer == 0 else [])
        rest = [w[k][layer].astype(bf16) for k in REST]
        for part, srcs in (("in", first), ("rest", rest)):
            gathers[layer, part], token = exchange_start(srcs, [False] * len(srcs), "gather_%s_start%d" % (part, layer))
            tokens.append(token[0:1, 0:1])
    all_started = functools.reduce(lambda a, b: a + b, tokens)

    def join(land, axis):
        if axis == 0:
            return land.reshape((N_DEV * land.shape[1],) + land.shape[2:])
        return jnp.concatenate([land[p] for p in range(N_DEV)], axis=axis)

    def split(g, axis):
        if isinstance(g, tuple):
            each = N_DEV // len(g)
            size = g[0].shape[axis] // each
            return jnp.stack([lax.slice_in_dim(piece, p * size, (p + 1) * size, axis=axis) for piece in g for p in range(each)])
        size = g.shape[axis] // N_DEV
        if axis == 0:
            return g.reshape((N_DEV, size) + g.shape[1:])
        return jnp.stack([lax.slice_in_dim(g, p * size, (p + 1) * size, axis=axis) for p in range(N_DEV)])

    def layer_weights(layer, x_in):
        lands = exchange_wait(gathers[layer, "in"], x_in, "gather_in_wait%d" % layer)
        lw = {"w_in": join(lands[0], 1)}
        if layer == 0:
            conv_full.update({k: join(lands[1 + i], axes[k]) for i, k in enumerate(conv_names)})
        lw.update({k: conv_full[k][layer] for k in conv_names})
        lw.update({k: w[k][layer] for k in REPLICATED if k != "final_norm"})
        if layer == 0:
            lw["attn_norm"] = lw["attn_norm"] + all_started[0]

        def fetch_rest(after):
            lands_r = exchange_wait(gathers[layer, "rest"], after, "gather_rest_wait%d" % layer)
            return {k: join(lands_r[i], axes[k] - 1) for i, k in enumerate(REST)}

        return lw, fetch_rest

    small_names = conv_names + [k for k in REPLICATED if k != "final_norm"]
    groups = {"ffn": ("w_up", "w_down"), "mix": ("w_in", "w_branch", "w_out")}
    scatters, small_shapes, head = {}, {}, {}

    def on_grads(layer, group, gr):
        if group == "small":
            small_shapes.update({k: gr[k].shape for k in small_names})
            srcs = [_pack([gr[k] for k in small_names], f32)]
            srcs += [_pack([head["loss"].reshape(1), head["d_final"]], f32)] if layer == DEPTH - 1 else []
            modes = [False] * len(srcs)
        else:
            srcs = [split(gr[k], axes[k] - 1).astype(bf16) for k in groups[group]]
            modes = [True] * len(srcs)
        scatters[layer, group], token = exchange_start(srcs, modes, "scatter_%s_start%d" % (group, layer))
        return token

    grad_x = forward_backward(x, positions, loss_target, layer_weights, final_norm,
                              lambda loss_part, d_final: head.update(loss=loss_part, d_final=d_final), on_grads)

    big_sums, small_sums = {}, {}
    for group in ("ffn", "mix"):
        for layer in reversed(range(DEPTH)):
            lands = exchange_wait(scatters[layer, group], grad_x, "scatter_%s_wait%d" % (group, layer))
            for i, k in enumerate(groups[group]):
                shard = w[k].shape[1:]
                big_sums[layer, k] = sum_slots(lands[i].reshape(N_DEV, -1, shard[-1]), "sum_" + k).reshape(shard)
    for layer in reversed(range(DEPTH)):
        lands = exchange_wait(scatters[layer, "small"], grad_x, "scatter_small_wait%d" % layer)
        small_sums[layer] = sum_slots(lands[0], "sum_small")
        if layer == DEPTH - 1:
            head_sum = _unpack(sum_slots(lands[1], "sum_head"), [(1,), final_norm.shape])
    grads = {k: jnp.stack([big_sums[layer, k] for layer in range(DEPTH)]) for k, _ in BIG}
    loss, grads["final_norm"] = head_sum[0][0], head_sum[1]
    small_flat = jnp.stack([small_sums[layer] for layer in range(DEPTH)]).reshape(DEPTH, -1)
    pos = 0
    for k in small_names:
        size = math.prod(small_shapes[k])
        g = lax.slice_in_dim(small_flat, pos, pos + size, axis=1).reshape((DEPTH,) + small_shapes[k])
        pos += size
        ax = dict(SMALL_SHARDED).get(k)
        if ax is None:
            grads[k] = g
        else:
            size = g.shape[ax] // N_DEV
            grads[k] = lax.dynamic_slice_in_dim(g, me * size, size, axis=ax)

    upd = {k: adamw(w[k], grads[k], m[k], v[k], "adamw_" + k) for k in WEIGHTS}
    return (loss, grad_x, *[grads[k] for k in WEIGHTS], *[upd[k][0] for k in WEIGHTS], *[upd[k][1] for k in WEIGHTS],
            *[upd[k][2] for k in WEIGHTS])
```

```python
import functools
import math

import jax
import jax.numpy as jnp
from jax import lax
from jax.experimental import pallas as pl
from jax.experimental.pallas import tpu as pltpu

f32 = jnp.float32
bf16 = jnp.bfloat16

D_MODEL = 1024
DEPTH = 4
CHUNK = 64
EPS = 1e-6
DN_HEADS, DN_DK = 4, 128
RET_HEADS, RET_DK, RET_DV = 4, 64, 128
ROPE_BASE = 10000.0
LRU_C = 8.0
D_FF = 2816
N_DEV = 8
LANE = 128
VMEM_LIMIT = 56 * 1024 * 1024

ADAM_LR, ADAM_B1, ADAM_B2, ADAM_EPS, ADAM_WD, ADAM_STEP = 0.001, 0.9, 0.999, 1e-8, 0.01, 10

U_GATES, U_QKV, U_RV, U_RG, U_Z, U_CX, U_CG, U_RQ, U_RK, U_AB = (
    0, 3072, 4608, 5120, 5632, 6144, 6656, 7168, 7424, 7680)
U_PAD = 8192
_IN_SEGS = ((0, 1536, U_QKV), (1536, 8, U_AB), (1544, 512, U_Z), (2056, 256, U_RQ), (2312, 256, U_RK),
            (2568, 512, U_RV), (3080, 512, U_RG), (3592, 512, U_CX), (4104, 512, U_CG), (4616, 3072, U_GATES))
N_IN = 7688


def _params():
    return pltpu.CompilerParams(vmem_limit_bytes=VMEM_LIMIT)


def _pick(dim, pref):
    best = None
    for d in range(LANE, min(dim, pref) + 1, LANE):
        if dim % d == 0:
            best = d
    return best if best is not None else dim


@functools.partial(jax.custom_vjp, nondiff_argnums=(1, 2))
def sroll(x, shift, axis):
    return pltpu.roll(x, shift, axis)


def _sroll_fwd(x, shift, axis):
    return pltpu.roll(x, shift, axis), None


def _sroll_bwd(shift, axis, _, g):
    n = g.shape[axis]
    return (pltpu.roll(g, (n - shift) % n, axis),)


sroll.defvjp(_sroll_fwd, _sroll_bwd)

_DIMS = {"nn": (((1,), (0,)), ((), ())), "nt": (((1,), (1,)), ((), ())), "tn": (((0,), (0,)), ((), ()))}


def _dg(a, b, dims):
    return lax.dot_general(a.astype(bf16), b.astype(bf16), _DIMS[dims], preferred_element_type=f32)


@functools.partial(jax.custom_vjp, nondiff_argnums=(2,))
def bdot(a, b, dims):
    return _dg(a, b, dims)


def _bdot_fwd(a, b, dims):
    return _dg(a, b, dims), (a.astype(bf16), b.astype(bf16))


def _bdot_bwd(dims, res, g):
    a, b = res
    if dims == "nn":
        return _dg(g, b, "nt"), _dg(a, g, "tn")
    if dims == "nt":
        return _dg(g, b, "nn"), _dg(g, a, "tn")
    return _dg(b, g, "nt"), _dg(a, g, "nn")


bdot.defvjp(_bdot_fwd, _bdot_bwd)


def _fdot(a, b, dims):
    return lax.dot_general(a, b, _DIMS[dims], precision=lax.Precision.HIGH, preferred_element_type=f32)


@jax.custom_vjp
def unit_lower_inv_all(mats):
    shape = mats[0].shape
    row = lax.broadcasted_iota(jnp.int32, shape, 0)
    col = lax.broadcasted_iota(jnp.int32, shape, 1)
    eye = jnp.where(row == col, 1.0, 0.0).astype(f32)
    n = [-a for a in mats]
    p = [eye + x for x in n]
    span = 2
    while span < shape[0]:
        n = [_fdot(x, x, "nn") for x in n]
        p = [y + _fdot(y, x, "nn") for y, x in zip(p, n)]
        span *= 2
    return p


def _uli_fwd(mats):
    x = unit_lower_inv_all(mats)
    return x, x


def _uli_bwd(xs, gs):
    t = [_fdot(x, g, "tn") for x, g in zip(xs, gs)]
    return ([-_fdot(y, x, "nt") for y, x in zip(t, xs)],)


unit_lower_inv_all.defvjp(_uli_fwd, _uli_bwd)


@jax.custom_vjp
def known_inverse(invs, mats):
    return invs


def _known_fwd(invs, mats):
    return invs, invs


def _known_bwd(xs, gs):
    return [jnp.zeros_like(x) for x in xs], _uli_bwd(xs, gs)[0]


known_inverse.defvjp(_known_fwd, _known_bwd)


def cumsum_rows(x):
    rows = x.shape[0]
    row = lax.broadcasted_iota(jnp.int32, x.shape, 0)
    s = 1
    while s < rows:
        x = x + jnp.where(row >= s, sroll(x, s, 0), 0.0)
        s *= 2
    return x


def _expm1(x):
    return jnp.tanh(0.5 * x) * (jnp.exp(x) + 1.0)


def _lane_pick(x, lane):
    idx = lax.broadcasted_iota(jnp.int32, x.shape, 1)
    return jnp.sum(jnp.where(idx == lane, x, 0.0), axis=1, keepdims=True)


def _row_pick(x, r):
    idx = lax.broadcasted_iota(jnp.int32, x.shape, 0)
    return jnp.sum(jnp.where(idx == r, x, 0.0), axis=0, keepdims=True)


def _causal_conv(x, halo, w, width):
    if halo is None:
        row = lax.broadcasted_iota(jnp.int32, x.shape, 0)
        acc = x * w[width - 1:width]
        for k in range(width - 1):
            shift = width - 1 - k
            acc = acc + jnp.where(row >= shift, sroll(x, shift, 0), 0.0) * w[k:k + 1]
        return acc
    xe = jnp.concatenate([halo, x], axis=0)
    acc = xe * w[width - 1:width]
    for k in range(width - 1):
        acc = acc + sroll(xe, width - 1 - k, 0) * w[k:k + 1]
    return acc[8:]


def f_norm(ins, ps):
    (x,), (g,) = ins, ps
    return [x * lax.rsqrt(jnp.mean(x * x, axis=-1, keepdims=True) + EPS) * g]


def f_dn_pre(kind, mains, halos, ps):
    y = _causal_conv(mains[0], halos[0], ps[0], 4)
    y = y * jax.nn.sigmoid(y)
    if kind < 2:
        y = y * lax.rsqrt(jnp.sum(y * y, axis=-1, keepdims=True) + EPS)
    if kind == 0:
        y = y * (DN_DK ** -0.5)
    return [y]


def f_dn_gates(ins, ps):
    u, p = ins[0][:, :LANE], ps[0]
    lane = lax.broadcasted_iota(jnp.int32, u.shape, 1)
    g = -jnp.exp(p[0:1]) * jax.nn.softplus(u + p[1:2])
    beta = jax.nn.sigmoid(u)
    return [jnp.where(lane < 4, g, jnp.where(lane < 8, beta, 0.0))]


def per_head(fn):
    def tile_fn(vals, ps):
        heads = [fn([v[:, h * LANE:(h + 1) * LANE] for v in vals], ps) for h in range(vals[0].shape[1] // LANE)]
        return [jnp.concatenate([o[i] for o in heads], axis=1) for i in range(len(heads[0]))]
    return tile_fn


def f_dn_post(ins, ps):
    (o, z), (nw,) = ins, ps
    y = o * lax.rsqrt(jnp.mean(o * o, axis=-1, keepdims=True) + EPS) * nw
    return [y * (z * jax.nn.sigmoid(z))]


def _rot_half(t):
    lane = lax.broadcasted_iota(jnp.int32, t.shape, 1)
    width = t.shape[1]
    first = (lane % RET_DK) < (RET_DK // 2)
    return jnp.where(first, -sroll(t, width - RET_DK // 2, 1), sroll(t, RET_DK // 2, 1))


def f_ret_pre(ins, ps):
    q, k, cos, sin = ins
    qr = q * cos + _rot_half(q) * sin
    kr = (k * cos + _rot_half(k) * sin) * (RET_DK ** -0.5)
    return [qr, kr]


def f_ret_post(ins, ps):
    o, g = ins
    mu = jnp.mean(o, axis=-1, keepdims=True)
    var = jnp.mean(jnp.square(o - mu), axis=-1, keepdims=True)
    return [(o - mu) * lax.rsqrt(var + EPS) * (g * jax.nn.sigmoid(g))]


def f_lru_pre(mains, halos, ps):
    cw, cb, wa, ba, wx, bx, lam = ps
    xc = _causal_conv(mains[0], halos[0], cw, 4) + cb
    r = jax.nn.sigmoid(bdot(xc, wa, "nn") + ba)
    i = jax.nn.sigmoid(bdot(xc, wx, "nn") + bx)
    log_a = -LRU_C * r * jax.nn.softplus(-lam)
    a = jnp.exp(log_a)
    b = jnp.sqrt(-_expm1(2.0 * log_a)) * (i * xc)
    return [a, b]


def f_lru_post(ins, ps):
    h, g = ins
    return [h * jax.nn.gelu(g)]


def f_ffn_mid(mains, halos, ps):
    cwg, cwv, cbg, cbv = ps
    gate = _causal_conv(mains[0], halos[0], cwg, 3) + cbg
    val = _causal_conv(mains[1], halos[1], cwv, 3) + cbv
    return [gate * jax.nn.sigmoid(gate) * val]


def mm(a, b, dims, name, add=None, dep=None, b_koff=0, tm=1536, tn=1536, tk=2816):
    if dims == "tn":
        kdim, m = a.shape
        n = b.shape[1]
    else:
        m, kdim = a.shape
        n = b.shape[0] if dims == "nt" else b.shape[1]
    tm, tn, tk = _pick(m, tm), _pick(n, tn), _pick(kdim, tk)
    nk = kdim // tk
    a_spec = pl.BlockSpec((tk, tm), lambda i, j, k: (k, i)) if dims == "tn" else pl.BlockSpec((tm, tk), lambda i, j, k: (i, k))
    b_spec = (pl.BlockSpec((tn, tk), lambda i, j, k: (j, k + b_koff * nk)) if dims == "nt"
              else pl.BlockSpec((tk, tn), lambda i, j, k: (k, j)))
    o_spec = pl.BlockSpec((tm, tn), lambda i, j, k: (i, j))
    has_add, has_dep = add is not None, dep is not None

    def body(*refs):
        a_ref, b_ref = refs[:2]
        add_ref = refs[2] if has_add else None
        o_ref = refs[2 + has_add + has_dep]
        if nk == 1:
            prod = _dg(a_ref[...], b_ref[...], dims)
            o_ref[...] = prod + add_ref[...] if has_add else prod
            return
        acc_ref = refs[-1]
        k = pl.program_id(2)

        @pl.when(k == 0)
        def _():
            acc_ref[...] = jnp.zeros_like(acc_ref)

        acc_ref[...] += _dg(a_ref[...], b_ref[...], dims)

        @pl.when(k == nk - 1)
        def _():
            o_ref[...] = acc_ref[...] + add_ref[...] if has_add else acc_ref[...]

    args = [a, b] + ([add] if has_add else []) + ([dep] if has_dep else [])
    in_specs = [a_spec, b_spec] + ([o_spec] if has_add else [])
    in_specs += [pl.BlockSpec((8, LANE), lambda i, j, k: (0, 0))] if has_dep else []
    return pl.pallas_call(
        body, name=name, grid=(m // tm, n // tn, nk), in_specs=in_specs, out_specs=o_spec,
        out_shape=jax.ShapeDtypeStruct((m, n), f32), scratch_shapes=[pltpu.VMEM((tm, tn), f32)] if nk > 1 else [],
        compiler_params=_params())(*args)


def rowmap(fn, ins, params, outs, ncol, name, rows=512):
    n = ins[0][0].shape[0]
    r = min(rows, n)
    nin, npar = len(ins), len(params)

    def body(*refs):
        vals = [x[...] for x in refs[:nin]]
        pv = [p[...] for p in refs[nin:nin + npar]]
        for o_ref, o in zip(refs[nin + npar:], fn(vals, pv)):
            o_ref[...] = o.astype(o_ref.dtype)

    in_specs = [pl.BlockSpec((r, cb), functools.partial(lambda j, i, off: (i, off + j), off=off)) for _, cb, off in ins]
    in_specs += [pl.BlockSpec(bs, functools.partial(lambda j, i, f: f(j), f=f)) for _, bs, f in params]
    out_specs = [pl.BlockSpec((r, cb), lambda j, i: (i, j)) for cb, _ in outs]
    out_shape = [jax.ShapeDtypeStruct((n, cb * ncol), dt) for cb, dt in outs]
    res = pl.pallas_call(body, name=name, grid=(ncol, n // r), in_specs=in_specs, out_specs=out_specs,
                         out_shape=out_shape, compiler_params=_params())(*[a for a, _, _ in ins], *[a for a, _, _ in params])
    return res


def rowmap_bwd(fn, ins, params, douts, ncol, name, rows=512, add=None, din_dtypes=None, into=None, copy16=None):
    n = ins[0][0].shape[0]
    r = min(rows, n)
    nin, npar, nout = len(ins), len(params), len(douts)
    add = [None] * nin if add is None else list(add)
    add_idx = [i for i in range(nin) if add[i] is not None]
    din_dtypes = [f32] * nin if din_dtypes is None else list(din_dtypes)
    into_buf, into_off, into_idx = into if into is not None else (None, 0, [])
    has_into, has_copy = into is not None, copy16 is not None
    kept = [i for i in range(nin) if din_dtypes[i] is not None and i not in into_idx]

    def body(*refs):
        vals = [x[...] for x in refs[:nin]]
        pv = [p[...] for p in refs[nin:nin + npar]]
        dys = [d[...] for d in refs[nin + npar:nin + npar + nout]]
        k0 = nin + npar + nout
        add_refs = dict(zip(add_idx, refs[k0:k0 + len(add_idx)]))
        k0 += len(add_idx) + has_into
        din_refs = refs[k0:k0 + len(kept)]
        k0 += len(kept)
        copy_ref = refs[k0] if has_copy else None
        into_ref = refs[k0 + has_copy] if has_into else None
        dp_refs = refs[k0 + has_copy + has_into:]
        _, vjp = jax.vjp(fn, vals, pv)
        dvals, dpv = vjp(dys)
        cot = lambda idx: dvals[idx] + add_refs[idx][...] if idx in add_refs else dvals[idx]
        for d_ref, idx in zip(din_refs, kept):
            d_ref[...] = cot(idx).astype(d_ref.dtype)
        if has_copy:
            copy_ref[...] = cot(copy16).astype(copy_ref.dtype)
        if has_into:
            parts = [cot(idx) for idx in into_idx]
            into_ref[...] = (parts[0] if len(parts) == 1 else jnp.concatenate(parts, axis=1)).astype(into_ref.dtype)

        @pl.when(pl.program_id(1) == 0)
        def _():
            for d_ref in dp_refs:
                d_ref[...] = jnp.zeros_like(d_ref)

        for d_ref, d in zip(dp_refs, dpv):
            d_ref[...] += d

    in_specs = [pl.BlockSpec((r, cb), functools.partial(lambda j, i, off: (i, off + j), off=off)) for _, cb, off in ins]
    in_specs += [pl.BlockSpec(bs, functools.partial(lambda j, i, f: f(j), f=f)) for _, bs, f in params]
    in_specs += [pl.BlockSpec((r, d.shape[1] // ncol), lambda j, i: (i, j)) for d in douts]
    in_specs += [pl.BlockSpec((r, ins[i][1]), lambda j, i: (i, j)) for i in add_idx]
    out_specs = [pl.BlockSpec((r, ins[i][1]), lambda j, i: (i, j)) for i in kept]
    out_shape = [jax.ShapeDtypeStruct((n, ins[i][1] * ncol), din_dtypes[i]) for i in kept]
    args = [a for a, _, _ in ins] + [a for a, _, _ in params] + list(douts) + [add[i] for i in add_idx]
    aliases = {}
    if has_copy:
        out_specs += [pl.BlockSpec((r, ins[copy16][1]), lambda j, i: (i, j))]
        out_shape += [jax.ShapeDtypeStruct((n, ins[copy16][1] * ncol), bf16)]
    if has_into:
        assert ncol == 1
        in_specs += [pl.BlockSpec(memory_space=pl.ANY)]
        aliases[len(args)] = len(out_shape)
        args += [into_buf]
        out_specs += [pl.BlockSpec((r, sum(ins[i][1] for i in into_idx)), lambda j, i: (i, into_off))]
        out_shape += [jax.ShapeDtypeStruct(into_buf.shape, into_buf.dtype)]
    pshapes = [tuple(d for d in bs if d is not None) for _, bs, _ in params]
    out_specs += [pl.BlockSpec((None,) + ps, functools.partial(lambda j, i, nd: (j,) + (0,) * nd, nd=len(ps))) for ps in pshapes]
    out_shape += [jax.ShapeDtypeStruct((ncol,) + ps, f32) for ps in pshapes]
    res = pl.pallas_call(body, name=name, grid=(ncol, n // r), in_specs=in_specs, out_specs=out_specs, out_shape=out_shape,
                         input_output_aliases=aliases, compiler_params=_params())(*args)
    dins = [None] * nin
    for pos, i in enumerate(kept):
        dins[i] = res[pos]
    pos = len(kept)
    extras = {}
    if has_copy:
        extras["copy16"] = res[pos]
        pos += 1
    if has_into:
        extras["into"] = res[pos]
        pos += 1
    return dins, res[pos:], extras


SEQ_ROWS = 2048


def seqmap(fn, ins, params, nouts, ncol, name, out_dtype=f32):
    bsz, seq, _ = ins[0][0].shape
    r = min(SEQ_ROWS, seq)
    nin, npar = len(ins), len(params)

    def body(*refs):
        in_refs = refs[:nin]
        pv = [p[...] for p in refs[nin:nin + npar]]
        out_refs = refs[nin + npar:]

        def step(i, carry):
            r0 = pl.multiple_of(i * r, r)
            h0 = pl.multiple_of(jnp.maximum(r0 - 8, 0), 8)
            mains = [x[pl.ds(r0, r), :] for x in in_refs]
            halos = [jnp.where(i == 0, 0.0, x[pl.ds(h0, 8), :]) for x in in_refs]
            for o_ref, o in zip(out_refs, fn(mains, halos, pv)):
                o_ref[pl.ds(r0, r), :] = o.astype(o_ref.dtype)
            return carry

        if r == seq:
            for o_ref, o in zip(out_refs, fn([x[...] for x in in_refs], [None] * nin, pv)):
                o_ref[...] = o.astype(o_ref.dtype)
        else:
            lax.fori_loop(0, seq // r, step, 0)

    in_specs = [pl.BlockSpec((None, seq, LANE), functools.partial(lambda j, b, off: (b, 0, off + j), off=off)) for _, off in ins]
    in_specs += [pl.BlockSpec(bs, functools.partial(lambda j, b, f: f(j), f=f)) for _, bs, f in params]
    out_specs = [pl.BlockSpec((None, seq, LANE), lambda j, b: (b, 0, j)) for _ in range(nouts)]
    out_shape = [jax.ShapeDtypeStruct((bsz, seq, LANE * ncol), out_dtype) for _ in range(nouts)]
    return pl.pallas_call(body, name=name, grid=(ncol, bsz), in_specs=in_specs, out_specs=out_specs,
                          out_shape=out_shape, compiler_params=_params())(*[a for a, _ in ins], *[a for a, _, _ in params])


def seqmap_bwd(fn, ins, params, douts, ncol, name, din_dtype=f32, into=None):
    bsz, seq, _ = ins[0][0].shape
    r = min(SEQ_ROWS, seq)
    nin, npar, nout = len(ins), len(params), len(douts)
    narrow = din_dtype != f32

    def body(*refs):
        in_refs = refs[:nin]
        pv = [p[...] for p in refs[nin:nin + npar]]
        dy_refs = refs[nin + npar:nin + npar + nout]
        k0 = nin + npar + nout + (into is not None)
        dout_refs = refs[k0:k0 + nin]
        dp_refs = refs[k0 + nin:k0 + nin + npar]
        din_refs = refs[k0 + nin + npar:] if narrow else dout_refs

        def step(i, dp_acc):
            r0 = pl.multiple_of(i * r, r)
            h0 = pl.multiple_of(jnp.maximum(r0 - 8, 0), 8)
            mains = [x[pl.ds(r0, r), :] for x in in_refs]
            halos_raw = [x[pl.ds(h0, 8), :] for x in in_refs]

            def tile(mains, halos_raw, pv):
                return fn(mains, [jnp.where(i == 0, 0.0, h) for h in halos_raw], pv)

            _, vjp = jax.vjp(tile, mains, halos_raw, pv)
            dm, dh, dp = vjp([d[pl.ds(r0, r), :] for d in dy_refs])
            for d_ref, m, h in zip(din_refs, dm, dh):
                d_ref[pl.ds(r0, r), :] = m
                d_ref[pl.ds(h0, 8), :] += h
            return [acc + d for acc, d in zip(dp_acc, dp)]

        if r == seq:
            _, vjp = jax.vjp(lambda mains, pv: fn(mains, [None] * nin, pv), [x[...] for x in in_refs], pv)
            dm, dp = vjp([d[...] for d in dy_refs])
            for o_ref, m in zip(dout_refs, dm):
                o_ref[...] = m.astype(o_ref.dtype)
        else:
            dp = lax.fori_loop(0, seq // r, step, [jnp.zeros(p.shape, f32) for p in pv])
            if narrow:
                for o_ref, d_ref in zip(dout_refs, din_refs):
                    o_ref[...] = d_ref[...].astype(o_ref.dtype)

        @pl.when(pl.program_id(1) == 0)
        def _():
            for d_ref in dp_refs:
                d_ref[...] = jnp.zeros_like(d_ref)

        for d_ref, d in zip(dp_refs, dp):
            d_ref[...] += d

    in_specs = [pl.BlockSpec((None, seq, LANE), functools.partial(lambda j, b, off: (b, 0, off + j), off=off)) for _, off in ins]
    in_specs += [pl.BlockSpec(bs, functools.partial(lambda j, b, f: f(j), f=f)) for _, bs, f in params]
    in_specs += [pl.BlockSpec((None, seq, LANE), lambda j, b: (b, 0, j)) for _ in range(nout)]
    out_specs = [pl.BlockSpec((None, seq, LANE), lambda j, b: (b, 0, j)) for _ in range(nin)]
    pshapes = [tuple(d for d in bs if d is not None) for _, bs, _ in params]
    out_specs += [pl.BlockSpec((None,) + ps, functools.partial(lambda j, b, nd: (j,) + (0,) * nd, nd=len(ps))) for ps in pshapes]
    out_shape = [jax.ShapeDtypeStruct((bsz, seq, LANE * ncol), din_dtype) for _ in range(nin)]
    out_shape += [jax.ShapeDtypeStruct((ncol,) + ps, f32) for ps in pshapes]
    args = [a for a, _ in ins] + [a for a, _, _ in params] + list(douts)
    aliases = {}
    if into is not None:
        assert nin == 1 and into[0].dtype == din_dtype
        in_specs += [pl.BlockSpec(memory_space=pl.ANY)]
        aliases[len(args)] = 0
        args += [into[0]]
        out_specs[0] = pl.BlockSpec((None, seq, LANE), lambda j, b: (b, 0, into[1] + j))
        out_shape[0] = jax.ShapeDtypeStruct(into[0].shape, din_dtype)
    res = pl.pallas_call(body, name=name, grid=(ncol, bsz), in_specs=in_specs, out_specs=out_specs, out_shape=out_shape,
                         scratch_shapes=[pltpu.VMEM((seq, LANE), f32) for _ in range(nin)] if narrow and r != seq else [],
                         input_output_aliases=aliases, compiler_params=_params())(*args)
    return res[:nin], res[nin:]


def _tri_masks():
    row = lax.broadcasted_iota(jnp.int32, (CHUNK, CHUNK), 0)
    col = lax.broadcasted_iota(jnp.int32, (CHUNK, CHUNK), 1)
    return row >= col, row > col


CHUNKS_PER_STEP = 4


def _by_rows(parts, per_row):
    rows = [jnp.concatenate(parts[i:i + per_row], axis=1) for i in range(0, len(parts), per_row)]
    return jnp.concatenate(rows, axis=0)


def dn_prep(vals, ps):
    q, k, v, gb = vals[:4]
    nchunk = q.shape[0] // CHUNK
    causal, strict = _tri_masks()
    gbs = [gb[c * CHUNK:(c + 1) * CHUNK] for c in range(nchunk)]
    gcs = [cumsum_rows(g) for g in gbs]
    gcts = [g.T for g in gcs]
    chains = [(c, h) for c in range(nchunk) for h in range(DN_HEADS)]
    part = lambda t, c, h: t[c * CHUNK:(c + 1) * CHUNK, h * DN_DK:(h + 1) * DN_DK]
    qh = [part(q, c, h) for c, h in chains]
    kh = [part(k, c, h) for c, h in chains]
    vh = [part(v, c, h) for c, h in chains]
    g_col = [_lane_pick(gcs[c], h) for c, h in chains]
    beta = [_lane_pick(gbs[c], DN_HEADS + h) for c, h in chains]
    g_row = [_row_pick(gcts[c], h)[:, :CHUNK] for c, h in chains]
    decay = [jnp.where(causal, jnp.exp(jnp.where(causal, gc - gr, 0.0)), 0.0) for gc, gr in zip(g_col, g_row)]
    k_beta = [a * b for a, b in zip(kh, beta)]
    eg = [jnp.exp(g) for g in g_col]
    kk = [bdot(a, b, "nt") for a, b in zip(k_beta, kh)]
    qk = [bdot(a, b, "nt") for a, b in zip(qh, kh)]
    lower = [jnp.where(strict, a * d, 0.0) for a, d in zip(kk, decay)]
    if len(vals) == 5:
        t_inv = known_inverse([part(vals[4], c, h)[:, :CHUNK] for c, h in chains], lower)
    else:
        t_inv = unit_lower_inv_all(lower)
    u = [bdot(t, a * b, "nn") for t, a, b in zip(t_inv, vh, beta)]
    w = [bdot(t, a * e, "nn") for t, a, e in zip(t_inv, k_beta, eg)]
    attn = [jnp.concatenate([a * d, jnp.zeros((CHUNK, DN_DK - CHUNK), f32)], axis=1) for a, d in zip(qk, decay)]
    qd = [a * e for a, e in zip(qh, eg)]
    kd = [a * jnp.exp(_row_pick(g, CHUNK - 1) - g) for a, g in zip(kh, g_col)]
    g_last = jnp.concatenate([jnp.broadcast_to(_row_pick(g, CHUNK - 1), g.shape) for g in gcs], axis=0)
    outs = [_by_rows(t, DN_HEADS) for t in (u, w, attn, qd, kd)] + [g_last]
    if len(vals) == 4:
        wide = [jnp.concatenate([t, jnp.zeros((CHUNK, DN_DK - CHUNK), f32)], axis=1) for t in t_inv]
        outs.append(_by_rows(wide, DN_HEADS))
    return outs


def dn_step(state, u, w, attn, qd, kd, g_last):
    bsz = u.shape[0]
    chains = [(b, h) for b in range(bsz) for h in range(DN_HEADS)]
    part = lambda t, b, h: t[b, :, h * DN_DK:(h + 1) * DN_DK]
    ws = [bdot(part(w, b, h), s, "nn") for (b, h), s in zip(chains, state)]
    qs = [bdot(part(qd, b, h), s, "nn") for (b, h), s in zip(chains, state)]
    v_new = [part(u, b, h) - x for (b, h), x in zip(chains, ws)]
    av = [bdot(attn[b, :, h * DN_DK:h * DN_DK + CHUNK], x, "nn") for (b, h), x in zip(chains, v_new)]
    kv = [bdot(part(kd, b, h), x, "tn") for (b, h), x in zip(chains, v_new)]
    ge = [jnp.exp(_row_pick(_lane_pick(g_last[b], h), 0)) for b, h in chains]
    new_state = [s * g + x for s, g, x in zip(state, ge, kv)]
    outs = [a + b for a, b in zip(qs, av)]
    return new_state, jnp.concatenate([jnp.concatenate(outs[b * DN_HEADS:(b + 1) * DN_HEADS], axis=1)[None]
                                       for b in range(bsz)], axis=0)


def _ret_log_gamma(h):
    return math.log(1.0 - 2.0 ** (-5.0 - h))


def ret_prep(vals, ps):
    q, k, v = vals
    nchunk = q.shape[0] // CHUNK
    causal, _ = _tri_masks()
    row = lax.broadcasted_iota(jnp.int32, (CHUNK, CHUNK), 0)
    col = lax.broadcasted_iota(jnp.int32, (CHUNK, CHUNK), 1)
    dist = (row - col).astype(f32)
    lane = lax.broadcasted_iota(jnp.int32, (CHUNK, q.shape[1]), 1)
    dmask = [jnp.where(causal, jnp.exp(jnp.where(causal, dist, 0.0) * _ret_log_gamma(h)), 0.0) for h in range(RET_HEADS)]
    chains = [(c, h) for c in range(nchunk) for h in range(RET_HEADS)]
    rows = lambda t, c: t[c * CHUNK:(c + 1) * CHUNK]
    scores = [bdot(jnp.where((lane // RET_DK) == h, rows(q, c), 0.0), rows(k, c), "nt") * dmask[h] for c, h in chains]
    inner = [bdot(s, rows(v, c)[:, h * RET_DV:(h + 1) * RET_DV], "nn") for s, (c, h) in zip(scores, chains)]
    return [_by_rows(inner, RET_HEADS)]


def ret_step(state, q, k, v, inner):
    bsz = q.shape[0]
    idx = lax.broadcasted_iota(jnp.int32, (CHUNK, 1), 0).astype(f32)
    lane = lax.broadcasted_iota(jnp.int32, (CHUNK, q.shape[2]), 1)
    chains = [(b, h) for b in range(bsz) for h in range(RET_HEADS)]
    part = lambda t, b, h: t[b, :, h * RET_DV:(h + 1) * RET_DV]
    cross = [bdot(q[b], s, "nn") for (b, h), s in zip(chains, state)]
    kz = [jnp.where((lane // RET_DK) == h, k[b], 0.0) * jnp.exp((CHUNK - 1.0 - idx) * _ret_log_gamma(h)) for b, h in chains]
    kv = [bdot(a, part(v, b, h), "tn") for a, (b, h) in zip(kz, chains)]
    outs = [x * jnp.exp((idx + 1.0) * _ret_log_gamma(h)) + part(inner, b, h) for x, (b, h) in zip(cross, chains)]
    new_state = [s * math.exp(CHUNK * _ret_log_gamma(h)) + x for s, x, (b, h) in zip(state, kv, chains)]
    return new_state, jnp.concatenate([jnp.concatenate(outs[b * RET_HEADS:(b + 1) * RET_HEADS], axis=1)[None]
                                       for b in range(bsz)], axis=0)


SCAN_CHUNKS = 8


def chunk_scan(step_fn, ins, state_shape, out_width, name):
    bsz, seq, _ = ins[0].shape
    nchunk = seq // CHUNK
    nin = len(ins)
    nh = state_shape[0]
    per = SCAN_CHUNKS if nchunk % SCAN_CHUNKS == 0 else 1

    def body(*refs):
        in_refs = refs[:nin]
        o_ref, ck_ref, s_ref = refs[nin:]

        @pl.when(pl.program_id(0) == 0)
        def _():
            s_ref[...] = jnp.zeros_like(s_ref)

        state = [s_ref[i] for i in range(bsz * nh)]
        for c in range(per):
            rows = slice(c * CHUNK, (c + 1) * CHUNK)
            for i in range(bsz * nh):
                ck_ref[i // nh, c, i % nh] = state[i]
            state, out = step_fn(state, *[x[:, rows, :].astype(f32) for x in in_refs])
            o_ref[:, rows, :] = out
        for i in range(bsz * nh):
            s_ref[i] = state[i]

    in_specs = [pl.BlockSpec((bsz, per * CHUNK, x.shape[2]), lambda n: (0, n, 0)) for x in ins]
    out_specs = [pl.BlockSpec((bsz, per * CHUNK, out_width), lambda n: (0, n, 0)),
                 pl.BlockSpec((bsz, per) + tuple(state_shape), lambda n: (0, n, 0, 0, 0))]
    out_shape = [jax.ShapeDtypeStruct((bsz, seq, out_width), f32),
                 jax.ShapeDtypeStruct((bsz, nchunk) + tuple(state_shape), f32)]
    return pl.pallas_call(body, name=name, grid=(nchunk // per,), in_specs=in_specs, out_specs=out_specs, out_shape=out_shape,
                          scratch_shapes=[pltpu.VMEM((bsz * nh,) + tuple(state_shape[1:]), f32)],
                          compiler_params=_params())(*ins)


def chunk_scan_bwd(step_fn, ins, ckpt, dout, name):
    bsz, seq, _ = ins[0].shape
    nchunk = seq // CHUNK
    nin = len(ins)
    state_shape = ckpt.shape[2:]
    nh = state_shape[0]
    per = SCAN_CHUNKS if nchunk % SCAN_CHUNKS == 0 else 1
    nstep = nchunk // per

    def body(*refs):
        in_refs = refs[:nin]
        ck_ref, do_ref = refs[nin:nin + 2]
        din_refs = refs[nin + 2:nin + 2 + nin]
        ds_ref = refs[-1]

        @pl.when(pl.program_id(0) == 0)
        def _():
            ds_ref[...] = jnp.zeros_like(ds_ref)

        dstate = [ds_ref[i] for i in range(bsz * nh)]
        for c in reversed(range(per)):
            rows = slice(c * CHUNK, (c + 1) * CHUNK)
            state = [ck_ref[i // nh, c, i % nh] for i in range(bsz * nh)]
            _, vjp = jax.vjp(step_fn, state, *[x[:, rows, :].astype(f32) for x in in_refs])
            grads = vjp((dstate, do_ref[:, rows, :]))
            dstate = grads[0]
            for d_ref, d in zip(din_refs, grads[1:]):
                d_ref[:, rows, :] = d
        for i in range(bsz * nh):
            ds_ref[i] = dstate[i]

    rev = lambda n: (0, nstep - 1 - n, 0)
    in_specs = [pl.BlockSpec((bsz, per * CHUNK, x.shape[2]), rev) for x in ins]
    in_specs += [pl.BlockSpec((bsz, per) + tuple(state_shape), lambda n: (0, nstep - 1 - n, 0, 0, 0)),
                 pl.BlockSpec((bsz, per * CHUNK, dout.shape[2]), rev)]
    out_specs = [pl.BlockSpec((bsz, per * CHUNK, x.shape[2]), rev) for x in ins]
    out_shape = [jax.ShapeDtypeStruct(x.shape, f32) for x in ins]
    return pl.pallas_call(body, name=name, grid=(nstep,), in_specs=in_specs, out_specs=out_specs, out_shape=out_shape,
                          scratch_shapes=[pltpu.VMEM((bsz * nh,) + tuple(state_shape[1:]), f32)],
                          compiler_params=_params())(*ins, ckpt, dout)


LRU_ROWS = 512


def lru_scan(a, b):
    bsz, seq, width = a.shape
    rb = min(LRU_ROWS, seq)
    seqs = range(bsz)

    def body(a_ref, b_ref, h_ref, hp_ref, carry_ref):
        @pl.when(pl.program_id(0) == 0)
        def _():
            carry_ref[...] = jnp.zeros_like(carry_ref)

        row = lax.broadcasted_iota(jnp.int32, (8, width), 0)

        def tile(t, hprev):
            r0 = pl.multiple_of(t * 8, 8)
            ca = [a_ref[i, pl.ds(r0, 8), :] for i in seqs]
            cb = [b_ref[i, pl.ds(r0, 8), :] for i in seqs]
            for s in (1, 2, 4):
                m = row >= s
                cb = [jnp.where(m, x * pltpu.roll(y, s, 0) + y, y) for x, y in zip(ca, cb)]
                ca = [jnp.where(m, x * pltpu.roll(x, s, 0), x) for x in ca]
            h = [y + x * p for x, y, p in zip(ca, cb, hprev)]
            for i in seqs:
                h_ref[i, pl.ds(r0, 8), :] = h[i]
                hp_ref[i, pl.ds(r0, 8), :] = jnp.where(row == 0, hprev[i], pltpu.roll(h[i], 1, 0))
            return tuple(_row_pick(x, 7) for x in h)

        last = lax.fori_loop(0, rb // 8, tile, tuple(carry_ref[i:i + 1, :] for i in seqs))
        for i in seqs:
            carry_ref[i:i + 1, :] = last[i]

    spec = pl.BlockSpec((bsz, rb, width), lambda i: (0, i, 0))
    return pl.pallas_call(body, name="lru_scan", grid=(seq // rb,), in_specs=[spec, spec], out_specs=[spec, spec],
                          out_shape=[jax.ShapeDtypeStruct(a.shape, f32)] * 2,
                          scratch_shapes=[pltpu.VMEM((max(8, bsz), width), f32)], compiler_params=_params())(a, b)


def lru_scan_bwd(a, hp, dh):
    bsz, seq, width = a.shape
    rb = min(LRU_ROWS, seq)
    nblk = seq // rb
    seqs = range(bsz)

    def body(a_ref, hp_ref, dh_ref, da_ref, db_ref, carry_ref):
        @pl.when(pl.program_id(0) == 0)
        def _():
            carry_ref[...] = jnp.zeros_like(carry_ref)

        row = lax.broadcasted_iota(jnp.int32, (8, width), 0)
        ntile = rb // 8

        def tile(t, mu_next):
            r0 = pl.multiple_of((ntile - 1 - t) * 8, 8)
            ca = [a_ref[i, pl.ds(r0, 8), :] for i in seqs]
            dh_t = [dh_ref[i, pl.ds(r0, 8), :] for i in seqs]
            cb = [x * y for x, y in zip(ca, dh_t)]
            for s in (1, 2, 4):
                m = row < 8 - s
                cb = [jnp.where(m, x * pltpu.roll(y, 8 - s, 0) + y, y) for x, y in zip(ca, cb)]
                ca = [jnp.where(m, x * pltpu.roll(x, 8 - s, 0), x) for x in ca]
            mu = [y + x * p for x, y, p in zip(ca, cb, mu_next)]
            for i in seqs:
                lam = dh_t[i] + jnp.where(row == 7, mu_next[i], pltpu.roll(mu[i], 7, 0))
                db_ref[i, pl.ds(r0, 8), :] = lam
                da_ref[i, pl.ds(r0, 8), :] = lam * hp_ref[i, pl.ds(r0, 8), :]
            return tuple(_row_pick(x, 0) for x in mu)

        last = lax.fori_loop(0, ntile, tile, tuple(carry_ref[i:i + 1, :] for i in seqs))
        for i in seqs:
            carry_ref[i:i + 1, :] = last[i]

    spec = pl.BlockSpec((bsz, rb, width), lambda i: (0, nblk - 1 - i, 0))
    return pl.pallas_call(body, name="lru_scan_bwd", grid=(nblk,), in_specs=[spec] * 3, out_specs=[spec, spec],
                          out_shape=[jax.ShapeDtypeStruct(a.shape, f32)] * 2,
                          scratch_shapes=[pltpu.VMEM((max(8, bsz), width), f32)], compiler_params=_params())(a, hp, dh)


MERGE_ROWS = 512


def branch_merge(ys, w_branch, u):
    n = ys[0].shape[0]
    tm = min(MERGE_ROWS, n)

    def body(ya, yb, yc, w_ref, g0, g1, g2, o_ref):
        acc = None
        for i, (y_ref, g_ref) in enumerate(((ya, g0), (yb, g1), (yc, g2))):
            term = jax.nn.sigmoid(g_ref[...]) * _dg(y_ref[...], w_ref[i], "nn")
            acc = term if acc is None else acc + term
        o_ref[...] = acc.astype(o_ref.dtype)

    y_spec = pl.BlockSpec((tm, ys[0].shape[1]), lambda i: (i, 0))
    g_specs = [pl.BlockSpec((tm, D_MODEL), functools.partial(lambda i, k: (i, k), k=k)) for k in range(3)]
    return pl.pallas_call(
        body, name="branch_merge", grid=(n // tm,),
        in_specs=[y_spec] * 3 + [pl.BlockSpec(w_branch.shape, lambda i: (0, 0, 0))] + g_specs,
        out_specs=pl.BlockSpec((tm, D_MODEL), lambda i: (i, 0)), out_shape=jax.ShapeDtypeStruct((n, D_MODEL), bf16),
        compiler_params=_params())(*ys, w_branch, u, u, u)


def branch_merge_bwd(ys, w_branch, u, d_merged, du):
    n = ys[0].shape[0]
    tm = min(MERGE_ROWS, n)

    def body(ya, yb, yc, w_ref, g0, g1, g2, dm_ref, du_in, db0, db1, db2, du_ref):
        dm = dm_ref[...]
        d_gates = []
        for i, (y_ref, g_ref, db_ref) in enumerate(((ya, g0, db0), (yb, g1, db1), (yc, g2, db2))):
            s = jax.nn.sigmoid(g_ref[...])
            db_ref[...] = (dm * s).astype(db_ref.dtype)
            d_gates.append(dm * _dg(y_ref[...], w_ref[i], "nn") * (s * (1.0 - s)))
        du_ref[...] = jnp.concatenate(d_gates, axis=1).astype(du_ref.dtype)

    y_spec = pl.BlockSpec((tm, ys[0].shape[1]), lambda i: (i, 0))
    row = pl.BlockSpec((tm, D_MODEL), lambda i: (i, 0))
    g_specs = [pl.BlockSpec((tm, D_MODEL), functools.partial(lambda i, k: (i, k), k=k)) for k in range(3)]
    res = pl.pallas_call(
        body, name="branch_merge_bwd", grid=(n // tm,),
        in_specs=[y_spec] * 3 + [pl.BlockSpec(w_branch.shape, lambda i: (0, 0, 0))] + g_specs + [row, pl.BlockSpec(memory_space=pl.ANY)],
        out_specs=[row] * 3 + [pl.BlockSpec((tm, 3 * D_MODEL), lambda i: (i, 0))],
        out_shape=[jax.ShapeDtypeStruct((n, D_MODEL), bf16)] * 3 + [jax.ShapeDtypeStruct(du.shape, du.dtype)],
        input_output_aliases={8: 3}, compiler_params=_params())(*ys, w_branch, u, u, u, d_merged, du)
    return list(res[:3]), res[3]


def final_loss(x, g, target):
    n, d = x.shape
    r = min(256, n)

    def body(x_ref, g_ref, t_ref, loss_ref, dx_ref, dg_ref, dx16_ref):
        @pl.when(pl.program_id(0) == 0)
        def _():
            loss_ref[...] = jnp.zeros_like(loss_ref)
            dg_ref[...] = jnp.zeros_like(dg_ref)

        tgt = t_ref[...]

        def loss_fn(xv, gv):
            y = f_norm([xv], [gv])[0]
            return 0.5 * jnp.sum(jnp.mean(jnp.square(y - tgt), axis=-1, keepdims=True), axis=0, keepdims=True)

        val, vjp = jax.vjp(loss_fn, x_ref[...], g_ref[...])
        dx, dg = vjp(jnp.ones_like(val))
        loss_ref[...] += val
        dx_ref[...] = dx
        dx16_ref[...] = dx.astype(dx16_ref.dtype)
        dg_ref[...] += dg

    row = pl.BlockSpec((r, d), lambda i: (i, 0))
    return pl.pallas_call(
        body, name="final_loss", grid=(n // r,), in_specs=[row, pl.BlockSpec((1, d), lambda i: (0, 0)), row],
        out_specs=[pl.BlockSpec((8, LANE), lambda i: (0, 0)), row, pl.BlockSpec((1, d), lambda i: (0, 0)), row],
        out_shape=[jax.ShapeDtypeStruct((8, LANE), f32), jax.ShapeDtypeStruct((n, d), f32), jax.ShapeDtypeStruct((1, d), f32),
                   jax.ShapeDtypeStruct((n, d), bf16)],
        compiler_params=_params())(x, g, target)


_HBM = pl.BlockSpec(memory_space=pltpu.HBM)
_SEM = pl.BlockSpec(memory_space=pltpu.SEMAPHORE)
_EFFECT = pltpu.SideEffectType.DATAFLOW_SIDE_EFFECTING


def _peer(k):
    mx, my, mc = lax.axis_index("x"), lax.axis_index("y"), lax.axis_index("c")
    px, py, pc = (mx + (k >> 2)) % 2, (my + ((k >> 1) & 1)) % 2, (mc + (k & 1)) % 2
    return (px, py, pc), 4 * px + 2 * py + pc


def _peer_copy(k, i, x_ref, land_ref, send_sems, recv_sems, scatter):
    me = 4 * lax.axis_index("x") + 2 * lax.axis_index("y") + lax.axis_index("c")
    dev, slot = _peer(k)
    sem = i * (N_DEV - 1) + k - 1
    return pltpu.make_async_remote_copy(
        src_ref=x_ref.at[slot] if scatter else x_ref, dst_ref=land_ref.at[me], send_sem=send_sems.at[sem],
        recv_sem=recv_sems.at[sem], device_id=dev, device_id_type=pl.DeviceIdType.MESH)


def _own_copy(i, x_ref, land_ref, own_sems, scatter):
    me = 4 * lax.axis_index("x") + 2 * lax.axis_index("y") + lax.axis_index("c")
    return pltpu.make_async_copy(x_ref.at[me] if scatter else x_ref, land_ref.at[me], own_sems.at[i])


def exchange_start(xs, scatters, name):
    nx = len(xs)
    lands = [lax.empty((N_DEV,) + tuple(x.shape[1:] if sc else x.shape), x.dtype) for x, sc in zip(xs, scatters)]
    nsem = nx * (N_DEV - 1)

    def body(*refs):
        x_refs, land_refs = refs[:nx], refs[nx:2 * nx]
        send_sems, recv_sems, own_sems = refs[2 * nx:2 * nx + 3]
        token = refs[-1]
        for i in range(nx):
            for k in range(1, N_DEV):
                _peer_copy(k, i, x_refs[i], land_refs[i], send_sems, recv_sems, scatters[i]).start()
            _own_copy(i, x_refs[i], land_refs[i], own_sems, scatters[i]).start()
        token[...] = jnp.zeros_like(token)

    hbm = lambda a: pltpu.HBM(a.shape, a.dtype)
    res = pl.pallas_call(
        body, name=name, in_specs=(_HBM,) * (2 * nx),
        out_specs=(_SEM, _SEM, _SEM) + (_HBM,) * (2 * nx) + (pl.BlockSpec(memory_space=pltpu.VMEM),),
        input_output_aliases={i: 3 + i for i in range(2 * nx)},
        out_shape=(pltpu.SemaphoreType.DMA((nsem,)), pltpu.SemaphoreType.DMA((nsem,)), pltpu.SemaphoreType.DMA((nx,)),
                   *[hbm(a) for a in xs], *[hbm(a) for a in lands], jax.ShapeDtypeStruct((8, LANE), f32)),
        compiler_params=pltpu.CompilerParams(has_side_effects=_EFFECT),
    )(*[pltpu.with_memory_space_constraint(a, pltpu.HBM) for a in list(xs) + lands])
    return (res[0], res[1], res[2], list(res[3:3 + nx]), list(res[3 + nx:3 + 2 * nx]), tuple(scatters)), res[-1]


def exchange_wait(started, after, name):
    send_sems, recv_sems, own_sems, x_thrus, land_thrus, scatters = started
    nx = len(x_thrus)

    def body(*refs):
        x_refs, land_refs = refs[:nx], refs[nx:2 * nx]
        send_sems, recv_sems, own_sems = refs[2 * nx:2 * nx + 3]
        for i in range(nx):
            for k in range(1, N_DEV):
                cp = _peer_copy(k, i, x_refs[i], land_refs[i], send_sems, recv_sems, scatters[i])
                cp.wait_send()
                cp.wait_recv()
            _own_copy(i, x_refs[i], land_refs[i], own_sems, scatters[i]).wait()

    hbm = lambda a: pltpu.HBM(a.shape, a.dtype)
    res = pl.pallas_call(
        body, name=name, in_specs=(_HBM,) * (2 * nx) + (_SEM, _SEM, _SEM, pl.BlockSpec(memory_space=pl.ANY)),
        out_specs=(_HBM,) * (2 * nx), input_output_aliases={i: i for i in range(2 * nx)},
        out_shape=tuple(hbm(a) for a in list(x_thrus) + list(land_thrus)),
        compiler_params=pltpu.CompilerParams(has_side_effects=_EFFECT),
    )(*x_thrus, *land_thrus, send_sems, recv_sems, own_sems, after)
    return list(res[nx:])


def sum_slots(x, name):
    _, rows_total, cols = x.shape
    row_bytes = N_DEV * ((cols + LANE - 1) // LANE) * LANE * x.dtype.itemsize
    r = _pick_rows(rows_total, max(16, (4 * 1024 * 1024) // row_bytes // 16 * 16))

    def body(x_ref, o_ref):
        acc = x_ref[0].astype(f32)
        for s in range(1, N_DEV):
            acc = acc + x_ref[s].astype(f32)
        o_ref[...] = acc

    return pl.pallas_call(body, name=name, grid=(rows_total // r,),
                          in_specs=[pl.BlockSpec((N_DEV, r, cols), lambda i: (0, i, 0))],
                          out_specs=pl.BlockSpec((r, cols), lambda i: (i, 0)),
                          out_shape=jax.ShapeDtypeStruct((rows_total, cols), f32), compiler_params=_params())(x)


def _pick_rows(total, pref):
    best = None
    for d in range(16, min(total, pref) + 1, 16):
        if total % d == 0:
            best = d
    return best if best is not None else total


def adamw(w, g, m, v, name):
    shape = w.shape
    view = (1,) * (3 - w.ndim) + shape if w.ndim < 3 else (math.prod(shape[:-2]),) + shape[-2:]
    w2, g2, m2, v2 = (t.reshape(view) for t in (w, g, m, v))
    lead, rows_total, cols = view
    r = _pick_rows(rows_total, max(16, (512 * 1024) // max(cols, 1) // 16 * 16))
    c1, c2 = 1.0 / (1.0 - ADAM_B1 ** ADAM_STEP), 1.0 / (1.0 - ADAM_B2 ** ADAM_STEP)

    def body(w_ref, g_ref, m_ref, v_ref, d_ref, nm_ref, nv_ref):
        gv = g_ref[...]
        nm = ADAM_B1 * m_ref[...] + (1.0 - ADAM_B1) * gv
        nv = ADAM_B2 * v_ref[...] + (1.0 - ADAM_B2) * jnp.square(gv)
        d_ref[...] = -ADAM_LR * ((nm * c1) / (jnp.sqrt(nv * c2) + ADAM_EPS) + ADAM_WD * w_ref[...])
        nm_ref[...] = nm
        nv_ref[...] = nv

    spec = pl.BlockSpec((None, r, cols), lambda l, i: (l, i, 0))
    outs = pl.pallas_call(body, name=name, grid=(lead, rows_total // r), in_specs=[spec] * 4, out_specs=[spec] * 3,
                          out_shape=[jax.ShapeDtypeStruct(view, f32)] * 3, compiler_params=_params())(w2, g2, m2, v2)
    return tuple(o.reshape(shape) for o in outs)


def _const(j):
    return lambda _: j


def _layer_fwd(x, wl, fetch_rest, cos, sin, bsz, seq):
    n = x.shape[0]
    sv = {"x_in": x}
    row1 = lambda a: (a, (1, a.shape[1]), lambda j: (0, 0))
    h = rowmap(f_norm, [(x, D_MODEL, 0)], [row1(wl["attn_norm"])], [(D_MODEL, bf16)], 1, "norm_fwd")[0]
    u = mm(h, wl["w_in"], "nn", "mm_in", tn=2048)
    sv["h"], sv["u"] = h, u
    u3 = u.reshape(bsz, seq, U_PAD)
    wl = dict(wl)
    wl.update(fetch_rest(u))
    sv["wl"] = wl

    qkv = []
    for kind in range(3):
        cw = (wl["dn_conv_w"], (4, LANE), functools.partial(lambda j, kind: (0, 4 * kind + j), kind=kind))
        qkv.append(seqmap(functools.partial(f_dn_pre, kind), [(u3, U_QKV // LANE + 4 * kind)], [cw], 1, 4, "dn_pre%d" % kind)[0])
    gb = rowmap(f_dn_gates, [(u, 512, U_AB // 512)], [(wl["dn_gate_p"], (8, LANE), lambda j: (0, 0))], [(LANE, f32)], 1,
                "dn_gates", rows=512)[0]
    gb3 = gb.reshape(bsz, seq, LANE)
    crow = CHUNK * CHUNKS_PER_STEP
    dn_in = [(t.reshape(n, 512), 512, 0) for t in qkv] + [(gb, LANE, 0)]
    prep_a = rowmap(dn_prep, dn_in, [], [(512, f32)] + [(512, bf16)] * 4 + [(LANE, f32), (512, f32)], 1, "dn_prep", rows=crow)
    dn_in = dn_in + [(prep_a[6], 512, 0)]
    prep_a = [t.reshape(bsz, seq, t.shape[1]) for t in prep_a[:6]]
    o_a, ck_a = chunk_scan(dn_step, prep_a, (DN_HEADS, DN_DK, DN_DK), 512, "dn_scan")
    y_a = rowmap(per_head(f_dn_post), [(o_a.reshape(n, 512), 512, 0), (u, 512, U_Z // 512)],
                 [(wl["dn_norm_w"], (1, LANE), lambda j: (0, 0))], [(512, bf16)], 1, "dn_post")[0]
    sv.update(dn_in=dn_in, prep_a=prep_a, o_a=o_a, ck_a=ck_a, y_a=y_a)

    q_b, k_b = rowmap(f_ret_pre, [(u, 256, U_RQ // 256), (u, 256, U_RK // 256), (cos, 256, 0), (sin, 256, 0)], [],
                      [(256, f32), (256, f32)], 1, "ret_pre")
    q_b3, k_b3 = q_b.reshape(bsz, seq, 256), k_b.reshape(bsz, seq, 256)
    v_b3 = lax.slice_in_dim(u3, U_RV, U_RV + 512, axis=2)
    ret_in = [(q_b, 256, 0), (k_b, 256, 0), (u, 512, U_RV // 512)]
    inner = rowmap(ret_prep, ret_in, [], [(512, f32)], 1, "ret_prep", rows=crow)[0]
    ret_seq = [q_b3, k_b3, v_b3, inner.reshape(bsz, seq, 512)]
    o_b, ck_b = chunk_scan(ret_step, ret_seq, (RET_HEADS, 256, RET_DV), 512, "ret_scan")
    y_b = rowmap(per_head(f_ret_post), [(o_b.reshape(n, 512), 512, 0), (u, 512, U_RG // 512)], [], [(512, bf16)], 1,
                 "ret_post")[0]
    sv.update(ret_in=ret_in, ret_seq=ret_seq, o_b=o_b, ck_b=ck_b, y_b=y_b)

    lru_params = _lru_params(wl)
    a_c, b_c = seqmap(f_lru_pre, [(u3, U_CX // LANE)], lru_params, 2, 4, "lru_pre")
    h_c, hp_c = lru_scan(a_c, b_c)
    y_c = rowmap(f_lru_post, [(h_c.reshape(n, 512), 512, 0), (u, 512, U_CG // 512)], [], [(512, bf16)], 1, "lru_post")[0]
    sv.update(a_c=a_c, hp_c=hp_c, h_c=h_c, y_c=y_c)

    merged = branch_merge((y_a, y_b, y_c), wl["w_branch"], u)
    x_mid = mm(merged, wl["w_out"], "nn", "mm_out", add=x)
    sv.update(merged=merged, x_mid=x_mid)

    h2 = rowmap(f_norm, [(x_mid, D_MODEL, 0)], [row1(wl["ffn_norm"])], [(D_MODEL, bf16)], 1, "norm_fwd")[0]
    up = mm(h2, wl["w_up"], "nn", "mm_up", tn=2816)
    act = seqmap(f_ffn_mid, [(up.reshape(bsz, seq, 2 * D_FF), 0), (up.reshape(bsz, seq, 2 * D_FF), D_FF // LANE)],
                 _ffn_params(wl), 1, D_FF // LANE, "ffn_mid", out_dtype=bf16)[0]
    act = act.reshape(n, D_FF)
    x_out = mm(act, wl["w_down"], "nn", "mm_down", add=x_mid)
    sv.update(h2=h2, up=up, act=act)
    return x_out, sv


def _lru_params(wl):
    col = lambda a: (a, (a.shape[0], LANE), lambda j: (0, j))
    blk = lambda a: (a, (None, LANE, LANE), lambda j: (j, 0, 0))
    return [col(wl["lru_conv_w"]), col(wl["lru_conv_b"]), blk(wl["lru_wa"]), col(wl["lru_ba"]), blk(wl["lru_wx"]),
            col(wl["lru_bx"]), col(wl["lru_lambda"])]


def _ffn_params(wl):
    nb = D_FF // LANE
    return [(wl["ffn_conv_w"], (3, LANE), lambda j: (0, j)), (wl["ffn_conv_w"], (3, LANE), lambda j: (0, nb + j)),
            (wl["ffn_conv_b"], (1, LANE), lambda j: (0, j)), (wl["ffn_conv_b"], (1, LANE), lambda j: (0, nb + j))]


def _layer_bwd(dx, dx16, sv, cos, sin, bsz, seq, emit, dep):
    n = dx.shape[0]
    gr = {}
    wl = sv["wl"]
    u, x_in, x_mid = sv["u"], sv["x_in"], sv["x_mid"]
    u3 = u.reshape(bsz, seq, U_PAD)
    row1 = lambda a: (a, (1, a.shape[1]), lambda j: (0, 0))

    d_act = mm(dx16, wl["w_down"], "nt", "mm_down_dx", dep=dep)
    gr["w_down"] = mm(sv["act"], dx16, "tn", "mm_down_dw")
    up3 = sv["up"].reshape(bsz, seq, 2 * D_FF)
    (d_gate, d_val), dps = seqmap_bwd(f_ffn_mid, [(up3, 0), (up3, D_FF // LANE)], _ffn_params(wl),
                                      [d_act.reshape(bsz, seq, D_FF)], D_FF // LANE, "ffn_mid_bwd", din_dtype=bf16)
    gr["ffn_conv_w"] = jnp.concatenate([_cols(dps[0]), _cols(dps[1])], axis=1)
    gr["ffn_conv_b"] = jnp.concatenate([_cols(dps[2]), _cols(dps[3])], axis=1)[0]
    d_gate, d_val = d_gate.reshape(n, D_FF), d_val.reshape(n, D_FF)
    gr["w_up"] = (mm(sv["h2"], d_gate, "tn", "mm_up_dw"), mm(sv["h2"], d_val, "tn", "mm_up_dw"))
    token = emit("ffn", {k: gr[k] for k in ("w_up", "w_down")})
    d_h2 = mm(d_gate, wl["w_up"], "nt", "mm_up_dx", dep=token)
    d_h2 = mm(d_val, wl["w_up"], "nt", "mm_up_dx", add=d_h2, b_koff=1)
    (dx_mid,), (dg,), ex = rowmap_bwd(f_norm, [(x_mid, D_MODEL, 0)], [row1(wl["ffn_norm"])], [d_h2], 1, "norm_bwd", add=[dx],
                                      copy16=0)
    dx_mid16 = ex["copy16"]
    gr["ffn_norm"] = dg[0, 0]

    du = lax.empty((n, U_PAD), bf16)
    du3 = lambda: du.reshape(bsz, seq, U_PAD)

    d_merged = mm(dx_mid16, wl["w_out"], "nt", "mm_out_dx")
    gr["w_out"] = mm(sv["merged"], dx_mid16, "tn", "mm_out_dw")
    ys = (sv["y_a"], sv["y_b"], sv["y_c"])
    d_br, du = branch_merge_bwd(ys, wl["w_branch"], u, d_merged, du)
    d_ys = [mm(d_br[i], wl["w_branch"][i], "nt", "mm_branch_dx") for i in range(3)]
    gr["w_branch"] = jnp.stack([mm(ys[i], d_br[i], "tn", "mm_branch_dw") for i in range(3)])

    (d_hc, _), _, ex = rowmap_bwd(f_lru_post, [(sv["h_c"].reshape(n, 512), 512, 0), (u, 512, U_CG // 512)], [], [d_ys[2]], 1,
                                  "lru_post_bwd", into=(du, U_CG // 512, [1]))
    du = ex["into"]
    d_a, d_b = lru_scan_bwd(sv["a_c"], sv["hp_c"], d_hc.reshape(bsz, seq, 512))
    (du_new,), dps = seqmap_bwd(f_lru_pre, [(u3, U_CX // LANE)], _lru_params(wl), [d_a, d_b], 4, "lru_pre_bwd", din_dtype=bf16,
                                into=(du3(), U_CX // LANE))
    du = du_new.reshape(n, U_PAD)
    gr["lru_conv_w"], gr["lru_conv_b"] = _cols(dps[0]), _cols(dps[1])[0]
    gr["lru_wa"], gr["lru_ba"], gr["lru_wx"], gr["lru_bx"] = dps[2], dps[3][:, 0], dps[4], dps[5][:, 0]
    gr["lru_lambda"] = _cols(dps[6])[0]

    (d_ob, _), _, ex = rowmap_bwd(per_head(f_ret_post), [(sv["o_b"].reshape(n, 512), 512, 0), (u, 512, U_RG // 512)], [],
                                  [d_ys[1]], 1, "ret_post_bwd", into=(du, U_RG // 512, [1]))
    du = ex["into"]
    crow = CHUNK * CHUNKS_PER_STEP
    d_ret = chunk_scan_bwd(ret_step, sv["ret_seq"], sv["ck_b"], d_ob.reshape(bsz, seq, 512), "ret_scan_bwd")
    d_ret = [t.reshape(n, t.shape[2]) for t in d_ret]
    (d_qb, d_kb, _), _, ex = rowmap_bwd(ret_prep, sv["ret_in"], [], [d_ret[3]], 1, "ret_prep_bwd", rows=crow, add=d_ret[:3],
                                        into=(du, U_RV // 512, [2]))
    du = ex["into"]
    _, _, ex = rowmap_bwd(f_ret_pre, [(u, 256, U_RQ // 256), (u, 256, U_RK // 256), (cos, 256, 0), (sin, 256, 0)], [],
                          [d_qb, d_kb], 1, "ret_pre_bwd", din_dtypes=[f32, f32, None, None], into=(du, U_RQ // 512, [0, 1]))
    du = ex["into"]

    (d_oa, _), (dnw,), ex = rowmap_bwd(per_head(f_dn_post), [(sv["o_a"].reshape(n, 512), 512, 0), (u, 512, U_Z // 512)],
                                       [(wl["dn_norm_w"], (1, LANE), lambda j: (0, 0))], [d_ys[0]], 1, "dn_post_bwd",
                                       into=(du, U_Z // 512, [1]))
    du = ex["into"]
    gr["dn_norm_w"] = dnw[0, 0]
    d_prep = chunk_scan_bwd(dn_step, sv["prep_a"], sv["ck_a"], d_oa.reshape(bsz, seq, 512), "dn_scan_bwd")
    (d_q, d_k, d_v, d_gb, _), _, _ = rowmap_bwd(dn_prep, sv["dn_in"], [], [t.reshape(n, t.shape[2]) for t in d_prep], 1,
                                                "dn_prep_bwd", rows=crow, din_dtypes=[f32] * 4 + [None])
    d_q, d_k, d_v = (t.reshape(bsz, seq, 512) for t in (d_q, d_k, d_v))
    _, (dgp,), ex = rowmap_bwd(f_dn_gates, [(u, 512, U_AB // 512)], [(wl["dn_gate_p"], (8, LANE), lambda j: (0, 0))],
                               [d_gb], 1, "dn_gates_bwd", rows=512, into=(du, U_AB // 512, [0]))
    du = ex["into"]
    gr["dn_a_log"], gr["dn_dt_bias"] = dgp[0, 0, :DN_HEADS], dgp[0, 1, :DN_HEADS]
    d_cw = []
    for kind, d_t in enumerate((d_q, d_k, d_v)):
        cw = (wl["dn_conv_w"], (4, LANE), functools.partial(lambda j, kind: (0, 4 * kind + j), kind=kind))
        (du_new,), (dcw,) = seqmap_bwd(functools.partial(f_dn_pre, kind), [(u3, U_QKV // LANE + 4 * kind)], [cw], [d_t], 4,
                                       "dn_pre%d_bwd" % kind, din_dtype=bf16, into=(du3(), U_QKV // LANE + 4 * kind))
        du = du_new.reshape(n, U_PAD)
        d_cw.append(_cols(dcw))
    gr["dn_conv_w"] = jnp.concatenate(d_cw, axis=1)

    gr["w_in"] = _unpad_w_in(mm(sv["h"], du, "tn", "mm_in_dw", tn=2048))
    token = emit("mix", {k: gr[k] for k in ("w_in", "w_branch", "w_out")})
    d_h = mm(du, wl["w_in"], "nt", "mm_in_dx", dep=token)
    (dx_in,), (dg,), ex = rowmap_bwd(f_norm, [(x_in, D_MODEL, 0)], [row1(wl["attn_norm"])], [d_h], 1, "norm_bwd", add=[dx_mid],
                                     copy16=0)
    gr["attn_norm"] = dg[0, 0]
    big = ("w_in", "w_branch", "w_out", "w_up", "w_down")
    return dx_in, ex["copy16"], emit("small", {k: g for k, g in gr.items() if k not in big})


def _cols(dp):
    ncol, p, _ = dp.shape
    return jnp.transpose(dp, (1, 0, 2)).reshape(p, ncol * LANE)


def _pad_w_in(w):
    segs = sorted(_IN_SEGS, key=lambda s: s[2])
    parts = [lax.slice_in_dim(w, src, src + width, axis=1) for src, width, _ in segs]
    end = segs[-1][2] + segs[-1][1]
    return jnp.concatenate(parts + [jnp.zeros((w.shape[0], U_PAD - end), w.dtype)], axis=1)


def _unpad_w_in(wp):
    return jnp.concatenate([lax.slice_in_dim(wp, dst, dst + width, axis=1) for _, width, dst in _IN_SEGS], axis=1)


def _rope_tables(positions):
    half = RET_DK // 2
    inv = ROPE_BASE ** (-jnp.arange(half, dtype=f32) / half)
    ang = positions.astype(f32).reshape(-1, 1) * inv
    cos, sin = jnp.cos(ang), jnp.sin(ang)
    return jnp.tile(cos, (1, 2 * RET_HEADS)), jnp.tile(sin, (1, 2 * RET_HEADS))


def _layer_weights(lw):
    wl = {}
    wl["w_in"] = _pad_w_in(lw["w_in"])
    for k in ("dn_conv_w", "lru_conv_w", "ffn_conv_w", "lru_wa", "lru_wx"):
        wl[k] = lw[k]
    for k in ("attn_norm", "ffn_norm", "dn_norm_w", "lru_conv_b", "lru_lambda", "ffn_conv_b", "lru_ba", "lru_bx"):
        wl[k] = lw[k].reshape(1, -1)
    gp = jnp.zeros((8, LANE), f32)
    wl["dn_gate_p"] = gp.at[0, :DN_HEADS].set(lw["dn_a_log"]).at[1, :DN_HEADS].set(lw["dn_dt_bias"])
    return wl


REST = ("w_branch", "w_out", "w_up", "w_down")


def forward_backward(x, positions, target, layer_weights, final_norm, on_head, on_grads):
    bsz, seq, d = x.shape
    n = bsz * seq
    cos, sin = _rope_tables(positions)
    xs = x.reshape(n, d)
    saved = []
    for layer in range(DEPTH):
        first, fetch_rest = layer_weights(layer, xs)
        xs, sv = _layer_fwd(xs, _layer_weights(first), fetch_rest, cos, sin, bsz, seq)
        saved.append(sv)
    loss, dx, d_final, dx16 = final_loss(xs, final_norm.reshape(1, d), target.reshape(n, d))
    on_head(loss[0, 0], d_final[0])
    token = None
    for layer in reversed(range(DEPTH)):
        dx, dx16, token = _layer_bwd(dx, dx16, saved[layer], cos, sin, bsz, seq, functools.partial(on_grads, layer), token)
    return dx.reshape(bsz, seq, d)


def local_step(x, positions, target, full):
    grads, head = {layer: {} for layer in range(DEPTH)}, {}

    def layer_weights(layer, _):
        return ({k: a[layer] for k, a in full.items() if k != "final_norm" and k not in REST},
                lambda after: {k: full[k][layer] for k in REST})

    gx = forward_backward(x, positions, target, layer_weights, full["final_norm"],
                          lambda loss, d_final: head.update(loss=loss, d_final=d_final),
                          lambda layer, group, gr: grads[layer].update(
                              {k: jnp.concatenate(g, axis=1) if isinstance(g, tuple) else g for k, g in gr.items()}))
    stacked = {k: jnp.stack([grads[layer][k] for layer in range(DEPTH)]) for k in grads[0]}
    stacked["final_norm"] = head["d_final"]
    return head["loss"], gx, stacked


BIG = (("w_in", 2), ("w_branch", 3), ("w_out", 1), ("w_up", 2), ("w_down", 1))
SMALL_SHARDED = (("dn_conv_w", 2), ("lru_conv_w", 2), ("ffn_conv_w", 2))
REPLICATED = ("attn_norm", "dn_a_log", "dn_dt_bias", "dn_norm_w", "lru_conv_b", "lru_wa", "lru_ba", "lru_wx", "lru_bx",
              "lru_lambda", "ffn_norm", "ffn_conv_b", "final_norm")
WEIGHTS = ("attn_norm", "w_in", "dn_conv_w", "dn_a_log", "dn_dt_bias", "dn_norm_w", "lru_conv_w", "lru_conv_b", "lru_wa",
           "lru_ba", "lru_wx", "lru_bx", "lru_lambda", "w_branch", "w_out", "ffn_norm", "w_up", "ffn_conv_w", "ffn_conv_b",
           "w_down", "final_norm")


def _pack(arrs, dtype, align=16 * LANE):
    flat = jnp.concatenate([a.reshape(-1).astype(dtype) for a in arrs])
    pad = (-flat.shape[0]) % align
    return jnp.pad(flat, (0, pad)).reshape(-1, LANE)


def _unpack(rows, shapes):
    flat = rows.reshape(-1)
    out, pos = [], 0
    for shp in shapes:
        size = math.prod(shp)
        out.append(lax.slice_in_dim(flat, pos, pos + size).reshape(shp))
        pos += size
    return out


def kernel(x, positions, attn_norm, w_in, dn_conv_w, dn_a_log, dn_dt_bias, dn_norm_w, lru_conv_w, lru_conv_b, lru_wa, lru_ba, lru_wx, lru_bx, lru_lambda, w_branch, w_out, ffn_norm, w_up, ffn_conv_w, ffn_conv_b, w_down, final_norm, loss_target, m_attn_norm, m_w_in, m_dn_conv_w, m_dn_a_log, m_dn_dt_bias, m_dn_norm_w, m_lru_conv_w, m_lru_conv_b, m_lru_wa, m_lru_ba, m_lru_wx, m_lru_bx, m_lru_lambda, m_w_branch, m_w_out, m_ffn_norm, m_w_up, m_ffn_conv_w, m_ffn_conv_b, m_w_down, m_final_norm, v_attn_norm, v_w_in, v_dn_conv_w, v_dn_a_log, v_dn_dt_bias, v_dn_norm_w, v_lru_conv_w, v_lru_conv_b, v_lru_wa, v_lru_ba, v_lru_wx, v_lru_bx, v_lru_lambda, v_w_branch, v_w_out, v_ffn_norm, v_w_up, v_ffn_conv_w, v_ffn_conv_b, v_w_down, v_final_norm):
    w = dict(attn_norm=attn_norm, w_in=w_in, dn_conv_w=dn_conv_w, dn_a_log=dn_a_log, dn_dt_bias=dn_dt_bias, dn_norm_w=dn_norm_w,
             lru_conv_w=lru_conv_w, lru_conv_b=lru_conv_b, lru_wa=lru_wa, lru_ba=lru_ba, lru_wx=lru_wx, lru_bx=lru_bx,
             lru_lambda=lru_lambda, w_branch=w_branch, w_out=w_out, ffn_norm=ffn_norm, w_up=w_up, ffn_conv_w=ffn_conv_w,
             ffn_conv_b=ffn_conv_b, w_down=w_down, final_norm=final_norm)
    m = dict(attn_norm=m_attn_norm, w_in=m_w_in, dn_conv_w=m_dn_conv_w, dn_a_log=m_dn_a_log, dn_dt_bias=m_dn_dt_bias,
             dn_norm_w=m_dn_norm_w, lru_conv_w=m_lru_conv_w, lru_conv_b=m_lru_conv_b, lru_wa=m_lru_wa, lru_ba=m_lru_ba,
             lru_wx=m_lru_wx, lru_bx=m_lru_bx, lru_lambda=m_lru_lambda, w_branch=m_w_branch, w_out=m_w_out, ffn_norm=m_ffn_norm,
             w_up=m_w_up, ffn_conv_w=m_ffn_conv_w, ffn_conv_b=m_ffn_conv_b, w_down=m_w_down, final_norm=m_final_norm)
    v = dict(attn_norm=v_attn_norm, w_in=v_w_in, dn_conv_w=v_dn_conv_w, dn_a_log=v_dn_a_log, dn_dt_bias=v_dn_dt_bias,
             dn_norm_w=v_dn_norm_w, lru_conv_w=v_lru_conv_w, lru_conv_b=v_lru_conv_b, lru_wa=v_lru_wa, lru_ba=v_lru_ba,
             lru_wx=v_lru_wx, lru_bx=v_lru_bx, lru_lambda=v_lru_lambda, w_branch=v_w_branch, w_out=v_w_out, ffn_norm=v_ffn_norm,
             w_up=v_w_up, ffn_conv_w=v_ffn_conv_w, ffn_conv_b=v_ffn_conv_b, w_down=v_w_down, final_norm=v_final_norm)

    me = 4 * lax.axis_index("x") + 2 * lax.axis_index("y") + lax.axis_index("c")
    axes = dict(BIG + SMALL_SHARDED)
    conv_names = [k for k, _ in SMALL_SHARDED]

    gathers, tokens, conv_full = {}, [], {}
    for layer in range(DEPTH):
        first = [w["w_in"][layer].astype(bf16)] + ([w[k] for k in conv_names] if layer == 0 else [])
        rest = [w[k][layer].astype(bf16) for k in REST]
        for part, srcs in (("in", first), ("rest", rest)):
            gathers[layer, part], token = exchange_start(srcs, [False] * len(srcs), "gather_%s_start%d" % (part, layer))
            tokens.append(token[0:1, 0:1])
    all_started = functools.reduce(lambda a, b: a + b, tokens)

    def join(land, axis):
        if axis == 0:
            return land.reshape((N_DEV * land.shape[1],) + land.shape[2:])
        return jnp.concatenate([land[p] for p in range(N_DEV)], axis=axis)

    def split(g, axis):
        if isinstance(g, tuple):
            each = N_DEV // len(g)
            size = g[0].shape[axis] // each
            return jnp.stack([lax.slice_in_dim(piece, p * size, (p + 1) * size, axis=axis) for piece in g for p in range(each)])
        size = g.shape[axis] // N_DEV
        if axis == 0:
            return g.reshape((N_DEV, size) + g.shape[1:])
        return jnp.stack([lax.slice_in_dim(g, p * size, (p + 1) * size, axis=axis) for p in range(N_DEV)])

    def layer_weights(layer, x_in):
        lands = exchange_wait(gathers[layer, "in"], x_in, "gather_in_wait%d" % layer)
        lw = {"w_in": join(lands[0], 1)}
        if layer == 0:
            conv_full.update({k: join(lands[1 + i], axes[k]) for i, k in enumerate(conv_names)})
        lw.update({k: conv_full[k][layer] for k in conv_names})
        lw.update({k: w[k][layer] for k in REPLICATED if k != "final_norm"})
        if layer == 0:
            lw["attn_norm"] = lw["attn_norm"] + all_started[0]

        def fetch_rest(after):
            lands_r = exchange_wait(gathers[layer, "rest"], after, "gather_rest_wait%d" % layer)
            return {k: join(lands_r[i], axes[k] - 1) for i, k in enumerate(REST)}

        return lw, fetch_rest

    small_names = conv_names + [k for k in REPLICATED if k != "final_norm"]
    groups = {"ffn": ("w_up", "w_down"), "mix": ("w_in", "w_branch", "w_out")}
    scatters, small_shapes, head = {}, {}, {}

    def on_grads(layer, group, gr):
        if group == "small":
            small_shapes.update({k: gr[k].shape for k in small_names})
            srcs = [_pack([gr[k] for k in small_names], f32)]
            srcs += [_pack([head["loss"].reshape(1), head["d_final"]], f32)] if layer == DEPTH - 1 else []
            modes = [False] * len(srcs)
        else:
            srcs = [split(gr[k], axes[k] - 1).astype(bf16) for k in groups[group]]
            modes = [True] * len(srcs)
        scatters[layer, group], token = exchange_start(srcs, modes, "scatter_%s_start%d" % (group, layer))
        return token

    grad_x = forward_backward(x, positions, loss_target, layer_weights, final_norm,
                              lambda loss_part, d_final: head.update(loss=loss_part, d_final=d_final), on_grads)

    big_sums, small_sums = {}, {}
    for group in ("ffn", "mix"):
        for layer in reversed(range(DEPTH)):
            lands = exchange_wait(scatters[layer, group], grad_x, "scatter_%s_wait%d" % (group, layer))
            for i, k in enumerate(groups[group]):
                shard = w[k].shape[1:]
                big_sums[layer, k] = sum_slots(lands[i].reshape(N_DEV, -1, shard[-1]), "sum_" + k).reshape(shard)
    for layer in reversed(range(DEPTH)):
        lands = exchange_wait(scatters[layer, "small"], grad_x, "scatter_small_wait%d" % layer)
        small_sums[layer] = sum_slots(lands[0], "sum_small")
        if layer == DEPTH - 1:
            head_sum = _unpack(sum_slots(lands[1], "sum_head"), [(1,), final_norm.shape])
    grads = {k: jnp.stack([big_sums[layer, k] for layer in range(DEPTH)]) for k, _ in BIG}
    loss, grads["final_norm"] = head_sum[0][0], head_sum[1]
    small_flat = jnp.stack([small_sums[layer] for layer in range(DEPTH)]).reshape(DEPTH, -1)
    pos = 0
    for k in small_names:
        size = math.prod(small_shapes[k])
        g = lax.slice_in_dim(small_flat, pos, pos + size, axis=1).reshape((DEPTH,) + small_shapes[k])
        pos += size
        ax = dict(SMALL_SHARDED).get(k)
        if ax is None:
            grads[k] = g
        else:
            size = g.shape[ax] // N_DEV
            grads[k] = lax.dynamic_slice_in_dim(g, me * size, size, axis=ax)

    upd = {k: adamw(w[k], grads[k], m[k], v[k], "adamw_" + k) for k in WEIGHTS}
    return (loss, grad_x, *[grads[k] for k in WEIGHTS], *[upd[k][0] for k in WEIGHTS], *[upd[k][1] for k in WEIGHTS],
            *[upd[k][2] for k in WEIGHTS])
```

```python
import functools
import math

import jax
import jax.numpy as jnp
from jax import lax
from jax.experimental import pallas as pl
from jax.experimental.pallas import tpu as pltpu

f32 = jnp.float32
bf16 = jnp.bfloat16

D_MODEL = 1024
DEPTH = 4
CHUNK = 64
EPS = 1e-6
DN_HEADS, DN_DK = 4, 128
RET_HEADS, RET_DK, RET_DV = 4, 64, 128
ROPE_BASE = 10000.0
LRU_C = 8.0
D_FF = 2816
N_DEV = 8
LANE = 128
VMEM_LIMIT = 56 * 1024 * 1024

ADAM_LR, ADAM_B1, ADAM_B2, ADAM_EPS, ADAM_WD, ADAM_STEP = 0.001, 0.9, 0.999, 1e-8, 0.01, 10

U_GATES, U_QKV, U_RV, U_RG, U_Z, U_CX, U_CG, U_RQ, U_RK, U_AB = (
    0, 3072, 4608, 5120, 5632, 6144, 6656, 7168, 7424, 7680)
U_PAD = 8192
_IN_SEGS = ((0, 1536, U_QKV), (1536, 8, U_AB), (1544, 512, U_Z), (2056, 256, U_RQ), (2312, 256, U_RK),
            (2568, 512, U_RV), (3080, 512, U_RG), (3592, 512, U_CX), (4104, 512, U_CG), (4616, 3072, U_GATES))
N_IN = 7688


def _params():
    return pltpu.CompilerParams(vmem_limit_bytes=VMEM_LIMIT)


def _pick(dim, pref):
    best = None
    for d in range(LANE, min(dim, pref) + 1, LANE):
        if dim % d == 0:
            best = d
    return best if best is not None else dim


@functools.partial(jax.custom_vjp, nondiff_argnums=(1, 2))
def sroll(x, shift, axis):
    return pltpu.roll(x, shift, axis)


def _sroll_fwd(x, shift, axis):
    return pltpu.roll(x, shift, axis), None


def _sroll_bwd(shift, axis, _, g):
    n = g.shape[axis]
    return (pltpu.roll(g, (n - shift) % n, axis),)


sroll.defvjp(_sroll_fwd, _sroll_bwd)

_DIMS = {"nn": (((1,), (0,)), ((), ())), "nt": (((1,), (1,)), ((), ())), "tn": (((0,), (0,)), ((), ()))}


def _dg(a, b, dims):
    return lax.dot_general(a.astype(bf16), b.astype(bf16), _DIMS[dims], preferred_element_type=f32)


@functools.partial(jax.custom_vjp, nondiff_argnums=(2,))
def bdot(a, b, dims):
    return _dg(a, b, dims)


def _bdot_fwd(a, b, dims):
    return _dg(a, b, dims), (a.astype(bf16), b.astype(bf16))


def _bdot_bwd(dims, res, g):
    a, b = res
    if dims == "nn":
        return _dg(g, b, "nt"), _dg(a, g, "tn")
    if dims == "nt":
        return _dg(g, b, "nn"), _dg(g, a, "tn")
    return _dg(b, g, "nt"), _dg(a, g, "nn")


bdot.defvjp(_bdot_fwd, _bdot_bwd)


def _fdot(a, b, dims):
    return lax.dot_general(a, b, _DIMS[dims], precision=lax.Precision.HIGH, preferred_element_type=f32)


@jax.custom_vjp
def unit_lower_inv_all(mats):
    shape = mats[0].shape
    row = lax.broadcasted_iota(jnp.int32, shape, 0)
    col = lax.broadcasted_iota(jnp.int32, shape, 1)
    eye = jnp.where(row == col, 1.0, 0.0).astype(f32)
    n = [-a for a in mats]
    p = [eye + x for x in n]
    span = 2
    while span < shape[0]:
        n = [_fdot(x, x, "nn") for x in n]
        p = [y + _fdot(y, x, "nn") for y, x in zip(p, n)]
        span *= 2
    return p


def _uli_fwd(mats):
    x = unit_lower_inv_all(mats)
    return x, x


def _uli_bwd(xs, gs):
    t = [_fdot(x, g, "tn") for x, g in zip(xs, gs)]
    return ([-_fdot(y, x, "nt") for y, x in zip(t, xs)],)


unit_lower_inv_all.defvjp(_uli_fwd, _uli_bwd)


@jax.custom_vjp
def known_inverse(invs, mats):
    return invs


def _known_fwd(invs, mats):
    return invs, invs


def _known_bwd(xs, gs):
    return [jnp.zeros_like(x) for x in xs], _uli_bwd(xs, gs)[0]


known_inverse.defvjp(_known_fwd, _known_bwd)


def cumsum_rows(x):
    rows = x.shape[0]
    row = lax.broadcasted_iota(jnp.int32, x.shape, 0)
    s = 1
    while s < rows:
        x = x + jnp.where(row >= s, sroll(x, s, 0), 0.0)
        s *= 2
    return x


def _expm1(x):
    return jnp.tanh(0.5 * x) * (jnp.exp(x) + 1.0)


def _lane_pick(x, lane):
    idx = lax.broadcasted_iota(jnp.int32, x.shape, 1)
    return jnp.sum(jnp.where(idx == lane, x, 0.0), axis=1, keepdims=True)


def _row_pick(x, r):
    idx = lax.broadcasted_iota(jnp.int32, x.shape, 0)
    return jnp.sum(jnp.where(idx == r, x, 0.0), axis=0, keepdims=True)


def _causal_conv(x, halo, w, width):
    if halo is None:
        row = lax.broadcasted_iota(jnp.int32, x.shape, 0)
        acc = x * w[width - 1:width]
        for k in range(width - 1):
            shift = width - 1 - k
            acc = acc + jnp.where(row >= shift, sroll(x, shift, 0), 0.0) * w[k:k + 1]
        return acc
    xe = jnp.concatenate([halo, x], axis=0)
    acc = xe * w[width - 1:width]
    for k in range(width - 1):
        acc = acc + sroll(xe, width - 1 - k, 0) * w[k:k + 1]
    return acc[8:]


def f_norm(ins, ps):
    (x,), (g,) = ins, ps
    return [x * lax.rsqrt(jnp.mean(x * x, axis=-1, keepdims=True) + EPS) * g]


def f_dn_pre(kind, mains, halos, ps):
    y = _causal_conv(mains[0], halos[0], ps[0], 4)
    y = y * jax.nn.sigmoid(y)
    if kind < 2:
        y = y * lax.rsqrt(jnp.sum(y * y, axis=-1, keepdims=True) + EPS)
    if kind == 0:
        y = y * (DN_DK ** -0.5)
    return [y]


def f_dn_gates(ins, ps):
    u, p = ins[0][:, :LANE], ps[0]
    lane = lax.broadcasted_iota(jnp.int32, u.shape, 1)
    g = -jnp.exp(p[0:1]) * jax.nn.softplus(u + p[1:2])
    beta = jax.nn.sigmoid(u)
    return [jnp.where(lane < 4, g, jnp.where(lane < 8, beta, 0.0))]


def per_head(fn):
    def tile_fn(vals, ps):
        heads = [fn([v[:, h * LANE:(h + 1) * LANE] for v in vals], ps) for h in range(vals[0].shape[1] // LANE)]
        return [jnp.concatenate([o[i] for o in heads], axis=1) for i in range(len(heads[0]))]
    return tile_fn


def f_dn_post(ins, ps):
    (o, z), (nw,) = ins, ps
    y = o * lax.rsqrt(jnp.mean(o * o, axis=-1, keepdims=True) + EPS) * nw
    return [y * (z * jax.nn.sigmoid(z))]


def _rot_half(t):
    lane = lax.broadcasted_iota(jnp.int32, t.shape, 1)
    width = t.shape[1]
    first = (lane % RET_DK) < (RET_DK // 2)
    return jnp.where(first, -sroll(t, width - RET_DK // 2, 1), sroll(t, RET_DK // 2, 1))


def f_ret_pre(ins, ps):
    q, k, cos, sin = ins
    qr = q * cos + _rot_half(q) * sin
    kr = (k * cos + _rot_half(k) * sin) * (RET_DK ** -0.5)
    return [qr, kr]


def f_ret_post(ins, ps):
    o, g = ins
    mu = jnp.mean(o, axis=-1, keepdims=True)
    var = jnp.mean(jnp.square(o - mu), axis=-1, keepdims=True)
    return [(o - mu) * lax.rsqrt(var + EPS) * (g * jax.nn.sigmoid(g))]


def f_lru_pre(mains, halos, ps):
    cw, cb, wa, ba, wx, bx, lam = ps
    xc = _causal_conv(mains[0], halos[0], cw, 4) + cb
    r = jax.nn.sigmoid(bdot(xc, wa, "nn") + ba)
    i = jax.nn.sigmoid(bdot(xc, wx, "nn") + bx)
    log_a = -LRU_C * r * jax.nn.softplus(-lam)
    a = jnp.exp(log_a)
    b = jnp.sqrt(-_expm1(2.0 * log_a)) * (i * xc)
    return [a, b]


def f_lru_post(ins, ps):
    h, g = ins
    return [h * jax.nn.gelu(g)]


def f_ffn_mid(mains, halos, ps):
    cwg, cwv, cbg, cbv = ps
    gate = _causal_conv(mains[0], halos[0], cwg, 3) + cbg
    val = _causal_conv(mains[1], halos[1], cwv, 3) + cbv
    return [gate * jax.nn.sigmoid(gate) * val]


def mm(a, b, dims, name, add=None, dep=None, b_koff=0, tm=1536, tn=1536, tk=2816):
    if dims == "tn":
        kdim, m = a.shape
        n = b.shape[1]
    else:
        m, kdim = a.shape
        n = b.shape[0] if dims == "nt" else b.shape[1]
    tm, tn, tk = _pick(m, tm), _pick(n, tn), _pick(kdim, tk)
    nk = kdim // tk
    a_spec = pl.BlockSpec((tk, tm), lambda i, j, k: (k, i)) if dims == "tn" else pl.BlockSpec((tm, tk), lambda i, j, k: (i, k))
    b_spec = (pl.BlockSpec((tn, tk), lambda i, j, k: (j, k + b_koff * nk)) if dims == "nt"
              else pl.BlockSpec((tk, tn), lambda i, j, k: (k, j)))
    o_spec = pl.BlockSpec((tm, tn), lambda i, j, k: (i, j))
    has_add, has_dep = add is not None, dep is not None

    def body(*refs):
        a_ref, b_ref = refs[:2]
        add_ref = refs[2] if has_add else None
        o_ref = refs[2 + has_add + has_dep]
        if nk == 1:
            prod = _dg(a_ref[...], b_ref[...], dims)
            o_ref[...] = prod + add_ref[...] if has_add else prod
            return
        acc_ref = refs[-1]
        k = pl.program_id(2)

        @pl.when(k == 0)
        def _():
            acc_ref[...] = jnp.zeros_like(acc_ref)

        acc_ref[...] += _dg(a_ref[...], b_ref[...], dims)

        @pl.when(k == nk - 1)
        def _():
            o_ref[...] = acc_ref[...] + add_ref[...] if has_add else acc_ref[...]

    args = [a, b] + ([add] if has_add else []) + ([dep] if has_dep else [])
    in_specs = [a_spec, b_spec] + ([o_spec] if has_add else [])
    in_specs += [pl.BlockSpec((8, LANE), lambda i, j, k: (0, 0))] if has_dep else []
    return pl.pallas_call(
        body, name=name, grid=(m // tm, n // tn, nk), in_specs=in_specs, out_specs=o_spec,
        out_shape=jax.ShapeDtypeStruct((m, n), f32), scratch_shapes=[pltpu.VMEM((tm, tn), f32)] if nk > 1 else [],
        compiler_params=_params())(*args)


def rowmap(fn, ins, params, outs, ncol, name, rows=512):
    n = ins[0][0].shape[0]
    r = min(rows, n)
    nin, npar = len(ins), len(params)

    def body(*refs):
        vals = [x[...] for x in refs[:nin]]
        pv = [p[...] for p in refs[nin:nin + npar]]
        for o_ref, o in zip(refs[nin + npar:], fn(vals, pv)):
            o_ref[...] = o.astype(o_ref.dtype)

    in_specs = [pl.BlockSpec((r, cb), functools.partial(lambda j, i, off: (i, off + j), off=off)) for _, cb, off in ins]
    in_specs += [pl.BlockSpec(bs, functools.partial(lambda j, i, f: f(j), f=f)) for _, bs, f in params]
    out_specs = [pl.BlockSpec((r, cb), lambda j, i: (i, j)) for cb, _ in outs]
    out_shape = [jax.ShapeDtypeStruct((n, cb * ncol), dt) for cb, dt in outs]
    res = pl.pallas_call(body, name=name, grid=(ncol, n // r), in_specs=in_specs, out_specs=out_specs,
                         out_shape=out_shape, compiler_params=_params())(*[a for a, _, _ in ins], *[a for a, _, _ in params])
    return res


def rowmap_bwd(fn, ins, params, douts, ncol, name, rows=512, add=None, din_dtypes=None, into=None, copy16=None):
    n = ins[0][0].shape[0]
    r = min(rows, n)
    nin, npar, nout = len(ins), len(params), len(douts)
    add = [None] * nin if add is None else list(add)
    add_idx = [i for i in range(nin) if add[i] is not None]
    din_dtypes = [f32] * nin if din_dtypes is None else list(din_dtypes)
    into_buf, into_off, into_idx = into if into is not None else (None, 0, [])
    has_into, has_copy = into is not None, copy16 is not None
    kept = [i for i in range(nin) if din_dtypes[i] is not None and i not in into_idx]

    def body(*refs):
        vals = [x[...] for x in refs[:nin]]
        pv = [p[...] for p in refs[nin:nin + npar]]
        dys = [d[...] for d in refs[nin + npar:nin + npar + nout]]
        k0 = nin + npar + nout
        add_refs = dict(zip(add_idx, refs[k0:k0 + len(add_idx)]))
        k0 += len(add_idx) + has_into
        din_refs = refs[k0:k0 + len(kept)]
        k0 += len(kept)
        copy_ref = refs[k0] if has_copy else None
        into_ref = refs[k0 + has_copy] if has_into else None
        dp_refs = refs[k0 + has_copy + has_into:]
        _, vjp = jax.vjp(fn, vals, pv)
        dvals, dpv = vjp(dys)
        cot = lambda idx: dvals[idx] + add_refs[idx][...] if idx in add_refs else dvals[idx]
        for d_ref, idx in zip(din_refs, kept):
            d_ref[...] = cot(idx).astype(d_ref.dtype)
        if has_copy:
            copy_ref[...] = cot(copy16).astype(copy_ref.dtype)
        if has_into:
            parts = [cot(idx) for idx in into_idx]
            into_ref[...] = (parts[0] if len(parts) == 1 else jnp.concatenate(parts, axis=1)).astype(into_ref.dtype)

        @pl.when(pl.program_id(1) == 0)
        def _():
            for d_ref in dp_refs:
                d_ref[...] = jnp.zeros_like(d_ref)

        for d_ref, d in zip(dp_refs, dpv):
            d_ref[...] += d

    in_specs = [pl.BlockSpec((r, cb), functools.partial(lambda j, i, off: (i, off + j), off=off)) for _, cb, off in ins]
    in_specs += [pl.BlockSpec(bs, functools.partial(lambda j, i, f: f(j), f=f)) for _, bs, f in params]
    in_specs += [pl.BlockSpec((r, d.shape[1] // ncol), lambda j, i: (i, j)) for d in douts]
    in_specs += [pl.BlockSpec((r, ins[i][1]), lambda j, i: (i, j)) for i in add_idx]
    out_specs = [pl.BlockSpec((r, ins[i][1]), lambda j, i: (i, j)) for i in kept]
    out_shape = [jax.ShapeDtypeStruct((n, ins[i][1] * ncol), din_dtypes[i]) for i in kept]
    args = [a for a, _, _ in ins] + [a for a, _, _ in params] + list(douts) + [add[i] for i in add_idx]
    aliases = {}
    if has_copy:
        out_specs += [pl.BlockSpec((r, ins[copy16][1]), lambda j, i: (i, j))]
        out_shape += [jax.ShapeDtypeStruct((n, ins[copy16][1] * ncol), bf16)]
    if has_into:
        assert ncol == 1
        in_specs += [pl.BlockSpec(memory_space=pl.ANY)]
        aliases[len(args)] = len(out_shape)
        args += [into_buf]
        out_specs += [pl.BlockSpec((r, sum(ins[i][1] for i in into_idx)), lambda j, i: (i, into_off))]
        out_shape += [jax.ShapeDtypeStruct(into_buf.shape, into_buf.dtype)]
    pshapes = [tuple(d for d in bs if d is not None) for _, bs, _ in params]
    out_specs += [pl.BlockSpec((None,) + ps, functools.partial(lambda j, i, nd: (j,) + (0,) * nd, nd=len(ps))) for ps in pshapes]
    out_shape += [jax.ShapeDtypeStruct((ncol,) + ps, f32) for ps in pshapes]
    res = pl.pallas_call(body, name=name, grid=(ncol, n // r), in_specs=in_specs, out_specs=out_specs, out_shape=out_shape,
                         input_output_aliases=aliases, compiler_params=_params())(*args)
    dins = [None] * nin
    for pos, i in enumerate(kept):
        dins[i] = res[pos]
    pos = len(kept)
    extras = {}
    if has_copy:
        extras["copy16"] = res[pos]
        pos += 1
    if has_into:
        extras["into"] = res[pos]
        pos += 1
    return dins, res[pos:], extras


SEQ_ROWS = 2048


def seqmap(fn, ins, params, nouts, ncol, name, out_dtype=f32):
    bsz, seq, _ = ins[0][0].shape
    r = min(SEQ_ROWS, seq)
    nin, npar = len(ins), len(params)

    def body(*refs):
        in_refs = refs[:nin]
        pv = [p[...] for p in refs[nin:nin + npar]]
        out_refs = refs[nin + npar:]

        def step(i, carry):
            r0 = pl.multiple_of(i * r, r)
            h0 = pl.multiple_of(jnp.maximum(r0 - 8, 0), 8)
            mains = [x[pl.ds(r0, r), :] for x in in_refs]
            halos = [jnp.where(i == 0, 0.0, x[pl.ds(h0, 8), :]) for x in in_refs]
            for o_ref, o in zip(out_refs, fn(mains, halos, pv)):
                o_ref[pl.ds(r0, r), :] = o.astype(o_ref.dtype)
            return carry

        if r == seq:
            for o_ref, o in zip(out_refs, fn([x[...] for x in in_refs], [None] * nin, pv)):
                o_ref[...] = o.astype(o_ref.dtype)
        else:
            lax.fori_loop(0, seq // r, step, 0)

    in_specs = [pl.BlockSpec((None, seq, LANE), functools.partial(lambda j, b, off: (b, 0, off + j), off=off)) for _, off in ins]
    in_specs += [pl.BlockSpec(bs, functools.partial(lambda j, b, f: f(j), f=f)) for _, bs, f in params]
    out_specs = [pl.BlockSpec((None, seq, LANE), lambda j, b: (b, 0, j)) for _ in range(nouts)]
    out_shape = [jax.ShapeDtypeStruct((bsz, seq, LANE * ncol), out_dtype) for _ in range(nouts)]
    return pl.pallas_call(body, name=name, grid=(ncol, bsz), in_specs=in_specs, out_specs=out_specs,
                          out_shape=out_shape, compiler_params=_params())(*[a for a, _ in ins], *[a for a, _, _ in params])


def seqmap_bwd(fn, ins, params, douts, ncol, name, din_dtype=f32, into=None):
    bsz, seq, _ = ins[0][0].shape
    r = min(SEQ_ROWS, seq)
    nin, npar, nout = len(ins), len(params), len(douts)
    narrow = din_dtype != f32

    def body(*refs):
        in_refs = refs[:nin]
        pv = [p[...] for p in refs[nin:nin + npar]]
        dy_refs = refs[nin + npar:nin + npar + nout]
        k0 = nin + npar + nout + (into is not None)
        dout_refs = refs[k0:k0 + nin]
        dp_refs = refs[k0 + nin:k0 + nin + npar]
        din_refs = refs[k0 + nin + npar:] if narrow else dout_refs

        def step(i, dp_acc):
            r0 = pl.multiple_of(i * r, r)
            h0 = pl.multiple_of(jnp.maximum(r0 - 8, 0), 8)
            mains = [x[pl.ds(r0, r), :] for x in in_refs]
            halos_raw = [x[pl.ds(h0, 8), :] for x in in_refs]

            def tile(mains, halos_raw, pv):
                return fn(mains, [jnp.where(i == 0, 0.0, h) for h in halos_raw], pv)

            _, vjp = jax.vjp(tile, mains, halos_raw, pv)
            dm, dh, dp = vjp([d[pl.ds(r0, r), :] for d in dy_refs])
            for d_ref, m, h in zip(din_refs, dm, dh):
                d_ref[pl.ds(r0, r), :] = m
                d_ref[pl.ds(h0, 8), :] += h
            return [acc + d for acc, d in zip(dp_acc, dp)]

        if r == seq:
            _, vjp = jax.vjp(lambda mains, pv: fn(mains, [None] * nin, pv), [x[...] for x in in_refs], pv)
            dm, dp = vjp([d[...] for d in dy_refs])
            for o_ref, m in zip(dout_refs, dm):
                o_ref[...] = m.astype(o_ref.dtype)
        else:
            dp = lax.fori_loop(0, seq // r, step, [jnp.zeros(p.shape, f32) for p in pv])
            if narrow:
                for o_ref, d_ref in zip(dout_refs, din_refs):
                    o_ref[...] = d_ref[...].astype(o_ref.dtype)

        @pl.when(pl.program_id(1) == 0)
        def _():
            for d_ref in dp_refs:
                d_ref[...] = jnp.zeros_like(d_ref)

        for d_ref, d in zip(dp_refs, dp):
            d_ref[...] += d

    in_specs = [pl.BlockSpec((None, seq, LANE), functools.partial(lambda j, b, off: (b, 0, off + j), off=off)) for _, off in ins]
    in_specs += [pl.BlockSpec(bs, functools.partial(lambda j, b, f: f(j), f=f)) for _, bs, f in params]
    in_specs += [pl.BlockSpec((None, seq, LANE), lambda j, b: (b, 0, j)) for _ in range(nout)]
    out_specs = [pl.BlockSpec((None, seq, LANE), lambda j, b: (b, 0, j)) for _ in range(nin)]
    pshapes = [tuple(d for d in bs if d is not None) for _, bs, _ in params]
    out_specs += [pl.BlockSpec((None,) + ps, functools.partial(lambda j, b, nd: (j,) + (0,) * nd, nd=len(ps))) for ps in pshapes]
    out_shape = [jax.ShapeDtypeStruct((bsz, seq, LANE * ncol), din_dtype) for _ in range(nin)]
    out_shape += [jax.ShapeDtypeStruct((ncol,) + ps, f32) for ps in pshapes]
    args = [a for a, _ in ins] + [a for a, _, _ in params] + list(douts)
    aliases = {}
    if into is not None:
        assert nin == 1 and into[0].dtype == din_dtype
        in_specs += [pl.BlockSpec(memory_space=pl.ANY)]
        aliases[len(args)] = 0
        args += [into[0]]
        out_specs[0] = pl.BlockSpec((None, seq, LANE), lambda j, b: (b, 0, into[1] + j))
        out_shape[0] = jax.ShapeDtypeStruct(into[0].shape, din_dtype)
    res = pl.pallas_call(body, name=name, grid=(ncol, bsz), in_specs=in_specs, out_specs=out_specs, out_shape=out_shape,
                         scratch_shapes=[pltpu.VMEM((seq, LANE), f32) for _ in range(nin)] if narrow and r != seq else [],
                         input_output_aliases=aliases, compiler_params=_params())(*args)
    return res[:nin], res[nin:]


def _tri_masks():
    row = lax.broadcasted_iota(jnp.int32, (CHUNK, CHUNK), 0)
    col = lax.broadcasted_iota(jnp.int32, (CHUNK, CHUNK), 1)
    return row >= col, row > col


CHUNKS_PER_STEP = 4


def _by_rows(parts, per_row):
    rows = [jnp.concatenate(parts[i:i + per_row], axis=1) for i in range(0, len(parts), per_row)]
    return jnp.concatenate(rows, axis=0)


def dn_prep(vals, ps):
    q, k, v, gb = vals[:4]
    nchunk = q.shape[0] // CHUNK
    causal, strict = _tri_masks()
    gbs = [gb[c * CHUNK:(c + 1) * CHUNK] for c in range(nchunk)]
    gcs = [cumsum_rows(g) for g in gbs]
    gcts = [g.T for g in gcs]
    chains = [(c, h) for c in range(nchunk) for h in range(DN_HEADS)]
    part = lambda t, c, h: t[c * CHUNK:(c + 1) * CHUNK, h * DN_DK:(h + 1) * DN_DK]
    qh = [part(q, c, h) for c, h in chains]
    kh = [part(k, c, h) for c, h in chains]
    vh = [part(v, c, h) for c, h in chains]
    g_col = [_lane_pick(gcs[c], h) for c, h in chains]
    beta = [_lane_pick(gbs[c], DN_HEADS + h) for c, h in chains]
    g_row = [_row_pick(gcts[c], h)[:, :CHUNK] for c, h in chains]
    decay = [jnp.where(causal, jnp.exp(jnp.where(causal, gc - gr, 0.0)), 0.0) for gc, gr in zip(g_col, g_row)]
    k_beta = [a * b for a, b in zip(kh, beta)]
    eg = [jnp.exp(g) for g in g_col]
    kk = [bdot(a, b, "nt") for a, b in zip(k_beta, kh)]
    qk = [bdot(a, b, "nt") for a, b in zip(qh, kh)]
    lower = [jnp.where(strict, a * d, 0.0) for a, d in zip(kk, decay)]
    if len(vals) == 5:
        t_inv = known_inverse([part(vals[4], c, h)[:, :CHUNK] for c, h in chains], lower)
    else:
        t_inv = unit_lower_inv_all(lower)
    u = [bdot(t, a * b, "nn") for t, a, b in zip(t_inv, vh, beta)]
    w = [bdot(t, a * e, "nn") for t, a, e in zip(t_inv, k_beta, eg)]
    attn = [jnp.concatenate([a * d, jnp.zeros((CHUNK, DN_DK - CHUNK), f32)], axis=1) for a, d in zip(qk, decay)]
    qd = [a * e for a, e in zip(qh, eg)]
    kd = [a * jnp.exp(_row_pick(g, CHUNK - 1) - g) for a, g in zip(kh, g_col)]
    g_last = jnp.concatenate([jnp.broadcast_to(_row_pick(g, CHUNK - 1), g.shape) for g in gcs], axis=0)
    outs = [_by_rows(t, DN_HEADS) for t in (u, w, attn, qd, kd)] + [g_last]
    if len(vals) == 4:
        wide = [jnp.concatenate([t, jnp.zeros((CHUNK, DN_DK - CHUNK), f32)], axis=1) for t in t_inv]
        outs.append(_by_rows(wide, DN_HEADS))
    return outs


def dn_step(state, u, w, attn, qd, kd, g_last):
    bsz = u.shape[0]
    chains = [(b, h) for b in range(bsz) for h in range(DN_HEADS)]
    part = lambda t, b, h: t[b, :, h * DN_DK:(h + 1) * DN_DK]
    ws = [bdot(part(w, b, h), s, "nn") for (b, h), s in zip(chains, state)]
    qs = [bdot(part(qd, b, h), s, "nn") for (b, h), s in zip(chains, state)]
    v_new = [part(u, b, h) - x for (b, h), x in zip(chains, ws)]
    av = [bdot(attn[b, :, h * DN_DK:h * DN_DK + CHUNK], x, "nn") for (b, h), x in zip(chains, v_new)]
    kv = [bdot(part(kd, b, h), x, "tn") for (b, h), x in zip(chains, v_new)]
    ge = [jnp.exp(_row_pick(_lane_pick(g_last[b], h), 0)) for b, h in chains]
    new_state = [s * g + x for s, g, x in zip(state, ge, kv)]
    outs = [a + b for a, b in zip(qs, av)]
    return new_state, jnp.concatenate([jnp.concatenate(outs[b * DN_HEADS:(b + 1) * DN_HEADS], axis=1)[None]
                                       for b in range(bsz)], axis=0)


def _ret_log_gamma(h):
    return math.log(1.0 - 2.0 ** (-5.0 - h))


def ret_prep(vals, ps):
    q, k, v = vals
    nchunk = q.shape[0] // CHUNK
    causal, _ = _tri_masks()
    row = lax.broadcasted_iota(jnp.int32, (CHUNK, CHUNK), 0)
    col = lax.broadcasted_iota(jnp.int32, (CHUNK, CHUNK), 1)
    dist = (row - col).astype(f32)
    lane = lax.broadcasted_iota(jnp.int32, (CHUNK, q.shape[1]), 1)
    dmask = [jnp.where(causal, jnp.exp(jnp.where(causal, dist, 0.0) * _ret_log_gamma(h)), 0.0) for h in range(RET_HEADS)]
    chains = [(c, h) for c in range(nchunk) for h in range(RET_HEADS)]
    rows = lambda t, c: t[c * CHUNK:(c + 1) * CHUNK]
    scores = [bdot(jnp.where((lane // RET_DK) == h, rows(q, c), 0.0), rows(k, c), "nt") * dmask[h] for c, h in chains]
    inner = [bdot(s, rows(v, c)[:, h * RET_DV:(h + 1) * RET_DV], "nn") for s, (c, h) in zip(scores, chains)]
    return [_by_rows(inner, RET_HEADS)]


def ret_step(state, q, k, v, inner):
    bsz = q.shape[0]
    idx = lax.broadcasted_iota(jnp.int32, (CHUNK, 1), 0).astype(f32)
    lane = lax.broadcasted_iota(jnp.int32, (CHUNK, q.shape[2]), 1)
    chains = [(b, h) for b in range(bsz) for h in range(RET_HEADS)]
    part = lambda t, b, h: t[b, :, h * RET_DV:(h + 1) * RET_DV]
    cross = [bdot(q[b], s, "nn") for (b, h), s in zip(chains, state)]
    kz = [jnp.where((lane // RET_DK) == h, k[b], 0.0) * jnp.exp((CHUNK - 1.0 - idx) * _ret_log_gamma(h)) for b, h in chains]
    kv = [bdot(a, part(v, b, h), "tn") for a, (b, h) in zip(kz, chains)]
    outs = [x * jnp.exp((idx + 1.0) * _ret_log_gamma(h)) + part(inner, b, h) for x, (b, h) in zip(cross, chains)]
    new_state = [s * math.exp(CHUNK * _ret_log_gamma(h)) + x for s, x, (b, h) in zip(state, kv, chains)]
    return new_state, jnp.concatenate([jnp.concatenate(outs[b * RET_HEADS:(b + 1) * RET_HEADS], axis=1)[None]
                                       for b in range(bsz)], axis=0)


SCAN_CHUNKS = 8


def chunk_scan(step_fn, ins, state_shape, out_width, name):
    bsz, seq, _ = ins[0].shape
    nchunk = seq // CHUNK
    nin = len(ins)
    nh = state_shape[0]
    per = SCAN_CHUNKS if nchunk % SCAN_CHUNKS == 0 else 1

    def body(*refs):
        in_refs = refs[:nin]
        o_ref, ck_ref, s_ref = refs[nin:]

        @pl.when(pl.program_id(0) == 0)
        def _():
            s_ref[...] = jnp.zeros_like(s_ref)

        state = [s_ref[i] for i in range(bsz * nh)]
        for c in range(per):
            rows = slice(c * CHUNK, (c + 1) * CHUNK)
            for i in range(bsz * nh):
                ck_ref[i // nh, c, i % nh] = state[i]
            state, out = step_fn(state, *[x[:, rows, :].astype(f32) for x in in_refs])
            o_ref[:, rows, :] = out
        for i in range(bsz * nh):
            s_ref[i] = state[i]

    in_specs = [pl.BlockSpec((bsz, per * CHUNK, x.shape[2]), lambda n: (0, n, 0)) for x in ins]
    out_specs = [pl.BlockSpec((bsz, per * CHUNK, out_width), lambda n: (0, n, 0)),
                 pl.BlockSpec((bsz, per) + tuple(state_shape), lambda n: (0, n, 0, 0, 0))]
    out_shape = [jax.ShapeDtypeStruct((bsz, seq, out_width), f32),
                 jax.ShapeDtypeStruct((bsz, nchunk) + tuple(state_shape), f32)]
    return pl.pallas_call(body, name=name, grid=(nchunk // per,), in_specs=in_specs, out_specs=out_specs, out_shape=out_shape,
                          scratch_shapes=[pltpu.VMEM((bsz * nh,) + tuple(state_shape[1:]), f32)],
                          compiler_params=_params())(*ins)


def chunk_scan_bwd(step_fn, ins, ckpt, dout, name):
    bsz, seq, _ = ins[0].shape
    nchunk = seq // CHUNK
    nin = len(ins)
    state_shape = ckpt.shape[2:]
    nh = state_shape[0]
    per = SCAN_CHUNKS if nchunk % SCAN_CHUNKS == 0 else 1
    nstep = nchunk // per

    def body(*refs):
        in_refs = refs[:nin]
        ck_ref, do_ref = refs[nin:nin + 2]
        din_refs = refs[nin + 2:nin + 2 + nin]
        ds_ref = refs[-1]

        @pl.when(pl.program_id(0) == 0)
        def _():
            ds_ref[...] = jnp.zeros_like(ds_ref)

        dstate = [ds_ref[i] for i in range(bsz * nh)]
        for c in reversed(range(per)):
            rows = slice(c * CHUNK, (c + 1) * CHUNK)
            state = [ck_ref[i // nh, c, i % nh] for i in range(bsz * nh)]
            _, vjp = jax.vjp(step_fn, state, *[x[:, rows, :].astype(f32) for x in in_refs])
            grads = vjp((dstate, do_ref[:, rows, :]))
            dstate = grads[0]
            for d_ref, d in zip(din_refs, grads[1:]):
                d_ref[:, rows, :] = d
        for i in range(bsz * nh):
            ds_ref[i] = dstate[i]

    rev = lambda n: (0, nstep - 1 - n, 0)
    in_specs = [pl.BlockSpec((bsz, per * CHUNK, x.shape[2]), rev) for x in ins]
    in_specs += [pl.BlockSpec((bsz, per) + tuple(state_shape), lambda n: (0, nstep - 1 - n, 0, 0, 0)),
                 pl.BlockSpec((bsz, per * CHUNK, dout.shape[2]), rev)]
    out_specs = [pl.BlockSpec((bsz, per * CHUNK, x.shape[2]), rev) for x in ins]
    out_shape = [jax.ShapeDtypeStruct(x.shape, f32) for x in ins]
    return pl.pallas_call(body, name=name, grid=(nstep,), in_specs=in_specs, out_specs=out_specs, out_shape=out_shape,
                          scratch_shapes=[pltpu.VMEM((bsz * nh,) + tuple(state_shape[1:]), f32)],
                          compiler_params=_params())(*ins, ckpt, dout)


LRU_ROWS = 512


def lru_scan(a, b):
    bsz, seq, width = a.shape
    rb = min(LRU_ROWS, seq)
    seqs = range(bsz)

    def body(a_ref, b_ref, h_ref, hp_ref, carry_ref):
        @pl.when(pl.program_id(0) == 0)
        def _():
            carry_ref[...] = jnp.zeros_like(carry_ref)

        row = lax.broadcasted_iota(jnp.int32, (8, width), 0)

        def tile(t, hprev):
            r0 = pl.multiple_of(t * 8, 8)
            ca = [a_ref[i, pl.ds(r0, 8), :] for i in seqs]
            cb = [b_ref[i, pl.ds(r0, 8), :] for i in seqs]
            for s in (1, 2, 4):
                m = row >= s
                cb = [jnp.where(m, x * pltpu.roll(y, s, 0) + y, y) for x, y in zip(ca, cb)]
                ca = [jnp.where(m, x * pltpu.roll(x, s, 0), x) for x in ca]
            h = [y + x * p for x, y, p in zip(ca, cb, hprev)]
            for i in seqs:
                h_ref[i, pl.ds(r0, 8), :] = h[i]
                hp_ref[i, pl.ds(r0, 8), :] = jnp.where(row == 0, hprev[i], pltpu.roll(h[i], 1, 0))
            return tuple(_row_pick(x, 7) for x in h)

        last = lax.fori_loop(0, rb // 8, tile, tuple(carry_ref[i:i + 1, :] for i in seqs))
        for i in seqs:
            carry_ref[i:i + 1, :] = last[i]

    spec = pl.BlockSpec((bsz, rb, width), lambda i: (0, i, 0))
    return pl.pallas_call(body, name="lru_scan", grid=(seq // rb,), in_specs=[spec, spec], out_specs=[spec, spec],
                          out_shape=[jax.ShapeDtypeStruct(a.shape, f32)] * 2,
                          scratch_shapes=[pltpu.VMEM((max(8, bsz), width), f32)], compiler_params=_params())(a, b)


def lru_scan_bwd(a, hp, dh):
    bsz, seq, width = a.shape
    rb = min(LRU_ROWS, seq)
    nblk = seq // rb
    seqs = range(bsz)

    def body(a_ref, hp_ref, dh_ref, da_ref, db_ref, carry_ref):
        @pl.when(pl.program_id(0) == 0)
        def _():
            carry_ref[...] = jnp.zeros_like(carry_ref)

        row = lax.broadcasted_iota(jnp.int32, (8, width), 0)
        ntile = rb // 8

        def tile(t, mu_next):
            r0 = pl.multiple_of((ntile - 1 - t) * 8, 8)
            ca = [a_ref[i, pl.ds(r0, 8), :] for i in seqs]
            dh_t = [dh_ref[i, pl.ds(r0, 8), :] for i in seqs]
            cb = [x * y for x, y in zip(ca, dh_t)]
            for s in (1, 2, 4):
                m = row < 8 - s
                cb = [jnp.where(m, x * pltpu.roll(y, 8 - s, 0) + y, y) for x, y in zip(ca, cb)]
                ca = [jnp.where(m, x * pltpu.roll(x, 8 - s, 0), x) for x in ca]
            mu = [y + x * p for x, y, p in zip(ca, cb, mu_next)]
            for i in seqs:
                lam = dh_t[i] + jnp.where(row == 7, mu_next[i], pltpu.roll(mu[i], 7, 0))
                db_ref[i, pl.ds(r0, 8), :] = lam
                da_ref[i, pl.ds(r0, 8), :] = lam * hp_ref[i, pl.ds(r0, 8), :]
            return tuple(_row_pick(x, 0) for x in mu)

        last = lax.fori_loop(0, ntile, tile, tuple(carry_ref[i:i + 1, :] for i in seqs))
        for i in seqs:
            carry_ref[i:i + 1, :] = last[i]

    spec = pl.BlockSpec((bsz, rb, width), lambda i: (0, nblk - 1 - i, 0))
    return pl.pallas_call(body, name="lru_scan_bwd", grid=(nblk,), in_specs=[spec] * 3, out_specs=[spec, spec],
                          out_shape=[jax.ShapeDtypeStruct(a.shape, f32)] * 2,
                          scratch_shapes=[pltpu.VMEM((max(8, bsz), width), f32)], compiler_params=_params())(a, hp, dh)


MERGE_ROWS = 512


def branch_merge(ys, w_branch, u):
    n = ys[0].shape[0]
    tm = min(MERGE_ROWS, n)

    def body(ya, yb, yc, w_ref, g0, g1, g2, o_ref):
        acc = None
        for i, (y_ref, g_ref) in enumerate(((ya, g0), (yb, g1), (yc, g2))):
            term = jax.nn.sigmoid(g_ref[...]) * _dg(y_ref[...], w_ref[i], "nn")
            acc = term if acc is None else acc + term
        o_ref[...] = acc.astype(o_ref.dtype)

    y_spec = pl.BlockSpec((tm, ys[0].shape[1]), lambda i: (i, 0))
    g_specs = [pl.BlockSpec((tm, D_MODEL), functools.partial(lambda i, k: (i, k), k=k)) for k in range(3)]
    return pl.pallas_call(
        body, name="branch_merge", grid=(n // tm,),
        in_specs=[y_spec] * 3 + [pl.BlockSpec(w_branch.shape, lambda i: (0, 0, 0))] + g_specs,
        out_specs=pl.BlockSpec((tm, D_MODEL), lambda i: (i, 0)), out_shape=jax.ShapeDtypeStruct((n, D_MODEL), bf16),
        compiler_params=_params())(*ys, w_branch, u, u, u)


def branch_merge_bwd(ys, w_branch, u, d_merged, du):
    n = ys[0].shape[0]
    tm = min(MERGE_ROWS, n)

    def body(ya, yb, yc, w_ref, g0, g1, g2, dm_ref, du_in, db0, db1, db2, du_ref):
        dm = dm_ref[...]
        d_gates = []
        for i, (y_ref, g_ref, db_ref) in enumerate(((ya, g0, db0), (yb, g1, db1), (yc, g2, db2))):
            s = jax.nn.sigmoid(g_ref[...])
            db_ref[...] = (dm * s).astype(db_ref.dtype)
            d_gates.append(dm * _dg(y_ref[...], w_ref[i], "nn") * (s * (1.0 - s)))
        du_ref[...] = jnp.concatenate(d_gates, axis=1).astype(du_ref.dtype)

    y_spec = pl.BlockSpec((tm, ys[0].shape[1]), lambda i: (i, 0))
    row = pl.BlockSpec((tm, D_MODEL), lambda i: (i, 0))
    g_specs = [pl.BlockSpec((tm, D_MODEL), functools.partial(lambda i, k: (i, k), k=k)) for k in range(3)]
    res = pl.pallas_call(
        body, name="branch_merge_bwd", grid=(n // tm,),
        in_specs=[y_spec] * 3 + [pl.BlockSpec(w_branch.shape, lambda i: (0, 0, 0))] + g_specs + [row, pl.BlockSpec(memory_space=pl.ANY)],
        out_specs=[row] * 3 + [pl.BlockSpec((tm, 3 * D_MODEL), lambda i: (i, 0))],
        out_shape=[jax.ShapeDtypeStruct((n, D_MODEL), bf16)] * 3 + [jax.ShapeDtypeStruct(du.shape, du.dtype)],
        input_output_aliases={8: 3}, compiler_params=_params())(*ys, w_branch, u, u, u, d_merged, du)
    return list(res[:3]), res[3]


def final_loss(x, g, target):
    n, d = x.shape
    r = min(256, n)

    def body(x_ref, g_ref, t_ref, loss_ref, dx_ref, dg_ref, dx16_ref):
        @pl.when(pl.program_id(0) == 0)
        def _():
            loss_ref[...] = jnp.zeros_like(loss_ref)
            dg_ref[...] = jnp.zeros_like(dg_ref)

        tgt = t_ref[...]

        def loss_fn(xv, gv):
            y = f_norm([xv], [gv])[0]
            return 0.5 * jnp.sum(jnp.mean(jnp.square(y - tgt), axis=-1, keepdims=True), axis=0, keepdims=True)

        val, vjp = jax.vjp(loss_fn, x_ref[...], g_ref[...])
        dx, dg = vjp(jnp.ones_like(val))
        loss_ref[...] += val
        dx_ref[...] = dx
        dx16_ref[...] = dx.astype(dx16_ref.dtype)
        dg_ref[...] += dg

    row = pl.BlockSpec((r, d), lambda i: (i, 0))
    return pl.pallas_call(
        body, name="final_loss", grid=(n // r,), in_specs=[row, pl.BlockSpec((1, d), lambda i: (0, 0)), row],
        out_specs=[pl.BlockSpec((8, LANE), lambda i: (0, 0)), row, pl.BlockSpec((1, d), lambda i: (0, 0)), row],
        out_shape=[jax.ShapeDtypeStruct((8, LANE), f32), jax.ShapeDtypeStruct((n, d), f32), jax.ShapeDtypeStruct((1, d), f32),
                   jax.ShapeDtypeStruct((n, d), bf16)],
        compiler_params=_params())(x, g, target)


_HBM = pl.BlockSpec(memory_space=pltpu.HBM)
_SEM = pl.BlockSpec(memory_space=pltpu.SEMAPHORE)
_EFFECT = pltpu.SideEffectType.DATAFLOW_SIDE_EFFECTING


def _peer(k):
    mx, my, mc = lax.axis_index("x"), lax.axis_index("y"), lax.axis_index("c")
    px, py, pc = (mx + (k >> 2)) % 2, (my + ((k >> 1) & 1)) % 2, (mc + (k & 1)) % 2
    return (px, py, pc), 4 * px + 2 * py + pc


def _peer_copy(k, i, x_ref, land_ref, send_sems, recv_sems, scatter):
    me = 4 * lax.axis_index("x") + 2 * lax.axis_index("y") + lax.axis_index("c")
    dev, slot = _peer(k)
    sem = i * (N_DEV - 1) + k - 1
    return pltpu.make_async_remote_copy(
        src_ref=x_ref.at[slot] if scatter else x_ref, dst_ref=land_ref.at[me], send_sem=send_sems.at[sem],
        recv_sem=recv_sems.at[sem], device_id=dev, device_id_type=pl.DeviceIdType.MESH)


def _own_copy(i, x_ref, land_ref, own_sems, scatter):
    me = 4 * lax.axis_index("x") + 2 * lax.axis_index("y") + lax.axis_index("c")
    return pltpu.make_async_copy(x_ref.at[me] if scatter else x_ref, land_ref.at[me], own_sems.at[i])


def exchange_start(xs, scatters, name):
    nx = len(xs)
    lands = [lax.empty((N_DEV,) + tuple(x.shape[1:] if sc else x.shape), x.dtype) for x, sc in zip(xs, scatters)]
    nsem = nx * (N_DEV - 1)

    def body(*refs):
        x_refs, land_refs = refs[:nx], refs[nx:2 * nx]
        send_sems, recv_sems, own_sems = refs[2 * nx:2 * nx + 3]
        token = refs[-1]
        for i in range(nx):
            for k in range(1, N_DEV):
                _peer_copy(k, i, x_refs[i], land_refs[i], send_sems, recv_sems, scatters[i]).start()
            _own_copy(i, x_refs[i], land_refs[i], own_sems, scatters[i]).start()
        token[...] = jnp.zeros_like(token)

    hbm = lambda a: pltpu.HBM(a.shape, a.dtype)
    res = pl.pallas_call(
        body, name=name, in_specs=(_HBM,) * (2 * nx),
        out_specs=(_SEM, _SEM, _SEM) + (_HBM,) * (2 * nx) + (pl.BlockSpec(memory_space=pltpu.VMEM),),
        input_output_aliases={i: 3 + i for i in range(2 * nx)},
        out_shape=(pltpu.SemaphoreType.DMA((nsem,)), pltpu.SemaphoreType.DMA((nsem,)), pltpu.SemaphoreType.DMA((nx,)),
                   *[hbm(a) for a in xs], *[hbm(a) for a in lands], jax.ShapeDtypeStruct((8, LANE), f32)),
        compiler_params=pltpu.CompilerParams(has_side_effects=_EFFECT),
    )(*[pltpu.with_memory_space_constraint(a, pltpu.HBM) for a in list(xs) + lands])
    return (res[0], res[1], res[2], list(res[3:3 + nx]), list(res[3 + nx:3 + 2 * nx]), tuple(scatters)), res[-1]


def exchange_wait(started, after, name):
    send_sems, recv_sems, own_sems, x_thrus, land_thrus, scatters = started
    nx = len(x_thrus)

    def body(*refs):
        x_refs, land_refs = refs[:nx], refs[nx:2 * nx]
        send_sems, recv_sems, own_sems = refs[2 * nx:2 * nx + 3]
        for i in range(nx):
            for k in range(1, N_DEV):
                cp = _peer_copy(k, i, x_refs[i], land_refs[i], send_sems, recv_sems, scatters[i])
                cp.wait_send()
                cp.wait_recv()
            _own_copy(i, x_refs[i], land_refs[i], own_sems, scatters[i]).wait()

    hbm = lambda a: pltpu.HBM(a.shape, a.dtype)
    res = pl.pallas_call(
        body, name=name, in_specs=(_HBM,) * (2 * nx) + (_SEM, _SEM, _SEM, pl.BlockSpec(memory_space=pl.ANY)),
        out_specs=(_HBM,) * (2 * nx), input_output_aliases={i: i for i in range(2 * nx)},
        out_shape=tuple(hbm(a) for a in list(x_thrus) + list(land_thrus)),
        compiler_params=pltpu.CompilerParams(has_side_effects=_EFFECT),
    )(*x_thrus, *land_thrus, send_sems, recv_sems, own_sems, after)
    return list(res[nx:])


def sum_slots(x, name):
    _, rows_total, cols = x.shape
    row_bytes = N_DEV * ((cols + LANE - 1) // LANE) * LANE * x.dtype.itemsize
    r = _pick_rows(rows_total, max(16, (4 * 1024 * 1024) // row_bytes // 16 * 16))

    def body(x_ref, o_ref):
        acc = x_ref[0].astype(f32)
        for s in range(1, N_DEV):
            acc = acc + x_ref[s].astype(f32)
        o_ref[...] = acc

    return pl.pallas_call(body, name=name, grid=(rows_total // r,),
                          in_specs=[pl.BlockSpec((N_DEV, r, cols), lambda i: (0, i, 0))],
                          out_specs=pl.BlockSpec((r, cols), lambda i: (i, 0)),
                          out_shape=jax.ShapeDtypeStruct((rows_total, cols), f32), compiler_params=_params())(x)


def _pick_rows(total, pref):
    best = None
    for d in range(16, min(total, pref) + 1, 16):
        if total % d == 0:
            best = d
    return best if best is not None else total


def adamw(w, g, m, v, name):
    shape = w.shape
    view = (1,) * (3 - w.ndim) + shape if w.ndim < 3 else (math.prod(shape[:-2]),) + shape[-2:]
    w2, g2, m2, v2 = (t.reshape(view) for t in (w, g, m, v))
    lead, rows_total, cols = view
    r = _pick_rows(rows_total, max(16, (512 * 1024) // max(cols, 1) // 16 * 16))
    c1, c2 = 1.0 / (1.0 - ADAM_B1 ** ADAM_STEP), 1.0 / (1.0 - ADAM_B2 ** ADAM_STEP)

    def body(w_ref, g_ref, m_ref, v_ref, d_ref, nm_ref, nv_ref):
        gv = g_ref[...]
        nm = ADAM_B1 * m_ref[...] + (1.0 - ADAM_B1) * gv
        nv = ADAM_B2 * v_ref[...] + (1.0 - ADAM_B2) * jnp.square(gv)
        d_ref[...] = -ADAM_LR * ((nm * c1) / (jnp.sqrt(nv * c2) + ADAM_EPS) + ADAM_WD * w_ref[...])
        nm_ref[...] = nm
        nv_ref[...] = nv

    padded = ((r + 7) // 8) * 8 * ((cols + LANE - 1) // LANE) * LANE * 4
    lb = max(d for d in range(1, lead + 1) if lead % d == 0 and d * padded <= max(padded, 1024 * 1024))
    spec = pl.BlockSpec((lb, r, cols), lambda l, i: (l, i, 0))
    outs = pl.pallas_call(body, name=name, grid=(lead // lb, rows_total // r), in_specs=[spec] * 4, out_specs=[spec] * 3,
                          out_shape=[jax.ShapeDtypeStruct(view, f32)] * 3, compiler_params=_params())(w2, g2, m2, v2)
    return tuple(o.reshape(shape) for o in outs)


def _const(j):
    return lambda _: j


def _layer_fwd(x, wl, fetch_rest, cos, sin, bsz, seq):
    n = x.shape[0]
    sv = {"x_in": x}
    row1 = lambda a: (a, (1, a.shape[1]), lambda j: (0, 0))
    h = rowmap(f_norm, [(x, D_MODEL, 0)], [row1(wl["attn_norm"])], [(D_MODEL, bf16)], 1, "norm_fwd")[0]
    u = mm(h, wl["w_in"], "nn", "mm_in", tn=2048)
    sv["h"], sv["u"] = h, u
    u3 = u.reshape(bsz, seq, U_PAD)
    wl = dict(wl)
    wl.update(fetch_rest(u))
    sv["wl"] = wl

    qkv = []
    for kind in range(3):
        cw = (wl["dn_conv_w"], (4, LANE), functools.partial(lambda j, kind: (0, 4 * kind + j), kind=kind))
        qkv.append(seqmap(functools.partial(f_dn_pre, kind), [(u3, U_QKV // LANE + 4 * kind)], [cw], 1, 4, "dn_pre%d" % kind)[0])
    gb = rowmap(f_dn_gates, [(u, 512, U_AB // 512)], [(wl["dn_gate_p"], (8, LANE), lambda j: (0, 0))], [(LANE, f32)], 1,
                "dn_gates", rows=512)[0]
    gb3 = gb.reshape(bsz, seq, LANE)
    crow = CHUNK * CHUNKS_PER_STEP
    dn_in = [(t.reshape(n, 512), 512, 0) for t in qkv] + [(gb, LANE, 0)]
    prep_a = rowmap(dn_prep, dn_in, [], [(512, f32)] + [(512, bf16)] * 4 + [(LANE, f32), (512, f32)], 1, "dn_prep", rows=crow)
    dn_in = dn_in + [(prep_a[6], 512, 0)]
    prep_a = [t.reshape(bsz, seq, t.shape[1]) for t in prep_a[:6]]
    o_a, ck_a = chunk_scan(dn_step, prep_a, (DN_HEADS, DN_DK, DN_DK), 512, "dn_scan")
    y_a = rowmap(per_head(f_dn_post), [(o_a.reshape(n, 512), 512, 0), (u, 512, U_Z // 512)],
                 [(wl["dn_norm_w"], (1, LANE), lambda j: (0, 0))], [(512, bf16)], 1, "dn_post")[0]
    sv.update(dn_in=dn_in, prep_a=prep_a, o_a=o_a, ck_a=ck_a, y_a=y_a)

    q_b, k_b = rowmap(f_ret_pre, [(u, 256, U_RQ // 256), (u, 256, U_RK // 256), (cos, 256, 0), (sin, 256, 0)], [],
                      [(256, f32), (256, f32)], 1, "ret_pre")
    q_b3, k_b3 = q_b.reshape(bsz, seq, 256), k_b.reshape(bsz, seq, 256)
    v_b3 = lax.slice_in_dim(u3, U_RV, U_RV + 512, axis=2)
    ret_in = [(q_b, 256, 0), (k_b, 256, 0), (u, 512, U_RV // 512)]
    inner = rowmap(ret_prep, ret_in, [], [(512, f32)], 1, "ret_prep", rows=crow)[0]
    ret_seq = [q_b3, k_b3, v_b3, inner.reshape(bsz, seq, 512)]
    o_b, ck_b = chunk_scan(ret_step, ret_seq, (RET_HEADS, 256, RET_DV), 512, "ret_scan")
    y_b = rowmap(per_head(f_ret_post), [(o_b.reshape(n, 512), 512, 0), (u, 512, U_RG // 512)], [], [(512, bf16)], 1,
                 "ret_post")[0]
    sv.update(ret_in=ret_in, ret_seq=ret_seq, o_b=o_b, ck_b=ck_b, y_b=y_b)

    lru_params = _lru_params(wl)
    a_c, b_c = seqmap(f_lru_pre, [(u3, U_CX // LANE)], lru_params, 2, 4, "lru_pre")
    h_c, hp_c = lru_scan(a_c, b_c)
    y_c = rowmap(f_lru_post, [(h_c.reshape(n, 512), 512, 0), (u, 512, U_CG // 512)], [], [(512, bf16)], 1, "lru_post")[0]
    sv.update(a_c=a_c, hp_c=hp_c, h_c=h_c, y_c=y_c)

    merged = branch_merge((y_a, y_b, y_c), wl["w_branch"], u)
    x_mid = mm(merged, wl["w_out"], "nn", "mm_out", add=x)
    sv.update(merged=merged, x_mid=x_mid)

    h2 = rowmap(f_norm, [(x_mid, D_MODEL, 0)], [row1(wl["ffn_norm"])], [(D_MODEL, bf16)], 1, "norm_fwd")[0]
    up = mm(h2, wl["w_up"], "nn", "mm_up", tn=2816)
    act = seqmap(f_ffn_mid, [(up.reshape(bsz, seq, 2 * D_FF), 0), (up.reshape(bsz, seq, 2 * D_FF), D_FF // LANE)],
                 _ffn_params(wl), 1, D_FF // LANE, "ffn_mid", out_dtype=bf16)[0]
    act = act.reshape(n, D_FF)
    x_out = mm(act, wl["w_down"], "nn", "mm_down", add=x_mid)
    sv.update(h2=h2, up=up, act=act)
    return x_out, sv


def _lru_params(wl):
    col = lambda a: (a, (a.shape[0], LANE), lambda j: (0, j))
    blk = lambda a: (a, (None, LANE, LANE), lambda j: (j, 0, 0))
    return [col(wl["lru_conv_w"]), col(wl["lru_conv_b"]), blk(wl["lru_wa"]), col(wl["lru_ba"]), blk(wl["lru_wx"]),
            col(wl["lru_bx"]), col(wl["lru_lambda"])]


def _ffn_params(wl):
    nb = D_FF // LANE
    return [(wl["ffn_conv_w"], (3, LANE), lambda j: (0, j)), (wl["ffn_conv_w"], (3, LANE), lambda j: (0, nb + j)),
            (wl["ffn_conv_b"], (1, LANE), lambda j: (0, j)), (wl["ffn_conv_b"], (1, LANE), lambda j: (0, nb + j))]


def _layer_bwd(dx, dx16, sv, cos, sin, bsz, seq, emit, dep):
    n = dx.shape[0]
    gr = {}
    wl = sv["wl"]
    u, x_in, x_mid = sv["u"], sv["x_in"], sv["x_mid"]
    u3 = u.reshape(bsz, seq, U_PAD)
    row1 = lambda a: (a, (1, a.shape[1]), lambda j: (0, 0))

    d_act = mm(dx16, wl["w_down"], "nt", "mm_down_dx", dep=dep)
    gr["w_down"] = mm(sv["act"], dx16, "tn", "mm_down_dw")
    up3 = sv["up"].reshape(bsz, seq, 2 * D_FF)
    (d_gate, d_val), dps = seqmap_bwd(f_ffn_mid, [(up3, 0), (up3, D_FF // LANE)], _ffn_params(wl),
                                      [d_act.reshape(bsz, seq, D_FF)], D_FF // LANE, "ffn_mid_bwd", din_dtype=bf16)
    gr["ffn_conv_w"] = jnp.concatenate([_cols(dps[0]), _cols(dps[1])], axis=1)
    gr["ffn_conv_b"] = jnp.concatenate([_cols(dps[2]), _cols(dps[3])], axis=1)[0]
    d_gate, d_val = d_gate.reshape(n, D_FF), d_val.reshape(n, D_FF)
    gr["w_up"] = (mm(sv["h2"], d_gate, "tn", "mm_up_dw"), mm(sv["h2"], d_val, "tn", "mm_up_dw"))
    token = emit("ffn", {k: gr[k] for k in ("w_up", "w_down")})
    d_h2 = mm(d_gate, wl["w_up"], "nt", "mm_up_dx", dep=token)
    d_h2 = mm(d_val, wl["w_up"], "nt", "mm_up_dx", add=d_h2, b_koff=1)
    (dx_mid,), (dg,), ex = rowmap_bwd(f_norm, [(x_mid, D_MODEL, 0)], [row1(wl["ffn_norm"])], [d_h2], 1, "norm_bwd", add=[dx],
                                      copy16=0)
    dx_mid16 = ex["copy16"]
    gr["ffn_norm"] = dg[0, 0]

    du = lax.empty((n, U_PAD), bf16)
    du3 = lambda: du.reshape(bsz, seq, U_PAD)

    d_merged = mm(dx_mid16, wl["w_out"], "nt", "mm_out_dx")
    gr["w_out"] = mm(sv["merged"], dx_mid16, "tn", "mm_out_dw")
    ys = (sv["y_a"], sv["y_b"], sv["y_c"])
    d_br, du = branch_merge_bwd(ys, wl["w_branch"], u, d_merged, du)
    d_ys = [mm(d_br[i], wl["w_branch"][i], "nt", "mm_branch_dx") for i in range(3)]
    gr["w_branch"] = jnp.stack([mm(ys[i], d_br[i], "tn", "mm_branch_dw") for i in range(3)])

    (d_hc, _), _, ex = rowmap_bwd(f_lru_post, [(sv["h_c"].reshape(n, 512), 512, 0), (u, 512, U_CG // 512)], [], [d_ys[2]], 1,
                                  "lru_post_bwd", into=(du, U_CG // 512, [1]))
    du = ex["into"]
    d_a, d_b = lru_scan_bwd(sv["a_c"], sv["hp_c"], d_hc.reshape(bsz, seq, 512))
    (du_new,), dps = seqmap_bwd(f_lru_pre, [(u3, U_CX // LANE)], _lru_params(wl), [d_a, d_b], 4, "lru_pre_bwd", din_dtype=bf16,
                                into=(du3(), U_CX // LANE))
    du = du_new.reshape(n, U_PAD)
    gr["lru_conv_w"], gr["lru_conv_b"] = _cols(dps[0]), _cols(dps[1])[0]
    gr["lru_wa"], gr["lru_ba"], gr["lru_wx"], gr["lru_bx"] = dps[2], dps[3][:, 0], dps[4], dps[5][:, 0]
    gr["lru_lambda"] = _cols(dps[6])[0]

    (d_ob, _), _, ex = rowmap_bwd(per_head(f_ret_post), [(sv["o_b"].reshape(n, 512), 512, 0), (u, 512, U_RG // 512)], [],
                                  [d_ys[1]], 1, "ret_post_bwd", into=(du, U_RG // 512, [1]))
    du = ex["into"]
    crow = CHUNK * CHUNKS_PER_STEP
    d_ret = chunk_scan_bwd(ret_step, sv["ret_seq"], sv["ck_b"], d_ob.reshape(bsz, seq, 512), "ret_scan_bwd")
    d_ret = [t.reshape(n, t.shape[2]) for t in d_ret]
    (d_qb, d_kb, _), _, ex = rowmap_bwd(ret_prep, sv["ret_in"], [], [d_ret[3]], 1, "ret_prep_bwd", rows=crow, add=d_ret[:3],
                                        into=(du, U_RV // 512, [2]))
    du = ex["into"]
    _, _, ex = rowmap_bwd(f_ret_pre, [(u, 256, U_RQ // 256), (u, 256, U_RK // 256), (cos, 256, 0), (sin, 256, 0)], [],
                          [d_qb, d_kb], 1, "ret_pre_bwd", din_dtypes=[f32, f32, None, None], into=(du, U_RQ // 512, [0, 1]))
    du = ex["into"]

    (d_oa, _), (dnw,), ex = rowmap_bwd(per_head(f_dn_post), [(sv["o_a"].reshape(n, 512), 512, 0), (u, 512, U_Z // 512)],
                                       [(wl["dn_norm_w"], (1, LANE), lambda j: (0, 0))], [d_ys[0]], 1, "dn_post_bwd",
                                       into=(du, U_Z // 512, [1]))
    du = ex["into"]
    gr["dn_norm_w"] = dnw[0, 0]
    d_prep = chunk_scan_bwd(dn_step, sv["prep_a"], sv["ck_a"], d_oa.reshape(bsz, seq, 512), "dn_scan_bwd")
    (d_q, d_k, d_v, d_gb, _), _, _ = rowmap_bwd(dn_prep, sv["dn_in"], [], [t.reshape(n, t.shape[2]) for t in d_prep], 1,
                                                "dn_prep_bwd", rows=crow, din_dtypes=[f32] * 4 + [None])
    d_q, d_k, d_v = (t.reshape(bsz, seq, 512) for t in (d_q, d_k, d_v))
    _, (dgp,), ex = rowmap_bwd(f_dn_gates, [(u, 512, U_AB // 512)], [(wl["dn_gate_p"], (8, LANE), lambda j: (0, 0))],
                               [d_gb], 1, "dn_gates_bwd", rows=512, into=(du, U_AB // 512, [0]))
    du = ex["into"]
    gr["dn_a_log"], gr["dn_dt_bias"] = dgp[0, 0, :DN_HEADS], dgp[0, 1, :DN_HEADS]
    d_cw = []
    for kind, d_t in enumerate((d_q, d_k, d_v)):
        cw = (wl["dn_conv_w"], (4, LANE), functools.partial(lambda j, kind: (0, 4 * kind + j), kind=kind))
        (du_new,), (dcw,) = seqmap_bwd(functools.partial(f_dn_pre, kind), [(u3, U_QKV // LANE + 4 * kind)], [cw], [d_t], 4,
                                       "dn_pre%d_bwd" % kind, din_dtype=bf16, into=(du3(), U_QKV // LANE + 4 * kind))
        du = du_new.reshape(n, U_PAD)
        d_cw.append(_cols(dcw))
    gr["dn_conv_w"] = jnp.concatenate(d_cw, axis=1)

    gr["w_in"] = _unpad_w_in(mm(sv["h"], du, "tn", "mm_in_dw", tn=2048))
    token = emit("mix", {k: gr[k] for k in ("w_in", "w_branch", "w_out")})
    d_h = mm(du, wl["w_in"], "nt", "mm_in_dx", dep=token)
    (dx_in,), (dg,), ex = rowmap_bwd(f_norm, [(x_in, D_MODEL, 0)], [row1(wl["attn_norm"])], [d_h], 1, "norm_bwd", add=[dx_mid],
                                     copy16=0)
    gr["attn_norm"] = dg[0, 0]
    big = ("w_in", "w_branch", "w_out", "w_up", "w_down")
    return dx_in, ex["copy16"], emit("small", {k: g for k, g in gr.items() if k not in big})


def _cols(dp):
    ncol, p, _ = dp.shape
    return jnp.transpose(dp, (1, 0, 2)).reshape(p, ncol * LANE)


def _pad_w_in(w):
    segs = sorted(_IN_SEGS, key=lambda s: s[2])
    parts = [lax.slice_in_dim(w, src, src + width, axis=1) for src, width, _ in segs]
    end = segs[-1][2] + segs[-1][1]
    return jnp.concatenate(parts + [jnp.zeros((w.shape[0], U_PAD - end), w.dtype)], axis=1)


def _unpad_w_in(wp):
    return jnp.concatenate([lax.slice_in_dim(wp, dst, dst + width, axis=1) for _, width, dst in _IN_SEGS], axis=1)


def _rope_tables(positions):
    half = RET_DK // 2
    inv = ROPE_BASE ** (-jnp.arange(half, dtype=f32) / half)
    ang = positions.astype(f32).reshape(-1, 1) * inv
    cos, sin = jnp.cos(ang), jnp.sin(ang)
    return jnp.tile(cos, (1, 2 * RET_HEADS)), jnp.tile(sin, (1, 2 * RET_HEADS))


def _layer_weights(lw):
    wl = {}
    wl["w_in"] = _pad_w_in(lw["w_in"])
    for k in ("dn_conv_w", "lru_conv_w", "ffn_conv_w", "lru_wa", "lru_wx"):
        wl[k] = lw[k]
    for k in ("attn_norm", "ffn_norm", "dn_norm_w", "lru_conv_b", "lru_lambda", "ffn_conv_b", "lru_ba", "lru_bx"):
        wl[k] = lw[k].reshape(1, -1)
    gp = jnp.zeros((8, LANE), f32)
    wl["dn_gate_p"] = gp.at[0, :DN_HEADS].set(lw["dn_a_log"]).at[1, :DN_HEADS].set(lw["dn_dt_bias"])
    return wl


REST = ("w_branch", "w_out", "w_up", "w_down")


def forward_backward(x, positions, target, layer_weights, final_norm, on_head, on_grads):
    bsz, seq, d = x.shape
    n = bsz * seq
    cos, sin = _rope_tables(positions)
    xs = x.reshape(n, d)
    saved = []
    for layer in range(DEPTH):
        first, fetch_rest = layer_weights(layer, xs)
        xs, sv = _layer_fwd(xs, _layer_weights(first), fetch_rest, cos, sin, bsz, seq)
        saved.append(sv)
    loss, dx, d_final, dx16 = final_loss(xs, final_norm.reshape(1, d), target.reshape(n, d))
    on_head(loss[0, 0], d_final[0])
    token = None
    for layer in reversed(range(DEPTH)):
        dx, dx16, token = _layer_bwd(dx, dx16, saved[layer], cos, sin, bsz, seq, functools.partial(on_grads, layer), token)
    return dx.reshape(bsz, seq, d)


def local_step(x, positions, target, full):
    grads, head = {layer: {} for layer in range(DEPTH)}, {}

    def layer_weights(layer, _):
        return ({k: a[layer] for k, a in full.items() if k != "final_norm" and k not in REST},
                lambda after: {k: full[k][layer] for k in REST})

    gx = forward_backward(x, positions, target, layer_weights, full["final_norm"],
                          lambda loss, d_final: head.update(loss=loss, d_final=d_final),
                          lambda layer, group, gr: grads[layer].update(
                              {k: jnp.concatenate(g, axis=1) if isinstance(g, tuple) else g for k, g in gr.items()}))
    stacked = {k: jnp.stack([grads[layer][k] for layer in range(DEPTH)]) for k in grads[0]}
    stacked["final_norm"] = head["d_final"]
    return head["loss"], gx, stacked


BIG = (("w_in", 2), ("w_branch", 3), ("w_out", 1), ("w_up", 2), ("w_down", 1))
SMALL_SHARDED = (("dn_conv_w", 2), ("lru_conv_w", 2), ("ffn_conv_w", 2))
REPLICATED = ("attn_norm", "dn_a_log", "dn_dt_bias", "dn_norm_w", "lru_conv_b", "lru_wa", "lru_ba", "lru_wx", "lru_bx",
              "lru_lambda", "ffn_norm", "ffn_conv_b", "final_norm")
WEIGHTS = ("attn_norm", "w_in", "dn_conv_w", "dn_a_log", "dn_dt_bias", "dn_norm_w", "lru_conv_w", "lru_conv_b", "lru_wa",
           "lru_ba", "lru_wx", "lru_bx", "lru_lambda", "w_branch", "w_out", "ffn_norm", "w_up", "ffn_conv_w", "ffn_conv_b",
           "w_down", "final_norm")


def _pack(arrs, dtype, align=16 * LANE):
    flat = jnp.concatenate([a.reshape(-1).astype(dtype) for a in arrs])
    pad = (-flat.shape[0]) % align
    return jnp.pad(flat, (0, pad)).reshape(-1, LANE)


def _unpack(rows, shapes):
    flat = rows.reshape(-1)
    out, pos = [], 0
    for shp in shapes:
        size = math.prod(shp)
        out.append(lax.slice_in_dim(flat, pos, pos + size).reshape(shp))
        pos += size
    return out


def kernel(x, positions, attn_norm, w_in, dn_conv_w, dn_a_log, dn_dt_bias, dn_norm_w, lru_conv_w, lru_conv_b, lru_wa, lru_ba, lru_wx, lru_bx, lru_lambda, w_branch, w_out, ffn_norm, w_up, ffn_conv_w, ffn_conv_b, w_down, final_norm, loss_target, m_attn_norm, m_w_in, m_dn_conv_w, m_dn_a_log, m_dn_dt_bias, m_dn_norm_w, m_lru_conv_w, m_lru_conv_b, m_lru_wa, m_lru_ba, m_lru_wx, m_lru_bx, m_lru_lambda, m_w_branch, m_w_out, m_ffn_norm, m_w_up, m_ffn_conv_w, m_ffn_conv_b, m_w_down, m_final_norm, v_attn_norm, v_w_in, v_dn_conv_w, v_dn_a_log, v_dn_dt_bias, v_dn_norm_w, v_lru_conv_w, v_lru_conv_b, v_lru_wa, v_lru_ba, v_lru_wx, v_lru_bx, v_lru_lambda, v_w_branch, v_w_out, v_ffn_norm, v_w_up, v_ffn_conv_w, v_ffn_conv_b, v_w_down, v_final_norm):
    w = dict(attn_norm=attn_norm, w_in=w_in, dn_conv_w=dn_conv_w, dn_a_log=dn_a_log, dn_dt_bias=dn_dt_bias, dn_norm_w=dn_norm_w,
             lru_conv_w=lru_conv_w, lru_conv_b=lru_conv_b, lru_wa=lru_wa, lru_ba=lru_ba, lru_wx=lru_wx, lru_bx=lru_bx,
             lru_lambda=lru_lambda, w_branch=w_branch, w_out=w_out, ffn_norm=ffn_norm, w_up=w_up, ffn_conv_w=ffn_conv_w,
             ffn_conv_b=ffn_conv_b, w_down=w_down, final_norm=final_norm)
    m = dict(attn_norm=m_attn_norm, w_in=m_w_in, dn_conv_w=m_dn_conv_w, dn_a_log=m_dn_a_log, dn_dt_bias=m_dn_dt_bias,
             dn_norm_w=m_dn_norm_w, lru_conv_w=m_lru_conv_w, lru_conv_b=m_lru_conv_b, lru_wa=m_lru_wa, lru_ba=m_lru_ba,
             lru_wx=m_lru_wx, lru_bx=m_lru_bx, lru_lambda=m_lru_lambda, w_branch=m_w_branch, w_out=m_w_out, ffn_norm=m_ffn_norm,
             w_up=m_w_up, ffn_conv_w=m_ffn_conv_w, ffn_conv_b=m_ffn_conv_b, w_down=m_w_down, final_norm=m_final_norm)
    v = dict(attn_norm=v_attn_norm, w_in=v_w_in, dn_conv_w=v_dn_conv_w, dn_a_log=v_dn_a_log, dn_dt_bias=v_dn_dt_bias,
             dn_norm_w=v_dn_norm_w, lru_conv_w=v_lru_conv_w, lru_conv_b=v_lru_conv_b, lru_wa=v_lru_wa, lru_ba=v_lru_ba,
             lru_wx=v_lru_wx, lru_bx=v_lru_bx, lru_lambda=v_lru_lambda, w_branch=v_w_branch, w_out=v_w_out, ffn_norm=v_ffn_norm,
             w_up=v_w_up, ffn_conv_w=v_ffn_conv_w, ffn_conv_b=v_ffn_conv_b, w_down=v_w_down, final_norm=v_final_norm)

    me = 4 * lax.axis_index("x") + 2 * lax.axis_index("y") + lax.axis_index("c")
    axes = dict(BIG + SMALL_SHARDED)
    conv_names = [k for k, _ in SMALL_SHARDED]

    gathers, tokens, conv_full = {}, [], {}
    for layer in range(DEPTH):
        first = [w["w_in"][layer].astype(bf16)] + ([w[k] for k in conv_names] if layer == 0 else [])
        rest = [w[k][layer].astype(bf16) for k in REST]
        for part, srcs in (("in", first), ("rest", rest)):
            gathers[layer, part], token = exchange_start(srcs, [False] * len(srcs), "gather_%s_start%d" % (part, layer))
            tokens.append(token[0:1, 0:1])
    all_started = functools.reduce(lambda a, b: a + b, tokens)

    def join(land, axis):
        if axis == 0:
            return land.reshape((N_DEV * land.shape[1],) + land.shape[2:])
        return jnp.concatenate([land[p] for p in range(N_DEV)], axis=axis)

    def split(g, axis):
        if isinstance(g, tuple):
            each = N_DEV // len(g)
            size = g[0].shape[axis] // each
            return jnp.stack([lax.slice_in_dim(piece, p * size, (p + 1) * size, axis=axis) for piece in g for p in range(each)])
        size = g.shape[axis] // N_DEV
        if axis == 0:
            return g.reshape((N_DEV, size) + g.shape[1:])
        return jnp.stack([lax.slice_in_dim(g, p * size, (p + 1) * size, axis=axis) for p in range(N_DEV)])

    def layer_weights(layer, x_in):
        lands = exchange_wait(gathers[layer, "in"], x_in, "gather_in_wait%d" % layer)
        lw = {"w_in": join(lands[0], 1)}
        if layer == 0:
            conv_full.update({k: join(lands[1 + i], axes[k]) for i, k in enumerate(conv_names)})
        lw.update({k: conv_full[k][layer] for k in conv_names})
        lw.update({k: w[k][layer] for k in REPLICATED if k != "final_norm"})
        if layer == 0:
            lw["attn_norm"] = lw["attn_norm"] + all_started[0]

        def fetch_rest(after):
            lands_r = exchange_wait(gathers[layer, "rest"], after, "gather_rest_wait%d" % layer)
            return {k: join(lands_r[i], axes[k] - 1) for i, k in enumerate(REST)}

        return lw, fetch_rest

    small_names = conv_names + [k for k in REPLICATED if k != "final_norm"]
    groups = {"ffn": ("w_up", "w_down"), "mix": ("w_in", "w_branch", "w_out")}
    scatters, small_shapes, head = {}, {}, {}

    def on_grads(layer, group, gr):
        if group == "small":
            small_shapes.update({k: gr[k].shape for k in small_names})
            srcs = [_pack([gr[k] for k in small_names], f32)]
            srcs += [_pack([head["loss"].reshape(1), head["d_final"]], f32)] if layer == DEPTH - 1 else []
            modes = [False] * len(srcs)
        else:
            srcs = [split(gr[k], axes[k] - 1).astype(bf16) for k in groups[group]]
            modes = [True] * len(srcs)
        scatters[layer, group], token = exchange_start(srcs, modes, "scatter_%s_start%d" % (group, layer))
        return token

    grad_x = forward_backward(x, positions, loss_target, layer_weights, final_norm,
                              lambda loss_part, d_final: head.update(loss=loss_part, d_final=d_final), on_grads)

    big_sums, small_sums = {}, {}
    for group in ("ffn", "mix"):
        for layer in reversed(range(DEPTH)):
            lands = exchange_wait(scatters[layer, group], grad_x, "scatter_%s_wait%d" % (group, layer))
            for i, k in enumerate(groups[group]):
                shard = w[k].shape[1:]
                big_sums[layer, k] = sum_slots(lands[i].reshape(N_DEV, -1, shard[-1]), "sum_" + k).reshape(shard)
    for layer in reversed(range(DEPTH)):
        lands = exchange_wait(scatters[layer, "small"], grad_x, "scatter_small_wait%d" % layer)
        small_sums[layer] = sum_slots(lands[0], "sum_small")
        if layer == DEPTH - 1:
            head_sum = _unpack(sum_slots(lands[1], "sum_head"), [(1,), final_norm.shape])
    grads = {k: jnp.stack([big_sums[layer, k] for layer in range(DEPTH)]) for k, _ in BIG}
    loss, grads["final_norm"] = head_sum[0][0], head_sum[1]
    small_flat = jnp.stack([small_sums[layer] for layer in range(DEPTH)]).reshape(DEPTH, -1)
    pos = 0
    for k in small_names:
        size = math.prod(small_shapes[k])
        g = lax.slice_in_dim(small_flat, pos, pos + size, axis=1).reshape((DEPTH,) + small_shapes[k])
        pos += size
        ax = dict(SMALL_SHARDED).get(k)
        if ax is None:
            grads[k] = g
        else:
            size = g.shape[ax] // N_DEV
            grads[k] = lax.dynamic_slice_in_dim(g, me * size, size, axis=ax)

    inner_last = {"w_in": (2, 0, 1), "w_up": (0, 2, 1)}
    upd = {}
    for k in WEIGHTS:
        if k in inner_last:
            perm = inner_last[k]
            back = tuple(perm.index(i) for i in range(3))
            g_t = jnp.transpose(grads[k], perm)
            res = adamw(jnp.transpose(w[k], perm), g_t, jnp.transpose(m[k], perm), jnp.transpose(v[k], perm), "adamw_" + k)
            upd[k] = tuple(jnp.transpose(t, back) for t in res)
            grads[k] = jnp.transpose(g_t, back)
        else:
            upd[k] = adamw(w[k], grads[k], m[k], v[k], "adamw_" + k)
    return (loss, grad_x, *[grads[k] for k in WEIGHTS], *[upd[k][0] for k in WEIGHTS], *[upd[k][1] for k in WEIGHTS],
            *[upd[k][2] for k in WEIGHTS])
```

```python
import functools
import math

import jax
import jax.numpy as jnp
from jax import lax
from jax.experimental import pallas as pl
from jax.experimental.pallas import tpu as pltpu

f32 = jnp.float32
bf16 = jnp.bfloat16

D_MODEL = 1024
DEPTH = 4
CHUNK = 64
EPS = 1e-6
DN_HEADS, DN_DK = 4, 128
RET_HEADS, RET_DK, RET_DV = 4, 64, 128
ROPE_BASE = 10000.0
LRU_C = 8.0
D_FF = 2816
N_DEV = 8
LANE = 128
VMEM_LIMIT = 56 * 1024 * 1024

ADAM_LR, ADAM_B1, ADAM_B2, ADAM_EPS, ADAM_WD, ADAM_STEP = 0.001, 0.9, 0.999, 1e-8, 0.01, 10

U_GATES, U_QKV, U_RV, U_RG, U_Z, U_CX, U_CG, U_RQ, U_RK, U_AB = (
    0, 3072, 4608, 5120, 5632, 6144, 6656, 7168, 7424, 7680)
U_PAD = 8192
_IN_SEGS = ((0, 1536, U_QKV), (1536, 8, U_AB), (1544, 512, U_Z), (2056, 256, U_RQ), (2312, 256, U_RK),
            (2568, 512, U_RV), (3080, 512, U_RG), (3592, 512, U_CX), (4104, 512, U_CG), (4616, 3072, U_GATES))


def _params():
    return pltpu.CompilerParams(vmem_limit_bytes=VMEM_LIMIT)


def _pick(dim, pref):
    best = None
    for d in range(LANE, min(dim, pref) + 1, LANE):
        if dim % d == 0:
            best = d
    return best if best is not None else dim


@functools.partial(jax.custom_vjp, nondiff_argnums=(1, 2))
def sroll(x, shift, axis):
    return pltpu.roll(x, shift, axis)


def _sroll_fwd(x, shift, axis):
    return pltpu.roll(x, shift, axis), None


def _sroll_bwd(shift, axis, _, g):
    n = g.shape[axis]
    return (pltpu.roll(g, (n - shift) % n, axis),)


sroll.defvjp(_sroll_fwd, _sroll_bwd)

_DIMS = {"nn": (((1,), (0,)), ((), ())), "nt": (((1,), (1,)), ((), ())), "tn": (((0,), (0,)), ((), ()))}


def _dg(a, b, dims):
    return lax.dot_general(a.astype(bf16), b.astype(bf16), _DIMS[dims], preferred_element_type=f32)


@functools.partial(jax.custom_vjp, nondiff_argnums=(2,))
def bdot(a, b, dims):
    return _dg(a, b, dims)


def _bdot_fwd(a, b, dims):
    return _dg(a, b, dims), (a.astype(bf16), b.astype(bf16))


def _bdot_bwd(dims, res, g):
    a, b = res
    if dims == "nn":
        return _dg(g, b, "nt"), _dg(a, g, "tn")
    if dims == "nt":
        return _dg(g, b, "nn"), _dg(g, a, "tn")
    return _dg(b, g, "nt"), _dg(a, g, "nn")


bdot.defvjp(_bdot_fwd, _bdot_bwd)


def _fdot(a, b, dims):
    return lax.dot_general(a, b, _DIMS[dims], precision=lax.Precision.HIGH, preferred_element_type=f32)


@jax.custom_vjp
def unit_lower_inv_all(mats):
    shape = mats[0].shape
    row = lax.broadcasted_iota(jnp.int32, shape, 0)
    col = lax.broadcasted_iota(jnp.int32, shape, 1)
    eye = jnp.where(row == col, 1.0, 0.0).astype(f32)
    n = [-a for a in mats]
    p = [eye + x for x in n]
    span = 2
    while span < shape[0]:
        n = [_fdot(x, x, "nn") for x in n]
        p = [y + _fdot(y, x, "nn") for y, x in zip(p, n)]
        span *= 2
    return p


def _uli_fwd(mats):
    x = unit_lower_inv_all(mats)
    return x, x


def _uli_bwd(xs, gs):
    t = [_fdot(x, g, "tn") for x, g in zip(xs, gs)]
    return ([-_fdot(y, x, "nt") for y, x in zip(t, xs)],)


unit_lower_inv_all.defvjp(_uli_fwd, _uli_bwd)


@jax.custom_vjp
def known_inverse(invs, mats):
    return invs


def _known_fwd(invs, mats):
    return invs, invs


def _known_bwd(xs, gs):
    return [jnp.zeros_like(x) for x in xs], _uli_bwd(xs, gs)[0]


known_inverse.defvjp(_known_fwd, _known_bwd)


def cumsum_rows(x):
    rows = x.shape[0]
    row = lax.broadcasted_iota(jnp.int32, x.shape, 0)
    s = 1
    while s < rows:
        x = x + jnp.where(row >= s, sroll(x, s, 0), 0.0)
        s *= 2
    return x


def _expm1(x):
    return jnp.tanh(0.5 * x) * (jnp.exp(x) + 1.0)


def _lane_pick(x, lane):
    idx = lax.broadcasted_iota(jnp.int32, x.shape, 1)
    return jnp.sum(jnp.where(idx == lane, x, 0.0), axis=1, keepdims=True)


def _row_pick(x, r):
    idx = lax.broadcasted_iota(jnp.int32, x.shape, 0)
    return jnp.sum(jnp.where(idx == r, x, 0.0), axis=0, keepdims=True)


def _causal_conv(x, halo, w, width):
    if halo is None:
        row = lax.broadcasted_iota(jnp.int32, x.shape, 0)
        acc = x * w[width - 1:width]
        for k in range(width - 1):
            shift = width - 1 - k
            acc = acc + jnp.where(row >= shift, sroll(x, shift, 0), 0.0) * w[k:k + 1]
        return acc
    xe = jnp.concatenate([halo, x], axis=0)
    acc = xe * w[width - 1:width]
    for k in range(width - 1):
        acc = acc + sroll(xe, width - 1 - k, 0) * w[k:k + 1]
    return acc[8:]


def f_norm(ins, ps):
    (x,), (g,) = ins, ps
    return [x * lax.rsqrt(jnp.mean(x * x, axis=-1, keepdims=True) + EPS) * g]


def f_dn_pre(kind, mains, halos, ps):
    y = _causal_conv(mains[0], halos[0], ps[0], 4)
    y = y * jax.nn.sigmoid(y)
    if kind < 2:
        y = y * lax.rsqrt(jnp.sum(y * y, axis=-1, keepdims=True) + EPS)
    if kind == 0:
        y = y * (DN_DK ** -0.5)
    return [y]


def f_dn_gates(ins, ps):
    u, p = ins[0][:, :LANE], ps[0]
    lane = lax.broadcasted_iota(jnp.int32, u.shape, 1)
    g = -jnp.exp(p[0:1]) * jax.nn.softplus(u + p[1:2])
    beta = jax.nn.sigmoid(u)
    return [jnp.where(lane < 4, g, jnp.where(lane < 8, beta, 0.0))]


def per_head(fn):
    def tile_fn(vals, ps):
        heads = [fn([v[:, h * LANE:(h + 1) * LANE] for v in vals], ps) for h in range(vals[0].shape[1] // LANE)]
        return [jnp.concatenate([o[i] for o in heads], axis=1) for i in range(len(heads[0]))]
    return tile_fn


def f_dn_post(ins, ps):
    (o, z), (nw,) = ins, ps
    y = o * lax.rsqrt(jnp.mean(o * o, axis=-1, keepdims=True) + EPS) * nw
    return [y * (z * jax.nn.sigmoid(z))]


def _rot_half(t):
    lane = lax.broadcasted_iota(jnp.int32, t.shape, 1)
    width = t.shape[1]
    first = (lane % RET_DK) < (RET_DK // 2)
    return jnp.where(first, -sroll(t, width - RET_DK // 2, 1), sroll(t, RET_DK // 2, 1))


def f_ret_pre(ins, ps):
    q, k, cos, sin = ins
    qr = q * cos + _rot_half(q) * sin
    kr = (k * cos + _rot_half(k) * sin) * (RET_DK ** -0.5)
    return [qr, kr]


def f_ret_post(ins, ps):
    o, g = ins
    mu = jnp.mean(o, axis=-1, keepdims=True)
    var = jnp.mean(jnp.square(o - mu), axis=-1, keepdims=True)
    return [(o - mu) * lax.rsqrt(var + EPS) * (g * jax.nn.sigmoid(g))]


def f_lru_pre(mains, halos, ps):
    cw, cb, wa, ba, wx, bx, lam = ps
    xc = _causal_conv(mains[0], halos[0], cw, 4) + cb
    r = jax.nn.sigmoid(bdot(xc, wa, "nn") + ba)
    i = jax.nn.sigmoid(bdot(xc, wx, "nn") + bx)
    log_a = -LRU_C * r * jax.nn.softplus(-lam)
    a = jnp.exp(log_a)
    b = jnp.sqrt(-_expm1(2.0 * log_a)) * (i * xc)
    return [a, b]


def f_lru_post(ins, ps):
    h, g = ins
    return [h * jax.nn.gelu(g)]


def f_ffn_mid(mains, halos, ps):
    cwg, cwv, cbg, cbv = ps
    gate = _causal_conv(mains[0], halos[0], cwg, 3) + cbg
    val = _causal_conv(mains[1], halos[1], cwv, 3) + cbv
    return [gate * jax.nn.sigmoid(gate) * val]


def mm(a, b, dims, name, add=None, dep=None, b_koff=0, tm=1536, tn=1536, tk=2816):
    if dims == "tn":
        kdim, m = a.shape
        n = b.shape[1]
    else:
        m, kdim = a.shape
        n = b.shape[0] if dims == "nt" else b.shape[1]
    tm, tn, tk = _pick(m, tm), _pick(n, tn), _pick(kdim, tk)
    nk = kdim // tk
    a_spec = pl.BlockSpec((tk, tm), lambda i, j, k: (k, i)) if dims == "tn" else pl.BlockSpec((tm, tk), lambda i, j, k: (i, k))
    b_spec = (pl.BlockSpec((tn, tk), lambda i, j, k: (j, k + b_koff * nk)) if dims == "nt"
              else pl.BlockSpec((tk, tn), lambda i, j, k: (k, j)))
    o_spec = pl.BlockSpec((tm, tn), lambda i, j, k: (i, j))
    has_add, has_dep = add is not None, dep is not None

    def body(*refs):
        a_ref, b_ref = refs[:2]
        add_ref = refs[2] if has_add else None
        o_ref = refs[2 + has_add + has_dep]
        if nk == 1:
            prod = _dg(a_ref[...], b_ref[...], dims)
            o_ref[...] = prod + add_ref[...] if has_add else prod
            return
        acc_ref = refs[-1]
        k = pl.program_id(2)

        @pl.when(k == 0)
        def _():
            acc_ref[...] = jnp.zeros_like(acc_ref)

        acc_ref[...] += _dg(a_ref[...], b_ref[...], dims)

        @pl.when(k == nk - 1)
        def _():
            o_ref[...] = acc_ref[...] + add_ref[...] if has_add else acc_ref[...]

    args = [a, b] + ([add] if has_add else []) + ([dep] if has_dep else [])
    in_specs = [a_spec, b_spec] + ([o_spec] if has_add else [])
    in_specs += [pl.BlockSpec((8, LANE), lambda i, j, k: (0, 0))] if has_dep else []
    return pl.pallas_call(
        body, name=name, grid=(m // tm, n // tn, nk), in_specs=in_specs, out_specs=o_spec,
        out_shape=jax.ShapeDtypeStruct((m, n), f32), scratch_shapes=[pltpu.VMEM((tm, tn), f32)] if nk > 1 else [],
        compiler_params=_params())(*args)


def rowmap(fn, ins, params, outs, ncol, name, rows=512):
    n = ins[0][0].shape[0]
    r = min(rows, n)
    nin, npar = len(ins), len(params)

    def body(*refs):
        vals = [x[...] for x in refs[:nin]]
        pv = [p[...] for p in refs[nin:nin + npar]]
        for o_ref, o in zip(refs[nin + npar:], fn(vals, pv)):
            o_ref[...] = o.astype(o_ref.dtype)

    in_specs = [pl.BlockSpec((r, cb), functools.partial(lambda j, i, off: (i, off + j), off=off)) for _, cb, off in ins]
    in_specs += [pl.BlockSpec(bs, functools.partial(lambda j, i, f: f(j), f=f)) for _, bs, f in params]
    out_specs = [pl.BlockSpec((r, cb), lambda j, i: (i, j)) for cb, _ in outs]
    out_shape = [jax.ShapeDtypeStruct((n, cb * ncol), dt) for cb, dt in outs]
    res = pl.pallas_call(body, name=name, grid=(ncol, n // r), in_specs=in_specs, out_specs=out_specs,
                         out_shape=out_shape, compiler_params=_params())(*[a for a, _, _ in ins], *[a for a, _, _ in params])
    return res


def rowmap_bwd(fn, ins, params, douts, ncol, name, rows=512, add=None, din_dtypes=None, into=None, copy16=None):
    n = ins[0][0].shape[0]
    r = min(rows, n)
    nin, npar, nout = len(ins), len(params), len(douts)
    add = [None] * nin if add is None else list(add)
    add_idx = [i for i in range(nin) if add[i] is not None]
    din_dtypes = [f32] * nin if din_dtypes is None else list(din_dtypes)
    into_buf, into_off, into_idx = into if into is not None else (None, 0, [])
    has_into, has_copy = into is not None, copy16 is not None
    kept = [i for i in range(nin) if din_dtypes[i] is not None and i not in into_idx]

    def body(*refs):
        vals = [x[...] for x in refs[:nin]]
        pv = [p[...] for p in refs[nin:nin + npar]]
        dys = [d[...] for d in refs[nin + npar:nin + npar + nout]]
        k0 = nin + npar + nout
        add_refs = dict(zip(add_idx, refs[k0:k0 + len(add_idx)]))
        k0 += len(add_idx) + has_into
        din_refs = refs[k0:k0 + len(kept)]
        k0 += len(kept)
        copy_ref = refs[k0] if has_copy else None
        into_ref = refs[k0 + has_copy] if has_into else None
        dp_refs = refs[k0 + has_copy + has_into:]
        _, vjp = jax.vjp(fn, vals, pv)
        dvals, dpv = vjp(dys)
        cot = lambda idx: dvals[idx] + add_refs[idx][...] if idx in add_refs else dvals[idx]
        for d_ref, idx in zip(din_refs, kept):
            d_ref[...] = cot(idx).astype(d_ref.dtype)
        if has_copy:
            copy_ref[...] = cot(copy16).astype(copy_ref.dtype)
        if has_into:
            parts = [cot(idx) for idx in into_idx]
            into_ref[...] = (parts[0] if len(parts) == 1 else jnp.concatenate(parts, axis=1)).astype(into_ref.dtype)

        @pl.when(pl.program_id(1) == 0)
        def _():
            for d_ref in dp_refs:
                d_ref[...] = jnp.zeros_like(d_ref)

        for d_ref, d in zip(dp_refs, dpv):
            d_ref[...] += d

    in_specs = [pl.BlockSpec((r, cb), functools.partial(lambda j, i, off: (i, off + j), off=off)) for _, cb, off in ins]
    in_specs += [pl.BlockSpec(bs, functools.partial(lambda j, i, f: f(j), f=f)) for _, bs, f in params]
    in_specs += [pl.BlockSpec((r, d.shape[1] // ncol), lambda j, i: (i, j)) for d in douts]
    in_specs += [pl.BlockSpec((r, ins[i][1]), lambda j, i: (i, j)) for i in add_idx]
    out_specs = [pl.BlockSpec((r, ins[i][1]), lambda j, i: (i, j)) for i in kept]
    out_shape = [jax.ShapeDtypeStruct((n, ins[i][1] * ncol), din_dtypes[i]) for i in kept]
    args = [a for a, _, _ in ins] + [a for a, _, _ in params] + list(douts) + [add[i] for i in add_idx]
    aliases = {}
    if has_copy:
        out_specs += [pl.BlockSpec((r, ins[copy16][1]), lambda j, i: (i, j))]
        out_shape += [jax.ShapeDtypeStruct((n, ins[copy16][1] * ncol), bf16)]
    if has_into:
        assert ncol == 1
        in_specs += [pl.BlockSpec(memory_space=pl.ANY)]
        aliases[len(args)] = len(out_shape)
        args += [into_buf]
        out_specs += [pl.BlockSpec((r, sum(ins[i][1] for i in into_idx)), lambda j, i: (i, into_off))]
        out_shape += [jax.ShapeDtypeStruct(into_buf.shape, into_buf.dtype)]
    pshapes = [tuple(d for d in bs if d is not None) for _, bs, _ in params]
    out_specs += [pl.BlockSpec((None,) + ps, functools.partial(lambda j, i, nd: (j,) + (0,) * nd, nd=len(ps))) for ps in pshapes]
    out_shape += [jax.ShapeDtypeStruct((ncol,) + ps, f32) for ps in pshapes]
    res = pl.pallas_call(body, name=name, grid=(ncol, n // r), in_specs=in_specs, out_specs=out_specs, out_shape=out_shape,
                         input_output_aliases=aliases, compiler_params=_params())(*args)
    dins = [None] * nin
    for pos, i in enumerate(kept):
        dins[i] = res[pos]
    pos = len(kept)
    extras = {}
    if has_copy:
        extras["copy16"] = res[pos]
        pos += 1
    if has_into:
        extras["into"] = res[pos]
        pos += 1
    return dins, res[pos:], extras


SEQ_ROWS = 2048


def seqmap(fn, ins, params, nouts, ncol, name, out_dtype=f32):
    bsz, seq, _ = ins[0][0].shape
    r = min(SEQ_ROWS, seq)
    nin, npar = len(ins), len(params)

    def body(*refs):
        in_refs = refs[:nin]
        pv = [p[...] for p in refs[nin:nin + npar]]
        out_refs = refs[nin + npar:]

        def step(i, carry):
            r0 = pl.multiple_of(i * r, r)
            h0 = pl.multiple_of(jnp.maximum(r0 - 8, 0), 8)
            mains = [x[pl.ds(r0, r), :] for x in in_refs]
            halos = [jnp.where(i == 0, 0.0, x[pl.ds(h0, 8), :]) for x in in_refs]
            for o_ref, o in zip(out_refs, fn(mains, halos, pv)):
                o_ref[pl.ds(r0, r), :] = o.astype(o_ref.dtype)
            return carry

        if r == seq:
            for o_ref, o in zip(out_refs, fn([x[...] for x in in_refs], [None] * nin, pv)):
                o_ref[...] = o.astype(o_ref.dtype)
        else:
            lax.fori_loop(0, seq // r, step, 0)

    in_specs = [pl.BlockSpec((None, seq, LANE), functools.partial(lambda j, b, off: (b, 0, off + j), off=off)) for _, off in ins]
    in_specs += [pl.BlockSpec(bs, functools.partial(lambda j, b, f: f(j), f=f)) for _, bs, f in params]
    out_specs = [pl.BlockSpec((None, seq, LANE), lambda j, b: (b, 0, j)) for _ in range(nouts)]
    out_shape = [jax.ShapeDtypeStruct((bsz, seq, LANE * ncol), out_dtype) for _ in range(nouts)]
    return pl.pallas_call(body, name=name, grid=(ncol, bsz), in_specs=in_specs, out_specs=out_specs,
                          out_shape=out_shape, compiler_params=_params())(*[a for a, _ in ins], *[a for a, _, _ in params])


def seqmap_bwd(fn, ins, params, douts, ncol, name, din_dtype=f32, into=None):
    bsz, seq, _ = ins[0][0].shape
    r = min(SEQ_ROWS, seq)
    nin, npar, nout = len(ins), len(params), len(douts)
    narrow = din_dtype != f32

    def body(*refs):
        in_refs = refs[:nin]
        pv = [p[...] for p in refs[nin:nin + npar]]
        dy_refs = refs[nin + npar:nin + npar + nout]
        k0 = nin + npar + nout + (into is not None)
        dout_refs = refs[k0:k0 + nin]
        dp_refs = refs[k0 + nin:k0 + nin + npar]
        din_refs = refs[k0 + nin + npar:] if narrow else dout_refs

        def step(i, dp_acc):
            r0 = pl.multiple_of(i * r, r)
            h0 = pl.multiple_of(jnp.maximum(r0 - 8, 0), 8)
            mains = [x[pl.ds(r0, r), :] for x in in_refs]
            halos_raw = [x[pl.ds(h0, 8), :] for x in in_refs]

            def tile(mains, halos_raw, pv):
                return fn(mains, [jnp.where(i == 0, 0.0, h) for h in halos_raw], pv)

            _, vjp = jax.vjp(tile, mains, halos_raw, pv)
            dm, dh, dp = vjp([d[pl.ds(r0, r), :] for d in dy_refs])
            for d_ref, m, h in zip(din_refs, dm, dh):
                d_ref[pl.ds(r0, r), :] = m
                d_ref[pl.ds(h0, 8), :] += h
            return [acc + d for acc, d in zip(dp_acc, dp)]

        if r == seq:
            _, vjp = jax.vjp(lambda mains, pv: fn(mains, [None] * nin, pv), [x[...] for x in in_refs], pv)
            dm, dp = vjp([d[...] for d in dy_refs])
            for o_ref, m in zip(dout_refs, dm):
                o_ref[...] = m.astype(o_ref.dtype)
        else:
            dp = lax.fori_loop(0, seq // r, step, [jnp.zeros(p.shape, f32) for p in pv])
            if narrow:
                for o_ref, d_ref in zip(dout_refs, din_refs):
                    o_ref[...] = d_ref[...].astype(o_ref.dtype)

        @pl.when(pl.program_id(1) == 0)
        def _():
            for d_ref in dp_refs:
                d_ref[...] = jnp.zeros_like(d_ref)

        for d_ref, d in zip(dp_refs, dp):
            d_ref[...] += d

    in_specs = [pl.BlockSpec((None, seq, LANE), functools.partial(lambda j, b, off: (b, 0, off + j), off=off)) for _, off in ins]
    in_specs += [pl.BlockSpec(bs, functools.partial(lambda j, b, f: f(j), f=f)) for _, bs, f in params]
    in_specs += [pl.BlockSpec((None, seq, LANE), lambda j, b: (b, 0, j)) for _ in range(nout)]
    out_specs = [pl.BlockSpec((None, seq, LANE), lambda j, b: (b, 0, j)) for _ in range(nin)]
    pshapes = [tuple(d for d in bs if d is not None) for _, bs, _ in params]
    out_specs += [pl.BlockSpec((None,) + ps, functools.partial(lambda j, b, nd: (j,) + (0,) * nd, nd=len(ps))) for ps in pshapes]
    out_shape = [jax.ShapeDtypeStruct((bsz, seq, LANE * ncol), din_dtype) for _ in range(nin)]
    out_shape += [jax.ShapeDtypeStruct((ncol,) + ps, f32) for ps in pshapes]
    args = [a for a, _ in ins] + [a for a, _, _ in params] + list(douts)
    aliases = {}
    if into is not None:
        assert nin == 1 and into[0].dtype == din_dtype
        in_specs += [pl.BlockSpec(memory_space=pl.ANY)]
        aliases[len(args)] = 0
        args += [into[0]]
        out_specs[0] = pl.BlockSpec((None, seq, LANE), lambda j, b: (b, 0, into[1] + j))
        out_shape[0] = jax.ShapeDtypeStruct(into[0].shape, din_dtype)
    res = pl.pallas_call(body, name=name, grid=(ncol, bsz), in_specs=in_specs, out_specs=out_specs, out_shape=out_shape,
                         scratch_shapes=[pltpu.VMEM((seq, LANE), f32) for _ in range(nin)] if narrow and r != seq else [],
                         input_output_aliases=aliases, compiler_params=_params())(*args)
    return res[:nin], res[nin:]


def _tri_masks():
    row = lax.broadcasted_iota(jnp.int32, (CHUNK, CHUNK), 0)
    col = lax.broadcasted_iota(jnp.int32, (CHUNK, CHUNK), 1)
    return row >= col, row > col


CHUNKS_PER_STEP = 4


def _by_rows(parts, per_row):
    rows = [jnp.concatenate(parts[i:i + per_row], axis=1) for i in range(0, len(parts), per_row)]
    return jnp.concatenate(rows, axis=0)


def dn_prep(vals, ps):
    q, k, v, gb = vals[:4]
    nchunk = q.shape[0] // CHUNK
    causal, strict = _tri_masks()
    gbs = [gb[c * CHUNK:(c + 1) * CHUNK] for c in range(nchunk)]
    gcs = [cumsum_rows(g) for g in gbs]
    gcts = [g.T for g in gcs]
    chains = [(c, h) for c in range(nchunk) for h in range(DN_HEADS)]
    part = lambda t, c, h: t[c * CHUNK:(c + 1) * CHUNK, h * DN_DK:(h + 1) * DN_DK]
    qh = [part(q, c, h) for c, h in chains]
    kh = [part(k, c, h) for c, h in chains]
    vh = [part(v, c, h) for c, h in chains]
    g_col = [_lane_pick(gcs[c], h) for c, h in chains]
    beta = [_lane_pick(gbs[c], DN_HEADS + h) for c, h in chains]
    g_row = [_row_pick(gcts[c], h)[:, :CHUNK] for c, h in chains]
    decay = [jnp.where(causal, jnp.exp(jnp.where(causal, gc - gr, 0.0)), 0.0) for gc, gr in zip(g_col, g_row)]
    k_beta = [a * b for a, b in zip(kh, beta)]
    eg = [jnp.exp(g) for g in g_col]
    kk = [bdot(a, b, "nt") for a, b in zip(k_beta, kh)]
    qk = [bdot(a, b, "nt") for a, b in zip(qh, kh)]
    lower = [jnp.where(strict, a * d, 0.0) for a, d in zip(kk, decay)]
    if len(vals) == 5:
        t_inv = known_inverse([part(vals[4], c, h)[:, :CHUNK] for c, h in chains], lower)
    else:
        t_inv = unit_lower_inv_all(lower)
    u = [bdot(t, a * b, "nn") for t, a, b in zip(t_inv, vh, beta)]
    w = [bdot(t, a * e, "nn") for t, a, e in zip(t_inv, k_beta, eg)]
    attn = [jnp.concatenate([a * d, jnp.zeros((CHUNK, DN_DK - CHUNK), f32)], axis=1) for a, d in zip(qk, decay)]
    qd = [a * e for a, e in zip(qh, eg)]
    kd = [a * jnp.exp(_row_pick(g, CHUNK - 1) - g) for a, g in zip(kh, g_col)]
    g_last = jnp.concatenate([jnp.broadcast_to(_row_pick(g, CHUNK - 1), g.shape) for g in gcs], axis=0)
    outs = [_by_rows(t, DN_HEADS) for t in (u, w, attn, qd, kd)] + [g_last]
    if len(vals) == 4:
        wide = [jnp.concatenate([t, jnp.zeros((CHUNK, DN_DK - CHUNK), f32)], axis=1) for t in t_inv]
        outs.append(_by_rows(wide, DN_HEADS))
    return outs


def dn_step(state, u, w, attn, qd, kd, g_last):
    bsz = u.shape[0]
    chains = [(b, h) for b in range(bsz) for h in range(DN_HEADS)]
    part = lambda t, b, h: t[b, :, h * DN_DK:(h + 1) * DN_DK]
    ws = [bdot(part(w, b, h), s, "nn") for (b, h), s in zip(chains, state)]
    qs = [bdot(part(qd, b, h), s, "nn") for (b, h), s in zip(chains, state)]
    v_new = [part(u, b, h) - x for (b, h), x in zip(chains, ws)]
    av = [bdot(attn[b, :, h * DN_DK:h * DN_DK + CHUNK], x, "nn") for (b, h), x in zip(chains, v_new)]
    kv = [bdot(part(kd, b, h), x, "tn") for (b, h), x in zip(chains, v_new)]
    ge = [jnp.exp(_row_pick(_lane_pick(g_last[b], h), 0)) for b, h in chains]
    new_state = [s * g + x for s, g, x in zip(state, ge, kv)]
    outs = [a + b for a, b in zip(qs, av)]
    return new_state, jnp.concatenate([jnp.concatenate(outs[b * DN_HEADS:(b + 1) * DN_HEADS], axis=1)[None]
                                       for b in range(bsz)], axis=0)


def _ret_log_gamma(h):
    return math.log(1.0 - 2.0 ** (-5.0 - h))


def ret_prep(vals, ps):
    q, k, v = vals
    nchunk = q.shape[0] // CHUNK
    causal, _ = _tri_masks()
    row = lax.broadcasted_iota(jnp.int32, (CHUNK, CHUNK), 0)
    col = lax.broadcasted_iota(jnp.int32, (CHUNK, CHUNK), 1)
    dist = (row - col).astype(f32)
    lane = lax.broadcasted_iota(jnp.int32, (CHUNK, q.shape[1]), 1)
    dmask = [jnp.where(causal, jnp.exp(jnp.where(causal, dist, 0.0) * _ret_log_gamma(h)), 0.0) for h in range(RET_HEADS)]
    chains = [(c, h) for c in range(nchunk) for h in range(RET_HEADS)]
    rows = lambda t, c: t[c * CHUNK:(c + 1) * CHUNK]
    scores = [bdot(jnp.where((lane // RET_DK) == h, rows(q, c), 0.0), rows(k, c), "nt") * dmask[h] for c, h in chains]
    inner = [bdot(s, rows(v, c)[:, h * RET_DV:(h + 1) * RET_DV], "nn") for s, (c, h) in zip(scores, chains)]
    return [_by_rows(inner, RET_HEADS)]


def ret_step(state, q, k, v, inner):
    bsz = q.shape[0]
    idx = lax.broadcasted_iota(jnp.int32, (CHUNK, 1), 0).astype(f32)
    lane = lax.broadcasted_iota(jnp.int32, (CHUNK, q.shape[2]), 1)
    chains = [(b, h) for b in range(bsz) for h in range(RET_HEADS)]
    part = lambda t, b, h: t[b, :, h * RET_DV:(h + 1) * RET_DV]
    cross = [bdot(q[b], s, "nn") for (b, h), s in zip(chains, state)]
    kz = [jnp.where((lane // RET_DK) == h, k[b], 0.0) * jnp.exp((CHUNK - 1.0 - idx) * _ret_log_gamma(h)) for b, h in chains]
    kv = [bdot(a, part(v, b, h), "tn") for a, (b, h) in zip(kz, chains)]
    outs = [x * jnp.exp((idx + 1.0) * _ret_log_gamma(h)) + part(inner, b, h) for x, (b, h) in zip(cross, chains)]
    new_state = [s * math.exp(CHUNK * _ret_log_gamma(h)) + x for s, x, (b, h) in zip(state, kv, chains)]
    return new_state, jnp.concatenate([jnp.concatenate(outs[b * RET_HEADS:(b + 1) * RET_HEADS], axis=1)[None]
                                       for b in range(bsz)], axis=0)


SCAN_CHUNKS = 8


def chunk_scan(step_fn, ins, state_shape, out_width, name):
    bsz, seq, _ = ins[0].shape
    nchunk = seq // CHUNK
    nin = len(ins)
    nh = state_shape[0]
    per = SCAN_CHUNKS if nchunk % SCAN_CHUNKS == 0 else 1

    def body(*refs):
        in_refs = refs[:nin]
        o_ref, ck_ref, s_ref = refs[nin:]

        @pl.when(pl.program_id(0) == 0)
        def _():
            s_ref[...] = jnp.zeros_like(s_ref)

        state = [s_ref[i] for i in range(bsz * nh)]
        for c in range(per):
            rows = slice(c * CHUNK, (c + 1) * CHUNK)
            for i in range(bsz * nh):
                ck_ref[i // nh, c, i % nh] = state[i]
            state, out = step_fn(state, *[x[:, rows, :].astype(f32) for x in in_refs])
            o_ref[:, rows, :] = out
        for i in range(bsz * nh):
            s_ref[i] = state[i]

    in_specs = [pl.BlockSpec((bsz, per * CHUNK, x.shape[2]), lambda n: (0, n, 0)) for x in ins]
    out_specs = [pl.BlockSpec((bsz, per * CHUNK, out_width), lambda n: (0, n, 0)),
                 pl.BlockSpec((bsz, per) + tuple(state_shape), lambda n: (0, n, 0, 0, 0))]
    out_shape = [jax.ShapeDtypeStruct((bsz, seq, out_width), f32),
                 jax.ShapeDtypeStruct((bsz, nchunk) + tuple(state_shape), f32)]
    return pl.pallas_call(body, name=name, grid=(nchunk // per,), in_specs=in_specs, out_specs=out_specs, out_shape=out_shape,
                          scratch_shapes=[pltpu.VMEM((bsz * nh,) + tuple(state_shape[1:]), f32)],
                          compiler_params=_params())(*ins)


def chunk_scan_bwd(step_fn, ins, ckpt, dout, name):
    bsz, seq, _ = ins[0].shape
    nchunk = seq // CHUNK
    nin = len(ins)
    state_shape = ckpt.shape[2:]
    nh = state_shape[0]
    per = SCAN_CHUNKS if nchunk % SCAN_CHUNKS == 0 else 1
    nstep = nchunk // per

    def body(*refs):
        in_refs = refs[:nin]
        ck_ref, do_ref = refs[nin:nin + 2]
        din_refs = refs[nin + 2:nin + 2 + nin]
        ds_ref = refs[-1]

        @pl.when(pl.program_id(0) == 0)
        def _():
            ds_ref[...] = jnp.zeros_like(ds_ref)

        dstate = [ds_ref[i] for i in range(bsz * nh)]
        for c in reversed(range(per)):
            rows = slice(c * CHUNK, (c + 1) * CHUNK)
            state = [ck_ref[i // nh, c, i % nh] for i in range(bsz * nh)]
            _, vjp = jax.vjp(step_fn, state, *[x[:, rows, :].astype(f32) for x in in_refs])
            grads = vjp((dstate, do_ref[:, rows, :]))
            dstate = grads[0]
            for d_ref, d in zip(din_refs, grads[1:]):
                d_ref[:, rows, :] = d
        for i in range(bsz * nh):
            ds_ref[i] = dstate[i]

    rev = lambda n: (0, nstep - 1 - n, 0)
    in_specs = [pl.BlockSpec((bsz, per * CHUNK, x.shape[2]), rev) for x in ins]
    in_specs += [pl.BlockSpec((bsz, per) + tuple(state_shape), lambda n: (0, nstep - 1 - n, 0, 0, 0)),
                 pl.BlockSpec((bsz, per * CHUNK, dout.shape[2]), rev)]
    out_specs = [pl.BlockSpec((bsz, per * CHUNK, x.shape[2]), rev) for x in ins]
    out_shape = [jax.ShapeDtypeStruct(x.shape, f32) for x in ins]
    return pl.pallas_call(body, name=name, grid=(nstep,), in_specs=in_specs, out_specs=out_specs, out_shape=out_shape,
                          scratch_shapes=[pltpu.VMEM((bsz * nh,) + tuple(state_shape[1:]), f32)],
                          compiler_params=_params())(*ins, ckpt, dout)


LRU_ROWS = 512


def lru_scan(a, b):
    bsz, seq, width = a.shape
    rb = min(LRU_ROWS, seq)
    seqs = range(bsz)

    def body(a_ref, b_ref, h_ref, hp_ref, carry_ref):
        @pl.when(pl.program_id(0) == 0)
        def _():
            carry_ref[...] = jnp.zeros_like(carry_ref)

        row = lax.broadcasted_iota(jnp.int32, (8, width), 0)

        def tile(t, hprev):
            r0 = pl.multiple_of(t * 8, 8)
            ca = [a_ref[i, pl.ds(r0, 8), :] for i in seqs]
            cb = [b_ref[i, pl.ds(r0, 8), :] for i in seqs]
            for s in (1, 2, 4):
                m = row >= s
                cb = [jnp.where(m, x * pltpu.roll(y, s, 0) + y, y) for x, y in zip(ca, cb)]
                ca = [jnp.where(m, x * pltpu.roll(x, s, 0), x) for x in ca]
            h = [y + x * p for x, y, p in zip(ca, cb, hprev)]
            for i in seqs:
                h_ref[i, pl.ds(r0, 8), :] = h[i]
                hp_ref[i, pl.ds(r0, 8), :] = jnp.where(row == 0, hprev[i], pltpu.roll(h[i], 1, 0))
            return tuple(_row_pick(x, 7) for x in h)

        last = lax.fori_loop(0, rb // 8, tile, tuple(carry_ref[i:i + 1, :] for i in seqs))
        for i in seqs:
            carry_ref[i:i + 1, :] = last[i]

    spec = pl.BlockSpec((bsz, rb, width), lambda i: (0, i, 0))
    return pl.pallas_call(body, name="lru_scan", grid=(seq // rb,), in_specs=[spec, spec], out_specs=[spec, spec],
                          out_shape=[jax.ShapeDtypeStruct(a.shape, f32)] * 2,
                          scratch_shapes=[pltpu.VMEM((max(8, bsz), width), f32)], compiler_params=_params())(a, b)


def lru_scan_bwd(a, hp, dh):
    bsz, seq, width = a.shape
    rb = min(LRU_ROWS, seq)
    nblk = seq // rb
    seqs = range(bsz)

    def body(a_ref, hp_ref, dh_ref, da_ref, db_ref, carry_ref):
        @pl.when(pl.program_id(0) == 0)
        def _():
            carry_ref[...] = jnp.zeros_like(carry_ref)

        row = lax.broadcasted_iota(jnp.int32, (8, width), 0)
        ntile = rb // 8

        def tile(t, mu_next):
            r0 = pl.multiple_of((ntile - 1 - t) * 8, 8)
            ca = [a_ref[i, pl.ds(r0, 8), :] for i in seqs]
            dh_t = [dh_ref[i, pl.ds(r0, 8), :] for i in seqs]
            cb = [x * y for x, y in zip(ca, dh_t)]
            for s in (1, 2, 4):
                m = row < 8 - s
                cb = [jnp.where(m, x * pltpu.roll(y, 8 - s, 0) + y, y) for x, y in zip(ca, cb)]
                ca = [jnp.where(m, x * pltpu.roll(x, 8 - s, 0), x) for x in ca]
            mu = [y + x * p for x, y, p in zip(ca, cb, mu_next)]
            for i in seqs:
                lam = dh_t[i] + jnp.where(row == 7, mu_next[i], pltpu.roll(mu[i], 7, 0))
                db_ref[i, pl.ds(r0, 8), :] = lam
                da_ref[i, pl.ds(r0, 8), :] = lam * hp_ref[i, pl.ds(r0, 8), :]
            return tuple(_row_pick(x, 0) for x in mu)

        last = lax.fori_loop(0, ntile, tile, tuple(carry_ref[i:i + 1, :] for i in seqs))
        for i in seqs:
            carry_ref[i:i + 1, :] = last[i]

    spec = pl.BlockSpec((bsz, rb, width), lambda i: (0, nblk - 1 - i, 0))
    return pl.pallas_call(body, name="lru_scan_bwd", grid=(nblk,), in_specs=[spec] * 3, out_specs=[spec, spec],
                          out_shape=[jax.ShapeDtypeStruct(a.shape, f32)] * 2,
                          scratch_shapes=[pltpu.VMEM((max(8, bsz), width), f32)], compiler_params=_params())(a, hp, dh)


MERGE_ROWS = 512


def branch_merge(ys, w_branch, u):
    n = ys[0].shape[0]
    tm = min(MERGE_ROWS, n)

    def body(ya, yb, yc, w_ref, g0, g1, g2, o_ref):
        acc = None
        for i, (y_ref, g_ref) in enumerate(((ya, g0), (yb, g1), (yc, g2))):
            term = jax.nn.sigmoid(g_ref[...]) * _dg(y_ref[...], w_ref[i], "nn")
            acc = term if acc is None else acc + term
        o_ref[...] = acc.astype(o_ref.dtype)

    y_spec = pl.BlockSpec((tm, ys[0].shape[1]), lambda i: (i, 0))
    g_specs = [pl.BlockSpec((tm, D_MODEL), functools.partial(lambda i, k: (i, k), k=k)) for k in range(3)]
    return pl.pallas_call(
        body, name="branch_merge", grid=(n // tm,),
        in_specs=[y_spec] * 3 + [pl.BlockSpec(w_branch.shape, lambda i: (0, 0, 0))] + g_specs,
        out_specs=pl.BlockSpec((tm, D_MODEL), lambda i: (i, 0)), out_shape=jax.ShapeDtypeStruct((n, D_MODEL), bf16),
        compiler_params=_params())(*ys, w_branch, u, u, u)


def branch_merge_bwd(ys, w_branch, u, d_merged, du):
    n = ys[0].shape[0]
    tm = min(MERGE_ROWS, n)

    def body(ya, yb, yc, w_ref, g0, g1, g2, dm_ref, du_in, db0, db1, db2, du_ref):
        dm = dm_ref[...]
        d_gates = []
        for i, (y_ref, g_ref, db_ref) in enumerate(((ya, g0, db0), (yb, g1, db1), (yc, g2, db2))):
            s = jax.nn.sigmoid(g_ref[...])
            db_ref[...] = (dm * s).astype(db_ref.dtype)
            d_gates.append(dm * _dg(y_ref[...], w_ref[i], "nn") * (s * (1.0 - s)))
        du_ref[...] = jnp.concatenate(d_gates, axis=1).astype(du_ref.dtype)

    y_spec = pl.BlockSpec((tm, ys[0].shape[1]), lambda i: (i, 0))
    row = pl.BlockSpec((tm, D_MODEL), lambda i: (i, 0))
    g_specs = [pl.BlockSpec((tm, D_MODEL), functools.partial(lambda i, k: (i, k), k=k)) for k in range(3)]
    res = pl.pallas_call(
        body, name="branch_merge_bwd", grid=(n // tm,),
        in_specs=[y_spec] * 3 + [pl.BlockSpec(w_branch.shape, lambda i: (0, 0, 0))] + g_specs + [row, pl.BlockSpec(memory_space=pl.ANY)],
        out_specs=[row] * 3 + [pl.BlockSpec((tm, 3 * D_MODEL), lambda i: (i, 0))],
        out_shape=[jax.ShapeDtypeStruct((n, D_MODEL), bf16)] * 3 + [jax.ShapeDtypeStruct(du.shape, du.dtype)],
        input_output_aliases={8: 3}, compiler_params=_params())(*ys, w_branch, u, u, u, d_merged, du)
    return list(res[:3]), res[3]


def final_loss(x, g, target):
    n, d = x.shape
    r = min(256, n)

    def body(x_ref, g_ref, t_ref, loss_ref, dx_ref, dg_ref, dx16_ref):
        @pl.when(pl.program_id(0) == 0)
        def _():
            loss_ref[...] = jnp.zeros_like(loss_ref)
            dg_ref[...] = jnp.zeros_like(dg_ref)

        tgt = t_ref[...]

        def loss_fn(xv, gv):
            y = f_norm([xv], [gv])[0]
            return 0.5 * jnp.sum(jnp.mean(jnp.square(y - tgt), axis=-1, keepdims=True), axis=0, keepdims=True)

        val, vjp = jax.vjp(loss_fn, x_ref[...], g_ref[...])
        dx, dg = vjp(jnp.ones_like(val))
        loss_ref[...] += val
        dx_ref[...] = dx
        dx16_ref[...] = dx.astype(dx16_ref.dtype)
        dg_ref[...] += dg

    row = pl.BlockSpec((r, d), lambda i: (i, 0))
    return pl.pallas_call(
        body, name="final_loss", grid=(n // r,), in_specs=[row, pl.BlockSpec((1, d), lambda i: (0, 0)), row],
        out_specs=[pl.BlockSpec((8, LANE), lambda i: (0, 0)), row, pl.BlockSpec((1, d), lambda i: (0, 0)), row],
        out_shape=[jax.ShapeDtypeStruct((8, LANE), f32), jax.ShapeDtypeStruct((n, d), f32), jax.ShapeDtypeStruct((1, d), f32),
                   jax.ShapeDtypeStruct((n, d), bf16)],
        compiler_params=_params())(x, g, target)


_HBM = pl.BlockSpec(memory_space=pltpu.HBM)
_SEM = pl.BlockSpec(memory_space=pltpu.SEMAPHORE)
_EFFECT = pltpu.SideEffectType.DATAFLOW_SIDE_EFFECTING


def _peer(k):
    mx, my, mc = lax.axis_index("x"), lax.axis_index("y"), lax.axis_index("c")
    px, py, pc = (mx + (k >> 2)) % 2, (my + ((k >> 1) & 1)) % 2, (mc + (k & 1)) % 2
    return (px, py, pc), 4 * px + 2 * py + pc


def _peer_copy(k, i, x_ref, land_ref, send_sems, recv_sems, scatter):
    me = 4 * lax.axis_index("x") + 2 * lax.axis_index("y") + lax.axis_index("c")
    dev, slot = _peer(k)
    sem = i * (N_DEV - 1) + k - 1
    return pltpu.make_async_remote_copy(
        src_ref=x_ref.at[slot] if scatter else x_ref, dst_ref=land_ref.at[me], send_sem=send_sems.at[sem],
        recv_sem=recv_sems.at[sem], device_id=dev, device_id_type=pl.DeviceIdType.MESH)


def _own_copy(i, x_ref, land_ref, own_sems, scatter):
    me = 4 * lax.axis_index("x") + 2 * lax.axis_index("y") + lax.axis_index("c")
    return pltpu.make_async_copy(x_ref.at[me] if scatter else x_ref, land_ref.at[me], own_sems.at[i])


def exchange_start(xs, scatters, name):
    nx = len(xs)
    lands = [lax.empty((N_DEV,) + tuple(x.shape[1:] if sc else x.shape), x.dtype) for x, sc in zip(xs, scatters)]
    nsem = nx * (N_DEV - 1)

    def body(*refs):
        x_refs, land_refs = refs[:nx], refs[nx:2 * nx]
        send_sems, recv_sems, own_sems = refs[2 * nx:2 * nx + 3]
        token = refs[-1]
        for i in range(nx):
            for k in range(1, N_DEV):
                _peer_copy(k, i, x_refs[i], land_refs[i], send_sems, recv_sems, scatters[i]).start()
            _own_copy(i, x_refs[i], land_refs[i], own_sems, scatters[i]).start()
        token[...] = jnp.zeros_like(token)

    hbm = lambda a: pltpu.HBM(a.shape, a.dtype)
    res = pl.pallas_call(
        body, name=name, in_specs=(_HBM,) * (2 * nx),
        out_specs=(_SEM, _SEM, _SEM) + (_HBM,) * (2 * nx) + (pl.BlockSpec(memory_space=pltpu.VMEM),),
        input_output_aliases={i: 3 + i for i in range(2 * nx)},
        out_shape=(pltpu.SemaphoreType.DMA((nsem,)), pltpu.SemaphoreType.DMA((nsem,)), pltpu.SemaphoreType.DMA((nx,)),
                   *[hbm(a) for a in xs], *[hbm(a) for a in lands], jax.ShapeDtypeStruct((8, LANE), f32)),
        compiler_params=pltpu.CompilerParams(has_side_effects=_EFFECT),
    )(*[pltpu.with_memory_space_constraint(a, pltpu.HBM) for a in list(xs) + lands])
    return (res[0], res[1], res[2], list(res[3:3 + nx]), list(res[3 + nx:3 + 2 * nx]), tuple(scatters)), res[-1]


def exchange_wait(started, after, name):
    send_sems, recv_sems, own_sems, x_thrus, land_thrus, scatters = started
    nx = len(x_thrus)

    def body(*refs):
        x_refs, land_refs = refs[:nx], refs[nx:2 * nx]
        send_sems, recv_sems, own_sems = refs[2 * nx:2 * nx + 3]
        for i in range(nx):
            for k in range(1, N_DEV):
                cp = _peer_copy(k, i, x_refs[i], land_refs[i], send_sems, recv_sems, scatters[i])
                cp.wait_send()
                cp.wait_recv()
            _own_copy(i, x_refs[i], land_refs[i], own_sems, scatters[i]).wait()

    hbm = lambda a: pltpu.HBM(a.shape, a.dtype)
    res = pl.pallas_call(
        body, name=name, in_specs=(_HBM,) * (2 * nx) + (_SEM, _SEM, _SEM, pl.BlockSpec(memory_space=pl.ANY)),
        out_specs=(_HBM,) * (2 * nx), input_output_aliases={i: i for i in range(2 * nx)},
        out_shape=tuple(hbm(a) for a in list(x_thrus) + list(land_thrus)),
        compiler_params=pltpu.CompilerParams(has_side_effects=_EFFECT),
    )(*x_thrus, *land_thrus, send_sems, recv_sems, own_sems, after)
    return list(res[nx:])


def sum_slots(x, name):
    _, rows_total, cols = x.shape
    row_bytes = N_DEV * ((cols + LANE - 1) // LANE) * LANE * x.dtype.itemsize
    r = _pick_rows(rows_total, max(16, (4 * 1024 * 1024) // row_bytes // 16 * 16))

    def body(x_ref, o_ref):
        acc = x_ref[0].astype(f32)
        for s in range(1, N_DEV):
            acc = acc + x_ref[s].astype(f32)
        o_ref[...] = acc

    return pl.pallas_call(body, name=name, grid=(rows_total // r,),
                          in_specs=[pl.BlockSpec((N_DEV, r, cols), lambda i: (0, i, 0))],
                          out_specs=pl.BlockSpec((r, cols), lambda i: (i, 0)),
                          out_shape=jax.ShapeDtypeStruct((rows_total, cols), f32), compiler_params=_params())(x)


def _pick_rows(total, pref):
    best = None
    for d in range(16, min(total, pref) + 1, 16):
        if total % d == 0:
            best = d
    return best if best is not None else total


def adamw(w, g, m, v, name):
    shape = w.shape
    view = (1,) * (3 - w.ndim) + shape if w.ndim < 3 else (math.prod(shape[:-2]),) + shape[-2:]
    w2, g2, m2, v2 = (t.reshape(view) for t in (w, g, m, v))
    lead, rows_total, cols = view
    r = _pick_rows(rows_total, max(16, (512 * 1024) // max(cols, 1) // 16 * 16))
    c1, c2 = 1.0 / (1.0 - ADAM_B1 ** ADAM_STEP), 1.0 / (1.0 - ADAM_B2 ** ADAM_STEP)

    def body(w_ref, g_ref, m_ref, v_ref, d_ref, nm_ref, nv_ref):
        gv = g_ref[...]
        nm = ADAM_B1 * m_ref[...] + (1.0 - ADAM_B1) * gv
        nv = ADAM_B2 * v_ref[...] + (1.0 - ADAM_B2) * jnp.square(gv)
        d_ref[...] = -ADAM_LR * ((nm * c1) / (jnp.sqrt(nv * c2) + ADAM_EPS) + ADAM_WD * w_ref[...])
        nm_ref[...] = nm
        nv_ref[...] = nv

    padded = ((r + 7) // 8) * 8 * ((cols + LANE - 1) // LANE) * LANE * 4
    lb = max(d for d in range(1, lead + 1) if lead % d == 0 and d * padded <= max(padded, 1024 * 1024))
    spec = pl.BlockSpec((lb, r, cols), lambda l, i: (l, i, 0))
    outs = pl.pallas_call(body, name=name, grid=(lead // lb, rows_total // r), in_specs=[spec] * 4, out_specs=[spec] * 3,
                          out_shape=[jax.ShapeDtypeStruct(view, f32)] * 3, compiler_params=_params())(w2, g2, m2, v2)
    return tuple(o.reshape(shape) for o in outs)


def _layer_fwd(x, wl, fetch_rest, cos, sin, bsz, seq):
    n = x.shape[0]
    sv = {"x_in": x}
    row1 = lambda a: (a, (1, a.shape[1]), lambda j: (0, 0))
    h = rowmap(f_norm, [(x, D_MODEL, 0)], [row1(wl["attn_norm"])], [(D_MODEL, bf16)], 1, "norm_fwd")[0]
    u = mm(h, wl["w_in"], "nn", "mm_in", tn=2048)
    sv["h"], sv["u"] = h, u
    u3 = u.reshape(bsz, seq, U_PAD)
    wl = dict(wl)
    wl.update(fetch_rest(u))
    sv["wl"] = wl

    qkv = []
    for kind in range(3):
        cw = (wl["dn_conv_w"], (4, LANE), functools.partial(lambda j, kind: (0, 4 * kind + j), kind=kind))
        qkv.append(seqmap(functools.partial(f_dn_pre, kind), [(u3, U_QKV // LANE + 4 * kind)], [cw], 1, 4, "dn_pre%d" % kind)[0])
    gb = rowmap(f_dn_gates, [(u, 512, U_AB // 512)], [(wl["dn_gate_p"], (8, LANE), lambda j: (0, 0))], [(LANE, f32)], 1,
                "dn_gates", rows=512)[0]
    gb3 = gb.reshape(bsz, seq, LANE)
    crow = CHUNK * CHUNKS_PER_STEP
    dn_in = [(t.reshape(n, 512), 512, 0) for t in qkv] + [(gb, LANE, 0)]
    prep_a = rowmap(dn_prep, dn_in, [], [(512, f32)] + [(512, bf16)] * 4 + [(LANE, f32), (512, f32)], 1, "dn_prep", rows=crow)
    dn_in = dn_in + [(prep_a[6], 512, 0)]
    prep_a = [t.reshape(bsz, seq, t.shape[1]) for t in prep_a[:6]]
    o_a, ck_a = chunk_scan(dn_step, prep_a, (DN_HEADS, DN_DK, DN_DK), 512, "dn_scan")
    y_a = rowmap(per_head(f_dn_post), [(o_a.reshape(n, 512), 512, 0), (u, 512, U_Z // 512)],
                 [(wl["dn_norm_w"], (1, LANE), lambda j: (0, 0))], [(512, bf16)], 1, "dn_post")[0]
    sv.update(dn_in=dn_in, prep_a=prep_a, o_a=o_a, ck_a=ck_a, y_a=y_a)

    q_b, k_b = rowmap(f_ret_pre, [(u, 256, U_RQ // 256), (u, 256, U_RK // 256), (cos, 256, 0), (sin, 256, 0)], [],
                      [(256, f32), (256, f32)], 1, "ret_pre")
    q_b3, k_b3 = q_b.reshape(bsz, seq, 256), k_b.reshape(bsz, seq, 256)
    v_b3 = lax.slice_in_dim(u3, U_RV, U_RV + 512, axis=2)
    ret_in = [(q_b, 256, 0), (k_b, 256, 0), (u, 512, U_RV // 512)]
    inner = rowmap(ret_prep, ret_in, [], [(512, f32)], 1, "ret_prep", rows=crow)[0]
    ret_seq = [q_b3, k_b3, v_b3, inner.reshape(bsz, seq, 512)]
    o_b, ck_b = chunk_scan(ret_step, ret_seq, (RET_HEADS, 256, RET_DV), 512, "ret_scan")
    y_b = rowmap(per_head(f_ret_post), [(o_b.reshape(n, 512), 512, 0), (u, 512, U_RG // 512)], [], [(512, bf16)], 1,
                 "ret_post")[0]
    sv.update(ret_in=ret_in, ret_seq=ret_seq, o_b=o_b, ck_b=ck_b, y_b=y_b)

    lru_params = _lru_params(wl)
    a_c, b_c = seqmap(f_lru_pre, [(u3, U_CX // LANE)], lru_params, 2, 4, "lru_pre")
    h_c, hp_c = lru_scan(a_c, b_c)
    y_c = rowmap(f_lru_post, [(h_c.reshape(n, 512), 512, 0), (u, 512, U_CG // 512)], [], [(512, bf16)], 1, "lru_post")[0]
    sv.update(a_c=a_c, hp_c=hp_c, h_c=h_c, y_c=y_c)

    merged = branch_merge((y_a, y_b, y_c), wl["w_branch"], u)
    x_mid = mm(merged, wl["w_out"], "nn", "mm_out", add=x)
    sv.update(merged=merged, x_mid=x_mid)

    h2 = rowmap(f_norm, [(x_mid, D_MODEL, 0)], [row1(wl["ffn_norm"])], [(D_MODEL, bf16)], 1, "norm_fwd")[0]
    up = mm(h2, wl["w_up"], "nn", "mm_up", tn=2816)
    act = seqmap(f_ffn_mid, [(up.reshape(bsz, seq, 2 * D_FF), 0), (up.reshape(bsz, seq, 2 * D_FF), D_FF // LANE)],
                 _ffn_params(wl), 1, D_FF // LANE, "ffn_mid", out_dtype=bf16)[0]
    act = act.reshape(n, D_FF)
    x_out = mm(act, wl["w_down"], "nn", "mm_down", add=x_mid)
    sv.update(h2=h2, up=up, act=act)
    return x_out, sv


def _lru_params(wl):
    col = lambda a: (a, (a.shape[0], LANE), lambda j: (0, j))
    blk = lambda a: (a, (None, LANE, LANE), lambda j: (j, 0, 0))
    return [col(wl["lru_conv_w"]), col(wl["lru_conv_b"]), blk(wl["lru_wa"]), col(wl["lru_ba"]), blk(wl["lru_wx"]),
            col(wl["lru_bx"]), col(wl["lru_lambda"])]


def _ffn_params(wl):
    nb = D_FF // LANE
    return [(wl["ffn_conv_w"], (3, LANE), lambda j: (0, j)), (wl["ffn_conv_w"], (3, LANE), lambda j: (0, nb + j)),
            (wl["ffn_conv_b"], (1, LANE), lambda j: (0, j)), (wl["ffn_conv_b"], (1, LANE), lambda j: (0, nb + j))]


def _layer_bwd(dx, dx16, sv, cos, sin, bsz, seq, emit, dep):
    n = dx.shape[0]
    gr = {}
    wl = sv["wl"]
    u, x_in, x_mid = sv["u"], sv["x_in"], sv["x_mid"]
    u3 = u.reshape(bsz, seq, U_PAD)
    row1 = lambda a: (a, (1, a.shape[1]), lambda j: (0, 0))

    d_act = mm(dx16, wl["w_down"], "nt", "mm_down_dx", dep=dep)
    gr["w_down"] = mm(sv["act"], dx16, "tn", "mm_down_dw")
    up3 = sv["up"].reshape(bsz, seq, 2 * D_FF)
    (d_gate, d_val), dps = seqmap_bwd(f_ffn_mid, [(up3, 0), (up3, D_FF // LANE)], _ffn_params(wl),
                                      [d_act.reshape(bsz, seq, D_FF)], D_FF // LANE, "ffn_mid_bwd", din_dtype=bf16)
    gr["ffn_conv_w"] = jnp.concatenate([_cols(dps[0]), _cols(dps[1])], axis=1)
    gr["ffn_conv_b"] = jnp.concatenate([_cols(dps[2]), _cols(dps[3])], axis=1)[0]
    d_gate, d_val = d_gate.reshape(n, D_FF), d_val.reshape(n, D_FF)
    gr["w_up"] = (mm(sv["h2"], d_gate, "tn", "mm_up_dw"), mm(sv["h2"], d_val, "tn", "mm_up_dw"))
    token = emit("ffn", {k: gr[k] for k in ("w_up", "w_down")})
    d_h2 = mm(d_gate, wl["w_up"], "nt", "mm_up_dx", dep=token)
    d_h2 = mm(d_val, wl["w_up"], "nt", "mm_up_dx", add=d_h2, b_koff=1)
    (dx_mid,), (dg,), ex = rowmap_bwd(f_norm, [(x_mid, D_MODEL, 0)], [row1(wl["ffn_norm"])], [d_h2], 1, "norm_bwd", add=[dx],
                                      copy16=0)
    dx_mid16 = ex["copy16"]
    gr["ffn_norm"] = dg[0, 0]

    du = lax.empty((n, U_PAD), bf16)
    du3 = lambda: du.reshape(bsz, seq, U_PAD)

    d_merged = mm(dx_mid16, wl["w_out"], "nt", "mm_out_dx")
    gr["w_out"] = mm(sv["merged"], dx_mid16, "tn", "mm_out_dw")
    ys = (sv["y_a"], sv["y_b"], sv["y_c"])
    d_br, du = branch_merge_bwd(ys, wl["w_branch"], u, d_merged, du)
    d_ys = [mm(d_br[i], wl["w_branch"][i], "nt", "mm_branch_dx") for i in range(3)]
    gr["w_branch"] = jnp.stack([mm(ys[i], d_br[i], "tn", "mm_branch_dw") for i in range(3)])
    mix_token = emit("mix", {k: gr[k] for k in ("w_branch", "w_out")})

    (d_hc, _), _, ex = rowmap_bwd(f_lru_post, [(sv["h_c"].reshape(n, 512), 512, 0), (u, 512, U_CG // 512)], [], [d_ys[2]], 1,
                                  "lru_post_bwd", into=(du, U_CG // 512, [1]))
    du = ex["into"]
    d_a, d_b = lru_scan_bwd(sv["a_c"], sv["hp_c"], d_hc.reshape(bsz, seq, 512))
    (du_new,), dps = seqmap_bwd(f_lru_pre, [(u3, U_CX // LANE)], _lru_params(wl), [d_a, d_b], 4, "lru_pre_bwd", din_dtype=bf16,
                                into=(du3(), U_CX // LANE))
    du = du_new.reshape(n, U_PAD)
    gr["lru_conv_w"], gr["lru_conv_b"] = _cols(dps[0]), _cols(dps[1])[0]
    gr["lru_wa"], gr["lru_ba"], gr["lru_wx"], gr["lru_bx"] = dps[2], dps[3][:, 0], dps[4], dps[5][:, 0]
    gr["lru_lambda"] = _cols(dps[6])[0]

    (d_ob, _), _, ex = rowmap_bwd(per_head(f_ret_post), [(sv["o_b"].reshape(n, 512), 512, 0), (u, 512, U_RG // 512)], [],
                                  [d_ys[1]], 1, "ret_post_bwd", into=(du, U_RG // 512, [1]))
    du = ex["into"]
    crow = CHUNK * CHUNKS_PER_STEP
    d_ret = chunk_scan_bwd(ret_step, sv["ret_seq"], sv["ck_b"], d_ob.reshape(bsz, seq, 512), "ret_scan_bwd")
    d_ret = [t.reshape(n, t.shape[2]) for t in d_ret]
    (d_qb, d_kb, _), _, ex = rowmap_bwd(ret_prep, sv["ret_in"], [], [d_ret[3]], 1, "ret_prep_bwd", rows=crow, add=d_ret[:3],
                                        into=(du, U_RV // 512, [2]))
    du = ex["into"]
    _, _, ex = rowmap_bwd(f_ret_pre, [(u, 256, U_RQ // 256), (u, 256, U_RK // 256), (cos, 256, 0), (sin, 256, 0)], [],
                          [d_qb, d_kb], 1, "ret_pre_bwd", din_dtypes=[f32, f32, None, None], into=(du, U_RQ // 512, [0, 1]))
    du = ex["into"]

    (d_oa, _), (dnw,), ex = rowmap_bwd(per_head(f_dn_post), [(sv["o_a"].reshape(n, 512), 512, 0), (u, 512, U_Z // 512)],
                                       [(wl["dn_norm_w"], (1, LANE), lambda j: (0, 0))], [d_ys[0]], 1, "dn_post_bwd",
                                       into=(du, U_Z // 512, [1]))
    du = ex["into"]
    gr["dn_norm_w"] = dnw[0, 0]
    d_prep = chunk_scan_bwd(dn_step, sv["prep_a"], sv["ck_a"], d_oa.reshape(bsz, seq, 512), "dn_scan_bwd")
    (d_q, d_k, d_v, d_gb, _), _, _ = rowmap_bwd(dn_prep, sv["dn_in"], [], [t.reshape(n, t.shape[2]) for t in d_prep], 1,
                                                "dn_prep_bwd", rows=crow, din_dtypes=[f32] * 4 + [None])
    d_q, d_k, d_v = (t.reshape(bsz, seq, 512) for t in (d_q, d_k, d_v))
    _, (dgp,), ex = rowmap_bwd(f_dn_gates, [(u, 512, U_AB // 512)], [(wl["dn_gate_p"], (8, LANE), lambda j: (0, 0))],
                               [d_gb], 1, "dn_gates_bwd", rows=512, into=(du, U_AB // 512, [0]))
    du = ex["into"]
    gr["dn_a_log"], gr["dn_dt_bias"] = dgp[0, 0, :DN_HEADS], dgp[0, 1, :DN_HEADS]
    d_cw = []
    for kind, d_t in enumerate((d_q, d_k, d_v)):
        cw = (wl["dn_conv_w"], (4, LANE), functools.partial(lambda j, kind: (0, 4 * kind + j), kind=kind))
        (du_new,), (dcw,) = seqmap_bwd(functools.partial(f_dn_pre, kind), [(u3, U_QKV // LANE + 4 * kind)], [cw], [d_t], 4,
                                       "dn_pre%d_bwd" % kind, din_dtype=bf16, into=(du3(), U_QKV // LANE + 4 * kind))
        du = du_new.reshape(n, U_PAD)
        d_cw.append(_cols(dcw))
    gr["dn_conv_w"] = jnp.concatenate(d_cw, axis=1)

    gr["w_in"] = _unpad_w_in(mm(sv["h"], du, "tn", "mm_in_dw", dep=mix_token, tn=2048))
    token = emit("in", {"w_in": gr["w_in"]})
    d_h = mm(du, wl["w_in"], "nt", "mm_in_dx", dep=token)
    (dx_in,), (dg,), ex = rowmap_bwd(f_norm, [(x_in, D_MODEL, 0)], [row1(wl["attn_norm"])], [d_h], 1, "norm_bwd", add=[dx_mid],
                                     copy16=0)
    gr["attn_norm"] = dg[0, 0]
    big = ("w_in", "w_branch", "w_out", "w_up", "w_down")
    return dx_in, ex["copy16"], emit("small", {k: g for k, g in gr.items() if k not in big})


def _cols(dp):
    ncol, p, _ = dp.shape
    return jnp.transpose(dp, (1, 0, 2)).reshape(p, ncol * LANE)


def _pad_w_in(w):
    segs = sorted(_IN_SEGS, key=lambda s: s[2])
    parts = [lax.slice_in_dim(w, src, src + width, axis=1) for src, width, _ in segs]
    end = segs[-1][2] + segs[-1][1]
    return jnp.concatenate(parts + [jnp.zeros((w.shape[0], U_PAD - end), w.dtype)], axis=1)


def _unpad_w_in(wp):
    return jnp.concatenate([lax.slice_in_dim(wp, dst, dst + width, axis=1) for _, width, dst in _IN_SEGS], axis=1)


def _rope_tables(positions):
    half = RET_DK // 2
    inv = ROPE_BASE ** (-jnp.arange(half, dtype=f32) / half)
    ang = positions.astype(f32).reshape(-1, 1) * inv
    cos, sin = jnp.cos(ang), jnp.sin(ang)
    return jnp.tile(cos, (1, 2 * RET_HEADS)), jnp.tile(sin, (1, 2 * RET_HEADS))


def _layer_weights(lw):
    wl = {}
    wl["w_in"] = _pad_w_in(lw["w_in"])
    for k in ("dn_conv_w", "lru_conv_w", "ffn_conv_w", "lru_wa", "lru_wx"):
        wl[k] = lw[k]
    for k in ("attn_norm", "ffn_norm", "dn_norm_w", "lru_conv_b", "lru_lambda", "ffn_conv_b", "lru_ba", "lru_bx"):
        wl[k] = lw[k].reshape(1, -1)
    gp = jnp.zeros((8, LANE), f32)
    wl["dn_gate_p"] = gp.at[0, :DN_HEADS].set(lw["dn_a_log"]).at[1, :DN_HEADS].set(lw["dn_dt_bias"])
    return wl


REST = ("w_branch", "w_out", "w_up", "w_down")


def forward_backward(x, positions, target, layer_weights, final_norm, on_head, on_grads):
    bsz, seq, d = x.shape
    n = bsz * seq
    cos, sin = _rope_tables(positions)
    xs = x.reshape(n, d)
    saved = []
    for layer in range(DEPTH):
        first, fetch_rest = layer_weights(layer, xs)
        xs, sv = _layer_fwd(xs, _layer_weights(first), fetch_rest, cos, sin, bsz, seq)
        saved.append(sv)
    loss, dx, d_final, dx16 = final_loss(xs, final_norm.reshape(1, d), target.reshape(n, d))
    on_head(loss[0, 0], d_final[0])
    token = None
    for layer in reversed(range(DEPTH)):
        dx, dx16, token = _layer_bwd(dx, dx16, saved[layer], cos, sin, bsz, seq, functools.partial(on_grads, layer), token)
    return dx.reshape(bsz, seq, d)


BIG = (("w_in", 2), ("w_branch", 3), ("w_out", 1), ("w_up", 2), ("w_down", 1))
SMALL_SHARDED = (("dn_conv_w", 2), ("lru_conv_w", 2), ("ffn_conv_w", 2))
REPLICATED = ("attn_norm", "dn_a_log", "dn_dt_bias", "dn_norm_w", "lru_conv_b", "lru_wa", "lru_ba", "lru_wx", "lru_bx",
              "lru_lambda", "ffn_norm", "ffn_conv_b", "final_norm")
WEIGHTS = ("attn_norm", "w_in", "dn_conv_w", "dn_a_log", "dn_dt_bias", "dn_norm_w", "lru_conv_w", "lru_conv_b", "lru_wa",
           "lru_ba", "lru_wx", "lru_bx", "lru_lambda", "w_branch", "w_out", "ffn_norm", "w_up", "ffn_conv_w", "ffn_conv_b",
           "w_down", "final_norm")


def _pack(arrs, dtype, align=16 * LANE):
    flat = jnp.concatenate([a.reshape(-1).astype(dtype) for a in arrs])
    pad = (-flat.shape[0]) % align
    return jnp.pad(flat, (0, pad)).reshape(-1, LANE)


def _unpack(rows, shapes):
    flat = rows.reshape(-1)
    out, pos = [], 0
    for shp in shapes:
        size = math.prod(shp)
        out.append(lax.slice_in_dim(flat, pos, pos + size).reshape(shp))
        pos += size
    return out


def kernel(x, positions, attn_norm, w_in, dn_conv_w, dn_a_log, dn_dt_bias, dn_norm_w, lru_conv_w, lru_conv_b, lru_wa, lru_ba, lru_wx, lru_bx, lru_lambda, w_branch, w_out, ffn_norm, w_up, ffn_conv_w, ffn_conv_b, w_down, final_norm, loss_target, m_attn_norm, m_w_in, m_dn_conv_w, m_dn_a_log, m_dn_dt_bias, m_dn_norm_w, m_lru_conv_w, m_lru_conv_b, m_lru_wa, m_lru_ba, m_lru_wx, m_lru_bx, m_lru_lambda, m_w_branch, m_w_out, m_ffn_norm, m_w_up, m_ffn_conv_w, m_ffn_conv_b, m_w_down, m_final_norm, v_attn_norm, v_w_in, v_dn_conv_w, v_dn_a_log, v_dn_dt_bias, v_dn_norm_w, v_lru_conv_w, v_lru_conv_b, v_lru_wa, v_lru_ba, v_lru_wx, v_lru_bx, v_lru_lambda, v_w_branch, v_w_out, v_ffn_norm, v_w_up, v_ffn_conv_w, v_ffn_conv_b, v_w_down, v_final_norm):
    w = dict(attn_norm=attn_norm, w_in=w_in, dn_conv_w=dn_conv_w, dn_a_log=dn_a_log, dn_dt_bias=dn_dt_bias, dn_norm_w=dn_norm_w,
             lru_conv_w=lru_conv_w, lru_conv_b=lru_conv_b, lru_wa=lru_wa, lru_ba=lru_ba, lru_wx=lru_wx, lru_bx=lru_bx,
             lru_lambda=lru_lambda, w_branch=w_branch, w_out=w_out, ffn_norm=ffn_norm, w_up=w_up, ffn_conv_w=ffn_conv_w,
             ffn_conv_b=ffn_conv_b, w_down=w_down, final_norm=final_norm)
    m = dict(attn_norm=m_attn_norm, w_in=m_w_in, dn_conv_w=m_dn_conv_w, dn_a_log=m_dn_a_log, dn_dt_bias=m_dn_dt_bias,
             dn_norm_w=m_dn_norm_w, lru_conv_w=m_lru_conv_w, lru_conv_b=m_lru_conv_b, lru_wa=m_lru_wa, lru_ba=m_lru_ba,
             lru_wx=m_lru_wx, lru_bx=m_lru_bx, lru_lambda=m_lru_lambda, w_branch=m_w_branch, w_out=m_w_out, ffn_norm=m_ffn_norm,
             w_up=m_w_up, ffn_conv_w=m_ffn_conv_w, ffn_conv_b=m_ffn_conv_b, w_down=m_w_down, final_norm=m_final_norm)
    v = dict(attn_norm=v_attn_norm, w_in=v_w_in, dn_conv_w=v_dn_conv_w, dn_a_log=v_dn_a_log, dn_dt_bias=v_dn_dt_bias,
             dn_norm_w=v_dn_norm_w, lru_conv_w=v_lru_conv_w, lru_conv_b=v_lru_conv_b, lru_wa=v_lru_wa, lru_ba=v_lru_ba,
             lru_wx=v_lru_wx, lru_bx=v_lru_bx, lru_lambda=v_lru_lambda, w_branch=v_w_branch, w_out=v_w_out, ffn_norm=v_ffn_norm,
             w_up=v_w_up, ffn_conv_w=v_ffn_conv_w, ffn_conv_b=v_ffn_conv_b, w_down=v_w_down, final_norm=v_final_norm)

    me = 4 * lax.axis_index("x") + 2 * lax.axis_index("y") + lax.axis_index("c")
    axes = dict(BIG + SMALL_SHARDED)
    conv_names = [k for k, _ in SMALL_SHARDED]

    gathers, tokens, conv_full = {}, [], {}
    for layer in range(DEPTH):
        first = [w["w_in"][layer].astype(bf16)] + ([w[k] for k in conv_names] if layer == 0 else [])
        rest = [w[k][layer].astype(bf16) for k in REST]
        for part, srcs in (("in", first), ("rest", rest)):
            gathers[layer, part], token = exchange_start(srcs, [False] * len(srcs), "gather_%s_start%d" % (part, layer))
            tokens.append(token[0:1, 0:1])
    all_started = functools.reduce(lambda a, b: a + b, tokens)

    def join(land, axis):
        if axis == 0:
            return land.reshape((N_DEV * land.shape[1],) + land.shape[2:])
        return jnp.concatenate([land[p] for p in range(N_DEV)], axis=axis)

    def split(g, axis):
        if isinstance(g, tuple):
            each = N_DEV // len(g)
            size = g[0].shape[axis] // each
            return jnp.stack([lax.slice_in_dim(piece, p * size, (p + 1) * size, axis=axis) for piece in g for p in range(each)])
        size = g.shape[axis] // N_DEV
        if axis == 0:
            return g.reshape((N_DEV, size) + g.shape[1:])
        return jnp.stack([lax.slice_in_dim(g, p * size, (p + 1) * size, axis=axis) for p in range(N_DEV)])

    def layer_weights(layer, x_in):
        lands = exchange_wait(gathers[layer, "in"], x_in, "gather_in_wait%d" % layer)
        lw = {"w_in": join(lands[0], 1)}
        if layer == 0:
            conv_full.update({k: join(lands[1 + i], axes[k]) for i, k in enumerate(conv_names)})
        lw.update({k: conv_full[k][layer] for k in conv_names})
        lw.update({k: w[k][layer] for k in REPLICATED if k != "final_norm"})
        if layer == 0:
            lw["attn_norm"] = lw["attn_norm"] + all_started[0]

        def fetch_rest(after):
            lands_r = exchange_wait(gathers[layer, "rest"], after, "gather_rest_wait%d" % layer)
            return {k: join(lands_r[i], axes[k] - 1) for i, k in enumerate(REST)}

        return lw, fetch_rest

    small_names = conv_names + [k for k in REPLICATED if k != "final_norm"]
    groups = {"ffn": ("w_up", "w_down"), "mix": ("w_branch", "w_out"), "in": ("w_in",)}
    scatters, small_shapes, head = {}, {}, {}

    def on_grads(layer, group, gr):
        if group == "small":
            small_shapes.update({k: gr[k].shape for k in small_names})
            srcs = [_pack([gr[k] for k in small_names], f32)]
            srcs += [_pack([head["loss"].reshape(1), head["d_final"]], f32)] if layer == DEPTH - 1 else []
            modes = [False] * len(srcs)
        else:
            srcs = [split(gr[k], axes[k] - 1).astype(bf16) for k in groups[group]]
            modes = [True] * len(srcs)
        scatters[layer, group], token = exchange_start(srcs, modes, "scatter_%s_start%d" % (group, layer))
        return token

    grad_x = forward_backward(x, positions, loss_target, layer_weights, final_norm,
                              lambda loss_part, d_final: head.update(loss=loss_part, d_final=d_final), on_grads)

    big_sums, small_sums = {}, {}
    for group in ("ffn", "mix", "in"):
        for layer in reversed(range(DEPTH)):
            lands = exchange_wait(scatters[layer, group], grad_x, "scatter_%s_wait%d" % (group, layer))
            for i, k in enumerate(groups[group]):
                shard = w[k].shape[1:]
                big_sums[layer, k] = sum_slots(lands[i].reshape(N_DEV, -1, shard[-1]), "sum_" + k).reshape(shard)
    for layer in reversed(range(DEPTH)):
        lands = exchange_wait(scatters[layer, "small"], grad_x, "scatter_small_wait%d" % layer)
        small_sums[layer] = sum_slots(lands[0], "sum_small")
        if layer == DEPTH - 1:
            head_sum = _unpack(sum_slots(lands[1], "sum_head"), [(1,), final_norm.shape])
    grads = {k: jnp.stack([big_sums[layer, k] for layer in range(DEPTH)]) for k, _ in BIG}
    loss, grads["final_norm"] = head_sum[0][0], head_sum[1]
    small_flat = jnp.stack([small_sums[layer] for layer in range(DEPTH)]).reshape(DEPTH, -1)
    pos = 0
    for k in small_names:
        size = math.prod(small_shapes[k])
        g = lax.slice_in_dim(small_flat, pos, pos + size, axis=1).reshape((DEPTH,) + small_shapes[k])
        pos += size
        ax = dict(SMALL_SHARDED).get(k)
        if ax is None:
            grads[k] = g
        else:
            size = g.shape[ax] // N_DEV
            grads[k] = lax.dynamic_slice_in_dim(g, me * size, size, axis=ax)

    inner_last = {"w_in": (2, 0, 1), "w_up": (0, 2, 1)}
    upd = {}
    for k in WEIGHTS:
        if k in inner_last:
            perm = inner_last[k]
            back = tuple(perm.index(i) for i in range(3))
            g_t = jnp.transpose(grads[k], perm)
            res = adamw(jnp.transpose(w[k], perm), g_t, jnp.transpose(m[k], perm), jnp.transpose(v[k], perm), "adamw_" + k)
            upd[k] = tuple(jnp.transpose(t, back) for t in res)
            grads[k] = jnp.transpose(g_t, back)
        else:
            upd[k] = adamw(w[k], grads[k], m[k], v[k], "adamw_" + k)
    return (loss, grad_x, *[grads[k] for k in WEIGHTS], *[upd[k][0] for k in WEIGHTS], *[upd[k][1] for k in WEIGHTS],
            *[upd[k][2] for k in WEIGHTS])
```

```python
import functools
import math

import jax
import jax.numpy as jnp
from jax import lax
from jax.experimental import pallas as pl
from jax.experimental.pallas import tpu as pltpu

f32 = jnp.float32
bf16 = jnp.bfloat16

D_MODEL = 1024
DEPTH = 4
CHUNK = 64
EPS = 1e-6
DN_HEADS, DN_DK = 4, 128
RET_HEADS, RET_DK, RET_DV = 4, 64, 128
ROPE_BASE = 10000.0
LRU_C = 8.0
D_FF = 2816
N_DEV = 8
LANE = 128
VMEM_LIMIT = 56 * 1024 * 1024

ADAM_LR, ADAM_B1, ADAM_B2, ADAM_EPS, ADAM_WD, ADAM_STEP = 0.001, 0.9, 0.999, 1e-8, 0.01, 10

U_GATES, U_QKV, U_RV, U_RG, U_Z, U_CX, U_CG, U_RQ, U_RK, U_AB = (
    0, 3072, 4608, 5120, 5632, 6144, 6656, 7168, 7424, 7680)
U_PAD = 8192
_IN_SEGS = ((0, 1536, U_QKV), (1536, 8, U_AB), (1544, 512, U_Z), (2056, 256, U_RQ), (2312, 256, U_RK),
            (2568, 512, U_RV), (3080, 512, U_RG), (3592, 512, U_CX), (4104, 512, U_CG), (4616, 3072, U_GATES))


def _params():
    return pltpu.CompilerParams(vmem_limit_bytes=VMEM_LIMIT)


def _pick(dim, pref):
    best = None
    for d in range(LANE, min(dim, pref) + 1, LANE):
        if dim % d == 0:
            best = d
    return best if best is not None else dim


@functools.partial(jax.custom_vjp, nondiff_argnums=(1, 2))
def sroll(x, shift, axis):
    return pltpu.roll(x, shift, axis)


def _sroll_fwd(x, shift, axis):
    return pltpu.roll(x, shift, axis), None


def _sroll_bwd(shift, axis, _, g):
    n = g.shape[axis]
    return (pltpu.roll(g, (n - shift) % n, axis),)


sroll.defvjp(_sroll_fwd, _sroll_bwd)

_DIMS = {"nn": (((1,), (0,)), ((), ())), "nt": (((1,), (1,)), ((), ())), "tn": (((0,), (0,)), ((), ()))}


def _dg(a, b, dims):
    return lax.dot_general(a.astype(bf16), b.astype(bf16), _DIMS[dims], preferred_element_type=f32)


@functools.partial(jax.custom_vjp, nondiff_argnums=(2,))
def bdot(a, b, dims):
    return _dg(a, b, dims)


def _bdot_fwd(a, b, dims):
    return _dg(a, b, dims), (a.astype(bf16), b.astype(bf16))


def _bdot_bwd(dims, res, g):
    a, b = res
    if dims == "nn":
        return _dg(g, b, "nt"), _dg(a, g, "tn")
    if dims == "nt":
        return _dg(g, b, "nn"), _dg(g, a, "tn")
    return _dg(b, g, "nt"), _dg(a, g, "nn")


bdot.defvjp(_bdot_fwd, _bdot_bwd)


def _fdot(a, b, dims):
    return lax.dot_general(a, b, _DIMS[dims], precision=lax.Precision.HIGH, preferred_element_type=f32)


@jax.custom_vjp
def unit_lower_inv_all(mats):
    shape = mats[0].shape
    row = lax.broadcasted_iota(jnp.int32, shape, 0)
    col = lax.broadcasted_iota(jnp.int32, shape, 1)
    eye = jnp.where(row == col, 1.0, 0.0).astype(f32)
    n = [-a for a in mats]
    p = [eye + x for x in n]
    span = 2
    while span < shape[0]:
        n = [_fdot(x, x, "nn") for x in n]
        p = [y + _fdot(y, x, "nn") for y, x in zip(p, n)]
        span *= 2
    return p


def _uli_fwd(mats):
    x = unit_lower_inv_all(mats)
    return x, x


def _uli_bwd(xs, gs):
    t = [_fdot(x, g, "tn") for x, g in zip(xs, gs)]
    return ([-_fdot(y, x, "nt") for y, x in zip(t, xs)],)


unit_lower_inv_all.defvjp(_uli_fwd, _uli_bwd)


@jax.custom_vjp
def known_inverse(invs, mats):
    return invs


def _known_fwd(invs, mats):
    return invs, invs


def _known_bwd(xs, gs):
    return [jnp.zeros_like(x) for x in xs], _uli_bwd(xs, gs)[0]


known_inverse.defvjp(_known_fwd, _known_bwd)


def cumsum_rows(x):
    rows = x.shape[0]
    row = lax.broadcasted_iota(jnp.int32, x.shape, 0)
    s = 1
    while s < rows:
        x = x + jnp.where(row >= s, sroll(x, s, 0), 0.0)
        s *= 2
    return x


def _expm1(x):
    return jnp.tanh(0.5 * x) * (jnp.exp(x) + 1.0)


def _lane_pick(x, lane):
    idx = lax.broadcasted_iota(jnp.int32, x.shape, 1)
    return jnp.sum(jnp.where(idx == lane, x, 0.0), axis=1, keepdims=True)


def _row_pick(x, r):
    idx = lax.broadcasted_iota(jnp.int32, x.shape, 0)
    return jnp.sum(jnp.where(idx == r, x, 0.0), axis=0, keepdims=True)


def _causal_conv(x, halo, w, width):
    if halo is None:
        row = lax.broadcasted_iota(jnp.int32, x.shape, 0)
        acc = x * w[width - 1:width]
        for k in range(width - 1):
            shift = width - 1 - k
            acc = acc + jnp.where(row >= shift, sroll(x, shift, 0), 0.0) * w[k:k + 1]
        return acc
    xe = jnp.concatenate([halo, x], axis=0)
    acc = xe * w[width - 1:width]
    for k in range(width - 1):
        acc = acc + sroll(xe, width - 1 - k, 0) * w[k:k + 1]
    return acc[8:]


def f_norm(ins, ps):
    (x,), (g,) = ins, ps
    return [x * lax.rsqrt(jnp.mean(x * x, axis=-1, keepdims=True) + EPS) * g]


def f_dn_pre(kind, mains, halos, ps):
    y = _causal_conv(mains[0], halos[0], ps[0], 4)
    y = y * jax.nn.sigmoid(y)
    if kind < 2:
        y = y * lax.rsqrt(jnp.sum(y * y, axis=-1, keepdims=True) + EPS)
    if kind == 0:
        y = y * (DN_DK ** -0.5)
    return [y]


def f_dn_gates(ins, ps):
    u, p = ins[0][:, :LANE], ps[0]
    lane = lax.broadcasted_iota(jnp.int32, u.shape, 1)
    g = -jnp.exp(p[0:1]) * jax.nn.softplus(u + p[1:2])
    beta = jax.nn.sigmoid(u)
    return [jnp.where(lane < 4, g, jnp.where(lane < 8, beta, 0.0))]


def per_head(fn):
    def tile_fn(vals, ps):
        heads = [fn([v[:, h * LANE:(h + 1) * LANE] for v in vals], ps) for h in range(vals[0].shape[1] // LANE)]
        return [jnp.concatenate([o[i] for o in heads], axis=1) for i in range(len(heads[0]))]
    return tile_fn


def f_dn_post(ins, ps):
    (o, z), (nw,) = ins, ps
    y = o * lax.rsqrt(jnp.mean(o * o, axis=-1, keepdims=True) + EPS) * nw
    return [y * (z * jax.nn.sigmoid(z))]


def _rot_half(t):
    lane = lax.broadcasted_iota(jnp.int32, t.shape, 1)
    width = t.shape[1]
    first = (lane % RET_DK) < (RET_DK // 2)
    return jnp.where(first, -sroll(t, width - RET_DK // 2, 1), sroll(t, RET_DK // 2, 1))


def f_ret_pre(ins, ps):
    q, k, cos, sin = ins
    qr = q * cos + _rot_half(q) * sin
    kr = (k * cos + _rot_half(k) * sin) * (RET_DK ** -0.5)
    return [qr, kr]


def f_ret_post(ins, ps):
    o, g = ins
    mu = jnp.mean(o, axis=-1, keepdims=True)
    var = jnp.mean(jnp.square(o - mu), axis=-1, keepdims=True)
    return [(o - mu) * lax.rsqrt(var + EPS) * (g * jax.nn.sigmoid(g))]


def f_lru_pre(mains, halos, ps):
    cw, cb, wa, ba, wx, bx, lam = ps
    xc = _causal_conv(mains[0], halos[0], cw, 4) + cb
    r = jax.nn.sigmoid(bdot(xc, wa, "nn") + ba)
    i = jax.nn.sigmoid(bdot(xc, wx, "nn") + bx)
    log_a = -LRU_C * r * jax.nn.softplus(-lam)
    a = jnp.exp(log_a)
    b = jnp.sqrt(-_expm1(2.0 * log_a)) * (i * xc)
    return [a, b]


def f_lru_post(ins, ps):
    h, g = ins
    return [h * jax.nn.gelu(g)]


def f_ffn_mid(mains, halos, ps):
    cwg, cwv, cbg, cbv = ps
    gate = _causal_conv(mains[0], halos[0], cwg, 3) + cbg
    val = _causal_conv(mains[1], halos[1], cwv, 3) + cbv
    return [gate * jax.nn.sigmoid(gate) * val]


def mm(a, b, dims, name, add=None, dep=None, b_koff=0, tm=1536, tn=1536, tk=2816):
    if dims == "tn":
        kdim, m = a.shape
        n = b.shape[1]
    else:
        m, kdim = a.shape
        n = b.shape[0] if dims == "nt" else b.shape[1]
    tm, tn, tk = _pick(m, tm), _pick(n, tn), _pick(kdim, tk)
    nk = kdim // tk
    a_spec = pl.BlockSpec((tk, tm), lambda i, j, k: (k, i)) if dims == "tn" else pl.BlockSpec((tm, tk), lambda i, j, k: (i, k))
    b_spec = (pl.BlockSpec((tn, tk), lambda i, j, k: (j, k + b_koff * nk)) if dims == "nt"
              else pl.BlockSpec((tk, tn), lambda i, j, k: (k, j)))
    o_spec = pl.BlockSpec((tm, tn), lambda i, j, k: (i, j))
    has_add, has_dep = add is not None, dep is not None

    def body(*refs):
        a_ref, b_ref = refs[:2]
        add_ref = refs[2] if has_add else None
        o_ref = refs[2 + has_add + has_dep]
        if nk == 1:
            prod = _dg(a_ref[...], b_ref[...], dims)
            o_ref[...] = prod + add_ref[...] if has_add else prod
            return
        acc_ref = refs[-1]
        k = pl.program_id(2)

        @pl.when(k == 0)
        def _():
            acc_ref[...] = jnp.zeros_like(acc_ref)

        acc_ref[...] += _dg(a_ref[...], b_ref[...], dims)

        @pl.when(k == nk - 1)
        def _():
            o_ref[...] = acc_ref[...] + add_ref[...] if has_add else acc_ref[...]

    args = [a, b] + ([add] if has_add else []) + ([dep] if has_dep else [])
    in_specs = [a_spec, b_spec] + ([o_spec] if has_add else [])
    in_specs += [pl.BlockSpec((8, LANE), lambda i, j, k: (0, 0))] if has_dep else []
    return pl.pallas_call(
        body, name=name, grid=(m // tm, n // tn, nk), in_specs=in_specs, out_specs=o_spec,
        out_shape=jax.ShapeDtypeStruct((m, n), f32), scratch_shapes=[pltpu.VMEM((tm, tn), f32)] if nk > 1 else [],
        compiler_params=_params())(*args)


def rowmap(fn, ins, params, outs, ncol, name, rows=512):
    n = ins[0][0].shape[0]
    r = min(rows, n)
    nin, npar = len(ins), len(params)

    def body(*refs):
        vals = [x[...] for x in refs[:nin]]
        pv = [p[...] for p in refs[nin:nin + npar]]
        for o_ref, o in zip(refs[nin + npar:], fn(vals, pv)):
            o_ref[...] = o.astype(o_ref.dtype)

    in_specs = [pl.BlockSpec((r, cb), functools.partial(lambda j, i, off: (i, off + j), off=off)) for _, cb, off in ins]
    in_specs += [pl.BlockSpec(bs, functools.partial(lambda j, i, f: f(j), f=f)) for _, bs, f in params]
    out_specs = [pl.BlockSpec((r, cb), lambda j, i: (i, j)) for cb, _ in outs]
    out_shape = [jax.ShapeDtypeStruct((n, cb * ncol), dt) for cb, dt in outs]
    res = pl.pallas_call(body, name=name, grid=(ncol, n // r), in_specs=in_specs, out_specs=out_specs,
                         out_shape=out_shape, compiler_params=_params())(*[a for a, _, _ in ins], *[a for a, _, _ in params])
    return res


def rowmap_bwd(fn, ins, params, douts, ncol, name, rows=512, add=None, din_dtypes=None, into=None, copy16=None):
    n = ins[0][0].shape[0]
    r = min(rows, n)
    nin, npar, nout = len(ins), len(params), len(douts)
    add = [None] * nin if add is None else list(add)
    add_idx = [i for i in range(nin) if add[i] is not None]
    din_dtypes = [f32] * nin if din_dtypes is None else list(din_dtypes)
    into_buf, into_off, into_idx = into if into is not None else (None, 0, [])
    has_into, has_copy = into is not None, copy16 is not None
    kept = [i for i in range(nin) if din_dtypes[i] is not None and i not in into_idx]

    def body(*refs):
        vals = [x[...] for x in refs[:nin]]
        pv = [p[...] for p in refs[nin:nin + npar]]
        dys = [d[...] for d in refs[nin + npar:nin + npar + nout]]
        k0 = nin + npar + nout
        add_refs = dict(zip(add_idx, refs[k0:k0 + len(add_idx)]))
        k0 += len(add_idx) + has_into
        din_refs = refs[k0:k0 + len(kept)]
        k0 += len(kept)
        copy_ref = refs[k0] if has_copy else None
        into_ref = refs[k0 + has_copy] if has_into else None
        dp_refs = refs[k0 + has_copy + has_into:]
        _, vjp = jax.vjp(fn, vals, pv)
        dvals, dpv = vjp(dys)
        cot = lambda idx: dvals[idx] + add_refs[idx][...] if idx in add_refs else dvals[idx]
        for d_ref, idx in zip(din_refs, kept):
            d_ref[...] = cot(idx).astype(d_ref.dtype)
        if has_copy:
            copy_ref[...] = cot(copy16).astype(copy_ref.dtype)
        if has_into:
            parts = [cot(idx) for idx in into_idx]
            into_ref[...] = (parts[0] if len(parts) == 1 else jnp.concatenate(parts, axis=1)).astype(into_ref.dtype)

        @pl.when(pl.program_id(1) == 0)
        def _():
            for d_ref in dp_refs:
                d_ref[...] = jnp.zeros_like(d_ref)

        for d_ref, d in zip(dp_refs, dpv):
            d_ref[...] += d

    in_specs = [pl.BlockSpec((r, cb), functools.partial(lambda j, i, off: (i, off + j), off=off)) for _, cb, off in ins]
    in_specs += [pl.BlockSpec(bs, functools.partial(lambda j, i, f: f(j), f=f)) for _, bs, f in params]
    in_specs += [pl.BlockSpec((r, d.shape[1] // ncol), lambda j, i: (i, j)) for d in douts]
    in_specs += [pl.BlockSpec((r, ins[i][1]), lambda j, i: (i, j)) for i in add_idx]
    out_specs = [pl.BlockSpec((r, ins[i][1]), lambda j, i: (i, j)) for i in kept]
    out_shape = [jax.ShapeDtypeStruct((n, ins[i][1] * ncol), din_dtypes[i]) for i in kept]
    args = [a for a, _, _ in ins] + [a for a, _, _ in params] + list(douts) + [add[i] for i in add_idx]
    aliases = {}
    if has_copy:
        out_specs += [pl.BlockSpec((r, ins[copy16][1]), lambda j, i: (i, j))]
        out_shape += [jax.ShapeDtypeStruct((n, ins[copy16][1] * ncol), bf16)]
    if has_into:
        assert ncol == 1
        in_specs += [pl.BlockSpec(memory_space=pl.ANY)]
        aliases[len(args)] = len(out_shape)
        args += [into_buf]
        out_specs += [pl.BlockSpec((r, sum(ins[i][1] for i in into_idx)), lambda j, i: (i, into_off))]
        out_shape += [jax.ShapeDtypeStruct(into_buf.shape, into_buf.dtype)]
    pshapes = [tuple(d for d in bs if d is not None) for _, bs, _ in params]
    out_specs += [pl.BlockSpec((None,) + ps, functools.partial(lambda j, i, nd: (j,) + (0,) * nd, nd=len(ps))) for ps in pshapes]
    out_shape += [jax.ShapeDtypeStruct((ncol,) + ps, f32) for ps in pshapes]
    res = pl.pallas_call(body, name=name, grid=(ncol, n // r), in_specs=in_specs, out_specs=out_specs, out_shape=out_shape,
                         input_output_aliases=aliases, compiler_params=_params())(*args)
    dins = [None] * nin
    for pos, i in enumerate(kept):
        dins[i] = res[pos]
    pos = len(kept)
    extras = {}
    if has_copy:
        extras["copy16"] = res[pos]
        pos += 1
    if has_into:
        extras["into"] = res[pos]
        pos += 1
    return dins, res[pos:], extras


SEQ_ROWS = 2048


def seqmap(fn, ins, params, nouts, ncol, name, out_dtype=f32):
    bsz, seq, _ = ins[0][0].shape
    r = min(SEQ_ROWS, seq)
    nin, npar = len(ins), len(params)

    def body(*refs):
        in_refs = refs[:nin]
        pv = [p[...] for p in refs[nin:nin + npar]]
        out_refs = refs[nin + npar:]

        def step(i, carry):
            r0 = pl.multiple_of(i * r, r)
            h0 = pl.multiple_of(jnp.maximum(r0 - 8, 0), 8)
            mains = [x[pl.ds(r0, r), :] for x in in_refs]
            halos = [jnp.where(i == 0, 0.0, x[pl.ds(h0, 8), :]) for x in in_refs]
            for o_ref, o in zip(out_refs, fn(mains, halos, pv)):
                o_ref[pl.ds(r0, r), :] = o.astype(o_ref.dtype)
            return carry

        if r == seq:
            for o_ref, o in zip(out_refs, fn([x[...] for x in in_refs], [None] * nin, pv)):
                o_ref[...] = o.astype(o_ref.dtype)
        else:
            lax.fori_loop(0, seq // r, step, 0)

    in_specs = [pl.BlockSpec((None, seq, LANE), functools.partial(lambda j, b, off: (b, 0, off + j), off=off)) for _, off in ins]
    in_specs += [pl.BlockSpec(bs, functools.partial(lambda j, b, f: f(j), f=f)) for _, bs, f in params]
    out_specs = [pl.BlockSpec((None, seq, LANE), lambda j, b: (b, 0, j)) for _ in range(nouts)]
    out_shape = [jax.ShapeDtypeStruct((bsz, seq, LANE * ncol), out_dtype) for _ in range(nouts)]
    return pl.pallas_call(body, name=name, grid=(ncol, bsz), in_specs=in_specs, out_specs=out_specs,
                          out_shape=out_shape, compiler_params=_params())(*[a for a, _ in ins], *[a for a, _, _ in params])


def seqmap_bwd(fn, ins, params, douts, ncol, name, din_dtype=f32, into=None):
    bsz, seq, _ = ins[0][0].shape
    r = min(SEQ_ROWS, seq)
    nin, npar, nout = len(ins), len(params), len(douts)
    narrow = din_dtype != f32

    def body(*refs):
        in_refs = refs[:nin]
        pv = [p[...] for p in refs[nin:nin + npar]]
        dy_refs = refs[nin + npar:nin + npar + nout]
        k0 = nin + npar + nout + (into is not None)
        dout_refs = refs[k0:k0 + nin]
        dp_refs = refs[k0 + nin:k0 + nin + npar]
        din_refs = refs[k0 + nin + npar:] if narrow else dout_refs

        def step(i, dp_acc):
            r0 = pl.multiple_of(i * r, r)
            h0 = pl.multiple_of(jnp.maximum(r0 - 8, 0), 8)
            mains = [x[pl.ds(r0, r), :] for x in in_refs]
            halos_raw = [x[pl.ds(h0, 8), :] for x in in_refs]

            def tile(mains, halos_raw, pv):
                return fn(mains, [jnp.where(i == 0, 0.0, h) for h in halos_raw], pv)

            _, vjp = jax.vjp(tile, mains, halos_raw, pv)
            dm, dh, dp = vjp([d[pl.ds(r0, r), :] for d in dy_refs])
            for d_ref, m, h in zip(din_refs, dm, dh):
                d_ref[pl.ds(r0, r), :] = m
                d_ref[pl.ds(h0, 8), :] += h
            return [acc + d for acc, d in zip(dp_acc, dp)]

        if r == seq:
            _, vjp = jax.vjp(lambda mains, pv: fn(mains, [None] * nin, pv), [x[...] for x in in_refs], pv)
            dm, dp = vjp([d[...] for d in dy_refs])
            for o_ref, m in zip(dout_refs, dm):
                o_ref[...] = m.astype(o_ref.dtype)
        else:
            dp = lax.fori_loop(0, seq // r, step, [jnp.zeros(p.shape, f32) for p in pv])
            if narrow:
                for o_ref, d_ref in zip(dout_refs, din_refs):
                    o_ref[...] = d_ref[...].astype(o_ref.dtype)

        @pl.when(pl.program_id(1) == 0)
        def _():
            for d_ref in dp_refs:
                d_ref[...] = jnp.zeros_like(d_ref)

        for d_ref, d in zip(dp_refs, dp):
            d_ref[...] += d

    in_specs = [pl.BlockSpec((None, seq, LANE), functools.partial(lambda j, b, off: (b, 0, off + j), off=off)) for _, off in ins]
    in_specs += [pl.BlockSpec(bs, functools.partial(lambda j, b, f: f(j), f=f)) for _, bs, f in params]
    in_specs += [pl.BlockSpec((None, seq, LANE), lambda j, b: (b, 0, j)) for _ in range(nout)]
    out_specs = [pl.BlockSpec((None, seq, LANE), lambda j, b: (b, 0, j)) for _ in range(nin)]
    pshapes = [tuple(d for d in bs if d is not None) for _, bs, _ in params]
    out_specs += [pl.BlockSpec((None,) + ps, functools.partial(lambda j, b, nd: (j,) + (0,) * nd, nd=len(ps))) for ps in pshapes]
    out_shape = [jax.ShapeDtypeStruct((bsz, seq, LANE * ncol), din_dtype) for _ in range(nin)]
    out_shape += [jax.ShapeDtypeStruct((ncol,) + ps, f32) for ps in pshapes]
    args = [a for a, _ in ins] + [a for a, _, _ in params] + list(douts)
    aliases = {}
    if into is not None:
        assert nin == 1 and into[0].dtype == din_dtype
        in_specs += [pl.BlockSpec(memory_space=pl.ANY)]
        aliases[len(args)] = 0
        args += [into[0]]
        out_specs[0] = pl.BlockSpec((None, seq, LANE), lambda j, b: (b, 0, into[1] + j))
        out_shape[0] = jax.ShapeDtypeStruct(into[0].shape, din_dtype)
    res = pl.pallas_call(body, name=name, grid=(ncol, bsz), in_specs=in_specs, out_specs=out_specs, out_shape=out_shape,
                         scratch_shapes=[pltpu.VMEM((seq, LANE), f32) for _ in range(nin)] if narrow and r != seq else [],
                         input_output_aliases=aliases, compiler_params=_params())(*args)
    return res[:nin], res[nin:]


def _tri_masks():
    row = lax.broadcasted_iota(jnp.int32, (CHUNK, CHUNK), 0)
    col = lax.broadcasted_iota(jnp.int32, (CHUNK, CHUNK), 1)
    return row >= col, row > col


CHUNKS_PER_STEP = 8


def _by_rows(parts, per_row):
    rows = [jnp.concatenate(parts[i:i + per_row], axis=1) for i in range(0, len(parts), per_row)]
    return jnp.concatenate(rows, axis=0)


def dn_prep(vals, ps):
    q, k, v, gb = vals[:4]
    nchunk = q.shape[0] // CHUNK
    causal, strict = _tri_masks()
    gbs = [gb[c * CHUNK:(c + 1) * CHUNK] for c in range(nchunk)]
    gcs = [cumsum_rows(g) for g in gbs]
    gcts = [g.T for g in gcs]
    chains = [(c, h) for c in range(nchunk) for h in range(DN_HEADS)]
    part = lambda t, c, h: t[c * CHUNK:(c + 1) * CHUNK, h * DN_DK:(h + 1) * DN_DK]
    qh = [part(q, c, h) for c, h in chains]
    kh = [part(k, c, h) for c, h in chains]
    vh = [part(v, c, h) for c, h in chains]
    g_col = [_lane_pick(gcs[c], h) for c, h in chains]
    beta = [_lane_pick(gbs[c], DN_HEADS + h) for c, h in chains]
    g_row = [_row_pick(gcts[c], h)[:, :CHUNK] for c, h in chains]
    decay = [jnp.where(causal, jnp.exp(jnp.where(causal, gc - gr, 0.0)), 0.0) for gc, gr in zip(g_col, g_row)]
    k_beta = [a * b for a, b in zip(kh, beta)]
    eg = [jnp.exp(g) for g in g_col]
    kk = [bdot(a, b, "nt") for a, b in zip(k_beta, kh)]
    qk = [bdot(a, b, "nt") for a, b in zip(qh, kh)]
    lower = [jnp.where(strict, a * d, 0.0) for a, d in zip(kk, decay)]
    if len(vals) == 5:
        t_inv = known_inverse([part(vals[4], c, h)[:, :CHUNK] for c, h in chains], lower)
    else:
        t_inv = unit_lower_inv_all(lower)
    u = [bdot(t, a * b, "nn") for t, a, b in zip(t_inv, vh, beta)]
    w = [bdot(t, a * e, "nn") for t, a, e in zip(t_inv, k_beta, eg)]
    attn = [jnp.concatenate([a * d, jnp.zeros((CHUNK, DN_DK - CHUNK), f32)], axis=1) for a, d in zip(qk, decay)]
    qd = [a * e for a, e in zip(qh, eg)]
    kd = [a * jnp.exp(_row_pick(g, CHUNK - 1) - g) for a, g in zip(kh, g_col)]
    g_last = jnp.concatenate([jnp.broadcast_to(_row_pick(g, CHUNK - 1), g.shape) for g in gcs], axis=0)
    outs = [_by_rows(t, DN_HEADS) for t in (u, w, attn, qd, kd)] + [g_last]
    if len(vals) == 4:
        wide = [jnp.concatenate([t, jnp.zeros((CHUNK, DN_DK - CHUNK), f32)], axis=1) for t in t_inv]
        outs.append(_by_rows(wide, DN_HEADS))
    return outs


def dn_step(state, u, w, attn, qd, kd, g_last):
    bsz = u.shape[0]
    chains = [(b, h) for b in range(bsz) for h in range(DN_HEADS)]
    part = lambda t, b, h: t[b, :, h * DN_DK:(h + 1) * DN_DK]
    ws = [bdot(part(w, b, h), s, "nn") for (b, h), s in zip(chains, state)]
    qs = [bdot(part(qd, b, h), s, "nn") for (b, h), s in zip(chains, state)]
    v_new = [part(u, b, h) - x for (b, h), x in zip(chains, ws)]
    av = [bdot(attn[b, :, h * DN_DK:h * DN_DK + CHUNK], x, "nn") for (b, h), x in zip(chains, v_new)]
    kv = [bdot(part(kd, b, h), x, "tn") for (b, h), x in zip(chains, v_new)]
    ge = [jnp.exp(_row_pick(_lane_pick(g_last[b], h), 0)) for b, h in chains]
    new_state = [s * g + x for s, g, x in zip(state, ge, kv)]
    outs = [a + b for a, b in zip(qs, av)]
    return new_state, jnp.concatenate([jnp.concatenate(outs[b * DN_HEADS:(b + 1) * DN_HEADS], axis=1)[None]
                                       for b in range(bsz)], axis=0)


def _ret_log_gamma(h):
    return math.log(1.0 - 2.0 ** (-5.0 - h))


def ret_prep(vals, ps):
    q, k, v = vals
    nchunk = q.shape[0] // CHUNK
    causal, _ = _tri_masks()
    row = lax.broadcasted_iota(jnp.int32, (CHUNK, CHUNK), 0)
    col = lax.broadcasted_iota(jnp.int32, (CHUNK, CHUNK), 1)
    dist = (row - col).astype(f32)
    lane = lax.broadcasted_iota(jnp.int32, (CHUNK, q.shape[1]), 1)
    dmask = [jnp.where(causal, jnp.exp(jnp.where(causal, dist, 0.0) * _ret_log_gamma(h)), 0.0) for h in range(RET_HEADS)]
    chains = [(c, h) for c in range(nchunk) for h in range(RET_HEADS)]
    rows = lambda t, c: t[c * CHUNK:(c + 1) * CHUNK]
    scores = [bdot(jnp.where((lane // RET_DK) == h, rows(q, c), 0.0), rows(k, c), "nt") * dmask[h] for c, h in chains]
    inner = [bdot(s, rows(v, c)[:, h * RET_DV:(h + 1) * RET_DV], "nn") for s, (c, h) in zip(scores, chains)]
    return [_by_rows(inner, RET_HEADS)]


def ret_step(state, q, k, v, inner):
    bsz = q.shape[0]
    idx = lax.broadcasted_iota(jnp.int32, (CHUNK, 1), 0).astype(f32)
    lane = lax.broadcasted_iota(jnp.int32, (CHUNK, q.shape[2]), 1)
    chains = [(b, h) for b in range(bsz) for h in range(RET_HEADS)]
    part = lambda t, b, h: t[b, :, h * RET_DV:(h + 1) * RET_DV]
    cross = [bdot(q[b], s, "nn") for (b, h), s in zip(chains, state)]
    kz = [jnp.where((lane // RET_DK) == h, k[b], 0.0) * jnp.exp((CHUNK - 1.0 - idx) * _ret_log_gamma(h)) for b, h in chains]
    kv = [bdot(a, part(v, b, h), "tn") for a, (b, h) in zip(kz, chains)]
    outs = [x * jnp.exp((idx + 1.0) * _ret_log_gamma(h)) + part(inner, b, h) for x, (b, h) in zip(cross, chains)]
    new_state = [s * math.exp(CHUNK * _ret_log_gamma(h)) + x for s, x, (b, h) in zip(state, kv, chains)]
    return new_state, jnp.concatenate([jnp.concatenate(outs[b * RET_HEADS:(b + 1) * RET_HEADS], axis=1)[None]
                                       for b in range(bsz)], axis=0)


SCAN_CHUNKS = 8


def chunk_scan(step_fn, ins, state_shape, out_width, name):
    bsz, seq, _ = ins[0].shape
    nchunk = seq // CHUNK
    nin = len(ins)
    nh = state_shape[0]
    per = SCAN_CHUNKS if nchunk % SCAN_CHUNKS == 0 else 1

    def body(*refs):
        in_refs = refs[:nin]
        o_ref, ck_ref, s_ref = refs[nin:]

        @pl.when(pl.program_id(0) == 0)
        def _():
            s_ref[...] = jnp.zeros_like(s_ref)

        state = [s_ref[i] for i in range(bsz * nh)]
        for c in range(per):
            rows = slice(c * CHUNK, (c + 1) * CHUNK)
            for i in range(bsz * nh):
                ck_ref[i // nh, c, i % nh] = state[i]
            state, out = step_fn(state, *[x[:, rows, :].astype(f32) for x in in_refs])
            o_ref[:, rows, :] = out
        for i in range(bsz * nh):
            s_ref[i] = state[i]

    in_specs = [pl.BlockSpec((bsz, per * CHUNK, x.shape[2]), lambda n: (0, n, 0)) for x in ins]
    out_specs = [pl.BlockSpec((bsz, per * CHUNK, out_width), lambda n: (0, n, 0)),
                 pl.BlockSpec((bsz, per) + tuple(state_shape), lambda n: (0, n, 0, 0, 0))]
    out_shape = [jax.ShapeDtypeStruct((bsz, seq, out_width), f32),
                 jax.ShapeDtypeStruct((bsz, nchunk) + tuple(state_shape), f32)]
    return pl.pallas_call(body, name=name, grid=(nchunk // per,), in_specs=in_specs, out_specs=out_specs, out_shape=out_shape,
                          scratch_shapes=[pltpu.VMEM((bsz * nh,) + tuple(state_shape[1:]), f32)],
                          compiler_params=_params())(*ins)


def chunk_scan_bwd(step_fn, ins, ckpt, dout, name):
    bsz, seq, _ = ins[0].shape
    nchunk = seq // CHUNK
    nin = len(ins)
    state_shape = ckpt.shape[2:]
    nh = state_shape[0]
    per = SCAN_CHUNKS if nchunk % SCAN_CHUNKS == 0 else 1
    nstep = nchunk // per

    def body(*refs):
        in_refs = refs[:nin]
        ck_ref, do_ref = refs[nin:nin + 2]
        din_refs = refs[nin + 2:nin + 2 + nin]
        ds_ref = refs[-1]

        @pl.when(pl.program_id(0) == 0)
        def _():
            ds_ref[...] = jnp.zeros_like(ds_ref)

        dstate = [ds_ref[i] for i in range(bsz * nh)]
        for c in reversed(range(per)):
            rows = slice(c * CHUNK, (c + 1) * CHUNK)
            state = [ck_ref[i // nh, c, i % nh] for i in range(bsz * nh)]
            _, vjp = jax.vjp(step_fn, state, *[x[:, rows, :].astype(f32) for x in in_refs])
            grads = vjp((dstate, do_ref[:, rows, :]))
            dstate = grads[0]
            for d_ref, d in zip(din_refs, grads[1:]):
                d_ref[:, rows, :] = d
        for i in range(bsz * nh):
            ds_ref[i] = dstate[i]

    rev = lambda n: (0, nstep - 1 - n, 0)
    in_specs = [pl.BlockSpec((bsz, per * CHUNK, x.shape[2]), rev) for x in ins]
    in_specs += [pl.BlockSpec((bsz, per) + tuple(state_shape), lambda n: (0, nstep - 1 - n, 0, 0, 0)),
                 pl.BlockSpec((bsz, per * CHUNK, dout.shape[2]), rev)]
    out_specs = [pl.BlockSpec((bsz, per * CHUNK, x.shape[2]), rev) for x in ins]
    out_shape = [jax.ShapeDtypeStruct(x.shape, f32) for x in ins]
    return pl.pallas_call(body, name=name, grid=(nstep,), in_specs=in_specs, out_specs=out_specs, out_shape=out_shape,
                          scratch_shapes=[pltpu.VMEM((bsz * nh,) + tuple(state_shape[1:]), f32)],
                          compiler_params=_params())(*ins, ckpt, dout)


LRU_ROWS = 1024


def lru_scan(a, b):
    bsz, seq, width = a.shape
    rb = min(LRU_ROWS, seq)
    seqs = range(bsz)

    def body(a_ref, b_ref, h_ref, hp_ref, carry_ref):
        @pl.when(pl.program_id(0) == 0)
        def _():
            carry_ref[...] = jnp.zeros_like(carry_ref)

        row = lax.broadcasted_iota(jnp.int32, (8, width), 0)

        def tile(t, hprev):
            r0 = pl.multiple_of(t * 8, 8)
            ca = [a_ref[i, pl.ds(r0, 8), :] for i in seqs]
            cb = [b_ref[i, pl.ds(r0, 8), :] for i in seqs]
            for s in (1, 2, 4):
                m = row >= s
                cb = [jnp.where(m, x * pltpu.roll(y, s, 0) + y, y) for x, y in zip(ca, cb)]
                ca = [jnp.where(m, x * pltpu.roll(x, s, 0), x) for x in ca]
            h = [y + x * p for x, y, p in zip(ca, cb, hprev)]
            for i in seqs:
                h_ref[i, pl.ds(r0, 8), :] = h[i]
                hp_ref[i, pl.ds(r0, 8), :] = jnp.where(row == 0, hprev[i], pltpu.roll(h[i], 1, 0))
            return tuple(_row_pick(x, 7) for x in h)

        last = lax.fori_loop(0, rb // 8, tile, tuple(carry_ref[i:i + 1, :] for i in seqs))
        for i in seqs:
            carry_ref[i:i + 1, :] = last[i]

    spec = pl.BlockSpec((bsz, rb, width), lambda i: (0, i, 0))
    return pl.pallas_call(body, name="lru_scan", grid=(seq // rb,), in_specs=[spec, spec], out_specs=[spec, spec],
                          out_shape=[jax.ShapeDtypeStruct(a.shape, f32)] * 2,
                          scratch_shapes=[pltpu.VMEM((max(8, bsz), width), f32)], compiler_params=_params())(a, b)


def lru_scan_bwd(a, hp, dh):
    bsz, seq, width = a.shape
    rb = min(LRU_ROWS, seq)
    nblk = seq // rb
    seqs = range(bsz)

    def body(a_ref, hp_ref, dh_ref, da_ref, db_ref, carry_ref):
        @pl.when(pl.program_id(0) == 0)
        def _():
            carry_ref[...] = jnp.zeros_like(carry_ref)

        row = lax.broadcasted_iota(jnp.int32, (8, width), 0)
        ntile = rb // 8

        def tile(t, mu_next):
            r0 = pl.multiple_of((ntile - 1 - t) * 8, 8)
            ca = [a_ref[i, pl.ds(r0, 8), :] for i in seqs]
            dh_t = [dh_ref[i, pl.ds(r0, 8), :] for i in seqs]
            cb = [x * y for x, y in zip(ca, dh_t)]
            for s in (1, 2, 4):
                m = row < 8 - s
                cb = [jnp.where(m, x * pltpu.roll(y, 8 - s, 0) + y, y) for x, y in zip(ca, cb)]
                ca = [jnp.where(m, x * pltpu.roll(x, 8 - s, 0), x) for x in ca]
            mu = [y + x * p for x, y, p in zip(ca, cb, mu_next)]
            for i in seqs:
                lam = dh_t[i] + jnp.where(row == 7, mu_next[i], pltpu.roll(mu[i], 7, 0))
                db_ref[i, pl.ds(r0, 8), :] = lam
                da_ref[i, pl.ds(r0, 8), :] = lam * hp_ref[i, pl.ds(r0, 8), :]
            return tuple(_row_pick(x, 0) for x in mu)

        last = lax.fori_loop(0, ntile, tile, tuple(carry_ref[i:i + 1, :] for i in seqs))
        for i in seqs:
            carry_ref[i:i + 1, :] = last[i]

    spec = pl.BlockSpec((bsz, rb, width), lambda i: (0, nblk - 1 - i, 0))
    return pl.pallas_call(body, name="lru_scan_bwd", grid=(nblk,), in_specs=[spec] * 3, out_specs=[spec, spec],
                          out_shape=[jax.ShapeDtypeStruct(a.shape, f32)] * 2,
                          scratch_shapes=[pltpu.VMEM((max(8, bsz), width), f32)], compiler_params=_params())(a, hp, dh)


MERGE_ROWS = 512


def branch_merge(ys, w_branch, u):
    n = ys[0].shape[0]
    tm = min(MERGE_ROWS, n)

    def body(ya, yb, yc, w_ref, g0, g1, g2, o_ref):
        acc = None
        for i, (y_ref, g_ref) in enumerate(((ya, g0), (yb, g1), (yc, g2))):
            term = jax.nn.sigmoid(g_ref[...]) * _dg(y_ref[...], w_ref[i], "nn")
            acc = term if acc is None else acc + term
        o_ref[...] = acc.astype(o_ref.dtype)

    y_spec = pl.BlockSpec((tm, ys[0].shape[1]), lambda i: (i, 0))
    g_specs = [pl.BlockSpec((tm, D_MODEL), functools.partial(lambda i, k: (i, k), k=k)) for k in range(3)]
    return pl.pallas_call(
        body, name="branch_merge", grid=(n // tm,),
        in_specs=[y_spec] * 3 + [pl.BlockSpec(w_branch.shape, lambda i: (0, 0, 0))] + g_specs,
        out_specs=pl.BlockSpec((tm, D_MODEL), lambda i: (i, 0)), out_shape=jax.ShapeDtypeStruct((n, D_MODEL), bf16),
        compiler_params=_params())(*ys, w_branch, u, u, u)


def branch_merge_bwd(ys, w_branch, u, d_merged, du):
    n = ys[0].shape[0]
    tm = min(MERGE_ROWS, n)

    def body(ya, yb, yc, w_ref, g0, g1, g2, dm_ref, du_in, db0, db1, db2, du_ref):
        dm = dm_ref[...]
        d_gates = []
        for i, (y_ref, g_ref, db_ref) in enumerate(((ya, g0, db0), (yb, g1, db1), (yc, g2, db2))):
            s = jax.nn.sigmoid(g_ref[...])
            db_ref[...] = (dm * s).astype(db_ref.dtype)
            d_gates.append(dm * _dg(y_ref[...], w_ref[i], "nn") * (s * (1.0 - s)))
        du_ref[...] = jnp.concatenate(d_gates, axis=1).astype(du_ref.dtype)

    y_spec = pl.BlockSpec((tm, ys[0].shape[1]), lambda i: (i, 0))
    row = pl.BlockSpec((tm, D_MODEL), lambda i: (i, 0))
    g_specs = [pl.BlockSpec((tm, D_MODEL), functools.partial(lambda i, k: (i, k), k=k)) for k in range(3)]
    res = pl.pallas_call(
        body, name="branch_merge_bwd", grid=(n // tm,),
        in_specs=[y_spec] * 3 + [pl.BlockSpec(w_branch.shape, lambda i: (0, 0, 0))] + g_specs + [row, pl.BlockSpec(memory_space=pl.ANY)],
        out_specs=[row] * 3 + [pl.BlockSpec((tm, 3 * D_MODEL), lambda i: (i, 0))],
        out_shape=[jax.ShapeDtypeStruct((n, D_MODEL), bf16)] * 3 + [jax.ShapeDtypeStruct(du.shape, du.dtype)],
        input_output_aliases={8: 3}, compiler_params=_params())(*ys, w_branch, u, u, u, d_merged, du)
    return list(res[:3]), res[3]


def final_loss(x, g, target):
    n, d = x.shape
    r = min(256, n)

    def body(x_ref, g_ref, t_ref, loss_ref, dx_ref, dg_ref, dx16_ref):
        @pl.when(pl.program_id(0) == 0)
        def _():
            loss_ref[...] = jnp.zeros_like(loss_ref)
            dg_ref[...] = jnp.zeros_like(dg_ref)

        tgt = t_ref[...]

        def loss_fn(xv, gv):
            y = f_norm([xv], [gv])[0]
            return 0.5 * jnp.sum(jnp.mean(jnp.square(y - tgt), axis=-1, keepdims=True), axis=0, keepdims=True)

        val, vjp = jax.vjp(loss_fn, x_ref[...], g_ref[...])
        dx, dg = vjp(jnp.ones_like(val))
        loss_ref[...] += val
        dx_ref[...] = dx
        dx16_ref[...] = dx.astype(dx16_ref.dtype)
        dg_ref[...] += dg

    row = pl.BlockSpec((r, d), lambda i: (i, 0))
    return pl.pallas_call(
        body, name="final_loss", grid=(n // r,), in_specs=[row, pl.BlockSpec((1, d), lambda i: (0, 0)), row],
        out_specs=[pl.BlockSpec((8, LANE), lambda i: (0, 0)), row, pl.BlockSpec((1, d), lambda i: (0, 0)), row],
        out_shape=[jax.ShapeDtypeStruct((8, LANE), f32), jax.ShapeDtypeStruct((n, d), f32), jax.ShapeDtypeStruct((1, d), f32),
                   jax.ShapeDtypeStruct((n, d), bf16)],
        compiler_params=_params())(x, g, target)


_HBM = pl.BlockSpec(memory_space=pltpu.HBM)
_SEM = pl.BlockSpec(memory_space=pltpu.SEMAPHORE)
_EFFECT = pltpu.SideEffectType.DATAFLOW_SIDE_EFFECTING


def _peer(k):
    mx, my, mc = lax.axis_index("x"), lax.axis_index("y"), lax.axis_index("c")
    px, py, pc = (mx + (k >> 2)) % 2, (my + ((k >> 1) & 1)) % 2, (mc + (k & 1)) % 2
    return (px, py, pc), 4 * px + 2 * py + pc


def _peer_copy(k, i, x_ref, land_ref, send_sems, recv_sems, scatter):
    me = 4 * lax.axis_index("x") + 2 * lax.axis_index("y") + lax.axis_index("c")
    dev, slot = _peer(k)
    sem = i * (N_DEV - 1) + k - 1
    return pltpu.make_async_remote_copy(
        src_ref=x_ref.at[slot] if scatter else x_ref, dst_ref=land_ref.at[me], send_sem=send_sems.at[sem],
        recv_sem=recv_sems.at[sem], device_id=dev, device_id_type=pl.DeviceIdType.MESH)


def _own_copy(i, x_ref, land_ref, own_sems, scatter):
    me = 4 * lax.axis_index("x") + 2 * lax.axis_index("y") + lax.axis_index("c")
    return pltpu.make_async_copy(x_ref.at[me] if scatter else x_ref, land_ref.at[me], own_sems.at[i])


def exchange_start(xs, scatters, name):
    nx = len(xs)
    lands = [lax.empty((N_DEV,) + tuple(x.shape[1:] if sc else x.shape), x.dtype) for x, sc in zip(xs, scatters)]
    nsem = nx * (N_DEV - 1)

    def body(*refs):
        x_refs, land_refs = refs[:nx], refs[nx:2 * nx]
        send_sems, recv_sems, own_sems = refs[2 * nx:2 * nx + 3]
        token = refs[-1]
        for i in range(nx):
            for k in range(1, N_DEV):
                _peer_copy(k, i, x_refs[i], land_refs[i], send_sems, recv_sems, scatters[i]).start()
            _own_copy(i, x_refs[i], land_refs[i], own_sems, scatters[i]).start()
        token[...] = jnp.zeros_like(token)

    hbm = lambda a: pltpu.HBM(a.shape, a.dtype)
    res = pl.pallas_call(
        body, name=name, in_specs=(_HBM,) * (2 * nx),
        out_specs=(_SEM, _SEM, _SEM) + (_HBM,) * (2 * nx) + (pl.BlockSpec(memory_space=pltpu.VMEM),),
        input_output_aliases={i: 3 + i for i in range(2 * nx)},
        out_shape=(pltpu.SemaphoreType.DMA((nsem,)), pltpu.SemaphoreType.DMA((nsem,)), pltpu.SemaphoreType.DMA((nx,)),
                   *[hbm(a) for a in xs], *[hbm(a) for a in lands], jax.ShapeDtypeStruct((8, LANE), f32)),
        compiler_params=pltpu.CompilerParams(has_side_effects=_EFFECT),
    )(*[pltpu.with_memory_space_constraint(a, pltpu.HBM) for a in list(xs) + lands])
    return (res[0], res[1], res[2], list(res[3:3 + nx]), list(res[3 + nx:3 + 2 * nx]), tuple(scatters)), res[-1]


def exchange_wait(started, after, name):
    send_sems, recv_sems, own_sems, x_thrus, land_thrus, scatters = started
    nx = len(x_thrus)

    def body(*refs):
        x_refs, land_refs = refs[:nx], refs[nx:2 * nx]
        send_sems, recv_sems, own_sems = refs[2 * nx:2 * nx + 3]
        for i in range(nx):
            for k in range(1, N_DEV):
                cp = _peer_copy(k, i, x_refs[i], land_refs[i], send_sems, recv_sems, scatters[i])
                cp.wait_send()
                cp.wait_recv()
            _own_copy(i, x_refs[i], land_refs[i], own_sems, scatters[i]).wait()

    hbm = lambda a: pltpu.HBM(a.shape, a.dtype)
    res = pl.pallas_call(
        body, name=name, in_specs=(_HBM,) * (2 * nx) + (_SEM, _SEM, _SEM, pl.BlockSpec(memory_space=pl.ANY)),
        out_specs=(_HBM,) * (2 * nx), input_output_aliases={i: i for i in range(2 * nx)},
        out_shape=tuple(hbm(a) for a in list(x_thrus) + list(land_thrus)),
        compiler_params=pltpu.CompilerParams(has_side_effects=_EFFECT),
    )(*x_thrus, *land_thrus, send_sems, recv_sems, own_sems, after)
    return list(res[nx:])


def sum_slots(x, name):
    _, rows_total, cols = x.shape
    row_bytes = N_DEV * ((cols + LANE - 1) // LANE) * LANE * x.dtype.itemsize
    r = _pick_rows(rows_total, max(16, (4 * 1024 * 1024) // row_bytes // 16 * 16))

    def body(x_ref, o_ref):
        acc = x_ref[0].astype(f32)
        for s in range(1, N_DEV):
            acc = acc + x_ref[s].astype(f32)
        o_ref[...] = acc

    return pl.pallas_call(body, name=name, grid=(rows_total // r,),
                          in_specs=[pl.BlockSpec((N_DEV, r, cols), lambda i: (0, i, 0))],
                          out_specs=pl.BlockSpec((r, cols), lambda i: (i, 0)),
                          out_shape=jax.ShapeDtypeStruct((rows_total, cols), f32), compiler_params=_params())(x)


def _pick_rows(total, pref):
    best = None
    for d in range(16, min(total, pref) + 1, 16):
        if total % d == 0:
            best = d
    return best if best is not None else total


def adamw(w, g, m, v, name):
    shape = w.shape
    view = (1,) * (3 - w.ndim) + shape if w.ndim < 3 else (math.prod(shape[:-2]),) + shape[-2:]
    w2, g2, m2, v2 = (t.reshape(view) for t in (w, g, m, v))
    lead, rows_total, cols = view
    r = _pick_rows(rows_total, max(16, (512 * 1024) // max(cols, 1) // 16 * 16))
    c1, c2 = 1.0 / (1.0 - ADAM_B1 ** ADAM_STEP), 1.0 / (1.0 - ADAM_B2 ** ADAM_STEP)

    def body(w_ref, g_ref, m_ref, v_ref, d_ref, nm_ref, nv_ref):
        gv = g_ref[...]
        nm = ADAM_B1 * m_ref[...] + (1.0 - ADAM_B1) * gv
        nv = ADAM_B2 * v_ref[...] + (1.0 - ADAM_B2) * jnp.square(gv)
        d_ref[...] = -ADAM_LR * ((nm * c1) / (jnp.sqrt(nv * c2) + ADAM_EPS) + ADAM_WD * w_ref[...])
        nm_ref[...] = nm
        nv_ref[...] = nv

    padded = ((r + 7) // 8) * 8 * ((cols + LANE - 1) // LANE) * LANE * 4
    lb = max(d for d in range(1, lead + 1) if lead % d == 0 and d * padded <= max(padded, 1024 * 1024))
    spec = pl.BlockSpec((lb, r, cols), lambda l, i: (l, i, 0))
    outs = pl.pallas_call(body, name=name, grid=(lead // lb, rows_total // r), in_specs=[spec] * 4, out_specs=[spec] * 3,
                          out_shape=[jax.ShapeDtypeStruct(view, f32)] * 3, compiler_params=_params())(w2, g2, m2, v2)
    return tuple(o.reshape(shape) for o in outs)


def _layer_fwd(x, wl, fetch_rest, cos, sin, bsz, seq):
    n = x.shape[0]
    sv = {"x_in": x}
    row1 = lambda a: (a, (1, a.shape[1]), lambda j: (0, 0))
    h = rowmap(f_norm, [(x, D_MODEL, 0)], [row1(wl["attn_norm"])], [(D_MODEL, bf16)], 1, "norm_fwd")[0]
    u = mm(h, wl["w_in"], "nn", "mm_in", tn=2048)
    sv["h"], sv["u"] = h, u
    u3 = u.reshape(bsz, seq, U_PAD)
    wl = dict(wl)
    wl.update(fetch_rest(u))
    sv["wl"] = wl

    qkv = []
    for kind in range(3):
        cw = (wl["dn_conv_w"], (4, LANE), functools.partial(lambda j, kind: (0, 4 * kind + j), kind=kind))
        qkv.append(seqmap(functools.partial(f_dn_pre, kind), [(u3, U_QKV // LANE + 4 * kind)], [cw], 1, 4, "dn_pre%d" % kind)[0])
    gb = rowmap(f_dn_gates, [(u, 512, U_AB // 512)], [(wl["dn_gate_p"], (8, LANE), lambda j: (0, 0))], [(LANE, f32)], 1,
                "dn_gates", rows=512)[0]
    gb3 = gb.reshape(bsz, seq, LANE)
    crow = CHUNK * CHUNKS_PER_STEP
    dn_in = [(t.reshape(n, 512), 512, 0) for t in qkv] + [(gb, LANE, 0)]
    prep_a = rowmap(dn_prep, dn_in, [], [(512, f32)] + [(512, bf16)] * 4 + [(LANE, f32), (512, f32)], 1, "dn_prep", rows=crow)
    dn_in = dn_in + [(prep_a[6], 512, 0)]
    prep_a = [t.reshape(bsz, seq, t.shape[1]) for t in prep_a[:6]]
    o_a, ck_a = chunk_scan(dn_step, prep_a, (DN_HEADS, DN_DK, DN_DK), 512, "dn_scan")
    y_a = rowmap(per_head(f_dn_post), [(o_a.reshape(n, 512), 512, 0), (u, 512, U_Z // 512)],
                 [(wl["dn_norm_w"], (1, LANE), lambda j: (0, 0))], [(512, bf16)], 1, "dn_post")[0]
    sv.update(dn_in=dn_in, prep_a=prep_a, o_a=o_a, ck_a=ck_a, y_a=y_a)

    q_b, k_b = rowmap(f_ret_pre, [(u, 256, U_RQ // 256), (u, 256, U_RK // 256), (cos, 256, 0), (sin, 256, 0)], [],
                      [(256, f32), (256, f32)], 1, "ret_pre")
    q_b3, k_b3 = q_b.reshape(bsz, seq, 256), k_b.reshape(bsz, seq, 256)
    v_b3 = lax.slice_in_dim(u3, U_RV, U_RV + 512, axis=2)
    ret_in = [(q_b, 256, 0), (k_b, 256, 0), (u, 512, U_RV // 512)]
    inner = rowmap(ret_prep, ret_in, [], [(512, f32)], 1, "ret_prep", rows=crow)[0]
    ret_seq = [q_b3, k_b3, v_b3, inner.reshape(bsz, seq, 512)]
    o_b, ck_b = chunk_scan(ret_step, ret_seq, (RET_HEADS, 256, RET_DV), 512, "ret_scan")
    y_b = rowmap(per_head(f_ret_post), [(o_b.reshape(n, 512), 512, 0), (u, 512, U_RG // 512)], [], [(512, bf16)], 1,
                 "ret_post")[0]
    sv.update(ret_in=ret_in, ret_seq=ret_seq, o_b=o_b, ck_b=ck_b, y_b=y_b)

    lru_params = _lru_params(wl)
    a_c, b_c = seqmap(f_lru_pre, [(u3, U_CX // LANE)], lru_params, 2, 4, "lru_pre")
    h_c, hp_c = lru_scan(a_c, b_c)
    y_c = rowmap(f_lru_post, [(h_c.reshape(n, 512), 512, 0), (u, 512, U_CG // 512)], [], [(512, bf16)], 1, "lru_post")[0]
    sv.update(a_c=a_c, hp_c=hp_c, h_c=h_c, y_c=y_c)

    merged = branch_merge((y_a, y_b, y_c), wl["w_branch"], u)
    x_mid = mm(merged, wl["w_out"], "nn", "mm_out", add=x)
    sv.update(merged=merged, x_mid=x_mid)

    h2 = rowmap(f_norm, [(x_mid, D_MODEL, 0)], [row1(wl["ffn_norm"])], [(D_MODEL, bf16)], 1, "norm_fwd")[0]
    up = mm(h2, wl["w_up"], "nn", "mm_up", tn=2816)
    act = seqmap(f_ffn_mid, [(up.reshape(bsz, seq, 2 * D_FF), 0), (up.reshape(bsz, seq, 2 * D_FF), D_FF // LANE)],
                 _ffn_params(wl), 1, D_FF // LANE, "ffn_mid", out_dtype=bf16)[0]
    act = act.reshape(n, D_FF)
    x_out = mm(act, wl["w_down"], "nn", "mm_down", add=x_mid)
    sv.update(h2=h2, up=up, act=act)
    return x_out, sv


def _lru_params(wl):
    col = lambda a: (a, (a.shape[0], LANE), lambda j: (0, j))
    blk = lambda a: (a, (None, LANE, LANE), lambda j: (j, 0, 0))
    return [col(wl["lru_conv_w"]), col(wl["lru_conv_b"]), blk(wl["lru_wa"]), col(wl["lru_ba"]), blk(wl["lru_wx"]),
            col(wl["lru_bx"]), col(wl["lru_lambda"])]


def _ffn_params(wl):
    nb = D_FF // LANE
    return [(wl["ffn_conv_w"], (3, LANE), lambda j: (0, j)), (wl["ffn_conv_w"], (3, LANE), lambda j: (0, nb + j)),
            (wl["ffn_conv_b"], (1, LANE), lambda j: (0, j)), (wl["ffn_conv_b"], (1, LANE), lambda j: (0, nb + j))]


def _layer_bwd(dx, dx16, sv, cos, sin, bsz, seq, emit, dep):
    n = dx.shape[0]
    gr = {}
    wl = sv["wl"]
    u, x_in, x_mid = sv["u"], sv["x_in"], sv["x_mid"]
    u3 = u.reshape(bsz, seq, U_PAD)
    row1 = lambda a: (a, (1, a.shape[1]), lambda j: (0, 0))

    d_act = mm(dx16, wl["w_down"], "nt", "mm_down_dx", dep=dep)
    gr["w_down"] = mm(sv["act"], dx16, "tn", "mm_down_dw")
    up3 = sv["up"].reshape(bsz, seq, 2 * D_FF)
    (d_gate, d_val), dps = seqmap_bwd(f_ffn_mid, [(up3, 0), (up3, D_FF // LANE)], _ffn_params(wl),
                                      [d_act.reshape(bsz, seq, D_FF)], D_FF // LANE, "ffn_mid_bwd", din_dtype=bf16)
    gr["ffn_conv_w"] = jnp.concatenate([_cols(dps[0]), _cols(dps[1])], axis=1)
    gr["ffn_conv_b"] = jnp.concatenate([_cols(dps[2]), _cols(dps[3])], axis=1)[0]
    d_gate, d_val = d_gate.reshape(n, D_FF), d_val.reshape(n, D_FF)
    gr["w_up"] = (mm(sv["h2"], d_gate, "tn", "mm_up_dw"), mm(sv["h2"], d_val, "tn", "mm_up_dw"))
    token = emit("ffn", {k: gr[k] for k in ("w_up", "w_down")})
    d_h2 = mm(d_gate, wl["w_up"], "nt", "mm_up_dx", dep=token)
    d_h2 = mm(d_val, wl["w_up"], "nt", "mm_up_dx", add=d_h2, b_koff=1)
    (dx_mid,), (dg,), ex = rowmap_bwd(f_norm, [(x_mid, D_MODEL, 0)], [row1(wl["ffn_norm"])], [d_h2], 1, "norm_bwd", add=[dx],
                                      copy16=0)
    dx_mid16 = ex["copy16"]
    gr["ffn_norm"] = dg[0, 0]

    du = lax.empty((n, U_PAD), bf16)
    du3 = lambda: du.reshape(bsz, seq, U_PAD)

    d_merged = mm(dx_mid16, wl["w_out"], "nt", "mm_out_dx")
    gr["w_out"] = mm(sv["merged"], dx_mid16, "tn", "mm_out_dw")
    ys = (sv["y_a"], sv["y_b"], sv["y_c"])
    d_br, du = branch_merge_bwd(ys, wl["w_branch"], u, d_merged, du)
    d_ys = [mm(d_br[i], wl["w_branch"][i], "nt", "mm_branch_dx") for i in range(3)]
    gr["w_branch"] = jnp.stack([mm(ys[i], d_br[i], "tn", "mm_branch_dw") for i in range(3)])
    mix_token = emit("mix", {k: gr[k] for k in ("w_branch", "w_out")})

    (d_hc, _), _, ex = rowmap_bwd(f_lru_post, [(sv["h_c"].reshape(n, 512), 512, 0), (u, 512, U_CG // 512)], [], [d_ys[2]], 1,
                                  "lru_post_bwd", into=(du, U_CG // 512, [1]))
    du = ex["into"]
    d_a, d_b = lru_scan_bwd(sv["a_c"], sv["hp_c"], d_hc.reshape(bsz, seq, 512))
    (du_new,), dps = seqmap_bwd(f_lru_pre, [(u3, U_CX // LANE)], _lru_params(wl), [d_a, d_b], 4, "lru_pre_bwd", din_dtype=bf16,
                                into=(du3(), U_CX // LANE))
    du = du_new.reshape(n, U_PAD)
    gr["lru_conv_w"], gr["lru_conv_b"] = _cols(dps[0]), _cols(dps[1])[0]
    gr["lru_wa"], gr["lru_ba"], gr["lru_wx"], gr["lru_bx"] = dps[2], dps[3][:, 0], dps[4], dps[5][:, 0]
    gr["lru_lambda"] = _cols(dps[6])[0]

    (d_ob, _), _, ex = rowmap_bwd(per_head(f_ret_post), [(sv["o_b"].reshape(n, 512), 512, 0), (u, 512, U_RG // 512)], [],
                                  [d_ys[1]], 1, "ret_post_bwd", into=(du, U_RG // 512, [1]))
    du = ex["into"]
    crow = CHUNK * CHUNKS_PER_STEP
    d_ret = chunk_scan_bwd(ret_step, sv["ret_seq"], sv["ck_b"], d_ob.reshape(bsz, seq, 512), "ret_scan_bwd")
    d_ret = [t.reshape(n, t.shape[2]) for t in d_ret]
    (d_qb, d_kb, _), _, ex = rowmap_bwd(ret_prep, sv["ret_in"], [], [d_ret[3]], 1, "ret_prep_bwd", rows=crow, add=d_ret[:3],
                                        into=(du, U_RV // 512, [2]))
    du = ex["into"]
    _, _, ex = rowmap_bwd(f_ret_pre, [(u, 256, U_RQ // 256), (u, 256, U_RK // 256), (cos, 256, 0), (sin, 256, 0)], [],
                          [d_qb, d_kb], 1, "ret_pre_bwd", din_dtypes=[f32, f32, None, None], into=(du, U_RQ // 512, [0, 1]))
    du = ex["into"]

    (d_oa, _), (dnw,), ex = rowmap_bwd(per_head(f_dn_post), [(sv["o_a"].reshape(n, 512), 512, 0), (u, 512, U_Z // 512)],
                                       [(wl["dn_norm_w"], (1, LANE), lambda j: (0, 0))], [d_ys[0]], 1, "dn_post_bwd",
                                       into=(du, U_Z // 512, [1]))
    du = ex["into"]
    gr["dn_norm_w"] = dnw[0, 0]
    d_prep = chunk_scan_bwd(dn_step, sv["prep_a"], sv["ck_a"], d_oa.reshape(bsz, seq, 512), "dn_scan_bwd")
    (d_q, d_k, d_v, d_gb, _), _, _ = rowmap_bwd(dn_prep, sv["dn_in"], [], [t.reshape(n, t.shape[2]) for t in d_prep], 1,
                                                "dn_prep_bwd", rows=crow, din_dtypes=[f32] * 4 + [None])
    d_q, d_k, d_v = (t.reshape(bsz, seq, 512) for t in (d_q, d_k, d_v))
    _, (dgp,), ex = rowmap_bwd(f_dn_gates, [(u, 512, U_AB // 512)], [(wl["dn_gate_p"], (8, LANE), lambda j: (0, 0))],
                               [d_gb], 1, "dn_gates_bwd", rows=512, into=(du, U_AB // 512, [0]))
    du = ex["into"]
    gr["dn_a_log"], gr["dn_dt_bias"] = dgp[0, 0, :DN_HEADS], dgp[0, 1, :DN_HEADS]
    d_cw = []
    for kind, d_t in enumerate((d_q, d_k, d_v)):
        cw = (wl["dn_conv_w"], (4, LANE), functools.partial(lambda j, kind: (0, 4 * kind + j), kind=kind))
        (du_new,), (dcw,) = seqmap_bwd(functools.partial(f_dn_pre, kind), [(u3, U_QKV // LANE + 4 * kind)], [cw], [d_t], 4,
                                       "dn_pre%d_bwd" % kind, din_dtype=bf16, into=(du3(), U_QKV // LANE + 4 * kind))
        du = du_new.reshape(n, U_PAD)
        d_cw.append(_cols(dcw))
    gr["dn_conv_w"] = jnp.concatenate(d_cw, axis=1)

    gr["w_in"] = _unpad_w_in(mm(sv["h"], du, "tn", "mm_in_dw", dep=mix_token, tn=2048))
    token = emit("in", {"w_in": gr["w_in"]})
    d_h = mm(du, wl["w_in"], "nt", "mm_in_dx", dep=token)
    (dx_in,), (dg,), ex = rowmap_bwd(f_norm, [(x_in, D_MODEL, 0)], [row1(wl["attn_norm"])], [d_h], 1, "norm_bwd", add=[dx_mid],
                                     copy16=0)
    gr["attn_norm"] = dg[0, 0]
    big = ("w_in", "w_branch", "w_out", "w_up", "w_down")
    return dx_in, ex["copy16"], emit("small", {k: g for k, g in gr.items() if k not in big})


def _cols(dp):
    ncol, p, _ = dp.shape
    return jnp.transpose(dp, (1, 0, 2)).reshape(p, ncol * LANE)


def _pad_w_in(w):
    segs = sorted(_IN_SEGS, key=lambda s: s[2])
    parts = [lax.slice_in_dim(w, src, src + width, axis=1) for src, width, _ in segs]
    end = segs[-1][2] + segs[-1][1]
    return jnp.concatenate(parts + [jnp.zeros((w.shape[0], U_PAD - end), w.dtype)], axis=1)


def _unpad_w_in(wp):
    return jnp.concatenate([lax.slice_in_dim(wp, dst, dst + width, axis=1) for _, width, dst in _IN_SEGS], axis=1)


def _rope_tables(positions):
    half = RET_DK // 2
    inv = ROPE_BASE ** (-jnp.arange(half, dtype=f32) / half)
    ang = positions.astype(f32).reshape(-1, 1) * inv
    cos, sin = jnp.cos(ang), jnp.sin(ang)
    return jnp.tile(cos, (1, 2 * RET_HEADS)), jnp.tile(sin, (1, 2 * RET_HEADS))


def _layer_weights(lw):
    wl = {}
    wl["w_in"] = _pad_w_in(lw["w_in"])
    for k in ("dn_conv_w", "lru_conv_w", "ffn_conv_w", "lru_wa", "lru_wx"):
        wl[k] = lw[k]
    for k in ("attn_norm", "ffn_norm", "dn_norm_w", "lru_conv_b", "lru_lambda", "ffn_conv_b", "lru_ba", "lru_bx"):
        wl[k] = lw[k].reshape(1, -1)
    gp = jnp.zeros((8, LANE), f32)
    wl["dn_gate_p"] = gp.at[0, :DN_HEADS].set(lw["dn_a_log"]).at[1, :DN_HEADS].set(lw["dn_dt_bias"])
    return wl


REST = ("w_branch", "w_out", "w_up", "w_down")


def forward_backward(x, positions, target, layer_weights, final_norm, on_head, on_grads):
    bsz, seq, d = x.shape
    n = bsz * seq
    cos, sin = _rope_tables(positions)
    xs = x.reshape(n, d)
    saved = []
    for layer in range(DEPTH):
        first, fetch_rest = layer_weights(layer, xs)
        xs, sv = _layer_fwd(xs, _layer_weights(first), fetch_rest, cos, sin, bsz, seq)
        saved.append(sv)
    loss, dx, d_final, dx16 = final_loss(xs, final_norm.reshape(1, d), target.reshape(n, d))
    on_head(loss[0, 0], d_final[0])
    token = None
    for layer in reversed(range(DEPTH)):
        dx, dx16, token = _layer_bwd(dx, dx16, saved[layer], cos, sin, bsz, seq, functools.partial(on_grads, layer), token)
    return dx.reshape(bsz, seq, d)


BIG = (("w_in", 2), ("w_branch", 3), ("w_out", 1), ("w_up", 2), ("w_down", 1))
SMALL_SHARDED = (("dn_conv_w", 2), ("lru_conv_w", 2), ("ffn_conv_w", 2))
REPLICATED = ("attn_norm", "dn_a_log", "dn_dt_bias", "dn_norm_w", "lru_conv_b", "lru_wa", "lru_ba", "lru_wx", "lru_bx",
              "lru_lambda", "ffn_norm", "ffn_conv_b", "final_norm")
WEIGHTS = ("attn_norm", "w_in", "dn_conv_w", "dn_a_log", "dn_dt_bias", "dn_norm_w", "lru_conv_w", "lru_conv_b", "lru_wa",
           "lru_ba", "lru_wx", "lru_bx", "lru_lambda", "w_branch", "w_out", "ffn_norm", "w_up", "ffn_conv_w", "ffn_conv_b",
           "w_down", "final_norm")


def _pack(arrs, dtype, align=16 * LANE):
    flat = jnp.concatenate([a.reshape(-1).astype(dtype) for a in arrs])
    pad = (-flat.shape[0]) % align
    return jnp.pad(flat, (0, pad)).reshape(-1, LANE)


def _unpack(rows, shapes):
    flat = rows.reshape(-1)
    out, pos = [], 0
    for shp in shapes:
        size = math.prod(shp)
        out.append(lax.slice_in_dim(flat, pos, pos + size).reshape(shp))
        pos += size
    return out


def kernel(x, positions, attn_norm, w_in, dn_conv_w, dn_a_log, dn_dt_bias, dn_norm_w, lru_conv_w, lru_conv_b, lru_wa, lru_ba, lru_wx, lru_bx, lru_lambda, w_branch, w_out, ffn_norm, w_up, ffn_conv_w, ffn_conv_b, w_down, final_norm, loss_target, m_attn_norm, m_w_in, m_dn_conv_w, m_dn_a_log, m_dn_dt_bias, m_dn_norm_w, m_lru_conv_w, m_lru_conv_b, m_lru_wa, m_lru_ba, m_lru_wx, m_lru_bx, m_lru_lambda, m_w_branch, m_w_out, m_ffn_norm, m_w_up, m_ffn_conv_w, m_ffn_conv_b, m_w_down, m_final_norm, v_attn_norm, v_w_in, v_dn_conv_w, v_dn_a_log, v_dn_dt_bias, v_dn_norm_w, v_lru_conv_w, v_lru_conv_b, v_lru_wa, v_lru_ba, v_lru_wx, v_lru_bx, v_lru_lambda, v_w_branch, v_w_out, v_ffn_norm, v_w_up, v_ffn_conv_w, v_ffn_conv_b, v_w_down, v_final_norm):
    w = dict(attn_norm=attn_norm, w_in=w_in, dn_conv_w=dn_conv_w, dn_a_log=dn_a_log, dn_dt_bias=dn_dt_bias, dn_norm_w=dn_norm_w,
             lru_conv_w=lru_conv_w, lru_conv_b=lru_conv_b, lru_wa=lru_wa, lru_ba=lru_ba, lru_wx=lru_wx, lru_bx=lru_bx,
             lru_lambda=lru_lambda, w_branch=w_branch, w_out=w_out, ffn_norm=ffn_norm, w_up=w_up, ffn_conv_w=ffn_conv_w,
             ffn_conv_b=ffn_conv_b, w_down=w_down, final_norm=final_norm)
    m = dict(attn_norm=m_attn_norm, w_in=m_w_in, dn_conv_w=m_dn_conv_w, dn_a_log=m_dn_a_log, dn_dt_bias=m_dn_dt_bias,
             dn_norm_w=m_dn_norm_w, lru_conv_w=m_lru_conv_w, lru_conv_b=m_lru_conv_b, lru_wa=m_lru_wa, lru_ba=m_lru_ba,
             lru_wx=m_lru_wx, lru_bx=m_lru_bx, lru_lambda=m_lru_lambda, w_branch=m_w_branch, w_out=m_w_out, ffn_norm=m_ffn_norm,
             w_up=m_w_up, ffn_conv_w=m_ffn_conv_w, ffn_conv_b=m_ffn_conv_b, w_down=m_w_down, final_norm=m_final_norm)
    v = dict(attn_norm=v_attn_norm, w_in=v_w_in, dn_conv_w=v_dn_conv_w, dn_a_log=v_dn_a_log, dn_dt_bias=v_dn_dt_bias,
             dn_norm_w=v_dn_norm_w, lru_conv_w=v_lru_conv_w, lru_conv_b=v_lru_conv_b, lru_wa=v_lru_wa, lru_ba=v_lru_ba,
             lru_wx=v_lru_wx, lru_bx=v_lru_bx, lru_lambda=v_lru_lambda, w_branch=v_w_branch, w_out=v_w_out, ffn_norm=v_ffn_norm,
             w_up=v_w_up, ffn_conv_w=v_ffn_conv_w, ffn_conv_b=v_ffn_conv_b, w_down=v_w_down, final_norm=v_final_norm)

    me = 4 * lax.axis_index("x") + 2 * lax.axis_index("y") + lax.axis_index("c")
    axes = dict(BIG + SMALL_SHARDED)
    conv_names = [k for k, _ in SMALL_SHARDED]

    gathers, tokens, conv_full = {}, [], {}
    for layer in range(DEPTH):
        first = [w["w_in"][layer].astype(bf16)] + ([w[k] for k in conv_names] if layer == 0 else [])
        rest = [w[k][layer].astype(bf16) for k in REST]
        for part, srcs in (("in", first), ("rest", rest)):
            gathers[layer, part], token = exchange_start(srcs, [False] * len(srcs), "gather_%s_start%d" % (part, layer))
            tokens.append(token[0:1, 0:1])
    all_started = functools.reduce(lambda a, b: a + b, tokens)

    def join(land, axis):
        if axis == 0:
            return land.reshape((N_DEV * land.shape[1],) + land.shape[2:])
        return jnp.concatenate([land[p] for p in range(N_DEV)], axis=axis)

    def split(g, axis):
        if isinstance(g, tuple):
            each = N_DEV // len(g)
            size = g[0].shape[axis] // each
            return jnp.stack([lax.slice_in_dim(piece, p * size, (p + 1) * size, axis=axis) for piece in g for p in range(each)])
        size = g.shape[axis] // N_DEV
        if axis == 0:
            return g.reshape((N_DEV, size) + g.shape[1:])
        return jnp.stack([lax.slice_in_dim(g, p * size, (p + 1) * size, axis=axis) for p in range(N_DEV)])

    def layer_weights(layer, x_in):
        lands = exchange_wait(gathers[layer, "in"], x_in, "gather_in_wait%d" % layer)
        lw = {"w_in": join(lands[0], 1)}
        if layer == 0:
            conv_full.update({k: join(lands[1 + i], axes[k]) for i, k in enumerate(conv_names)})
        lw.update({k: conv_full[k][layer] for k in conv_names})
        lw.update({k: w[k][layer] for k in REPLICATED if k != "final_norm"})
        if layer == 0:
            lw["attn_norm"] = lw["attn_norm"] + all_started[0]

        def fetch_rest(after):
            lands_r = exchange_wait(gathers[layer, "rest"], after, "gather_rest_wait%d" % layer)
            return {k: join(lands_r[i], axes[k] - 1) for i, k in enumerate(REST)}

        return lw, fetch_rest

    small_names = conv_names + [k for k in REPLICATED if k != "final_norm"]
    groups = {"ffn": ("w_up", "w_down"), "mix": ("w_branch", "w_out"), "in": ("w_in",)}
    scatters, small_shapes, head = {}, {}, {}

    def on_grads(layer, group, gr):
        if group == "small":
            small_shapes.update({k: gr[k].shape for k in small_names})
            srcs = [_pack([gr[k] for k in small_names], f32)]
            srcs += [_pack([head["loss"].reshape(1), head["d_final"]], f32)] if layer == DEPTH - 1 else []
            modes = [False] * len(srcs)
        else:
            srcs = [split(gr[k], axes[k] - 1).astype(bf16) for k in groups[group]]
            modes = [True] * len(srcs)
        scatters[layer, group], token = exchange_start(srcs, modes, "scatter_%s_start%d" % (group, layer))
        return token

    grad_x = forward_backward(x, positions, loss_target, layer_weights, final_norm,
                              lambda loss_part, d_final: head.update(loss=loss_part, d_final=d_final), on_grads)

    big_sums, small_sums = {}, {}
    for group in ("ffn", "mix", "in"):
        for layer in reversed(range(DEPTH)):
            lands = exchange_wait(scatters[layer, group], grad_x, "scatter_%s_wait%d" % (group, layer))
            for i, k in enumerate(groups[group]):
                shard = w[k].shape[1:]
                big_sums[layer, k] = sum_slots(lands[i].reshape(N_DEV, -1, shard[-1]), "sum_" + k).reshape(shard)
    for layer in reversed(range(DEPTH)):
        lands = exchange_wait(scatters[layer, "small"], grad_x, "scatter_small_wait%d" % layer)
        small_sums[layer] = sum_slots(lands[0], "sum_small")
        if layer == DEPTH - 1:
            head_sum = _unpack(sum_slots(lands[1], "sum_head"), [(1,), final_norm.shape])
    grads = {k: jnp.stack([big_sums[layer, k] for layer in range(DEPTH)]) for k, _ in BIG}
    loss, grads["final_norm"] = head_sum[0][0], head_sum[1]
    small_flat = jnp.stack([small_sums[layer] for layer in range(DEPTH)]).reshape(DEPTH, -1)
    pos = 0
    for k in small_names:
        size = math.prod(small_shapes[k])
        g = lax.slice_in_dim(small_flat, pos, pos + size, axis=1).reshape((DEPTH,) + small_shapes[k])
        pos += size
        ax = dict(SMALL_SHARDED).get(k)
        if ax is None:
            grads[k] = g
        else:
            size = g.shape[ax] // N_DEV
            grads[k] = lax.dynamic_slice_in_dim(g, me * size, size, axis=ax)

    inner_last = {"w_in": (2, 0, 1), "w_up": (0, 2, 1)}
    upd = {}
    for k in WEIGHTS:
        if k in inner_last:
            perm = inner_last[k]
            back = tuple(perm.index(i) for i in range(3))
            g_t = jnp.transpose(grads[k], perm)
            res = adamw(jnp.transpose(w[k], perm), g_t, jnp.transpose(m[k], perm), jnp.transpose(v[k], perm), "adamw_" + k)
            upd[k] = tuple(jnp.transpose(t, back) for t in res)
            grads[k] = jnp.transpose(g_t, back)
        else:
            upd[k] = adamw(w[k], grads[k], m[k], v[k], "adamw_" + k)
    return (loss, grad_x, *[grads[k] for k in WEIGHTS], *[upd[k][0] for k in WEIGHTS], *[upd[k][1] for k in WEIGHTS],
            *[upd[k][2] for k in WEIGHTS])
```

```python
import functools
import math

import jax
import jax.numpy as jnp
from jax import lax
from jax.experimental import pallas as pl
from jax.experimental.pallas import tpu as pltpu

f32 = jnp.float32
bf16 = jnp.bfloat16

D_MODEL = 1024
DEPTH = 4
CHUNK = 64
EPS = 1e-6
DN_HEADS, DN_DK = 4, 128
RET_HEADS, RET_DK, RET_DV = 4, 64, 128
ROPE_BASE = 10000.0
LRU_C = 8.0
D_FF = 2816
N_DEV = 8
LANE = 128
VMEM_LIMIT = 56 * 1024 * 1024

ADAM_LR, ADAM_B1, ADAM_B2, ADAM_EPS, ADAM_WD, ADAM_STEP = 0.001, 0.9, 0.999, 1e-8, 0.01, 10

U_GATES, U_QKV, U_RV, U_RG, U_Z, U_CX, U_CG, U_RQ, U_RK, U_AB = (
    0, 3072, 4608, 5120, 5632, 6144, 6656, 7168, 7424, 7680)
U_PAD = 8192
_IN_SEGS = ((0, 1536, U_QKV), (1536, 8, U_AB), (1544, 512, U_Z), (2056, 256, U_RQ), (2312, 256, U_RK),
            (2568, 512, U_RV), (3080, 512, U_RG), (3592, 512, U_CX), (4104, 512, U_CG), (4616, 3072, U_GATES))


def _params():
    return pltpu.CompilerParams(vmem_limit_bytes=VMEM_LIMIT)


def _pick(dim, pref):
    best = None
    for d in range(LANE, min(dim, pref) + 1, LANE):
        if dim % d == 0:
            best = d
    return best if best is not None else dim


@functools.partial(jax.custom_vjp, nondiff_argnums=(1, 2))
def sroll(x, shift, axis):
    return pltpu.roll(x, shift, axis)


def _sroll_fwd(x, shift, axis):
    return pltpu.roll(x, shift, axis), None


def _sroll_bwd(shift, axis, _, g):
    n = g.shape[axis]
    return (pltpu.roll(g, (n - shift) % n, axis),)


sroll.defvjp(_sroll_fwd, _sroll_bwd)

_DIMS = {"nn": (((1,), (0,)), ((), ())), "nt": (((1,), (1,)), ((), ())), "tn": (((0,), (0,)), ((), ()))}


def _dg(a, b, dims):
    return lax.dot_general(a.astype(bf16), b.astype(bf16), _DIMS[dims], preferred_element_type=f32)


@functools.partial(jax.custom_vjp, nondiff_argnums=(2,))
def bdot(a, b, dims):
    return _dg(a, b, dims)


def _bdot_fwd(a, b, dims):
    return _dg(a, b, dims), (a.astype(bf16), b.astype(bf16))


def _bdot_bwd(dims, res, g):
    a, b = res
    if dims == "nn":
        return _dg(g, b, "nt"), _dg(a, g, "tn")
    if dims == "nt":
        return _dg(g, b, "nn"), _dg(g, a, "tn")
    return _dg(b, g, "nt"), _dg(a, g, "nn")


bdot.defvjp(_bdot_fwd, _bdot_bwd)


def _fdot(a, b, dims):
    return lax.dot_general(a, b, _DIMS[dims], precision=lax.Precision.HIGH, preferred_element_type=f32)


@jax.custom_vjp
def unit_lower_inv_all(mats):
    shape = mats[0].shape
    row = lax.broadcasted_iota(jnp.int32, shape, 0)
    col = lax.broadcasted_iota(jnp.int32, shape, 1)
    eye = jnp.where(row == col, 1.0, 0.0).astype(f32)
    n = [-a for a in mats]
    p = [eye + x for x in n]
    span = 2
    while span < shape[0]:
        n = [_fdot(x, x, "nn") for x in n]
        p = [y + _fdot(y, x, "nn") for y, x in zip(p, n)]
        span *= 2
    return p


def _uli_fwd(mats):
    x = unit_lower_inv_all(mats)
    return x, x


def _uli_bwd(xs, gs):
    t = [_fdot(x, g, "tn") for x, g in zip(xs, gs)]
    return ([-_fdot(y, x, "nt") for y, x in zip(t, xs)],)


unit_lower_inv_all.defvjp(_uli_fwd, _uli_bwd)


@jax.custom_vjp
def known_inverse(invs, mats):
    return invs


def _known_fwd(invs, mats):
    return invs, invs


def _known_bwd(xs, gs):
    return [jnp.zeros_like(x) for x in xs], _uli_bwd(xs, gs)[0]


known_inverse.defvjp(_known_fwd, _known_bwd)


def cumsum_rows(x):
    rows = x.shape[0]
    row = lax.broadcasted_iota(jnp.int32, x.shape, 0)
    s = 1
    while s < rows:
        x = x + jnp.where(row >= s, sroll(x, s, 0), 0.0)
        s *= 2
    return x


def _expm1(x):
    return jnp.tanh(0.5 * x) * (jnp.exp(x) + 1.0)


def _lane_pick(x, lane):
    idx = lax.broadcasted_iota(jnp.int32, x.shape, 1)
    return jnp.sum(jnp.where(idx == lane, x, 0.0), axis=1, keepdims=True)


def _row_pick(x, r):
    idx = lax.broadcasted_iota(jnp.int32, x.shape, 0)
    return jnp.sum(jnp.where(idx == r, x, 0.0), axis=0, keepdims=True)


def _causal_conv(x, halo, w, width):
    if halo is None:
        row = lax.broadcasted_iota(jnp.int32, x.shape, 0)
        acc = x * w[width - 1:width]
        for k in range(width - 1):
            shift = width - 1 - k
            acc = acc + jnp.where(row >= shift, sroll(x, shift, 0), 0.0) * w[k:k + 1]
        return acc
    xe = jnp.concatenate([halo, x], axis=0)
    acc = xe * w[width - 1:width]
    for k in range(width - 1):
        acc = acc + sroll(xe, width - 1 - k, 0) * w[k:k + 1]
    return acc[8:]


def f_norm(ins, ps):
    (x,), (g,) = ins, ps
    return [x * lax.rsqrt(jnp.mean(x * x, axis=-1, keepdims=True) + EPS) * g]


def f_dn_pre(kind, mains, halos, ps):
    y = _causal_conv(mains[0], halos[0], ps[0], 4)
    y = y * jax.nn.sigmoid(y)
    if kind < 2:
        y = y * lax.rsqrt(jnp.sum(y * y, axis=-1, keepdims=True) + EPS)
    if kind == 0:
        y = y * (DN_DK ** -0.5)
    return [y]


def f_dn_gates(ins, ps):
    u, p = ins[0][:, :LANE], ps[0]
    lane = lax.broadcasted_iota(jnp.int32, u.shape, 1)
    g = -jnp.exp(p[0:1]) * jax.nn.softplus(u + p[1:2])
    beta = jax.nn.sigmoid(u)
    return [jnp.where(lane < 4, g, jnp.where(lane < 8, beta, 0.0))]


def per_head(fn):
    def tile_fn(vals, ps):
        heads = [fn([v[:, h * LANE:(h + 1) * LANE] for v in vals], ps) for h in range(vals[0].shape[1] // LANE)]
        return [jnp.concatenate([o[i] for o in heads], axis=1) for i in range(len(heads[0]))]
    return tile_fn


def f_dn_post(ins, ps):
    (o, z), (nw,) = ins, ps
    y = o * lax.rsqrt(jnp.mean(o * o, axis=-1, keepdims=True) + EPS) * nw
    return [y * (z * jax.nn.sigmoid(z))]


def _rot_half(t):
    lane = lax.broadcasted_iota(jnp.int32, t.shape, 1)
    width = t.shape[1]
    first = (lane % RET_DK) < (RET_DK // 2)
    return jnp.where(first, -sroll(t, width - RET_DK // 2, 1), sroll(t, RET_DK // 2, 1))


def f_ret_pre(ins, ps):
    q, k, cos, sin = ins
    qr = q * cos + _rot_half(q) * sin
    kr = (k * cos + _rot_half(k) * sin) * (RET_DK ** -0.5)
    return [qr, kr]


def f_ret_post(ins, ps):
    o, g = ins
    mu = jnp.mean(o, axis=-1, keepdims=True)
    var = jnp.mean(jnp.square(o - mu), axis=-1, keepdims=True)
    return [(o - mu) * lax.rsqrt(var + EPS) * (g * jax.nn.sigmoid(g))]


def f_lru_pre(mains, halos, ps):
    cw, cb, wa, ba, wx, bx, lam = ps
    xc = _causal_conv(mains[0], halos[0], cw, 4) + cb
    r = jax.nn.sigmoid(bdot(xc, wa, "nn") + ba)
    i = jax.nn.sigmoid(bdot(xc, wx, "nn") + bx)
    log_a = -LRU_C * r * jax.nn.softplus(-lam)
    a = jnp.exp(log_a)
    b = jnp.sqrt(-_expm1(2.0 * log_a)) * (i * xc)
    return [a, b]


def f_lru_post(ins, ps):
    h, g = ins
    return [h * jax.nn.gelu(g)]


def f_ffn_mid(mains, halos, ps):
    cwg, cwv, cbg, cbv = ps
    gate = _causal_conv(mains[0], halos[0], cwg, 3) + cbg
    val = _causal_conv(mains[1], halos[1], cwv, 3) + cbv
    return [gate * jax.nn.sigmoid(gate) * val]


def mm(a, b, dims, name, add=None, dep=None, b_koff=0, tm=1536, tn=1536, tk=2816):
    if dims == "tn":
        kdim, m = a.shape
        n = b.shape[1]
    else:
        m, kdim = a.shape
        n = b.shape[0] if dims == "nt" else b.shape[1]
    tm, tn, tk = _pick(m, tm), _pick(n, tn), _pick(kdim, tk)
    nk = kdim // tk
    a_spec = pl.BlockSpec((tk, tm), lambda i, j, k: (k, i)) if dims == "tn" else pl.BlockSpec((tm, tk), lambda i, j, k: (i, k))
    b_spec = (pl.BlockSpec((tn, tk), lambda i, j, k: (j, k + b_koff * nk)) if dims == "nt"
              else pl.BlockSpec((tk, tn), lambda i, j, k: (k, j)))
    o_spec = pl.BlockSpec((tm, tn), lambda i, j, k: (i, j))
    has_add, has_dep = add is not None, dep is not None

    def body(*refs):
        a_ref, b_ref = refs[:2]
        add_ref = refs[2] if has_add else None
        o_ref = refs[2 + has_add + has_dep]
        if nk == 1:
            prod = _dg(a_ref[...], b_ref[...], dims)
            o_ref[...] = prod + add_ref[...] if has_add else prod
            return
        acc_ref = refs[-1]
        k = pl.program_id(2)

        @pl.when(k == 0)
        def _():
            acc_ref[...] = jnp.zeros_like(acc_ref)

        acc_ref[...] += _dg(a_ref[...], b_ref[...], dims)

        @pl.when(k == nk - 1)
        def _():
            o_ref[...] = acc_ref[...] + add_ref[...] if has_add else acc_ref[...]

    args = [a, b] + ([add] if has_add else []) + ([dep] if has_dep else [])
    in_specs = [a_spec, b_spec] + ([o_spec] if has_add else [])
    in_specs += [pl.BlockSpec((8, LANE), lambda i, j, k: (0, 0))] if has_dep else []
    return pl.pallas_call(
        body, name=name, grid=(m // tm, n // tn, nk), in_specs=in_specs, out_specs=o_spec,
        out_shape=jax.ShapeDtypeStruct((m, n), f32), scratch_shapes=[pltpu.VMEM((tm, tn), f32)] if nk > 1 else [],
        compiler_params=_params())(*args)


def mm_add_norm(a, b, add, g, name, tm=1024):
    m, kdim = a.shape
    n = b.shape[1]
    tm = _pick(m, tm)

    def body(a_ref, b_ref, add_ref, g_ref, x_ref, h_ref):
        x = _dg(a_ref[...], b_ref[...], "nn") + add_ref[...]
        x_ref[...] = x
        h_ref[...] = f_norm([x], [g_ref[...]])[0].astype(h_ref.dtype)

    row = pl.BlockSpec((tm, n), lambda i: (i, 0))
    return pl.pallas_call(
        body, name=name, grid=(m // tm,),
        in_specs=[pl.BlockSpec((tm, kdim), lambda i: (i, 0)), pl.BlockSpec((kdim, n), lambda i: (0, 0)), row,
                  pl.BlockSpec((1, n), lambda i: (0, 0))],
        out_specs=[row, row], out_shape=[jax.ShapeDtypeStruct((m, n), f32), jax.ShapeDtypeStruct((m, n), bf16)],
        compiler_params=_params())(a, b, add, g)


def rowmap(fn, ins, params, outs, ncol, name, rows=512):
    n = ins[0][0].shape[0]
    r = min(rows, n)
    nin, npar = len(ins), len(params)

    def body(*refs):
        vals = [x[...] for x in refs[:nin]]
        pv = [p[...] for p in refs[nin:nin + npar]]
        for o_ref, o in zip(refs[nin + npar:], fn(vals, pv)):
            o_ref[...] = o.astype(o_ref.dtype)

    in_specs = [pl.BlockSpec((r, cb), functools.partial(lambda j, i, off: (i, off + j), off=off)) for _, cb, off in ins]
    in_specs += [pl.BlockSpec(bs, functools.partial(lambda j, i, f: f(j), f=f)) for _, bs, f in params]
    out_specs = [pl.BlockSpec((r, cb), lambda j, i: (i, j)) for cb, _ in outs]
    out_shape = [jax.ShapeDtypeStruct((n, cb * ncol), dt) for cb, dt in outs]
    res = pl.pallas_call(body, name=name, grid=(ncol, n // r), in_specs=in_specs, out_specs=out_specs,
                         out_shape=out_shape, compiler_params=_params())(*[a for a, _, _ in ins], *[a for a, _, _ in params])
    return res


def rowmap_bwd(fn, ins, params, douts, ncol, name, rows=512, add=None, din_dtypes=None, into=None, copy16=None):
    n = ins[0][0].shape[0]
    r = min(rows, n)
    nin, npar, nout = len(ins), len(params), len(douts)
    add = [None] * nin if add is None else list(add)
    add_idx = [i for i in range(nin) if add[i] is not None]
    din_dtypes = [f32] * nin if din_dtypes is None else list(din_dtypes)
    into_buf, into_off, into_idx = into if into is not None else (None, 0, [])
    has_into, has_copy = into is not None, copy16 is not None
    kept = [i for i in range(nin) if din_dtypes[i] is not None and i not in into_idx]

    def body(*refs):
        vals = [x[...] for x in refs[:nin]]
        pv = [p[...] for p in refs[nin:nin + npar]]
        dys = [d[...] for d in refs[nin + npar:nin + npar + nout]]
        k0 = nin + npar + nout
        add_refs = dict(zip(add_idx, refs[k0:k0 + len(add_idx)]))
        k0 += len(add_idx) + has_into
        din_refs = refs[k0:k0 + len(kept)]
        k0 += len(kept)
        copy_ref = refs[k0] if has_copy else None
        into_ref = refs[k0 + has_copy] if has_into else None
        dp_refs = refs[k0 + has_copy + has_into:]
        _, vjp = jax.vjp(fn, vals, pv)
        dvals, dpv = vjp(dys)
        cot = lambda idx: dvals[idx] + add_refs[idx][...] if idx in add_refs else dvals[idx]
        for d_ref, idx in zip(din_refs, kept):
            d_ref[...] = cot(idx).astype(d_ref.dtype)
        if has_copy:
            copy_ref[...] = cot(copy16).astype(copy_ref.dtype)
        if has_into:
            parts = [cot(idx) for idx in into_idx]
            into_ref[...] = (parts[0] if len(parts) == 1 else jnp.concatenate(parts, axis=1)).astype(into_ref.dtype)

        @pl.when(pl.program_id(1) == 0)
        def _():
            for d_ref in dp_refs:
                d_ref[...] = jnp.zeros_like(d_ref)

        for d_ref, d in zip(dp_refs, dpv):
            d_ref[...] += d

    in_specs = [pl.BlockSpec((r, cb), functools.partial(lambda j, i, off: (i, off + j), off=off)) for _, cb, off in ins]
    in_specs += [pl.BlockSpec(bs, functools.partial(lambda j, i, f: f(j), f=f)) for _, bs, f in params]
    in_specs += [pl.BlockSpec((r, d.shape[1] // ncol), lambda j, i: (i, j)) for d in douts]
    in_specs += [pl.BlockSpec((r, ins[i][1]), lambda j, i: (i, j)) for i in add_idx]
    out_specs = [pl.BlockSpec((r, ins[i][1]), lambda j, i: (i, j)) for i in kept]
    out_shape = [jax.ShapeDtypeStruct((n, ins[i][1] * ncol), din_dtypes[i]) for i in kept]
    args = [a for a, _, _ in ins] + [a for a, _, _ in params] + list(douts) + [add[i] for i in add_idx]
    aliases = {}
    if has_copy:
        out_specs += [pl.BlockSpec((r, ins[copy16][1]), lambda j, i: (i, j))]
        out_shape += [jax.ShapeDtypeStruct((n, ins[copy16][1] * ncol), bf16)]
    if has_into:
        assert ncol == 1
        in_specs += [pl.BlockSpec(memory_space=pl.ANY)]
        aliases[len(args)] = len(out_shape)
        args += [into_buf]
        out_specs += [pl.BlockSpec((r, sum(ins[i][1] for i in into_idx)), lambda j, i: (i, into_off))]
        out_shape += [jax.ShapeDtypeStruct(into_buf.shape, into_buf.dtype)]
    pshapes = [tuple(d for d in bs if d is not None) for _, bs, _ in params]
    out_specs += [pl.BlockSpec((None,) + ps, functools.partial(lambda j, i, nd: (j,) + (0,) * nd, nd=len(ps))) for ps in pshapes]
    out_shape += [jax.ShapeDtypeStruct((ncol,) + ps, f32) for ps in pshapes]
    res = pl.pallas_call(body, name=name, grid=(ncol, n // r), in_specs=in_specs, out_specs=out_specs, out_shape=out_shape,
                         input_output_aliases=aliases, compiler_params=_params())(*args)
    dins = [None] * nin
    for pos, i in enumerate(kept):
        dins[i] = res[pos]
    pos = len(kept)
    extras = {}
    if has_copy:
        extras["copy16"] = res[pos]
        pos += 1
    if has_into:
        extras["into"] = res[pos]
        pos += 1
    return dins, res[pos:], extras


SEQ_ROWS = 2048


def seqmap(fn, ins, params, nouts, ncol, name, out_dtype=f32):
    bsz, seq, _ = ins[0][0].shape
    r = min(SEQ_ROWS, seq)
    nin, npar = len(ins), len(params)

    def body(*refs):
        in_refs = refs[:nin]
        pv = [p[...] for p in refs[nin:nin + npar]]
        out_refs = refs[nin + npar:]

        def step(i, carry):
            r0 = pl.multiple_of(i * r, r)
            h0 = pl.multiple_of(jnp.maximum(r0 - 8, 0), 8)
            mains = [x[pl.ds(r0, r), :] for x in in_refs]
            halos = [jnp.where(i == 0, 0.0, x[pl.ds(h0, 8), :]) for x in in_refs]
            for o_ref, o in zip(out_refs, fn(mains, halos, pv)):
                o_ref[pl.ds(r0, r), :] = o.astype(o_ref.dtype)
            return carry

        if r == seq:
            for o_ref, o in zip(out_refs, fn([x[...] for x in in_refs], [None] * nin, pv)):
                o_ref[...] = o.astype(o_ref.dtype)
        else:
            lax.fori_loop(0, seq // r, step, 0)

    in_specs = [pl.BlockSpec((None, seq, LANE), functools.partial(lambda j, b, off: (b, 0, off + j), off=off)) for _, off in ins]
    in_specs += [pl.BlockSpec(bs, functools.partial(lambda j, b, f: f(j), f=f)) for _, bs, f in params]
    out_specs = [pl.BlockSpec((None, seq, LANE), lambda j, b: (b, 0, j)) for _ in range(nouts)]
    out_shape = [jax.ShapeDtypeStruct((bsz, seq, LANE * ncol), out_dtype) for _ in range(nouts)]
    return pl.pallas_call(body, name=name, grid=(ncol, bsz), in_specs=in_specs, out_specs=out_specs,
                          out_shape=out_shape, compiler_params=_params())(*[a for a, _ in ins], *[a for a, _, _ in params])


def seqmap_bwd(fn, ins, params, douts, ncol, name, din_dtype=f32, into=None):
    bsz, seq, _ = ins[0][0].shape
    r = min(SEQ_ROWS, seq)
    nin, npar, nout = len(ins), len(params), len(douts)
    narrow = din_dtype != f32

    def body(*refs):
        in_refs = refs[:nin]
        pv = [p[...] for p in refs[nin:nin + npar]]
        dy_refs = refs[nin + npar:nin + npar + nout]
        k0 = nin + npar + nout + (into is not None)
        dout_refs = refs[k0:k0 + nin]
        dp_refs = refs[k0 + nin:k0 + nin + npar]
        din_refs = refs[k0 + nin + npar:] if narrow else dout_refs

        def step(i, dp_acc):
            r0 = pl.multiple_of(i * r, r)
            h0 = pl.multiple_of(jnp.maximum(r0 - 8, 0), 8)
            mains = [x[pl.ds(r0, r), :] for x in in_refs]
            halos_raw = [x[pl.ds(h0, 8), :] for x in in_refs]

            def tile(mains, halos_raw, pv):
                return fn(mains, [jnp.where(i == 0, 0.0, h) for h in halos_raw], pv)

            _, vjp = jax.vjp(tile, mains, halos_raw, pv)
            dm, dh, dp = vjp([d[pl.ds(r0, r), :] for d in dy_refs])
            for d_ref, m, h in zip(din_refs, dm, dh):
                d_ref[pl.ds(r0, r), :] = m
                d_ref[pl.ds(h0, 8), :] += h
            return [acc + d for acc, d in zip(dp_acc, dp)]

        if r == seq:
            _, vjp = jax.vjp(lambda mains, pv: fn(mains, [None] * nin, pv), [x[...] for x in in_refs], pv)
            dm, dp = vjp([d[...] for d in dy_refs])
            for o_ref, m in zip(dout_refs, dm):
                o_ref[...] = m.astype(o_ref.dtype)
        else:
            dp = lax.fori_loop(0, seq // r, step, [jnp.zeros(p.shape, f32) for p in pv])
            if narrow:
                for o_ref, d_ref in zip(dout_refs, din_refs):
                    o_ref[...] = d_ref[...].astype(o_ref.dtype)

        @pl.when(pl.program_id(1) == 0)
        def _():
            for d_ref in dp_refs:
                d_ref[...] = jnp.zeros_like(d_ref)

        for d_ref, d in zip(dp_refs, dp):
            d_ref[...] += d

    in_specs = [pl.BlockSpec((None, seq, LANE), functools.partial(lambda j, b, off: (b, 0, off + j), off=off)) for _, off in ins]
    in_specs += [pl.BlockSpec(bs, functools.partial(lambda j, b, f: f(j), f=f)) for _, bs, f in params]
    in_specs += [pl.BlockSpec((None, seq, LANE), lambda j, b: (b, 0, j)) for _ in range(nout)]
    out_specs = [pl.BlockSpec((None, seq, LANE), lambda j, b: (b, 0, j)) for _ in range(nin)]
    pshapes = [tuple(d for d in bs if d is not None) for _, bs, _ in params]
    out_specs += [pl.BlockSpec((None,) + ps, functools.partial(lambda j, b, nd: (j,) + (0,) * nd, nd=len(ps))) for ps in pshapes]
    out_shape = [jax.ShapeDtypeStruct((bsz, seq, LANE * ncol), din_dtype) for _ in range(nin)]
    out_shape += [jax.ShapeDtypeStruct((ncol,) + ps, f32) for ps in pshapes]
    args = [a for a, _ in ins] + [a for a, _, _ in params] + list(douts)
    aliases = {}
    if into is not None:
        assert nin == 1 and into[0].dtype == din_dtype
        in_specs += [pl.BlockSpec(memory_space=pl.ANY)]
        aliases[len(args)] = 0
        args += [into[0]]
        out_specs[0] = pl.BlockSpec((None, seq, LANE), lambda j, b: (b, 0, into[1] + j))
        out_shape[0] = jax.ShapeDtypeStruct(into[0].shape, din_dtype)
    res = pl.pallas_call(body, name=name, grid=(ncol, bsz), in_specs=in_specs, out_specs=out_specs, out_shape=out_shape,
                         scratch_shapes=[pltpu.VMEM((seq, LANE), f32) for _ in range(nin)] if narrow and r != seq else [],
                         input_output_aliases=aliases, compiler_params=_params())(*args)
    return res[:nin], res[nin:]


def _tri_masks():
    row = lax.broadcasted_iota(jnp.int32, (CHUNK, CHUNK), 0)
    col = lax.broadcasted_iota(jnp.int32, (CHUNK, CHUNK), 1)
    return row >= col, row > col


CHUNKS_PER_STEP = 8


def _by_rows(parts, per_row):
    rows = [jnp.concatenate(parts[i:i + per_row], axis=1) for i in range(0, len(parts), per_row)]
    return jnp.concatenate(rows, axis=0)


def dn_prep(vals, ps):
    q, k, v, gb = vals[:4]
    nchunk = q.shape[0] // CHUNK
    causal, strict = _tri_masks()
    gbs = [gb[c * CHUNK:(c + 1) * CHUNK] for c in range(nchunk)]
    gcs = [cumsum_rows(g) for g in gbs]
    gcts = [g.T for g in gcs]
    chains = [(c, h) for c in range(nchunk) for h in range(DN_HEADS)]
    part = lambda t, c, h: t[c * CHUNK:(c + 1) * CHUNK, h * DN_DK:(h + 1) * DN_DK]
    qh = [part(q, c, h) for c, h in chains]
    kh = [part(k, c, h) for c, h in chains]
    vh = [part(v, c, h) for c, h in chains]
    g_col = [_lane_pick(gcs[c], h) for c, h in chains]
    beta = [_lane_pick(gbs[c], DN_HEADS + h) for c, h in chains]
    g_row = [_row_pick(gcts[c], h)[:, :CHUNK] for c, h in chains]
    decay = [jnp.where(causal, jnp.exp(jnp.where(causal, gc - gr, 0.0)), 0.0) for gc, gr in zip(g_col, g_row)]
    k_beta = [a * b for a, b in zip(kh, beta)]
    eg = [jnp.exp(g) for g in g_col]
    kk = [bdot(a, b, "nt") for a, b in zip(k_beta, kh)]
    qk = [bdot(a, b, "nt") for a, b in zip(qh, kh)]
    lower = [jnp.where(strict, a * d, 0.0) for a, d in zip(kk, decay)]
    if len(vals) == 5:
        t_inv = known_inverse([part(vals[4], c, h)[:, :CHUNK] for c, h in chains], lower)
    else:
        t_inv = unit_lower_inv_all(lower)
    u = [bdot(t, a * b, "nn") for t, a, b in zip(t_inv, vh, beta)]
    w = [bdot(t, a * e, "nn") for t, a, e in zip(t_inv, k_beta, eg)]
    attn = [jnp.concatenate([a * d, jnp.zeros((CHUNK, DN_DK - CHUNK), f32)], axis=1) for a, d in zip(qk, decay)]
    qd = [a * e for a, e in zip(qh, eg)]
    kd = [a * jnp.exp(_row_pick(g, CHUNK - 1) - g) for a, g in zip(kh, g_col)]
    g_last = jnp.concatenate([jnp.broadcast_to(_row_pick(g, CHUNK - 1), g.shape) for g in gcs], axis=0)
    outs = [_by_rows(t, DN_HEADS) for t in (u, w, attn, qd, kd)] + [g_last]
    if len(vals) == 4:
        wide = [jnp.concatenate([t, jnp.zeros((CHUNK, DN_DK - CHUNK), f32)], axis=1) for t in t_inv]
        outs.append(_by_rows(wide, DN_HEADS))
    return outs


def dn_step(state, u, w, attn, qd, kd, g_last):
    bsz = u.shape[0]
    chains = [(b, h) for b in range(bsz) for h in range(DN_HEADS)]
    part = lambda t, b, h: t[b, :, h * DN_DK:(h + 1) * DN_DK]
    ws = [bdot(part(w, b, h), s, "nn") for (b, h), s in zip(chains, state)]
    qs = [bdot(part(qd, b, h), s, "nn") for (b, h), s in zip(chains, state)]
    v_new = [part(u, b, h) - x for (b, h), x in zip(chains, ws)]
    av = [bdot(attn[b, :, h * DN_DK:h * DN_DK + CHUNK], x, "nn") for (b, h), x in zip(chains, v_new)]
    kv = [bdot(part(kd, b, h), x, "tn") for (b, h), x in zip(chains, v_new)]
    ge = [jnp.exp(_row_pick(_lane_pick(g_last[b], h), 0)) for b, h in chains]
    new_state = [s * g + x for s, g, x in zip(state, ge, kv)]
    outs = [a + b for a, b in zip(qs, av)]
    return new_state, jnp.concatenate([jnp.concatenate(outs[b * DN_HEADS:(b + 1) * DN_HEADS], axis=1)[None]
                                       for b in range(bsz)], axis=0)


def _ret_log_gamma(h):
    return math.log(1.0 - 2.0 ** (-5.0 - h))


def ret_prep(vals, ps):
    q, k, v = vals
    nchunk = q.shape[0] // CHUNK
    causal, _ = _tri_masks()
    row = lax.broadcasted_iota(jnp.int32, (CHUNK, CHUNK), 0)
    col = lax.broadcasted_iota(jnp.int32, (CHUNK, CHUNK), 1)
    dist = (row - col).astype(f32)
    lane = lax.broadcasted_iota(jnp.int32, (CHUNK, q.shape[1]), 1)
    dmask = [jnp.where(causal, jnp.exp(jnp.where(causal, dist, 0.0) * _ret_log_gamma(h)), 0.0) for h in range(RET_HEADS)]
    chains = [(c, h) for c in range(nchunk) for h in range(RET_HEADS)]
    rows = lambda t, c: t[c * CHUNK:(c + 1) * CHUNK]
    scores = [bdot(jnp.where((lane // RET_DK) == h, rows(q, c), 0.0), rows(k, c), "nt") * dmask[h] for c, h in chains]
    inner = [bdot(s, rows(v, c)[:, h * RET_DV:(h + 1) * RET_DV], "nn") for s, (c, h) in zip(scores, chains)]
    return [_by_rows(inner, RET_HEADS)]


def ret_step(state, q, k, v, inner):
    bsz = q.shape[0]
    idx = lax.broadcasted_iota(jnp.int32, (CHUNK, 1), 0).astype(f32)
    lane = lax.broadcasted_iota(jnp.int32, (CHUNK, q.shape[2]), 1)
    chains = [(b, h) for b in range(bsz) for h in range(RET_HEADS)]
    part = lambda t, b, h: t[b, :, h * RET_DV:(h + 1) * RET_DV]
    cross = [bdot(q[b], s, "nn") for (b, h), s in zip(chains, state)]
    kz = [jnp.where((lane // RET_DK) == h, k[b], 0.0) * jnp.exp((CHUNK - 1.0 - idx) * _ret_log_gamma(h)) for b, h in chains]
    kv = [bdot(a, part(v, b, h), "tn") for a, (b, h) in zip(kz, chains)]
    outs = [x * jnp.exp((idx + 1.0) * _ret_log_gamma(h)) + part(inner, b, h) for x, (b, h) in zip(cross, chains)]
    new_state = [s * math.exp(CHUNK * _ret_log_gamma(h)) + x for s, x, (b, h) in zip(state, kv, chains)]
    return new_state, jnp.concatenate([jnp.concatenate(outs[b * RET_HEADS:(b + 1) * RET_HEADS], axis=1)[None]
                                       for b in range(bsz)], axis=0)


SCAN_CHUNKS = 8


def chunk_scan(step_fn, ins, state_shape, out_width, name):
    bsz, seq, _ = ins[0].shape
    nchunk = seq // CHUNK
    nin = len(ins)
    nh = state_shape[0]
    per = SCAN_CHUNKS if nchunk % SCAN_CHUNKS == 0 else 1

    def body(*refs):
        in_refs = refs[:nin]
        o_ref, ck_ref, s_ref = refs[nin:]

        @pl.when(pl.program_id(0) == 0)
        def _():
            s_ref[...] = jnp.zeros_like(s_ref)

        state = [s_ref[i] for i in range(bsz * nh)]
        for c in range(per):
            rows = slice(c * CHUNK, (c + 1) * CHUNK)
            for i in range(bsz * nh):
                ck_ref[i // nh, c, i % nh] = state[i]
            state, out = step_fn(state, *[x[:, rows, :].astype(f32) for x in in_refs])
            o_ref[:, rows, :] = out
        for i in range(bsz * nh):
            s_ref[i] = state[i]

    in_specs = [pl.BlockSpec((bsz, per * CHUNK, x.shape[2]), lambda n: (0, n, 0)) for x in ins]
    out_specs = [pl.BlockSpec((bsz, per * CHUNK, out_width), lambda n: (0, n, 0)),
                 pl.BlockSpec((bsz, per) + tuple(state_shape), lambda n: (0, n, 0, 0, 0))]
    out_shape = [jax.ShapeDtypeStruct((bsz, seq, out_width), f32),
                 jax.ShapeDtypeStruct((bsz, nchunk) + tuple(state_shape), f32)]
    return pl.pallas_call(body, name=name, grid=(nchunk // per,), in_specs=in_specs, out_specs=out_specs, out_shape=out_shape,
                          scratch_shapes=[pltpu.VMEM((bsz * nh,) + tuple(state_shape[1:]), f32)],
                          compiler_params=_params())(*ins)


def chunk_scan_bwd(step_fn, ins, ckpt, dout, name):
    bsz, seq, _ = ins[0].shape
    nchunk = seq // CHUNK
    nin = len(ins)
    state_shape = ckpt.shape[2:]
    nh = state_shape[0]
    per = SCAN_CHUNKS if nchunk % SCAN_CHUNKS == 0 else 1
    nstep = nchunk // per

    def body(*refs):
        in_refs = refs[:nin]
        ck_ref, do_ref = refs[nin:nin + 2]
        din_refs = refs[nin + 2:nin + 2 + nin]
        ds_ref = refs[-1]

        @pl.when(pl.program_id(0) == 0)
        def _():
            ds_ref[...] = jnp.zeros_like(ds_ref)

        dstate = [ds_ref[i] for i in range(bsz * nh)]
        for c in reversed(range(per)):
            rows = slice(c * CHUNK, (c + 1) * CHUNK)
            state = [ck_ref[i // nh, c, i % nh] for i in range(bsz * nh)]
            _, vjp = jax.vjp(step_fn, state, *[x[:, rows, :].astype(f32) for x in in_refs])
            grads = vjp((dstate, do_ref[:, rows, :]))
            dstate = grads[0]
            for d_ref, d in zip(din_refs, grads[1:]):
                d_ref[:, rows, :] = d
        for i in range(bsz * nh):
            ds_ref[i] = dstate[i]

    rev = lambda n: (0, nstep - 1 - n, 0)
    in_specs = [pl.BlockSpec((bsz, per * CHUNK, x.shape[2]), rev) for x in ins]
    in_specs += [pl.BlockSpec((bsz, per) + tuple(state_shape), lambda n: (0, nstep - 1 - n, 0, 0, 0)),
                 pl.BlockSpec((bsz, per * CHUNK, dout.shape[2]), rev)]
    out_specs = [pl.BlockSpec((bsz, per * CHUNK, x.shape[2]), rev) for x in ins]
    out_shape = [jax.ShapeDtypeStruct(x.shape, f32) for x in ins]
    return pl.pallas_call(body, name=name, grid=(nstep,), in_specs=in_specs, out_specs=out_specs, out_shape=out_shape,
                          scratch_shapes=[pltpu.VMEM((bsz * nh,) + tuple(state_shape[1:]), f32)],
                          compiler_params=_params())(*ins, ckpt, dout)


LRU_ROWS = 1024


def lru_scan(a, b):
    bsz, seq, width = a.shape
    rb = min(LRU_ROWS, seq)
    seqs = range(bsz)

    def body(a_ref, b_ref, h_ref, hp_ref, carry_ref):
        @pl.when(pl.program_id(0) == 0)
        def _():
            carry_ref[...] = jnp.zeros_like(carry_ref)

        row = lax.broadcasted_iota(jnp.int32, (8, width), 0)

        def tile(t, hprev):
            r0 = pl.multiple_of(t * 8, 8)
            ca = [a_ref[i, pl.ds(r0, 8), :] for i in seqs]
            cb = [b_ref[i, pl.ds(r0, 8), :] for i in seqs]
            for s in (1, 2, 4):
                m = row >= s
                cb = [jnp.where(m, x * pltpu.roll(y, s, 0) + y, y) for x, y in zip(ca, cb)]
                ca = [jnp.where(m, x * pltpu.roll(x, s, 0), x) for x in ca]
            h = [y + x * p for x, y, p in zip(ca, cb, hprev)]
            for i in seqs:
                h_ref[i, pl.ds(r0, 8), :] = h[i]
                hp_ref[i, pl.ds(r0, 8), :] = jnp.where(row == 0, hprev[i], pltpu.roll(h[i], 1, 0))
            return tuple(_row_pick(x, 7) for x in h)

        last = lax.fori_loop(0, rb // 8, tile, tuple(carry_ref[i:i + 1, :] for i in seqs))
        for i in seqs:
            carry_ref[i:i + 1, :] = last[i]

    spec = pl.BlockSpec((bsz, rb, width), lambda i: (0, i, 0))
    return pl.pallas_call(body, name="lru_scan", grid=(seq // rb,), in_specs=[spec, spec], out_specs=[spec, spec],
                          out_shape=[jax.ShapeDtypeStruct(a.shape, f32)] * 2,
                          scratch_shapes=[pltpu.VMEM((max(8, bsz), width), f32)], compiler_params=_params())(a, b)


def lru_scan_bwd(a, hp, dh):
    bsz, seq, width = a.shape
    rb = min(LRU_ROWS, seq)
    nblk = seq // rb
    seqs = range(bsz)

    def body(a_ref, hp_ref, dh_ref, da_ref, db_ref, carry_ref):
        @pl.when(pl.program_id(0) == 0)
        def _():
            carry_ref[...] = jnp.zeros_like(carry_ref)

        row = lax.broadcasted_iota(jnp.int32, (8, width), 0)
        ntile = rb // 8

        def tile(t, mu_next):
            r0 = pl.multiple_of((ntile - 1 - t) * 8, 8)
            ca = [a_ref[i, pl.ds(r0, 8), :] for i in seqs]
            dh_t = [dh_ref[i, pl.ds(r0, 8), :] for i in seqs]
            cb = [x * y for x, y in zip(ca, dh_t)]
            for s in (1, 2, 4):
                m = row < 8 - s
                cb = [jnp.where(m, x * pltpu.roll(y, 8 - s, 0) + y, y) for x, y in zip(ca, cb)]
                ca = [jnp.where(m, x * pltpu.roll(x, 8 - s, 0), x) for x in ca]
            mu = [y + x * p for x, y, p in zip(ca, cb, mu_next)]
            for i in seqs:
                lam = dh_t[i] + jnp.where(row == 7, mu_next[i], pltpu.roll(mu[i], 7, 0))
                db_ref[i, pl.ds(r0, 8), :] = lam
                da_ref[i, pl.ds(r0, 8), :] = lam * hp_ref[i, pl.ds(r0, 8), :]
            return tuple(_row_pick(x, 0) for x in mu)

        last = lax.fori_loop(0, ntile, tile, tuple(carry_ref[i:i + 1, :] for i in seqs))
        for i in seqs:
            carry_ref[i:i + 1, :] = last[i]

    spec = pl.BlockSpec((bsz, rb, width), lambda i: (0, nblk - 1 - i, 0))
    return pl.pallas_call(body, name="lru_scan_bwd", grid=(nblk,), in_specs=[spec] * 3, out_specs=[spec, spec],
                          out_shape=[jax.ShapeDtypeStruct(a.shape, f32)] * 2,
                          scratch_shapes=[pltpu.VMEM((max(8, bsz), width), f32)], compiler_params=_params())(a, hp, dh)


MERGE_ROWS = 512


def branch_merge(ys, w_branch, u):
    n = ys[0].shape[0]
    tm = min(MERGE_ROWS, n)

    def body(ya, yb, yc, w_ref, g0, g1, g2, o_ref):
        acc = None
        for i, (y_ref, g_ref) in enumerate(((ya, g0), (yb, g1), (yc, g2))):
            term = jax.nn.sigmoid(g_ref[...]) * _dg(y_ref[...], w_ref[i], "nn")
            acc = term if acc is None else acc + term
        o_ref[...] = acc.astype(o_ref.dtype)

    y_spec = pl.BlockSpec((tm, ys[0].shape[1]), lambda i: (i, 0))
    g_specs = [pl.BlockSpec((tm, D_MODEL), functools.partial(lambda i, k: (i, k), k=k)) for k in range(3)]
    return pl.pallas_call(
        body, name="branch_merge", grid=(n // tm,),
        in_specs=[y_spec] * 3 + [pl.BlockSpec(w_branch.shape, lambda i: (0, 0, 0))] + g_specs,
        out_specs=pl.BlockSpec((tm, D_MODEL), lambda i: (i, 0)), out_shape=jax.ShapeDtypeStruct((n, D_MODEL), bf16),
        compiler_params=_params())(*ys, w_branch, u, u, u)


def branch_merge_bwd(ys, w_branch, u, d_merged, du):
    n = ys[0].shape[0]
    tm = min(MERGE_ROWS, n)

    def body(ya, yb, yc, w_ref, g0, g1, g2, dm_ref, du_in, db0, db1, db2, du_ref):
        dm = dm_ref[...]
        d_gates = []
        for i, (y_ref, g_ref, db_ref) in enumerate(((ya, g0, db0), (yb, g1, db1), (yc, g2, db2))):
            s = jax.nn.sigmoid(g_ref[...])
            db_ref[...] = (dm * s).astype(db_ref.dtype)
            d_gates.append(dm * _dg(y_ref[...], w_ref[i], "nn") * (s * (1.0 - s)))
        du_ref[...] = jnp.concatenate(d_gates, axis=1).astype(du_ref.dtype)

    y_spec = pl.BlockSpec((tm, ys[0].shape[1]), lambda i: (i, 0))
    row = pl.BlockSpec((tm, D_MODEL), lambda i: (i, 0))
    g_specs = [pl.BlockSpec((tm, D_MODEL), functools.partial(lambda i, k: (i, k), k=k)) for k in range(3)]
    res = pl.pallas_call(
        body, name="branch_merge_bwd", grid=(n // tm,),
        in_specs=[y_spec] * 3 + [pl.BlockSpec(w_branch.shape, lambda i: (0, 0, 0))] + g_specs + [row, pl.BlockSpec(memory_space=pl.ANY)],
        out_specs=[row] * 3 + [pl.BlockSpec((tm, 3 * D_MODEL), lambda i: (i, 0))],
        out_shape=[jax.ShapeDtypeStruct((n, D_MODEL), bf16)] * 3 + [jax.ShapeDtypeStruct(du.shape, du.dtype)],
        input_output_aliases={8: 3}, compiler_params=_params())(*ys, w_branch, u, u, u, d_merged, du)
    return list(res[:3]), res[3]


def final_loss(x, g, target):
    n, d = x.shape
    r = min(256, n)

    def body(x_ref, g_ref, t_ref, loss_ref, dx_ref, dg_ref, dx16_ref):
        @pl.when(pl.program_id(0) == 0)
        def _():
            loss_ref[...] = jnp.zeros_like(loss_ref)
            dg_ref[...] = jnp.zeros_like(dg_ref)

        tgt = t_ref[...]

        def loss_fn(xv, gv):
            y = f_norm([xv], [gv])[0]
            return 0.5 * jnp.sum(jnp.mean(jnp.square(y - tgt), axis=-1, keepdims=True), axis=0, keepdims=True)

        val, vjp = jax.vjp(loss_fn, x_ref[...], g_ref[...])
        dx, dg = vjp(jnp.ones_like(val))
        loss_ref[...] += val
        dx_ref[...] = dx
        dx16_ref[...] = dx.astype(dx16_ref.dtype)
        dg_ref[...] += dg

    row = pl.BlockSpec((r, d), lambda i: (i, 0))
    return pl.pallas_call(
        body, name="final_loss", grid=(n // r,), in_specs=[row, pl.BlockSpec((1, d), lambda i: (0, 0)), row],
        out_specs=[pl.BlockSpec((8, LANE), lambda i: (0, 0)), row, pl.BlockSpec((1, d), lambda i: (0, 0)), row],
        out_shape=[jax.ShapeDtypeStruct((8, LANE), f32), jax.ShapeDtypeStruct((n, d), f32), jax.ShapeDtypeStruct((1, d), f32),
                   jax.ShapeDtypeStruct((n, d), bf16)],
        compiler_params=_params())(x, g, target)


_HBM = pl.BlockSpec(memory_space=pltpu.HBM)
_SEM = pl.BlockSpec(memory_space=pltpu.SEMAPHORE)
_EFFECT = pltpu.SideEffectType.DATAFLOW_SIDE_EFFECTING


def _peer(k):
    mx, my, mc = lax.axis_index("x"), lax.axis_index("y"), lax.axis_index("c")
    px, py, pc = (mx + (k >> 2)) % 2, (my + ((k >> 1) & 1)) % 2, (mc + (k & 1)) % 2
    return (px, py, pc), 4 * px + 2 * py + pc


def _peer_copy(k, i, x_ref, land_ref, send_sems, recv_sems, scatter):
    me = 4 * lax.axis_index("x") + 2 * lax.axis_index("y") + lax.axis_index("c")
    dev, slot = _peer(k)
    sem = i * (N_DEV - 1) + k - 1
    return pltpu.make_async_remote_copy(
        src_ref=x_ref.at[slot] if scatter else x_ref, dst_ref=land_ref.at[me], send_sem=send_sems.at[sem],
        recv_sem=recv_sems.at[sem], device_id=dev, device_id_type=pl.DeviceIdType.MESH)


def _own_copy(i, x_ref, land_ref, own_sems, scatter):
    me = 4 * lax.axis_index("x") + 2 * lax.axis_index("y") + lax.axis_index("c")
    return pltpu.make_async_copy(x_ref.at[me] if scatter else x_ref, land_ref.at[me], own_sems.at[i])


def exchange_start(xs, scatters, name):
    nx = len(xs)
    lands = [lax.empty((N_DEV,) + tuple(x.shape[1:] if sc else x.shape), x.dtype) for x, sc in zip(xs, scatters)]
    nsem = nx * (N_DEV - 1)

    def body(*refs):
        x_refs, land_refs = refs[:nx], refs[nx:2 * nx]
        send_sems, recv_sems, own_sems = refs[2 * nx:2 * nx + 3]
        token = refs[-1]
        for i in range(nx):
            for k in range(1, N_DEV):
                _peer_copy(k, i, x_refs[i], land_refs[i], send_sems, recv_sems, scatters[i]).start()
            _own_copy(i, x_refs[i], land_refs[i], own_sems, scatters[i]).start()
        token[...] = jnp.zeros_like(token)

    hbm = lambda a: pltpu.HBM(a.shape, a.dtype)
    res = pl.pallas_call(
        body, name=name, in_specs=(_HBM,) * (2 * nx),
        out_specs=(_SEM, _SEM, _SEM) + (_HBM,) * (2 * nx) + (pl.BlockSpec(memory_space=pltpu.VMEM),),
        input_output_aliases={i: 3 + i for i in range(2 * nx)},
        out_shape=(pltpu.SemaphoreType.DMA((nsem,)), pltpu.SemaphoreType.DMA((nsem,)), pltpu.SemaphoreType.DMA((nx,)),
                   *[hbm(a) for a in xs], *[hbm(a) for a in lands], jax.ShapeDtypeStruct((8, LANE), f32)),
        compiler_params=pltpu.CompilerParams(has_side_effects=_EFFECT),
    )(*[pltpu.with_memory_space_constraint(a, pltpu.HBM) for a in list(xs) + lands])
    return (res[0], res[1], res[2], list(res[3:3 + nx]), list(res[3 + nx:3 + 2 * nx]), tuple(scatters)), res[-1]


def exchange_wait(started, after, name):
    send_sems, recv_sems, own_sems, x_thrus, land_thrus, scatters = started
    nx = len(x_thrus)

    def body(*refs):
        x_refs, land_refs = refs[:nx], refs[nx:2 * nx]
        send_sems, recv_sems, own_sems = refs[2 * nx:2 * nx + 3]
        for i in range(nx):
            for k in range(1, N_DEV):
                cp = _peer_copy(k, i, x_refs[i], land_refs[i], send_sems, recv_sems, scatters[i])
                cp.wait_send()
                cp.wait_recv()
            _own_copy(i, x_refs[i], land_refs[i], own_sems, scatters[i]).wait()

    hbm = lambda a: pltpu.HBM(a.shape, a.dtype)
    res = pl.pallas_call(
        body, name=name, in_specs=(_HBM,) * (2 * nx) + (_SEM, _SEM, _SEM, pl.BlockSpec(memory_space=pl.ANY)),
        out_specs=(_HBM,) * (2 * nx), input_output_aliases={i: i for i in range(2 * nx)},
        out_shape=tuple(hbm(a) for a in list(x_thrus) + list(land_thrus)),
        compiler_params=pltpu.CompilerParams(has_side_effects=_EFFECT),
    )(*x_thrus, *land_thrus, send_sems, recv_sems, own_sems, after)
    return list(res[nx:])


def sum_slots(x, name):
    _, rows_total, cols = x.shape
    row_bytes = N_DEV * ((cols + LANE - 1) // LANE) * LANE * x.dtype.itemsize
    r = _pick_rows(rows_total, max(16, (4 * 1024 * 1024) // row_bytes // 16 * 16))

    def body(x_ref, o_ref):
        acc = x_ref[0].astype(f32)
        for s in range(1, N_DEV):
            acc = acc + x_ref[s].astype(f32)
        o_ref[...] = acc

    return pl.pallas_call(body, name=name, grid=(rows_total // r,),
                          in_specs=[pl.BlockSpec((N_DEV, r, cols), lambda i: (0, i, 0))],
                          out_specs=pl.BlockSpec((r, cols), lambda i: (i, 0)),
                          out_shape=jax.ShapeDtypeStruct((rows_total, cols), f32), compiler_params=_params())(x)


def _pick_rows(total, pref):
    best = None
    for d in range(16, min(total, pref) + 1, 16):
        if total % d == 0:
            best = d
    return best if best is not None else total


def adamw(w, g, m, v, name):
    shape = w.shape
    view = (1,) * (3 - w.ndim) + shape if w.ndim < 3 else (math.prod(shape[:-2]),) + shape[-2:]
    w2, g2, m2, v2 = (t.reshape(view) for t in (w, g, m, v))
    lead, rows_total, cols = view
    r = _pick_rows(rows_total, max(16, (512 * 1024) // max(cols, 1) // 16 * 16))
    c1, c2 = 1.0 / (1.0 - ADAM_B1 ** ADAM_STEP), 1.0 / (1.0 - ADAM_B2 ** ADAM_STEP)

    def body(w_ref, g_ref, m_ref, v_ref, d_ref, nm_ref, nv_ref):
        gv = g_ref[...]
        nm = ADAM_B1 * m_ref[...] + (1.0 - ADAM_B1) * gv
        nv = ADAM_B2 * v_ref[...] + (1.0 - ADAM_B2) * jnp.square(gv)
        d_ref[...] = -ADAM_LR * ((nm * c1) / (jnp.sqrt(nv * c2) + ADAM_EPS) + ADAM_WD * w_ref[...])
        nm_ref[...] = nm
        nv_ref[...] = nv

    padded = ((r + 7) // 8) * 8 * ((cols + LANE - 1) // LANE) * LANE * 4
    lb = max(d for d in range(1, lead + 1) if lead % d == 0 and d * padded <= max(padded, 1024 * 1024))
    spec = pl.BlockSpec((lb, r, cols), lambda l, i: (l, i, 0))
    outs = pl.pallas_call(body, name=name, grid=(lead // lb, rows_total // r), in_specs=[spec] * 4, out_specs=[spec] * 3,
                          out_shape=[jax.ShapeDtypeStruct(view, f32)] * 3, compiler_params=_params())(w2, g2, m2, v2)
    return tuple(o.reshape(shape) for o in outs)


def _layer_fwd(x, wl, fetch_rest, cos, sin, bsz, seq):
    n = x.shape[0]
    sv = {"x_in": x}
    row1 = lambda a: (a, (1, a.shape[1]), lambda j: (0, 0))
    h = rowmap(f_norm, [(x, D_MODEL, 0)], [row1(wl["attn_norm"])], [(D_MODEL, bf16)], 1, "norm_fwd")[0]
    u = mm(h, wl["w_in"], "nn", "mm_in", tn=2048)
    sv["h"], sv["u"] = h, u
    u3 = u.reshape(bsz, seq, U_PAD)
    wl = dict(wl)
    wl.update(fetch_rest(u))
    sv["wl"] = wl

    qkv = []
    for kind in range(3):
        cw = (wl["dn_conv_w"], (4, LANE), functools.partial(lambda j, kind: (0, 4 * kind + j), kind=kind))
        qkv.append(seqmap(functools.partial(f_dn_pre, kind), [(u3, U_QKV // LANE + 4 * kind)], [cw], 1, 4, "dn_pre%d" % kind)[0])
    gb = rowmap(f_dn_gates, [(u, 512, U_AB // 512)], [(wl["dn_gate_p"], (8, LANE), lambda j: (0, 0))], [(LANE, f32)], 1,
                "dn_gates", rows=512)[0]
    gb3 = gb.reshape(bsz, seq, LANE)
    crow = CHUNK * CHUNKS_PER_STEP
    dn_in = [(t.reshape(n, 512), 512, 0) for t in qkv] + [(gb, LANE, 0)]
    prep_a = rowmap(dn_prep, dn_in, [], [(512, f32)] + [(512, bf16)] * 4 + [(LANE, f32), (512, f32)], 1, "dn_prep", rows=crow)
    dn_in = dn_in + [(prep_a[6], 512, 0)]
    prep_a = [t.reshape(bsz, seq, t.shape[1]) for t in prep_a[:6]]
    o_a, ck_a = chunk_scan(dn_step, prep_a, (DN_HEADS, DN_DK, DN_DK), 512, "dn_scan")
    y_a = rowmap(per_head(f_dn_post), [(o_a.reshape(n, 512), 512, 0), (u, 512, U_Z // 512)],
                 [(wl["dn_norm_w"], (1, LANE), lambda j: (0, 0))], [(512, bf16)], 1, "dn_post")[0]
    sv.update(dn_in=dn_in, prep_a=prep_a, o_a=o_a, ck_a=ck_a, y_a=y_a)

    q_b, k_b = rowmap(f_ret_pre, [(u, 256, U_RQ // 256), (u, 256, U_RK // 256), (cos, 256, 0), (sin, 256, 0)], [],
                      [(256, f32), (256, f32)], 1, "ret_pre")
    q_b3, k_b3 = q_b.reshape(bsz, seq, 256), k_b.reshape(bsz, seq, 256)
    v_b3 = lax.slice_in_dim(u3, U_RV, U_RV + 512, axis=2)
    ret_in = [(q_b, 256, 0), (k_b, 256, 0), (u, 512, U_RV // 512)]
    inner = rowmap(ret_prep, ret_in, [], [(512, f32)], 1, "ret_prep", rows=crow)[0]
    ret_seq = [q_b3, k_b3, v_b3, inner.reshape(bsz, seq, 512)]
    o_b, ck_b = chunk_scan(ret_step, ret_seq, (RET_HEADS, 256, RET_DV), 512, "ret_scan")
    y_b = rowmap(per_head(f_ret_post), [(o_b.reshape(n, 512), 512, 0), (u, 512, U_RG // 512)], [], [(512, bf16)], 1,
                 "ret_post")[0]
    sv.update(ret_in=ret_in, ret_seq=ret_seq, o_b=o_b, ck_b=ck_b, y_b=y_b)

    lru_params = _lru_params(wl)
    a_c, b_c = seqmap(f_lru_pre, [(u3, U_CX // LANE)], lru_params, 2, 4, "lru_pre")
    h_c, hp_c = lru_scan(a_c, b_c)
    y_c = rowmap(f_lru_post, [(h_c.reshape(n, 512), 512, 0), (u, 512, U_CG // 512)], [], [(512, bf16)], 1, "lru_post")[0]
    sv.update(a_c=a_c, hp_c=hp_c, h_c=h_c, y_c=y_c)

    merged = branch_merge((y_a, y_b, y_c), wl["w_branch"], u)
    x_mid, h2 = mm_add_norm(merged, wl["w_out"], x, wl["ffn_norm"], "mm_out_norm")
    sv.update(merged=merged, x_mid=x_mid)

    up = mm(h2, wl["w_up"], "nn", "mm_up", tn=2816)
    act = seqmap(f_ffn_mid, [(up.reshape(bsz, seq, 2 * D_FF), 0), (up.reshape(bsz, seq, 2 * D_FF), D_FF // LANE)],
                 _ffn_params(wl), 1, D_FF // LANE, "ffn_mid", out_dtype=bf16)[0]
    act = act.reshape(n, D_FF)
    x_out = mm(act, wl["w_down"], "nn", "mm_down", add=x_mid)
    sv.update(h2=h2, up=up, act=act)
    return x_out, sv


def _lru_params(wl):
    col = lambda a: (a, (a.shape[0], LANE), lambda j: (0, j))
    blk = lambda a: (a, (None, LANE, LANE), lambda j: (j, 0, 0))
    return [col(wl["lru_conv_w"]), col(wl["lru_conv_b"]), blk(wl["lru_wa"]), col(wl["lru_ba"]), blk(wl["lru_wx"]),
            col(wl["lru_bx"]), col(wl["lru_lambda"])]


def _ffn_params(wl):
    nb = D_FF // LANE
    return [(wl["ffn_conv_w"], (3, LANE), lambda j: (0, j)), (wl["ffn_conv_w"], (3, LANE), lambda j: (0, nb + j)),
            (wl["ffn_conv_b"], (1, LANE), lambda j: (0, j)), (wl["ffn_conv_b"], (1, LANE), lambda j: (0, nb + j))]


def _layer_bwd(dx, dx16, sv, cos, sin, bsz, seq, emit, dep):
    n = dx.shape[0]
    gr = {}
    wl = sv["wl"]
    u, x_in, x_mid = sv["u"], sv["x_in"], sv["x_mid"]
    u3 = u.reshape(bsz, seq, U_PAD)
    row1 = lambda a: (a, (1, a.shape[1]), lambda j: (0, 0))

    d_act = mm(dx16, wl["w_down"], "nt", "mm_down_dx", dep=dep)
    gr["w_down"] = mm(sv["act"], dx16, "tn", "mm_down_dw")
    up3 = sv["up"].reshape(bsz, seq, 2 * D_FF)
    (d_gate, d_val), dps = seqmap_bwd(f_ffn_mid, [(up3, 0), (up3, D_FF // LANE)], _ffn_params(wl),
                                      [d_act.reshape(bsz, seq, D_FF)], D_FF // LANE, "ffn_mid_bwd", din_dtype=bf16)
    gr["ffn_conv_w"] = jnp.concatenate([_cols(dps[0]), _cols(dps[1])], axis=1)
    gr["ffn_conv_b"] = jnp.concatenate([_cols(dps[2]), _cols(dps[3])], axis=1)[0]
    d_gate, d_val = d_gate.reshape(n, D_FF), d_val.reshape(n, D_FF)
    gr["w_up"] = (mm(sv["h2"], d_gate, "tn", "mm_up_dw"), mm(sv["h2"], d_val, "tn", "mm_up_dw"))
    token = emit("ffn", {k: gr[k] for k in ("w_up", "w_down")})
    d_h2 = mm(d_gate, wl["w_up"], "nt", "mm_up_dx", dep=token)
    d_h2 = mm(d_val, wl["w_up"], "nt", "mm_up_dx", add=d_h2, b_koff=1)
    (dx_mid,), (dg,), ex = rowmap_bwd(f_norm, [(x_mid, D_MODEL, 0)], [row1(wl["ffn_norm"])], [d_h2], 1, "norm_bwd", add=[dx],
                                      copy16=0)
    dx_mid16 = ex["copy16"]
    gr["ffn_norm"] = dg[0, 0]

    du = lax.empty((n, U_PAD), bf16)
    du3 = lambda: du.reshape(bsz, seq, U_PAD)

    d_merged = mm(dx_mid16, wl["w_out"], "nt", "mm_out_dx")
    gr["w_out"] = mm(sv["merged"], dx_mid16, "tn", "mm_out_dw")
    ys = (sv["y_a"], sv["y_b"], sv["y_c"])
    d_br, du = branch_merge_bwd(ys, wl["w_branch"], u, d_merged, du)
    d_ys = [mm(d_br[i], wl["w_branch"][i], "nt", "mm_branch_dx") for i in range(3)]
    gr["w_branch"] = jnp.stack([mm(ys[i], d_br[i], "tn", "mm_branch_dw") for i in range(3)])
    mix_token = emit("mix", {k: gr[k] for k in ("w_branch", "w_out")})

    (d_hc, _), _, ex = rowmap_bwd(f_lru_post, [(sv["h_c"].reshape(n, 512), 512, 0), (u, 512, U_CG // 512)], [], [d_ys[2]], 1,
                                  "lru_post_bwd", into=(du, U_CG // 512, [1]))
    du = ex["into"]
    d_a, d_b = lru_scan_bwd(sv["a_c"], sv["hp_c"], d_hc.reshape(bsz, seq, 512))
    (du_new,), dps = seqmap_bwd(f_lru_pre, [(u3, U_CX // LANE)], _lru_params(wl), [d_a, d_b], 4, "lru_pre_bwd", din_dtype=bf16,
                                into=(du3(), U_CX // LANE))
    du = du_new.reshape(n, U_PAD)
    gr["lru_conv_w"], gr["lru_conv_b"] = _cols(dps[0]), _cols(dps[1])[0]
    gr["lru_wa"], gr["lru_ba"], gr["lru_wx"], gr["lru_bx"] = dps[2], dps[3][:, 0], dps[4], dps[5][:, 0]
    gr["lru_lambda"] = _cols(dps[6])[0]

    (d_ob, _), _, ex = rowmap_bwd(per_head(f_ret_post), [(sv["o_b"].reshape(n, 512), 512, 0), (u, 512, U_RG // 512)], [],
                                  [d_ys[1]], 1, "ret_post_bwd", into=(du, U_RG // 512, [1]))
    du = ex["into"]
    crow = CHUNK * CHUNKS_PER_STEP
    d_ret = chunk_scan_bwd(ret_step, sv["ret_seq"], sv["ck_b"], d_ob.reshape(bsz, seq, 512), "ret_scan_bwd")
    d_ret = [t.reshape(n, t.shape[2]) for t in d_ret]
    (d_qb, d_kb, _), _, ex = rowmap_bwd(ret_prep, sv["ret_in"], [], [d_ret[3]], 1, "ret_prep_bwd", rows=crow, add=d_ret[:3],
                                        into=(du, U_RV // 512, [2]))
    du = ex["into"]
    _, _, ex = rowmap_bwd(f_ret_pre, [(u, 256, U_RQ // 256), (u, 256, U_RK // 256), (cos, 256, 0), (sin, 256, 0)], [],
                          [d_qb, d_kb], 1, "ret_pre_bwd", din_dtypes=[f32, f32, None, None], into=(du, U_RQ // 512, [0, 1]))
    du = ex["into"]

    (d_oa, _), (dnw,), ex = rowmap_bwd(per_head(f_dn_post), [(sv["o_a"].reshape(n, 512), 512, 0), (u, 512, U_Z // 512)],
                                       [(wl["dn_norm_w"], (1, LANE), lambda j: (0, 0))], [d_ys[0]], 1, "dn_post_bwd",
                                       into=(du, U_Z // 512, [1]))
    du = ex["into"]
    gr["dn_norm_w"] = dnw[0, 0]
    d_prep = chunk_scan_bwd(dn_step, sv["prep_a"], sv["ck_a"], d_oa.reshape(bsz, seq, 512), "dn_scan_bwd")
    (d_q, d_k, d_v, d_gb, _), _, _ = rowmap_bwd(dn_prep, sv["dn_in"], [], [t.reshape(n, t.shape[2]) for t in d_prep], 1,
                                                "dn_prep_bwd", rows=crow, din_dtypes=[f32] * 4 + [None])
    d_q, d_k, d_v = (t.reshape(bsz, seq, 512) for t in (d_q, d_k, d_v))
    _, (dgp,), ex = rowmap_bwd(f_dn_gates, [(u, 512, U_AB // 512)], [(wl["dn_gate_p"], (8, LANE), lambda j: (0, 0))],
                               [d_gb], 1, "dn_gates_bwd", rows=512, into=(du, U_AB // 512, [0]))
    du = ex["into"]
    gr["dn_a_log"], gr["dn_dt_bias"] = dgp[0, 0, :DN_HEADS], dgp[0, 1, :DN_HEADS]
    d_cw = []
    for kind, d_t in enumerate((d_q, d_k, d_v)):
        cw = (wl["dn_conv_w"], (4, LANE), functools.partial(lambda j, kind: (0, 4 * kind + j), kind=kind))
        (du_new,), (dcw,) = seqmap_bwd(functools.partial(f_dn_pre, kind), [(u3, U_QKV // LANE + 4 * kind)], [cw], [d_t], 4,
                                       "dn_pre%d_bwd" % kind, din_dtype=bf16, into=(du3(), U_QKV // LANE + 4 * kind))
        du = du_new.reshape(n, U_PAD)
        d_cw.append(_cols(dcw))
    gr["dn_conv_w"] = jnp.concatenate(d_cw, axis=1)

    gr["w_in"] = _unpad_w_in(mm(sv["h"], du, "tn", "mm_in_dw", dep=mix_token, tn=2048))
    token = emit("in", {"w_in": gr["w_in"]})
    d_h = mm(du, wl["w_in"], "nt", "mm_in_dx", dep=token)
    (dx_in,), (dg,), ex = rowmap_bwd(f_norm, [(x_in, D_MODEL, 0)], [row1(wl["attn_norm"])], [d_h], 1, "norm_bwd", add=[dx_mid],
                                     copy16=0)
    gr["attn_norm"] = dg[0, 0]
    big = ("w_in", "w_branch", "w_out", "w_up", "w_down")
    return dx_in, ex["copy16"], emit("small", {k: g for k, g in gr.items() if k not in big})


def _cols(dp):
    ncol, p, _ = dp.shape
    return jnp.transpose(dp, (1, 0, 2)).reshape(p, ncol * LANE)


def _pad_w_in(w):
    segs = sorted(_IN_SEGS, key=lambda s: s[2])
    parts = [lax.slice_in_dim(w, src, src + width, axis=1) for src, width, _ in segs]
    end = segs[-1][2] + segs[-1][1]
    return jnp.concatenate(parts + [jnp.zeros((w.shape[0], U_PAD - end), w.dtype)], axis=1)


def _unpad_w_in(wp):
    return jnp.concatenate([lax.slice_in_dim(wp, dst, dst + width, axis=1) for _, width, dst in _IN_SEGS], axis=1)


def _rope_tables(positions):
    half = RET_DK // 2
    inv = ROPE_BASE ** (-jnp.arange(half, dtype=f32) / half)
    ang = positions.astype(f32).reshape(-1, 1) * inv
    cos, sin = jnp.cos(ang), jnp.sin(ang)
    return jnp.tile(cos, (1, 2 * RET_HEADS)), jnp.tile(sin, (1, 2 * RET_HEADS))


def _layer_weights(lw):
    wl = {}
    wl["w_in"] = _pad_w_in(lw["w_in"])
    for k in ("dn_conv_w", "lru_conv_w", "ffn_conv_w", "lru_wa", "lru_wx"):
        wl[k] = lw[k]
    for k in ("attn_norm", "ffn_norm", "dn_norm_w", "lru_conv_b", "lru_lambda", "ffn_conv_b", "lru_ba", "lru_bx"):
        wl[k] = lw[k].reshape(1, -1)
    gp = jnp.zeros((8, LANE), f32)
    wl["dn_gate_p"] = gp.at[0, :DN_HEADS].set(lw["dn_a_log"]).at[1, :DN_HEADS].set(lw["dn_dt_bias"])
    return wl


REST = ("w_branch", "w_out", "w_up", "w_down")


def forward_backward(x, positions, target, layer_weights, final_norm, on_head, on_grads):
    bsz, seq, d = x.shape
    n = bsz * seq
    cos, sin = _rope_tables(positions)
    xs = x.reshape(n, d)
    saved = []
    for layer in range(DEPTH):
        first, fetch_rest = layer_weights(layer, xs)
        xs, sv = _layer_fwd(xs, _layer_weights(first), fetch_rest, cos, sin, bsz, seq)
        saved.append(sv)
    loss, dx, d_final, dx16 = final_loss(xs, final_norm.reshape(1, d), target.reshape(n, d))
    on_head(loss[0, 0], d_final[0])
    token = None
    for layer in reversed(range(DEPTH)):
        dx, dx16, token = _layer_bwd(dx, dx16, saved[layer], cos, sin, bsz, seq, functools.partial(on_grads, layer), token)
    return dx.reshape(bsz, seq, d)


BIG = (("w_in", 2), ("w_branch", 3), ("w_out", 1), ("w_up", 2), ("w_down", 1))
SMALL_SHARDED = (("dn_conv_w", 2), ("lru_conv_w", 2), ("ffn_conv_w", 2))
REPLICATED = ("attn_norm", "dn_a_log", "dn_dt_bias", "dn_norm_w", "lru_conv_b", "lru_wa", "lru_ba", "lru_wx", "lru_bx",
              "lru_lambda", "ffn_norm", "ffn_conv_b", "final_norm")
WEIGHTS = ("attn_norm", "w_in", "dn_conv_w", "dn_a_log", "dn_dt_bias", "dn_norm_w", "lru_conv_w", "lru_conv_b", "lru_wa",
           "lru_ba", "lru_wx", "lru_bx", "lru_lambda", "w_branch", "w_out", "ffn_norm", "w_up", "ffn_conv_w", "ffn_conv_b",
           "w_down", "final_norm")


def _pack(arrs, dtype, align=16 * LANE):
    flat = jnp.concatenate([a.reshape(-1).astype(dtype) for a in arrs])
    pad = (-flat.shape[0]) % align
    return jnp.pad(flat, (0, pad)).reshape(-1, LANE)


def _unpack(rows, shapes):
    flat = rows.reshape(-1)
    out, pos = [], 0
    for shp in shapes:
        size = math.prod(shp)
        out.append(lax.slice_in_dim(flat, pos, pos + size).reshape(shp))
        pos += size
    return out


def kernel(x, positions, attn_norm, w_in, dn_conv_w, dn_a_log, dn_dt_bias, dn_norm_w, lru_conv_w, lru_conv_b, lru_wa, lru_ba, lru_wx, lru_bx, lru_lambda, w_branch, w_out, ffn_norm, w_up, ffn_conv_w, ffn_conv_b, w_down, final_norm, loss_target, m_attn_norm, m_w_in, m_dn_conv_w, m_dn_a_log, m_dn_dt_bias, m_dn_norm_w, m_lru_conv_w, m_lru_conv_b, m_lru_wa, m_lru_ba, m_lru_wx, m_lru_bx, m_lru_lambda, m_w_branch, m_w_out, m_ffn_norm, m_w_up, m_ffn_conv_w, m_ffn_conv_b, m_w_down, m_final_norm, v_attn_norm, v_w_in, v_dn_conv_w, v_dn_a_log, v_dn_dt_bias, v_dn_norm_w, v_lru_conv_w, v_lru_conv_b, v_lru_wa, v_lru_ba, v_lru_wx, v_lru_bx, v_lru_lambda, v_w_branch, v_w_out, v_ffn_norm, v_w_up, v_ffn_conv_w, v_ffn_conv_b, v_w_down, v_final_norm):
    w = dict(attn_norm=attn_norm, w_in=w_in, dn_conv_w=dn_conv_w, dn_a_log=dn_a_log, dn_dt_bias=dn_dt_bias, dn_norm_w=dn_norm_w,
             lru_conv_w=lru_conv_w, lru_conv_b=lru_conv_b, lru_wa=lru_wa, lru_ba=lru_ba, lru_wx=lru_wx, lru_bx=lru_bx,
             lru_lambda=lru_lambda, w_branch=w_branch, w_out=w_out, ffn_norm=ffn_norm, w_up=w_up, ffn_conv_w=ffn_conv_w,
             ffn_conv_b=ffn_conv_b, w_down=w_down, final_norm=final_norm)
    m = dict(attn_norm=m_attn_norm, w_in=m_w_in, dn_conv_w=m_dn_conv_w, dn_a_log=m_dn_a_log, dn_dt_bias=m_dn_dt_bias,
             dn_norm_w=m_dn_norm_w, lru_conv_w=m_lru_conv_w, lru_conv_b=m_lru_conv_b, lru_wa=m_lru_wa, lru_ba=m_lru_ba,
             lru_wx=m_lru_wx, lru_bx=m_lru_bx, lru_lambda=m_lru_lambda, w_branch=m_w_branch, w_out=m_w_out, ffn_norm=m_ffn_norm,
             w_up=m_w_up, ffn_conv_w=m_ffn_conv_w, ffn_conv_b=m_ffn_conv_b, w_down=m_w_down, final_norm=m_final_norm)
    v = dict(attn_norm=v_attn_norm, w_in=v_w_in, dn_conv_w=v_dn_conv_w, dn_a_log=v_dn_a_log, dn_dt_bias=v_dn_dt_bias,
             dn_norm_w=v_dn_norm_w, lru_conv_w=v_lru_conv_w, lru_conv_b=v_lru_conv_b, lru_wa=v_lru_wa, lru_ba=v_lru_ba,
             lru_wx=v_lru_wx, lru_bx=v_lru_bx, lru_lambda=v_lru_lambda, w_branch=v_w_branch, w_out=v_w_out, ffn_norm=v_ffn_norm,
             w_up=v_w_up, ffn_conv_w=v_ffn_conv_w, ffn_conv_b=v_ffn_conv_b, w_down=v_w_down, final_norm=v_final_norm)

    me = 4 * lax.axis_index("x") + 2 * lax.axis_index("y") + lax.axis_index("c")
    axes = dict(BIG + SMALL_SHARDED)
    conv_names = [k for k, _ in SMALL_SHARDED]

    gathers, tokens, conv_full = {}, [], {}
    for layer in range(DEPTH):
        first = [w["w_in"][layer].astype(bf16)] + ([w[k] for k in conv_names] if layer == 0 else [])
        rest = [w[k][layer].astype(bf16) for k in REST]
        for part, srcs in (("in", first), ("rest", rest)):
            gathers[layer, part], token = exchange_start(srcs, [False] * len(srcs), "gather_%s_start%d" % (part, layer))
            tokens.append(token[0:1, 0:1])
    all_started = functools.reduce(lambda a, b: a + b, tokens)

    def join(land, axis):
        if axis == 0:
            return land.reshape((N_DEV * land.shape[1],) + land.shape[2:])
        return jnp.concatenate([land[p] for p in range(N_DEV)], axis=axis)

    def split(g, axis):
        if isinstance(g, tuple):
            each = N_DEV // len(g)
            size = g[0].shape[axis] // each
            return jnp.stack([lax.slice_in_dim(piece, p * size, (p + 1) * size, axis=axis) for piece in g for p in range(each)])
        size = g.shape[axis] // N_DEV
        if axis == 0:
            return g.reshape((N_DEV, size) + g.shape[1:])
        return jnp.stack([lax.slice_in_dim(g, p * size, (p + 1) * size, axis=axis) for p in range(N_DEV)])

    def layer_weights(layer, x_in):
        lands = exchange_wait(gathers[layer, "in"], x_in, "gather_in_wait%d" % layer)
        lw = {"w_in": join(lands[0], 1)}
        if layer == 0:
            conv_full.update({k: join(lands[1 + i], axes[k]) for i, k in enumerate(conv_names)})
        lw.update({k: conv_full[k][layer] for k in conv_names})
        lw.update({k: w[k][layer] for k in REPLICATED if k != "final_norm"})
        if layer == 0:
            lw["attn_norm"] = lw["attn_norm"] + all_started[0]

        def fetch_rest(after):
            lands_r = exchange_wait(gathers[layer, "rest"], after, "gather_rest_wait%d" % layer)
            return {k: join(lands_r[i], axes[k] - 1) for i, k in enumerate(REST)}

        return lw, fetch_rest

    small_names = conv_names + [k for k in REPLICATED if k != "final_norm"]
    groups = {"ffn": ("w_up", "w_down"), "mix": ("w_branch", "w_out"), "in": ("w_in",)}
    scatters, small_shapes, head = {}, {}, {}

    def on_grads(layer, group, gr):
        if group == "small":
            small_shapes.update({k: gr[k].shape for k in small_names})
            srcs = [_pack([gr[k] for k in small_names], f32)]
            srcs += [_pack([head["loss"].reshape(1), head["d_final"]], f32)] if layer == DEPTH - 1 else []
            modes = [False] * len(srcs)
        else:
            srcs = [split(gr[k], axes[k] - 1).astype(bf16) for k in groups[group]]
            modes = [True] * len(srcs)
        scatters[layer, group], token = exchange_start(srcs, modes, "scatter_%s_start%d" % (group, layer))
        return token

    grad_x = forward_backward(x, positions, loss_target, layer_weights, final_norm,
                              lambda loss_part, d_final: head.update(loss=loss_part, d_final=d_final), on_grads)

    big_sums, small_sums = {}, {}
    for group in ("ffn", "mix", "in"):
        for layer in reversed(range(DEPTH)):
            lands = exchange_wait(scatters[layer, group], grad_x, "scatter_%s_wait%d" % (group, layer))
            for i, k in enumerate(groups[group]):
                shard = w[k].shape[1:]
                big_sums[layer, k] = sum_slots(lands[i].reshape(N_DEV, -1, shard[-1]), "sum_" + k).reshape(shard)
    for layer in reversed(range(DEPTH)):
        lands = exchange_wait(scatters[layer, "small"], grad_x, "scatter_small_wait%d" % layer)
        small_sums[layer] = sum_slots(lands[0], "sum_small")
        if layer == DEPTH - 1:
            head_sum = _unpack(sum_slots(lands[1], "sum_head"), [(1,), final_norm.shape])
    grads = {k: jnp.stack([big_sums[layer, k] for layer in range(DEPTH)]) for k, _ in BIG}
    loss, grads["final_norm"] = head_sum[0][0], head_sum[1]
    small_flat = jnp.stack([small_sums[layer] for layer in range(DEPTH)]).reshape(DEPTH, -1)
    pos = 0
    for k in small_names:
        size = math.prod(small_shapes[k])
        g = lax.slice_in_dim(small_flat, pos, pos + size, axis=1).reshape((DEPTH,) + small_shapes[k])
        pos += size
        ax = dict(SMALL_SHARDED).get(k)
        if ax is None:
            grads[k] = g
        else:
            size = g.shape[ax] // N_DEV
            grads[k] = lax.dynamic_slice_in_dim(g, me * size, size, axis=ax)

    inner_last = {"w_in": (2, 0, 1), "w_up": (0, 2, 1)}
    upd = {}
    for k in WEIGHTS:
        if k in inner_last:
            perm = inner_last[k]
            back = tuple(perm.index(i) for i in range(3))
            g_t = jnp.transpose(grads[k], perm)
            res = adamw(jnp.transpose(w[k], perm), g_t, jnp.transpose(m[k], perm), jnp.transpose(v[k], perm), "adamw_" + k)
            upd[k] = tuple(jnp.transpose(t, back) for t in res)
            grads[k] = jnp.transpose(g_t, back)
        else:
            upd[k] = adamw(w[k], grads[k], m[k], v[k], "adamw_" + k)
    return (loss, grad_x, *[grads[k] for k in WEIGHTS], *[upd[k][0] for k in WEIGHTS], *[upd[k][1] for k in WEIGHTS],
            *[upd[k][2] for k in WEIGHTS])
```
